```python
import math
import jax, jax.numpy as jnp
from jax import lax
import numpy as np

D_MODEL = 1024
BATCH = 8
SEQ = 4096
DEPTH = 4

CHUNK = 64
Q_BLOCK = 128
N_MIXERS = 3

MLA_HEADS = 16
QK_NOPE = 64
QK_ROPE = 32
V_HEAD = 64
Q_LORA = 384
KV_LORA = 256
ROPE_THETA = 10000.0

CONV_WIDTH = 31

POOL_WINDOWS = (2, 4, 8, 16)
POOL_GROUPS = len(POOL_WINDOWS)

D_FF = 4 * D_MODEL

NORM_EPS = 1e-6
NEG_INF = -1e30

kernel_name = "hybrid_mla_conformer_pool_trunk"


def _rmsnorm(x, g):
    x32 = x.astype(jnp.float32)
    y = x32 * lax.rsqrt(jnp.mean(x32 * x32, axis=-1, keepdims=True) + NORM_EPS)
    return (y * g.astype(jnp.float32)).astype(x.dtype)


def _layernorm(x, g, b):
    x32 = x.astype(jnp.float32)
    mu = jnp.mean(x32, axis=-1, keepdims=True)
    xc = x32 - mu
    y = xc * lax.rsqrt(jnp.mean(xc * xc, axis=-1, keepdims=True) + NORM_EPS)
    return (y * g.astype(jnp.float32) + b.astype(jnp.float32)).astype(x.dtype)


def _rope_tables(positions):
    inv_freq = ROPE_THETA ** (-jnp.arange(0, QK_ROPE, 2, dtype=jnp.float32) / QK_ROPE)
    ang = positions.astype(jnp.float32)[..., None] * inv_freq
    return jnp.cos(ang), jnp.sin(ang)


def _apply_rope(x, cos, sin):
    half = x.shape[-1] // 2
    x1 = x[..., :half].astype(jnp.float32)
    x2 = x[..., half:].astype(jnp.float32)
    out = jnp.concatenate([x1 * cos - x2 * sin, x1 * sin + x2 * cos], axis=-1)
    return out.astype(x.dtype)


def _chunk_causal_attention(q, k, v):
    B, S, H, Dk = q.shape
    nb = S // Q_BLOCK
    qb = q.reshape(B, nb, Q_BLOCK, H, Dk).swapaxes(0, 1)
    k_chunk = jnp.arange(S) // CHUNK
    scale = 1.0 / math.sqrt(Dk)

    def one_block(args):
        qblk, bi = args
        q_chunk = (bi * Q_BLOCK + jnp.arange(Q_BLOCK)) // CHUNK
        mask = k_chunk[None, :] <= q_chunk[:, None]
        s = jnp.einsum('bqhd,bkhd->bhqk', qblk, k).astype(jnp.float32) * scale
        s = jnp.where(mask[None, None], s, NEG_INF)
        p = jax.nn.softmax(s, axis=-1).astype(v.dtype)
        return jnp.einsum('bhqk,bkhd->bqhd', p, v)

    o = lax.map(one_block, (qb, jnp.arange(nb)))
    return o.swapaxes(0, 1).reshape(B, S, H, v.shape[-1])


def _mla(h, positions, w_dq, q_norm_g, w_uq, w_dkv, kv_norm_g, w_ukv, w_o):
    B, S, _ = h.shape
    cq = _rmsnorm(h @ w_dq, q_norm_g)
    q = (cq @ w_uq).reshape(B, S, MLA_HEADS, QK_NOPE + QK_ROPE)
    q_nope, q_rope = q[..., :QK_NOPE], q[..., QK_NOPE:]
    ckv_all = h @ w_dkv
    ckv = _rmsnorm(ckv_all[..., :KV_LORA], kv_norm_g)
    k_rope = ckv_all[..., KV_LORA:]
    kv = (ckv @ w_ukv).reshape(B, S, MLA_HEADS, QK_NOPE + V_HEAD)
    k_nope, v = kv[..., :QK_NOPE], kv[..., QK_NOPE:]
    cos, sin = _rope_tables(positions)
    q_rope = _apply_rope(q_rope, cos[:, :, None], sin[:, :, None])
    k_rope = _apply_rope(k_rope, cos, sin)
    qf = jnp.concatenate([q_nope, q_rope], axis=-1)
    kf = jnp.concatenate(
        [k_nope, jnp.broadcast_to(k_rope[:, :, None], (B, S, MLA_HEADS, QK_ROPE))], axis=-1)
    o = _chunk_causal_attention(qf, kf, v)
    return o.reshape(B, S, MLA_HEADS * V_HEAD) @ w_o


def _conformer_conv(h, w_pw1, b_pw1, w_dw, b_dw, ln_g, ln_b, w_pw2, b_pw2):
    D = h.shape[-1]
    a = h @ w_pw1 + b_pw1
    u = a[..., :D] * jax.nn.sigmoid(a[..., D:])
    u = lax.conv_general_dilated(
        u, w_dw[:, None, :].astype(u.dtype), window_strides=(1,),
        padding=[(CONV_WIDTH - 1, 0)], dimension_numbers=('NWC', 'WIO', 'NWC'),
        feature_group_count=D) + b_dw
    u = jax.nn.silu(_layernorm(u, ln_g, ln_b))
    return u @ w_pw2 + b_pw2


def _pool_mixer(h, w, b, scale):
    B, S, D = h.shape
    C = D // POOL_GROUPS
    csum = jnp.cumsum(h.astype(jnp.float32), axis=1)
    t = jnp.arange(S)
    pooled = []
    for g, win in enumerate(POOL_WINDOWS):
        cs = csum[..., g * C:(g + 1) * C]
        lag = jnp.pad(cs, ((0, 0), (win, 0), (0, 0)))[:, :S]
        cnt = jnp.minimum(t + 1, win).astype(jnp.float32)
        pooled.append((cs - lag) / cnt[None, :, None])
    p = jnp.concatenate(pooled, axis=-1).astype(h.dtype) - h
    y = jnp.einsum('bsgc,gcd->bsgd', p.reshape(B, S, POOL_GROUPS, C), w) + b
    return y.reshape(B, S, D) * scale


def _sq_relu_mlp(h, w1, w2):
    return jnp.square(jax.nn.relu(h @ w1)) @ w2


def _fwd_setup_inputs(seed: int = 0) -> dict:
    key = jax.random.key(seed)
    ks = jax.random.split(key, 32)
    D = D_MODEL
    n_mla = (DEPTH + 2) // 3
    n_conv = (DEPTH + 1) // 3
    n_pool = DEPTH // 3
    C = D // POOL_GROUPS

    def nrm(k, shape, fan_in, mult=1.0):
        return jax.random.normal(k, shape, jnp.float32) * (mult * fan_in ** -0.5)

    def gain(k, shape):
        return 1.0 + 0.05 * jax.random.normal(k, shape, jnp.float32)

    def bias(k, shape):
        return 0.02 * jax.random.normal(k, shape, jnp.float32)

    x = jax.random.normal(ks[0], (BATCH, SEQ, D), jnp.float32)
    c = jax.random.normal(ks[1], (BATCH, D), jnp.float32)
    offsets = jax.random.randint(ks[2], (BATCH,), 0, 64, dtype=jnp.int32) * CHUNK
    positions = offsets[:, None] + jnp.arange(SEQ, dtype=jnp.int32)[None, :]
    return {
        "x": x,
        "c": c,
        "positions": positions,
        "ada_w": nrm(ks[3], (DEPTH, D, 6 * D), D, 0.5),
        "ada_b": bias(ks[4], (DEPTH, 6 * D)),
        "norm_g": gain(ks[5], (DEPTH, 4, D)),
        "mla_w_dq": nrm(ks[6], (n_mla, D, Q_LORA), D),
        "mla_q_norm_g": gain(ks[7], (n_mla, Q_LORA)),
        "mla_w_uq": nrm(ks[8], (n_mla, Q_LORA, MLA_HEADS * (QK_NOPE + QK_ROPE)), Q_LORA),
        "mla_w_dkv": nrm(ks[9], (n_mla, D, KV_LORA + QK_ROPE), D),
        "mla_kv_norm_g": gain(ks[10], (n_mla, KV_LORA)),
        "mla_w_ukv": nrm(ks[11], (n_mla, KV_LORA, MLA_HEADS * (QK_NOPE + V_HEAD)), KV_LORA),
        "mla_w_o": nrm(ks[12], (n_mla, MLA_HEADS * V_HEAD, D), MLA_HEADS * V_HEAD),
        "conv_w_pw1": nrm(ks[13], (n_conv, D, 2 * D), D),
        "conv_b_pw1": bias(ks[14], (n_conv, 2 * D)),
        "conv_w_dw": nrm(ks[15], (n_conv, CONV_WIDTH, D), CONV_WIDTH),
        "conv_b_dw": bias(ks[16], (n_conv, D)),
        "conv_ln_g": gain(ks[17], (n_conv, D)),
        "conv_ln_b": bias(ks[18], (n_conv, D)),
        "conv_w_pw2": nrm(ks[19], (n_conv, D, D), D),
        "conv_b_pw2": bias(ks[20], (n_conv, D)),
        "pool_w": nrm(ks[21], (n_pool, POOL_GROUPS, C, C), C),
        "pool_b": bias(ks[22], (n_pool, POOL_GROUPS, C)),
        "pool_scale": gain(ks[23], (n_pool, D)),
        "ffn_w1": nrm(ks[24], (DEPTH, D, D_FF), D),
        "ffn_w2": nrm(ks[25], (DEPTH, D_FF, D), D_FF),
    }


def _fwd_reference(x, c, positions, ada_w, ada_b, norm_g,
              mla_w_dq, mla_q_norm_g, mla_w_uq, mla_w_dkv, mla_kv_norm_g, mla_w_ukv, mla_w_o,
              conv_w_pw1, conv_b_pw1, conv_w_dw, conv_b_dw, conv_ln_g, conv_ln_b,
              conv_w_pw2, conv_b_pw2,
              pool_w, pool_b, pool_scale,
              ffn_w1, ffn_w2):
    c_act = jax.nn.silu(c)
    for i in range(DEPTH):
        kind = i % N_MIXERS
        j = i // N_MIXERS
        mod = c_act @ ada_w[i] + ada_b[i]
        sh_m, sc_m, gt_m, sh_f, sc_f, gt_f = jnp.split(mod, 6, axis=-1)

        h = _rmsnorm(x, norm_g[i, 0]) * (1.0 + sc_m[:, None]) + sh_m[:, None]
        if kind == 0:
            y = _mla(h, positions, mla_w_dq[j], mla_q_norm_g[j], mla_w_uq[j],
                     mla_w_dkv[j], mla_kv_norm_g[j], mla_w_ukv[j], mla_w_o[j])
        elif kind == 1:
            y = _conformer_conv(h, conv_w_pw1[j], conv_b_pw1[j], conv_w_dw[j], conv_b_dw[j],
                                conv_ln_g[j], conv_ln_b[j], conv_w_pw2[j], conv_b_pw2[j])
        else:
            y = _pool_mixer(h, pool_w[j], pool_b[j], pool_scale[j])
        x = x + gt_m[:, None] * _rmsnorm(y, norm_g[i, 1])

        h = _rmsnorm(x, norm_g[i, 2]) * (1.0 + sc_f[:, None]) + sh_f[:, None]
        y = _sq_relu_mlp(h, ffn_w1[i], ffn_w2[i])
        x = x + gt_f[:, None] * _rmsnorm(y, norm_g[i, 3])
    return x


import jax as _jax
import jax.numpy as _jnp

TWIN_FORMAT = 'train_step'
FWD_PARAMS = ['x', 'c', 'positions', 'ada_w', 'ada_b', 'norm_g', 'mla_w_dq', 'mla_q_norm_g', 'mla_w_uq', 'mla_w_dkv', 'mla_kv_norm_g', 'mla_w_ukv', 'mla_w_o', 'conv_w_pw1', 'conv_b_pw1', 'conv_w_dw', 'conv_b_dw', 'conv_ln_g', 'conv_ln_b', 'conv_w_pw2', 'conv_b_pw2', 'pool_w', 'pool_b', 'pool_scale', 'ffn_w1', 'ffn_w2']
TWIN_WEIGHTS = ['ada_w', 'ada_b', 'norm_g', 'mla_w_dq', 'mla_q_norm_g', 'mla_w_uq', 'mla_w_dkv', 'mla_kv_norm_g', 'mla_w_ukv', 'mla_w_o', 'conv_w_pw1', 'conv_b_pw1', 'conv_w_dw', 'conv_b_dw', 'conv_ln_g', 'conv_ln_b', 'conv_w_pw2', 'conv_b_pw2', 'pool_w', 'pool_b', 'pool_scale', 'ffn_w1', 'ffn_w2']
TWIN_DIFF_INPUT = 'x'
TWIN_INPUTS = ['x', 'c', 'positions', 'ada_w', 'ada_b', 'norm_g', 'mla_w_dq', 'mla_q_norm_g', 'mla_w_uq', 'mla_w_dkv', 'mla_kv_norm_g', 'mla_w_ukv', 'mla_w_o', 'conv_w_pw1', 'conv_b_pw1', 'conv_w_dw', 'conv_b_dw', 'conv_ln_g', 'conv_ln_b', 'conv_w_pw2', 'conv_b_pw2', 'pool_w', 'pool_b', 'pool_scale', 'ffn_w1', 'ffn_w2', 'loss_target', 'm_ada_w', 'm_ada_b', 'm_norm_g', 'm_mla_w_dq', 'm_mla_q_norm_g', 'm_mla_w_uq', 'm_mla_w_dkv', 'm_mla_kv_norm_g', 'm_mla_w_ukv', 'm_mla_w_o', 'm_conv_w_pw1', 'm_conv_b_pw1', 'm_conv_w_dw', 'm_conv_b_dw', 'm_conv_ln_g', 'm_conv_ln_b', 'm_conv_w_pw2', 'm_conv_b_pw2', 'm_pool_w', 'm_pool_b', 'm_pool_scale', 'm_ffn_w1', 'm_ffn_w2', 'v_ada_w', 'v_ada_b', 'v_norm_g', 'v_mla_w_dq', 'v_mla_q_norm_g', 'v_mla_w_uq', 'v_mla_w_dkv', 'v_mla_kv_norm_g', 'v_mla_w_ukv', 'v_mla_w_o', 'v_conv_w_pw1', 'v_conv_b_pw1', 'v_conv_w_dw', 'v_conv_b_dw', 'v_conv_ln_g', 'v_conv_ln_b', 'v_conv_w_pw2', 'v_conv_b_pw2', 'v_pool_w', 'v_pool_b', 'v_pool_scale', 'v_ffn_w1', 'v_ffn_w2']
TWIN_OUTPUTS = ['loss', 'grad_x', 'grad_ada_w', 'grad_ada_b', 'grad_norm_g', 'grad_mla_w_dq', 'grad_mla_q_norm_g', 'grad_mla_w_uq', 'grad_mla_w_dkv', 'grad_mla_kv_norm_g', 'grad_mla_w_ukv', 'grad_mla_w_o', 'grad_conv_w_pw1', 'grad_conv_b_pw1', 'grad_conv_w_dw', 'grad_conv_b_dw', 'grad_conv_ln_g', 'grad_conv_ln_b', 'grad_conv_w_pw2', 'grad_conv_b_pw2', 'grad_pool_w', 'grad_pool_b', 'grad_pool_scale', 'grad_ffn_w1', 'grad_ffn_w2', 'delta_ada_w', 'delta_ada_b', 'delta_norm_g', 'delta_mla_w_dq', 'delta_mla_q_norm_g', 'delta_mla_w_uq', 'delta_mla_w_dkv', 'delta_mla_kv_norm_g', 'delta_mla_w_ukv', 'delta_mla_w_o', 'delta_conv_w_pw1', 'delta_conv_b_pw1', 'delta_conv_w_dw', 'delta_conv_b_dw', 'delta_conv_ln_g', 'delta_conv_ln_b', 'delta_conv_w_pw2', 'delta_conv_b_pw2', 'delta_pool_w', 'delta_pool_b', 'delta_pool_scale', 'delta_ffn_w1', 'delta_ffn_w2', 'new_m_ada_w', 'new_m_ada_b', 'new_m_norm_g', 'new_m_mla_w_dq', 'new_m_mla_q_norm_g', 'new_m_mla_w_uq', 'new_m_mla_w_dkv', 'new_m_mla_kv_norm_g', 'new_m_mla_w_ukv', 'new_m_mla_w_o', 'new_m_conv_w_pw1', 'new_m_conv_b_pw1', 'new_m_conv_w_dw', 'new_m_conv_b_dw', 'new_m_conv_ln_g', 'new_m_conv_ln_b', 'new_m_conv_w_pw2', 'new_m_conv_b_pw2', 'new_m_pool_w', 'new_m_pool_b', 'new_m_pool_scale', 'new_m_ffn_w1', 'new_m_ffn_w2', 'new_v_ada_w', 'new_v_ada_b', 'new_v_norm_g', 'new_v_mla_w_dq', 'new_v_mla_q_norm_g', 'new_v_mla_w_uq', 'new_v_mla_w_dkv', 'new_v_mla_kv_norm_g', 'new_v_mla_w_ukv', 'new_v_mla_w_o', 'new_v_conv_w_pw1', 'new_v_conv_b_pw1', 'new_v_conv_w_dw', 'new_v_conv_b_dw', 'new_v_conv_ln_g', 'new_v_conv_ln_b', 'new_v_conv_w_pw2', 'new_v_conv_b_pw2', 'new_v_pool_w', 'new_v_pool_b', 'new_v_pool_scale', 'new_v_ffn_w1', 'new_v_ffn_w2']
TWIN_LEAF_KINDS = {'loss': 'loss', 'grad_x': 'grad_x', 'grad_ada_w': 'grad_w', 'grad_ada_b': 'grad_w', 'grad_norm_g': 'grad_w', 'grad_mla_w_dq': 'grad_w', 'grad_mla_q_norm_g': 'grad_w', 'grad_mla_w_uq': 'grad_w', 'grad_mla_w_dkv': 'grad_w', 'grad_mla_kv_norm_g': 'grad_w', 'grad_mla_w_ukv': 'grad_w', 'grad_mla_w_o': 'grad_w', 'grad_conv_w_pw1': 'grad_w', 'grad_conv_b_pw1': 'grad_w', 'grad_conv_w_dw': 'grad_w', 'grad_conv_b_dw': 'grad_w', 'grad_conv_ln_g': 'grad_w', 'grad_conv_ln_b': 'grad_w', 'grad_conv_w_pw2': 'grad_w', 'grad_conv_b_pw2': 'grad_w', 'grad_pool_w': 'grad_w', 'grad_pool_b': 'grad_w', 'grad_pool_scale': 'grad_w', 'grad_ffn_w1': 'grad_w', 'grad_ffn_w2': 'grad_w', 'delta_ada_w': 'delta_w', 'delta_ada_b': 'delta_w', 'delta_norm_g': 'delta_w', 'delta_mla_w_dq': 'delta_w', 'delta_mla_q_norm_g': 'delta_w', 'delta_mla_w_uq': 'delta_w', 'delta_mla_w_dkv': 'delta_w', 'delta_mla_kv_norm_g': 'delta_w', 'delta_mla_w_ukv': 'delta_w', 'delta_mla_w_o': 'delta_w', 'delta_conv_w_pw1': 'delta_w', 'delta_conv_b_pw1': 'delta_w', 'delta_conv_w_dw': 'delta_w', 'delta_conv_b_dw': 'delta_w', 'delta_conv_ln_g': 'delta_w', 'delta_conv_ln_b': 'delta_w', 'delta_conv_w_pw2': 'delta_w', 'delta_conv_b_pw2': 'delta_w', 'delta_pool_w': 'delta_w', 'delta_pool_b': 'delta_w', 'delta_pool_scale': 'delta_w', 'delta_ffn_w1': 'delta_w', 'delta_ffn_w2': 'delta_w', 'new_m_ada_w': 'new_m', 'new_m_ada_b': 'new_m', 'new_m_norm_g': 'new_m', 'new_m_mla_w_dq': 'new_m', 'new_m_mla_q_norm_g': 'new_m', 'new_m_mla_w_uq': 'new_m', 'new_m_mla_w_dkv': 'new_m', 'new_m_mla_kv_norm_g': 'new_m', 'new_m_mla_w_ukv': 'new_m', 'new_m_mla_w_o': 'new_m', 'new_m_conv_w_pw1': 'new_m', 'new_m_conv_b_pw1': 'new_m', 'new_m_conv_w_dw': 'new_m', 'new_m_conv_b_dw': 'new_m', 'new_m_conv_ln_g': 'new_m', 'new_m_conv_ln_b': 'new_m', 'new_m_conv_w_pw2': 'new_m', 'new_m_conv_b_pw2': 'new_m', 'new_m_pool_w': 'new_m', 'new_m_pool_b': 'new_m', 'new_m_pool_scale': 'new_m', 'new_m_ffn_w1': 'new_m', 'new_m_ffn_w2': 'new_m', 'new_v_ada_w': 'new_v', 'new_v_ada_b': 'new_v', 'new_v_norm_g': 'new_v', 'new_v_mla_w_dq': 'new_v', 'new_v_mla_q_norm_g': 'new_v', 'new_v_mla_w_uq': 'new_v', 'new_v_mla_w_dkv': 'new_v', 'new_v_mla_kv_norm_g': 'new_v', 'new_v_mla_w_ukv': 'new_v', 'new_v_mla_w_o': 'new_v', 'new_v_conv_w_pw1': 'new_v', 'new_v_conv_b_pw1': 'new_v', 'new_v_conv_w_dw': 'new_v', 'new_v_conv_b_dw': 'new_v', 'new_v_conv_ln_g': 'new_v', 'new_v_conv_ln_b': 'new_v', 'new_v_conv_w_pw2': 'new_v', 'new_v_conv_b_pw2': 'new_v', 'new_v_pool_w': 'new_v', 'new_v_pool_b': 'new_v', 'new_v_pool_scale': 'new_v', 'new_v_ffn_w1': 'new_v', 'new_v_ffn_w2': 'new_v'}


def _forward(args):
    return _fwd_reference(*[args[k] for k in FWD_PARAMS])


def _output_shape():
    def fwd():
        inp = _fwd_setup_inputs(0)
        return _fwd_reference(*[inp[k] for k in FWD_PARAMS])
    out = _jax.eval_shape(fwd)
    return out.shape, out.dtype

N_MICROBATCH = 1
ADAM_LR = 0.001
ADAM_B1 = 0.9
ADAM_B2 = 0.999
ADAM_EPS = 1e-08
ADAM_WD = 0.01
ADAM_STEP = 10
PER_EXAMPLE_BATCH_AXIS = {'x': 0, 'c': 0, 'positions': 0, 'loss_target': 0}
SHARED_INPUTS = []
_WEIGHT_DTYPES = {'ada_w': _jnp.float32, 'ada_b': _jnp.float32, 'norm_g': _jnp.float32, 'mla_w_dq': _jnp.float32, 'mla_q_norm_g': _jnp.float32, 'mla_w_uq': _jnp.float32, 'mla_w_dkv': _jnp.float32, 'mla_kv_norm_g': _jnp.float32, 'mla_w_ukv': _jnp.float32, 'mla_w_o': _jnp.float32, 'conv_w_pw1': _jnp.float32, 'conv_b_pw1': _jnp.float32, 'conv_w_dw': _jnp.float32, 'conv_b_dw': _jnp.float32, 'conv_ln_g': _jnp.float32, 'conv_ln_b': _jnp.float32, 'conv_w_pw2': _jnp.float32, 'conv_b_pw2': _jnp.float32, 'pool_w': _jnp.float32, 'pool_b': _jnp.float32, 'pool_scale': _jnp.float32, 'ffn_w1': _jnp.float32, 'ffn_w2': _jnp.float32}
MOMENT_SCALE = {'ada_w': 2.468460e+00, 'ada_b': 4.514251e+00, 'norm_g': 2.757539e+00, 'mla_w_dq': 1.780378e-01, 'mla_q_norm_g': 1.724530e-01, 'mla_w_uq': 8.844280e-02, 'mla_w_dkv': 4.301994e+00, 'mla_kv_norm_g': 4.435758e+00, 'mla_w_ukv': 1.581013e+00, 'mla_w_o': 2.228338e+00, 'conv_w_pw1': 5.691264e-01, 'conv_b_pw1': 1.306363e+00, 'conv_w_dw': 7.321624e-01, 'conv_b_dw': 3.269166e+00, 'conv_ln_g': 1.519598e+00, 'conv_ln_b': 2.040528e+00, 'conv_w_pw2': 1.159229e+00, 'conv_b_pw2': 3.979137e+00, 'pool_w': 2.668761e-01, 'pool_b': 2.331502e+00, 'pool_scale': 1.587638e+00, 'ffn_w1': 3.346423e-01, 'ffn_w2': 1.128327e+00}


def _to_microbatches(a, axis):
    t = _jnp.moveaxis(a, axis, 0)
    t = t.reshape((N_MICROBATCH, t.shape[0] // N_MICROBATCH) + t.shape[1:])
    return _jnp.moveaxis(t, 1, axis + 1)


def setup_inputs(seed: int = 0) -> dict:
    inp = _fwd_setup_inputs(seed)
    key = _jax.random.fold_in(_jax.random.key(seed), 7919)
    shape, _ = _output_shape()
    out = dict(inp)
    out["loss_target"] = _jax.random.normal(_jax.random.fold_in(key, 0), shape, _jnp.float32)
    for i, name in enumerate(TWIN_WEIGHTS):
        w = inp[name].astype(_jnp.float32)
        if MOMENT_SCALE is None:
            s = _jnp.sqrt(_jnp.mean(_jnp.square(w)) + 1e-30)
        else:
            s = MOMENT_SCALE[name]
        km, kv = _jax.random.split(_jax.random.fold_in(key, i + 1))
        out[name] = w
        out["m_" + name] = s * _jax.random.normal(km, w.shape, _jnp.float32)
        out["v_" + name] = (s * s) * _jax.random.uniform(kv, w.shape, _jnp.float32, 0.5, 1.5)
    if N_MICROBATCH > 1:
        for name, axis in PER_EXAMPLE_BATCH_AXIS.items():
            out[name] = _to_microbatches(out[name], axis)
    return {'x': out['x'], 'c': out['c'], 'positions': out['positions'], 'ada_w': out['ada_w'], 'ada_b': out['ada_b'], 'norm_g': out['norm_g'], 'mla_w_dq': out['mla_w_dq'], 'mla_q_norm_g': out['mla_q_norm_g'], 'mla_w_uq': out['mla_w_uq'], 'mla_w_dkv': out['mla_w_dkv'], 'mla_kv_norm_g': out['mla_kv_norm_g'], 'mla_w_ukv': out['mla_w_ukv'], 'mla_w_o': out['mla_w_o'], 'conv_w_pw1': out['conv_w_pw1'], 'conv_b_pw1': out['conv_b_pw1'], 'conv_w_dw': out['conv_w_dw'], 'conv_b_dw': out['conv_b_dw'], 'conv_ln_g': out['conv_ln_g'], 'conv_ln_b': out['conv_ln_b'], 'conv_w_pw2': out['conv_w_pw2'], 'conv_b_pw2': out['conv_b_pw2'], 'pool_w': out['pool_w'], 'pool_b': out['pool_b'], 'pool_scale': out['pool_scale'], 'ffn_w1': out['ffn_w1'], 'ffn_w2': out['ffn_w2'], 'loss_target': out['loss_target'], 'm_ada_w': out['m_ada_w'], 'm_ada_b': out['m_ada_b'], 'm_norm_g': out['m_norm_g'], 'm_mla_w_dq': out['m_mla_w_dq'], 'm_mla_q_norm_g': out['m_mla_q_norm_g'], 'm_mla_w_uq': out['m_mla_w_uq'], 'm_mla_w_dkv': out['m_mla_w_dkv'], 'm_mla_kv_norm_g': out['m_mla_kv_norm_g'], 'm_mla_w_ukv': out['m_mla_w_ukv'], 'm_mla_w_o': out['m_mla_w_o'], 'm_conv_w_pw1': out['m_conv_w_pw1'], 'm_conv_b_pw1': out['m_conv_b_pw1'], 'm_conv_w_dw': out['m_conv_w_dw'], 'm_conv_b_dw': out['m_conv_b_dw'], 'm_conv_ln_g': out['m_conv_ln_g'], 'm_conv_ln_b': out['m_conv_ln_b'], 'm_conv_w_pw2': out['m_conv_w_pw2'], 'm_conv_b_pw2': out['m_conv_b_pw2'], 'm_pool_w': out['m_pool_w'], 'm_pool_b': out['m_pool_b'], 'm_pool_scale': out['m_pool_scale'], 'm_ffn_w1': out['m_ffn_w1'], 'm_ffn_w2': out['m_ffn_w2'], 'v_ada_w': out['v_ada_w'], 'v_ada_b': out['v_ada_b'], 'v_norm_g': out['v_norm_g'], 'v_mla_w_dq': out['v_mla_w_dq'], 'v_mla_q_norm_g': out['v_mla_q_norm_g'], 'v_mla_w_uq': out['v_mla_w_uq'], 'v_mla_w_dkv': out['v_mla_w_dkv'], 'v_mla_kv_norm_g': out['v_mla_kv_norm_g'], 'v_mla_w_ukv': out['v_mla_w_ukv'], 'v_mla_w_o': out['v_mla_w_o'], 'v_conv_w_pw1': out['v_conv_w_pw1'], 'v_conv_b_pw1': out['v_conv_b_pw1'], 'v_conv_w_dw': out['v_conv_w_dw'], 'v_conv_b_dw': out['v_conv_b_dw'], 'v_conv_ln_g': out['v_conv_ln_g'], 'v_conv_ln_b': out['v_conv_ln_b'], 'v_conv_w_pw2': out['v_conv_w_pw2'], 'v_conv_b_pw2': out['v_conv_b_pw2'], 'v_pool_w': out['v_pool_w'], 'v_pool_b': out['v_pool_b'], 'v_pool_scale': out['v_pool_scale'], 'v_ffn_w1': out['v_ffn_w1'], 'v_ffn_w2': out['v_ffn_w2']}


def _loss(weights, diff, rest, loss_target):
    with _jax.named_scope("forward"):
        args = {**rest, TWIN_DIFF_INPUT: diff, **{k: w.astype(_WEIGHT_DTYPES[k]) for k, w in weights.items()}}
        y = _forward(args)
    with _jax.named_scope("loss_head"):
        err = _jnp.square(y.astype(_jnp.float32) - loss_target)
        return 0.5 * _jnp.sum(_jnp.mean(err, axis=-1)) if err.ndim else 0.5 * err


def _adamw(w, g, m, v):
    m = ADAM_B1 * m + (1.0 - ADAM_B1) * g
    v = ADAM_B2 * v + (1.0 - ADAM_B2) * _jnp.square(g)
    m_hat = m / (1.0 - ADAM_B1 ** ADAM_STEP)
    v_hat = v / (1.0 - ADAM_B2 ** ADAM_STEP)
    delta = -ADAM_LR * (m_hat / (_jnp.sqrt(v_hat) + ADAM_EPS) + ADAM_WD * w)
    return delta, m, v


def reference(x, c, positions, ada_w, ada_b, norm_g, mla_w_dq, mla_q_norm_g, mla_w_uq, mla_w_dkv, mla_kv_norm_g, mla_w_ukv, mla_w_o, conv_w_pw1, conv_b_pw1, conv_w_dw, conv_b_dw, conv_ln_g, conv_ln_b, conv_w_pw2, conv_b_pw2, pool_w, pool_b, pool_scale, ffn_w1, ffn_w2, loss_target, m_ada_w, m_ada_b, m_norm_g, m_mla_w_dq, m_mla_q_norm_g, m_mla_w_uq, m_mla_w_dkv, m_mla_kv_norm_g, m_mla_w_ukv, m_mla_w_o, m_conv_w_pw1, m_conv_b_pw1, m_conv_w_dw, m_conv_b_dw, m_conv_ln_g, m_conv_ln_b, m_conv_w_pw2, m_conv_b_pw2, m_pool_w, m_pool_b, m_pool_scale, m_ffn_w1, m_ffn_w2, v_ada_w, v_ada_b, v_norm_g, v_mla_w_dq, v_mla_q_norm_g, v_mla_w_uq, v_mla_w_dkv, v_mla_kv_norm_g, v_mla_w_ukv, v_mla_w_o, v_conv_w_pw1, v_conv_b_pw1, v_conv_w_dw, v_conv_b_dw, v_conv_ln_g, v_conv_ln_b, v_conv_w_pw2, v_conv_b_pw2, v_pool_w, v_pool_b, v_pool_scale, v_ffn_w1, v_ffn_w2):
    given = dict(x=x, c=c, positions=positions, ada_w=ada_w, ada_b=ada_b, norm_g=norm_g, mla_w_dq=mla_w_dq, mla_q_norm_g=mla_q_norm_g, mla_w_uq=mla_w_uq, mla_w_dkv=mla_w_dkv, mla_kv_norm_g=mla_kv_norm_g, mla_w_ukv=mla_w_ukv, mla_w_o=mla_w_o, conv_w_pw1=conv_w_pw1, conv_b_pw1=conv_b_pw1, conv_w_dw=conv_w_dw, conv_b_dw=conv_b_dw, conv_ln_g=conv_ln_g, conv_ln_b=conv_ln_b, conv_w_pw2=conv_w_pw2, conv_b_pw2=conv_b_pw2, pool_w=pool_w, pool_b=pool_b, pool_scale=pool_scale, ffn_w1=ffn_w1, ffn_w2=ffn_w2, loss_target=loss_target, m_ada_w=m_ada_w, m_ada_b=m_ada_b, m_norm_g=m_norm_g, m_mla_w_dq=m_mla_w_dq, m_mla_q_norm_g=m_mla_q_norm_g, m_mla_w_uq=m_mla_w_uq, m_mla_w_dkv=m_mla_w_dkv, m_mla_kv_norm_g=m_mla_kv_norm_g, m_mla_w_ukv=m_mla_w_ukv, m_mla_w_o=m_mla_w_o, m_conv_w_pw1=m_conv_w_pw1, m_conv_b_pw1=m_conv_b_pw1, m_conv_w_dw=m_conv_w_dw, m_conv_b_dw=m_conv_b_dw, m_conv_ln_g=m_conv_ln_g, m_conv_ln_b=m_conv_ln_b, m_conv_w_pw2=m_conv_w_pw2, m_conv_b_pw2=m_conv_b_pw2, m_pool_w=m_pool_w, m_pool_b=m_pool_b, m_pool_scale=m_pool_scale, m_ffn_w1=m_ffn_w1, m_ffn_w2=m_ffn_w2, v_ada_w=v_ada_w, v_ada_b=v_ada_b, v_norm_g=v_norm_g, v_mla_w_dq=v_mla_w_dq, v_mla_q_norm_g=v_mla_q_norm_g, v_mla_w_uq=v_mla_w_uq, v_mla_w_dkv=v_mla_w_dkv, v_mla_kv_norm_g=v_mla_kv_norm_g, v_mla_w_ukv=v_mla_w_ukv, v_mla_w_o=v_mla_w_o, v_conv_w_pw1=v_conv_w_pw1, v_conv_b_pw1=v_conv_b_pw1, v_conv_w_dw=v_conv_w_dw, v_conv_b_dw=v_conv_b_dw, v_conv_ln_g=v_conv_ln_g, v_conv_ln_b=v_conv_ln_b, v_conv_w_pw2=v_conv_w_pw2, v_conv_b_pw2=v_conv_b_pw2, v_pool_w=v_pool_w, v_pool_b=v_pool_b, v_pool_scale=v_pool_scale, v_ffn_w1=v_ffn_w1, v_ffn_w2=v_ffn_w2)
    weights = {n: given[n] for n in TWIN_WEIGHTS}
    shared = {n: given[n] for n in SHARED_INPUTS}
    per_example = {n: given[n] for n in ['x', 'c', 'positions']}
    grad_fn = _jax.value_and_grad(_loss, argnums=(0, 1))

    def one_microbatch(ex, loss_target):
        ex = dict(ex)
        diff = ex.pop(TWIN_DIFF_INPUT)
        return grad_fn(weights, diff, {**shared, **ex}, loss_target)

    if N_MICROBATCH == 1:
        loss, (grad_w, grad_x) = one_microbatch(per_example, given["loss_target"])
    else:
        def body(carry, xs):
            loss_sum, grad_sum = carry
            l_k, (gw_k, gx_k) = one_microbatch(xs[0], xs[1])
            with _jax.named_scope("update"):
                return (loss_sum + l_k, _jax.tree.map(_jnp.add, grad_sum, gw_k)), gx_k

        init = (_jnp.zeros((), _jnp.float32), _jax.tree.map(_jnp.zeros_like, weights))
        (loss, grad_w), grad_x = _jax.lax.scan(body, init, (per_example, given["loss_target"]))
    with _jax.named_scope("update"):
        delta_w, new_m, new_v = {}, {}, {}
        for n in TWIN_WEIGHTS:
            delta_w[n], new_m[n], new_v[n] = _adamw(weights[n], grad_w[n], given["m_" + n], given["v_" + n])
    return (loss, grad_x, *[grad_w[n] for n in TWIN_WEIGHTS], *[delta_w[n] for n in TWIN_WEIGHTS],
            *[new_m[n] for n in TWIN_WEIGHTS], *[new_v[n] for n in TWIN_WEIGHTS])
```

```python
import functools
import math

import jax
import jax.numpy as jnp
from jax import lax
from jax.experimental import pallas as pl
from jax.experimental.pallas import tpu as pltpu

F32 = jnp.float32
BF16 = jnp.bfloat16

D_MODEL = 1024
DEPTH = 4
N_HEADS = 16
QK_NOPE = 64
QK_ROPE = 32
V_HEAD = 64
Q_LORA = 384
KV_LORA = 256
HEAD_PAD = 128
QW = N_HEADS * HEAD_PAD
KVW = 2 * QW
DKV = KV_LORA + QK_ROPE
DQKV = Q_LORA + DKV
D_FF = 4096
CONV_WIDTH = 31
POOL_WINDOWS = (2, 4, 8, 16)
CHUNK_SHIFT = 6
ROPE_THETA = 10000.0
NORM_EPS = 1e-6
NEG_INF = -1e30
ATT_SCALE = 1.0 / math.sqrt(QK_NOPE + QK_ROPE)
BQ = 256
PAD_ROWS = 32
ADA_ROWS = 128
VMEM_LIMIT = 56 * 1024 * 1024

ADAM_LR = 0.001
ADAM_B1 = 0.9
ADAM_B2 = 0.999
ADAM_EPS = 1e-08
ADAM_WD = 0.01
ADAM_STEP = 10

WEIGHTS = ['ada_w', 'ada_b', 'norm_g', 'mla_w_dq', 'mla_q_norm_g', 'mla_w_uq', 'mla_w_dkv', 'mla_kv_norm_g',
           'mla_w_ukv', 'mla_w_o', 'conv_w_pw1', 'conv_b_pw1', 'conv_w_dw', 'conv_b_dw', 'conv_ln_g', 'conv_ln_b',
           'conv_w_pw2', 'conv_b_pw2', 'pool_w', 'pool_b', 'pool_scale', 'ffn_w1', 'ffn_w2']
SHARD_AXIS = {'ada_w': 2, 'ada_b': None, 'norm_g': 2, 'mla_w_dq': 1, 'mla_q_norm_g': 1, 'mla_w_uq': 2,
              'mla_w_dkv': 1, 'mla_kv_norm_g': 1, 'mla_w_ukv': 2, 'mla_w_o': 1, 'conv_w_pw1': 2,
              'conv_b_pw1': None, 'conv_w_dw': 2, 'conv_b_dw': None, 'conv_ln_g': None, 'conv_ln_b': None,
              'conv_w_pw2': 1, 'conv_b_pw2': None, 'pool_w': 2, 'pool_b': 2, 'pool_scale': 1,
              'ffn_w1': 2, 'ffn_w2': 1}
BIG = ['mla_w_dq', 'mla_w_uq', 'mla_w_dkv', 'mla_w_ukv', 'mla_w_o', 'conv_w_pw1', 'conv_w_pw2', 'pool_w']
SMALL = ['ada_b', 'norm_g', 'mla_q_norm_g', 'mla_kv_norm_g', 'conv_b_pw1', 'conv_w_dw', 'conv_b_dw',
         'conv_ln_g', 'conv_ln_b', 'conv_b_pw2', 'pool_b', 'pool_scale']


def _cparams(*sem):
    return pltpu.CompilerParams(dimension_semantics=sem, vmem_limit_bytes=VMEM_LIMIT)


def _colsum(v):
    return jnp.sum(v, axis=0, keepdims=True)


def _rowmean(v):
    return jnp.mean(v, axis=-1, keepdims=True)


def _sigmoid(v):
    return 1.0 / (1.0 + jnp.exp(-v))


def _rowk(name, fn, rows, bcast, out_row, out_acc, tm=256):
    S = rows[0].shape[0]
    tm = min(tm, S)
    assert S % tm == 0
    nin, no, na = len(rows) + len(bcast), len(out_row), len(out_acc)

    def body(*refs):
        vals = [r[...] for r in refs[:nin]]
        outs = refs[nin:nin + no]
        accs = refs[nin + no:]
        ro, ao = fn(*vals)
        for r, v in zip(outs, ro):
            r[...] = v.astype(r.dtype)
        if na:
            @pl.when(pl.program_id(0) == 0)
            def _():
                for r in accs:
                    r[...] = jnp.zeros(r.shape, r.dtype)
            for r, v in zip(accs, ao):
                r[...] += v

    in_specs = [pl.BlockSpec((tm, a.shape[1]), lambda i: (i, 0)) for a in rows]
    in_specs += [pl.BlockSpec(b.shape, lambda i, n=b.ndim: (0,) * n) for b in bcast]
    out_shape = [jax.ShapeDtypeStruct((S, w), dt) for w, dt in out_row]
    out_shape += [jax.ShapeDtypeStruct((1, w), F32) for w in out_acc]
    out_specs = [pl.BlockSpec((tm, w), lambda i: (i, 0)) for w, _ in out_row]
    out_specs += [pl.BlockSpec((1, w), lambda i: (0, 0)) for w in out_acc]
    res = pl.pallas_call(body, name=name, grid=(S // tm,), in_specs=in_specs, out_specs=out_specs,
                         out_shape=out_shape, compiler_params=_cparams("arbitrary"))(*rows, *bcast)
    return list(res[:no]), list(res[no:])


_DIMS = {'nn': ((1,), (0,)), 'nt': ((1,), (1,)), 'tn': ((0,), (0,))}


def _mm(name, a, b, mode, M, N, K, *, tm=1024, tn=1024, tk=512, a_spec=None, b_spec=None, pro_a=None,
        extras=(), extra_specs=(), epi=None, outs=None, out_specs=None):
    tm, tn, tk = (t if d % t == 0 else d for t, d in ((min(tm, M), M), (min(tn, N), N), (min(tk, K), K)))
    nk = K // tk
    if a_spec is None:
        a_spec = (pl.BlockSpec((tk, tm), lambda i, j, k: (k, i)) if mode == 'tn'
                  else pl.BlockSpec((tm, tk), lambda i, j, k: (i, k)))
    if b_spec is None:
        b_spec = (pl.BlockSpec((tn, tk), lambda i, j, k: (j, k)) if mode == 'nt'
                  else pl.BlockSpec((tk, tn), lambda i, j, k: (k, j)))
    if outs is None:
        outs = [jax.ShapeDtypeStruct((M, N), F32)]
    if out_specs is None:
        out_specs = [pl.BlockSpec((tm, tn), lambda i, j, k: (i, j)) for _ in outs]
    ne, no = len(extras), len(outs)
    dims = (_DIMS[mode], ((), ()))

    def body(a_ref, b_ref, *rest):
        ex, out_refs, acc_ref = rest[:ne], rest[ne:ne + no], rest[ne + no]
        k = pl.program_id(2)

        @pl.when(k == 0)
        def _():
            acc_ref[...] = jnp.zeros(acc_ref.shape, F32)

        av = a_ref[...]
        if pro_a is not None:
            av = pro_a(av)
        acc_ref[...] += lax.dot_general(av.astype(BF16), b_ref[...].astype(BF16), dims,
                                        preferred_element_type=F32)

        @pl.when(k == nk - 1)
        def _():
            acc = acc_ref[...]
            vals = (acc,) if epi is None else epi(acc, *[e[...] for e in ex])
            for r, v in zip(out_refs, vals):
                r[...] = v.astype(r.dtype)

    res = pl.pallas_call(
        body, name=name, grid=(M // tm, N // tn, nk),
        in_specs=[a_spec, b_spec, *extra_specs], out_specs=list(out_specs), out_shape=list(outs),
        scratch_shapes=[pltpu.VMEM((tm, tn), F32)],
        compiler_params=_cparams("parallel", "parallel", "arbitrary"))(a, b, *extras)
    return res[0] if no == 1 else list(res)


def _row_tile(R, C, itemsize=4, budget=1 << 20):
    if R * C * itemsize <= budget or R % 8:
        return R
    t = 8
    while R % (t * 2) == 0 and t * 2 * C * itemsize <= budget:
        t *= 2
    return t


def _ew(name, fn, ins, out_dtypes, shape):
    C = shape[-1]
    R = 1
    for s in shape[:-1]:
        R *= s
    tr = _row_tile(R, C)
    ops, specs = [], []
    for it in ins:
        if isinstance(it, tuple):
            arr, idx = it
            ops.append(arr.reshape(arr.shape[0], R, C))
            specs.append(pl.BlockSpec((None, tr, C), lambda i, n=idx: (n, i, 0)))
        else:
            ops.append(it.reshape(R, C))
            specs.append(pl.BlockSpec((tr, C), lambda i: (i, 0)))
    nin = len(ops)

    def body(*refs):
        vals = fn(*[r[...] for r in refs[:nin]])
        for r, v in zip(refs[nin:], vals):
            r[...] = v.astype(r.dtype)

    res = pl.pallas_call(
        body, name=name, grid=(R // tr,), in_specs=specs,
        out_specs=[pl.BlockSpec((tr, C), lambda i: (i, 0)) for _ in out_dtypes],
        out_shape=[jax.ShapeDtypeStruct((R, C), dt) for dt in out_dtypes],
        compiler_params=_cparams("parallel"))(*ops)
    return [r.reshape(shape) for r in res]


_FLIPS = {'xyc': [(fx, fy, fc) for fx in (0, 1) for fy in (0, 1) for fc in (0, 1)][1:],
          'xy': [(1, 0, 0), (0, 1, 0), (1, 1, 0)],
          'c': [(0, 0, 1)]}
_NSLOT = {'xyc': 8, 'xy': 4, 'c': 2}


def _slot(kind, cx, cy, cc):
    return {'xyc': 4 * cx + 2 * cy + cc, 'xy': 2 * cx + cy, 'c': cc}[kind]


def _exchange(name, arrays, group, src_by=None):
    flips, nsl, n = _FLIPS[group], _NSLOT[group], len(arrays)
    nf = len(flips)

    def body(*refs):
        ins, outs = refs[:n], refs[n:2 * n]
        send_sems, recv_sems, loc_sems = refs[2 * n:]
        mx, my, mc = lax.axis_index("x"), lax.axis_index("y"), lax.axis_index("c")
        me = _slot(group, mx, my, mc)

        def payload(a, cx, cy, cc):
            return ins[a] if src_by is None else ins[a].at[_slot(src_by, cx, cy, cc)]

        local = [pltpu.make_async_copy(payload(a, mx, my, mc), outs[a].at[me], loc_sems.at[a])
                 for a in range(n)]
        for cp in local:
            cp.start()
        sends, recvs = [], []
        for a in range(n):
            for f, (fx, fy, fc) in enumerate(flips):
                px = 1 - mx if fx else mx
                py = 1 - my if fy else my
                pc = 1 - mc if fc else mc
                src = payload(a, px, py, pc)
                sends.append(pltpu.make_async_remote_copy(
                    src_ref=src, dst_ref=outs[a].at[me], send_sem=send_sems.at[a, f],
                    recv_sem=recv_sems.at[a, f], device_id=(px, py, pc),
                    device_id_type=pl.DeviceIdType.MESH))
                recvs.append(pltpu.make_async_remote_copy(
                    src_ref=src, dst_ref=outs[a].at[_slot(group, px, py, pc)], send_sem=send_sems.at[a, f],
                    recv_sem=recv_sems.at[a, f], device_id=(px, py, pc),
                    device_id_type=pl.DeviceIdType.MESH))
        for cp in sends:
            cp.start()
        for cp in recvs:
            cp.wait_recv()
        for cp in sends:
            cp.wait_send()
        for cp in local:
            cp.wait()

    out_shape = [jax.ShapeDtypeStruct((nsl,) + (a.shape if src_by is None else a.shape[1:]), a.dtype)
                 for a in arrays]
    any_spec = pl.BlockSpec(memory_space=pl.ANY)
    res = pl.pallas_call(
        body, name=name, in_specs=[any_spec] * n, out_specs=[any_spec] * n, out_shape=out_shape,
        scratch_shapes=[pltpu.SemaphoreType.DMA((n, nf)), pltpu.SemaphoreType.DMA((n, nf)),
                        pltpu.SemaphoreType.DMA((n,))],
        compiler_params=pltpu.CompilerParams(has_side_effects=True))(*arrays)
    return list(res)


def _unshard(g, axis):
    t = jnp.moveaxis(g, 0, axis)
    s = t.shape
    return t.reshape(s[:axis] + (s[axis] * s[axis + 1],) + s[axis + 2:])


def _to_shards(w, axis):
    s = w.shape
    t = w.reshape(s[:axis] + (4, s[axis] // 4) + s[axis + 1:])
    return jnp.moveaxis(t, axis, 0)


def _pre_fwd(x, g, sc, sh):
    r = lax.rsqrt(_rowmean(x * x) + NORM_EPS)
    return (x * r) * g * (1.0 + sc) + sh


def _pre_bwd(dh, x, g, sc):
    r = lax.rsqrt(_rowmean(x * x) + NORM_EPS)
    xn = x * r
    dxn = dh * (g * (1.0 + sc))
    dx = r * (dxn - xn * _rowmean(dxn * xn))
    t = dh * xn
    return dx, _colsum(dh), _colsum(t * g), _colsum(t * (1.0 + sc))


def _post_fwd(x, y, gt, g):
    r = lax.rsqrt(_rowmean(y * y) + NORM_EPS)
    return x + gt * ((y * r) * g)


def _post_bwd(dxo, y, gt, g):
    r = lax.rsqrt(_rowmean(y * y) + NORM_EPS)
    yn = y * r
    t = dxo * yn
    dyn = dxo * (gt * g)
    dy = r * (dyn - yn * _rowmean(dyn * yn))
    return dy, _colsum(t * g), _colsum(t * gt)


def _gain_bwd(dy, x, g):
    r = lax.rsqrt(_rowmean(x * x) + NORM_EPS)
    xn = x * r
    dxn = dy * g
    return r * (dxn - xn * _rowmean(dxn * xn)), _colsum(dy * xn)


def _rope(x, cos, sa, sb):
    return x * cos + pltpu.roll(x, HEAD_PAD - 16, 1) * sa + pltpu.roll(x, 16, 1) * sb


def _rope_t(d, cos, sa, sb):
    return d * cos + pltpu.roll(d * sa, 16, 1) + pltpu.roll(d * sb, HEAD_PAD - 16, 1)


def _rope_tables(pos_f):
    S = pos_f.shape[0]
    inv = ROPE_THETA ** (-jnp.arange(0, QK_ROPE, 2, dtype=F32) / QK_ROPE)
    inv_ext = jnp.concatenate([jnp.zeros((QK_NOPE,), F32), inv, inv,
                               jnp.zeros((HEAD_PAD - QK_NOPE - QK_ROPE,), F32)]).reshape(1, HEAD_PAD)

    def fn(p, iv):
        ang = p * iv
        lane = lax.broadcasted_iota(jnp.int32, ang.shape, 1)
        s = jnp.sin(ang)
        first = (lane >= QK_NOPE) & (lane < QK_NOPE + QK_ROPE // 2)
        second = (lane >= QK_NOPE + QK_ROPE // 2) & (lane < QK_NOPE + QK_ROPE)
        return (jnp.cos(ang), jnp.where(first, -s, 0.0), jnp.where(second, s, 0.0)), ()

    (cos, sa, sb), _ = _rowk("rope_tables", fn, [pos_f], [inv_ext], [(HEAD_PAD, F32)] * 3, [])
    return cos, sa, sb


def _diag_mask(transposed):
    r = lax.broadcasted_iota(jnp.int32, (BQ, BQ), 0) >> CHUNK_SHIFT
    c = lax.broadcasted_iota(jnp.int32, (BQ, BQ), 1) >> CHUNK_SHIFT
    return (r <= c) if transposed else (c <= r)


_NT = (((1,), (1,)), ((), ()))
_NN = (((1,), (0,)), ((), ()))


def _attn_fwd(qf, kvf):
    S = qf.shape[0]
    nq = S // BQ

    def body(q_ref, k_ref, v_ref, o_ref, lse_ref):
        qi = pl.program_id(1)
        q = q_ref[...]

        def step(j, carry, diag):
            m, l, acc = carry
            off = pl.multiple_of(j * BQ, BQ)
            k = k_ref[pl.ds(off, BQ), :]
            v = v_ref[pl.ds(off, BQ), :]
            s = lax.dot_general(q, k, _NT, preferred_element_type=F32) * ATT_SCALE
            if diag:
                s = jnp.where(_diag_mask(False), s, NEG_INF)
            m2 = jnp.maximum(m, jnp.max(s, axis=1, keepdims=True))
            al = jnp.exp(m - m2)
            p = jnp.exp(s - m2)
            l2 = l * al + jnp.sum(p, axis=1, keepdims=True)
            acc2 = acc * al + lax.dot_general(p.astype(BF16), v, _NN, preferred_element_type=F32)
            return m2, l2, acc2

        init = (jnp.full((BQ, 1), NEG_INF, F32), jnp.zeros((BQ, 1), F32), jnp.zeros((BQ, HEAD_PAD), F32))
        carry = lax.fori_loop(0, qi, lambda j, c: step(j, c, False), init)
        m, l, acc = step(qi, carry, True)
        o_ref[...] = acc / l
        lse_ref[...] = m + jnp.log(l)

    return pl.pallas_call(
        body, name="attn_fwd", grid=(N_HEADS, nq),
        in_specs=[pl.BlockSpec((BQ, HEAD_PAD), lambda h, i: (i, h)),
                  pl.BlockSpec((S, HEAD_PAD), lambda h, i: (0, 2 * h)),
                  pl.BlockSpec((S, HEAD_PAD), lambda h, i: (0, 2 * h + 1))],
        out_specs=[pl.BlockSpec((BQ, HEAD_PAD), lambda h, i: (i, h)),
                   pl.BlockSpec((None, BQ, 1), lambda h, i: (h, i, 0))],
        out_shape=[jax.ShapeDtypeStruct((S, QW), F32), jax.ShapeDtypeStruct((N_HEADS, S, 1), F32)],
        compiler_params=_cparams("parallel", "arbitrary"))(qf, kvf, kvf)


def _attn_bwd_dq(qf, kvf, do, o, lse, cos, sa, sb):
    S = qf.shape[0]
    nq = S // BQ

    def body(q_ref, k_ref, v_ref, do_ref, o_ref, lse_ref, cos_ref, sa_ref, sb_ref, dq_ref, dd_ref):
        qi = pl.program_id(1)
        q = q_ref[...]
        dov = do_ref[...]
        dd = jnp.sum(dov * o_ref[...], axis=1, keepdims=True)
        dd_ref[...] = dd
        dob = dov.astype(BF16)
        lse_v = lse_ref[...]

        def step(j, dq, diag):
            off = pl.multiple_of(j * BQ, BQ)
            k = k_ref[pl.ds(off, BQ), :]
            v = v_ref[pl.ds(off, BQ), :]
            s = lax.dot_general(q, k, _NT, preferred_element_type=F32) * ATT_SCALE
            if diag:
                s = jnp.where(_diag_mask(False), s, NEG_INF)
            p = jnp.exp(s - lse_v)
            dp = lax.dot_general(dob, v, _NT, preferred_element_type=F32)
            ds = p * (dp - dd) * ATT_SCALE
            return dq + lax.dot_general(ds.astype(BF16), k, _NN, preferred_element_type=F32)

        dq = lax.fori_loop(0, qi, lambda j, c: step(j, c, False), jnp.zeros((BQ, HEAD_PAD), F32))
        dq = step(qi, dq, True)
        dq_ref[...] = _rope_t(dq, cos_ref[...], sa_ref[...], sb_ref[...]).astype(BF16)

    tile = pl.BlockSpec((BQ, HEAD_PAD), lambda h, i: (i, h))
    tab = pl.BlockSpec((BQ, HEAD_PAD), lambda h, i: (i, 0))
    col = pl.BlockSpec((None, BQ, 1), lambda h, i: (h, i, 0))
    return pl.pallas_call(
        body, name="attn_bwd_dq", grid=(N_HEADS, nq),
        in_specs=[tile, pl.BlockSpec((S, HEAD_PAD), lambda h, i: (0, 2 * h)),
                  pl.BlockSpec((S, HEAD_PAD), lambda h, i: (0, 2 * h + 1)), tile, tile, col, tab, tab, tab],
        out_specs=[tile, col],
        out_shape=[jax.ShapeDtypeStruct((S, QW), BF16), jax.ShapeDtypeStruct((N_HEADS, S, 1), F32)],
        compiler_params=_cparams("parallel", "arbitrary"))(qf, kvf, kvf, do, o, lse, cos, sa, sb)


def _attn_bwd_dkv(qf, kvf, do, lse_row, dd_row, cos, sa, sb):
    S = qf.shape[0]
    nq = S // BQ

    def body(k_ref, v_ref, q_ref, do_ref, lse_ref, dd_ref, cos_ref, sa_ref, sb_ref, dkv_ref):
        kj = pl.program_id(1)
        k = k_ref[...]
        v = v_ref[...]

        def step(i, carry, diag):
            dk, dv = carry
            off = pl.multiple_of(i * BQ, BQ)
            q = q_ref[pl.ds(off, BQ), :]
            dob = do_ref[pl.ds(off, BQ), :].astype(BF16)
            st = lax.dot_general(k, q, _NT, preferred_element_type=F32) * ATT_SCALE
            if diag:
                st = jnp.where(_diag_mask(True), st, NEG_INF)
            pt = jnp.exp(st - lse_ref[i])
            dv2 = dv + lax.dot_general(pt.astype(BF16), dob, _NN, preferred_element_type=F32)
            dpt = lax.dot_general(v, dob, _NT, preferred_element_type=F32)
            dst = pt * (dpt - dd_ref[i]) * ATT_SCALE
            dk2 = dk + lax.dot_general(dst.astype(BF16), q, _NN, preferred_element_type=F32)
            return dk2, dv2

        zero = jnp.zeros((BQ, HEAD_PAD), F32)
        carry = step(kj, (zero, zero), True)
        dk, dv = lax.fori_loop(kj + 1, nq, lambda i, c: step(i, c, False), carry)
        dk = _rope_t(dk, cos_ref[...], sa_ref[...], sb_ref[...])
        dkv_ref[...] = jnp.concatenate([dk, dv], axis=1).astype(BF16)

    tab = pl.BlockSpec((BQ, HEAD_PAD), lambda h, j: (j, 0))
    row = pl.BlockSpec((None, nq, 1, BQ), lambda h, j: (h, 0, 0, 0))
    return pl.pallas_call(
        body, name="attn_bwd_dkv", grid=(N_HEADS, nq),
        in_specs=[pl.BlockSpec((BQ, HEAD_PAD), lambda h, j: (j, 2 * h)),
                  pl.BlockSpec((BQ, HEAD_PAD), lambda h, j: (j, 2 * h + 1)),
                  pl.BlockSpec((S, HEAD_PAD), lambda h, j: (0, h)),
                  pl.BlockSpec((S, HEAD_PAD), lambda h, j: (0, h)), row, row, tab, tab, tab],
        out_specs=pl.BlockSpec((BQ, 2 * HEAD_PAD), lambda h, j: (j, h)),
        out_shape=jax.ShapeDtypeStruct((S, KVW), BF16),
        compiler_params=_cparams("parallel", "arbitrary"))(kvf, kvf, qf, do, lse_row, dd_row, cos, sa, sb)


DC = 128
TR = 256


def _dwconv_fwd(u, w, b):
    S, Dm = u.shape
    tr = min(TR, S)

    def body(u_ref, w_ref, b_ref, o_ref, pad_ref):
        pad_ref[pl.ds(0, PAD_ROWS), :] = jnp.zeros((PAD_ROWS, DC), F32)
        pad_ref[pl.ds(PAD_ROWS, S), :] = u_ref[...]
        wv = w_ref[...]
        for r in range(S // tr):
            acc = jnp.broadcast_to(b_ref[...], (tr, DC))
            for j in range(CONV_WIDTH):
                acc = acc + wv[j:j + 1, :] * pad_ref[pl.ds(r * tr + PAD_ROWS - (CONV_WIDTH - 1) + j, tr), :]
            o_ref[pl.ds(r * tr, tr), :] = acc

    return pl.pallas_call(
        body, name="dwconv_fwd", grid=(Dm // DC,),
        in_specs=[pl.BlockSpec((S, DC), lambda c: (0, c)), pl.BlockSpec((CONV_WIDTH, DC), lambda c: (0, c)),
                  pl.BlockSpec((1, DC), lambda c: (0, c))],
        out_specs=pl.BlockSpec((S, DC), lambda c: (0, c)),
        out_shape=jax.ShapeDtypeStruct((S, Dm), F32),
        scratch_shapes=[pltpu.VMEM((S + PAD_ROWS, DC), F32)],
        compiler_params=_cparams("parallel"))(u, w, b)


def _dwconv_bwd(d, u, w):
    S, Dm = u.shape
    tr = min(TR, S)

    def body(d_ref, u_ref, w_ref, du_ref, dw_ref, padd_ref, padu_ref):
        padd_ref[pl.ds(0, S), :] = d_ref[...]
        padd_ref[pl.ds(S, PAD_ROWS), :] = jnp.zeros((PAD_ROWS, DC), F32)
        padu_ref[pl.ds(0, PAD_ROWS), :] = jnp.zeros((PAD_ROWS, DC), F32)
        padu_ref[pl.ds(PAD_ROWS, S), :] = u_ref[...]
        wv = w_ref[...]
        dws = [jnp.zeros((1, DC), F32) for _ in range(CONV_WIDTH)]
        for r in range(S // tr):
            acc = jnp.zeros((tr, DC), F32)
            for j in range(CONV_WIDTH):
                acc = acc + wv[j:j + 1, :] * padd_ref[pl.ds(r * tr + (CONV_WIDTH - 1) - j, tr), :]
            du_ref[pl.ds(r * tr, tr), :] = acc
            dt = d_ref[pl.ds(r * tr, tr), :]
            for j in range(CONV_WIDTH):
                ut = padu_ref[pl.ds(r * tr + PAD_ROWS - (CONV_WIDTH - 1) + j, tr), :]
                dws[j] = dws[j] + _colsum(dt * ut)
        for j in range(CONV_WIDTH):
            dw_ref[pl.ds(j, 1), :] = dws[j]
        dw_ref[pl.ds(CONV_WIDTH, 1), :] = jnp.zeros((1, DC), F32)

    blk = pl.BlockSpec((S, DC), lambda c: (0, c))
    return pl.pallas_call(
        body, name="dwconv_bwd", grid=(Dm // DC,),
        in_specs=[blk, blk, pl.BlockSpec((CONV_WIDTH, DC), lambda c: (0, c))],
        out_specs=[blk, pl.BlockSpec((PAD_ROWS, DC), lambda c: (0, c))],
        out_shape=[jax.ShapeDtypeStruct((S, Dm), F32), jax.ShapeDtypeStruct((PAD_ROWS, Dm), F32)],
        scratch_shapes=[pltpu.VMEM((S + PAD_ROWS, DC), F32), pltpu.VMEM((S + PAD_ROWS, DC), F32)],
        compiler_params=_cparams("parallel"))(d, u, w)


POOL_C = D_MODEL // len(POOL_WINDOWS)
MAX_WIN = max(POOL_WINDOWS)


def _pool_counts(r, tr, win):
    t = r * tr + lax.broadcasted_iota(jnp.int32, (tr, 1), 0)
    return jnp.minimum(t + 1, win).astype(F32)


def _pool_fwd(h):
    S, Dm = h.shape
    tr = min(TR, S)

    def body(h_ref, o_ref, pad_ref):
        win = jnp.left_shift(2, pl.program_id(0))
        pad_ref[pl.ds(0, PAD_ROWS), :] = jnp.zeros((PAD_ROWS, POOL_C), F32)
        pad_ref[pl.ds(PAD_ROWS, S), :] = h_ref[...]
        for r in range(S // tr):
            acc = jnp.zeros((tr, POOL_C), F32)
            for j in range(MAX_WIN):
                use = jnp.where(j < win, 1.0, 0.0)
                acc = acc + use * pad_ref[pl.ds(r * tr + PAD_ROWS - j, tr), :]
            pooled = acc / _pool_counts(r, tr, win)
            o_ref[pl.ds(r * tr, tr), :] = (pooled - h_ref[pl.ds(r * tr, tr), :]).astype(BF16)

    blk = pl.BlockSpec((S, POOL_C), lambda g: (0, g))
    return pl.pallas_call(
        body, name="pool_fwd", grid=(len(POOL_WINDOWS),), in_specs=[blk], out_specs=blk,
        out_shape=jax.ShapeDtypeStruct((S, Dm), BF16),
        scratch_shapes=[pltpu.VMEM((S + PAD_ROWS, POOL_C), F32)],
        compiler_params=_cparams("parallel"))(h)


def _pool_bwd(dp):
    S, Dm = dp.shape
    tr = min(TR, S)

    def body(d_ref, o_ref, pad_ref):
        win = jnp.left_shift(2, pl.program_id(0))
        for r in range(S // tr):
            pad_ref[pl.ds(r * tr, tr), :] = d_ref[pl.ds(r * tr, tr), :] / _pool_counts(r, tr, win)
        pad_ref[pl.ds(S, PAD_ROWS), :] = jnp.zeros((PAD_ROWS, POOL_C), F32)
        for r in range(S // tr):
            acc = jnp.zeros((tr, POOL_C), F32)
            for j in range(MAX_WIN):
                use = jnp.where(j < win, 1.0, 0.0)
                acc = acc + use * pad_ref[pl.ds(r * tr + j, tr), :]
            o_ref[pl.ds(r * tr, tr), :] = acc - d_ref[pl.ds(r * tr, tr), :]

    blk = pl.BlockSpec((S, POOL_C), lambda g: (0, g))
    return pl.pallas_call(
        body, name="pool_bwd", grid=(len(POOL_WINDOWS),), in_specs=[blk], out_specs=blk,
        out_shape=jax.ShapeDtypeStruct((S, Dm), F32),
        scratch_shapes=[pltpu.VMEM((S + PAD_ROWS, POOL_C), F32)],
        compiler_params=_cparams("parallel"))(dp)


def _bias_spec(tn):
    return pl.BlockSpec((1, tn), lambda i, j, k: (0, j))


def _mla_weights(w_dq, w_dkv, w_uq, w_ukv, w_o):
    wd = jnp.concatenate([w_dq, w_dkv], axis=1)
    wq = jnp.pad(w_uq.reshape(Q_LORA, N_HEADS, QK_NOPE + QK_ROPE),
                 ((0, 0), (0, 0), (0, HEAD_PAD - QK_NOPE - QK_ROPE))).reshape(Q_LORA, QW)
    ukv = w_ukv.reshape(KV_LORA, N_HEADS, QK_NOPE + V_HEAD)
    wkv = jnp.zeros((DKV, N_HEADS, 2 * HEAD_PAD), BF16)
    wkv = wkv.at[:KV_LORA, :, :QK_NOPE].set(ukv[:, :, :QK_NOPE])
    wkv = wkv.at[:KV_LORA, :, HEAD_PAD:HEAD_PAD + V_HEAD].set(ukv[:, :, QK_NOPE:])
    eye = jnp.broadcast_to(jnp.eye(QK_ROPE, dtype=BF16)[:, None, :], (QK_ROPE, N_HEADS, QK_ROPE))
    wkv = wkv.at[KV_LORA:, :, QK_NOPE:QK_NOPE + QK_ROPE].set(eye).reshape(DKV, KVW)
    wo = jnp.pad(w_o.reshape(N_HEADS, V_HEAD, D_MODEL),
                 ((0, 0), (0, HEAD_PAD - V_HEAD), (0, 0))).reshape(QW, D_MODEL)
    return dict(wd=wd, wq=wq, wkv=wkv, wo=wo)


def _mla_weight_grads(g_wd, g_wq, g_wkv, g_wo):
    g_uq = g_wq.reshape(Q_LORA, N_HEADS, HEAD_PAD)[:, :, :QK_NOPE + QK_ROPE].reshape(Q_LORA, -1)
    t = g_wkv.reshape(DKV, N_HEADS, 2 * HEAD_PAD)[:KV_LORA]
    g_ukv = jnp.concatenate([t[:, :, :QK_NOPE], t[:, :, HEAD_PAD:HEAD_PAD + V_HEAD]], axis=2)
    g_o = g_wo.reshape(N_HEADS, HEAD_PAD, D_MODEL)[:, :V_HEAD].reshape(N_HEADS * V_HEAD, D_MODEL)
    return dict(mla_w_dq=g_wd[:, :Q_LORA], mla_w_uq=g_uq, mla_w_dkv=g_wd[:, Q_LORA:],
                mla_w_ukv=g_ukv.reshape(KV_LORA, -1), mla_w_o=g_o)


def _rope_epilogue(kv):
    def epi(acc, cos, sa, sb):
        parts = []
        for t in range(acc.shape[1] // HEAD_PAD):
            x = acc[:, t * HEAD_PAD:(t + 1) * HEAD_PAD]
            parts.append(x if (kv and t % 2) else _rope(x, cos, sa, sb))
        return (jnp.concatenate(parts, axis=1),)
    return epi


def _mla_fwd(tag, h, P, rope):
    S = h.shape[0]
    cos, sa, sb = rope
    tabs = [pl.BlockSpec((min(1024, S), HEAD_PAD), lambda i, j, k: (i, 0))] * 3
    cqkv = _mm(f"mla_down{tag}", h, P['wd'], 'nn', S, DQKV, D_MODEL)

    def norms(x, qg, kg):
        xq, xk, xr = x[:, :Q_LORA], x[:, Q_LORA:Q_LORA + KV_LORA], x[:, Q_LORA + KV_LORA:]
        cq = xq * lax.rsqrt(_rowmean(xq * xq) + NORM_EPS) * qg
        ck = xk * lax.rsqrt(_rowmean(xk * xk) + NORM_EPS) * kg
        return (cq, jnp.concatenate([ck, xr], axis=1)), ()

    (cq, ckv), _ = _rowk(f"mla_norms{tag}", norms, [cqkv], [P['qg'], P['kg']], [(Q_LORA, BF16), (DKV, BF16)], [])
    qf = _mm(f"mla_q{tag}", cq, P['wq'], 'nn', S, QW, Q_LORA, extras=[cos, sa, sb], extra_specs=tabs,
             epi=_rope_epilogue(False), outs=[jax.ShapeDtypeStruct((S, QW), BF16)])
    kvf = _mm(f"mla_kv{tag}", ckv, P['wkv'], 'nn', S, KVW, DKV, extras=[cos, sa, sb], extra_specs=tabs,
              epi=_rope_epilogue(True), outs=[jax.ShapeDtypeStruct((S, KVW), BF16)])
    o, lse = _attn_fwd(qf, kvf)
    y = _mm(f"mla_o{tag}", o, P['wo'], 'nn', S, D_MODEL, QW)
    return y, dict(cqkv=cqkv, cq=cq, ckv=ckv, qf=qf, kvf=kvf, o=o, lse=lse)


def _mla_bwd(tag, dy, h, sv, P, rope):
    S = h.shape[0]
    nq = S // BQ
    cos, sa, sb = rope
    g_wo = _mm(f"mla_o_wg{tag}", sv['o'], dy, 'tn', QW, D_MODEL, S)
    do = _mm(f"mla_o_dg{tag}", dy, P['wo'], 'nt', S, QW, D_MODEL)
    dq, dd = _attn_bwd_dq(sv['qf'], sv['kvf'], do, sv['o'], sv['lse'], cos, sa, sb)
    lse_row = sv['lse'].reshape(N_HEADS, nq, 1, BQ)
    dd_row = dd.reshape(N_HEADS, nq, 1, BQ)
    dkv = _attn_bwd_dkv(sv['qf'], sv['kvf'], do, lse_row, dd_row, cos, sa, sb)
    g_wq = _mm(f"mla_q_wg{tag}", sv['cq'], dq, 'tn', Q_LORA, QW, S)
    dcq = _mm(f"mla_q_dg{tag}", dq, P['wq'], 'nt', S, Q_LORA, QW)
    g_wkv = _mm(f"mla_kv_wg{tag}", sv['ckv'], dkv, 'tn', DKV, KVW, S)
    dckv = _mm(f"mla_kv_dg{tag}", dkv, P['wkv'], 'nt', S, DKV, KVW)

    def norms_bwd(dcq_v, dckv_v, x, qg, kg):
        xq, xk = x[:, :Q_LORA], x[:, Q_LORA:Q_LORA + KV_LORA]
        dxq, dqg = _gain_bwd(dcq_v, xq, qg)
        dxk, dkg = _gain_bwd(dckv_v[:, :KV_LORA], xk, kg)
        return (jnp.concatenate([dxq, dxk, dckv_v[:, KV_LORA:]], axis=1),), (dqg, dkg)

    (dcqkv,), (dqg, dkg) = _rowk(f"mla_norms_bwd{tag}", norms_bwd, [dcq, dckv, sv['cqkv']], [P['qg'], P['kg']],
                                 [(DQKV, BF16)], [Q_LORA, KV_LORA])
    g_wd = _mm(f"mla_down_wg{tag}", h, dcqkv, 'tn', D_MODEL, DQKV, S)
    dh = _mm(f"mla_down_dg{tag}", dcqkv, P['wd'], 'nt', S, D_MODEL, DQKV)
    grads = _mla_weight_grads(g_wd, g_wq, g_wkv, g_wo)
    grads.update(mla_q_norm_g=dqg.reshape(-1), mla_kv_norm_g=dkg.reshape(-1))
    return dh, grads


def _conv_fwd(h, P):
    S = h.shape[0]
    a = _mm("conv_pw1", h, P['w_pw1'], 'nn', S, 2 * D_MODEL, D_MODEL, extras=[P['b_pw1']],
            extra_specs=[_bias_spec(1024)], epi=lambda acc, b: (acc + b,))
    (u0,), _ = _rowk("conv_glu", lambda av: ((av[:, :D_MODEL] * _sigmoid(av[:, D_MODEL:]),), ()),
                     [a], [], [(D_MODEL, F32)], [])
    u1 = _dwconv_fwd(u0, P['w_dw'], P['b_dw'])

    def ln_silu(u, g, b):
        xc = u - _rowmean(u)
        z = xc * lax.rsqrt(_rowmean(xc * xc) + NORM_EPS) * g + b
        return (z * _sigmoid(z),), ()

    (u3,), _ = _rowk("conv_ln", ln_silu, [u1], [P['ln_g'], P['ln_b']], [(D_MODEL, BF16)], [])
    y = _mm("conv_pw2", u3, P['w_pw2'], 'nn', S, D_MODEL, D_MODEL, extras=[P['b_pw2']],
            extra_specs=[_bias_spec(1024)], epi=lambda acc, b: (acc + b,))
    return y, dict(a=a, u0=u0, u1=u1, u3=u3)


def _conv_bwd(dy, dy_colsum, h, sv, P):
    S = h.shape[0]
    g_pw2 = _mm("conv_pw2_wg", sv['u3'], dy, 'tn', D_MODEL, D_MODEL, S)
    du3 = _mm("conv_pw2_dg", dy, P['w_pw2'], 'nt', S, D_MODEL, D_MODEL)

    def ln_bwd(d3, u, g, b):
        xc = u - _rowmean(u)
        rstd = lax.rsqrt(_rowmean(xc * xc) + NORM_EPS)
        xh = xc * rstd
        z = xh * g + b
        sg = _sigmoid(z)
        dz = d3 * (sg * (1.0 + z * (1.0 - sg)))
        dxh = dz * g
        du = rstd * (dxh - _rowmean(dxh) - xh * _rowmean(dxh * xh))
        return (du,), (_colsum(dz * xh), _colsum(dz), _colsum(du))

    (du1,), (d_lng, d_lnb, d_bdw) = _rowk("conv_ln_bwd", ln_bwd, [du3, sv['u1']], [P['ln_g'], P['ln_b']],
                                          [(D_MODEL, F32)], [D_MODEL] * 3)
    du0, d_wdw = _dwconv_bwd(du1, sv['u0'], P['w_dw'])

    def glu_bwd(d0, av):
        a1, sg = av[:, :D_MODEL], _sigmoid(av[:, D_MODEL:])
        da = jnp.concatenate([d0 * sg, d0 * a1 * sg * (1.0 - sg)], axis=1)
        return (da,), (_colsum(da),)

    (da,), (d_bpw1,) = _rowk("conv_glu_bwd", glu_bwd, [du0, sv['a']], [], [(2 * D_MODEL, BF16)], [2 * D_MODEL])
    g_pw1 = _mm("conv_pw1_wg", h, da, 'tn', D_MODEL, 2 * D_MODEL, S)
    dh = _mm("conv_pw1_dg", da, P['w_pw1'], 'nt', S, D_MODEL, 2 * D_MODEL)
    grads = dict(conv_w_pw1=g_pw1, conv_b_pw1=d_bpw1.reshape(-1), conv_w_dw=d_wdw[:CONV_WIDTH],
                 conv_b_dw=d_bdw.reshape(-1), conv_ln_g=d_lng.reshape(-1), conv_ln_b=d_lnb.reshape(-1),
                 conv_w_pw2=g_pw2, conv_b_pw2=dy_colsum.reshape(-1))
    return dh, grads


def _pool_group_specs(tm):
    return (pl.BlockSpec((tm, POOL_C), lambda i, j, k: (i, j)),
            pl.BlockSpec((None, POOL_C, POOL_C), lambda i, j, k: (j, 0, 0)))


def _pool_mixer_fwd(h, P):
    S = h.shape[0]
    p = _pool_fwd(h)
    a_spec, b_spec = _pool_group_specs(min(1024, S))
    y, z = _mm("pool_mm", p, P['w'], 'nn', S, D_MODEL, POOL_C, tn=POOL_C, a_spec=a_spec, b_spec=b_spec,
               extras=[P['b'], P['scale']], extra_specs=[_bias_spec(POOL_C)] * 2,
               epi=lambda acc, b, s: ((acc + b) * s, acc + b),
               outs=[jax.ShapeDtypeStruct((S, D_MODEL), F32)] * 2)
    return y, dict(p=p, z=z)


def _pool_mixer_bwd(dy, sv, P):
    S = dy.shape[0]

    def scale_bwd(d, z, s):
        dz = d * s
        return (dz,), (_colsum(d * z), _colsum(dz))

    (dz,), (d_scale, d_b) = _rowk("pool_scale_bwd", scale_bwd, [dy, sv['z']], [P['scale']],
                                  [(D_MODEL, BF16)], [D_MODEL] * 2)
    a_spec, b_spec = _pool_group_specs(min(1024, S))
    dp = _mm("pool_mm_dg", dz, P['w'], 'nt', S, D_MODEL, POOL_C, tn=POOL_C, a_spec=a_spec, b_spec=b_spec)
    tk = min(512, S)
    grp = pl.BlockSpec((tk, POOL_C), lambda i, j, k: (k, j))
    g_w = _mm("pool_mm_wg", sv['p'], dz, 'tn', POOL_C, D_MODEL, S, tn=POOL_C, tk=tk, a_spec=grp, b_spec=grp,
              outs=[jax.ShapeDtypeStruct((len(POOL_WINDOWS), POOL_C, POOL_C), F32)],
              out_specs=[pl.BlockSpec((None, POOL_C, POOL_C), lambda i, j, k: (j, 0, 0))])
    dh = _pool_bwd(dp)
    return dh, dict(pool_w=g_w, pool_b=d_b.reshape(-1), pool_scale=d_scale.reshape(-1))


def _adamw(w, g, m, v):
    m2 = ADAM_B1 * m + (1.0 - ADAM_B1) * g
    v2 = ADAM_B2 * v + (1.0 - ADAM_B2) * (g * g)
    m_hat = m2 / (1.0 - ADAM_B1 ** ADAM_STEP)
    v_hat = v2 / (1.0 - ADAM_B2 ** ADAM_STEP)
    delta = -ADAM_LR * (m_hat / (jnp.sqrt(v_hat) + ADAM_EPS) + ADAM_WD * w)
    return delta, m2, v2


def _row(v):
    return v.reshape(1, -1)


def kernel(x, c, positions, ada_w, ada_b, norm_g, mla_w_dq, mla_q_norm_g, mla_w_uq, mla_w_dkv, mla_kv_norm_g, mla_w_ukv, mla_w_o, conv_w_pw1, conv_b_pw1, conv_w_dw, conv_b_dw, conv_ln_g, conv_ln_b, conv_w_pw2, conv_b_pw2, pool_w, pool_b, pool_scale, ffn_w1, ffn_w2, loss_target, m_ada_w, m_ada_b, m_norm_g, m_mla_w_dq, m_mla_q_norm_g, m_mla_w_uq, m_mla_w_dkv, m_mla_kv_norm_g, m_mla_w_ukv, m_mla_w_o, m_conv_w_pw1, m_conv_b_pw1, m_conv_w_dw, m_conv_b_dw, m_conv_ln_g, m_conv_ln_b, m_conv_w_pw2, m_conv_b_pw2, m_pool_w, m_pool_b, m_pool_scale, m_ffn_w1, m_ffn_w2, v_ada_w, v_ada_b, v_norm_g, v_mla_w_dq, v_mla_q_norm_g, v_mla_w_uq, v_mla_w_dkv, v_mla_kv_norm_g, v_mla_w_ukv, v_mla_w_o, v_conv_w_pw1, v_conv_b_pw1, v_conv_w_dw, v_conv_b_dw, v_conv_ln_g, v_conv_ln_b, v_conv_w_pw2, v_conv_b_pw2, v_pool_w, v_pool_b, v_pool_scale, v_ffn_w1, v_ffn_w2):
    args = dict(locals())
    W = {n: args[n] for n in WEIGHTS}
    MOM = {n: args['m_' + n] for n in WEIGHTS}
    VAR = {n: args['v_' + n] for n in WEIGHTS}
    S = x.shape[1]
    xs = x.reshape(S, D_MODEL)
    tgt = loss_target.reshape(S, D_MODEL)
    mx, my, mc = lax.axis_index("x"), lax.axis_index("y"), lax.axis_index("c")
    chip = 2 * mx + my
    n_sh = ada_w.shape[2]

    c8 = _exchange("gather_c", [c.reshape(8, D_MODEL // 8)], 'xyc')[0].reshape(8, D_MODEL)
    c8 = jnp.pad(c8, ((0, ADA_ROWS - 8), (0, 0)))
    silu = lambda v: v * _sigmoid(v)
    mod_sh = []
    for l in range(DEPTH):
        b_l = lax.dynamic_slice(ada_b[l], (chip * n_sh,), (n_sh,)).reshape(1, n_sh)
        mod_sh.append(_mm(f"ada_fwd{l}", c8, ada_w, 'nn', ADA_ROWS, n_sh, D_MODEL, tn=n_sh // 2, pro_a=silu,
                          b_spec=pl.BlockSpec((None, 512, n_sh // 2), lambda i, j, k, l=l: (l, k, j)),
                          extras=[b_l], extra_specs=[_bias_spec(n_sh // 2)], epi=lambda acc, b: (acc + b,))[:8])
    mod_sh = jnp.stack(mod_sh, axis=1).reshape(8, DEPTH * n_sh // 128, 128)
    mod = _exchange("scatter_mod", [mod_sh], 'xy', src_by='xyc')[0]
    mod = mod.reshape(4, DEPTH, n_sh).transpose(1, 0, 2).reshape(DEPTH, 6, 1, D_MODEL)

    names = [n for n in WEIGHTS if n not in ('ada_w', 'ada_b') and SHARD_AXIS[n] is not None]
    sent = [W[n].astype(BF16) if n in BIG or n in ('ffn_w1', 'ffn_w2') else W[n] for n in names]
    G = dict(zip(names, _exchange("gather_w", sent, 'xy')))
    full = {n: _unshard(G[n], SHARD_AXIS[n]) for n in names if n not in ('ffn_w1', 'ffn_w2')}
    gains = full['norm_g']
    mla_p = []
    for j in range(mla_w_dq.shape[0]):
        P = _mla_weights(full['mla_w_dq'][j], full['mla_w_dkv'][j], full['mla_w_uq'][j], full['mla_w_ukv'][j],
                         full['mla_w_o'][j])
        P.update(qg=_row(full['mla_q_norm_g'][j]), kg=_row(full['mla_kv_norm_g'][j]))
        mla_p.append(P)
    conv_p = dict(w_pw1=full['conv_w_pw1'][0], b_pw1=_row(conv_b_pw1[0]), w_dw=full['conv_w_dw'][0],
                  b_dw=_row(conv_b_dw[0]), ln_g=_row(conv_ln_g[0]), ln_b=_row(conv_ln_b[0]),
                  w_pw2=full['conv_w_pw2'][0], b_pw2=_row(conv_b_pw2[0]))
    pool_p = dict(w=full['pool_w'][0], b=_row(full['pool_b'][0]), scale=_row(full['pool_scale'][0]))
    rope = _rope_tables(positions.reshape(S, 1).astype(F32))

    w1_nn = lambda l: pl.BlockSpec((None, None, 512, 1024), lambda i, j, k: (j, l, k, 0))
    w1_nt = lambda l: pl.BlockSpec((None, None, 1024, 512), lambda i, j, k: (k // 2, l, j, k % 2))
    w2_nn = lambda l: pl.BlockSpec((None, None, 512, 1024), lambda i, j, k: (k // 2, l, k % 2, j))
    w2_nt = lambda l: pl.BlockSpec((None, None, 1024, 512), lambda i, j, k: (j, l, 0, k))
    sq_relu = lambda v: jnp.square(jnp.maximum(v, 0.0))

    def md(i, k):
        return mod[i, k]

    (h,), _ = _rowk("pre0", lambda xv, g, sc, sh: ((_pre_fwd(xv, g, sc, sh),), ()),
                    [xs], [_row(gains[0, 0]), md(0, 1), md(0, 0)], [(D_MODEL, BF16)], [])
    saved = []
    xin = xs
    loss_acc = dxf = None
    for i in range(DEPTH):
        kind, j = i % 3, i // 3
        if kind == 0:
            y, sv = _mla_fwd(j, h, mla_p[j], rope)
        elif kind == 1:
            y, sv = _conv_fwd(h, conv_p)
        else:
            y, sv = _pool_mixer_fwd(h, pool_p)

        def mid(xv, yv, gt, g1, g2, sc, sh):
            x1 = _post_fwd(xv, yv, gt, g1)
            return (x1, _pre_fwd(x1, g2, sc, sh)), ()

        (x1, h2), _ = _rowk(f"mid{i}", mid, [xin, y], [md(i, 2), _row(gains[i, 1]), _row(gains[i, 2]), md(i, 4), md(i, 3)],
                            [(D_MODEL, F32), (D_MODEL, BF16)], [])
        a = _mm(f"ffn1_{i}", h2, G['ffn_w1'], 'nn', S, D_FF, D_MODEL, b_spec=w1_nn(i))
        y2 = _mm(f"ffn2_{i}", a, G['ffn_w2'], 'nn', S, D_MODEL, D_FF, pro_a=sq_relu, b_spec=w2_nn(i))
        saved.append(dict(x0=xin, h=h, y=y, x1=x1, h2=h2, a=a, y2=y2, mix=sv))
        if i + 1 < DEPTH:
            def nxt(xv, yv, gt, g3, g0, sc, sh):
                x2 = _post_fwd(xv, yv, gt, g3)
                return (x2, _pre_fwd(x2, g0, sc, sh)), ()

            hdt = F32 if (i + 1) % 3 == 2 else BF16
            (xin, h), _ = _rowk(f"next{i}", nxt, [x1, y2],
                                [md(i, 5), _row(gains[i, 3]), _row(gains[i + 1, 0]), md(i + 1, 1), md(i + 1, 0)],
                                [(D_MODEL, F32), (D_MODEL, hdt)], [])
        else:
            def head(xv, yv, tv, gt, g3):
                err = _post_fwd(xv, yv, gt, g3) - tv
                per_row = jnp.sum(err * err, axis=1, keepdims=True) * (0.5 / D_MODEL)
                return (err * (1.0 / D_MODEL),), (jnp.broadcast_to(jnp.sum(per_row, axis=0, keepdims=True), (1, 128)),)

            (dxf,), (loss_acc,) = _rowk("loss_head", head, [x1, y2, tgt], [md(i, 5), _row(gains[i, 3])],
                                        [(D_MODEL, F32)], [128])
    loss = lax.psum(loss_acc[0, 0], ("x", "y", "c"))

    grads = {}
    ffn_g1, ffn_g2 = [None] * DEPTH, [None] * DEPTH
    d_mod = [None] * DEPTH
    d_gain = [None] * DEPTH
    dx = dxf
    for i in reversed(range(DEPTH)):
        kind, j = i % 3, i // 3
        sv = saved[i]
        def post2_bwd(d, yv, gt, g):
            dyv, d_gt, d_g = _post_bwd(d, yv, gt, g)
            return (dyv,), (d_gt, d_g)

        (dy2,), (d_gtf, d_g3) = _rowk(f"post2_bwd{i}", post2_bwd, [dx, sv['y2']], [md(i, 5), _row(gains[i, 3])],
                                      [(D_MODEL, BF16)], [D_MODEL] * 2)
        da = _mm(f"ffn2_dg{i}", dy2, G['ffn_w2'], 'nt', S, D_FF, D_MODEL, b_spec=w2_nt(i), extras=[sv['a']],
                 extra_specs=[pl.BlockSpec((min(1024, S), 1024), lambda i_, j_, k_: (i_, j_))],
                 epi=lambda acc, av: (acc * (2.0 * jnp.maximum(av, 0.0)),),
                 outs=[jax.ShapeDtypeStruct((S, D_FF), BF16)])
        ffn_g2[i] = _mm(f"ffn2_wg{i}", sv['a'], dy2, 'tn', D_FF, D_MODEL, S, pro_a=sq_relu,
                        outs=[jax.ShapeDtypeStruct((4, 1024, D_MODEL), BF16)],
                        out_specs=[pl.BlockSpec((None, 1024, 1024), lambda i_, j_, k_: (i_, 0, j_))])
        ffn_g1[i] = _mm(f"ffn1_wg{i}", sv['h2'], da, 'tn', D_MODEL, D_FF, S,
                        outs=[jax.ShapeDtypeStruct((4, D_MODEL, 1024), BF16)],
                        out_specs=[pl.BlockSpec((None, 1024, 1024), lambda i_, j_, k_: (j_, i_, 0))])
        dh2 = _mm(f"ffn1_dg{i}", da, G['ffn_w1'], 'nt', S, D_MODEL, D_FF, b_spec=w1_nt(i))

        def mid_bwd(d2, dh2v, x1v, yv, g2, scf, gtm, g1):
            dpre, d_sh, d_sc, d_g2 = _pre_bwd(dh2v, x1v, g2, scf)
            d1 = d2 + dpre
            dyv, d_gt, d_g1 = _post_bwd(d1, yv, gtm, g1)
            return (d1, dyv), (d_sh, d_sc, d_g2, d_gt, d_g1, _colsum(dyv))

        ydt = F32 if kind == 2 else BF16
        (dx1, dy), (d_shf, d_scf, d_g2, d_gtm, d_g1, dy_cs) = _rowk(
            f"mid_bwd{i}", mid_bwd, [dx, dh2, sv['x1'], sv['y']],
            [_row(gains[i, 2]), md(i, 4), md(i, 2), _row(gains[i, 1])],
            [(D_MODEL, F32), (D_MODEL, ydt)], [D_MODEL] * 6)
        if kind == 0:
            dh, gm = _mla_bwd(j, dy, sv['h'], sv['mix'], mla_p[j], rope)
            for n, g in gm.items():
                grads.setdefault(n, [None] * mla_w_dq.shape[0])[j] = g
        elif kind == 1:
            dh, gm = _conv_bwd(dy, dy_cs, sv['h'], sv['mix'], conv_p)
            for n, g in gm.items():
                grads[n] = [g]
        else:
            dh, gm = _pool_mixer_bwd(dy, sv['mix'], pool_p)
            for n, g in gm.items():
                grads[n] = [g]

        def pre_bwd(d1, dhv, x0v, g0, scm):
            dpre, d_sh, d_sc, d_g0 = _pre_bwd(dhv, x0v, g0, scm)
            return (d1 + dpre,), (d_sh, d_sc, d_g0)

        (dx,), (d_shm, d_scm, d_g0) = _rowk(f"pre_bwd{i}", pre_bwd, [dx1, dh, sv['x0']],
                                            [_row(gains[i, 0]), md(i, 1)], [(D_MODEL, F32)], [D_MODEL] * 3)
        d_mod[i] = jnp.concatenate([d_shm, d_scm, d_gtm, d_shf, d_scf, d_gtf], axis=1).reshape(-1)
        d_gain[i] = jnp.concatenate([d_g0, d_g1, d_g2, d_g3], axis=0)
    grad_x = dx.reshape(x.shape)
    grads = {n: jnp.stack(g) for n, g in grads.items()}
    grads['norm_g'] = jnp.stack(d_gain)
    grads['ada_b'] = jnp.stack(d_mod)

    pack = jnp.concatenate([grads[n].reshape(-1) for n in SMALL])
    n_pack = pack.shape[0]
    rows = -(-n_pack // 1024) * 8
    pack = jnp.pad(pack, (0, rows * 128 - n_pack)).reshape(rows, 128)
    pack8 = _exchange("gather_small", [pack], 'xyc')[0]
    (tot,) = _ew("sum_small", lambda *v: (functools.reduce(lambda p, q: p + q, v),), [(pack8, s) for s in range(8)],
                 [F32], (rows, 128))
    tot = tot.reshape(-1)
    d_mod_all = pack8.reshape(8, -1)[:, :DEPTH * 6 * D_MODEL].reshape(8, DEPTH, 6 * D_MODEL)
    final = {}
    off = 0
    for n in SMALL:
        ax = SHARD_AXIS[n]
        shape = tuple(d * 4 if k == ax else d for k, d in enumerate(W[n].shape))
        size = grads[n].size
        g = tot[off:off + size].reshape(shape)
        off += size
        if ax is not None:
            g = lax.dynamic_index_in_dim(_to_shards(g, ax), chip, 0, keepdims=False)
        final[n] = g

    g_ada = []
    for l in range(DEPTH):
        dm_l = jnp.pad(lax.dynamic_slice(d_mod_all[:, l], (0, chip * n_sh), (8, n_sh)), ((0, ADA_ROWS - 8), (0, 0)))
        g_ada.append(_mm(f"ada_wg{l}", c8, dm_l, 'tn', D_MODEL, n_sh, ADA_ROWS, tn=n_sh // 2, pro_a=silu))
    final['ada_w'] = jnp.stack(g_ada)

    red_names, red = [], []
    for n in BIG:
        red_names.append(n)
        red.append(_to_shards(grads[n], SHARD_AXIS[n]).astype(BF16))
    for l in range(DEPTH):
        red_names += [f"ffn_w1:{l}", f"ffn_w2:{l}"]
        red += [ffn_g1[l], ffn_g2[l]]
    landed = _exchange("scatter_grads", red, 'xy', src_by='xy')
    partial = []
    for n, arr in zip(red_names, landed):
        (p_,) = _ew("sum_" + n.replace(':', '_'), lambda a0, a1, a2, a3: (((a0.astype(F32) + a1.astype(F32)) + a2.astype(F32)) + a3.astype(F32),),
                    [(arr, s) for s in range(4)], [F32], arr.shape[1:])
        partial.append(p_)
    pairs = dict(zip(red_names, _exchange("swap_grads", partial, 'c')))

    out_g, out_d, out_m, out_v = {}, {}, {}, {}
    for n in WEIGHTS:
        shape = W[n].shape
        if n in ('ffn_w1', 'ffn_w2'):
            res = [_ew(f"adamw_{n}{l}", lambda w, g0, g1, m, v: (lambda g: (g,) + _adamw(w, g, m, v))(g0 + g1),
                       [W[n][l], (pairs[f"{n}:{l}"], 0), (pairs[f"{n}:{l}"], 1), MOM[n][l], VAR[n][l]],
                       [F32] * 4, shape[1:]) for l in range(DEPTH)]
            out_g[n], out_d[n], out_m[n], out_v[n] = [jnp.stack([r[k] for r in res]) for k in range(4)]
        elif n in BIG:
            out_g[n], out_d[n], out_m[n], out_v[n] = _ew(
                f"adamw_{n}", lambda w, g0, g1, m, v: (lambda g: (g,) + _adamw(w, g, m, v))(g0 + g1),
                [W[n], (pairs[n], 0), (pairs[n], 1), MOM[n], VAR[n]], [F32] * 4, shape)
        else:
            out_g[n] = final[n].reshape(shape)
            out_d[n], out_m[n], out_v[n] = _ew(f"adamw_{n}", lambda w, g, m, v: _adamw(w, g, m, v),
                                               [W[n], out_g[n], MOM[n], VAR[n]], [F32] * 3, shape)
    return (loss, grad_x, *[out_g[n] for n in WEIGHTS], *[out_d[n] for n in WEIGHTS],
            *[out_m[n] for n in WEIGHTS], *[out_v[n] for n in WEIGHTS])
```

```python
import functools
import math

import jax
import jax.numpy as jnp
from jax import lax
from jax.experimental import pallas as pl
from jax.experimental.pallas import tpu as pltpu

F32 = jnp.float32
BF16 = jnp.bfloat16

D_MODEL = 1024
DEPTH = 4
N_HEADS = 16
QK_NOPE = 64
QK_ROPE = 32
V_HEAD = 64
Q_LORA = 384
KV_LORA = 256
HEAD_PAD = 128
QW = N_HEADS * HEAD_PAD
KVW = 2 * QW
DKV = KV_LORA + QK_ROPE
DQKV = Q_LORA + DKV
D_FF = 4096
CONV_WIDTH = 31
POOL_WINDOWS = (2, 4, 8, 16)
CHUNK_SHIFT = 6
ROPE_THETA = 10000.0
NORM_EPS = 1e-6
NEG_INF = -1e30
ATT_SCALE = 1.0 / math.sqrt(QK_NOPE + QK_ROPE)
BQ = 256
HB = 4
LOG2E = 1.4426950408889634
SCALE_LOG2E = ATT_SCALE * LOG2E
PAD_ROWS = 32
ADA_ROWS = 128
VMEM_LIMIT = 56 * 1024 * 1024

ADAM_LR = 0.001
ADAM_B1 = 0.9
ADAM_B2 = 0.999
ADAM_EPS = 1e-08
ADAM_WD = 0.01
ADAM_STEP = 10

WEIGHTS = ['ada_w', 'ada_b', 'norm_g', 'mla_w_dq', 'mla_q_norm_g', 'mla_w_uq', 'mla_w_dkv', 'mla_kv_norm_g',
           'mla_w_ukv', 'mla_w_o', 'conv_w_pw1', 'conv_b_pw1', 'conv_w_dw', 'conv_b_dw', 'conv_ln_g', 'conv_ln_b',
           'conv_w_pw2', 'conv_b_pw2', 'pool_w', 'pool_b', 'pool_scale', 'ffn_w1', 'ffn_w2']
SHARD_AXIS = {'ada_w': 2, 'ada_b': None, 'norm_g': 2, 'mla_w_dq': 1, 'mla_q_norm_g': 1, 'mla_w_uq': 2,
              'mla_w_dkv': 1, 'mla_kv_norm_g': 1, 'mla_w_ukv': 2, 'mla_w_o': 1, 'conv_w_pw1': 2,
              'conv_b_pw1': None, 'conv_w_dw': 2, 'conv_b_dw': None, 'conv_ln_g': None, 'conv_ln_b': None,
              'conv_w_pw2': 1, 'conv_b_pw2': None, 'pool_w': 2, 'pool_b': 2, 'pool_scale': 1,
              'ffn_w1': 2, 'ffn_w2': 1}
BIG = ['mla_w_dq', 'mla_w_uq', 'mla_w_dkv', 'mla_w_ukv', 'mla_w_o', 'conv_w_pw1', 'conv_w_pw2', 'pool_w']
SMALL = ['ada_b', 'norm_g', 'mla_q_norm_g', 'mla_kv_norm_g', 'conv_b_pw1', 'conv_w_dw', 'conv_b_dw',
         'conv_ln_g', 'conv_ln_b', 'conv_b_pw2', 'pool_b', 'pool_scale']


def _cparams(*sem):
    return pltpu.CompilerParams(dimension_semantics=sem, vmem_limit_bytes=VMEM_LIMIT)


def _colsum(v):
    return jnp.sum(v, axis=0, keepdims=True)


def _rowmean(v):
    return jnp.mean(v, axis=-1, keepdims=True)


def _sigmoid(v):
    return 1.0 / (1.0 + jnp.exp(-v))


def _rowk(name, fn, rows, bcast, out_row, out_acc, tm=256):
    S = rows[0].shape[0]
    tm = min(tm, S)
    assert S % tm == 0
    nin, no, na = len(rows) + len(bcast), len(out_row), len(out_acc)

    def body(*refs):
        vals = [r[...] for r in refs[:nin]]
        outs = refs[nin:nin + no]
        accs = refs[nin + no:]
        ro, ao = fn(*vals)
        for r, v in zip(outs, ro):
            r[...] = v.astype(r.dtype)
        if na:
            @pl.when(pl.program_id(0) == 0)
            def _():
                for r in accs:
                    r[...] = jnp.zeros(r.shape, r.dtype)
            for r, v in zip(accs, ao):
                r[...] += v

    in_specs = [pl.BlockSpec((tm, a.shape[1]), lambda i: (i, 0)) for a in rows]
    in_specs += [pl.BlockSpec(b.shape, lambda i, n=b.ndim: (0,) * n) for b in bcast]
    out_shape = [jax.ShapeDtypeStruct((S, w), dt) for w, dt in out_row]
    out_shape += [jax.ShapeDtypeStruct((1, w), F32) for w in out_acc]
    out_specs = [pl.BlockSpec((tm, w), lambda i: (i, 0)) for w, _ in out_row]
    out_specs += [pl.BlockSpec((1, w), lambda i: (0, 0)) for w in out_acc]
    res = pl.pallas_call(body, name=name, grid=(S // tm,), in_specs=in_specs, out_specs=out_specs,
                         out_shape=out_shape, compiler_params=_cparams("arbitrary"))(*rows, *bcast)
    return list(res[:no]), list(res[no:])


_DIMS = {'nn': ((1,), (0,)), 'nt': ((1,), (1,)), 'tn': ((0,), (0,))}


def _mm(name, a, b, mode, M, N, K, *, tm=1024, tn=1024, tk=1024, a_spec=None, b_spec=None, pro_a=None,
        extras=(), extra_specs=(), epi=None, outs=None, out_specs=None):
    tm, tn, tk = (t if d % t == 0 else d for t, d in ((min(tm, M), M), (min(tn, N), N), (min(tk, K), K)))
    nk = K // tk
    if a_spec is None:
        a_spec = (pl.BlockSpec((tk, tm), lambda i, j, k: (k, i)) if mode == 'tn'
                  else pl.BlockSpec((tm, tk), lambda i, j, k: (i, k)))
    if b_spec is None:
        b_spec = (pl.BlockSpec((tn, tk), lambda i, j, k: (j, k)) if mode == 'nt'
                  else pl.BlockSpec((tk, tn), lambda i, j, k: (k, j)))
    if outs is None:
        outs = [jax.ShapeDtypeStruct((M, N), F32)]
    if out_specs is None:
        out_specs = [pl.BlockSpec((tm, tn), lambda i, j, k: (i, j)) for _ in outs]
    ne, no = len(extras), len(outs)
    dims = (_DIMS[mode], ((), ()))

    def body(a_ref, b_ref, *rest):
        ex, out_refs = rest[:ne], rest[ne:ne + no]
        av = a_ref[...]
        if pro_a is not None:
            av = pro_a(av)
        part = lax.dot_general(av.astype(BF16), b_ref[...].astype(BF16), dims, preferred_element_type=F32)

        def finish(acc):
            vals = (acc,) if epi is None else epi(acc, *[e[...] for e in ex])
            for r, v in zip(out_refs, vals):
                r[...] = v.astype(r.dtype)

        if nk == 1:
            finish(part)
            return
        acc_ref = rest[ne + no]
        k = pl.program_id(2)

        @pl.when(k == 0)
        def _():
            acc_ref[...] = part

        @pl.when(k > 0)
        def _():
            acc_ref[...] += part

        @pl.when(k == nk - 1)
        def _():
            finish(acc_ref[...])

    res = pl.pallas_call(
        body, name=name, grid=(M // tm, N // tn, nk),
        in_specs=[a_spec, b_spec, *extra_specs], out_specs=list(out_specs), out_shape=list(outs),
        scratch_shapes=[pltpu.VMEM((tm, tn), F32)] if nk > 1 else [],
        compiler_params=_cparams("parallel", "parallel", "arbitrary"))(a, b, *extras)
    return res[0] if no == 1 else list(res)


def _row_tile(R, C, itemsize=4, budget=1 << 20):
    if R * C * itemsize <= budget or R % 8:
        return R
    t = 8
    while R % (t * 2) == 0 and t * 2 * C * itemsize <= budget:
        t *= 2
    return t


def _ew(name, fn, ins, out_dtypes, shape):
    C = shape[-1]
    R = 1
    for s in shape[:-1]:
        R *= s
    tr = _row_tile(R, C)
    ops, specs = [], []
    for it in ins:
        if isinstance(it, tuple):
            arr, idx = it
            ops.append(arr.reshape(arr.shape[0], R, C))
            specs.append(pl.BlockSpec((None, tr, C), lambda i, n=idx: (n, i, 0)))
        else:
            ops.append(it.reshape(R, C))
            specs.append(pl.BlockSpec((tr, C), lambda i: (i, 0)))
    nin = len(ops)

    def body(*refs):
        vals = fn(*[r[...] for r in refs[:nin]])
        for r, v in zip(refs[nin:], vals):
            r[...] = v.astype(r.dtype)

    res = pl.pallas_call(
        body, name=name, grid=(R // tr,), in_specs=specs,
        out_specs=[pl.BlockSpec((tr, C), lambda i: (i, 0)) for _ in out_dtypes],
        out_shape=[jax.ShapeDtypeStruct((R, C), dt) for dt in out_dtypes],
        compiler_params=_cparams("parallel"))(*ops)
    return [r.reshape(shape) for r in res]


_FLIPS = {'xyc': [(fx, fy, fc) for fx in (0, 1) for fy in (0, 1) for fc in (0, 1)][1:],
          'xy': [(1, 0, 0), (0, 1, 0), (1, 1, 0)],
          'c': [(0, 0, 1)]}
_NSLOT = {'xyc': 8, 'xy': 4, 'c': 2}


def _slot(kind, cx, cy, cc):
    return {'xyc': 4 * cx + 2 * cy + cc, 'xy': 2 * cx + cy, 'c': cc}[kind]


def _exchange(name, arrays, group, src_by=None):
    flips, nsl, n = _FLIPS[group], _NSLOT[group], len(arrays)
    nf = len(flips)

    def body(*refs):
        ins, outs = refs[:n], refs[n:2 * n]
        send_sems, recv_sems, loc_sems = refs[2 * n:]
        mx, my, mc = lax.axis_index("x"), lax.axis_index("y"), lax.axis_index("c")
        me = _slot(group, mx, my, mc)

        def payload(a, cx, cy, cc):
            return ins[a] if src_by is None else ins[a].at[_slot(src_by, cx, cy, cc)]

        local = [pltpu.make_async_copy(payload(a, mx, my, mc), outs[a].at[me], loc_sems.at[a])
                 for a in range(n)]
        for cp in local:
            cp.start()
        sends, recvs = [], []
        for a in range(n):
            for f, (fx, fy, fc) in enumerate(flips):
                px = 1 - mx if fx else mx
                py = 1 - my if fy else my
                pc = 1 - mc if fc else mc
                src = payload(a, px, py, pc)
                sends.append(pltpu.make_async_remote_copy(
                    src_ref=src, dst_ref=outs[a].at[me], send_sem=send_sems.at[a, f],
                    recv_sem=recv_sems.at[a, f], device_id=(px, py, pc),
                    device_id_type=pl.DeviceIdType.MESH))
                recvs.append(pltpu.make_async_remote_copy(
                    src_ref=src, dst_ref=outs[a].at[_slot(group, px, py, pc)], send_sem=send_sems.at[a, f],
                    recv_sem=recv_sems.at[a, f], device_id=(px, py, pc),
                    device_id_type=pl.DeviceIdType.MESH))
        for cp in sends:
            cp.start()
        for cp in recvs:
            cp.wait_recv()
        for cp in sends:
            cp.wait_send()
        for cp in local:
            cp.wait()

    out_shape = [jax.ShapeDtypeStruct((nsl,) + (a.shape if src_by is None else a.shape[1:]), a.dtype)
                 for a in arrays]
    any_spec = pl.BlockSpec(memory_space=pl.ANY)
    res = pl.pallas_call(
        body, name=name, in_specs=[any_spec] * n, out_specs=[any_spec] * n, out_shape=out_shape,
        scratch_shapes=[pltpu.SemaphoreType.DMA((n, nf)), pltpu.SemaphoreType.DMA((n, nf)),
                        pltpu.SemaphoreType.DMA((n,))],
        compiler_params=pltpu.CompilerParams(has_side_effects=True))(*arrays)
    return list(res)


def _unshard(g, axis):
    t = jnp.moveaxis(g, 0, axis)
    s = t.shape
    return t.reshape(s[:axis] + (s[axis] * s[axis + 1],) + s[axis + 2:])


def _to_shards(w, axis):
    s = w.shape
    t = w.reshape(s[:axis] + (4, s[axis] // 4) + s[axis + 1:])
    return jnp.moveaxis(t, axis, 0)


def _pre_fwd(x, g, sc, sh):
    r = lax.rsqrt(_rowmean(x * x) + NORM_EPS)
    return (x * r) * g * (1.0 + sc) + sh


def _pre_bwd(dh, x, g, sc):
    r = lax.rsqrt(_rowmean(x * x) + NORM_EPS)
    xn = x * r
    dxn = dh * (g * (1.0 + sc))
    dx = r * (dxn - xn * _rowmean(dxn * xn))
    t = dh * xn
    return dx, _colsum(dh), _colsum(t * g), _colsum(t * (1.0 + sc))


def _post_fwd(x, y, gt, g):
    r = lax.rsqrt(_rowmean(y * y) + NORM_EPS)
    return x + gt * ((y * r) * g)


def _post_bwd(dxo, y, gt, g):
    r = lax.rsqrt(_rowmean(y * y) + NORM_EPS)
    yn = y * r
    t = dxo * yn
    dyn = dxo * (gt * g)
    dy = r * (dyn - yn * _rowmean(dyn * yn))
    return dy, _colsum(t * g), _colsum(t * gt)


def _gain_bwd(dy, x, g):
    r = lax.rsqrt(_rowmean(x * x) + NORM_EPS)
    xn = x * r
    dxn = dy * g
    return r * (dxn - xn * _rowmean(dxn * xn)), _colsum(dy * xn)


def _rope(x, cos, sa, sb):
    return x * cos + pltpu.roll(x, HEAD_PAD - 16, 1) * sa + pltpu.roll(x, 16, 1) * sb


def _rope_t(d, cos, sa, sb):
    return d * cos + pltpu.roll(d * sa, 16, 1) + pltpu.roll(d * sb, HEAD_PAD - 16, 1)


def _rope_tables(pos_f):
    S = pos_f.shape[0]
    inv = ROPE_THETA ** (-jnp.arange(0, QK_ROPE, 2, dtype=F32) / QK_ROPE)
    inv_ext = jnp.concatenate([jnp.zeros((QK_NOPE,), F32), inv, inv,
                               jnp.zeros((HEAD_PAD - QK_NOPE - QK_ROPE,), F32)]).reshape(1, HEAD_PAD)

    def fn(p, iv):
        ang = p * iv
        lane = lax.broadcasted_iota(jnp.int32, ang.shape, 1)
        s = jnp.sin(ang)
        first = (lane >= QK_NOPE) & (lane < QK_NOPE + QK_ROPE // 2)
        second = (lane >= QK_NOPE + QK_ROPE // 2) & (lane < QK_NOPE + QK_ROPE)
        return (jnp.cos(ang), jnp.where(first, -s, 0.0), jnp.where(second, s, 0.0)), ()

    (cos, sa, sb), _ = _rowk("rope_tables", fn, [pos_f], [inv_ext], [(HEAD_PAD, F32)] * 3, [])
    return cos, sa, sb


def _diag_mask(transposed):
    r = lax.broadcasted_iota(jnp.int32, (BQ, BQ), 0) >> CHUNK_SHIFT
    c = lax.broadcasted_iota(jnp.int32, (BQ, BQ), 1) >> CHUNK_SHIFT
    return (r <= c) if transposed else (c <= r)


_NT = (((1,), (1,)), ((), ()))
_NN = (((1,), (0,)), ((), ()))


def _attn_fwd(qf, kvf):
    S = qf.shape[0]
    nq = S // BQ

    def body(q_ref, kv_ref, o_ref, lse_ref):
        qi = pl.program_id(1)
        qs = [q_ref[:, hh * HEAD_PAD:(hh + 1) * HEAD_PAD] for hh in range(HB)]

        def step(j, carry, diag):
            off = pl.multiple_of(j * BQ, BQ)
            sts = [lax.dot_general(kv_ref[pl.ds(off, BQ), pl.ds(2 * hh * HEAD_PAD, HEAD_PAD)], qs[hh], _NT,
                                   preferred_element_type=F32) for hh in range(HB)]
            mid = []
            for hh in range(HB):
                m, l, acc = carry[hh]
                st = jnp.where(_diag_mask(True), sts[hh], NEG_INF) if diag else sts[hh]
                m2 = jnp.maximum(m, jnp.max(st, axis=0, keepdims=True))
                al = jnp.exp2((m - m2) * SCALE_LOG2E)
                pt = jnp.exp2((st - m2) * SCALE_LOG2E)
                mid.append((m2, l * al + jnp.sum(pt, axis=0, keepdims=True), acc * al, pt.astype(BF16)))
            out = []
            for hh in range(HB):
                m2, l2, acc_s, ptb = mid[hh]
                v = kv_ref[pl.ds(off, BQ), pl.ds((2 * hh + 1) * HEAD_PAD, HEAD_PAD)]
                out.append((m2, l2, acc_s + lax.dot_general(v, ptb, _TN, preferred_element_type=F32)))
            return tuple(out)

        init = tuple((jnp.full((1, BQ), NEG_INF, F32), jnp.zeros((1, BQ), F32), jnp.zeros((HEAD_PAD, BQ), F32))
                     for _ in range(HB))
        carry = lax.fori_loop(0, qi, lambda j, c: step(j, c, False), init)
        carry = step(qi, carry, True)
        for hh in range(HB):
            m, l, acc = carry[hh]
            o_ref[:, hh * HEAD_PAD:(hh + 1) * HEAD_PAD] = (acc / l).T
            lse_ref[hh] = m * SCALE_LOG2E + jnp.log(l) * LOG2E

    return pl.pallas_call(
        body, name="attn_fwd", grid=(N_HEADS // HB, nq),
        in_specs=[pl.BlockSpec((BQ, HB * HEAD_PAD), lambda g, i: (i, g)),
                  pl.BlockSpec((S, 2 * HB * HEAD_PAD), lambda g, i: (0, g))],
        out_specs=[pl.BlockSpec((BQ, HB * HEAD_PAD), lambda g, i: (i, g)),
                   pl.BlockSpec((HB, None, 1, BQ), lambda g, i: (g, i, 0, 0))],
        out_shape=[jax.ShapeDtypeStruct((S, QW), F32), jax.ShapeDtypeStruct((N_HEADS, nq, 1, BQ), F32)],
        compiler_params=_cparams("parallel", "arbitrary"))(qf, kvf)


def _attn_delta(dob, o):
    S = o.shape[0]

    def body(do_ref, o_ref, dd_ref):
        dd_ref[...] = jnp.sum(do_ref[...].astype(F32) * o_ref[...], axis=1, keepdims=True)

    blk = pl.BlockSpec((S, HEAD_PAD), lambda h: (0, h))
    return pl.pallas_call(
        body, name="attn_delta", grid=(N_HEADS,), in_specs=[blk, blk],
        out_specs=pl.BlockSpec((None, S, 1), lambda h: (h, 0, 0)),
        out_shape=jax.ShapeDtypeStruct((N_HEADS, S, 1), F32),
        compiler_params=_cparams("parallel"))(dob, o)


_TN = (((0,), (0,)), ((), ()))


def _attn_bwd(qf, kvf, dob, lse_row, dd_row, cos, sa, sb):
    S = qf.shape[0]
    nq = S // BQ

    def body(kv_ref, q_ref, do_ref, lse_ref, dd_ref, cos_ref, sa_ref, sb_ref, dq_ref, dkv_ref):
        kj = pl.program_id(1)

        @pl.when(kj == 0)
        def _():
            dq_ref[...] = jnp.zeros(dq_ref.shape, F32)

        ks = [kv_ref[:, 2 * hh * HEAD_PAD:(2 * hh + 1) * HEAD_PAD] for hh in range(HB)]
        vs = [kv_ref[:, (2 * hh + 1) * HEAD_PAD:(2 * hh + 2) * HEAD_PAD] for hh in range(HB)]

        def step(i, carry, diag):
            off = pl.multiple_of(i * BQ, BQ)
            cols = [pl.ds(hh * HEAD_PAD, HEAD_PAD) for hh in range(HB)]
            q = [q_ref[pl.ds(off, BQ), cols[hh]] for hh in range(HB)]
            do = [do_ref[pl.ds(off, BQ), cols[hh]] for hh in range(HB)]
            sts = [lax.dot_general(ks[hh], q[hh], _NT, preferred_element_type=F32) for hh in range(HB)]
            dpts = [lax.dot_general(vs[hh], do[hh], _NT, preferred_element_type=F32) for hh in range(HB)]
            mid = []
            for hh in range(HB):
                st = jnp.where(_diag_mask(True), sts[hh], NEG_INF) if diag else sts[hh]
                pt = jnp.exp2(st * SCALE_LOG2E - lse_ref[hh, i])
                mid.append((pt.astype(BF16), (pt * (dpts[hh] - dd_ref[hh, i])).astype(BF16)))
            out = []
            for hh in range(HB):
                dk, dv = carry[hh]
                ptb, dsb = mid[hh]
                dv2 = dv + lax.dot_general(ptb, do[hh], _NN, preferred_element_type=F32)
                dk2 = dk + lax.dot_general(dsb, q[hh], _NN, preferred_element_type=F32)
                dq_ref[pl.ds(off, BQ), cols[hh]] += lax.dot_general(dsb, ks[hh], _TN, preferred_element_type=F32)
                out.append((dk2, dv2))
            return tuple(out)

        zero = jnp.zeros((BQ, HEAD_PAD), F32)
        carry = step(kj, tuple((zero, zero) for _ in range(HB)), True)
        carry = lax.fori_loop(kj + 1, nq, lambda i, c: step(i, c, False), carry)
        for hh in range(HB):
            dk, dv = carry[hh]
            dk = _rope_t(dk * ATT_SCALE, cos_ref[...], sa_ref[...], sb_ref[...])
            dkv_ref[:, 2 * hh * HEAD_PAD:(2 * hh + 1) * HEAD_PAD] = dk.astype(BF16)
            dkv_ref[:, (2 * hh + 1) * HEAD_PAD:(2 * hh + 2) * HEAD_PAD] = dv.astype(BF16)

    tab = pl.BlockSpec((BQ, HEAD_PAD), lambda g, j: (j, 0))
    row = pl.BlockSpec((HB, nq, 1, BQ), lambda g, j: (g, 0, 0, 0))
    seq = pl.BlockSpec((S, HB * HEAD_PAD), lambda g, j: (0, g))
    kvb = pl.BlockSpec((BQ, 2 * HB * HEAD_PAD), lambda g, j: (j, g))
    return pl.pallas_call(
        body, name="attn_bwd", grid=(N_HEADS // HB, nq),
        in_specs=[kvb, seq, seq, row, row, tab, tab, tab],
        out_specs=[seq, kvb],
        out_shape=[jax.ShapeDtypeStruct((S, QW), F32), jax.ShapeDtypeStruct((S, KVW), BF16)],
        compiler_params=_cparams("parallel", "arbitrary"))(kvf, qf, dob, lse_row, dd_row, cos, sa, sb)


DC = 128
TR = 256


def _dwconv_fwd(u, w, b):
    S, Dm = u.shape
    tr = min(TR, S)

    def body(u_ref, w_ref, b_ref, o_ref, pad_ref):
        pad_ref[pl.ds(0, PAD_ROWS), :] = jnp.zeros((PAD_ROWS, DC), F32)
        pad_ref[pl.ds(PAD_ROWS, S), :] = u_ref[...]
        wv = w_ref[...]
        for r in range(S // tr):
            acc = jnp.broadcast_to(b_ref[...], (tr, DC))
            for j in range(CONV_WIDTH):
                acc = acc + wv[j:j + 1, :] * pad_ref[pl.ds(r * tr + PAD_ROWS - (CONV_WIDTH - 1) + j, tr), :]
            o_ref[pl.ds(r * tr, tr), :] = acc

    return pl.pallas_call(
        body, name="dwconv_fwd", grid=(Dm // DC,),
        in_specs=[pl.BlockSpec((S, DC), lambda c: (0, c)), pl.BlockSpec((CONV_WIDTH, DC), lambda c: (0, c)),
                  pl.BlockSpec((1, DC), lambda c: (0, c))],
        out_specs=pl.BlockSpec((S, DC), lambda c: (0, c)),
        out_shape=jax.ShapeDtypeStruct((S, Dm), F32),
        scratch_shapes=[pltpu.VMEM((S + PAD_ROWS, DC), F32)],
        compiler_params=_cparams("parallel"))(u, w, b)


def _dwconv_bwd(d, u, w):
    S, Dm = u.shape
    tr = min(TR, S)

    def body(d_ref, u_ref, w_ref, du_ref, dw_ref, padd_ref, padu_ref):
        padd_ref[pl.ds(0, S), :] = d_ref[...]
        padd_ref[pl.ds(S, PAD_ROWS), :] = jnp.zeros((PAD_ROWS, DC), F32)
        padu_ref[pl.ds(0, PAD_ROWS), :] = jnp.zeros((PAD_ROWS, DC), F32)
        padu_ref[pl.ds(PAD_ROWS, S), :] = u_ref[...]
        wv = w_ref[...]
        dws = [jnp.zeros((1, DC), F32) for _ in range(CONV_WIDTH)]
        for r in range(S // tr):
            acc = jnp.zeros((tr, DC), F32)
            for j in range(CONV_WIDTH):
                acc = acc + wv[j:j + 1, :] * padd_ref[pl.ds(r * tr + (CONV_WIDTH - 1) - j, tr), :]
            du_ref[pl.ds(r * tr, tr), :] = acc
            dt = d_ref[pl.ds(r * tr, tr), :]
            for j in range(CONV_WIDTH):
                ut = padu_ref[pl.ds(r * tr + PAD_ROWS - (CONV_WIDTH - 1) + j, tr), :]
                dws[j] = dws[j] + _colsum(dt * ut)
        for j in range(CONV_WIDTH):
            dw_ref[pl.ds(j, 1), :] = dws[j]
        dw_ref[pl.ds(CONV_WIDTH, 1), :] = jnp.zeros((1, DC), F32)

    blk = pl.BlockSpec((S, DC), lambda c: (0, c))
    return pl.pallas_call(
        body, name="dwconv_bwd", grid=(Dm // DC,),
        in_specs=[blk, blk, pl.BlockSpec((CONV_WIDTH, DC), lambda c: (0, c))],
        out_specs=[blk, pl.BlockSpec((PAD_ROWS, DC), lambda c: (0, c))],
        out_shape=[jax.ShapeDtypeStruct((S, Dm), F32), jax.ShapeDtypeStruct((PAD_ROWS, Dm), F32)],
        scratch_shapes=[pltpu.VMEM((S + PAD_ROWS, DC), F32), pltpu.VMEM((S + PAD_ROWS, DC), F32)],
        compiler_params=_cparams("parallel"))(d, u, w)


POOL_C = D_MODEL // len(POOL_WINDOWS)
MAX_WIN = max(POOL_WINDOWS)


def _pool_counts(r, tr, win):
    t = r * tr + lax.broadcasted_iota(jnp.int32, (tr, 1), 0)
    return jnp.minimum(t + 1, win).astype(F32)


def _pool_fwd(h):
    S, Dm = h.shape
    tr = min(TR, S)

    def body(h_ref, o_ref, pad_ref):
        win = jnp.left_shift(2, pl.program_id(0))
        pad_ref[pl.ds(0, PAD_ROWS), :] = jnp.zeros((PAD_ROWS, POOL_C), F32)
        pad_ref[pl.ds(PAD_ROWS, S), :] = h_ref[...]
        for r in range(S // tr):
            acc = jnp.zeros((tr, POOL_C), F32)
            for j in range(MAX_WIN):
                use = jnp.where(j < win, 1.0, 0.0)
                acc = acc + use * pad_ref[pl.ds(r * tr + PAD_ROWS - j, tr), :]
            pooled = acc / _pool_counts(r, tr, win)
            o_ref[pl.ds(r * tr, tr), :] = (pooled - h_ref[pl.ds(r * tr, tr), :]).astype(BF16)

    blk = pl.BlockSpec((S, POOL_C), lambda g: (0, g))
    return pl.pallas_call(
        body, name="pool_fwd", grid=(len(POOL_WINDOWS),), in_specs=[blk], out_specs=blk,
        out_shape=jax.ShapeDtypeStruct((S, Dm), BF16),
        scratch_shapes=[pltpu.VMEM((S + PAD_ROWS, POOL_C), F32)],
        compiler_params=_cparams("parallel"))(h)


def _pool_bwd(dp):
    S, Dm = dp.shape
    tr = min(TR, S)

    def body(d_ref, o_ref, pad_ref):
        win = jnp.left_shift(2, pl.program_id(0))
        for r in range(S // tr):
            pad_ref[pl.ds(r * tr, tr), :] = d_ref[pl.ds(r * tr, tr), :] / _pool_counts(r, tr, win)
        pad_ref[pl.ds(S, PAD_ROWS), :] = jnp.zeros((PAD_ROWS, POOL_C), F32)
        for r in range(S // tr):
            acc = jnp.zeros((tr, POOL_C), F32)
            for j in range(MAX_WIN):
                use = jnp.where(j < win, 1.0, 0.0)
                acc = acc + use * pad_ref[pl.ds(r * tr + j, tr), :]
            o_ref[pl.ds(r * tr, tr), :] = acc - d_ref[pl.ds(r * tr, tr), :]

    blk = pl.BlockSpec((S, POOL_C), lambda g: (0, g))
    return pl.pallas_call(
        body, name="pool_bwd", grid=(len(POOL_WINDOWS),), in_specs=[blk], out_specs=blk,
        out_shape=jax.ShapeDtypeStruct((S, Dm), F32),
        scratch_shapes=[pltpu.VMEM((S + PAD_ROWS, POOL_C), F32)],
        compiler_params=_cparams("parallel"))(dp)


def _bias_spec(tn):
    return pl.BlockSpec((1, tn), lambda i, j, k: (0, j))


def _mla_weights(w_dq, w_dkv, w_uq, w_ukv, w_o):
    wd = jnp.concatenate([w_dq, w_dkv], axis=1)
    wq = jnp.pad(w_uq.reshape(Q_LORA, N_HEADS, QK_NOPE + QK_ROPE),
                 ((0, 0), (0, 0), (0, HEAD_PAD - QK_NOPE - QK_ROPE))).reshape(Q_LORA, QW)
    ukv = w_ukv.reshape(KV_LORA, N_HEADS, QK_NOPE + V_HEAD)
    wkv = jnp.zeros((DKV, N_HEADS, 2 * HEAD_PAD), BF16)
    wkv = wkv.at[:KV_LORA, :, :QK_NOPE].set(ukv[:, :, :QK_NOPE])
    wkv = wkv.at[:KV_LORA, :, HEAD_PAD:HEAD_PAD + V_HEAD].set(ukv[:, :, QK_NOPE:])
    eye = jnp.broadcast_to(jnp.eye(QK_ROPE, dtype=BF16)[:, None, :], (QK_ROPE, N_HEADS, QK_ROPE))
    wkv = wkv.at[KV_LORA:, :, QK_NOPE:QK_NOPE + QK_ROPE].set(eye).reshape(DKV, KVW)
    wo = jnp.pad(w_o.reshape(N_HEADS, V_HEAD, D_MODEL),
                 ((0, 0), (0, HEAD_PAD - V_HEAD), (0, 0))).reshape(QW, D_MODEL)
    return dict(wd=wd, wq=wq, wkv=wkv, wo=wo)


def _mla_weight_grads(g_wd, g_wq, g_wkv, g_wo):
    g_uq = g_wq.reshape(Q_LORA, N_HEADS, HEAD_PAD)[:, :, :QK_NOPE + QK_ROPE].reshape(Q_LORA, -1)
    t = g_wkv.reshape(DKV, N_HEADS, 2 * HEAD_PAD)[:KV_LORA]
    g_ukv = jnp.concatenate([t[:, :, :QK_NOPE], t[:, :, HEAD_PAD:HEAD_PAD + V_HEAD]], axis=2)
    g_o = g_wo.reshape(N_HEADS, HEAD_PAD, D_MODEL)[:, :V_HEAD].reshape(N_HEADS * V_HEAD, D_MODEL)
    return dict(mla_w_dq=g_wd[:, :Q_LORA], mla_w_uq=g_uq, mla_w_dkv=g_wd[:, Q_LORA:],
                mla_w_ukv=g_ukv.reshape(KV_LORA, -1), mla_w_o=g_o)


def _rope_epilogue(kv):
    def epi(acc, cos, sa, sb):
        parts = []
        for t in range(acc.shape[1] // HEAD_PAD):
            x = acc[:, t * HEAD_PAD:(t + 1) * HEAD_PAD]
            parts.append(x if (kv and t % 2) else _rope(x, cos, sa, sb))
        return (jnp.concatenate(parts, axis=1),)
    return epi


def _mla_fwd(tag, h, P, rope):
    S = h.shape[0]
    cos, sa, sb = rope
    tabs = [pl.BlockSpec((min(1024, S), HEAD_PAD), lambda i, j, k: (i, 0))] * 3
    cqkv = _mm(f"mla_down{tag}", h, P['wd'], 'nn', S, DQKV, D_MODEL)

    def norms(x, qg, kg):
        xq, xk, xr = x[:, :Q_LORA], x[:, Q_LORA:Q_LORA + KV_LORA], x[:, Q_LORA + KV_LORA:]
        cq = xq * lax.rsqrt(_rowmean(xq * xq) + NORM_EPS) * qg
        ck = xk * lax.rsqrt(_rowmean(xk * xk) + NORM_EPS) * kg
        return (cq, jnp.concatenate([ck, xr], axis=1)), ()

    (cq, ckv), _ = _rowk(f"mla_norms{tag}", norms, [cqkv], [P['qg'], P['kg']], [(Q_LORA, BF16), (DKV, BF16)], [])
    qf = _mm(f"mla_q{tag}", cq, P['wq'], 'nn', S, QW, Q_LORA, extras=[cos, sa, sb], extra_specs=tabs,
             epi=_rope_epilogue(False), outs=[jax.ShapeDtypeStruct((S, QW), BF16)])
    kvf = _mm(f"mla_kv{tag}", ckv, P['wkv'], 'nn', S, KVW, DKV, extras=[cos, sa, sb], extra_specs=tabs,
              epi=_rope_epilogue(True), outs=[jax.ShapeDtypeStruct((S, KVW), BF16)])
    o, lse = _attn_fwd(qf, kvf)
    y = _mm(f"mla_o{tag}", o, P['wo'], 'nn', S, D_MODEL, QW)
    return y, dict(cqkv=cqkv, cq=cq, ckv=ckv, qf=qf, kvf=kvf, o=o, lse=lse)


def _mla_bwd(tag, dy, h, sv, P, rope):
    S = h.shape[0]
    nq = S // BQ
    cos, sa, sb = rope
    g_wo = _mm(f"mla_o_wg{tag}", sv['o'], dy, 'tn', QW, D_MODEL, S)
    dob = _mm(f"mla_o_dg{tag}", dy, P['wo'], 'nt', S, QW, D_MODEL, outs=[jax.ShapeDtypeStruct((S, QW), BF16)])
    dd = _attn_delta(dob, sv['o'])
    dq_raw, dkv = _attn_bwd(sv['qf'], sv['kvf'], dob, sv['lse'].reshape(N_HEADS, nq, 1, BQ),
                            dd.reshape(N_HEADS, nq, 1, BQ), cos, sa, sb)

    def rope_bwd_q(d, cv, sav, sbv):
        parts = [_rope_t(d[:, t * HEAD_PAD:(t + 1) * HEAD_PAD] * ATT_SCALE, cv, sav, sbv) for t in range(N_HEADS)]
        return (jnp.concatenate(parts, axis=1),), ()

    (dq,), _ = _rowk(f"rope_bwd_q{tag}", rope_bwd_q, [dq_raw, cos, sa, sb], [], [(QW, BF16)], [])
    g_wq = _mm(f"mla_q_wg{tag}", sv['cq'], dq, 'tn', Q_LORA, QW, S)
    dcq = _mm(f"mla_q_dg{tag}", dq, P['wq'], 'nt', S, Q_LORA, QW)
    g_wkv = _mm(f"mla_kv_wg{tag}", sv['ckv'], dkv, 'tn', DKV, KVW, S)
    dckv = _mm(f"mla_kv_dg{tag}", dkv, P['wkv'], 'nt', S, DKV, KVW)

    def norms_bwd(dcq_v, dckv_v, x, qg, kg):
        xq, xk = x[:, :Q_LORA], x[:, Q_LORA:Q_LORA + KV_LORA]
        dxq, dqg = _gain_bwd(dcq_v, xq, qg)
        dxk, dkg = _gain_bwd(dckv_v[:, :KV_LORA], xk, kg)
        return (jnp.concatenate([dxq, dxk, dckv_v[:, KV_LORA:]], axis=1),), (dqg, dkg)

    (dcqkv,), (dqg, dkg) = _rowk(f"mla_norms_bwd{tag}", norms_bwd, [dcq, dckv, sv['cqkv']], [P['qg'], P['kg']],
                                 [(DQKV, BF16)], [Q_LORA, KV_LORA])
    g_wd = _mm(f"mla_down_wg{tag}", h, dcqkv, 'tn', D_MODEL, DQKV, S)
    dh = _mm(f"mla_down_dg{tag}", dcqkv, P['wd'], 'nt', S, D_MODEL, DQKV)
    grads = _mla_weight_grads(g_wd, g_wq, g_wkv, g_wo)
    grads.update(mla_q_norm_g=dqg.reshape(-1), mla_kv_norm_g=dkg.reshape(-1))
    return dh, grads


def _conv_fwd(h, P):
    S = h.shape[0]
    a = _mm("conv_pw1", h, P['w_pw1'], 'nn', S, 2 * D_MODEL, D_MODEL, extras=[P['b_pw1']],
            extra_specs=[_bias_spec(1024)], epi=lambda acc, b: (acc + b,))
    (u0,), _ = _rowk("conv_glu", lambda av: ((av[:, :D_MODEL] * _sigmoid(av[:, D_MODEL:]),), ()),
                     [a], [], [(D_MODEL, F32)], [])
    u1 = _dwconv_fwd(u0, P['w_dw'], P['b_dw'])

    def ln_silu(u, g, b):
        xc = u - _rowmean(u)
        z = xc * lax.rsqrt(_rowmean(xc * xc) + NORM_EPS) * g + b
        return (z * _sigmoid(z),), ()

    (u3,), _ = _rowk("conv_ln", ln_silu, [u1], [P['ln_g'], P['ln_b']], [(D_MODEL, BF16)], [])
    y = _mm("conv_pw2", u3, P['w_pw2'], 'nn', S, D_MODEL, D_MODEL, extras=[P['b_pw2']],
            extra_specs=[_bias_spec(1024)], epi=lambda acc, b: (acc + b,))
    return y, dict(a=a, u0=u0, u1=u1, u3=u3)


def _conv_bwd(dy, dy_colsum, h, sv, P):
    S = h.shape[0]
    g_pw2 = _mm("conv_pw2_wg", sv['u3'], dy, 'tn', D_MODEL, D_MODEL, S)
    du3 = _mm("conv_pw2_dg", dy, P['w_pw2'], 'nt', S, D_MODEL, D_MODEL)

    def ln_bwd(d3, u, g, b):
        xc = u - _rowmean(u)
        rstd = lax.rsqrt(_rowmean(xc * xc) + NORM_EPS)
        xh = xc * rstd
        z = xh * g + b
        sg = _sigmoid(z)
        dz = d3 * (sg * (1.0 + z * (1.0 - sg)))
        dxh = dz * g
        du = rstd * (dxh - _rowmean(dxh) - xh * _rowmean(dxh * xh))
        return (du,), (_colsum(dz * xh), _colsum(dz), _colsum(du))

    (du1,), (d_lng, d_lnb, d_bdw) = _rowk("conv_ln_bwd", ln_bwd, [du3, sv['u1']], [P['ln_g'], P['ln_b']],
                                          [(D_MODEL, F32)], [D_MODEL] * 3)
    du0, d_wdw = _dwconv_bwd(du1, sv['u0'], P['w_dw'])

    def glu_bwd(d0, av):
        a1, sg = av[:, :D_MODEL], _sigmoid(av[:, D_MODEL:])
        da = jnp.concatenate([d0 * sg, d0 * a1 * sg * (1.0 - sg)], axis=1)
        return (da,), (_colsum(da),)

    (da,), (d_bpw1,) = _rowk("conv_glu_bwd", glu_bwd, [du0, sv['a']], [], [(2 * D_MODEL, BF16)], [2 * D_MODEL])
    g_pw1 = _mm("conv_pw1_wg", h, da, 'tn', D_MODEL, 2 * D_MODEL, S)
    dh = _mm("conv_pw1_dg", da, P['w_pw1'], 'nt', S, D_MODEL, 2 * D_MODEL)
    grads = dict(conv_w_pw1=g_pw1, conv_b_pw1=d_bpw1.reshape(-1), conv_w_dw=d_wdw[:CONV_WIDTH],
                 conv_b_dw=d_bdw.reshape(-1), conv_ln_g=d_lng.reshape(-1), conv_ln_b=d_lnb.reshape(-1),
                 conv_w_pw2=g_pw2, conv_b_pw2=dy_colsum.reshape(-1))
    return dh, grads


def _pool_group_specs(tm):
    return (pl.BlockSpec((tm, POOL_C), lambda i, j, k: (i, j)),
            pl.BlockSpec((None, POOL_C, POOL_C), lambda i, j, k: (j, 0, 0)))


def _pool_mixer_fwd(h, P):
    S = h.shape[0]
    p = _pool_fwd(h)
    a_spec, b_spec = _pool_group_specs(min(1024, S))
    y, z = _mm("pool_mm", p, P['w'], 'nn', S, D_MODEL, POOL_C, tn=POOL_C, a_spec=a_spec, b_spec=b_spec,
               extras=[P['b'], P['scale']], extra_specs=[_bias_spec(POOL_C)] * 2,
               epi=lambda acc, b, s: ((acc + b) * s, acc + b),
               outs=[jax.ShapeDtypeStruct((S, D_MODEL), F32)] * 2)
    return y, dict(p=p, z=z)


def _pool_mixer_bwd(dy, sv, P):
    S = dy.shape[0]

    def scale_bwd(d, z, s):
        dz = d * s
        return (dz,), (_colsum(d * z), _colsum(dz))

    (dz,), (d_scale, d_b) = _rowk("pool_scale_bwd", scale_bwd, [dy, sv['z']], [P['scale']],
                                  [(D_MODEL, BF16)], [D_MODEL] * 2)
    a_spec, b_spec = _pool_group_specs(min(1024, S))
    dp = _mm("pool_mm_dg", dz, P['w'], 'nt', S, D_MODEL, POOL_C, tn=POOL_C, a_spec=a_spec, b_spec=b_spec)
    tk = min(512, S)
    grp = pl.BlockSpec((tk, POOL_C), lambda i, j, k: (k, j))
    g_w = _mm("pool_mm_wg", sv['p'], dz, 'tn', POOL_C, D_MODEL, S, tn=POOL_C, tk=tk, a_spec=grp, b_spec=grp,
              outs=[jax.ShapeDtypeStruct((len(POOL_WINDOWS), POOL_C, POOL_C), F32)],
              out_specs=[pl.BlockSpec((None, POOL_C, POOL_C), lambda i, j, k: (j, 0, 0))])
    dh = _pool_bwd(dp)
    return dh, dict(pool_w=g_w, pool_b=d_b.reshape(-1), pool_scale=d_scale.reshape(-1))


def _adamw(w, g, m, v):
    m2 = ADAM_B1 * m + (1.0 - ADAM_B1) * g
    v2 = ADAM_B2 * v + (1.0 - ADAM_B2) * (g * g)
    m_hat = m2 / (1.0 - ADAM_B1 ** ADAM_STEP)
    v_hat = v2 / (1.0 - ADAM_B2 ** ADAM_STEP)
    delta = -ADAM_LR * (m_hat / (jnp.sqrt(v_hat) + ADAM_EPS) + ADAM_WD * w)
    return delta, m2, v2


def _row(v):
    return v.reshape(1, -1)


def kernel(x, c, positions, ada_w, ada_b, norm_g, mla_w_dq, mla_q_norm_g, mla_w_uq, mla_w_dkv, mla_kv_norm_g, mla_w_ukv, mla_w_o, conv_w_pw1, conv_b_pw1, conv_w_dw, conv_b_dw, conv_ln_g, conv_ln_b, conv_w_pw2, conv_b_pw2, pool_w, pool_b, pool_scale, ffn_w1, ffn_w2, loss_target, m_ada_w, m_ada_b, m_norm_g, m_mla_w_dq, m_mla_q_norm_g, m_mla_w_uq, m_mla_w_dkv, m_mla_kv_norm_g, m_mla_w_ukv, m_mla_w_o, m_conv_w_pw1, m_conv_b_pw1, m_conv_w_dw, m_conv_b_dw, m_conv_ln_g, m_conv_ln_b, m_conv_w_pw2, m_conv_b_pw2, m_pool_w, m_pool_b, m_pool_scale, m_ffn_w1, m_ffn_w2, v_ada_w, v_ada_b, v_norm_g, v_mla_w_dq, v_mla_q_norm_g, v_mla_w_uq, v_mla_w_dkv, v_mla_kv_norm_g, v_mla_w_ukv, v_mla_w_o, v_conv_w_pw1, v_conv_b_pw1, v_conv_w_dw, v_conv_b_dw, v_conv_ln_g, v_conv_ln_b, v_conv_w_pw2, v_conv_b_pw2, v_pool_w, v_pool_b, v_pool_scale, v_ffn_w1, v_ffn_w2):
    args = dict(locals())
    W = {n: args[n] for n in WEIGHTS}
    MOM = {n: args['m_' + n] for n in WEIGHTS}
    VAR = {n: args['v_' + n] for n in WEIGHTS}
    S = x.shape[1]
    xs = x.reshape(S, D_MODEL)
    tgt = loss_target.reshape(S, D_MODEL)
    mx, my, mc = lax.axis_index("x"), lax.axis_index("y"), lax.axis_index("c")
    chip = 2 * mx + my
    n_sh = ada_w.shape[2]

    c8 = _exchange("gather_c", [c.reshape(8, D_MODEL // 8)], 'xyc')[0].reshape(8, D_MODEL)
    c8 = jnp.pad(c8, ((0, ADA_ROWS - 8), (0, 0)))
    silu = lambda v: v * _sigmoid(v)
    mod_sh = []
    for l in range(DEPTH):
        b_l = lax.dynamic_slice(ada_b[l], (chip * n_sh,), (n_sh,)).reshape(1, n_sh)
        mod_sh.append(_mm(f"ada_fwd{l}", c8, ada_w, 'nn', ADA_ROWS, n_sh, D_MODEL, tn=n_sh // 2, tk=512, pro_a=silu,
                          b_spec=pl.BlockSpec((None, 512, n_sh // 2), lambda i, j, k, l=l: (l, k, j)),
                          extras=[b_l], extra_specs=[_bias_spec(n_sh // 2)], epi=lambda acc, b: (acc + b,))[:8])
    mod_sh = jnp.stack(mod_sh, axis=1).reshape(8, DEPTH * n_sh // 128, 128)
    mod = _exchange("scatter_mod", [mod_sh], 'xy', src_by='xyc')[0]
    mod = mod.reshape(4, DEPTH, n_sh).transpose(1, 0, 2).reshape(DEPTH, 6, 1, D_MODEL)

    names = [n for n in WEIGHTS if n not in ('ada_w', 'ada_b') and SHARD_AXIS[n] is not None]
    sent = [W[n].astype(BF16) if n in BIG or n in ('ffn_w1', 'ffn_w2') else W[n] for n in names]
    G = dict(zip(names, _exchange("gather_w", sent, 'xy')))
    full = {n: _unshard(G[n], SHARD_AXIS[n]) for n in names if n not in ('ffn_w1', 'ffn_w2')}
    gains = full['norm_g']
    mla_p = []
    for j in range(mla_w_dq.shape[0]):
        P = _mla_weights(full['mla_w_dq'][j], full['mla_w_dkv'][j], full['mla_w_uq'][j], full['mla_w_ukv'][j],
                         full['mla_w_o'][j])
        P.update(qg=_row(full['mla_q_norm_g'][j]), kg=_row(full['mla_kv_norm_g'][j]))
        mla_p.append(P)
    conv_p = dict(w_pw1=full['conv_w_pw1'][0], b_pw1=_row(conv_b_pw1[0]), w_dw=full['conv_w_dw'][0],
                  b_dw=_row(conv_b_dw[0]), ln_g=_row(conv_ln_g[0]), ln_b=_row(conv_ln_b[0]),
                  w_pw2=full['conv_w_pw2'][0], b_pw2=_row(conv_b_pw2[0]))
    pool_p = dict(w=full['pool_w'][0], b=_row(full['pool_b'][0]), scale=_row(full['pool_scale'][0]))
    rope = _rope_tables(positions.reshape(S, 1).astype(F32))

    w1_nn = lambda l: pl.BlockSpec((None, None, 1024, 1024), lambda i, j, k: (j, l, 0, 0))
    w1_nt = lambda l: pl.BlockSpec((None, None, 1024, 1024), lambda i, j, k: (k, l, 0, 0))
    w2_nn = lambda l: pl.BlockSpec((None, None, 1024, 1024), lambda i, j, k: (k, l, 0, 0))
    w2_nt = lambda l: pl.BlockSpec((None, None, 1024, 1024), lambda i, j, k: (j, l, 0, 0))
    sq_relu = lambda v: jnp.square(jnp.maximum(v, 0.0))

    def md(i, k):
        return mod[i, k]

    (h,), _ = _rowk("pre0", lambda xv, g, sc, sh: ((_pre_fwd(xv, g, sc, sh),), ()),
                    [xs], [_row(gains[0, 0]), md(0, 1), md(0, 0)], [(D_MODEL, BF16)], [])
    saved = []
    xin = xs
    loss_acc = dxf = None
    for i in range(DEPTH):
        kind, j = i % 3, i // 3
        if kind == 0:
            y, sv = _mla_fwd(j, h, mla_p[j], rope)
        elif kind == 1:
            y, sv = _conv_fwd(h, conv_p)
        else:
            y, sv = _pool_mixer_fwd(h, pool_p)

        def mid(xv, yv, gt, g1, g2, sc, sh):
            x1 = _post_fwd(xv, yv, gt, g1)
            return (x1, _pre_fwd(x1, g2, sc, sh)), ()

        (x1, h2), _ = _rowk(f"mid{i}", mid, [xin, y], [md(i, 2), _row(gains[i, 1]), _row(gains[i, 2]), md(i, 4), md(i, 3)],
                            [(D_MODEL, F32), (D_MODEL, BF16)], [])
        a = _mm(f"ffn1_{i}", h2, G['ffn_w1'], 'nn', S, D_FF, D_MODEL, b_spec=w1_nn(i))
        y2 = _mm(f"ffn2_{i}", a, G['ffn_w2'], 'nn', S, D_MODEL, D_FF, pro_a=sq_relu, b_spec=w2_nn(i))
        saved.append(dict(x0=xin, h=h, y=y, x1=x1, h2=h2, a=a, y2=y2, mix=sv))
        if i + 1 < DEPTH:
            def nxt(xv, yv, gt, g3, g0, sc, sh):
                x2 = _post_fwd(xv, yv, gt, g3)
                return (x2, _pre_fwd(x2, g0, sc, sh)), ()

            hdt = F32 if (i + 1) % 3 == 2 else BF16
            (xin, h), _ = _rowk(f"next{i}", nxt, [x1, y2],
                                [md(i, 5), _row(gains[i, 3]), _row(gains[i + 1, 0]), md(i + 1, 1), md(i + 1, 0)],
                                [(D_MODEL, F32), (D_MODEL, hdt)], [])
        else:
            def head(xv, yv, tv, gt, g3):
                err = _post_fwd(xv, yv, gt, g3) - tv
                per_row = jnp.sum(err * err, axis=1, keepdims=True) * (0.5 / D_MODEL)
                return (err * (1.0 / D_MODEL),), (jnp.broadcast_to(jnp.sum(per_row, axis=0, keepdims=True), (1, 128)),)

            (dxf,), (loss_acc,) = _rowk("loss_head", head, [x1, y2, tgt], [md(i, 5), _row(gains[i, 3])],
                                        [(D_MODEL, F32)], [128])
    loss = lax.psum(loss_acc[0, 0], ("x", "y", "c"))

    grads = {}
    ffn_g1, ffn_g2 = [None] * DEPTH, [None] * DEPTH
    d_mod = [None] * DEPTH
    d_gain = [None] * DEPTH
    dx = dxf
    for i in reversed(range(DEPTH)):
        kind, j = i % 3, i // 3
        sv = saved[i]
        def post2_bwd(d, yv, gt, g):
            dyv, d_gt, d_g = _post_bwd(d, yv, gt, g)
            return (dyv,), (d_gt, d_g)

        (dy2,), (d_gtf, d_g3) = _rowk(f"post2_bwd{i}", post2_bwd, [dx, sv['y2']], [md(i, 5), _row(gains[i, 3])],
                                      [(D_MODEL, BF16)], [D_MODEL] * 2)
        da = _mm(f"ffn2_dg{i}", dy2, G['ffn_w2'], 'nt', S, D_FF, D_MODEL, b_spec=w2_nt(i), extras=[sv['a']],
                 extra_specs=[pl.BlockSpec((min(1024, S), 1024), lambda i_, j_, k_: (i_, j_))],
                 epi=lambda acc, av: (acc * (2.0 * jnp.maximum(av, 0.0)),),
                 outs=[jax.ShapeDtypeStruct((S, D_FF), BF16)])
        ffn_g2[i] = _mm(f"ffn2_wg{i}", sv['a'], dy2, 'tn', D_FF, D_MODEL, S, pro_a=sq_relu,
                        outs=[jax.ShapeDtypeStruct((4, 1024, D_MODEL), BF16)],
                        out_specs=[pl.BlockSpec((None, 1024, 1024), lambda i_, j_, k_: (i_, 0, j_))])
        ffn_g1[i] = _mm(f"ffn1_wg{i}", sv['h2'], da, 'tn', D_MODEL, D_FF, S,
                        outs=[jax.ShapeDtypeStruct((4, D_MODEL, 1024), BF16)],
                        out_specs=[pl.BlockSpec((None, 1024, 1024), lambda i_, j_, k_: (j_, i_, 0))])
        dh2 = _mm(f"ffn1_dg{i}", da, G['ffn_w1'], 'nt', S, D_MODEL, D_FF, b_spec=w1_nt(i))

        def mid_bwd(d2, dh2v, x1v, yv, g2, scf, gtm, g1):
            dpre, d_sh, d_sc, d_g2 = _pre_bwd(dh2v, x1v, g2, scf)
            d1 = d2 + dpre
            dyv, d_gt, d_g1 = _post_bwd(d1, yv, gtm, g1)
            return (d1, dyv), (d_sh, d_sc, d_g2, d_gt, d_g1, _colsum(dyv))

        ydt = F32 if kind == 2 else BF16
        (dx1, dy), (d_shf, d_scf, d_g2, d_gtm, d_g1, dy_cs) = _rowk(
            f"mid_bwd{i}", mid_bwd, [dx, dh2, sv['x1'], sv['y']],
            [_row(gains[i, 2]), md(i, 4), md(i, 2), _row(gains[i, 1])],
            [(D_MODEL, F32), (D_MODEL, ydt)], [D_MODEL] * 6)
        if kind == 0:
            dh, gm = _mla_bwd(j, dy, sv['h'], sv['mix'], mla_p[j], rope)
            for n, g in gm.items():
                grads.setdefault(n, [None] * mla_w_dq.shape[0])[j] = g
        elif kind == 1:
            dh, gm = _conv_bwd(dy, dy_cs, sv['h'], sv['mix'], conv_p)
            for n, g in gm.items():
                grads[n] = [g]
        else:
            dh, gm = _pool_mixer_bwd(dy, sv['mix'], pool_p)
            for n, g in gm.items():
                grads[n] = [g]

        def pre_bwd(d1, dhv, x0v, g0, scm):
            dpre, d_sh, d_sc, d_g0 = _pre_bwd(dhv, x0v, g0, scm)
            return (d1 + dpre,), (d_sh, d_sc, d_g0)

        (dx,), (d_shm, d_scm, d_g0) = _rowk(f"pre_bwd{i}", pre_bwd, [dx1, dh, sv['x0']],
                                            [_row(gains[i, 0]), md(i, 1)], [(D_MODEL, F32)], [D_MODEL] * 3)
        d_mod[i] = jnp.concatenate([d_shm, d_scm, d_gtm, d_shf, d_scf, d_gtf], axis=1).reshape(-1)
        d_gain[i] = jnp.concatenate([d_g0, d_g1, d_g2, d_g3], axis=0)
    grad_x = dx.reshape(x.shape)
    grads = {n: jnp.stack(g) for n, g in grads.items()}
    grads['norm_g'] = jnp.stack(d_gain)
    grads['ada_b'] = jnp.stack(d_mod)

    pack = jnp.concatenate([grads[n].reshape(-1) for n in SMALL])
    n_pack = pack.shape[0]
    rows = -(-n_pack // 1024) * 8
    pack = jnp.pad(pack, (0, rows * 128 - n_pack)).reshape(rows, 128)
    pack8 = _exchange("gather_small", [pack], 'xyc')[0]
    (tot,) = _ew("sum_small", lambda *v: (functools.reduce(lambda p, q: p + q, v),), [(pack8, s) for s in range(8)],
                 [F32], (rows, 128))
    tot = tot.reshape(-1)
    d_mod_all = pack8.reshape(8, -1)[:, :DEPTH * 6 * D_MODEL].reshape(8, DEPTH, 6 * D_MODEL)
    final = {}
    off = 0
    for n in SMALL:
        ax = SHARD_AXIS[n]
        shape = tuple(d * 4 if k == ax else d for k, d in enumerate(W[n].shape))
        size = grads[n].size
        g = tot[off:off + size].reshape(shape)
        off += size
        if ax is not None:
            g = lax.dynamic_index_in_dim(_to_shards(g, ax), chip, 0, keepdims=False)
        final[n] = g

    g_ada = []
    for l in range(DEPTH):
        dm_l = jnp.pad(lax.dynamic_slice(d_mod_all[:, l], (0, chip * n_sh), (8, n_sh)), ((0, ADA_ROWS - 8), (0, 0)))
        g_ada.append(_mm(f"ada_wg{l}", c8, dm_l, 'tn', D_MODEL, n_sh, ADA_ROWS, tn=n_sh // 2, pro_a=silu))
    final['ada_w'] = jnp.stack(g_ada)

    red_names, red = [], []
    for n in BIG:
        red_names.append(n)
        red.append(_to_shards(grads[n], SHARD_AXIS[n]).astype(BF16))
    for l in range(DEPTH):
        red_names += [f"ffn_w1:{l}", f"ffn_w2:{l}"]
        red += [ffn_g1[l], ffn_g2[l]]
    landed = _exchange("scatter_grads", red, 'xy', src_by='xy')
    partial = []
    for n, arr in zip(red_names, landed):
        (p_,) = _ew("sum_" + n.replace(':', '_'), lambda a0, a1, a2, a3: (((a0.astype(F32) + a1.astype(F32)) + a2.astype(F32)) + a3.astype(F32),),
                    [(arr, s) for s in range(4)], [F32], arr.shape[1:])
        partial.append(p_)
    pairs = dict(zip(red_names, _exchange("swap_grads", partial, 'c')))

    out_g, out_d, out_m, out_v = {}, {}, {}, {}
    for n in WEIGHTS:
        shape = W[n].shape
        if n in ('ffn_w1', 'ffn_w2'):
            res = [_ew(f"adamw_{n}{l}", lambda w, g0, g1, m, v: (lambda g: (g,) + _adamw(w, g, m, v))(g0 + g1),
                       [W[n][l], (pairs[f"{n}:{l}"], 0), (pairs[f"{n}:{l}"], 1), MOM[n][l], VAR[n][l]],
                       [F32] * 4, shape[1:]) for l in range(DEPTH)]
            out_g[n], out_d[n], out_m[n], out_v[n] = [jnp.stack([r[k] for r in res]) for k in range(4)]
        elif n in BIG:
            out_g[n], out_d[n], out_m[n], out_v[n] = _ew(
                f"adamw_{n}", lambda w, g0, g1, m, v: (lambda g: (g,) + _adamw(w, g, m, v))(g0 + g1),
                [W[n], (pairs[n], 0), (pairs[n], 1), MOM[n], VAR[n]], [F32] * 4, shape)
        else:
            out_g[n] = final[n].reshape(shape)
            out_d[n], out_m[n], out_v[n] = _ew(f"adamw_{n}", lambda w, g, m, v: _adamw(w, g, m, v),
                                               [W[n], out_g[n], MOM[n], VAR[n]], [F32] * 3, shape)
    return (loss, grad_x, *[out_g[n] for n in WEIGHTS], *[out_d[n] for n in WEIGHTS],
            *[out_m[n] for n in WEIGHTS], *[out_v[n] for n in WEIGHTS])
```

```python
import functools
import math

import jax
import jax.numpy as jnp
from jax import lax
from jax.experimental import pallas as pl
from jax.experimental.pallas import tpu as pltpu

F32 = jnp.float32
BF16 = jnp.bfloat16

D_MODEL = 1024
DEPTH = 4
N_HEADS = 16
QK_NOPE = 64
QK_ROPE = 32
V_HEAD = 64
Q_LORA = 384
KV_LORA = 256
HEAD_PAD = 128
QW = N_HEADS * HEAD_PAD
KVW = 2 * QW
DKV = KV_LORA + QK_ROPE
DQKV = Q_LORA + DKV
D_FF = 4096
CONV_WIDTH = 31
POOL_WINDOWS = (2, 4, 8, 16)
CHUNK_SHIFT = 6
ROPE_THETA = 10000.0
NORM_EPS = 1e-6
NEG_INF = -1e30
ATT_SCALE = 1.0 / math.sqrt(QK_NOPE + QK_ROPE)
BQ = 256
HB = 4
LOG2E = 1.4426950408889634
SCALE_LOG2E = ATT_SCALE * LOG2E
PAD_ROWS = 32
ADA_ROWS = 128
VMEM_LIMIT = 56 * 1024 * 1024

ADAM_LR = 0.001
ADAM_B1 = 0.9
ADAM_B2 = 0.999
ADAM_EPS = 1e-08
ADAM_WD = 0.01
ADAM_STEP = 10

WEIGHTS = ['ada_w', 'ada_b', 'norm_g', 'mla_w_dq', 'mla_q_norm_g', 'mla_w_uq', 'mla_w_dkv', 'mla_kv_norm_g',
           'mla_w_ukv', 'mla_w_o', 'conv_w_pw1', 'conv_b_pw1', 'conv_w_dw', 'conv_b_dw', 'conv_ln_g', 'conv_ln_b',
           'conv_w_pw2', 'conv_b_pw2', 'pool_w', 'pool_b', 'pool_scale', 'ffn_w1', 'ffn_w2']
SHARD_AXIS = {'ada_w': 2, 'ada_b': None, 'norm_g': 2, 'mla_w_dq': 1, 'mla_q_norm_g': 1, 'mla_w_uq': 2,
              'mla_w_dkv': 1, 'mla_kv_norm_g': 1, 'mla_w_ukv': 2, 'mla_w_o': 1, 'conv_w_pw1': 2,
              'conv_b_pw1': None, 'conv_w_dw': 2, 'conv_b_dw': None, 'conv_ln_g': None, 'conv_ln_b': None,
              'conv_w_pw2': 1, 'conv_b_pw2': None, 'pool_w': 2, 'pool_b': 2, 'pool_scale': 1,
              'ffn_w1': 2, 'ffn_w2': 1}
BIG = ['mla_w_dq', 'mla_w_uq', 'mla_w_dkv', 'mla_w_ukv', 'mla_w_o', 'conv_w_pw1', 'conv_w_pw2', 'pool_w']
SMALL = ['ada_b', 'norm_g', 'mla_q_norm_g', 'mla_kv_norm_g', 'conv_b_pw1', 'conv_w_dw', 'conv_b_dw',
         'conv_ln_g', 'conv_ln_b', 'conv_b_pw2', 'pool_b', 'pool_scale']


def _cparams(*sem):
    return pltpu.CompilerParams(dimension_semantics=sem, vmem_limit_bytes=VMEM_LIMIT)


def _colsum(v):
    return jnp.sum(v, axis=0, keepdims=True)


def _rowmean(v):
    return jnp.mean(v, axis=-1, keepdims=True)


def _sigmoid(v):
    return 1.0 / (1.0 + jnp.exp(-v))


def _rowk(name, fn, rows, bcast, out_row, out_acc, tm=256):
    S = rows[0].shape[0]
    tm = min(tm, S)
    assert S % tm == 0
    nin, no, na = len(rows) + len(bcast), len(out_row), len(out_acc)

    def body(*refs):
        vals = [r[...] for r in refs[:nin]]
        outs = refs[nin:nin + no]
        accs = refs[nin + no:]
        ro, ao = fn(*vals)
        for r, v in zip(outs, ro):
            r[...] = v.astype(r.dtype)
        if na:
            @pl.when(pl.program_id(0) == 0)
            def _():
                for r in accs:
                    r[...] = jnp.zeros(r.shape, r.dtype)
            for r, v in zip(accs, ao):
                r[...] += v

    in_specs = [pl.BlockSpec((tm, a.shape[1]), lambda i: (i, 0)) for a in rows]
    in_specs += [pl.BlockSpec(b.shape, lambda i, n=b.ndim: (0,) * n) for b in bcast]
    out_shape = [jax.ShapeDtypeStruct((S, w), dt) for w, dt in out_row]
    out_shape += [jax.ShapeDtypeStruct((1, w), F32) for w in out_acc]
    out_specs = [pl.BlockSpec((tm, w), lambda i: (i, 0)) for w, _ in out_row]
    out_specs += [pl.BlockSpec((1, w), lambda i: (0, 0)) for w in out_acc]
    res = pl.pallas_call(body, name=name, grid=(S // tm,), in_specs=in_specs, out_specs=out_specs,
                         out_shape=out_shape, compiler_params=_cparams("arbitrary"))(*rows, *bcast)
    return list(res[:no]), list(res[no:])


_DIMS = {'nn': ((1,), (0,)), 'nt': ((1,), (1,)), 'tn': ((0,), (0,))}


def _mm(name, a, b, mode, M, N, K, *, tm=1024, tn=1024, tk=1024, a_spec=None, b_spec=None, pro_a=None,
        extras=(), extra_specs=(), epi=None, outs=None, out_specs=None):
    tm, tn, tk = (t if d % t == 0 else d for t, d in ((min(tm, M), M), (min(tn, N), N), (min(tk, K), K)))
    nk = K // tk
    if a_spec is None:
        a_spec = (pl.BlockSpec((tk, tm), lambda i, j, k: (k, i)) if mode == 'tn'
                  else pl.BlockSpec((tm, tk), lambda i, j, k: (i, k)))
    if b_spec is None:
        b_spec = (pl.BlockSpec((tn, tk), lambda i, j, k: (j, k)) if mode == 'nt'
                  else pl.BlockSpec((tk, tn), lambda i, j, k: (k, j)))
    if outs is None:
        outs = [jax.ShapeDtypeStruct((M, N), F32)]
    if out_specs is None:
        out_specs = [pl.BlockSpec((tm, tn), lambda i, j, k: (i, j)) for _ in outs]
    ne, no = len(extras), len(outs)
    dims = (_DIMS[mode], ((), ()))

    def body(a_ref, b_ref, *rest):
        ex, out_refs = rest[:ne], rest[ne:ne + no]
        av = a_ref[...]
        if pro_a is not None:
            av = pro_a(av)
        part = lax.dot_general(av.astype(BF16), b_ref[...].astype(BF16), dims, preferred_element_type=F32)

        def finish(acc):
            vals = (acc,) if epi is None else epi(acc, *[e[...] for e in ex])
            for r, v in zip(out_refs, vals):
                r[...] = v.astype(r.dtype)

        if nk == 1:
            finish(part)
            return
        acc_ref = rest[ne + no]
        k = pl.program_id(2)

        @pl.when(k == 0)
        def _():
            acc_ref[...] = part

        @pl.when(k > 0)
        def _():
            acc_ref[...] += part

        @pl.when(k == nk - 1)
        def _():
            finish(acc_ref[...])

    res = pl.pallas_call(
        body, name=name, grid=(M // tm, N // tn, nk),
        in_specs=[a_spec, b_spec, *extra_specs], out_specs=list(out_specs), out_shape=list(outs),
        scratch_shapes=[pltpu.VMEM((tm, tn), F32)] if nk > 1 else [],
        compiler_params=_cparams("parallel", "parallel", "arbitrary"))(a, b, *extras)
    return res[0] if no == 1 else list(res)


def _row_tile(R, C, itemsize=4, budget=1 << 20):
    if R * C * itemsize <= budget or R % 8:
        return R
    t = 8
    while R % (t * 2) == 0 and t * 2 * C * itemsize <= budget:
        t *= 2
    return t


def _ew(name, fn, ins, out_dtypes, shape):
    C = shape[-1]
    R = 1
    for s in shape[:-1]:
        R *= s
    tr = _row_tile(R, C)
    ops, specs = [], []
    for it in ins:
        if isinstance(it, tuple):
            arr, idx = it
            ops.append(arr.reshape(arr.shape[0], R, C))
            specs.append(pl.BlockSpec((None, tr, C), lambda i, n=idx: (n, i, 0)))
        else:
            ops.append(it.reshape(R, C))
            specs.append(pl.BlockSpec((tr, C), lambda i: (i, 0)))
    nin = len(ops)

    def body(*refs):
        vals = fn(*[r[...] for r in refs[:nin]])
        for r, v in zip(refs[nin:], vals):
            r[...] = v.astype(r.dtype)

    res = pl.pallas_call(
        body, name=name, grid=(R // tr,), in_specs=specs,
        out_specs=[pl.BlockSpec((tr, C), lambda i: (i, 0)) for _ in out_dtypes],
        out_shape=[jax.ShapeDtypeStruct((R, C), dt) for dt in out_dtypes],
        compiler_params=_cparams("parallel"))(*ops)
    return [r.reshape(shape) for r in res]


_FLIPS = {'xyc': [(fx, fy, fc) for fx in (0, 1) for fy in (0, 1) for fc in (0, 1)][1:],
          'xy': [(1, 0, 0), (0, 1, 0), (1, 1, 0)],
          'c': [(0, 0, 1)]}
_NSLOT = {'xyc': 8, 'xy': 4, 'c': 2}


def _slot(kind, cx, cy, cc):
    return {'xyc': 4 * cx + 2 * cy + cc, 'xy': 2 * cx + cy, 'c': cc}[kind]


def _exchange(name, arrays, group, src_by=None, in_space=pl.ANY, out_space=pl.ANY):
    flips, nsl, n = _FLIPS[group], _NSLOT[group], len(arrays)
    nf = len(flips)

    def body(*refs):
        ins, outs = refs[:n], refs[n:2 * n]
        send_sems, recv_sems, loc_sems = refs[2 * n:]
        mx, my, mc = lax.axis_index("x"), lax.axis_index("y"), lax.axis_index("c")
        me = _slot(group, mx, my, mc)

        def payload(a, cx, cy, cc):
            return ins[a] if src_by is None else ins[a].at[_slot(src_by, cx, cy, cc)]

        local = [pltpu.make_async_copy(payload(a, mx, my, mc), outs[a].at[me], loc_sems.at[a])
                 for a in range(n)]
        for cp in local:
            cp.start()
        sends, recvs = [], []
        for a in range(n):
            for f, (fx, fy, fc) in enumerate(flips):
                px = 1 - mx if fx else mx
                py = 1 - my if fy else my
                pc = 1 - mc if fc else mc
                src = payload(a, px, py, pc)
                sends.append(pltpu.make_async_remote_copy(
                    src_ref=src, dst_ref=outs[a].at[me], send_sem=send_sems.at[a, f],
                    recv_sem=recv_sems.at[a, f], device_id=(px, py, pc),
                    device_id_type=pl.DeviceIdType.MESH))
                recvs.append(pltpu.make_async_remote_copy(
                    src_ref=src, dst_ref=outs[a].at[_slot(group, px, py, pc)], send_sem=send_sems.at[a, f],
                    recv_sem=recv_sems.at[a, f], device_id=(px, py, pc),
                    device_id_type=pl.DeviceIdType.MESH))
        for cp in sends:
            cp.start()
        for cp in recvs:
            cp.wait_recv()
        for cp in sends:
            cp.wait_send()
        for cp in local:
            cp.wait()

    out_shape = [jax.ShapeDtypeStruct((nsl,) + (a.shape if src_by is None else a.shape[1:]), a.dtype)
                 for a in arrays]
    res = pl.pallas_call(
        body, name=name, in_specs=[pl.BlockSpec(memory_space=in_space)] * n,
        out_specs=[pl.BlockSpec(memory_space=out_space)] * n, out_shape=out_shape,
        scratch_shapes=[pltpu.SemaphoreType.DMA((n, nf)), pltpu.SemaphoreType.DMA((n, nf)),
                        pltpu.SemaphoreType.DMA((n,))],
        compiler_params=pltpu.CompilerParams(has_side_effects=True))(*arrays)
    return list(res)


def _unshard(g, axis):
    t = jnp.moveaxis(g, 0, axis)
    s = t.shape
    return t.reshape(s[:axis] + (s[axis] * s[axis + 1],) + s[axis + 2:])


def _to_shards(w, axis):
    s = w.shape
    t = w.reshape(s[:axis] + (4, s[axis] // 4) + s[axis + 1:])
    return jnp.moveaxis(t, axis, 0)


def _pre_fwd(x, g, sc, sh):
    r = lax.rsqrt(_rowmean(x * x) + NORM_EPS)
    return (x * r) * g * (1.0 + sc) + sh


def _pre_bwd(dh, x, g, sc):
    r = lax.rsqrt(_rowmean(x * x) + NORM_EPS)
    xn = x * r
    dxn = dh * (g * (1.0 + sc))
    dx = r * (dxn - xn * _rowmean(dxn * xn))
    t = dh * xn
    return dx, _colsum(dh), _colsum(t * g), _colsum(t * (1.0 + sc))


def _post_fwd(x, y, gt, g):
    r = lax.rsqrt(_rowmean(y * y) + NORM_EPS)
    return x + gt * ((y * r) * g)


def _post_bwd(dxo, y, gt, g):
    r = lax.rsqrt(_rowmean(y * y) + NORM_EPS)
    yn = y * r
    t = dxo * yn
    dyn = dxo * (gt * g)
    dy = r * (dyn - yn * _rowmean(dyn * yn))
    return dy, _colsum(t * g), _colsum(t * gt)


def _gain_bwd(dy, x, g):
    r = lax.rsqrt(_rowmean(x * x) + NORM_EPS)
    xn = x * r
    dxn = dy * g
    return r * (dxn - xn * _rowmean(dxn * xn)), _colsum(dy * xn)


def _rope(x, cos, sa, sb):
    return x * cos + pltpu.roll(x, HEAD_PAD - 16, 1) * sa + pltpu.roll(x, 16, 1) * sb


def _rope_t(d, cos, sa, sb):
    return d * cos + pltpu.roll(d * sa, 16, 1) + pltpu.roll(d * sb, HEAD_PAD - 16, 1)


def _rope_tables(pos_f):
    S = pos_f.shape[0]
    inv = ROPE_THETA ** (-jnp.arange(0, QK_ROPE, 2, dtype=F32) / QK_ROPE)
    inv_ext = jnp.concatenate([jnp.zeros((QK_NOPE,), F32), inv, inv,
                               jnp.zeros((HEAD_PAD - QK_NOPE - QK_ROPE,), F32)]).reshape(1, HEAD_PAD)

    def fn(p, iv):
        ang = p * iv
        lane = lax.broadcasted_iota(jnp.int32, ang.shape, 1)
        s = jnp.sin(ang)
        first = (lane >= QK_NOPE) & (lane < QK_NOPE + QK_ROPE // 2)
        second = (lane >= QK_NOPE + QK_ROPE // 2) & (lane < QK_NOPE + QK_ROPE)
        return (jnp.cos(ang), jnp.where(first, -s, 0.0), jnp.where(second, s, 0.0)), ()

    (cos, sa, sb), _ = _rowk("rope_tables", fn, [pos_f], [inv_ext], [(HEAD_PAD, F32)] * 3, [])
    return cos, sa, sb


def _diag_mask(transposed):
    r = lax.broadcasted_iota(jnp.int32, (BQ, BQ), 0) >> CHUNK_SHIFT
    c = lax.broadcasted_iota(jnp.int32, (BQ, BQ), 1) >> CHUNK_SHIFT
    return (r <= c) if transposed else (c <= r)


_NT = (((1,), (1,)), ((), ()))
_NN = (((1,), (0,)), ((), ()))


def _attn_fwd(qf, kvf):
    S = qf.shape[0]
    nq = S // BQ

    def body(q_ref, kv_ref, o_ref, lse_ref):
        qi = pl.program_id(1)
        qs = [q_ref[:, hh * HEAD_PAD:(hh + 1) * HEAD_PAD] for hh in range(HB)]

        def step(j, carry, diag):
            off = pl.multiple_of(j * BQ, BQ)
            sts = [lax.dot_general(kv_ref[pl.ds(off, BQ), pl.ds(2 * hh * HEAD_PAD, HEAD_PAD)], qs[hh], _NT,
                                   preferred_element_type=F32) for hh in range(HB)]
            mid = []
            for hh in range(HB):
                m, l, acc = carry[hh]
                st = jnp.where(_diag_mask(True), sts[hh], NEG_INF) if diag else sts[hh]
                m2 = jnp.maximum(m, jnp.max(st, axis=0, keepdims=True))
                al = jnp.exp2((m - m2) * SCALE_LOG2E)
                pt = jnp.exp2((st - m2) * SCALE_LOG2E)
                mid.append((m2, l * al + jnp.sum(pt, axis=0, keepdims=True), acc * al, pt.astype(BF16)))
            out = []
            for hh in range(HB):
                m2, l2, acc_s, ptb = mid[hh]
                v = kv_ref[pl.ds(off, BQ), pl.ds((2 * hh + 1) * HEAD_PAD, HEAD_PAD)]
                out.append((m2, l2, acc_s + lax.dot_general(v, ptb, _TN, preferred_element_type=F32)))
            return tuple(out)

        init = tuple((jnp.full((1, BQ), NEG_INF, F32), jnp.zeros((1, BQ), F32), jnp.zeros((HEAD_PAD, BQ), F32))
                     for _ in range(HB))
        carry = lax.fori_loop(0, qi, lambda j, c: step(j, c, False), init)
        carry = step(qi, carry, True)
        for hh in range(HB):
            m, l, acc = carry[hh]
            o_ref[:, hh * HEAD_PAD:(hh + 1) * HEAD_PAD] = (acc / l).T
            lse_ref[hh] = m * SCALE_LOG2E + jnp.log(l) * LOG2E

    return pl.pallas_call(
        body, name="attn_fwd", grid=(N_HEADS // HB, nq),
        in_specs=[pl.BlockSpec((BQ, HB * HEAD_PAD), lambda g, i: (i, g)),
                  pl.BlockSpec((S, 2 * HB * HEAD_PAD), lambda g, i: (0, g))],
        out_specs=[pl.BlockSpec((BQ, HB * HEAD_PAD), lambda g, i: (i, g)),
                   pl.BlockSpec((HB, None, 1, BQ), lambda g, i: (g, i, 0, 0))],
        out_shape=[jax.ShapeDtypeStruct((S, QW), F32), jax.ShapeDtypeStruct((N_HEADS, nq, 1, BQ), F32)],
        compiler_params=_cparams("parallel", "arbitrary"))(qf, kvf)


def _attn_delta(dob, o):
    S = o.shape[0]

    def body(do_ref, o_ref, dd_ref):
        dd_ref[...] = jnp.sum(do_ref[...].astype(F32) * o_ref[...], axis=1, keepdims=True)

    blk = pl.BlockSpec((S, HEAD_PAD), lambda h: (0, h))
    return pl.pallas_call(
        body, name="attn_delta", grid=(N_HEADS,), in_specs=[blk, blk],
        out_specs=pl.BlockSpec((None, S, 1), lambda h: (h, 0, 0)),
        out_shape=jax.ShapeDtypeStruct((N_HEADS, S, 1), F32),
        compiler_params=_cparams("parallel"))(dob, o)


_TN = (((0,), (0,)), ((), ()))


def _attn_bwd(qf, kvf, dob, lse_row, dd_row, cos, sa, sb):
    S = qf.shape[0]
    nq = S // BQ

    def body(kv_ref, q_ref, do_ref, lse_ref, dd_ref, cos_ref, sa_ref, sb_ref, dq_ref, dkv_ref):
        kj = pl.program_id(1)

        @pl.when(kj == 0)
        def _():
            dq_ref[...] = jnp.zeros(dq_ref.shape, F32)

        ks = [kv_ref[:, 2 * hh * HEAD_PAD:(2 * hh + 1) * HEAD_PAD] for hh in range(HB)]
        vs = [kv_ref[:, (2 * hh + 1) * HEAD_PAD:(2 * hh + 2) * HEAD_PAD] for hh in range(HB)]

        def step(i, carry, diag):
            off = pl.multiple_of(i * BQ, BQ)
            cols = [pl.ds(hh * HEAD_PAD, HEAD_PAD) for hh in range(HB)]
            q = [q_ref[pl.ds(off, BQ), cols[hh]] for hh in range(HB)]
            do = [do_ref[pl.ds(off, BQ), cols[hh]] for hh in range(HB)]
            sts = [lax.dot_general(ks[hh], q[hh], _NT, preferred_element_type=F32) for hh in range(HB)]
            dpts = [lax.dot_general(vs[hh], do[hh], _NT, preferred_element_type=F32) for hh in range(HB)]
            mid = []
            for hh in range(HB):
                st = jnp.where(_diag_mask(True), sts[hh], NEG_INF) if diag else sts[hh]
                pt = jnp.exp2(st * SCALE_LOG2E - lse_ref[hh, i])
                mid.append((pt.astype(BF16), (pt * (dpts[hh] - dd_ref[hh, i])).astype(BF16)))
            out = []
            for hh in range(HB):
                dk, dv = carry[hh]
                ptb, dsb = mid[hh]
                dv2 = dv + lax.dot_general(ptb, do[hh], _NN, preferred_element_type=F32)
                dk2 = dk + lax.dot_general(dsb, q[hh], _NN, preferred_element_type=F32)
                dq_ref[pl.ds(off, BQ), cols[hh]] += lax.dot_general(dsb, ks[hh], _TN, preferred_element_type=F32)
                out.append((dk2, dv2))
            return tuple(out)

        zero = jnp.zeros((BQ, HEAD_PAD), F32)
        carry = step(kj, tuple((zero, zero) for _ in range(HB)), True)
        carry = lax.fori_loop(kj + 1, nq, lambda i, c: step(i, c, False), carry)
        for hh in range(HB):
            dk, dv = carry[hh]
            dk = _rope_t(dk * ATT_SCALE, cos_ref[...], sa_ref[...], sb_ref[...])
            dkv_ref[:, 2 * hh * HEAD_PAD:(2 * hh + 1) * HEAD_PAD] = dk.astype(BF16)
            dkv_ref[:, (2 * hh + 1) * HEAD_PAD:(2 * hh + 2) * HEAD_PAD] = dv.astype(BF16)

    tab = pl.BlockSpec((BQ, HEAD_PAD), lambda g, j: (j, 0))
    row = pl.BlockSpec((HB, nq, 1, BQ), lambda g, j: (g, 0, 0, 0))
    seq = pl.BlockSpec((S, HB * HEAD_PAD), lambda g, j: (0, g))
    kvb = pl.BlockSpec((BQ, 2 * HB * HEAD_PAD), lambda g, j: (j, g))
    return pl.pallas_call(
        body, name="attn_bwd", grid=(N_HEADS // HB, nq),
        in_specs=[kvb, seq, seq, row, row, tab, tab, tab],
        out_specs=[seq, kvb],
        out_shape=[jax.ShapeDtypeStruct((S, QW), F32), jax.ShapeDtypeStruct((S, KVW), BF16)],
        compiler_params=_cparams("parallel", "arbitrary"))(kvf, qf, dob, lse_row, dd_row, cos, sa, sb)


DC = 128
TR = 256


def _dwconv_fwd(u, w, b):
    S, Dm = u.shape
    tr = min(TR, S)

    def body(u_ref, w_ref, b_ref, o_ref, pad_ref):
        pad_ref[pl.ds(0, PAD_ROWS), :] = jnp.zeros((PAD_ROWS, DC), F32)
        pad_ref[pl.ds(PAD_ROWS, S), :] = u_ref[...]
        wv = w_ref[...]
        for r in range(S // tr):
            acc = jnp.broadcast_to(b_ref[...], (tr, DC))
            for j in range(CONV_WIDTH):
                acc = acc + wv[j:j + 1, :] * pad_ref[pl.ds(r * tr + PAD_ROWS - (CONV_WIDTH - 1) + j, tr), :]
            o_ref[pl.ds(r * tr, tr), :] = acc

    return pl.pallas_call(
        body, name="dwconv_fwd", grid=(Dm // DC,),
        in_specs=[pl.BlockSpec((S, DC), lambda c: (0, c)), pl.BlockSpec((CONV_WIDTH, DC), lambda c: (0, c)),
                  pl.BlockSpec((1, DC), lambda c: (0, c))],
        out_specs=pl.BlockSpec((S, DC), lambda c: (0, c)),
        out_shape=jax.ShapeDtypeStruct((S, Dm), F32),
        scratch_shapes=[pltpu.VMEM((S + PAD_ROWS, DC), F32)],
        compiler_params=_cparams("parallel"))(u, w, b)


def _dwconv_bwd(d, u, w):
    S, Dm = u.shape
    tr = min(TR, S)

    def body(d_ref, u_ref, w_ref, du_ref, dw_ref, padd_ref, padu_ref):
        padd_ref[pl.ds(0, S), :] = d_ref[...]
        padd_ref[pl.ds(S, PAD_ROWS), :] = jnp.zeros((PAD_ROWS, DC), F32)
        padu_ref[pl.ds(0, PAD_ROWS), :] = jnp.zeros((PAD_ROWS, DC), F32)
        padu_ref[pl.ds(PAD_ROWS, S), :] = u_ref[...]
        wv = w_ref[...]
        dws = [jnp.zeros((1, DC), F32) for _ in range(CONV_WIDTH)]
        for r in range(S // tr):
            acc = jnp.zeros((tr, DC), F32)
            for j in range(CONV_WIDTH):
                acc = acc + wv[j:j + 1, :] * padd_ref[pl.ds(r * tr + (CONV_WIDTH - 1) - j, tr), :]
            du_ref[pl.ds(r * tr, tr), :] = acc
            dt = d_ref[pl.ds(r * tr, tr), :]
            for j in range(CONV_WIDTH):
                ut = padu_ref[pl.ds(r * tr + PAD_ROWS - (CONV_WIDTH - 1) + j, tr), :]
                dws[j] = dws[j] + _colsum(dt * ut)
        for j in range(CONV_WIDTH):
            dw_ref[pl.ds(j, 1), :] = dws[j]
        dw_ref[pl.ds(CONV_WIDTH, 1), :] = jnp.zeros((1, DC), F32)

    blk = pl.BlockSpec((S, DC), lambda c: (0, c))
    return pl.pallas_call(
        body, name="dwconv_bwd", grid=(Dm // DC,),
        in_specs=[blk, blk, pl.BlockSpec((CONV_WIDTH, DC), lambda c: (0, c))],
        out_specs=[blk, pl.BlockSpec((PAD_ROWS, DC), lambda c: (0, c))],
        out_shape=[jax.ShapeDtypeStruct((S, Dm), F32), jax.ShapeDtypeStruct((PAD_ROWS, Dm), F32)],
        scratch_shapes=[pltpu.VMEM((S + PAD_ROWS, DC), F32), pltpu.VMEM((S + PAD_ROWS, DC), F32)],
        compiler_params=_cparams("parallel"))(d, u, w)


POOL_C = D_MODEL // len(POOL_WINDOWS)
MAX_WIN = max(POOL_WINDOWS)


def _pool_counts(r, tr, win):
    t = r * tr + lax.broadcasted_iota(jnp.int32, (tr, 1), 0)
    return jnp.minimum(t + 1, win).astype(F32)


def _pool_fwd(h):
    S, Dm = h.shape
    tr = min(TR, S)

    def body(h_ref, o_ref, pad_ref):
        win = jnp.left_shift(2, pl.program_id(0))
        pad_ref[pl.ds(0, PAD_ROWS), :] = jnp.zeros((PAD_ROWS, POOL_C), F32)
        pad_ref[pl.ds(PAD_ROWS, S), :] = h_ref[...]
        for r in range(S // tr):
            acc = jnp.zeros((tr, POOL_C), F32)
            for j in range(MAX_WIN):
                use = jnp.where(j < win, 1.0, 0.0)
                acc = acc + use * pad_ref[pl.ds(r * tr + PAD_ROWS - j, tr), :]
            pooled = acc / _pool_counts(r, tr, win)
            o_ref[pl.ds(r * tr, tr), :] = (pooled - h_ref[pl.ds(r * tr, tr), :]).astype(BF16)

    blk = pl.BlockSpec((S, POOL_C), lambda g: (0, g))
    return pl.pallas_call(
        body, name="pool_fwd", grid=(len(POOL_WINDOWS),), in_specs=[blk], out_specs=blk,
        out_shape=jax.ShapeDtypeStruct((S, Dm), BF16),
        scratch_shapes=[pltpu.VMEM((S + PAD_ROWS, POOL_C), F32)],
        compiler_params=_cparams("parallel"))(h)


def _pool_bwd(dp):
    S, Dm = dp.shape
    tr = min(TR, S)

    def body(d_ref, o_ref, pad_ref):
        win = jnp.left_shift(2, pl.program_id(0))
        for r in range(S // tr):
            pad_ref[pl.ds(r * tr, tr), :] = d_ref[pl.ds(r * tr, tr), :] / _pool_counts(r, tr, win)
        pad_ref[pl.ds(S, PAD_ROWS), :] = jnp.zeros((PAD_ROWS, POOL_C), F32)
        for r in range(S // tr):
            acc = jnp.zeros((tr, POOL_C), F32)
            for j in range(MAX_WIN):
                use = jnp.where(j < win, 1.0, 0.0)
                acc = acc + use * pad_ref[pl.ds(r * tr + j, tr), :]
            o_ref[pl.ds(r * tr, tr), :] = acc - d_ref[pl.ds(r * tr, tr), :]

    blk = pl.BlockSpec((S, POOL_C), lambda g: (0, g))
    return pl.pallas_call(
        body, name="pool_bwd", grid=(len(POOL_WINDOWS),), in_specs=[blk], out_specs=blk,
        out_shape=jax.ShapeDtypeStruct((S, Dm), F32),
        scratch_shapes=[pltpu.VMEM((S + PAD_ROWS, POOL_C), F32)],
        compiler_params=_cparams("parallel"))(dp)


def _bias_spec(tn):
    return pl.BlockSpec((1, tn), lambda i, j, k: (0, j))


def _mla_weights(w_dq, w_dkv, w_uq, w_ukv, w_o):
    wd = jnp.concatenate([w_dq, w_dkv], axis=1)
    wq = jnp.pad(w_uq.reshape(Q_LORA, N_HEADS, QK_NOPE + QK_ROPE),
                 ((0, 0), (0, 0), (0, HEAD_PAD - QK_NOPE - QK_ROPE))).reshape(Q_LORA, QW)
    ukv = w_ukv.reshape(KV_LORA, N_HEADS, QK_NOPE + V_HEAD)
    wkv = jnp.zeros((DKV, N_HEADS, 2 * HEAD_PAD), BF16)
    wkv = wkv.at[:KV_LORA, :, :QK_NOPE].set(ukv[:, :, :QK_NOPE])
    wkv = wkv.at[:KV_LORA, :, HEAD_PAD:HEAD_PAD + V_HEAD].set(ukv[:, :, QK_NOPE:])
    eye = jnp.broadcast_to(jnp.eye(QK_ROPE, dtype=BF16)[:, None, :], (QK_ROPE, N_HEADS, QK_ROPE))
    wkv = wkv.at[KV_LORA:, :, QK_NOPE:QK_NOPE + QK_ROPE].set(eye).reshape(DKV, KVW)
    wo = jnp.pad(w_o.reshape(N_HEADS, V_HEAD, D_MODEL),
                 ((0, 0), (0, HEAD_PAD - V_HEAD), (0, 0))).reshape(QW, D_MODEL)
    return dict(wd=wd, wq=wq, wkv=wkv, wo=wo)


def _mla_weight_grads(g_wd, g_wq, g_wkv, g_wo):
    g_uq = g_wq.reshape(Q_LORA, N_HEADS, HEAD_PAD)[:, :, :QK_NOPE + QK_ROPE].reshape(Q_LORA, -1)
    t = g_wkv.reshape(DKV, N_HEADS, 2 * HEAD_PAD)[:KV_LORA]
    g_ukv = jnp.concatenate([t[:, :, :QK_NOPE], t[:, :, HEAD_PAD:HEAD_PAD + V_HEAD]], axis=2)
    g_o = g_wo.reshape(N_HEADS, HEAD_PAD, D_MODEL)[:, :V_HEAD].reshape(N_HEADS * V_HEAD, D_MODEL)
    return dict(mla_w_dq=g_wd[:, :Q_LORA], mla_w_uq=g_uq, mla_w_dkv=g_wd[:, Q_LORA:],
                mla_w_ukv=g_ukv.reshape(KV_LORA, -1), mla_w_o=g_o)


def _rope_epilogue(kv):
    def epi(acc, cos, sa, sb):
        parts = []
        for t in range(acc.shape[1] // HEAD_PAD):
            x = acc[:, t * HEAD_PAD:(t + 1) * HEAD_PAD]
            parts.append(x if (kv and t % 2) else _rope(x, cos, sa, sb))
        return (jnp.concatenate(parts, axis=1),)
    return epi


def _mla_fwd(tag, h, P, rope):
    S = h.shape[0]
    cos, sa, sb = rope
    tabs = [pl.BlockSpec((min(1024, S), HEAD_PAD), lambda i, j, k: (i, 0))] * 3
    cqkv = _mm(f"mla_down{tag}", h, P['wd'], 'nn', S, DQKV, D_MODEL)

    def norms(x, qg, kg):
        xq, xk, xr = x[:, :Q_LORA], x[:, Q_LORA:Q_LORA + KV_LORA], x[:, Q_LORA + KV_LORA:]
        cq = xq * lax.rsqrt(_rowmean(xq * xq) + NORM_EPS) * qg
        ck = xk * lax.rsqrt(_rowmean(xk * xk) + NORM_EPS) * kg
        return (cq, jnp.concatenate([ck, xr], axis=1)), ()

    (cq, ckv), _ = _rowk(f"mla_norms{tag}", norms, [cqkv], [P['qg'], P['kg']], [(Q_LORA, BF16), (DKV, BF16)], [])
    qf = _mm(f"mla_q{tag}", cq, P['wq'], 'nn', S, QW, Q_LORA, extras=[cos, sa, sb], extra_specs=tabs,
             epi=_rope_epilogue(False), outs=[jax.ShapeDtypeStruct((S, QW), BF16)])
    kvf = _mm(f"mla_kv{tag}", ckv, P['wkv'], 'nn', S, KVW, DKV, extras=[cos, sa, sb], extra_specs=tabs,
              epi=_rope_epilogue(True), outs=[jax.ShapeDtypeStruct((S, KVW), BF16)])
    o, lse = _attn_fwd(qf, kvf)
    y = _mm(f"mla_o{tag}", o, P['wo'], 'nn', S, D_MODEL, QW)
    return y, dict(cqkv=cqkv, cq=cq, ckv=ckv, qf=qf, kvf=kvf, o=o, lse=lse)


def _mla_bwd(tag, dy, h, sv, P, rope):
    S = h.shape[0]
    nq = S // BQ
    cos, sa, sb = rope
    g_wo = _mm(f"mla_o_wg{tag}", sv['o'], dy, 'tn', QW, D_MODEL, S)
    dob = _mm(f"mla_o_dg{tag}", dy, P['wo'], 'nt', S, QW, D_MODEL, outs=[jax.ShapeDtypeStruct((S, QW), BF16)])
    dd = _attn_delta(dob, sv['o'])
    dq_raw, dkv = _attn_bwd(sv['qf'], sv['kvf'], dob, sv['lse'].reshape(N_HEADS, nq, 1, BQ),
                            dd.reshape(N_HEADS, nq, 1, BQ), cos, sa, sb)

    def rope_bwd_q(d, cv, sav, sbv):
        parts = [_rope_t(d[:, t * HEAD_PAD:(t + 1) * HEAD_PAD] * ATT_SCALE, cv, sav, sbv) for t in range(N_HEADS)]
        return (jnp.concatenate(parts, axis=1),), ()

    (dq,), _ = _rowk(f"rope_bwd_q{tag}", rope_bwd_q, [dq_raw, cos, sa, sb], [], [(QW, BF16)], [])
    g_wq = _mm(f"mla_q_wg{tag}", sv['cq'], dq, 'tn', Q_LORA, QW, S)
    dcq = _mm(f"mla_q_dg{tag}", dq, P['wq'], 'nt', S, Q_LORA, QW)
    g_wkv = _mm(f"mla_kv_wg{tag}", sv['ckv'], dkv, 'tn', DKV, KVW, S)
    dckv = _mm(f"mla_kv_dg{tag}", dkv, P['wkv'], 'nt', S, DKV, KVW)

    def norms_bwd(dcq_v, dckv_v, x, qg, kg):
        xq, xk = x[:, :Q_LORA], x[:, Q_LORA:Q_LORA + KV_LORA]
        dxq, dqg = _gain_bwd(dcq_v, xq, qg)
        dxk, dkg = _gain_bwd(dckv_v[:, :KV_LORA], xk, kg)
        return (jnp.concatenate([dxq, dxk, dckv_v[:, KV_LORA:]], axis=1),), (dqg, dkg)

    (dcqkv,), (dqg, dkg) = _rowk(f"mla_norms_bwd{tag}", norms_bwd, [dcq, dckv, sv['cqkv']], [P['qg'], P['kg']],
                                 [(DQKV, BF16)], [Q_LORA, KV_LORA])
    g_wd = _mm(f"mla_down_wg{tag}", h, dcqkv, 'tn', D_MODEL, DQKV, S)
    dh = _mm(f"mla_down_dg{tag}", dcqkv, P['wd'], 'nt', S, D_MODEL, DQKV)
    grads = _mla_weight_grads(g_wd, g_wq, g_wkv, g_wo)
    grads.update(mla_q_norm_g=dqg.reshape(-1), mla_kv_norm_g=dkg.reshape(-1))
    return dh, grads


def _conv_fwd(h, P):
    S = h.shape[0]
    a = _mm("conv_pw1", h, P['w_pw1'], 'nn', S, 2 * D_MODEL, D_MODEL, extras=[P['b_pw1']],
            extra_specs=[_bias_spec(1024)], epi=lambda acc, b: (acc + b,))
    (u0,), _ = _rowk("conv_glu", lambda av: ((av[:, :D_MODEL] * _sigmoid(av[:, D_MODEL:]),), ()),
                     [a], [], [(D_MODEL, F32)], [])
    u1 = _dwconv_fwd(u0, P['w_dw'], P['b_dw'])

    def ln_silu(u, g, b):
        xc = u - _rowmean(u)
        z = xc * lax.rsqrt(_rowmean(xc * xc) + NORM_EPS) * g + b
        return (z * _sigmoid(z),), ()

    (u3,), _ = _rowk("conv_ln", ln_silu, [u1], [P['ln_g'], P['ln_b']], [(D_MODEL, BF16)], [])
    y = _mm("conv_pw2", u3, P['w_pw2'], 'nn', S, D_MODEL, D_MODEL, extras=[P['b_pw2']],
            extra_specs=[_bias_spec(1024)], epi=lambda acc, b: (acc + b,))
    return y, dict(a=a, u0=u0, u1=u1, u3=u3)


def _conv_bwd(dy, dy_colsum, h, sv, P):
    S = h.shape[0]
    g_pw2 = _mm("conv_pw2_wg", sv['u3'], dy, 'tn', D_MODEL, D_MODEL, S)
    du3 = _mm("conv_pw2_dg", dy, P['w_pw2'], 'nt', S, D_MODEL, D_MODEL)

    def ln_bwd(d3, u, g, b):
        xc = u - _rowmean(u)
        rstd = lax.rsqrt(_rowmean(xc * xc) + NORM_EPS)
        xh = xc * rstd
        z = xh * g + b
        sg = _sigmoid(z)
        dz = d3 * (sg * (1.0 + z * (1.0 - sg)))
        dxh = dz * g
        du = rstd * (dxh - _rowmean(dxh) - xh * _rowmean(dxh * xh))
        return (du,), (_colsum(dz * xh), _colsum(dz), _colsum(du))

    (du1,), (d_lng, d_lnb, d_bdw) = _rowk("conv_ln_bwd", ln_bwd, [du3, sv['u1']], [P['ln_g'], P['ln_b']],
                                          [(D_MODEL, F32)], [D_MODEL] * 3)
    du0, d_wdw = _dwconv_bwd(du1, sv['u0'], P['w_dw'])

    def glu_bwd(d0, av):
        a1, sg = av[:, :D_MODEL], _sigmoid(av[:, D_MODEL:])
        da = jnp.concatenate([d0 * sg, d0 * a1 * sg * (1.0 - sg)], axis=1)
        return (da,), (_colsum(da),)

    (da,), (d_bpw1,) = _rowk("conv_glu_bwd", glu_bwd, [du0, sv['a']], [], [(2 * D_MODEL, BF16)], [2 * D_MODEL])
    g_pw1 = _mm("conv_pw1_wg", h, da, 'tn', D_MODEL, 2 * D_MODEL, S)
    dh = _mm("conv_pw1_dg", da, P['w_pw1'], 'nt', S, D_MODEL, 2 * D_MODEL)
    grads = dict(conv_w_pw1=g_pw1, conv_b_pw1=d_bpw1.reshape(-1), conv_w_dw=d_wdw[:CONV_WIDTH],
                 conv_b_dw=d_bdw.reshape(-1), conv_ln_g=d_lng.reshape(-1), conv_ln_b=d_lnb.reshape(-1),
                 conv_w_pw2=g_pw2, conv_b_pw2=dy_colsum.reshape(-1))
    return dh, grads


def _pool_group_specs(tm):
    return (pl.BlockSpec((tm, POOL_C), lambda i, j, k: (i, j)),
            pl.BlockSpec((None, POOL_C, POOL_C), lambda i, j, k: (j, 0, 0)))


def _pool_mixer_fwd(h, P):
    S = h.shape[0]
    p = _pool_fwd(h)
    a_spec, b_spec = _pool_group_specs(min(1024, S))
    y, z = _mm("pool_mm", p, P['w'], 'nn', S, D_MODEL, POOL_C, tn=POOL_C, a_spec=a_spec, b_spec=b_spec,
               extras=[P['b'], P['scale']], extra_specs=[_bias_spec(POOL_C)] * 2,
               epi=lambda acc, b, s: ((acc + b) * s, acc + b),
               outs=[jax.ShapeDtypeStruct((S, D_MODEL), F32)] * 2)
    return y, dict(p=p, z=z)


def _pool_mixer_bwd(dy, sv, P):
    S = dy.shape[0]

    def scale_bwd(d, z, s):
        dz = d * s
        return (dz,), (_colsum(d * z), _colsum(dz))

    (dz,), (d_scale, d_b) = _rowk("pool_scale_bwd", scale_bwd, [dy, sv['z']], [P['scale']],
                                  [(D_MODEL, BF16)], [D_MODEL] * 2)
    a_spec, b_spec = _pool_group_specs(min(1024, S))
    dp = _mm("pool_mm_dg", dz, P['w'], 'nt', S, D_MODEL, POOL_C, tn=POOL_C, a_spec=a_spec, b_spec=b_spec)
    tk = min(512, S)
    grp = pl.BlockSpec((tk, POOL_C), lambda i, j, k: (k, j))
    g_w = _mm("pool_mm_wg", sv['p'], dz, 'tn', POOL_C, D_MODEL, S, tn=POOL_C, tk=tk, a_spec=grp, b_spec=grp,
              outs=[jax.ShapeDtypeStruct((len(POOL_WINDOWS), POOL_C, POOL_C), F32)],
              out_specs=[pl.BlockSpec((None, POOL_C, POOL_C), lambda i, j, k: (j, 0, 0))])
    dh = _pool_bwd(dp)
    return dh, dict(pool_w=g_w, pool_b=d_b.reshape(-1), pool_scale=d_scale.reshape(-1))


def _adamw(w, g, m, v):
    m2 = ADAM_B1 * m + (1.0 - ADAM_B1) * g
    v2 = ADAM_B2 * v + (1.0 - ADAM_B2) * (g * g)
    m_hat = m2 / (1.0 - ADAM_B1 ** ADAM_STEP)
    v_hat = v2 / (1.0 - ADAM_B2 ** ADAM_STEP)
    delta = -ADAM_LR * (m_hat / (jnp.sqrt(v_hat) + ADAM_EPS) + ADAM_WD * w)
    return delta, m2, v2


def _finish(name, w, land, m, v, layer=None, prev=None):
    local = land.shape[1:]
    C = local[-1]
    R = land[0].size // C
    tr = 64 if R % 64 == 0 else R
    lead = () if layer is None else (w.shape[0],)
    as2d = lambda a: a.reshape(lead + (R, C))
    n_prev = 0 if prev is None else 4

    def body(*refs):
        w_hbm, land_hbm, m_hbm, v_hbm = refs[:4]
        outs = refs[4 + n_prev:8 + n_prev]
        w_v, land_v, m_v, v_v, g_v, d_v, nm_v, nv_v, recv_v, io_sems, send_sem, recv_sem = refs[8 + n_prev:]
        pick = (lambda r: r) if layer is None else (lambda r: r.at[layer])
        loads = [pltpu.make_async_copy(src, dst, io_sems.at[k]) for k, (src, dst) in enumerate(
            [(pick(w_hbm), w_v), (land_hbm, land_v), (pick(m_hbm), m_v), (pick(v_hbm), v_v)])]
        for cp in loads:
            cp.start()
        for cp in loads:
            cp.wait()

        def rows_of(i):
            return pl.ds(pl.multiple_of(i * tr, tr), tr)

        def sum_chunk(i, carry):
            rows = rows_of(i)
            g_v[rows, :] = ((land_v[0, rows, :].astype(F32) + land_v[1, rows, :].astype(F32))
                            + land_v[2, rows, :].astype(F32)) + land_v[3, rows, :].astype(F32)
            return carry

        lax.fori_loop(0, R // tr, sum_chunk, 0)
        swap = pltpu.make_async_remote_copy(
            src_ref=g_v, dst_ref=recv_v, send_sem=send_sem, recv_sem=recv_sem,
            device_id=(lax.axis_index("x"), lax.axis_index("y"), 1 - lax.axis_index("c")),
            device_id_type=pl.DeviceIdType.MESH)
        swap.start()
        swap.wait()

        def update_chunk(i, carry):
            rows = rows_of(i)
            g = g_v[rows, :] + recv_v[rows, :]
            d, nm, nv = _adamw(w_v[rows, :], g, m_v[rows, :], v_v[rows, :])
            g_v[rows, :] = g
            d_v[rows, :] = d
            nm_v[rows, :] = nm
            nv_v[rows, :] = nv
            return carry

        lax.fori_loop(0, R // tr, update_chunk, 0)
        stores = [pltpu.make_async_copy(src, pick(dst), io_sems.at[k])
                  for k, (src, dst) in enumerate(zip([g_v, d_v, nm_v, nv_v], outs))]
        for cp in stores:
            cp.start()
        for cp in stores:
            cp.wait()

    any_spec = pl.BlockSpec(memory_space=pl.ANY)
    f32_buf = pltpu.VMEM((R, C), F32)
    res = pl.pallas_call(
        body, name=name, in_specs=[any_spec] * (4 + n_prev), out_specs=[any_spec] * 4,
        out_shape=[jax.ShapeDtypeStruct(lead + (R, C), F32)] * 4,
        input_output_aliases={4 + k: k for k in range(n_prev)},
        scratch_shapes=[f32_buf, pltpu.VMEM((4, R, C), BF16)] + [f32_buf] * 7
        + [pltpu.SemaphoreType.DMA((4,)), pltpu.SemaphoreType.DMA, pltpu.SemaphoreType.DMA],
        compiler_params=pltpu.CompilerParams(has_side_effects=True, vmem_limit_bytes=VMEM_LIMIT))(
            as2d(w), land.reshape(4, R, C), as2d(m), as2d(v), *([] if prev is None else [as2d(p) for p in prev]))
    return [r.reshape(w.shape) for r in res]


def _row(v):
    return v.reshape(1, -1)


def kernel(x, c, positions, ada_w, ada_b, norm_g, mla_w_dq, mla_q_norm_g, mla_w_uq, mla_w_dkv, mla_kv_norm_g, mla_w_ukv, mla_w_o, conv_w_pw1, conv_b_pw1, conv_w_dw, conv_b_dw, conv_ln_g, conv_ln_b, conv_w_pw2, conv_b_pw2, pool_w, pool_b, pool_scale, ffn_w1, ffn_w2, loss_target, m_ada_w, m_ada_b, m_norm_g, m_mla_w_dq, m_mla_q_norm_g, m_mla_w_uq, m_mla_w_dkv, m_mla_kv_norm_g, m_mla_w_ukv, m_mla_w_o, m_conv_w_pw1, m_conv_b_pw1, m_conv_w_dw, m_conv_b_dw, m_conv_ln_g, m_conv_ln_b, m_conv_w_pw2, m_conv_b_pw2, m_pool_w, m_pool_b, m_pool_scale, m_ffn_w1, m_ffn_w2, v_ada_w, v_ada_b, v_norm_g, v_mla_w_dq, v_mla_q_norm_g, v_mla_w_uq, v_mla_w_dkv, v_mla_kv_norm_g, v_mla_w_ukv, v_mla_w_o, v_conv_w_pw1, v_conv_b_pw1, v_conv_w_dw, v_conv_b_dw, v_conv_ln_g, v_conv_ln_b, v_conv_w_pw2, v_conv_b_pw2, v_pool_w, v_pool_b, v_pool_scale, v_ffn_w1, v_ffn_w2):
    args = dict(locals())
    W = {n: args[n] for n in WEIGHTS}
    MOM = {n: args['m_' + n] for n in WEIGHTS}
    VAR = {n: args['v_' + n] for n in WEIGHTS}
    S = x.shape[1]
    xs = x.reshape(S, D_MODEL)
    tgt = loss_target.reshape(S, D_MODEL)
    mx, my, mc = lax.axis_index("x"), lax.axis_index("y"), lax.axis_index("c")
    chip = 2 * mx + my
    n_sh = ada_w.shape[2]

    c8 = _exchange("gather_c", [c.reshape(8, D_MODEL // 8)], 'xyc')[0].reshape(8, D_MODEL)
    c8 = jnp.pad(c8, ((0, ADA_ROWS - 8), (0, 0)))
    silu = lambda v: v * _sigmoid(v)
    mod_sh = []
    for l in range(DEPTH):
        b_l = lax.dynamic_slice(ada_b[l], (chip * n_sh,), (n_sh,)).reshape(1, n_sh)
        mod_sh.append(_mm(f"ada_fwd{l}", c8, ada_w, 'nn', ADA_ROWS, n_sh, D_MODEL, tn=n_sh // 2, tk=512, pro_a=silu,
                          b_spec=pl.BlockSpec((None, 512, n_sh // 2), lambda i, j, k, l=l: (l, k, j)),
                          extras=[b_l], extra_specs=[_bias_spec(n_sh // 2)], epi=lambda acc, b: (acc + b,))[:8])
    mod_sh = jnp.stack(mod_sh, axis=1).reshape(8, DEPTH * n_sh // 128, 128)
    mod = _exchange("scatter_mod", [mod_sh], 'xy', src_by='xyc')[0]
    mod = mod.reshape(4, DEPTH, n_sh).transpose(1, 0, 2).reshape(DEPTH, 6, 1, D_MODEL)

    names = [n for n in WEIGHTS if n not in ('ada_w', 'ada_b') and SHARD_AXIS[n] is not None]
    sent = [W[n].astype(BF16) if n in BIG or n in ('ffn_w1', 'ffn_w2') else W[n] for n in names]
    G = dict(zip(names, _exchange("gather_w", sent, 'xy')))
    full = {n: _unshard(G[n], SHARD_AXIS[n]) for n in names if n not in ('ffn_w1', 'ffn_w2')}
    gains = full['norm_g']
    mla_p = []
    for j in range(mla_w_dq.shape[0]):
        P = _mla_weights(full['mla_w_dq'][j], full['mla_w_dkv'][j], full['mla_w_uq'][j], full['mla_w_ukv'][j],
                         full['mla_w_o'][j])
        P.update(qg=_row(full['mla_q_norm_g'][j]), kg=_row(full['mla_kv_norm_g'][j]))
        mla_p.append(P)
    conv_p = dict(w_pw1=full['conv_w_pw1'][0], b_pw1=_row(conv_b_pw1[0]), w_dw=full['conv_w_dw'][0],
                  b_dw=_row(conv_b_dw[0]), ln_g=_row(conv_ln_g[0]), ln_b=_row(conv_ln_b[0]),
                  w_pw2=full['conv_w_pw2'][0], b_pw2=_row(conv_b_pw2[0]))
    pool_p = dict(w=full['pool_w'][0], b=_row(full['pool_b'][0]), scale=_row(full['pool_scale'][0]))
    rope = _rope_tables(positions.reshape(S, 1).astype(F32))

    w1_nn = lambda l: pl.BlockSpec((None, None, 1024, 1024), lambda i, j, k: (j, l, 0, 0))
    w1_nt = lambda l: pl.BlockSpec((None, None, 1024, 1024), lambda i, j, k: (k, l, 0, 0))
    w2_nn = lambda l: pl.BlockSpec((None, None, 1024, 1024), lambda i, j, k: (k, l, 0, 0))
    w2_nt = lambda l: pl.BlockSpec((None, None, 1024, 1024), lambda i, j, k: (j, l, 0, 0))
    sq_relu = lambda v: jnp.square(jnp.maximum(v, 0.0))

    def md(i, k):
        return mod[i, k]

    (h,), _ = _rowk("pre0", lambda xv, g, sc, sh: ((_pre_fwd(xv, g, sc, sh),), ()),
                    [xs], [_row(gains[0, 0]), md(0, 1), md(0, 0)], [(D_MODEL, BF16)], [])
    saved = []
    xin = xs
    loss_acc = dxf = None
    for i in range(DEPTH):
        kind, j = i % 3, i // 3
        if kind == 0:
            y, sv = _mla_fwd(j, h, mla_p[j], rope)
        elif kind == 1:
            y, sv = _conv_fwd(h, conv_p)
        else:
            y, sv = _pool_mixer_fwd(h, pool_p)

        def mid(xv, yv, gt, g1, g2, sc, sh):
            x1 = _post_fwd(xv, yv, gt, g1)
            return (x1, _pre_fwd(x1, g2, sc, sh)), ()

        (x1, h2), _ = _rowk(f"mid{i}", mid, [xin, y], [md(i, 2), _row(gains[i, 1]), _row(gains[i, 2]), md(i, 4), md(i, 3)],
                            [(D_MODEL, F32), (D_MODEL, BF16)], [])
        a = _mm(f"ffn1_{i}", h2, G['ffn_w1'], 'nn', S, D_FF, D_MODEL, b_spec=w1_nn(i))
        y2 = _mm(f"ffn2_{i}", a, G['ffn_w2'], 'nn', S, D_MODEL, D_FF, pro_a=sq_relu, b_spec=w2_nn(i))
        saved.append(dict(x0=xin, h=h, y=y, x1=x1, h2=h2, a=a, y2=y2, mix=sv))
        if i + 1 < DEPTH:
            def nxt(xv, yv, gt, g3, g0, sc, sh):
                x2 = _post_fwd(xv, yv, gt, g3)
                return (x2, _pre_fwd(x2, g0, sc, sh)), ()

            hdt = F32 if (i + 1) % 3 == 2 else BF16
            (xin, h), _ = _rowk(f"next{i}", nxt, [x1, y2],
                                [md(i, 5), _row(gains[i, 3]), _row(gains[i + 1, 0]), md(i + 1, 1), md(i + 1, 0)],
                                [(D_MODEL, F32), (D_MODEL, hdt)], [])
        else:
            def head(xv, yv, tv, gt, g3):
                err = _post_fwd(xv, yv, gt, g3) - tv
                per_row = jnp.sum(err * err, axis=1, keepdims=True) * (0.5 / D_MODEL)
                return (err * (1.0 / D_MODEL),), (jnp.broadcast_to(jnp.sum(per_row, axis=0, keepdims=True), (1, 128)),)

            (dxf,), (loss_acc,) = _rowk("loss_head", head, [x1, y2, tgt], [md(i, 5), _row(gains[i, 3])],
                                        [(D_MODEL, F32)], [128])
    loss = lax.psum(loss_acc[0, 0], ("x", "y", "c"))

    grads = {}
    ffn_g1, ffn_g2 = [None] * DEPTH, [None] * DEPTH
    d_mod = [None] * DEPTH
    d_gain = [None] * DEPTH
    dx = dxf
    for i in reversed(range(DEPTH)):
        kind, j = i % 3, i // 3
        sv = saved[i]
        def post2_bwd(d, yv, gt, g):
            dyv, d_gt, d_g = _post_bwd(d, yv, gt, g)
            return (dyv,), (d_gt, d_g)

        (dy2,), (d_gtf, d_g3) = _rowk(f"post2_bwd{i}", post2_bwd, [dx, sv['y2']], [md(i, 5), _row(gains[i, 3])],
                                      [(D_MODEL, BF16)], [D_MODEL] * 2)
        da = _mm(f"ffn2_dg{i}", dy2, G['ffn_w2'], 'nt', S, D_FF, D_MODEL, b_spec=w2_nt(i), extras=[sv['a']],
                 extra_specs=[pl.BlockSpec((min(1024, S), 1024), lambda i_, j_, k_: (i_, j_))],
                 epi=lambda acc, av: (acc * (2.0 * jnp.maximum(av, 0.0)),),
                 outs=[jax.ShapeDtypeStruct((S, D_FF), BF16)])
        ffn_g2[i] = _mm(f"ffn2_wg{i}", sv['a'], dy2, 'tn', D_FF, D_MODEL, S, pro_a=sq_relu,
                        outs=[jax.ShapeDtypeStruct((4, 1024, D_MODEL), BF16)],
                        out_specs=[pl.BlockSpec((None, 1024, 1024), lambda i_, j_, k_: (i_, 0, j_))])
        ffn_g1[i] = _mm(f"ffn1_wg{i}", sv['h2'], da, 'tn', D_MODEL, D_FF, S,
                        outs=[jax.ShapeDtypeStruct((4, D_MODEL, 1024), BF16)],
                        out_specs=[pl.BlockSpec((None, 1024, 1024), lambda i_, j_, k_: (j_, i_, 0))])
        dh2 = _mm(f"ffn1_dg{i}", da, G['ffn_w1'], 'nt', S, D_MODEL, D_FF, b_spec=w1_nt(i))

        def mid_bwd(d2, dh2v, x1v, yv, g2, scf, gtm, g1):
            dpre, d_sh, d_sc, d_g2 = _pre_bwd(dh2v, x1v, g2, scf)
            d1 = d2 + dpre
            dyv, d_gt, d_g1 = _post_bwd(d1, yv, gtm, g1)
            return (d1, dyv), (d_sh, d_sc, d_g2, d_gt, d_g1, _colsum(dyv))

        ydt = F32 if kind == 2 else BF16
        (dx1, dy), (d_shf, d_scf, d_g2, d_gtm, d_g1, dy_cs) = _rowk(
            f"mid_bwd{i}", mid_bwd, [dx, dh2, sv['x1'], sv['y']],
            [_row(gains[i, 2]), md(i, 4), md(i, 2), _row(gains[i, 1])],
            [(D_MODEL, F32), (D_MODEL, ydt)], [D_MODEL] * 6)
        if kind == 0:
            dh, gm = _mla_bwd(j, dy, sv['h'], sv['mix'], mla_p[j], rope)
            for n, g in gm.items():
                grads.setdefault(n, [None] * mla_w_dq.shape[0])[j] = g
        elif kind == 1:
            dh, gm = _conv_bwd(dy, dy_cs, sv['h'], sv['mix'], conv_p)
            for n, g in gm.items():
                grads[n] = [g]
        else:
            dh, gm = _pool_mixer_bwd(dy, sv['mix'], pool_p)
            for n, g in gm.items():
                grads[n] = [g]

        def pre_bwd(d1, dhv, x0v, g0, scm):
            dpre, d_sh, d_sc, d_g0 = _pre_bwd(dhv, x0v, g0, scm)
            return (d1 + dpre,), (d_sh, d_sc, d_g0)

        (dx,), (d_shm, d_scm, d_g0) = _rowk(f"pre_bwd{i}", pre_bwd, [dx1, dh, sv['x0']],
                                            [_row(gains[i, 0]), md(i, 1)], [(D_MODEL, F32)], [D_MODEL] * 3)
        d_mod[i] = jnp.concatenate([d_shm, d_scm, d_gtm, d_shf, d_scf, d_gtf], axis=1).reshape(-1)
        d_gain[i] = jnp.concatenate([d_g0, d_g1, d_g2, d_g3], axis=0)
    grad_x = dx.reshape(x.shape)
    grads = {n: jnp.stack(g) for n, g in grads.items()}
    grads['norm_g'] = jnp.stack(d_gain)
    grads['ada_b'] = jnp.stack(d_mod)

    pack = jnp.concatenate([grads[n].reshape(-1) for n in SMALL])
    n_pack = pack.shape[0]
    rows = -(-n_pack // 1024) * 8
    pack = jnp.pad(pack, (0, rows * 128 - n_pack)).reshape(rows, 128)
    pack8 = _exchange("gather_small", [pack], 'xyc')[0]
    (tot,) = _ew("sum_small", lambda *v: (functools.reduce(lambda p, q: p + q, v),), [(pack8, s) for s in range(8)],
                 [F32], (rows, 128))
    tot = tot.reshape(-1)
    d_mod_all = pack8.reshape(8, -1)[:, :DEPTH * 6 * D_MODEL].reshape(8, DEPTH, 6 * D_MODEL)
    final = {}
    off = 0
    for n in SMALL:
        ax = SHARD_AXIS[n]
        shape = tuple(d * 4 if k == ax else d for k, d in enumerate(W[n].shape))
        size = grads[n].size
        g = tot[off:off + size].reshape(shape)
        off += size
        if ax is not None:
            g = lax.dynamic_index_in_dim(_to_shards(g, ax), chip, 0, keepdims=False)
        final[n] = g

    g_ada = []
    for l in range(DEPTH):
        dm_l = jnp.pad(lax.dynamic_slice(d_mod_all[:, l], (0, chip * n_sh), (8, n_sh)), ((0, ADA_ROWS - 8), (0, 0)))
        g_ada.append(_mm(f"ada_wg{l}", c8, dm_l, 'tn', D_MODEL, n_sh, ADA_ROWS, tn=n_sh // 2, pro_a=silu))
    final['ada_w'] = jnp.stack(g_ada)

    red_names, red = [], []
    for n in BIG:
        red_names.append(n)
        red.append(_to_shards(grads[n], SHARD_AXIS[n]).astype(BF16))
    for l in range(DEPTH):
        red_names += [f"ffn_w1:{l}", f"ffn_w2:{l}"]
        red += [ffn_g1[l], ffn_g2[l]]
    landed = dict(zip(red_names, _exchange("scatter_grads", red, 'xy', src_by='xy')))

    out_g, out_d, out_m, out_v = {}, {}, {}, {}
    for n in WEIGHTS:
        shape = W[n].shape
        if n in ('ffn_w1', 'ffn_w2'):
            res = None
            for l in range(DEPTH):
                res = _finish(f"finish_{n}{l}", W[n], landed[f"{n}:{l}"], MOM[n], VAR[n], layer=l, prev=res)
            out_g[n], out_d[n], out_m[n], out_v[n] = res
        elif n in BIG:
            out_g[n], out_d[n], out_m[n], out_v[n] = _finish(f"finish_{n}", W[n], landed[n], MOM[n], VAR[n])
        else:
            out_g[n] = final[n].reshape(shape)
            out_d[n], out_m[n], out_v[n] = _ew(f"adamw_{n}", lambda w, g, m, v: _adamw(w, g, m, v),
                                               [W[n], out_g[n], MOM[n], VAR[n]], [F32] * 3, shape)
    return (loss, grad_x, *[out_g[n] for n in WEIGHTS], *[out_d[n] for n in WEIGHTS],
            *[out_m[n] for n in WEIGHTS], *[out_v[n] for n in WEIGHTS])
```

```python
import functools
import math

import jax
import jax.numpy as jnp
from jax import lax
from jax.experimental import pallas as pl
from jax.experimental.pallas import tpu as pltpu

F32 = jnp.float32
BF16 = jnp.bfloat16

D_MODEL = 1024
DEPTH = 4
N_HEADS = 16
QK_NOPE = 64
QK_ROPE = 32
V_HEAD = 64
Q_LORA = 384
KV_LORA = 256
HEAD_PAD = 128
QW = N_HEADS * HEAD_PAD
KVW = 2 * QW
DKV = KV_LORA + QK_ROPE
DQKV = Q_LORA + DKV
D_FF = 4096
CONV_WIDTH = 31
POOL_WINDOWS = (2, 4, 8, 16)
CHUNK_SHIFT = 6
ROPE_THETA = 10000.0
NORM_EPS = 1e-6
NEG_INF = -1e30
ATT_SCALE = 1.0 / math.sqrt(QK_NOPE + QK_ROPE)
BQ = 256
HB = 4
LOG2E = 1.4426950408889634
SCALE_LOG2E = ATT_SCALE * LOG2E
PAD_ROWS = 32
ADA_ROWS = 128
VMEM_LIMIT = 56 * 1024 * 1024

ADAM_LR = 0.001
ADAM_B1 = 0.9
ADAM_B2 = 0.999
ADAM_EPS = 1e-08
ADAM_WD = 0.01
ADAM_STEP = 10

WEIGHTS = ['ada_w', 'ada_b', 'norm_g', 'mla_w_dq', 'mla_q_norm_g', 'mla_w_uq', 'mla_w_dkv', 'mla_kv_norm_g',
           'mla_w_ukv', 'mla_w_o', 'conv_w_pw1', 'conv_b_pw1', 'conv_w_dw', 'conv_b_dw', 'conv_ln_g', 'conv_ln_b',
           'conv_w_pw2', 'conv_b_pw2', 'pool_w', 'pool_b', 'pool_scale', 'ffn_w1', 'ffn_w2']
SHARD_AXIS = {'ada_w': 2, 'ada_b': None, 'norm_g': 2, 'mla_w_dq': 1, 'mla_q_norm_g': 1, 'mla_w_uq': 2,
              'mla_w_dkv': 1, 'mla_kv_norm_g': 1, 'mla_w_ukv': 2, 'mla_w_o': 1, 'conv_w_pw1': 2,
              'conv_b_pw1': None, 'conv_w_dw': 2, 'conv_b_dw': None, 'conv_ln_g': None, 'conv_ln_b': None,
              'conv_w_pw2': 1, 'conv_b_pw2': None, 'pool_w': 2, 'pool_b': 2, 'pool_scale': 1,
              'ffn_w1': 2, 'ffn_w2': 1}
MLA_MATS = ['mla_w_dq', 'mla_w_uq', 'mla_w_dkv', 'mla_w_ukv', 'mla_w_o']
BIG = MLA_MATS + ['conv_w_pw1', 'conv_w_pw2', 'pool_w']
SMALL = ['ada_b', 'norm_g', 'mla_q_norm_g', 'mla_kv_norm_g', 'conv_b_pw1', 'conv_w_dw', 'conv_b_dw',
         'conv_ln_g', 'conv_ln_b', 'conv_b_pw2', 'pool_b', 'pool_scale']


def _cparams(*sem):
    return pltpu.CompilerParams(dimension_semantics=sem, vmem_limit_bytes=VMEM_LIMIT)


def _colsum(v):
    return jnp.sum(v, axis=0, keepdims=True)


def _rowmean(v):
    return jnp.mean(v, axis=-1, keepdims=True)


def _sigmoid(v):
    return 1.0 / (1.0 + jnp.exp(-v))


def _rowk(name, fn, rows, bcast, out_row, out_acc, tm=256):
    S = rows[0].shape[0]
    tm = min(tm, S)
    assert S % tm == 0
    nin, no, na = len(rows) + len(bcast), len(out_row), len(out_acc)

    def body(*refs):
        vals = [r[...] for r in refs[:nin]]
        outs = refs[nin:nin + no]
        accs = refs[nin + no:]
        ro, ao = fn(*vals)
        for r, v in zip(outs, ro):
            r[...] = v.astype(r.dtype)
        if na:
            @pl.when(pl.program_id(0) == 0)
            def _():
                for r in accs:
                    r[...] = jnp.zeros(r.shape, r.dtype)
            for r, v in zip(accs, ao):
                r[...] += v

    in_specs = [pl.BlockSpec((tm, a.shape[1]), lambda i: (i, 0)) for a in rows]
    in_specs += [pl.BlockSpec(b.shape, lambda i, n=b.ndim: (0,) * n) for b in bcast]
    out_shape = [jax.ShapeDtypeStruct((S, w), dt) for w, dt in out_row]
    out_shape += [jax.ShapeDtypeStruct((1, w), F32) for w in out_acc]
    out_specs = [pl.BlockSpec((tm, w), lambda i: (i, 0)) for w, _ in out_row]
    out_specs += [pl.BlockSpec((1, w), lambda i: (0, 0)) for w in out_acc]
    res = pl.pallas_call(body, name=name, grid=(S // tm,), in_specs=in_specs, out_specs=out_specs,
                         out_shape=out_shape, compiler_params=_cparams("arbitrary"))(*rows, *bcast)
    return list(res[:no]), list(res[no:])


_DIMS = {'nn': ((1,), (0,)), 'nt': ((1,), (1,)), 'tn': ((0,), (0,))}


def _mm(name, a, b, mode, M, N, K, *, tm=1024, tn=1024, tk=1024, a_spec=None, b_spec=None, pro_a=None,
        extras=(), extra_specs=(), epi=None, outs=None, out_specs=None):
    tm, tn, tk = (t if d % t == 0 else d for t, d in ((min(tm, M), M), (min(tn, N), N), (min(tk, K), K)))
    nk = K // tk
    if a_spec is None:
        a_spec = (pl.BlockSpec((tk, tm), lambda i, j, k: (k, i)) if mode == 'tn'
                  else pl.BlockSpec((tm, tk), lambda i, j, k: (i, k)))
    if b_spec is None:
        b_spec = (pl.BlockSpec((tn, tk), lambda i, j, k: (j, k)) if mode == 'nt'
                  else pl.BlockSpec((tk, tn), lambda i, j, k: (k, j)))
    if outs is None:
        outs = [jax.ShapeDtypeStruct((M, N), F32)]
    if out_specs is None:
        out_specs = [pl.BlockSpec((tm, tn), lambda i, j, k: (i, j)) for _ in outs]
    ne, no = len(extras), len(outs)
    dims = (_DIMS[mode], ((), ()))

    def body(a_ref, b_ref, *rest):
        ex, out_refs = rest[:ne], rest[ne:ne + no]
        av = a_ref[...]
        if pro_a is not None:
            av = pro_a(av)
        part = lax.dot_general(av.astype(BF16), b_ref[...].astype(BF16), dims, preferred_element_type=F32)

        def finish(acc):
            vals = (acc,) if epi is None else epi(acc, *[e[...] for e in ex])
            for r, v in zip(out_refs, vals):
                r[...] = v.astype(r.dtype)

        if nk == 1:
            finish(part)
            return
        acc_ref = rest[ne + no]
        k = pl.program_id(2)

        @pl.when(k == 0)
        def _():
            acc_ref[...] = part

        @pl.when(k > 0)
        def _():
            acc_ref[...] += part

        @pl.when(k == nk - 1)
        def _():
            finish(acc_ref[...])

    res = pl.pallas_call(
        body, name=name, grid=(M // tm, N // tn, nk),
        in_specs=[a_spec, b_spec, *extra_specs], out_specs=list(out_specs), out_shape=list(outs),
        scratch_shapes=[pltpu.VMEM((tm, tn), F32)] if nk > 1 else [],
        compiler_params=_cparams("parallel", "parallel", "arbitrary"))(a, b, *extras)
    return res[0] if no == 1 else list(res)


def _row_tile(R, C, itemsize=4, budget=1 << 20):
    if R * C * itemsize <= budget or R % 8:
        return R
    t = 8
    while R % (t * 2) == 0 and t * 2 * C * itemsize <= budget:
        t *= 2
    return t


def _ew(name, fn, ins, out_dtypes, shape):
    C = shape[-1]
    R = 1
    for s in shape[:-1]:
        R *= s
    tr = _row_tile(R, C)
    ops, specs = [], []
    for it in ins:
        if isinstance(it, tuple):
            arr, idx = it
            ops.append(arr.reshape(arr.shape[0], R, C))
            specs.append(pl.BlockSpec((None, tr, C), lambda i, n=idx: (n, i, 0)))
        else:
            ops.append(it.reshape(R, C))
            specs.append(pl.BlockSpec((tr, C), lambda i: (i, 0)))
    nin = len(ops)

    def body(*refs):
        vals = fn(*[r[...] for r in refs[:nin]])
        for r, v in zip(refs[nin:], vals):
            r[...] = v.astype(r.dtype)

    res = pl.pallas_call(
        body, name=name, grid=(R // tr,), in_specs=specs,
        out_specs=[pl.BlockSpec((tr, C), lambda i: (i, 0)) for _ in out_dtypes],
        out_shape=[jax.ShapeDtypeStruct((R, C), dt) for dt in out_dtypes],
        compiler_params=_cparams("parallel"))(*ops)
    return [r.reshape(shape) for r in res]


_FLIPS = {'xyc': [(fx, fy, fc) for fx in (0, 1) for fy in (0, 1) for fc in (0, 1)][1:],
          'xy': [(1, 0, 0), (0, 1, 0), (1, 1, 0)],
          'c': [(0, 0, 1)]}
_NSLOT = {'xyc': 8, 'xy': 4, 'c': 2}


def _slot(kind, cx, cy, cc):
    return {'xyc': 4 * cx + 2 * cy + cc, 'xy': 2 * cx + cy, 'c': cc}[kind]


def _exchange(name, arrays, group, src_by=None, in_space=pl.ANY, out_space=pl.ANY):
    flips, nsl, n = _FLIPS[group], _NSLOT[group], len(arrays)
    nf = len(flips)

    def body(*refs):
        ins, outs = refs[:n], refs[n:2 * n]
        send_sems, recv_sems, loc_sems = refs[2 * n:]
        mx, my, mc = lax.axis_index("x"), lax.axis_index("y"), lax.axis_index("c")
        me = _slot(group, mx, my, mc)

        def payload(a, cx, cy, cc):
            return ins[a] if src_by is None else ins[a].at[_slot(src_by, cx, cy, cc)]

        local = [pltpu.make_async_copy(payload(a, mx, my, mc), outs[a].at[me], loc_sems.at[a])
                 for a in range(n)]
        for cp in local:
            cp.start()
        sends, recvs = [], []
        for a in range(n):
            for f, (fx, fy, fc) in enumerate(flips):
                px = 1 - mx if fx else mx
                py = 1 - my if fy else my
                pc = 1 - mc if fc else mc
                src = payload(a, px, py, pc)
                sends.append(pltpu.make_async_remote_copy(
                    src_ref=src, dst_ref=outs[a].at[me], send_sem=send_sems.at[a, f],
                    recv_sem=recv_sems.at[a, f], device_id=(px, py, pc),
                    device_id_type=pl.DeviceIdType.MESH))
                recvs.append(pltpu.make_async_remote_copy(
                    src_ref=src, dst_ref=outs[a].at[_slot(group, px, py, pc)], send_sem=send_sems.at[a, f],
                    recv_sem=recv_sems.at[a, f], device_id=(px, py, pc),
                    device_id_type=pl.DeviceIdType.MESH))
        for cp in sends:
            cp.start()
        for cp in recvs:
            cp.wait_recv()
        for cp in sends:
            cp.wait_send()
        for cp in local:
            cp.wait()

    out_shape = [jax.ShapeDtypeStruct((nsl,) + (a.shape if src_by is None else a.shape[1:]), a.dtype)
                 for a in arrays]
    res = pl.pallas_call(
        body, name=name, in_specs=[pl.BlockSpec(memory_space=in_space)] * n,
        out_specs=[pl.BlockSpec(memory_space=out_space)] * n, out_shape=out_shape,
        scratch_shapes=[pltpu.SemaphoreType.DMA((n, nf)), pltpu.SemaphoreType.DMA((n, nf)),
                        pltpu.SemaphoreType.DMA((n,))],
        compiler_params=pltpu.CompilerParams(has_side_effects=True))(*arrays)
    return list(res)


_HBM = pl.BlockSpec(memory_space=pltpu.HBM)
_SEM = pl.BlockSpec(memory_space=pltpu.SEMAPHORE)
_DATAFLOW = pltpu.SideEffectType.DATAFLOW_SIDE_EFFECTING


def _xy_copies(ins, lands, send_sems, recv_sems, src_by):
    mx, my, mc = lax.axis_index("x"), lax.axis_index("y"), lax.axis_index("c")
    me = _slot('xy', mx, my, mc)
    pairs = []
    for a in range(len(ins)):
        for f, (fx, fy, _) in enumerate(_FLIPS['xy']):
            px = 1 - mx if fx else mx
            py = 1 - my if fy else my
            src = ins[a] if src_by is None else ins[a].at[_slot(src_by, px, py, mc)]
            mk = functools.partial(pltpu.make_async_remote_copy, src_ref=src, send_sem=send_sems,
                                   recv_sem=recv_sems, device_id=(px, py, mc),
                                   device_id_type=pl.DeviceIdType.MESH)
            pairs.append((mk(dst_ref=lands[a].at[me]), mk(dst_ref=lands[a].at[_slot('xy', px, py, mc)])))
    return pairs


def _split_start(name, arrays, src_by=None):
    n = len(arrays)
    lands = [lax.empty((4,) + (a.shape if src_by is None else a.shape[1:]), a.dtype) for a in arrays]

    def body(*refs):
        ins, lnd, send_sems, recv_sems, token = refs[:n], refs[n:2 * n], refs[2 * n], refs[2 * n + 1], refs[-1]
        for to_peer, _ in _xy_copies(ins, lnd, send_sems, recv_sems, src_by):
            to_peer.start()
        token[...] = jnp.zeros(token.shape, F32)

    ops = [pltpu.with_memory_space_constraint(a, pltpu.HBM) for a in [*arrays, *lands]]
    res = pl.pallas_call(
        body, name=name, in_specs=[_HBM] * (2 * n),
        out_specs=[_SEM, _SEM] + [_HBM] * (2 * n) + [pl.BlockSpec(memory_space=pltpu.VMEM)],
        out_shape=[pltpu.SemaphoreType.DMA(()), pltpu.SemaphoreType.DMA(())]
        + [pltpu.HBM(a.shape, a.dtype) for a in ops] + [jax.ShapeDtypeStruct((8, 128), F32)],
        input_output_aliases={k: 2 + k for k in range(2 * n)},
        compiler_params=pltpu.CompilerParams(has_side_effects=_DATAFLOW))(*ops)
    return dict(n=n, src_by=src_by, send=res[0], recv=res[1], arrays=list(res[2:2 + n]),
                lands=list(res[2 + n:2 + 2 * n]), token=res[-1])


def _split_wait(name, st, after):
    n, src_by = st['n'], st['src_by']

    def wait_body(*refs):
        ins, lnd, send_sems, recv_sems = refs[:n], refs[n:2 * n], refs[2 * n], refs[2 * n + 1]
        for to_peer, from_peer in _xy_copies(ins, lnd, send_sems, recv_sems, src_by):
            to_peer.wait_send()
            from_peer.wait_recv()

    shapes = [pltpu.HBM(a.shape, a.dtype) for a in [*st['arrays'], *st['lands']]]
    res = pl.pallas_call(
        wait_body, name=name, in_specs=[_HBM] * (2 * n) + [_SEM, _SEM, pl.BlockSpec(memory_space=pl.ANY)],
        out_specs=[_HBM] * (2 * n), out_shape=shapes, input_output_aliases={k: k for k in range(2 * n)},
        compiler_params=pltpu.CompilerParams(has_side_effects=_DATAFLOW))(
            *st['arrays'], *st['lands'], st['send'], st['recv'], after)
    arrays, lands = res[:n], res[n:]

    def own_body(*refs):
        ins, outs, sems = refs[:n], refs[2 * n:3 * n], refs[3 * n]
        me = _slot('xy', lax.axis_index("x"), lax.axis_index("y"), 0)
        cps = [pltpu.make_async_copy(ins[a] if src_by is None else ins[a].at[me], outs[a].at[me], sems.at[a])
               for a in range(n)]
        for cp in cps:
            cp.start()
        for cp in cps:
            cp.wait()

    any_spec = pl.BlockSpec(memory_space=pl.ANY)
    return list(pl.pallas_call(
        own_body, name=name + "_own", in_specs=[any_spec] * (2 * n), out_specs=[any_spec] * n,
        out_shape=[jax.ShapeDtypeStruct(l.shape, l.dtype) for l in lands],
        input_output_aliases={n + a: a for a in range(n)},
        scratch_shapes=[pltpu.SemaphoreType.DMA((n,))])(*arrays, *lands))


def _unshard(g, axis):
    t = jnp.moveaxis(g, 0, axis)
    s = t.shape
    return t.reshape(s[:axis] + (s[axis] * s[axis + 1],) + s[axis + 2:])


def _to_shards(w, axis):
    s = w.shape
    t = w.reshape(s[:axis] + (4, s[axis] // 4) + s[axis + 1:])
    return jnp.moveaxis(t, axis, 0)


def _pre_fwd(x, g, sc, sh):
    r = lax.rsqrt(_rowmean(x * x) + NORM_EPS)
    return (x * r) * g * (1.0 + sc) + sh


def _pre_bwd(dh, x, g, sc):
    r = lax.rsqrt(_rowmean(x * x) + NORM_EPS)
    xn = x * r
    dxn = dh * (g * (1.0 + sc))
    dx = r * (dxn - xn * _rowmean(dxn * xn))
    t = dh * xn
    return dx, _colsum(dh), _colsum(t * g), _colsum(t * (1.0 + sc))


def _post_fwd(x, y, gt, g):
    r = lax.rsqrt(_rowmean(y * y) + NORM_EPS)
    return x + gt * ((y * r) * g)


def _post_bwd(dxo, y, gt, g):
    r = lax.rsqrt(_rowmean(y * y) + NORM_EPS)
    yn = y * r
    t = dxo * yn
    dyn = dxo * (gt * g)
    dy = r * (dyn - yn * _rowmean(dyn * yn))
    return dy, _colsum(t * g), _colsum(t * gt)


def _gain_bwd(dy, x, g):
    r = lax.rsqrt(_rowmean(x * x) + NORM_EPS)
    xn = x * r
    dxn = dy * g
    return r * (dxn - xn * _rowmean(dxn * xn)), _colsum(dy * xn)


def _rope(x, cos, sa, sb):
    return x * cos + pltpu.roll(x, HEAD_PAD - 16, 1) * sa + pltpu.roll(x, 16, 1) * sb


def _rope_t(d, cos, sa, sb):
    return d * cos + pltpu.roll(d * sa, 16, 1) + pltpu.roll(d * sb, HEAD_PAD - 16, 1)


def _rope_tables(pos_f):
    S = pos_f.shape[0]
    inv = ROPE_THETA ** (-jnp.arange(0, QK_ROPE, 2, dtype=F32) / QK_ROPE)
    inv_ext = jnp.concatenate([jnp.zeros((QK_NOPE,), F32), inv, inv,
                               jnp.zeros((HEAD_PAD - QK_NOPE - QK_ROPE,), F32)]).reshape(1, HEAD_PAD)

    def fn(p, iv):
        ang = p * iv
        lane = lax.broadcasted_iota(jnp.int32, ang.shape, 1)
        s = jnp.sin(ang)
        first = (lane >= QK_NOPE) & (lane < QK_NOPE + QK_ROPE // 2)
        second = (lane >= QK_NOPE + QK_ROPE // 2) & (lane < QK_NOPE + QK_ROPE)
        return (jnp.cos(ang), jnp.where(first, -s, 0.0), jnp.where(second, s, 0.0)), ()

    (cos, sa, sb), _ = _rowk("rope_tables", fn, [pos_f], [inv_ext], [(HEAD_PAD, F32)] * 3, [])
    return cos, sa, sb


def _diag_mask(transposed):
    r = lax.broadcasted_iota(jnp.int32, (BQ, BQ), 0) >> CHUNK_SHIFT
    c = lax.broadcasted_iota(jnp.int32, (BQ, BQ), 1) >> CHUNK_SHIFT
    return (r <= c) if transposed else (c <= r)


_NT = (((1,), (1,)), ((), ()))
_NN = (((1,), (0,)), ((), ()))


def _attn_fwd(qf, kvf):
    S = qf.shape[0]
    nq = S // BQ

    def body(q_ref, kv_ref, o_ref, lse_ref):
        qi = pl.program_id(1)
        qs = [q_ref[:, hh * HEAD_PAD:(hh + 1) * HEAD_PAD] for hh in range(HB)]

        def step(j, carry, diag):
            off = pl.multiple_of(j * BQ, BQ)
            sts = [lax.dot_general(kv_ref[pl.ds(off, BQ), pl.ds(2 * hh * HEAD_PAD, HEAD_PAD)], qs[hh], _NT,
                                   preferred_element_type=F32) for hh in range(HB)]
            mid = []
            for hh in range(HB):
                m, l, acc = carry[hh]
                st = jnp.where(_diag_mask(True), sts[hh], NEG_INF) if diag else sts[hh]
                m2 = jnp.maximum(m, jnp.max(st, axis=0, keepdims=True))
                al = jnp.exp2((m - m2) * SCALE_LOG2E)
                pt = jnp.exp2((st - m2) * SCALE_LOG2E)
                mid.append((m2, l * al + jnp.sum(pt, axis=0, keepdims=True), acc * al, pt.astype(BF16)))
            out = []
            for hh in range(HB):
                m2, l2, acc_s, ptb = mid[hh]
                v = kv_ref[pl.ds(off, BQ), pl.ds((2 * hh + 1) * HEAD_PAD, HEAD_PAD)]
                out.append((m2, l2, acc_s + lax.dot_general(v, ptb, _TN, preferred_element_type=F32)))
            return tuple(out)

        init = tuple((jnp.full((1, BQ), NEG_INF, F32), jnp.zeros((1, BQ), F32), jnp.zeros((HEAD_PAD, BQ), F32))
                     for _ in range(HB))
        carry = lax.fori_loop(0, qi, lambda j, c: step(j, c, False), init)
        carry = step(qi, carry, True)
        for hh in range(HB):
            m, l, acc = carry[hh]
            o_ref[:, hh * HEAD_PAD:(hh + 1) * HEAD_PAD] = (acc / l).T
            lse_ref[hh] = m * SCALE_LOG2E + jnp.log(l) * LOG2E

    return pl.pallas_call(
        body, name="attn_fwd", grid=(N_HEADS // HB, nq),
        in_specs=[pl.BlockSpec((BQ, HB * HEAD_PAD), lambda g, i: (i, g)),
                  pl.BlockSpec((S, 2 * HB * HEAD_PAD), lambda g, i: (0, g))],
        out_specs=[pl.BlockSpec((BQ, HB * HEAD_PAD), lambda g, i: (i, g)),
                   pl.BlockSpec((HB, None, 1, BQ), lambda g, i: (g, i, 0, 0))],
        out_shape=[jax.ShapeDtypeStruct((S, QW), F32), jax.ShapeDtypeStruct((N_HEADS, nq, 1, BQ), F32)],
        compiler_params=_cparams("parallel", "arbitrary"))(qf, kvf)


def _attn_delta(dob, o):
    S = o.shape[0]

    def body(do_ref, o_ref, dd_ref):
        dd_ref[...] = jnp.sum(do_ref[...].astype(F32) * o_ref[...], axis=1, keepdims=True)

    blk = pl.BlockSpec((S, HEAD_PAD), lambda h: (0, h))
    return pl.pallas_call(
        body, name="attn_delta", grid=(N_HEADS,), in_specs=[blk, blk],
        out_specs=pl.BlockSpec((None, S, 1), lambda h: (h, 0, 0)),
        out_shape=jax.ShapeDtypeStruct((N_HEADS, S, 1), F32),
        compiler_params=_cparams("parallel"))(dob, o)


_TN = (((0,), (0,)), ((), ()))


def _attn_bwd(qf, kvf, dob, lse_row, dd_row, cos, sa, sb):
    S = qf.shape[0]
    nq = S // BQ

    def body(kv_ref, q_ref, do_ref, lse_ref, dd_ref, cos_ref, sa_ref, sb_ref, dq_ref, dkv_ref):
        kj = pl.program_id(1)

        @pl.when(kj == 0)
        def _():
            dq_ref[...] = jnp.zeros(dq_ref.shape, F32)

        ks = [kv_ref[:, 2 * hh * HEAD_PAD:(2 * hh + 1) * HEAD_PAD] for hh in range(HB)]
        vs = [kv_ref[:, (2 * hh + 1) * HEAD_PAD:(2 * hh + 2) * HEAD_PAD] for hh in range(HB)]

        def step(i, carry, diag):
            off = pl.multiple_of(i * BQ, BQ)
            cols = [pl.ds(hh * HEAD_PAD, HEAD_PAD) for hh in range(HB)]
            q = [q_ref[pl.ds(off, BQ), cols[hh]] for hh in range(HB)]
            do = [do_ref[pl.ds(off, BQ), cols[hh]] for hh in range(HB)]
            sts = [lax.dot_general(ks[hh], q[hh], _NT, preferred_element_type=F32) for hh in range(HB)]
            dpts = [lax.dot_general(vs[hh], do[hh], _NT, preferred_element_type=F32) for hh in range(HB)]
            mid = []
            for hh in range(HB):
                st = jnp.where(_diag_mask(True), sts[hh], NEG_INF) if diag else sts[hh]
                pt = jnp.exp2(st * SCALE_LOG2E - lse_ref[hh, i])
                mid.append((pt.astype(BF16), (pt * (dpts[hh] - dd_ref[hh, i])).astype(BF16)))
            out = []
            for hh in range(HB):
                dk, dv = carry[hh]
                ptb, dsb = mid[hh]
                dv2 = dv + lax.dot_general(ptb, do[hh], _NN, preferred_element_type=F32)
                dk2 = dk + lax.dot_general(dsb, q[hh], _NN, preferred_element_type=F32)
                dq_ref[pl.ds(off, BQ), cols[hh]] += lax.dot_general(dsb, ks[hh], _TN, preferred_element_type=F32)
                out.append((dk2, dv2))
            return tuple(out)

        zero = jnp.zeros((BQ, HEAD_PAD), F32)
        carry = step(kj, tuple((zero, zero) for _ in range(HB)), True)
        carry = lax.fori_loop(kj + 1, nq, lambda i, c: step(i, c, False), carry)
        for hh in range(HB):
            dk, dv = carry[hh]
            dk = _rope_t(dk * ATT_SCALE, cos_ref[...], sa_ref[...], sb_ref[...])
            dkv_ref[:, 2 * hh * HEAD_PAD:(2 * hh + 1) * HEAD_PAD] = dk.astype(BF16)
            dkv_ref[:, (2 * hh + 1) * HEAD_PAD:(2 * hh + 2) * HEAD_PAD] = dv.astype(BF16)

    tab = pl.BlockSpec((BQ, HEAD_PAD), lambda g, j: (j, 0))
    row = pl.BlockSpec((HB, nq, 1, BQ), lambda g, j: (g, 0, 0, 0))
    seq = pl.BlockSpec((S, HB * HEAD_PAD), lambda g, j: (0, g))
    kvb = pl.BlockSpec((BQ, 2 * HB * HEAD_PAD), lambda g, j: (j, g))
    return pl.pallas_call(
        body, name="attn_bwd", grid=(N_HEADS // HB, nq),
        in_specs=[kvb, seq, seq, row, row, tab, tab, tab],
        out_specs=[seq, kvb],
        out_shape=[jax.ShapeDtypeStruct((S, QW), F32), jax.ShapeDtypeStruct((S, KVW), BF16)],
        compiler_params=_cparams("parallel", "arbitrary"))(kvf, qf, dob, lse_row, dd_row, cos, sa, sb)


DC = 128
TR = 256


def _dwconv_fwd(u, w, b):
    S, Dm = u.shape
    tr = min(TR, S)

    def body(u_ref, w_ref, b_ref, o_ref, pad_ref):
        pad_ref[pl.ds(0, PAD_ROWS), :] = jnp.zeros((PAD_ROWS, DC), F32)
        pad_ref[pl.ds(PAD_ROWS, S), :] = u_ref[...]
        wv = w_ref[...]
        for r in range(S // tr):
            acc = jnp.broadcast_to(b_ref[...], (tr, DC))
            for j in range(CONV_WIDTH):
                acc = acc + wv[j:j + 1, :] * pad_ref[pl.ds(r * tr + PAD_ROWS - (CONV_WIDTH - 1) + j, tr), :]
            o_ref[pl.ds(r * tr, tr), :] = acc

    return pl.pallas_call(
        body, name="dwconv_fwd", grid=(Dm // DC,),
        in_specs=[pl.BlockSpec((S, DC), lambda c: (0, c)), pl.BlockSpec((CONV_WIDTH, DC), lambda c: (0, c)),
                  pl.BlockSpec((1, DC), lambda c: (0, c))],
        out_specs=pl.BlockSpec((S, DC), lambda c: (0, c)),
        out_shape=jax.ShapeDtypeStruct((S, Dm), F32),
        scratch_shapes=[pltpu.VMEM((S + PAD_ROWS, DC), F32)],
        compiler_params=_cparams("parallel"))(u, w, b)


def _dwconv_bwd(d, u, w):
    S, Dm = u.shape
    tr = min(TR, S)

    def body(d_ref, u_ref, w_ref, du_ref, dw_ref, padd_ref, padu_ref):
        padd_ref[pl.ds(0, S), :] = d_ref[...]
        padd_ref[pl.ds(S, PAD_ROWS), :] = jnp.zeros((PAD_ROWS, DC), F32)
        padu_ref[pl.ds(0, PAD_ROWS), :] = jnp.zeros((PAD_ROWS, DC), F32)
        padu_ref[pl.ds(PAD_ROWS, S), :] = u_ref[...]
        wv = w_ref[...]
        dws = [jnp.zeros((1, DC), F32) for _ in range(CONV_WIDTH)]
        for r in range(S // tr):
            acc = jnp.zeros((tr, DC), F32)
            for j in range(CONV_WIDTH):
                acc = acc + wv[j:j + 1, :] * padd_ref[pl.ds(r * tr + (CONV_WIDTH - 1) - j, tr), :]
            du_ref[pl.ds(r * tr, tr), :] = acc
            dt = d_ref[pl.ds(r * tr, tr), :]
            for j in range(CONV_WIDTH):
                ut = padu_ref[pl.ds(r * tr + PAD_ROWS - (CONV_WIDTH - 1) + j, tr), :]
                dws[j] = dws[j] + _colsum(dt * ut)
        for j in range(CONV_WIDTH):
            dw_ref[pl.ds(j, 1), :] = dws[j]
        dw_ref[pl.ds(CONV_WIDTH, 1), :] = jnp.zeros((1, DC), F32)

    blk = pl.BlockSpec((S, DC), lambda c: (0, c))
    return pl.pallas_call(
        body, name="dwconv_bwd", grid=(Dm // DC,),
        in_specs=[blk, blk, pl.BlockSpec((CONV_WIDTH, DC), lambda c: (0, c))],
        out_specs=[blk, pl.BlockSpec((PAD_ROWS, DC), lambda c: (0, c))],
        out_shape=[jax.ShapeDtypeStruct((S, Dm), F32), jax.ShapeDtypeStruct((PAD_ROWS, Dm), F32)],
        scratch_shapes=[pltpu.VMEM((S + PAD_ROWS, DC), F32), pltpu.VMEM((S + PAD_ROWS, DC), F32)],
        compiler_params=_cparams("parallel"))(d, u, w)


POOL_C = D_MODEL // len(POOL_WINDOWS)
MAX_WIN = max(POOL_WINDOWS)


def _pool_counts(r, tr, win):
    t = r * tr + lax.broadcasted_iota(jnp.int32, (tr, 1), 0)
    return jnp.minimum(t + 1, win).astype(F32)


def _pool_fwd(h):
    S, Dm = h.shape
    tr = min(TR, S)

    def body(h_ref, o_ref, pad_ref):
        win = jnp.left_shift(2, pl.program_id(0))
        pad_ref[pl.ds(0, PAD_ROWS), :] = jnp.zeros((PAD_ROWS, POOL_C), F32)
        pad_ref[pl.ds(PAD_ROWS, S), :] = h_ref[...]
        for r in range(S // tr):
            acc = jnp.zeros((tr, POOL_C), F32)
            for j in range(MAX_WIN):
                use = jnp.where(j < win, 1.0, 0.0)
                acc = acc + use * pad_ref[pl.ds(r * tr + PAD_ROWS - j, tr), :]
            pooled = acc / _pool_counts(r, tr, win)
            o_ref[pl.ds(r * tr, tr), :] = (pooled - h_ref[pl.ds(r * tr, tr), :]).astype(BF16)

    blk = pl.BlockSpec((S, POOL_C), lambda g: (0, g))
    return pl.pallas_call(
        body, name="pool_fwd", grid=(len(POOL_WINDOWS),), in_specs=[blk], out_specs=blk,
        out_shape=jax.ShapeDtypeStruct((S, Dm), BF16),
        scratch_shapes=[pltpu.VMEM((S + PAD_ROWS, POOL_C), F32)],
        compiler_params=_cparams("parallel"))(h)


def _pool_bwd(dp):
    S, Dm = dp.shape
    tr = min(TR, S)

    def body(d_ref, o_ref, pad_ref):
        win = jnp.left_shift(2, pl.program_id(0))
        for r in range(S // tr):
            pad_ref[pl.ds(r * tr, tr), :] = d_ref[pl.ds(r * tr, tr), :] / _pool_counts(r, tr, win)
        pad_ref[pl.ds(S, PAD_ROWS), :] = jnp.zeros((PAD_ROWS, POOL_C), F32)
        for r in range(S // tr):
            acc = jnp.zeros((tr, POOL_C), F32)
            for j in range(MAX_WIN):
                use = jnp.where(j < win, 1.0, 0.0)
                acc = acc + use * pad_ref[pl.ds(r * tr + j, tr), :]
            o_ref[pl.ds(r * tr, tr), :] = acc - d_ref[pl.ds(r * tr, tr), :]

    blk = pl.BlockSpec((S, POOL_C), lambda g: (0, g))
    return pl.pallas_call(
        body, name="pool_bwd", grid=(len(POOL_WINDOWS),), in_specs=[blk], out_specs=blk,
        out_shape=jax.ShapeDtypeStruct((S, Dm), F32),
        scratch_shapes=[pltpu.VMEM((S + PAD_ROWS, POOL_C), F32)],
        compiler_params=_cparams("parallel"))(dp)


def _bias_spec(tn):
    return pl.BlockSpec((1, tn), lambda i, j, k: (0, j))


def _mla_weights(w_dq, w_dkv, w_uq, w_ukv, w_o):
    wd = jnp.concatenate([w_dq, w_dkv], axis=1)
    wq = jnp.pad(w_uq.reshape(Q_LORA, N_HEADS, QK_NOPE + QK_ROPE),
                 ((0, 0), (0, 0), (0, HEAD_PAD - QK_NOPE - QK_ROPE))).reshape(Q_LORA, QW)
    ukv = w_ukv.reshape(KV_LORA, N_HEADS, QK_NOPE + V_HEAD)
    wkv = jnp.zeros((DKV, N_HEADS, 2 * HEAD_PAD), BF16)
    wkv = wkv.at[:KV_LORA, :, :QK_NOPE].set(ukv[:, :, :QK_NOPE])
    wkv = wkv.at[:KV_LORA, :, HEAD_PAD:HEAD_PAD + V_HEAD].set(ukv[:, :, QK_NOPE:])
    eye = jnp.broadcast_to(jnp.eye(QK_ROPE, dtype=BF16)[:, None, :], (QK_ROPE, N_HEADS, QK_ROPE))
    wkv = wkv.at[KV_LORA:, :, QK_NOPE:QK_NOPE + QK_ROPE].set(eye).reshape(DKV, KVW)
    wo = jnp.pad(w_o.reshape(N_HEADS, V_HEAD, D_MODEL),
                 ((0, 0), (0, HEAD_PAD - V_HEAD), (0, 0))).reshape(QW, D_MODEL)
    return dict(wd=wd, wq=wq, wkv=wkv, wo=wo)


def _mla_weight_grads(g_wd, g_wq, g_wkv, g_wo):
    g_uq = g_wq.reshape(Q_LORA, N_HEADS, HEAD_PAD)[:, :, :QK_NOPE + QK_ROPE].reshape(Q_LORA, -1)
    t = g_wkv.reshape(DKV, N_HEADS, 2 * HEAD_PAD)[:KV_LORA]
    g_ukv = jnp.concatenate([t[:, :, :QK_NOPE], t[:, :, HEAD_PAD:HEAD_PAD + V_HEAD]], axis=2)
    g_o = g_wo.reshape(N_HEADS, HEAD_PAD, D_MODEL)[:, :V_HEAD].reshape(N_HEADS * V_HEAD, D_MODEL)
    return dict(mla_w_dq=g_wd[:, :Q_LORA], mla_w_uq=g_uq, mla_w_dkv=g_wd[:, Q_LORA:],
                mla_w_ukv=g_ukv.reshape(KV_LORA, -1), mla_w_o=g_o)


def _rope_epilogue(kv):
    def epi(acc, cos, sa, sb):
        parts = []
        for t in range(acc.shape[1] // HEAD_PAD):
            x = acc[:, t * HEAD_PAD:(t + 1) * HEAD_PAD]
            parts.append(x if (kv and t % 2) else _rope(x, cos, sa, sb))
        return (jnp.concatenate(parts, axis=1),)
    return epi


def _mla_fwd(tag, h, P, rope):
    S = h.shape[0]
    cos, sa, sb = rope
    tabs = [pl.BlockSpec((min(1024, S), HEAD_PAD), lambda i, j, k: (i, 0))] * 3
    cqkv = _mm(f"mla_down{tag}", h, P['wd'], 'nn', S, DQKV, D_MODEL)

    def norms(x, qg, kg):
        xq, xk, xr = x[:, :Q_LORA], x[:, Q_LORA:Q_LORA + KV_LORA], x[:, Q_LORA + KV_LORA:]
        cq = xq * lax.rsqrt(_rowmean(xq * xq) + NORM_EPS) * qg
        ck = xk * lax.rsqrt(_rowmean(xk * xk) + NORM_EPS) * kg
        return (cq, jnp.concatenate([ck, xr], axis=1)), ()

    (cq, ckv), _ = _rowk(f"mla_norms{tag}", norms, [cqkv], [P['qg'], P['kg']], [(Q_LORA, BF16), (DKV, BF16)], [])
    qf = _mm(f"mla_q{tag}", cq, P['wq'], 'nn', S, QW, Q_LORA, extras=[cos, sa, sb], extra_specs=tabs,
             epi=_rope_epilogue(False), outs=[jax.ShapeDtypeStruct((S, QW), BF16)])
    kvf = _mm(f"mla_kv{tag}", ckv, P['wkv'], 'nn', S, KVW, DKV, extras=[cos, sa, sb], extra_specs=tabs,
              epi=_rope_epilogue(True), outs=[jax.ShapeDtypeStruct((S, KVW), BF16)])
    o, lse = _attn_fwd(qf, kvf)
    y = _mm(f"mla_o{tag}", o, P['wo'], 'nn', S, D_MODEL, QW)
    return y, dict(cqkv=cqkv, cq=cq, ckv=ckv, qf=qf, kvf=kvf, o=o, lse=lse)


def _mla_bwd(tag, dy, h, sv, P, rope):
    S = h.shape[0]
    nq = S // BQ
    cos, sa, sb = rope
    g_wo = _mm(f"mla_o_wg{tag}", sv['o'], dy, 'tn', QW, D_MODEL, S)
    dob = _mm(f"mla_o_dg{tag}", dy, P['wo'], 'nt', S, QW, D_MODEL, outs=[jax.ShapeDtypeStruct((S, QW), BF16)])
    dd = _attn_delta(dob, sv['o'])
    dq_raw, dkv = _attn_bwd(sv['qf'], sv['kvf'], dob, sv['lse'].reshape(N_HEADS, nq, 1, BQ),
                            dd.reshape(N_HEADS, nq, 1, BQ), cos, sa, sb)

    def rope_bwd_q(d, cv, sav, sbv):
        parts = [_rope_t(d[:, t * HEAD_PAD:(t + 1) * HEAD_PAD] * ATT_SCALE, cv, sav, sbv) for t in range(N_HEADS)]
        return (jnp.concatenate(parts, axis=1),), ()

    (dq,), _ = _rowk(f"rope_bwd_q{tag}", rope_bwd_q, [dq_raw, cos, sa, sb], [], [(QW, BF16)], [])
    g_wq = _mm(f"mla_q_wg{tag}", sv['cq'], dq, 'tn', Q_LORA, QW, S)
    dcq = _mm(f"mla_q_dg{tag}", dq, P['wq'], 'nt', S, Q_LORA, QW)
    g_wkv = _mm(f"mla_kv_wg{tag}", sv['ckv'], dkv, 'tn', DKV, KVW, S)
    dckv = _mm(f"mla_kv_dg{tag}", dkv, P['wkv'], 'nt', S, DKV, KVW)

    def norms_bwd(dcq_v, dckv_v, x, qg, kg):
        xq, xk = x[:, :Q_LORA], x[:, Q_LORA:Q_LORA + KV_LORA]
        dxq, dqg = _gain_bwd(dcq_v, xq, qg)
        dxk, dkg = _gain_bwd(dckv_v[:, :KV_LORA], xk, kg)
        return (jnp.concatenate([dxq, dxk, dckv_v[:, KV_LORA:]], axis=1),), (dqg, dkg)

    (dcqkv,), (dqg, dkg) = _rowk(f"mla_norms_bwd{tag}", norms_bwd, [dcq, dckv, sv['cqkv']], [P['qg'], P['kg']],
                                 [(DQKV, BF16)], [Q_LORA, KV_LORA])
    g_wd = _mm(f"mla_down_wg{tag}", h, dcqkv, 'tn', D_MODEL, DQKV, S)
    dh = _mm(f"mla_down_dg{tag}", dcqkv, P['wd'], 'nt', S, D_MODEL, DQKV)
    grads = _mla_weight_grads(g_wd, g_wq, g_wkv, g_wo)
    grads.update(mla_q_norm_g=dqg.reshape(-1), mla_kv_norm_g=dkg.reshape(-1))
    return dh, grads


def _conv_fwd(h, P):
    S = h.shape[0]
    a = _mm("conv_pw1", h, P['w_pw1'], 'nn', S, 2 * D_MODEL, D_MODEL, extras=[P['b_pw1']],
            extra_specs=[_bias_spec(1024)], epi=lambda acc, b: (acc + b,))
    (u0,), _ = _rowk("conv_glu", lambda av: ((av[:, :D_MODEL] * _sigmoid(av[:, D_MODEL:]),), ()),
                     [a], [], [(D_MODEL, F32)], [])
    u1 = _dwconv_fwd(u0, P['w_dw'], P['b_dw'])

    def ln_silu(u, g, b):
        xc = u - _rowmean(u)
        z = xc * lax.rsqrt(_rowmean(xc * xc) + NORM_EPS) * g + b
        return (z * _sigmoid(z),), ()

    (u3,), _ = _rowk("conv_ln", ln_silu, [u1], [P['ln_g'], P['ln_b']], [(D_MODEL, BF16)], [])
    y = _mm("conv_pw2", u3, P['w_pw2'], 'nn', S, D_MODEL, D_MODEL, extras=[P['b_pw2']],
            extra_specs=[_bias_spec(1024)], epi=lambda acc, b: (acc + b,))
    return y, dict(a=a, u0=u0, u1=u1, u3=u3)


def _conv_bwd(dy, dy_colsum, h, sv, P):
    S = h.shape[0]
    g_pw2 = _mm("conv_pw2_wg", sv['u3'], dy, 'tn', D_MODEL, D_MODEL, S)
    du3 = _mm("conv_pw2_dg", dy, P['w_pw2'], 'nt', S, D_MODEL, D_MODEL)

    def ln_bwd(d3, u, g, b):
        xc = u - _rowmean(u)
        rstd = lax.rsqrt(_rowmean(xc * xc) + NORM_EPS)
        xh = xc * rstd
        z = xh * g + b
        sg = _sigmoid(z)
        dz = d3 * (sg * (1.0 + z * (1.0 - sg)))
        dxh = dz * g
        du = rstd * (dxh - _rowmean(dxh) - xh * _rowmean(dxh * xh))
        return (du,), (_colsum(dz * xh), _colsum(dz), _colsum(du))

    (du1,), (d_lng, d_lnb, d_bdw) = _rowk("conv_ln_bwd", ln_bwd, [du3, sv['u1']], [P['ln_g'], P['ln_b']],
                                          [(D_MODEL, F32)], [D_MODEL] * 3)
    du0, d_wdw = _dwconv_bwd(du1, sv['u0'], P['w_dw'])

    def glu_bwd(d0, av):
        a1, sg = av[:, :D_MODEL], _sigmoid(av[:, D_MODEL:])
        da = jnp.concatenate([d0 * sg, d0 * a1 * sg * (1.0 - sg)], axis=1)
        return (da,), (_colsum(da),)

    (da,), (d_bpw1,) = _rowk("conv_glu_bwd", glu_bwd, [du0, sv['a']], [], [(2 * D_MODEL, BF16)], [2 * D_MODEL])
    g_pw1 = _mm("conv_pw1_wg", h, da, 'tn', D_MODEL, 2 * D_MODEL, S)
    dh = _mm("conv_pw1_dg", da, P['w_pw1'], 'nt', S, D_MODEL, 2 * D_MODEL)
    grads = dict(conv_w_pw1=g_pw1, conv_b_pw1=d_bpw1.reshape(-1), conv_w_dw=d_wdw[:CONV_WIDTH],
                 conv_b_dw=d_bdw.reshape(-1), conv_ln_g=d_lng.reshape(-1), conv_ln_b=d_lnb.reshape(-1),
                 conv_w_pw2=g_pw2, conv_b_pw2=dy_colsum.reshape(-1))
    return dh, grads


def _pool_group_specs(tm):
    return (pl.BlockSpec((tm, POOL_C), lambda i, j, k: (i, j)),
            pl.BlockSpec((None, POOL_C, POOL_C), lambda i, j, k: (j, 0, 0)))


def _pool_mixer_fwd(h, P):
    S = h.shape[0]
    p = _pool_fwd(h)
    a_spec, b_spec = _pool_group_specs(min(1024, S))
    y, z = _mm("pool_mm", p, P['w'], 'nn', S, D_MODEL, POOL_C, tn=POOL_C, a_spec=a_spec, b_spec=b_spec,
               extras=[P['b'], P['scale']], extra_specs=[_bias_spec(POOL_C)] * 2,
               epi=lambda acc, b, s: ((acc + b) * s, acc + b),
               outs=[jax.ShapeDtypeStruct((S, D_MODEL), F32)] * 2)
    return y, dict(p=p, z=z)


def _pool_mixer_bwd(dy, sv, P):
    S = dy.shape[0]

    def scale_bwd(d, z, s):
        dz = d * s
        return (dz,), (_colsum(d * z), _colsum(dz))

    (dz,), (d_scale, d_b) = _rowk("pool_scale_bwd", scale_bwd, [dy, sv['z']], [P['scale']],
                                  [(D_MODEL, BF16)], [D_MODEL] * 2)
    a_spec, b_spec = _pool_group_specs(min(1024, S))
    dp = _mm("pool_mm_dg", dz, P['w'], 'nt', S, D_MODEL, POOL_C, tn=POOL_C, a_spec=a_spec, b_spec=b_spec)
    tk = min(512, S)
    grp = pl.BlockSpec((tk, POOL_C), lambda i, j, k: (k, j))
    g_w = _mm("pool_mm_wg", sv['p'], dz, 'tn', POOL_C, D_MODEL, S, tn=POOL_C, tk=tk, a_spec=grp, b_spec=grp,
              outs=[jax.ShapeDtypeStruct((len(POOL_WINDOWS), POOL_C, POOL_C), F32)],
              out_specs=[pl.BlockSpec((None, POOL_C, POOL_C), lambda i, j, k: (j, 0, 0))])
    dh = _pool_bwd(dp)
    return dh, dict(pool_w=g_w, pool_b=d_b.reshape(-1), pool_scale=d_scale.reshape(-1))


def _adamw(w, g, m, v):
    m2 = ADAM_B1 * m + (1.0 - ADAM_B1) * g
    v2 = ADAM_B2 * v + (1.0 - ADAM_B2) * (g * g)
    m_hat = m2 / (1.0 - ADAM_B1 ** ADAM_STEP)
    v_hat = v2 / (1.0 - ADAM_B2 ** ADAM_STEP)
    delta = -ADAM_LR * (m_hat / (jnp.sqrt(v_hat) + ADAM_EPS) + ADAM_WD * w)
    return delta, m2, v2


def _finish(name, w, land, m, v, layer=None, prev=None):
    local = land.shape[1:]
    C = local[-1]
    R = land[0].size // C
    tr = 64 if R % 64 == 0 else R
    lead = () if layer is None else (w.shape[0],)
    as2d = lambda a: a.reshape(lead + (R, C))
    n_prev = 0 if prev is None else 4

    def body(*refs):
        w_hbm, land_hbm, m_hbm, v_hbm = refs[:4]
        outs = refs[4 + n_prev:8 + n_prev]
        w_v, land_v, m_v, v_v, g_v, d_v, nm_v, nv_v, recv_v, io_sems, send_sem, recv_sem = refs[8 + n_prev:]
        pick = (lambda r: r) if layer is None else (lambda r: r.at[layer])
        loads = [pltpu.make_async_copy(src, dst, io_sems.at[k]) for k, (src, dst) in enumerate(
            [(pick(w_hbm), w_v), (land_hbm, land_v), (pick(m_hbm), m_v), (pick(v_hbm), v_v)])]
        for cp in loads:
            cp.start()
        for cp in loads:
            cp.wait()

        def rows_of(i):
            return pl.ds(pl.multiple_of(i * tr, tr), tr)

        def sum_chunk(i, carry):
            rows = rows_of(i)
            g_v[rows, :] = ((land_v[0, rows, :].astype(F32) + land_v[1, rows, :].astype(F32))
                            + land_v[2, rows, :].astype(F32)) + land_v[3, rows, :].astype(F32)
            return carry

        lax.fori_loop(0, R // tr, sum_chunk, 0)
        swap = pltpu.make_async_remote_copy(
            src_ref=g_v, dst_ref=recv_v, send_sem=send_sem, recv_sem=recv_sem,
            device_id=(lax.axis_index("x"), lax.axis_index("y"), 1 - lax.axis_index("c")),
            device_id_type=pl.DeviceIdType.MESH)
        swap.start()
        swap.wait()

        def update_chunk(i, carry):
            rows = rows_of(i)
            g = g_v[rows, :] + recv_v[rows, :]
            d, nm, nv = _adamw(w_v[rows, :], g, m_v[rows, :], v_v[rows, :])
            g_v[rows, :] = g
            d_v[rows, :] = d
            nm_v[rows, :] = nm
            nv_v[rows, :] = nv
            return carry

        lax.fori_loop(0, R // tr, update_chunk, 0)
        stores = [pltpu.make_async_copy(src, pick(dst), io_sems.at[k])
                  for k, (src, dst) in enumerate(zip([g_v, d_v, nm_v, nv_v], outs))]
        for cp in stores:
            cp.start()
        for cp in stores:
            cp.wait()

    any_spec = pl.BlockSpec(memory_space=pl.ANY)
    f32_buf = pltpu.VMEM((R, C), F32)
    res = pl.pallas_call(
        body, name=name, in_specs=[any_spec] * (4 + n_prev), out_specs=[any_spec] * 4,
        out_shape=[jax.ShapeDtypeStruct(lead + (R, C), F32)] * 4,
        input_output_aliases={4 + k: k for k in range(n_prev)},
        scratch_shapes=[f32_buf, pltpu.VMEM((4, R, C), BF16)] + [f32_buf] * 7
        + [pltpu.SemaphoreType.DMA((4,)), pltpu.SemaphoreType.DMA, pltpu.SemaphoreType.DMA],
        compiler_params=pltpu.CompilerParams(has_side_effects=True, vmem_limit_bytes=VMEM_LIMIT))(
            as2d(w), land.reshape(4, R, C), as2d(m), as2d(v), *([] if prev is None else [as2d(p) for p in prev]))
    return [r.reshape(w.shape) for r in res]


def _row(v):
    return v.reshape(1, -1)


def kernel(x, c, positions, ada_w, ada_b, norm_g, mla_w_dq, mla_q_norm_g, mla_w_uq, mla_w_dkv, mla_kv_norm_g, mla_w_ukv, mla_w_o, conv_w_pw1, conv_b_pw1, conv_w_dw, conv_b_dw, conv_ln_g, conv_ln_b, conv_w_pw2, conv_b_pw2, pool_w, pool_b, pool_scale, ffn_w1, ffn_w2, loss_target, m_ada_w, m_ada_b, m_norm_g, m_mla_w_dq, m_mla_q_norm_g, m_mla_w_uq, m_mla_w_dkv, m_mla_kv_norm_g, m_mla_w_ukv, m_mla_w_o, m_conv_w_pw1, m_conv_b_pw1, m_conv_w_dw, m_conv_b_dw, m_conv_ln_g, m_conv_ln_b, m_conv_w_pw2, m_conv_b_pw2, m_pool_w, m_pool_b, m_pool_scale, m_ffn_w1, m_ffn_w2, v_ada_w, v_ada_b, v_norm_g, v_mla_w_dq, v_mla_q_norm_g, v_mla_w_uq, v_mla_w_dkv, v_mla_kv_norm_g, v_mla_w_ukv, v_mla_w_o, v_conv_w_pw1, v_conv_b_pw1, v_conv_w_dw, v_conv_b_dw, v_conv_ln_g, v_conv_ln_b, v_conv_w_pw2, v_conv_b_pw2, v_pool_w, v_pool_b, v_pool_scale, v_ffn_w1, v_ffn_w2):
    args = dict(locals())
    W = {n: args[n] for n in WEIGHTS}
    MOM = {n: args['m_' + n] for n in WEIGHTS}
    VAR = {n: args['v_' + n] for n in WEIGHTS}
    S = x.shape[1]
    xs = x.reshape(S, D_MODEL)
    tgt = loss_target.reshape(S, D_MODEL)
    mx, my, mc = lax.axis_index("x"), lax.axis_index("y"), lax.axis_index("c")
    chip = 2 * mx + my
    n_sh = ada_w.shape[2]

    c8 = _exchange("gather_c", [c.reshape(8, D_MODEL // 8)], 'xyc')[0].reshape(8, D_MODEL)
    c8 = jnp.pad(c8, ((0, ADA_ROWS - 8), (0, 0)))
    silu = lambda v: v * _sigmoid(v)
    mod_sh = []
    for l in range(DEPTH):
        b_l = lax.dynamic_slice(ada_b[l], (chip * n_sh,), (n_sh,)).reshape(1, n_sh)
        mod_sh.append(_mm(f"ada_fwd{l}", c8, ada_w, 'nn', ADA_ROWS, n_sh, D_MODEL, tn=n_sh // 2, tk=512, pro_a=silu,
                          b_spec=pl.BlockSpec((None, 512, n_sh // 2), lambda i, j, k, l=l: (l, k, j)),
                          extras=[b_l], extra_specs=[_bias_spec(n_sh // 2)], epi=lambda acc, b: (acc + b,))[:8])
    mod_sh = jnp.stack(mod_sh, axis=1).reshape(8, DEPTH * n_sh // 128, 128)
    mod = _exchange("scatter_mod", [mod_sh], 'xy', src_by='xyc')[0]
    mod = mod.reshape(4, DEPTH, n_sh).transpose(1, 0, 2).reshape(DEPTH, 6, 1, D_MODEL)

    def sent_of(key):
        n, l = key
        arr = W[n] if l is None else W[n][l]
        return arr.astype(BF16) if n in BIG or n in ('ffn_w1', 'ffn_w2') else arr

    keys0 = [(n, 0) for n in MLA_MATS] + [(n, None) for n in ('norm_g', 'mla_q_norm_g', 'mla_kv_norm_g',
                                                               'conv_w_dw', 'pool_b', 'pool_scale')]
    keys1 = [('ffn_w1', 0), ('ffn_w2', 0), ('conv_w_pw1', None), ('conv_w_pw2', None), ('pool_w', None)]
    keys2 = [(n, l) for l in range(1, DEPTH) for n in ('ffn_w1', 'ffn_w2')] + [(n, 1) for n in MLA_MATS]
    G = dict(zip(keys0, _exchange("gather_w0", [sent_of(k) for k in keys0], 'xy')))
    fly1 = _split_start("gather_w1_start", [sent_of(k) for k in keys1])
    fly2 = _split_start("gather_w2_start", [sent_of(k) for k in keys2])
    mod = mod + (fly1['token'][0, 0] + fly2['token'][0, 0])

    def whole(key):
        n, l = key
        return _unshard(G[key], SHARD_AXIS[n] - (0 if l is None else 1))

    def mla_params(j):
        P = _mla_weights(*[whole((n, j)) for n in ('mla_w_dq', 'mla_w_dkv', 'mla_w_uq', 'mla_w_ukv', 'mla_w_o')])
        P.update(qg=_row(whole(('mla_q_norm_g', None))[j]), kg=_row(whole(('mla_kv_norm_g', None))[j]))
        return P

    gains = whole(('norm_g', None))
    mla_p = {0: mla_params(0)}
    conv_p = pool_p = None
    rope = _rope_tables(positions.reshape(S, 1).astype(F32))

    by_j = pl.BlockSpec((None, 1024, 1024), lambda i, j, k: (j, 0, 0))
    by_k = pl.BlockSpec((None, 1024, 1024), lambda i, j, k: (k, 0, 0))
    sq_relu = lambda v: jnp.square(jnp.maximum(v, 0.0))

    def md(i, k):
        return mod[i, k]

    (h,), _ = _rowk("pre0", lambda xv, g, sc, sh: ((_pre_fwd(xv, g, sc, sh),), ()),
                    [xs], [_row(gains[0, 0]), md(0, 1), md(0, 0)], [(D_MODEL, BF16)], [])
    saved = []
    xin = xs
    loss_acc = dxf = None
    for i in range(DEPTH):
        kind, j = i % 3, i // 3
        if kind == 0:
            if j not in mla_p:
                mla_p[j] = mla_params(j)
            y, sv = _mla_fwd(j, h, mla_p[j], rope)
        elif kind == 1:
            y, sv = _conv_fwd(h, conv_p)
        else:
            y, sv = _pool_mixer_fwd(h, pool_p)

        def mid(xv, yv, gt, g1, g2, sc, sh):
            x1 = _post_fwd(xv, yv, gt, g1)
            return (x1, _pre_fwd(x1, g2, sc, sh)), ()

        (x1, h2), _ = _rowk(f"mid{i}", mid, [xin, y], [md(i, 2), _row(gains[i, 1]), _row(gains[i, 2]), md(i, 4), md(i, 3)],
                            [(D_MODEL, F32), (D_MODEL, BF16)], [])
        if i == 0:
            G.update(zip(keys1, _split_wait("gather_w1_wait", fly1, h2)))
            conv_p = dict(w_pw1=whole(('conv_w_pw1', None))[0], b_pw1=_row(conv_b_pw1[0]),
                          w_dw=whole(('conv_w_dw', None))[0], b_dw=_row(conv_b_dw[0]), ln_g=_row(conv_ln_g[0]),
                          ln_b=_row(conv_ln_b[0]), w_pw2=whole(('conv_w_pw2', None))[0], b_pw2=_row(conv_b_pw2[0]))
            pool_p = dict(w=whole(('pool_w', None))[0], b=_row(whole(('pool_b', None))[0]),
                          scale=_row(whole(('pool_scale', None))[0]))
        if i == 1:
            G.update(zip(keys2, _split_wait("gather_w2_wait", fly2, h2)))
        a = _mm(f"ffn1_{i}", h2, G[('ffn_w1', i)], 'nn', S, D_FF, D_MODEL, b_spec=by_j)
        y2 = _mm(f"ffn2_{i}", a, G[('ffn_w2', i)], 'nn', S, D_MODEL, D_FF, pro_a=sq_relu, b_spec=by_k)
        saved.append(dict(x0=xin, h=h, y=y, x1=x1, h2=h2, a=a, y2=y2, mix=sv))
        if i + 1 < DEPTH:
            def nxt(xv, yv, gt, g3, g0, sc, sh):
                x2 = _post_fwd(xv, yv, gt, g3)
                return (x2, _pre_fwd(x2, g0, sc, sh)), ()

            hdt = F32 if (i + 1) % 3 == 2 else BF16
            (xin, h), _ = _rowk(f"next{i}", nxt, [x1, y2],
                                [md(i, 5), _row(gains[i, 3]), _row(gains[i + 1, 0]), md(i + 1, 1), md(i + 1, 0)],
                                [(D_MODEL, F32), (D_MODEL, hdt)], [])
        else:
            def head(xv, yv, tv, gt, g3):
                err = _post_fwd(xv, yv, gt, g3) - tv
                per_row = jnp.sum(err * err, axis=1, keepdims=True) * (0.5 / D_MODEL)
                return (err * (1.0 / D_MODEL),), (jnp.broadcast_to(jnp.sum(per_row, axis=0, keepdims=True), (1, 128)),)

            (dxf,), (loss_acc,) = _rowk("loss_head", head, [x1, y2, tgt], [md(i, 5), _row(gains[i, 3])],
                                        [(D_MODEL, F32)], [128])
    loss = lax.psum(loss_acc[0, 0], ("x", "y", "c"))

    small = {}
    big = {}
    landed = {}

    def keep(gm, layer):
        for n, g in gm.items():
            if n in BIG:
                g = g[None] if layer is None else g
                big[(n, layer)] = _to_shards(g, SHARD_AXIS[n] - (0 if layer is None else 1)).astype(BF16)
            else:
                small.setdefault(n, {})[layer or 0] = g

    d_mod = [None] * DEPTH
    d_gain = [None] * DEPTH
    dx = dxf
    for i in reversed(range(DEPTH)):
        kind, j = i % 3, i // 3
        sv = saved[i]
        def post2_bwd(d, yv, gt, g):
            dyv, d_gt, d_g = _post_bwd(d, yv, gt, g)
            return (dyv,), (d_gt, d_g)

        (dy2,), (d_gtf, d_g3) = _rowk(f"post2_bwd{i}", post2_bwd, [dx, sv['y2']], [md(i, 5), _row(gains[i, 3])],
                                      [(D_MODEL, BF16)], [D_MODEL] * 2)
        da = _mm(f"ffn2_dg{i}", dy2, G[('ffn_w2', i)], 'nt', S, D_FF, D_MODEL, b_spec=by_j, extras=[sv['a']],
                 extra_specs=[pl.BlockSpec((min(1024, S), 1024), lambda i_, j_, k_: (i_, j_))],
                 epi=lambda acc, av: (acc * (2.0 * jnp.maximum(av, 0.0)),),
                 outs=[jax.ShapeDtypeStruct((S, D_FF), BF16)])
        big[('ffn_w2', i)] = _mm(f"ffn2_wg{i}", sv['a'], dy2, 'tn', D_FF, D_MODEL, S, pro_a=sq_relu,
                        outs=[jax.ShapeDtypeStruct((4, 1024, D_MODEL), BF16)],
                        out_specs=[pl.BlockSpec((None, 1024, 1024), lambda i_, j_, k_: (i_, 0, j_))])
        big[('ffn_w1', i)] = _mm(f"ffn1_wg{i}", sv['h2'], da, 'tn', D_MODEL, D_FF, S,
                        outs=[jax.ShapeDtypeStruct((4, D_MODEL, 1024), BF16)],
                        out_specs=[pl.BlockSpec((None, 1024, 1024), lambda i_, j_, k_: (j_, i_, 0))])
        dh2 = _mm(f"ffn1_dg{i}", da, G[('ffn_w1', i)], 'nt', S, D_MODEL, D_FF, b_spec=by_k)
        if i == DEPTH - 1:
            keys_a = [('ffn_w1', i), ('ffn_w2', i)]
            fly_a = _split_start("scatter_ga_start", [big[k] for k in keys_a], src_by='xy')
            mod = mod + fly_a['token'][0, 0]
        if i == 0:
            keys_b = [k for k in big if k not in keys_a]
            fly_b = _split_start("scatter_gb_start", [big[k] for k in keys_b], src_by='xy')
            mod = mod + fly_b['token'][0, 0]

        def mid_bwd(d2, dh2v, x1v, yv, g2, scf, gtm, g1):
            dpre, d_sh, d_sc, d_g2 = _pre_bwd(dh2v, x1v, g2, scf)
            d1 = d2 + dpre
            dyv, d_gt, d_g1 = _post_bwd(d1, yv, gtm, g1)
            return (d1, dyv), (d_sh, d_sc, d_g2, d_gt, d_g1, _colsum(dyv))

        ydt = F32 if kind == 2 else BF16
        (dx1, dy), (d_shf, d_scf, d_g2, d_gtm, d_g1, dy_cs) = _rowk(
            f"mid_bwd{i}", mid_bwd, [dx, dh2, sv['x1'], sv['y']],
            [_row(gains[i, 2]), md(i, 4), md(i, 2), _row(gains[i, 1])],
            [(D_MODEL, F32), (D_MODEL, ydt)], [D_MODEL] * 6)
        if kind == 0:
            dh, gm = _mla_bwd(j, dy, sv['h'], sv['mix'], mla_p[j], rope)
            keep(gm, j)
        elif kind == 1:
            dh, gm = _conv_bwd(dy, dy_cs, sv['h'], sv['mix'], conv_p)
            keep(gm, None)
        else:
            dh, gm = _pool_mixer_bwd(dy, sv['mix'], pool_p)
            keep(gm, None)

        def pre_bwd(d1, dhv, x0v, g0, scm):
            dpre, d_sh, d_sc, d_g0 = _pre_bwd(dhv, x0v, g0, scm)
            return (d1 + dpre,), (d_sh, d_sc, d_g0)

        (dx,), (d_shm, d_scm, d_g0) = _rowk(f"pre_bwd{i}", pre_bwd, [dx1, dh, sv['x0']],
                                            [_row(gains[i, 0]), md(i, 1)], [(D_MODEL, F32)], [D_MODEL] * 3)
        d_mod[i] = jnp.concatenate([d_shm, d_scm, d_gtm, d_shf, d_scf, d_gtf], axis=1).reshape(-1)
        d_gain[i] = jnp.concatenate([d_g0, d_g1, d_g2, d_g3], axis=0)
        if i == DEPTH - 1:
            landed.update(zip(keys_a, _split_wait("scatter_ga_wait", fly_a, dx)))
    landed.update(zip(keys_b, _split_wait("scatter_gb_wait", fly_b, dx)))
    keys_c = [(n, 0) for n in MLA_MATS]
    landed.update(zip(keys_c, _exchange("scatter_gc", [big[k] for k in keys_c], 'xy', src_by='xy')))
    grad_x = dx.reshape(x.shape)
    grads = {n: jnp.stack([g[l] for l in sorted(g)]) for n, g in small.items()}
    grads['norm_g'] = jnp.stack(d_gain)
    grads['ada_b'] = jnp.stack(d_mod)

    pack = jnp.concatenate([grads[n].reshape(-1) for n in SMALL])
    n_pack = pack.shape[0]
    rows = -(-n_pack // 1024) * 8
    pack = jnp.pad(pack, (0, rows * 128 - n_pack)).reshape(rows, 128)
    pack8 = _exchange("gather_small", [pack], 'xyc')[0]
    (tot,) = _ew("sum_small", lambda *v: (functools.reduce(lambda p, q: p + q, v),), [(pack8, s) for s in range(8)],
                 [F32], (rows, 128))
    tot = tot.reshape(-1)
    d_mod_all = pack8.reshape(8, -1)[:, :DEPTH * 6 * D_MODEL].reshape(8, DEPTH, 6 * D_MODEL)
    final = {}
    off = 0
    for n in SMALL:
        ax = SHARD_AXIS[n]
        shape = tuple(d * 4 if k == ax else d for k, d in enumerate(W[n].shape))
        size = grads[n].size
        g = tot[off:off + size].reshape(shape)
        off += size
        if ax is not None:
            g = lax.dynamic_index_in_dim(_to_shards(g, ax), chip, 0, keepdims=False)
        final[n] = g

    g_ada = []
    for l in range(DEPTH):
        dm_l = jnp.pad(lax.dynamic_slice(d_mod_all[:, l], (0, chip * n_sh), (8, n_sh)), ((0, ADA_ROWS - 8), (0, 0)))
        g_ada.append(_mm(f"ada_wg{l}", c8, dm_l, 'tn', D_MODEL, n_sh, ADA_ROWS, tn=n_sh // 2, pro_a=silu))
    final['ada_w'] = jnp.stack(g_ada)

    out_g, out_d, out_m, out_v = {}, {}, {}, {}
    for n in WEIGHTS:
        shape = W[n].shape
        if n in ('ffn_w1', 'ffn_w2') or n in MLA_MATS:
            res = None
            for l in range(shape[0]):
                res = _finish(f"finish_{n}{l}", W[n], landed[(n, l)], MOM[n], VAR[n], layer=l, prev=res)
            out_g[n], out_d[n], out_m[n], out_v[n] = res
        elif n in BIG:
            out_g[n], out_d[n], out_m[n], out_v[n] = _finish(f"finish_{n}", W[n], landed[(n, None)], MOM[n], VAR[n])
        else:
            out_g[n] = final[n].reshape(shape)
            out_d[n], out_m[n], out_v[n] = _ew(f"adamw_{n}", lambda w, g, m, v: _adamw(w, g, m, v),
                                               [W[n], out_g[n], MOM[n], VAR[n]], [F32] * 3, shape)
    return (loss, grad_x, *[out_g[n] for n in WEIGHTS], *[out_d[n] for n in WEIGHTS],
            *[out_m[n] for n in WEIGHTS], *[out_v[n] for n in WEIGHTS])
```

```python
import functools
import math

import jax
import jax.numpy as jnp
from jax import lax
from jax.experimental import pallas as pl
from jax.experimental.pallas import tpu as pltpu

F32 = jnp.float32
BF16 = jnp.bfloat16

D_MODEL = 1024
DEPTH = 4
N_HEADS = 16
QK_NOPE = 64
QK_ROPE = 32
V_HEAD = 64
Q_LORA = 384
KV_LORA = 256
HEAD_PAD = 128
QW = N_HEADS * HEAD_PAD
KVW = 2 * QW
DKV = KV_LORA + QK_ROPE
DQKV = Q_LORA + DKV
D_FF = 4096
CONV_WIDTH = 31
POOL_WINDOWS = (2, 4, 8, 16)
CHUNK_SHIFT = 6
ROPE_THETA = 10000.0
NORM_EPS = 1e-6
NEG_INF = -1e30
ATT_SCALE = 1.0 / math.sqrt(QK_NOPE + QK_ROPE)
BQ = 256
HB = 4
LOG2E = 1.4426950408889634
SCALE_LOG2E = ATT_SCALE * LOG2E
PAD_ROWS = 32
ADA_ROWS = 128
VMEM_LIMIT = 56 * 1024 * 1024

ADAM_LR = 0.001
ADAM_B1 = 0.9
ADAM_B2 = 0.999
ADAM_EPS = 1e-08
ADAM_WD = 0.01
ADAM_STEP = 10

WEIGHTS = ['ada_w', 'ada_b', 'norm_g', 'mla_w_dq', 'mla_q_norm_g', 'mla_w_uq', 'mla_w_dkv', 'mla_kv_norm_g',
           'mla_w_ukv', 'mla_w_o', 'conv_w_pw1', 'conv_b_pw1', 'conv_w_dw', 'conv_b_dw', 'conv_ln_g', 'conv_ln_b',
           'conv_w_pw2', 'conv_b_pw2', 'pool_w', 'pool_b', 'pool_scale', 'ffn_w1', 'ffn_w2']
SHARD_AXIS = {'ada_w': 2, 'ada_b': None, 'norm_g': 2, 'mla_w_dq': 1, 'mla_q_norm_g': 1, 'mla_w_uq': 2,
              'mla_w_dkv': 1, 'mla_kv_norm_g': 1, 'mla_w_ukv': 2, 'mla_w_o': 1, 'conv_w_pw1': 2,
              'conv_b_pw1': None, 'conv_w_dw': 2, 'conv_b_dw': None, 'conv_ln_g': None, 'conv_ln_b': None,
              'conv_w_pw2': 1, 'conv_b_pw2': None, 'pool_w': 2, 'pool_b': 2, 'pool_scale': 1,
              'ffn_w1': 2, 'ffn_w2': 1}
MLA_MATS = ['mla_w_dq', 'mla_w_uq', 'mla_w_dkv', 'mla_w_ukv', 'mla_w_o']
BIG = MLA_MATS + ['conv_w_pw1', 'conv_w_pw2', 'pool_w']
SMALL = ['ada_b', 'norm_g', 'mla_q_norm_g', 'mla_kv_norm_g', 'conv_b_pw1', 'conv_w_dw', 'conv_b_dw',
         'conv_ln_g', 'conv_ln_b', 'conv_b_pw2', 'pool_b', 'pool_scale']


def _cparams(*sem):
    return pltpu.CompilerParams(dimension_semantics=sem, vmem_limit_bytes=VMEM_LIMIT)


def _colsum(v):
    return jnp.sum(v, axis=0, keepdims=True)


def _rowmean(v):
    return jnp.mean(v, axis=-1, keepdims=True)


def _sigmoid(v):
    return 1.0 / (1.0 + jnp.exp(-v))


def _rowk(name, fn, rows, bcast, out_row, out_acc, tm=256):
    S = rows[0].shape[0]
    tm = min(tm, S)
    assert S % tm == 0
    nin, no, na = len(rows) + len(bcast), len(out_row), len(out_acc)

    def body(*refs):
        vals = [r[...] for r in refs[:nin]]
        outs = refs[nin:nin + no]
        accs = refs[nin + no:]
        ro, ao = fn(*vals)
        for r, v in zip(outs, ro):
            r[...] = v.astype(r.dtype)
        if na:
            @pl.when(pl.program_id(0) == 0)
            def _():
                for r in accs:
                    r[...] = jnp.zeros(r.shape, r.dtype)
            for r, v in zip(accs, ao):
                r[...] += v

    in_specs = [pl.BlockSpec((tm, a.shape[1]), lambda i: (i, 0)) for a in rows]
    in_specs += [pl.BlockSpec(b.shape, lambda i, n=b.ndim: (0,) * n) for b in bcast]
    out_shape = [jax.ShapeDtypeStruct((S, w), dt) for w, dt in out_row]
    out_shape += [jax.ShapeDtypeStruct((1, w), F32) for w in out_acc]
    out_specs = [pl.BlockSpec((tm, w), lambda i: (i, 0)) for w, _ in out_row]
    out_specs += [pl.BlockSpec((1, w), lambda i: (0, 0)) for w in out_acc]
    res = pl.pallas_call(body, name=name, grid=(S // tm,), in_specs=in_specs, out_specs=out_specs,
                         out_shape=out_shape, compiler_params=_cparams("arbitrary"))(*rows, *bcast)
    return list(res[:no]), list(res[no:])


_DIMS = {'nn': ((1,), (0,)), 'nt': ((1,), (1,)), 'tn': ((0,), (0,))}


def _mm(name, a, b, mode, M, N, K, *, tm=1024, tn=1024, tk=1024, a_spec=None, b_spec=None, pro_a=None,
        extras=(), extra_specs=(), epi=None, outs=None, out_specs=None):
    tm, tn, tk = (t if d % t == 0 else d for t, d in ((min(tm, M), M), (min(tn, N), N), (min(tk, K), K)))
    nk = K // tk
    if a_spec is None:
        a_spec = (pl.BlockSpec((tk, tm), lambda i, j, k: (k, i)) if mode == 'tn'
                  else pl.BlockSpec((tm, tk), lambda i, j, k: (i, k)))
    if b_spec is None:
        b_spec = (pl.BlockSpec((tn, tk), lambda i, j, k: (j, k)) if mode == 'nt'
                  else pl.BlockSpec((tk, tn), lambda i, j, k: (k, j)))
    if outs is None:
        outs = [jax.ShapeDtypeStruct((M, N), F32)]
    if out_specs is None:
        out_specs = [pl.BlockSpec((tm, tn), lambda i, j, k: (i, j)) for _ in outs]
    ne, no = len(extras), len(outs)
    dims = (_DIMS[mode], ((), ()))

    def body(a_ref, b_ref, *rest):
        ex, out_refs = rest[:ne], rest[ne:ne + no]
        av = a_ref[...]
        if pro_a is not None:
            av = pro_a(av)
        part = lax.dot_general(av.astype(BF16), b_ref[...].astype(BF16), dims, preferred_element_type=F32)

        def finish(acc):
            vals = (acc,) if epi is None else epi(acc, *[e[...] for e in ex])
            for r, v in zip(out_refs, vals):
                r[...] = v.astype(r.dtype)

        if nk == 1:
            finish(part)
            return
        acc_ref = rest[ne + no]
        k = pl.program_id(2)

        @pl.when(k == 0)
        def _():
            acc_ref[...] = part

        @pl.when(k > 0)
        def _():
            acc_ref[...] += part

        @pl.when(k == nk - 1)
        def _():
            finish(acc_ref[...])

    res = pl.pallas_call(
        body, name=name, grid=(M // tm, N // tn, nk),
        in_specs=[a_spec, b_spec, *extra_specs], out_specs=list(out_specs), out_shape=list(outs),
        scratch_shapes=[pltpu.VMEM((tm, tn), F32)] if nk > 1 else [],
        compiler_params=_cparams("parallel", "parallel", "arbitrary"))(a, b, *extras)
    return res[0] if no == 1 else list(res)


def _row_tile(R, C, itemsize=4, budget=1 << 20):
    if R * C * itemsize <= budget or R % 8:
        return R
    t = 8
    while R % (t * 2) == 0 and t * 2 * C * itemsize <= budget:
        t *= 2
    return t


def _ew(name, fn, ins, out_dtypes, shape):
    C = shape[-1]
    R = 1
    for s in shape[:-1]:
        R *= s
    tr = _row_tile(R, C)
    ops, specs = [], []
    for it in ins:
        if isinstance(it, tuple):
            arr, idx = it
            ops.append(arr.reshape(arr.shape[0], R, C))
            specs.append(pl.BlockSpec((None, tr, C), lambda i, n=idx: (n, i, 0)))
        else:
            ops.append(it.reshape(R, C))
            specs.append(pl.BlockSpec((tr, C), lambda i: (i, 0)))
    nin = len(ops)

    def body(*refs):
        vals = fn(*[r[...] for r in refs[:nin]])
        for r, v in zip(refs[nin:], vals):
            r[...] = v.astype(r.dtype)

    res = pl.pallas_call(
        body, name=name, grid=(R // tr,), in_specs=specs,
        out_specs=[pl.BlockSpec((tr, C), lambda i: (i, 0)) for _ in out_dtypes],
        out_shape=[jax.ShapeDtypeStruct((R, C), dt) for dt in out_dtypes],
        compiler_params=_cparams("parallel"))(*ops)
    return [r.reshape(shape) for r in res]


_FLIPS = {'xyc': [(fx, fy, fc) for fx in (0, 1) for fy in (0, 1) for fc in (0, 1)][1:],
          'xy': [(1, 0, 0), (0, 1, 0), (1, 1, 0)],
          'c': [(0, 0, 1)]}
_NSLOT = {'xyc': 8, 'xy': 4, 'c': 2}


def _slot(kind, cx, cy, cc):
    return {'xyc': 4 * cx + 2 * cy + cc, 'xy': 2 * cx + cy, 'c': cc}[kind]


def _put_own(land, arr, group, src_by):
    coords = (lax.axis_index("x"), lax.axis_index("y"), lax.axis_index("c"))
    pay = arr if src_by is None else lax.dynamic_index_in_dim(arr, _slot(src_by, *coords), 0, keepdims=False)
    return lax.dynamic_update_index_in_dim(land, pay, _slot(group, *coords), 0)


def _exchange(name, arrays, group, src_by=None):
    flips, nsl, n = _FLIPS[group], _NSLOT[group], len(arrays)
    nf = len(flips)

    def body(*refs):
        ins, outs = refs[:n], refs[n:2 * n]
        send_sems, recv_sems = refs[2 * n:]
        mx, my, mc = lax.axis_index("x"), lax.axis_index("y"), lax.axis_index("c")
        me = _slot(group, mx, my, mc)

        def payload(a, cx, cy, cc):
            return ins[a] if src_by is None else ins[a].at[_slot(src_by, cx, cy, cc)]

        sends, recvs = [], []
        for a in range(n):
            for f, (fx, fy, fc) in enumerate(flips):
                px = 1 - mx if fx else mx
                py = 1 - my if fy else my
                pc = 1 - mc if fc else mc
                src = payload(a, px, py, pc)
                sends.append(pltpu.make_async_remote_copy(
                    src_ref=src, dst_ref=outs[a].at[me], send_sem=send_sems.at[a, f],
                    recv_sem=recv_sems.at[a, f], device_id=(px, py, pc),
                    device_id_type=pl.DeviceIdType.MESH))
                recvs.append(pltpu.make_async_remote_copy(
                    src_ref=src, dst_ref=outs[a].at[_slot(group, px, py, pc)], send_sem=send_sems.at[a, f],
                    recv_sem=recv_sems.at[a, f], device_id=(px, py, pc),
                    device_id_type=pl.DeviceIdType.MESH))
        for cp in sends:
            cp.start()
        for cp in recvs:
            cp.wait_recv()
        for cp in sends:
            cp.wait_send()

    out_shape = [jax.ShapeDtypeStruct((nsl,) + (a.shape if src_by is None else a.shape[1:]), a.dtype)
                 for a in arrays]
    any_spec = pl.BlockSpec(memory_space=pl.ANY)
    res = pl.pallas_call(
        body, name=name, in_specs=[any_spec] * n, out_specs=[any_spec] * n, out_shape=out_shape,
        scratch_shapes=[pltpu.SemaphoreType.DMA((n, nf)), pltpu.SemaphoreType.DMA((n, nf))],
        compiler_params=pltpu.CompilerParams(has_side_effects=True))(*arrays)
    return [_put_own(l, a, group, src_by) for a, l in zip(arrays, res)]


_HBM = pl.BlockSpec(memory_space=pltpu.HBM)
_SEM = pl.BlockSpec(memory_space=pltpu.SEMAPHORE)
_DATAFLOW = pltpu.SideEffectType.DATAFLOW_SIDE_EFFECTING


def _xy_copies(ins, lands, send_sems, recv_sems, src_by):
    mx, my, mc = lax.axis_index("x"), lax.axis_index("y"), lax.axis_index("c")
    me = _slot('xy', mx, my, mc)
    pairs = []
    for a in range(len(ins)):
        for f, (fx, fy, _) in enumerate(_FLIPS['xy']):
            px = 1 - mx if fx else mx
            py = 1 - my if fy else my
            src = ins[a] if src_by is None else ins[a].at[_slot(src_by, px, py, mc)]
            mk = functools.partial(pltpu.make_async_remote_copy, src_ref=src, send_sem=send_sems,
                                   recv_sem=recv_sems, device_id=(px, py, mc),
                                   device_id_type=pl.DeviceIdType.MESH)
            pairs.append((mk(dst_ref=lands[a].at[me]), mk(dst_ref=lands[a].at[_slot('xy', px, py, mc)])))
    return pairs


def _split_start(name, arrays, src_by=None):
    n = len(arrays)
    lands = [lax.empty((4,) + (a.shape if src_by is None else a.shape[1:]), a.dtype) for a in arrays]

    def body(*refs):
        ins, lnd, send_sems, recv_sems, token = refs[:n], refs[n:2 * n], refs[2 * n], refs[2 * n + 1], refs[-1]
        for to_peer, _ in _xy_copies(ins, lnd, send_sems, recv_sems, src_by):
            to_peer.start()
        token[...] = jnp.zeros(token.shape, F32)

    ops = [pltpu.with_memory_space_constraint(a, pltpu.HBM) for a in [*arrays, *lands]]
    res = pl.pallas_call(
        body, name=name, in_specs=[_HBM] * (2 * n),
        out_specs=[_SEM, _SEM] + [_HBM] * (2 * n) + [pl.BlockSpec(memory_space=pltpu.VMEM)],
        out_shape=[pltpu.SemaphoreType.DMA(()), pltpu.SemaphoreType.DMA(())]
        + [pltpu.HBM(a.shape, a.dtype) for a in ops] + [jax.ShapeDtypeStruct((8, 128), F32)],
        input_output_aliases={k: 2 + k for k in range(2 * n)},
        compiler_params=pltpu.CompilerParams(has_side_effects=_DATAFLOW))(*ops)
    return dict(n=n, src_by=src_by, send=res[0], recv=res[1], arrays=list(res[2:2 + n]),
                lands=list(res[2 + n:2 + 2 * n]), token=res[-1])


def _split_wait(name, st, after):
    n, src_by = st['n'], st['src_by']

    def wait_body(*refs):
        ins, lnd, send_sems, recv_sems = refs[:n], refs[n:2 * n], refs[2 * n], refs[2 * n + 1]
        for to_peer, from_peer in _xy_copies(ins, lnd, send_sems, recv_sems, src_by):
            to_peer.wait_send()
            from_peer.wait_recv()

    shapes = [pltpu.HBM(a.shape, a.dtype) for a in [*st['arrays'], *st['lands']]]
    res = pl.pallas_call(
        wait_body, name=name, in_specs=[_HBM] * (2 * n) + [_SEM, _SEM, pl.BlockSpec(memory_space=pl.ANY)],
        out_specs=[_HBM] * (2 * n), out_shape=shapes, input_output_aliases={k: k for k in range(2 * n)},
        compiler_params=pltpu.CompilerParams(has_side_effects=_DATAFLOW))(
            *st['arrays'], *st['lands'], st['send'], st['recv'], after)
    arrays, lands = res[:n], res[n:]

    return [_put_own(l, a, 'xy', src_by) for a, l in zip(arrays, lands)]


def _unshard(g, axis):
    t = jnp.moveaxis(g, 0, axis)
    s = t.shape
    return t.reshape(s[:axis] + (s[axis] * s[axis + 1],) + s[axis + 2:])


def _to_shards(w, axis):
    s = w.shape
    t = w.reshape(s[:axis] + (4, s[axis] // 4) + s[axis + 1:])
    return jnp.moveaxis(t, axis, 0)


def _pre_fwd(x, g, sc, sh):
    r = lax.rsqrt(_rowmean(x * x) + NORM_EPS)
    return (x * r) * g * (1.0 + sc) + sh


def _pre_bwd(dh, x, g, sc):
    r = lax.rsqrt(_rowmean(x * x) + NORM_EPS)
    xn = x * r
    dxn = dh * (g * (1.0 + sc))
    dx = r * (dxn - xn * _rowmean(dxn * xn))
    t = dh * xn
    return dx, _colsum(dh), _colsum(t * g), _colsum(t * (1.0 + sc))


def _post_fwd(x, y, gt, g):
    r = lax.rsqrt(_rowmean(y * y) + NORM_EPS)
    return x + gt * ((y * r) * g)


def _post_bwd(dxo, y, gt, g):
    r = lax.rsqrt(_rowmean(y * y) + NORM_EPS)
    yn = y * r
    t = dxo * yn
    dyn = dxo * (gt * g)
    dy = r * (dyn - yn * _rowmean(dyn * yn))
    return dy, _colsum(t * g), _colsum(t * gt)


def _gain_bwd(dy, x, g):
    r = lax.rsqrt(_rowmean(x * x) + NORM_EPS)
    xn = x * r
    dxn = dy * g
    return r * (dxn - xn * _rowmean(dxn * xn)), _colsum(dy * xn)


def _rope(x, cos, sa, sb):
    return x * cos + pltpu.roll(x, HEAD_PAD - 16, 1) * sa + pltpu.roll(x, 16, 1) * sb


def _rope_t(d, cos, sa, sb):
    return d * cos + pltpu.roll(d * sa, 16, 1) + pltpu.roll(d * sb, HEAD_PAD - 16, 1)


def _rope_tables(pos_f):
    S = pos_f.shape[0]
    inv = ROPE_THETA ** (-jnp.arange(0, QK_ROPE, 2, dtype=F32) / QK_ROPE)
    inv_ext = jnp.concatenate([jnp.zeros((QK_NOPE,), F32), inv, inv,
                               jnp.zeros((HEAD_PAD - QK_NOPE - QK_ROPE,), F32)]).reshape(1, HEAD_PAD)

    def fn(p, iv):
        ang = p * iv
        lane = lax.broadcasted_iota(jnp.int32, ang.shape, 1)
        s = jnp.sin(ang)
        first = (lane >= QK_NOPE) & (lane < QK_NOPE + QK_ROPE // 2)
        second = (lane >= QK_NOPE + QK_ROPE // 2) & (lane < QK_NOPE + QK_ROPE)
        return (jnp.cos(ang), jnp.where(first, -s, 0.0), jnp.where(second, s, 0.0)), ()

    (cos, sa, sb), _ = _rowk("rope_tables", fn, [pos_f], [inv_ext], [(HEAD_PAD, F32)] * 3, [])
    return cos, sa, sb


def _diag_mask(transposed):
    r = lax.broadcasted_iota(jnp.int32, (BQ, BQ), 0) >> CHUNK_SHIFT
    c = lax.broadcasted_iota(jnp.int32, (BQ, BQ), 1) >> CHUNK_SHIFT
    return (r <= c) if transposed else (c <= r)


_NT = (((1,), (1,)), ((), ()))
_NN = (((1,), (0,)), ((), ()))


def _attn_fwd(qf, kvf):
    S = qf.shape[0]
    nq = S // BQ

    def body(q_ref, kv_ref, o_ref, lse_ref):
        qi = pl.program_id(1)
        qs = [q_ref[:, hh * HEAD_PAD:(hh + 1) * HEAD_PAD] for hh in range(HB)]

        def step(j, carry, diag):
            off = pl.multiple_of(j * BQ, BQ)
            sts = [lax.dot_general(kv_ref[pl.ds(off, BQ), pl.ds(2 * hh * HEAD_PAD, HEAD_PAD)], qs[hh], _NT,
                                   preferred_element_type=F32) for hh in range(HB)]
            mid = []
            for hh in range(HB):
                m, l, acc = carry[hh]
                st = jnp.where(_diag_mask(True), sts[hh], NEG_INF) if diag else sts[hh]
                m2 = jnp.maximum(m, jnp.max(st, axis=0, keepdims=True))
                al = jnp.exp2((m - m2) * SCALE_LOG2E)
                pt = jnp.exp2((st - m2) * SCALE_LOG2E)
                mid.append((m2, l * al + jnp.sum(pt, axis=0, keepdims=True), acc * al, pt.astype(BF16)))
            out = []
            for hh in range(HB):
                m2, l2, acc_s, ptb = mid[hh]
                v = kv_ref[pl.ds(off, BQ), pl.ds((2 * hh + 1) * HEAD_PAD, HEAD_PAD)]
                out.append((m2, l2, acc_s + lax.dot_general(v, ptb, _TN, preferred_element_type=F32)))
            return tuple(out)

        init = tuple((jnp.full((1, BQ), NEG_INF, F32), jnp.zeros((1, BQ), F32), jnp.zeros((HEAD_PAD, BQ), F32))
                     for _ in range(HB))
        carry = lax.fori_loop(0, qi, lambda j, c: step(j, c, False), init)
        carry = step(qi, carry, True)
        for hh in range(HB):
            m, l, acc = carry[hh]
            o_ref[:, hh * HEAD_PAD:(hh + 1) * HEAD_PAD] = (acc / l).T
            lse_ref[hh] = m * SCALE_LOG2E + jnp.log(l) * LOG2E

    return pl.pallas_call(
        body, name="attn_fwd", grid=(N_HEADS // HB, nq),
        in_specs=[pl.BlockSpec((BQ, HB * HEAD_PAD), lambda g, i: (i, g)),
                  pl.BlockSpec((S, 2 * HB * HEAD_PAD), lambda g, i: (0, g))],
        out_specs=[pl.BlockSpec((BQ, HB * HEAD_PAD), lambda g, i: (i, g)),
                   pl.BlockSpec((HB, None, 1, BQ), lambda g, i: (g, i, 0, 0))],
        out_shape=[jax.ShapeDtypeStruct((S, QW), F32), jax.ShapeDtypeStruct((N_HEADS, nq, 1, BQ), F32)],
        compiler_params=_cparams("parallel", "arbitrary"))(qf, kvf)


def _attn_delta(dob, o):
    S = o.shape[0]

    def body(do_ref, o_ref, dd_ref):
        dd_ref[...] = jnp.sum(do_ref[...].astype(F32) * o_ref[...], axis=1, keepdims=True)

    blk = pl.BlockSpec((S, HEAD_PAD), lambda h: (0, h))
    return pl.pallas_call(
        body, name="attn_delta", grid=(N_HEADS,), in_specs=[blk, blk],
        out_specs=pl.BlockSpec((None, S, 1), lambda h: (h, 0, 0)),
        out_shape=jax.ShapeDtypeStruct((N_HEADS, S, 1), F32),
        compiler_params=_cparams("parallel"))(dob, o)


_TN = (((0,), (0,)), ((), ()))


def _attn_bwd(qf, kvf, dob, lse_row, dd_row, cos, sa, sb):
    S = qf.shape[0]
    nq = S // BQ

    def body(kv_ref, q_ref, do_ref, lse_ref, dd_ref, cos_ref, sa_ref, sb_ref, dq_ref, dkv_ref):
        kj = pl.program_id(1)

        @pl.when(kj == 0)
        def _():
            dq_ref[...] = jnp.zeros(dq_ref.shape, F32)

        ks = [kv_ref[:, 2 * hh * HEAD_PAD:(2 * hh + 1) * HEAD_PAD] for hh in range(HB)]
        vs = [kv_ref[:, (2 * hh + 1) * HEAD_PAD:(2 * hh + 2) * HEAD_PAD] for hh in range(HB)]

        def step(i, carry, diag):
            off = pl.multiple_of(i * BQ, BQ)
            cols = [pl.ds(hh * HEAD_PAD, HEAD_PAD) for hh in range(HB)]
            q = [q_ref[pl.ds(off, BQ), cols[hh]] for hh in range(HB)]
            do = [do_ref[pl.ds(off, BQ), cols[hh]] for hh in range(HB)]
            sts = [lax.dot_general(ks[hh], q[hh], _NT, preferred_element_type=F32) for hh in range(HB)]
            dpts = [lax.dot_general(vs[hh], do[hh], _NT, preferred_element_type=F32) for hh in range(HB)]
            mid = []
            for hh in range(HB):
                st = jnp.where(_diag_mask(True), sts[hh], NEG_INF) if diag else sts[hh]
                pt = jnp.exp2(st * SCALE_LOG2E - lse_ref[hh, i])
                mid.append((pt.astype(BF16), (pt * (dpts[hh] - dd_ref[hh, i])).astype(BF16)))
            out = []
            for hh in range(HB):
                dk, dv = carry[hh]
                ptb, dsb = mid[hh]
                dv2 = dv + lax.dot_general(ptb, do[hh], _NN, preferred_element_type=F32)
                dk2 = dk + lax.dot_general(dsb, q[hh], _NN, preferred_element_type=F32)
                dq_ref[pl.ds(off, BQ), cols[hh]] += lax.dot_general(dsb, ks[hh], _TN, preferred_element_type=F32)
                out.append((dk2, dv2))
            return tuple(out)

        zero = jnp.zeros((BQ, HEAD_PAD), F32)
        carry = step(kj, tuple((zero, zero) for _ in range(HB)), True)
        carry = lax.fori_loop(kj + 1, nq, lambda i, c: step(i, c, False), carry)
        for hh in range(HB):
            dk, dv = carry[hh]
            dk = _rope_t(dk * ATT_SCALE, cos_ref[...], sa_ref[...], sb_ref[...])
            dkv_ref[:, 2 * hh * HEAD_PAD:(2 * hh + 1) * HEAD_PAD] = dk.astype(BF16)
            dkv_ref[:, (2 * hh + 1) * HEAD_PAD:(2 * hh + 2) * HEAD_PAD] = dv.astype(BF16)

    tab = pl.BlockSpec((BQ, HEAD_PAD), lambda g, j: (j, 0))
    row = pl.BlockSpec((HB, nq, 1, BQ), lambda g, j: (g, 0, 0, 0))
    seq = pl.BlockSpec((S, HB * HEAD_PAD), lambda g, j: (0, g))
    kvb = pl.BlockSpec((BQ, 2 * HB * HEAD_PAD), lambda g, j: (j, g))
    return pl.pallas_call(
        body, name="attn_bwd", grid=(N_HEADS // HB, nq),
        in_specs=[kvb, seq, seq, row, row, tab, tab, tab],
        out_specs=[seq, kvb],
        out_shape=[jax.ShapeDtypeStruct((S, QW), F32), jax.ShapeDtypeStruct((S, KVW), BF16)],
        compiler_params=_cparams("parallel", "arbitrary"))(kvf, qf, dob, lse_row, dd_row, cos, sa, sb)


DC = 128
TR = 256


def _dwconv_fwd(u, w, b):
    S, Dm = u.shape
    tr = min(TR, S)

    def body(u_ref, w_ref, b_ref, o_ref, pad_ref):
        pad_ref[pl.ds(0, PAD_ROWS), :] = jnp.zeros((PAD_ROWS, DC), F32)
        pad_ref[pl.ds(PAD_ROWS, S), :] = u_ref[...]
        wv = w_ref[...]
        for r in range(S // tr):
            acc = jnp.broadcast_to(b_ref[...], (tr, DC))
            for j in range(CONV_WIDTH):
                acc = acc + wv[j:j + 1, :] * pad_ref[pl.ds(r * tr + PAD_ROWS - (CONV_WIDTH - 1) + j, tr), :]
            o_ref[pl.ds(r * tr, tr), :] = acc

    return pl.pallas_call(
        body, name="dwconv_fwd", grid=(Dm // DC,),
        in_specs=[pl.BlockSpec((S, DC), lambda c: (0, c)), pl.BlockSpec((CONV_WIDTH, DC), lambda c: (0, c)),
                  pl.BlockSpec((1, DC), lambda c: (0, c))],
        out_specs=pl.BlockSpec((S, DC), lambda c: (0, c)),
        out_shape=jax.ShapeDtypeStruct((S, Dm), F32),
        scratch_shapes=[pltpu.VMEM((S + PAD_ROWS, DC), F32)],
        compiler_params=_cparams("parallel"))(u, w, b)


def _dwconv_bwd(d, u, w):
    S, Dm = u.shape
    tr = min(TR, S)

    def body(d_ref, u_ref, w_ref, du_ref, dw_ref, padd_ref, padu_ref):
        padd_ref[pl.ds(0, S), :] = d_ref[...]
        padd_ref[pl.ds(S, PAD_ROWS), :] = jnp.zeros((PAD_ROWS, DC), F32)
        padu_ref[pl.ds(0, PAD_ROWS), :] = jnp.zeros((PAD_ROWS, DC), F32)
        padu_ref[pl.ds(PAD_ROWS, S), :] = u_ref[...]
        wv = w_ref[...]
        dws = [jnp.zeros((1, DC), F32) for _ in range(CONV_WIDTH)]
        for r in range(S // tr):
            acc = jnp.zeros((tr, DC), F32)
            for j in range(CONV_WIDTH):
                acc = acc + wv[j:j + 1, :] * padd_ref[pl.ds(r * tr + (CONV_WIDTH - 1) - j, tr), :]
            du_ref[pl.ds(r * tr, tr), :] = acc
            dt = d_ref[pl.ds(r * tr, tr), :]
            for j in range(CONV_WIDTH):
                ut = padu_ref[pl.ds(r * tr + PAD_ROWS - (CONV_WIDTH - 1) + j, tr), :]
                dws[j] = dws[j] + _colsum(dt * ut)
        for j in range(CONV_WIDTH):
            dw_ref[pl.ds(j, 1), :] = dws[j]
        dw_ref[pl.ds(CONV_WIDTH, 1), :] = jnp.zeros((1, DC), F32)

    blk = pl.BlockSpec((S, DC), lambda c: (0, c))
    return pl.pallas_call(
        body, name="dwconv_bwd", grid=(Dm // DC,),
        in_specs=[blk, blk, pl.BlockSpec((CONV_WIDTH, DC), lambda c: (0, c))],
        out_specs=[blk, pl.BlockSpec((PAD_ROWS, DC), lambda c: (0, c))],
        out_shape=[jax.ShapeDtypeStruct((S, Dm), F32), jax.ShapeDtypeStruct((PAD_ROWS, Dm), F32)],
        scratch_shapes=[pltpu.VMEM((S + PAD_ROWS, DC), F32), pltpu.VMEM((S + PAD_ROWS, DC), F32)],
        compiler_params=_cparams("parallel"))(d, u, w)


POOL_C = D_MODEL // len(POOL_WINDOWS)
MAX_WIN = max(POOL_WINDOWS)


def _pool_counts(r, tr, win):
    t = r * tr + lax.broadcasted_iota(jnp.int32, (tr, 1), 0)
    return jnp.minimum(t + 1, win).astype(F32)


def _pool_fwd(h):
    S, Dm = h.shape
    tr = min(TR, S)

    def body(h_ref, o_ref, pad_ref):
        win = jnp.left_shift(2, pl.program_id(0))
        pad_ref[pl.ds(0, PAD_ROWS), :] = jnp.zeros((PAD_ROWS, POOL_C), F32)
        pad_ref[pl.ds(PAD_ROWS, S), :] = h_ref[...]
        for r in range(S // tr):
            acc = jnp.zeros((tr, POOL_C), F32)
            for j in range(MAX_WIN):
                use = jnp.where(j < win, 1.0, 0.0)
                acc = acc + use * pad_ref[pl.ds(r * tr + PAD_ROWS - j, tr), :]
            pooled = acc / _pool_counts(r, tr, win)
            o_ref[pl.ds(r * tr, tr), :] = (pooled - h_ref[pl.ds(r * tr, tr), :]).astype(BF16)

    blk = pl.BlockSpec((S, POOL_C), lambda g: (0, g))
    return pl.pallas_call(
        body, name="pool_fwd", grid=(len(POOL_WINDOWS),), in_specs=[blk], out_specs=blk,
        out_shape=jax.ShapeDtypeStruct((S, Dm), BF16),
        scratch_shapes=[pltpu.VMEM((S + PAD_ROWS, POOL_C), F32)],
        compiler_params=_cparams("parallel"))(h)


def _pool_bwd(dp):
    S, Dm = dp.shape
    tr = min(TR, S)

    def body(d_ref, o_ref, pad_ref):
        win = jnp.left_shift(2, pl.program_id(0))
        for r in range(S // tr):
            pad_ref[pl.ds(r * tr, tr), :] = d_ref[pl.ds(r * tr, tr), :] / _pool_counts(r, tr, win)
        pad_ref[pl.ds(S, PAD_ROWS), :] = jnp.zeros((PAD_ROWS, POOL_C), F32)
        for r in range(S // tr):
            acc = jnp.zeros((tr, POOL_C), F32)
            for j in range(MAX_WIN):
                use = jnp.where(j < win, 1.0, 0.0)
                acc = acc + use * pad_ref[pl.ds(r * tr + j, tr), :]
            o_ref[pl.ds(r * tr, tr), :] = acc - d_ref[pl.ds(r * tr, tr), :]

    blk = pl.BlockSpec((S, POOL_C), lambda g: (0, g))
    return pl.pallas_call(
        body, name="pool_bwd", grid=(len(POOL_WINDOWS),), in_specs=[blk], out_specs=blk,
        out_shape=jax.ShapeDtypeStruct((S, Dm), F32),
        scratch_shapes=[pltpu.VMEM((S + PAD_ROWS, POOL_C), F32)],
        compiler_params=_cparams("parallel"))(dp)


def _bias_spec(tn):
    return pl.BlockSpec((1, tn), lambda i, j, k: (0, j))


def _mla_weights(w_dq, w_dkv, w_uq, w_ukv, w_o):
    wd = jnp.concatenate([w_dq, w_dkv], axis=1)
    wq = jnp.pad(w_uq.reshape(Q_LORA, N_HEADS, QK_NOPE + QK_ROPE),
                 ((0, 0), (0, 0), (0, HEAD_PAD - QK_NOPE - QK_ROPE))).reshape(Q_LORA, QW)
    ukv = w_ukv.reshape(KV_LORA, N_HEADS, QK_NOPE + V_HEAD)
    wkv = jnp.zeros((DKV, N_HEADS, 2 * HEAD_PAD), BF16)
    wkv = wkv.at[:KV_LORA, :, :QK_NOPE].set(ukv[:, :, :QK_NOPE])
    wkv = wkv.at[:KV_LORA, :, HEAD_PAD:HEAD_PAD + V_HEAD].set(ukv[:, :, QK_NOPE:])
    eye = jnp.broadcast_to(jnp.eye(QK_ROPE, dtype=BF16)[:, None, :], (QK_ROPE, N_HEADS, QK_ROPE))
    wkv = wkv.at[KV_LORA:, :, QK_NOPE:QK_NOPE + QK_ROPE].set(eye).reshape(DKV, KVW)
    wo = jnp.pad(w_o.reshape(N_HEADS, V_HEAD, D_MODEL),
                 ((0, 0), (0, HEAD_PAD - V_HEAD), (0, 0))).reshape(QW, D_MODEL)
    return dict(wd=wd, wq=wq, wkv=wkv, wo=wo)


def _mla_weight_grads(g_wd, g_wq, g_wkv, g_wo):
    g_uq = g_wq.reshape(Q_LORA, N_HEADS, HEAD_PAD)[:, :, :QK_NOPE + QK_ROPE].reshape(Q_LORA, -1)
    t = g_wkv.reshape(DKV, N_HEADS, 2 * HEAD_PAD)[:KV_LORA]
    g_ukv = jnp.concatenate([t[:, :, :QK_NOPE], t[:, :, HEAD_PAD:HEAD_PAD + V_HEAD]], axis=2)
    g_o = g_wo.reshape(N_HEADS, HEAD_PAD, D_MODEL)[:, :V_HEAD].reshape(N_HEADS * V_HEAD, D_MODEL)
    return dict(mla_w_dq=g_wd[:, :Q_LORA], mla_w_uq=g_uq, mla_w_dkv=g_wd[:, Q_LORA:],
                mla_w_ukv=g_ukv.reshape(KV_LORA, -1), mla_w_o=g_o)


def _rope_epilogue(kv):
    def epi(acc, cos, sa, sb):
        parts = []
        for t in range(acc.shape[1] // HEAD_PAD):
            x = acc[:, t * HEAD_PAD:(t + 1) * HEAD_PAD]
            parts.append(x if (kv and t % 2) else _rope(x, cos, sa, sb))
        return (jnp.concatenate(parts, axis=1),)
    return epi


def _mla_fwd(tag, h, P, rope):
    S = h.shape[0]
    cos, sa, sb = rope
    tabs = [pl.BlockSpec((min(1024, S), HEAD_PAD), lambda i, j, k: (i, 0))] * 3
    cqkv = _mm(f"mla_down{tag}", h, P['wd'], 'nn', S, DQKV, D_MODEL)

    def norms(x, qg, kg):
        xq, xk, xr = x[:, :Q_LORA], x[:, Q_LORA:Q_LORA + KV_LORA], x[:, Q_LORA + KV_LORA:]
        cq = xq * lax.rsqrt(_rowmean(xq * xq) + NORM_EPS) * qg
        ck = xk * lax.rsqrt(_rowmean(xk * xk) + NORM_EPS) * kg
        return (cq, jnp.concatenate([ck, xr], axis=1)), ()

    (cq, ckv), _ = _rowk(f"mla_norms{tag}", norms, [cqkv], [P['qg'], P['kg']], [(Q_LORA, BF16), (DKV, BF16)], [])
    qf = _mm(f"mla_q{tag}", cq, P['wq'], 'nn', S, QW, Q_LORA, extras=[cos, sa, sb], extra_specs=tabs,
             epi=_rope_epilogue(False), outs=[jax.ShapeDtypeStruct((S, QW), BF16)])
    kvf = _mm(f"mla_kv{tag}", ckv, P['wkv'], 'nn', S, KVW, DKV, extras=[cos, sa, sb], extra_specs=tabs,
              epi=_rope_epilogue(True), outs=[jax.ShapeDtypeStruct((S, KVW), BF16)])
    o, lse = _attn_fwd(qf, kvf)
    y = _mm(f"mla_o{tag}", o, P['wo'], 'nn', S, D_MODEL, QW)
    return y, dict(cqkv=cqkv, cq=cq, ckv=ckv, qf=qf, kvf=kvf, o=o, lse=lse)


def _mla_bwd(tag, dy, h, sv, P, rope):
    S = h.shape[0]
    nq = S // BQ
    cos, sa, sb = rope
    g_wo = _mm(f"mla_o_wg{tag}", sv['o'], dy, 'tn', QW, D_MODEL, S)
    dob = _mm(f"mla_o_dg{tag}", dy, P['wo'], 'nt', S, QW, D_MODEL, outs=[jax.ShapeDtypeStruct((S, QW), BF16)])
    dd = _attn_delta(dob, sv['o'])
    dq_raw, dkv = _attn_bwd(sv['qf'], sv['kvf'], dob, sv['lse'].reshape(N_HEADS, nq, 1, BQ),
                            dd.reshape(N_HEADS, nq, 1, BQ), cos, sa, sb)

    def rope_bwd_q(d, cv, sav, sbv):
        parts = [_rope_t(d[:, t * HEAD_PAD:(t + 1) * HEAD_PAD] * ATT_SCALE, cv, sav, sbv) for t in range(N_HEADS)]
        return (jnp.concatenate(parts, axis=1),), ()

    (dq,), _ = _rowk(f"rope_bwd_q{tag}", rope_bwd_q, [dq_raw, cos, sa, sb], [], [(QW, BF16)], [])
    g_wq = _mm(f"mla_q_wg{tag}", sv['cq'], dq, 'tn', Q_LORA, QW, S)
    dcq = _mm(f"mla_q_dg{tag}", dq, P['wq'], 'nt', S, Q_LORA, QW)
    g_wkv = _mm(f"mla_kv_wg{tag}", sv['ckv'], dkv, 'tn', DKV, KVW, S)
    dckv = _mm(f"mla_kv_dg{tag}", dkv, P['wkv'], 'nt', S, DKV, KVW)

    def norms_bwd(dcq_v, dckv_v, x, qg, kg):
        xq, xk = x[:, :Q_LORA], x[:, Q_LORA:Q_LORA + KV_LORA]
        dxq, dqg = _gain_bwd(dcq_v, xq, qg)
        dxk, dkg = _gain_bwd(dckv_v[:, :KV_LORA], xk, kg)
        return (jnp.concatenate([dxq, dxk, dckv_v[:, KV_LORA:]], axis=1),), (dqg, dkg)

    (dcqkv,), (dqg, dkg) = _rowk(f"mla_norms_bwd{tag}", norms_bwd, [dcq, dckv, sv['cqkv']], [P['qg'], P['kg']],
                                 [(DQKV, BF16)], [Q_LORA, KV_LORA])
    g_wd = _mm(f"mla_down_wg{tag}", h, dcqkv, 'tn', D_MODEL, DQKV, S)
    dh = _mm(f"mla_down_dg{tag}", dcqkv, P['wd'], 'nt', S, D_MODEL, DQKV)
    grads = _mla_weight_grads(g_wd, g_wq, g_wkv, g_wo)
    grads.update(mla_q_norm_g=dqg.reshape(-1), mla_kv_norm_g=dkg.reshape(-1))
    return dh, grads


def _conv_fwd(h, P):
    S = h.shape[0]
    a = _mm("conv_pw1", h, P['w_pw1'], 'nn', S, 2 * D_MODEL, D_MODEL, extras=[P['b_pw1']],
            extra_specs=[_bias_spec(1024)], epi=lambda acc, b: (acc + b,))
    (u0,), _ = _rowk("conv_glu", lambda av: ((av[:, :D_MODEL] * _sigmoid(av[:, D_MODEL:]),), ()),
                     [a], [], [(D_MODEL, F32)], [])
    u1 = _dwconv_fwd(u0, P['w_dw'], P['b_dw'])

    def ln_silu(u, g, b):
        xc = u - _rowmean(u)
        z = xc * lax.rsqrt(_rowmean(xc * xc) + NORM_EPS) * g + b
        return (z * _sigmoid(z),), ()

    (u3,), _ = _rowk("conv_ln", ln_silu, [u1], [P['ln_g'], P['ln_b']], [(D_MODEL, BF16)], [])
    y = _mm("conv_pw2", u3, P['w_pw2'], 'nn', S, D_MODEL, D_MODEL, extras=[P['b_pw2']],
            extra_specs=[_bias_spec(1024)], epi=lambda acc, b: (acc + b,))
    return y, dict(a=a, u0=u0, u1=u1, u3=u3)


def _conv_bwd(dy, dy_colsum, h, sv, P):
    S = h.shape[0]
    g_pw2 = _mm("conv_pw2_wg", sv['u3'], dy, 'tn', D_MODEL, D_MODEL, S)
    du3 = _mm("conv_pw2_dg", dy, P['w_pw2'], 'nt', S, D_MODEL, D_MODEL)

    def ln_bwd(d3, u, g, b):
        xc = u - _rowmean(u)
        rstd = lax.rsqrt(_rowmean(xc * xc) + NORM_EPS)
        xh = xc * rstd
        z = xh * g + b
        sg = _sigmoid(z)
        dz = d3 * (sg * (1.0 + z * (1.0 - sg)))
        dxh = dz * g
        du = rstd * (dxh - _rowmean(dxh) - xh * _rowmean(dxh * xh))
        return (du,), (_colsum(dz * xh), _colsum(dz), _colsum(du))

    (du1,), (d_lng, d_lnb, d_bdw) = _rowk("conv_ln_bwd", ln_bwd, [du3, sv['u1']], [P['ln_g'], P['ln_b']],
                                          [(D_MODEL, F32)], [D_MODEL] * 3)
    du0, d_wdw = _dwconv_bwd(du1, sv['u0'], P['w_dw'])

    def glu_bwd(d0, av):
        a1, sg = av[:, :D_MODEL], _sigmoid(av[:, D_MODEL:])
        da = jnp.concatenate([d0 * sg, d0 * a1 * sg * (1.0 - sg)], axis=1)
        return (da,), (_colsum(da),)

    (da,), (d_bpw1,) = _rowk("conv_glu_bwd", glu_bwd, [du0, sv['a']], [], [(2 * D_MODEL, BF16)], [2 * D_MODEL])
    g_pw1 = _mm("conv_pw1_wg", h, da, 'tn', D_MODEL, 2 * D_MODEL, S)
    dh = _mm("conv_pw1_dg", da, P['w_pw1'], 'nt', S, D_MODEL, 2 * D_MODEL)
    grads = dict(conv_w_pw1=g_pw1, conv_b_pw1=d_bpw1.reshape(-1), conv_w_dw=d_wdw[:CONV_WIDTH],
                 conv_b_dw=d_bdw.reshape(-1), conv_ln_g=d_lng.reshape(-1), conv_ln_b=d_lnb.reshape(-1),
                 conv_w_pw2=g_pw2, conv_b_pw2=dy_colsum.reshape(-1))
    return dh, grads


def _pool_group_specs(tm):
    return (pl.BlockSpec((tm, POOL_C), lambda i, j, k: (i, j)),
            pl.BlockSpec((None, POOL_C, POOL_C), lambda i, j, k: (j, 0, 0)))


def _pool_mixer_fwd(h, P):
    S = h.shape[0]
    p = _pool_fwd(h)
    a_spec, b_spec = _pool_group_specs(min(1024, S))
    y, z = _mm("pool_mm", p, P['w'], 'nn', S, D_MODEL, POOL_C, tn=POOL_C, a_spec=a_spec, b_spec=b_spec,
               extras=[P['b'], P['scale']], extra_specs=[_bias_spec(POOL_C)] * 2,
               epi=lambda acc, b, s: ((acc + b) * s, acc + b),
               outs=[jax.ShapeDtypeStruct((S, D_MODEL), F32)] * 2)
    return y, dict(p=p, z=z)


def _pool_mixer_bwd(dy, sv, P):
    S = dy.shape[0]

    def scale_bwd(d, z, s):
        dz = d * s
        return (dz,), (_colsum(d * z), _colsum(dz))

    (dz,), (d_scale, d_b) = _rowk("pool_scale_bwd", scale_bwd, [dy, sv['z']], [P['scale']],
                                  [(D_MODEL, BF16)], [D_MODEL] * 2)
    a_spec, b_spec = _pool_group_specs(min(1024, S))
    dp = _mm("pool_mm_dg", dz, P['w'], 'nt', S, D_MODEL, POOL_C, tn=POOL_C, a_spec=a_spec, b_spec=b_spec)
    tk = min(512, S)
    grp = pl.BlockSpec((tk, POOL_C), lambda i, j, k: (k, j))
    g_w = _mm("pool_mm_wg", sv['p'], dz, 'tn', POOL_C, D_MODEL, S, tn=POOL_C, tk=tk, a_spec=grp, b_spec=grp,
              outs=[jax.ShapeDtypeStruct((len(POOL_WINDOWS), POOL_C, POOL_C), F32)],
              out_specs=[pl.BlockSpec((None, POOL_C, POOL_C), lambda i, j, k: (j, 0, 0))])
    dh = _pool_bwd(dp)
    return dh, dict(pool_w=g_w, pool_b=d_b.reshape(-1), pool_scale=d_scale.reshape(-1))


def _adamw(w, g, m, v):
    m2 = ADAM_B1 * m + (1.0 - ADAM_B1) * g
    v2 = ADAM_B2 * v + (1.0 - ADAM_B2) * (g * g)
    m_hat = m2 / (1.0 - ADAM_B1 ** ADAM_STEP)
    v_hat = v2 / (1.0 - ADAM_B2 ** ADAM_STEP)
    delta = -ADAM_LR * (m_hat / (jnp.sqrt(v_hat) + ADAM_EPS) + ADAM_WD * w)
    return delta, m2, v2


def _finish(name, w, land, m, v, layer=None, prev=None):
    local = land.shape[1:]
    C = local[-1]
    R = land[0].size // C
    tr = 64 if R % 64 == 0 else R
    lead = () if layer is None else (w.shape[0],)
    as2d = lambda a: a.reshape(lead + (R, C))
    n_prev = 0 if prev is None else 4

    def body(*refs):
        w_hbm, land_hbm, m_hbm, v_hbm = refs[:4]
        outs = refs[4 + n_prev:8 + n_prev]
        w_v, land_v, m_v, v_v, g_v, d_v, nm_v, nv_v, recv_v, io_sems, send_sem, recv_sem = refs[8 + n_prev:]
        pick = (lambda r: r) if layer is None else (lambda r: r.at[layer])
        loads = [pltpu.make_async_copy(src, dst, io_sems.at[k]) for k, (src, dst) in enumerate(
            [(pick(w_hbm), w_v), (land_hbm, land_v), (pick(m_hbm), m_v), (pick(v_hbm), v_v)])]
        for cp in loads:
            cp.start()
        for cp in loads:
            cp.wait()

        def rows_of(i):
            return pl.ds(pl.multiple_of(i * tr, tr), tr)

        def sum_chunk(i, carry):
            rows = rows_of(i)
            g_v[rows, :] = ((land_v[0, rows, :].astype(F32) + land_v[1, rows, :].astype(F32))
                            + land_v[2, rows, :].astype(F32)) + land_v[3, rows, :].astype(F32)
            return carry

        lax.fori_loop(0, R // tr, sum_chunk, 0)
        swap = pltpu.make_async_remote_copy(
            src_ref=g_v, dst_ref=recv_v, send_sem=send_sem, recv_sem=recv_sem,
            device_id=(lax.axis_index("x"), lax.axis_index("y"), 1 - lax.axis_index("c")),
            device_id_type=pl.DeviceIdType.MESH)
        swap.start()
        swap.wait()

        def update_chunk(i, carry):
            rows = rows_of(i)
            g = g_v[rows, :] + recv_v[rows, :]
            d, nm, nv = _adamw(w_v[rows, :], g, m_v[rows, :], v_v[rows, :])
            g_v[rows, :] = g
            d_v[rows, :] = d
            nm_v[rows, :] = nm
            nv_v[rows, :] = nv
            return carry

        lax.fori_loop(0, R // tr, update_chunk, 0)
        stores = [pltpu.make_async_copy(src, pick(dst), io_sems.at[k])
                  for k, (src, dst) in enumerate(zip([g_v, d_v, nm_v, nv_v], outs))]
        for cp in stores:
            cp.start()
        for cp in stores:
            cp.wait()

    any_spec = pl.BlockSpec(memory_space=pl.ANY)
    f32_buf = pltpu.VMEM((R, C), F32)
    res = pl.pallas_call(
        body, name=name, in_specs=[any_spec] * (4 + n_prev), out_specs=[any_spec] * 4,
        out_shape=[jax.ShapeDtypeStruct(lead + (R, C), F32)] * 4,
        input_output_aliases={4 + k: k for k in range(n_prev)},
        scratch_shapes=[f32_buf, pltpu.VMEM((4, R, C), BF16)] + [f32_buf] * 7
        + [pltpu.SemaphoreType.DMA((4,)), pltpu.SemaphoreType.DMA, pltpu.SemaphoreType.DMA],
        compiler_params=pltpu.CompilerParams(has_side_effects=True, vmem_limit_bytes=VMEM_LIMIT))(
            as2d(w), land.reshape(4, R, C), as2d(m), as2d(v), *([] if prev is None else [as2d(p) for p in prev]))
    return [r.reshape(w.shape) for r in res]


def _row(v):
    return v.reshape(1, -1)


def kernel(x, c, positions, ada_w, ada_b, norm_g, mla_w_dq, mla_q_norm_g, mla_w_uq, mla_w_dkv, mla_kv_norm_g, mla_w_ukv, mla_w_o, conv_w_pw1, conv_b_pw1, conv_w_dw, conv_b_dw, conv_ln_g, conv_ln_b, conv_w_pw2, conv_b_pw2, pool_w, pool_b, pool_scale, ffn_w1, ffn_w2, loss_target, m_ada_w, m_ada_b, m_norm_g, m_mla_w_dq, m_mla_q_norm_g, m_mla_w_uq, m_mla_w_dkv, m_mla_kv_norm_g, m_mla_w_ukv, m_mla_w_o, m_conv_w_pw1, m_conv_b_pw1, m_conv_w_dw, m_conv_b_dw, m_conv_ln_g, m_conv_ln_b, m_conv_w_pw2, m_conv_b_pw2, m_pool_w, m_pool_b, m_pool_scale, m_ffn_w1, m_ffn_w2, v_ada_w, v_ada_b, v_norm_g, v_mla_w_dq, v_mla_q_norm_g, v_mla_w_uq, v_mla_w_dkv, v_mla_kv_norm_g, v_mla_w_ukv, v_mla_w_o, v_conv_w_pw1, v_conv_b_pw1, v_conv_w_dw, v_conv_b_dw, v_conv_ln_g, v_conv_ln_b, v_conv_w_pw2, v_conv_b_pw2, v_pool_w, v_pool_b, v_pool_scale, v_ffn_w1, v_ffn_w2):
    args = dict(locals())
    W = {n: args[n] for n in WEIGHTS}
    MOM = {n: args['m_' + n] for n in WEIGHTS}
    VAR = {n: args['v_' + n] for n in WEIGHTS}
    S = x.shape[1]
    xs = x.reshape(S, D_MODEL)
    tgt = loss_target.reshape(S, D_MODEL)
    mx, my, mc = lax.axis_index("x"), lax.axis_index("y"), lax.axis_index("c")
    chip = 2 * mx + my
    n_sh = ada_w.shape[2]

    c8 = _exchange("gather_c", [c.reshape(8, D_MODEL // 8)], 'xyc')[0].reshape(8, D_MODEL)
    c8 = jnp.pad(c8, ((0, ADA_ROWS - 8), (0, 0)))
    silu = lambda v: v * _sigmoid(v)
    mod_sh = []
    for l in range(DEPTH):
        b_l = lax.dynamic_slice(ada_b[l], (chip * n_sh,), (n_sh,)).reshape(1, n_sh)
        mod_sh.append(_mm(f"ada_fwd{l}", c8, ada_w, 'nn', ADA_ROWS, n_sh, D_MODEL, tn=n_sh // 2, tk=512, pro_a=silu,
                          b_spec=pl.BlockSpec((None, 512, n_sh // 2), lambda i, j, k, l=l: (l, k, j)),
                          extras=[b_l], extra_specs=[_bias_spec(n_sh // 2)], epi=lambda acc, b: (acc + b,))[:8])
    mod_sh = jnp.stack(mod_sh, axis=1).reshape(8, DEPTH * n_sh // 128, 128)
    mod = _exchange("scatter_mod", [mod_sh], 'xy', src_by='xyc')[0]
    mod = mod.reshape(4, DEPTH, n_sh).transpose(1, 0, 2).reshape(DEPTH, 6, 1, D_MODEL)

    def sent_of(key):
        n, l = key
        arr = W[n] if l is None else W[n][l]
        return arr.astype(BF16) if n in BIG or n in ('ffn_w1', 'ffn_w2') else arr

    keys0 = [(n, 0) for n in MLA_MATS] + [(n, None) for n in ('norm_g', 'mla_q_norm_g', 'mla_kv_norm_g',
                                                               'conv_w_dw', 'pool_b', 'pool_scale')]
    keys1 = [('ffn_w1', 0), ('ffn_w2', 0), ('conv_w_pw1', None), ('conv_w_pw2', None), ('pool_w', None)]
    keys2 = [(n, l) for l in range(1, DEPTH) for n in ('ffn_w1', 'ffn_w2')] + [(n, 1) for n in MLA_MATS]
    G = dict(zip(keys0, _exchange("gather_w0", [sent_of(k) for k in keys0], 'xy')))
    fly1 = _split_start("gather_w1_start", [sent_of(k) for k in keys1])
    fly2 = _split_start("gather_w2_start", [sent_of(k) for k in keys2])
    mod = mod + (fly1['token'][0, 0] + fly2['token'][0, 0])

    def whole(key):
        n, l = key
        return _unshard(G[key], SHARD_AXIS[n] - (0 if l is None else 1))

    def mla_params(j):
        P = _mla_weights(*[whole((n, j)) for n in ('mla_w_dq', 'mla_w_dkv', 'mla_w_uq', 'mla_w_ukv', 'mla_w_o')])
        P.update(qg=_row(whole(('mla_q_norm_g', None))[j]), kg=_row(whole(('mla_kv_norm_g', None))[j]))
        return P

    gains = whole(('norm_g', None))
    mla_p = {0: mla_params(0)}
    conv_p = pool_p = None
    rope = _rope_tables(positions.reshape(S, 1).astype(F32))

    by_j = pl.BlockSpec((None, 1024, 1024), lambda i, j, k: (j, 0, 0))
    by_k = pl.BlockSpec((None, 1024, 1024), lambda i, j, k: (k, 0, 0))
    sq_relu = lambda v: jnp.square(jnp.maximum(v, 0.0))

    def md(i, k):
        return mod[i, k]

    (h,), _ = _rowk("pre0", lambda xv, g, sc, sh: ((_pre_fwd(xv, g, sc, sh),), ()),
                    [xs], [_row(gains[0, 0]), md(0, 1), md(0, 0)], [(D_MODEL, BF16)], [])
    saved = []
    xin = xs
    loss_acc = dxf = None
    for i in range(DEPTH):
        kind, j = i % 3, i // 3
        if kind == 0:
            if j not in mla_p:
                mla_p[j] = mla_params(j)
            y, sv = _mla_fwd(j, h, mla_p[j], rope)
        elif kind == 1:
            y, sv = _conv_fwd(h, conv_p)
        else:
            y, sv = _pool_mixer_fwd(h, pool_p)

        def mid(xv, yv, gt, g1, g2, sc, sh):
            x1 = _post_fwd(xv, yv, gt, g1)
            return (x1, _pre_fwd(x1, g2, sc, sh)), ()

        (x1, h2), _ = _rowk(f"mid{i}", mid, [xin, y], [md(i, 2), _row(gains[i, 1]), _row(gains[i, 2]), md(i, 4), md(i, 3)],
                            [(D_MODEL, F32), (D_MODEL, BF16)], [])
        if i == 0:
            G.update(zip(keys1, _split_wait("gather_w1_wait", fly1, h2)))
            conv_p = dict(w_pw1=whole(('conv_w_pw1', None))[0], b_pw1=_row(conv_b_pw1[0]),
                          w_dw=whole(('conv_w_dw', None))[0], b_dw=_row(conv_b_dw[0]), ln_g=_row(conv_ln_g[0]),
                          ln_b=_row(conv_ln_b[0]), w_pw2=whole(('conv_w_pw2', None))[0], b_pw2=_row(conv_b_pw2[0]))
            pool_p = dict(w=whole(('pool_w', None))[0], b=_row(whole(('pool_b', None))[0]),
                          scale=_row(whole(('pool_scale', None))[0]))
        if i == 1:
            G.update(zip(keys2, _split_wait("gather_w2_wait", fly2, h2)))
        a = _mm(f"ffn1_{i}", h2, G[('ffn_w1', i)], 'nn', S, D_FF, D_MODEL, b_spec=by_j)
        y2 = _mm(f"ffn2_{i}", a, G[('ffn_w2', i)], 'nn', S, D_MODEL, D_FF, pro_a=sq_relu, b_spec=by_k)
        saved.append(dict(x0=xin, h=h, y=y, x1=x1, h2=h2, a=a, y2=y2, mix=sv))
        if i + 1 < DEPTH:
            def nxt(xv, yv, gt, g3, g0, sc, sh):
                x2 = _post_fwd(xv, yv, gt, g3)
                return (x2, _pre_fwd(x2, g0, sc, sh)), ()

            hdt = F32 if (i + 1) % 3 == 2 else BF16
            (xin, h), _ = _rowk(f"next{i}", nxt, [x1, y2],
                                [md(i, 5), _row(gains[i, 3]), _row(gains[i + 1, 0]), md(i + 1, 1), md(i + 1, 0)],
                                [(D_MODEL, F32), (D_MODEL, hdt)], [])
        else:
            def head(xv, yv, tv, gt, g3):
                err = _post_fwd(xv, yv, gt, g3) - tv
                per_row = jnp.sum(err * err, axis=1, keepdims=True) * (0.5 / D_MODEL)
                return (err * (1.0 / D_MODEL),), (jnp.broadcast_to(jnp.sum(per_row, axis=0, keepdims=True), (1, 128)),)

            (dxf,), (loss_acc,) = _rowk("loss_head", head, [x1, y2, tgt], [md(i, 5), _row(gains[i, 3])],
                                        [(D_MODEL, F32)], [128])
    loss = lax.psum(loss_acc[0, 0], ("x", "y", "c"))

    small = {}
    big = {}
    landed = {}

    def keep(gm, layer):
        for n, g in gm.items():
            if n in BIG:
                g = g[None] if layer is None else g
                big[(n, layer)] = _to_shards(g, SHARD_AXIS[n] - (0 if layer is None else 1)).astype(BF16)
            else:
                small.setdefault(n, {})[layer or 0] = g

    d_mod = [None] * DEPTH
    d_gain = [None] * DEPTH
    dx = dxf
    for i in reversed(range(DEPTH)):
        kind, j = i % 3, i // 3
        sv = saved[i]
        def post2_bwd(d, yv, gt, g):
            dyv, d_gt, d_g = _post_bwd(d, yv, gt, g)
            return (dyv,), (d_gt, d_g)

        (dy2,), (d_gtf, d_g3) = _rowk(f"post2_bwd{i}", post2_bwd, [dx, sv['y2']], [md(i, 5), _row(gains[i, 3])],
                                      [(D_MODEL, BF16)], [D_MODEL] * 2)
        da = _mm(f"ffn2_dg{i}", dy2, G[('ffn_w2', i)], 'nt', S, D_FF, D_MODEL, b_spec=by_j, extras=[sv['a']],
                 extra_specs=[pl.BlockSpec((min(1024, S), 1024), lambda i_, j_, k_: (i_, j_))],
                 epi=lambda acc, av: (acc * (2.0 * jnp.maximum(av, 0.0)),),
                 outs=[jax.ShapeDtypeStruct((S, D_FF), BF16)])
        big[('ffn_w2', i)] = _mm(f"ffn2_wg{i}", sv['a'], dy2, 'tn', D_FF, D_MODEL, S, pro_a=sq_relu,
                        outs=[jax.ShapeDtypeStruct((4, 1024, D_MODEL), BF16)],
                        out_specs=[pl.BlockSpec((None, 1024, 1024), lambda i_, j_, k_: (i_, 0, j_))])
        big[('ffn_w1', i)] = _mm(f"ffn1_wg{i}", sv['h2'], da, 'tn', D_MODEL, D_FF, S,
                        outs=[jax.ShapeDtypeStruct((4, D_MODEL, 1024), BF16)],
                        out_specs=[pl.BlockSpec((None, 1024, 1024), lambda i_, j_, k_: (j_, i_, 0))])
        dh2 = _mm(f"ffn1_dg{i}", da, G[('ffn_w1', i)], 'nt', S, D_MODEL, D_FF, b_spec=by_k)
        if i == DEPTH - 1:
            keys_a = [('ffn_w1', i), ('ffn_w2', i)]
            fly_a = _split_start("scatter_ga_start", [big[k] for k in keys_a], src_by='xy')
            mod = mod + fly_a['token'][0, 0]
        if i == 0:
            keys_b = [k for k in big if k not in keys_a]
            fly_b = _split_start("scatter_gb_start", [big[k] for k in keys_b], src_by='xy')
            mod = mod + fly_b['token'][0, 0]

        def mid_bwd(d2, dh2v, x1v, yv, g2, scf, gtm, g1):
            dpre, d_sh, d_sc, d_g2 = _pre_bwd(dh2v, x1v, g2, scf)
            d1 = d2 + dpre
            dyv, d_gt, d_g1 = _post_bwd(d1, yv, gtm, g1)
            return (d1, dyv), (d_sh, d_sc, d_g2, d_gt, d_g1, _colsum(dyv))

        ydt = F32 if kind == 2 else BF16
        (dx1, dy), (d_shf, d_scf, d_g2, d_gtm, d_g1, dy_cs) = _rowk(
            f"mid_bwd{i}", mid_bwd, [dx, dh2, sv['x1'], sv['y']],
            [_row(gains[i, 2]), md(i, 4), md(i, 2), _row(gains[i, 1])],
            [(D_MODEL, F32), (D_MODEL, ydt)], [D_MODEL] * 6)
        if kind == 0:
            dh, gm = _mla_bwd(j, dy, sv['h'], sv['mix'], mla_p[j], rope)
            keep(gm, j)
        elif kind == 1:
            dh, gm = _conv_bwd(dy, dy_cs, sv['h'], sv['mix'], conv_p)
            keep(gm, None)
        else:
            dh, gm = _pool_mixer_bwd(dy, sv['mix'], pool_p)
            keep(gm, None)

        def pre_bwd(d1, dhv, x0v, g0, scm):
            dpre, d_sh, d_sc, d_g0 = _pre_bwd(dhv, x0v, g0, scm)
            return (d1 + dpre,), (d_sh, d_sc, d_g0)

        (dx,), (d_shm, d_scm, d_g0) = _rowk(f"pre_bwd{i}", pre_bwd, [dx1, dh, sv['x0']],
                                            [_row(gains[i, 0]), md(i, 1)], [(D_MODEL, F32)], [D_MODEL] * 3)
        d_mod[i] = jnp.concatenate([d_shm, d_scm, d_gtm, d_shf, d_scf, d_gtf], axis=1).reshape(-1)
        d_gain[i] = jnp.concatenate([d_g0, d_g1, d_g2, d_g3], axis=0)
        if i == DEPTH - 1:
            landed.update(zip(keys_a, _split_wait("scatter_ga_wait", fly_a, dx)))
    landed.update(zip(keys_b, _split_wait("scatter_gb_wait", fly_b, dx)))
    keys_c = [(n, 0) for n in MLA_MATS]
    landed.update(zip(keys_c, _exchange("scatter_gc", [big[k] for k in keys_c], 'xy', src_by='xy')))
    grad_x = dx.reshape(x.shape)
    grads = {n: jnp.stack([g[l] for l in sorted(g)]) for n, g in small.items()}
    grads['norm_g'] = jnp.stack(d_gain)
    grads['ada_b'] = jnp.stack(d_mod)

    pack = jnp.concatenate([grads[n].reshape(-1) for n in SMALL])
    n_pack = pack.shape[0]
    rows = -(-n_pack // 1024) * 8
    pack = jnp.pad(pack, (0, rows * 128 - n_pack)).reshape(rows, 128)
    pack8 = _exchange("gather_small", [pack], 'xyc')[0]
    (tot,) = _ew("sum_small", lambda *v: (functools.reduce(lambda p, q: p + q, v),), [(pack8, s) for s in range(8)],
                 [F32], (rows, 128))
    tot = tot.reshape(-1)
    d_mod_all = pack8.reshape(8, -1)[:, :DEPTH * 6 * D_MODEL].reshape(8, DEPTH, 6 * D_MODEL)
    final = {}
    off = 0
    for n in SMALL:
        ax = SHARD_AXIS[n]
        shape = tuple(d * 4 if k == ax else d for k, d in enumerate(W[n].shape))
        size = grads[n].size
        g = tot[off:off + size].reshape(shape)
        off += size
        if ax is not None:
            g = lax.dynamic_index_in_dim(_to_shards(g, ax), chip, 0, keepdims=False)
        final[n] = g

    g_ada = []
    for l in range(DEPTH):
        dm_l = jnp.pad(lax.dynamic_slice(d_mod_all[:, l], (0, chip * n_sh), (8, n_sh)), ((0, ADA_ROWS - 8), (0, 0)))
        g_ada.append(_mm(f"ada_wg{l}", c8, dm_l, 'tn', D_MODEL, n_sh, ADA_ROWS, tn=n_sh // 2, pro_a=silu))
    final['ada_w'] = jnp.stack(g_ada)

    out_g, out_d, out_m, out_v = {}, {}, {}, {}
    for n in WEIGHTS:
        shape = W[n].shape
        if n in ('ffn_w1', 'ffn_w2') or n in MLA_MATS:
            res = None
            for l in range(shape[0]):
                res = _finish(f"finish_{n}{l}", W[n], landed[(n, l)], MOM[n], VAR[n], layer=l, prev=res)
            out_g[n], out_d[n], out_m[n], out_v[n] = res
        elif n in BIG:
            out_g[n], out_d[n], out_m[n], out_v[n] = _finish(f"finish_{n}", W[n], landed[(n, None)], MOM[n], VAR[n])
        else:
            out_g[n] = final[n].reshape(shape)
            out_d[n], out_m[n], out_v[n] = _ew(f"adamw_{n}", lambda w, g, m, v: _adamw(w, g, m, v),
                                               [W[n], out_g[n], MOM[n], VAR[n]], [F32] * 3, shape)
    return (loss, grad_x, *[out_g[n] for n in WEIGHTS], *[out_d[n] for n in WEIGHTS],
            *[out_m[n] for n in WEIGHTS], *[out_v[n] for n in WEIGHTS])
```

```python
import functools
import math

import jax
import jax.numpy as jnp
from jax import lax
from jax.experimental import pallas as pl
from jax.experimental.pallas import tpu as pltpu

F32 = jnp.float32
BF16 = jnp.bfloat16

D_MODEL = 1024
DEPTH = 4
N_HEADS = 16
QK_NOPE = 64
QK_ROPE = 32
V_HEAD = 64
Q_LORA = 384
KV_LORA = 256
HEAD_PAD = 128
QW = N_HEADS * HEAD_PAD
KVW = 2 * QW
DKV = KV_LORA + QK_ROPE
DQKV = Q_LORA + DKV
D_FF = 4096
CONV_WIDTH = 31
POOL_WINDOWS = (2, 4, 8, 16)
CHUNK_SHIFT = 6
ROPE_THETA = 10000.0
NORM_EPS = 1e-6
NEG_INF = -1e30
ATT_SCALE = 1.0 / math.sqrt(QK_NOPE + QK_ROPE)
BQ = 256
HB = 4
LOG2E = 1.4426950408889634
SCALE_LOG2E = ATT_SCALE * LOG2E
PAD_ROWS = 32
ADA_ROWS = 128
VMEM_LIMIT = 56 * 1024 * 1024

ADAM_LR = 0.001
ADAM_B1 = 0.9
ADAM_B2 = 0.999
ADAM_EPS = 1e-08
ADAM_WD = 0.01
ADAM_STEP = 10

WEIGHTS = ['ada_w', 'ada_b', 'norm_g', 'mla_w_dq', 'mla_q_norm_g', 'mla_w_uq', 'mla_w_dkv', 'mla_kv_norm_g',
           'mla_w_ukv', 'mla_w_o', 'conv_w_pw1', 'conv_b_pw1', 'conv_w_dw', 'conv_b_dw', 'conv_ln_g', 'conv_ln_b',
           'conv_w_pw2', 'conv_b_pw2', 'pool_w', 'pool_b', 'pool_scale', 'ffn_w1', 'ffn_w2']
SHARD_AXIS = {'ada_w': 2, 'ada_b': None, 'norm_g': 2, 'mla_w_dq': 1, 'mla_q_norm_g': 1, 'mla_w_uq': 2,
              'mla_w_dkv': 1, 'mla_kv_norm_g': 1, 'mla_w_ukv': 2, 'mla_w_o': 1, 'conv_w_pw1': 2,
              'conv_b_pw1': None, 'conv_w_dw': 2, 'conv_b_dw': None, 'conv_ln_g': None, 'conv_ln_b': None,
              'conv_w_pw2': 1, 'conv_b_pw2': None, 'pool_w': 2, 'pool_b': 2, 'pool_scale': 1,
              'ffn_w1': 2, 'ffn_w2': 1}
MLA_MATS = ['mla_w_dq', 'mla_w_uq', 'mla_w_dkv', 'mla_w_ukv', 'mla_w_o']
BIG = MLA_MATS + ['conv_w_pw1', 'conv_w_pw2', 'pool_w']
SMALL = ['ada_b', 'norm_g', 'mla_q_norm_g', 'mla_kv_norm_g', 'conv_b_pw1', 'conv_w_dw', 'conv_b_dw',
         'conv_ln_g', 'conv_ln_b', 'conv_b_pw2', 'pool_b', 'pool_scale']


def _cparams(*sem):
    return pltpu.CompilerParams(dimension_semantics=sem, vmem_limit_bytes=VMEM_LIMIT)


def _colsum(v):
    return jnp.sum(v, axis=0, keepdims=True)


def _rowmean(v):
    return jnp.mean(v, axis=-1, keepdims=True)


def _sigmoid(v):
    return 1.0 / (1.0 + jnp.exp(-v))


def _rowk(name, fn, rows, bcast, out_row, out_acc, tm=256):
    S = rows[0].shape[0]
    tm = min(tm, S)
    assert S % tm == 0
    nin, no, na = len(rows) + len(bcast), len(out_row), len(out_acc)

    def body(*refs):
        vals = [r[...] for r in refs[:nin]]
        outs = refs[nin:nin + no]
        accs = refs[nin + no:]
        ro, ao = fn(*vals)
        for r, v in zip(outs, ro):
            r[...] = v.astype(r.dtype)
        if na:
            @pl.when(pl.program_id(0) == 0)
            def _():
                for r in accs:
                    r[...] = jnp.zeros(r.shape, r.dtype)
            for r, v in zip(accs, ao):
                r[...] += v

    in_specs = [pl.BlockSpec((tm, a.shape[1]), lambda i: (i, 0)) for a in rows]
    in_specs += [pl.BlockSpec(b.shape, lambda i, n=b.ndim: (0,) * n) for b in bcast]
    out_shape = [jax.ShapeDtypeStruct((S, w), dt) for w, dt in out_row]
    out_shape += [jax.ShapeDtypeStruct((1, w), F32) for w in out_acc]
    out_specs = [pl.BlockSpec((tm, w), lambda i: (i, 0)) for w, _ in out_row]
    out_specs += [pl.BlockSpec((1, w), lambda i: (0, 0)) for w in out_acc]
    res = pl.pallas_call(body, name=name, grid=(S // tm,), in_specs=in_specs, out_specs=out_specs,
                         out_shape=out_shape, compiler_params=_cparams("arbitrary"))(*rows, *bcast)
    return list(res[:no]), list(res[no:])


_DIMS = {'nn': ((1,), (0,)), 'nt': ((1,), (1,)), 'tn': ((0,), (0,))}


def _mm(name, a, b, mode, M, N, K, *, tm=1024, tn=1024, tk=1024, a_spec=None, b_spec=None, pro_a=None,
        extras=(), extra_specs=(), epi=None, outs=None, out_specs=None):
    tm, tn, tk = (t if d % t == 0 else d for t, d in ((min(tm, M), M), (min(tn, N), N), (min(tk, K), K)))
    nk = K // tk
    if a_spec is None:
        a_spec = (pl.BlockSpec((tk, tm), lambda i, j, k: (k, i)) if mode == 'tn'
                  else pl.BlockSpec((tm, tk), lambda i, j, k: (i, k)))
    if b_spec is None:
        b_spec = (pl.BlockSpec((tn, tk), lambda i, j, k: (j, k)) if mode == 'nt'
                  else pl.BlockSpec((tk, tn), lambda i, j, k: (k, j)))
    if outs is None:
        outs = [jax.ShapeDtypeStruct((M, N), F32)]
    if out_specs is None:
        out_specs = [pl.BlockSpec((tm, tn), lambda i, j, k: (i, j)) for _ in outs]
    ne, no = len(extras), len(outs)
    dims = (_DIMS[mode], ((), ()))

    def body(a_ref, b_ref, *rest):
        ex, out_refs = rest[:ne], rest[ne:ne + no]
        av = a_ref[...]
        if pro_a is not None:
            av = pro_a(av)
        part = lax.dot_general(av.astype(BF16), b_ref[...].astype(BF16), dims, preferred_element_type=F32)

        def finish(acc):
            vals = (acc,) if epi is None else epi(acc, *[e[...] for e in ex])
            for r, v in zip(out_refs, vals):
                r[...] = v.astype(r.dtype)

        if nk == 1:
            finish(part)
            return
        acc_ref = rest[ne + no]
        k = pl.program_id(2)

        @pl.when(k == 0)
        def _():
            acc_ref[...] = part

        @pl.when(k > 0)
        def _():
            acc_ref[...] += part

        @pl.when(k == nk - 1)
        def _():
            finish(acc_ref[...])

    res = pl.pallas_call(
        body, name=name, grid=(M // tm, N // tn, nk),
        in_specs=[a_spec, b_spec, *extra_specs], out_specs=list(out_specs), out_shape=list(outs),
        scratch_shapes=[pltpu.VMEM((tm, tn), F32)] if nk > 1 else [],
        compiler_params=_cparams("parallel", "parallel", "arbitrary"))(a, b, *extras)
    return res[0] if no == 1 else list(res)


def _row_tile(R, C, itemsize=4, budget=1 << 20):
    if R * C * itemsize <= budget or R % 8:
        return R
    t = 8
    while R % (t * 2) == 0 and t * 2 * C * itemsize <= budget:
        t *= 2
    return t


def _ew(name, fn, ins, out_dtypes, shape):
    C = shape[-1]
    R = 1
    for s in shape[:-1]:
        R *= s
    tr = _row_tile(R, C)
    ops, specs = [], []
    for it in ins:
        if isinstance(it, tuple):
            arr, idx = it
            ops.append(arr.reshape(arr.shape[0], R, C))
            specs.append(pl.BlockSpec((None, tr, C), lambda i, n=idx: (n, i, 0)))
        else:
            ops.append(it.reshape(R, C))
            specs.append(pl.BlockSpec((tr, C), lambda i: (i, 0)))
    nin = len(ops)

    def body(*refs):
        vals = fn(*[r[...] for r in refs[:nin]])
        for r, v in zip(refs[nin:], vals):
            r[...] = v.astype(r.dtype)

    res = pl.pallas_call(
        body, name=name, grid=(R // tr,), in_specs=specs,
        out_specs=[pl.BlockSpec((tr, C), lambda i: (i, 0)) for _ in out_dtypes],
        out_shape=[jax.ShapeDtypeStruct((R, C), dt) for dt in out_dtypes],
        compiler_params=_cparams("parallel"))(*ops)
    return [r.reshape(shape) for r in res]


_FLIPS = {'xyc': [(fx, fy, fc) for fx in (0, 1) for fy in (0, 1) for fc in (0, 1)][1:],
          'xy': [(1, 0, 0), (0, 1, 0), (1, 1, 0)],
          'c': [(0, 0, 1)]}
_NSLOT = {'xyc': 8, 'xy': 4, 'c': 2}


def _slot(kind, cx, cy, cc):
    return {'xyc': 4 * cx + 2 * cy + cc, 'xy': 2 * cx + cy, 'c': cc}[kind]


def _put_own(land, arr, group, src_by):
    coords = (lax.axis_index("x"), lax.axis_index("y"), lax.axis_index("c"))
    pay = arr if src_by is None else lax.dynamic_index_in_dim(arr, _slot(src_by, *coords), 0, keepdims=False)
    return lax.dynamic_update_index_in_dim(land, pay, _slot(group, *coords), 0)


def _exchange(name, arrays, group, src_by=None):
    flips, nsl, n = _FLIPS[group], _NSLOT[group], len(arrays)
    nf = len(flips)

    def body(*refs):
        ins, outs = refs[:n], refs[n:2 * n]
        send_sems, recv_sems = refs[2 * n:]
        mx, my, mc = lax.axis_index("x"), lax.axis_index("y"), lax.axis_index("c")
        me = _slot(group, mx, my, mc)

        def payload(a, cx, cy, cc):
            return ins[a] if src_by is None else ins[a].at[_slot(src_by, cx, cy, cc)]

        sends, recvs = [], []
        for a in range(n):
            for f, (fx, fy, fc) in enumerate(flips):
                px = 1 - mx if fx else mx
                py = 1 - my if fy else my
                pc = 1 - mc if fc else mc
                src = payload(a, px, py, pc)
                sends.append(pltpu.make_async_remote_copy(
                    src_ref=src, dst_ref=outs[a].at[me], send_sem=send_sems.at[a, f],
                    recv_sem=recv_sems.at[a, f], device_id=(px, py, pc),
                    device_id_type=pl.DeviceIdType.MESH))
                recvs.append(pltpu.make_async_remote_copy(
                    src_ref=src, dst_ref=outs[a].at[_slot(group, px, py, pc)], send_sem=send_sems.at[a, f],
                    recv_sem=recv_sems.at[a, f], device_id=(px, py, pc),
                    device_id_type=pl.DeviceIdType.MESH))
        for cp in sends:
            cp.start()
        for cp in recvs:
            cp.wait_recv()
        for cp in sends:
            cp.wait_send()

    out_shape = [jax.ShapeDtypeStruct((nsl,) + (a.shape if src_by is None else a.shape[1:]), a.dtype)
                 for a in arrays]
    any_spec = pl.BlockSpec(memory_space=pl.ANY)
    res = pl.pallas_call(
        body, name=name, in_specs=[any_spec] * n, out_specs=[any_spec] * n, out_shape=out_shape,
        scratch_shapes=[pltpu.SemaphoreType.DMA((n, nf)), pltpu.SemaphoreType.DMA((n, nf))],
        compiler_params=pltpu.CompilerParams(has_side_effects=True))(*arrays)
    return [_put_own(l, a, group, src_by) for a, l in zip(arrays, res)]


_HBM = pl.BlockSpec(memory_space=pltpu.HBM)
_SEM = pl.BlockSpec(memory_space=pltpu.SEMAPHORE)
_DATAFLOW = pltpu.SideEffectType.DATAFLOW_SIDE_EFFECTING


def _group_copies(ins, lands, send_sems, recv_sems, group, src_by):
    mx, my, mc = lax.axis_index("x"), lax.axis_index("y"), lax.axis_index("c")
    me = _slot(group, mx, my, mc)
    pairs = []
    for a in range(len(ins)):
        for fx, fy, fc in _FLIPS[group]:
            peer = (1 - mx if fx else mx, 1 - my if fy else my, 1 - mc if fc else mc)
            src = ins[a] if src_by is None else ins[a].at[_slot(src_by, *peer)]
            mk = functools.partial(pltpu.make_async_remote_copy, src_ref=src, send_sem=send_sems,
                                   recv_sem=recv_sems, device_id=peer, device_id_type=pl.DeviceIdType.MESH)
            pairs.append((mk(dst_ref=lands[a].at[me]), mk(dst_ref=lands[a].at[_slot(group, *peer)])))
    return pairs


def _split_start(name, arrays, group='xy', src_by=None):
    n = len(arrays)
    lands = [lax.empty((_NSLOT[group],) + (a.shape if src_by is None else a.shape[1:]), a.dtype) for a in arrays]

    def body(*refs):
        ins, lnd, send_sems, recv_sems, token = refs[:n], refs[n:2 * n], refs[2 * n], refs[2 * n + 1], refs[-1]
        for to_peer, _ in _group_copies(ins, lnd, send_sems, recv_sems, group, src_by):
            to_peer.start()
        token[...] = jnp.zeros(token.shape, F32)

    ops = [pltpu.with_memory_space_constraint(a, pltpu.HBM) for a in [*arrays, *lands]]
    res = pl.pallas_call(
        body, name=name, in_specs=[_HBM] * (2 * n),
        out_specs=[_SEM, _SEM] + [_HBM] * (2 * n) + [pl.BlockSpec(memory_space=pltpu.VMEM)],
        out_shape=[pltpu.SemaphoreType.DMA(()), pltpu.SemaphoreType.DMA(())]
        + [pltpu.HBM(a.shape, a.dtype) for a in ops] + [jax.ShapeDtypeStruct((8, 128), F32)],
        input_output_aliases={k: 2 + k for k in range(2 * n)},
        compiler_params=pltpu.CompilerParams(has_side_effects=_DATAFLOW))(*ops)
    return dict(n=n, group=group, src_by=src_by, send=res[0], recv=res[1], arrays=list(res[2:2 + n]),
                lands=list(res[2 + n:2 + 2 * n]), token=res[-1])


def _split_wait(name, st, after):
    n, group, src_by = st['n'], st['group'], st['src_by']

    def wait_body(*refs):
        ins, lnd, send_sems, recv_sems = refs[:n], refs[n:2 * n], refs[2 * n], refs[2 * n + 1]
        for to_peer, from_peer in _group_copies(ins, lnd, send_sems, recv_sems, group, src_by):
            to_peer.wait_send()
            from_peer.wait_recv()

    shapes = [pltpu.HBM(a.shape, a.dtype) for a in [*st['arrays'], *st['lands']]]
    res = pl.pallas_call(
        wait_body, name=name, in_specs=[_HBM] * (2 * n) + [_SEM, _SEM, pl.BlockSpec(memory_space=pl.ANY)],
        out_specs=[_HBM] * (2 * n), out_shape=shapes, input_output_aliases={k: k for k in range(2 * n)},
        compiler_params=pltpu.CompilerParams(has_side_effects=_DATAFLOW))(
            *st['arrays'], *st['lands'], st['send'], st['recv'], after)
    return [_put_own(l, a, group, src_by) for a, l in zip(res[:n], res[n:])]


def _unshard(g, axis):
    t = jnp.moveaxis(g, 0, axis)
    s = t.shape
    return t.reshape(s[:axis] + (s[axis] * s[axis + 1],) + s[axis + 2:])


def _to_shards(w, axis):
    s = w.shape
    t = w.reshape(s[:axis] + (4, s[axis] // 4) + s[axis + 1:])
    return jnp.moveaxis(t, axis, 0)


def _pre_fwd(x, g, sc, sh):
    r = lax.rsqrt(_rowmean(x * x) + NORM_EPS)
    return (x * r) * g * (1.0 + sc) + sh


def _pre_bwd(dh, x, g, sc):
    r = lax.rsqrt(_rowmean(x * x) + NORM_EPS)
    xn = x * r
    dxn = dh * (g * (1.0 + sc))
    dx = r * (dxn - xn * _rowmean(dxn * xn))
    t = dh * xn
    return dx, _colsum(dh), _colsum(t * g), _colsum(t * (1.0 + sc))


def _post_fwd(x, y, gt, g):
    r = lax.rsqrt(_rowmean(y * y) + NORM_EPS)
    return x + gt * ((y * r) * g)


def _post_bwd(dxo, y, gt, g):
    r = lax.rsqrt(_rowmean(y * y) + NORM_EPS)
    yn = y * r
    t = dxo * yn
    dyn = dxo * (gt * g)
    dy = r * (dyn - yn * _rowmean(dyn * yn))
    return dy, _colsum(t * g), _colsum(t * gt)


def _gain_bwd(dy, x, g):
    r = lax.rsqrt(_rowmean(x * x) + NORM_EPS)
    xn = x * r
    dxn = dy * g
    return r * (dxn - xn * _rowmean(dxn * xn)), _colsum(dy * xn)


def _rope(x, cos, sa, sb):
    return x * cos + pltpu.roll(x, HEAD_PAD - 16, 1) * sa + pltpu.roll(x, 16, 1) * sb


def _rope_t(d, cos, sa, sb):
    return d * cos + pltpu.roll(d * sa, 16, 1) + pltpu.roll(d * sb, HEAD_PAD - 16, 1)


def _rope_tables(pos_f):
    S = pos_f.shape[0]
    inv = ROPE_THETA ** (-jnp.arange(0, QK_ROPE, 2, dtype=F32) / QK_ROPE)
    inv_ext = jnp.concatenate([jnp.zeros((QK_NOPE,), F32), inv, inv,
                               jnp.zeros((HEAD_PAD - QK_NOPE - QK_ROPE,), F32)]).reshape(1, HEAD_PAD)

    def fn(p, iv):
        ang = p * iv
        lane = lax.broadcasted_iota(jnp.int32, ang.shape, 1)
        s = jnp.sin(ang)
        first = (lane >= QK_NOPE) & (lane < QK_NOPE + QK_ROPE // 2)
        second = (lane >= QK_NOPE + QK_ROPE // 2) & (lane < QK_NOPE + QK_ROPE)
        return (jnp.cos(ang), jnp.where(first, -s, 0.0), jnp.where(second, s, 0.0)), ()

    (cos, sa, sb), _ = _rowk("rope_tables", fn, [pos_f], [inv_ext], [(HEAD_PAD, F32)] * 3, [])
    return cos, sa, sb


def _diag_mask(transposed):
    r = lax.broadcasted_iota(jnp.int32, (BQ, BQ), 0) >> CHUNK_SHIFT
    c = lax.broadcasted_iota(jnp.int32, (BQ, BQ), 1) >> CHUNK_SHIFT
    return (r <= c) if transposed else (c <= r)


_NT = (((1,), (1,)), ((), ()))
_NN = (((1,), (0,)), ((), ()))


def _attn_fwd(qf, kvf):
    S = qf.shape[0]
    nq = S // BQ

    def body(q_ref, kv_ref, o_ref, lse_ref):
        qi = pl.program_id(1)
        qs = [q_ref[:, hh * HEAD_PAD:(hh + 1) * HEAD_PAD] for hh in range(HB)]

        def step(j, carry, diag):
            off = pl.multiple_of(j * BQ, BQ)
            sts = [lax.dot_general(kv_ref[pl.ds(off, BQ), pl.ds(2 * hh * HEAD_PAD, HEAD_PAD)], qs[hh], _NT,
                                   preferred_element_type=F32) for hh in range(HB)]
            mid = []
            for hh in range(HB):
                m, l, acc = carry[hh]
                st = jnp.where(_diag_mask(True), sts[hh], NEG_INF) if diag else sts[hh]
                m2 = jnp.maximum(m, jnp.max(st, axis=0, keepdims=True))
                al = jnp.exp2((m - m2) * SCALE_LOG2E)
                pt = jnp.exp2((st - m2) * SCALE_LOG2E)
                mid.append((m2, l * al + jnp.sum(pt, axis=0, keepdims=True), acc * al, pt.astype(BF16)))
            out = []
            for hh in range(HB):
                m2, l2, acc_s, ptb = mid[hh]
                v = kv_ref[pl.ds(off, BQ), pl.ds((2 * hh + 1) * HEAD_PAD, HEAD_PAD)]
                out.append((m2, l2, acc_s + lax.dot_general(v, ptb, _TN, preferred_element_type=F32)))
            return tuple(out)

        init = tuple((jnp.full((1, BQ), NEG_INF, F32), jnp.zeros((1, BQ), F32), jnp.zeros((HEAD_PAD, BQ), F32))
                     for _ in range(HB))
        carry = lax.fori_loop(0, qi, lambda j, c: step(j, c, False), init)
        carry = step(qi, carry, True)
        for hh in range(HB):
            m, l, acc = carry[hh]
            o_ref[:, hh * HEAD_PAD:(hh + 1) * HEAD_PAD] = (acc / l).T
            lse_ref[hh] = m * SCALE_LOG2E + jnp.log(l) * LOG2E

    return pl.pallas_call(
        body, name="attn_fwd", grid=(N_HEADS // HB, nq),
        in_specs=[pl.BlockSpec((BQ, HB * HEAD_PAD), lambda g, i: (i, g)),
                  pl.BlockSpec((S, 2 * HB * HEAD_PAD), lambda g, i: (0, g))],
        out_specs=[pl.BlockSpec((BQ, HB * HEAD_PAD), lambda g, i: (i, g)),
                   pl.BlockSpec((HB, None, 1, BQ), lambda g, i: (g, i, 0, 0))],
        out_shape=[jax.ShapeDtypeStruct((S, QW), F32), jax.ShapeDtypeStruct((N_HEADS, nq, 1, BQ), F32)],
        compiler_params=_cparams("parallel", "arbitrary"))(qf, kvf)


def _attn_delta(dob, o):
    S = o.shape[0]

    def body(do_ref, o_ref, dd_ref):
        dd_ref[...] = jnp.sum(do_ref[...].astype(F32) * o_ref[...], axis=1, keepdims=True)

    blk = pl.BlockSpec((S, HEAD_PAD), lambda h: (0, h))
    return pl.pallas_call(
        body, name="attn_delta", grid=(N_HEADS,), in_specs=[blk, blk],
        out_specs=pl.BlockSpec((None, S, 1), lambda h: (h, 0, 0)),
        out_shape=jax.ShapeDtypeStruct((N_HEADS, S, 1), F32),
        compiler_params=_cparams("parallel"))(dob, o)


_TN = (((0,), (0,)), ((), ()))


def _attn_bwd(qf, kvf, dob, lse_row, dd_row, cos, sa, sb):
    S = qf.shape[0]
    nq = S // BQ

    def body(kv_ref, q_ref, do_ref, lse_ref, dd_ref, cos_ref, sa_ref, sb_ref, dq_ref, dkv_ref):
        kj = pl.program_id(1)

        @pl.when(kj == 0)
        def _():
            dq_ref[...] = jnp.zeros(dq_ref.shape, F32)

        ks = [kv_ref[:, 2 * hh * HEAD_PAD:(2 * hh + 1) * HEAD_PAD] for hh in range(HB)]
        vs = [kv_ref[:, (2 * hh + 1) * HEAD_PAD:(2 * hh + 2) * HEAD_PAD] for hh in range(HB)]

        def step(i, carry, diag):
            off = pl.multiple_of(i * BQ, BQ)
            cols = [pl.ds(hh * HEAD_PAD, HEAD_PAD) for hh in range(HB)]
            q = [q_ref[pl.ds(off, BQ), cols[hh]] for hh in range(HB)]
            do = [do_ref[pl.ds(off, BQ), cols[hh]] for hh in range(HB)]
            sts = [lax.dot_general(ks[hh], q[hh], _NT, preferred_element_type=F32) for hh in range(HB)]
            dpts = [lax.dot_general(vs[hh], do[hh], _NT, preferred_element_type=F32) for hh in range(HB)]
            mid = []
            for hh in range(HB):
                st = jnp.where(_diag_mask(True), sts[hh], NEG_INF) if diag else sts[hh]
                pt = jnp.exp2(st * SCALE_LOG2E - lse_ref[hh, i])
                mid.append((pt.astype(BF16), (pt * (dpts[hh] - dd_ref[hh, i])).astype(BF16)))
            out = []
            for hh in range(HB):
                dk, dv = carry[hh]
                ptb, dsb = mid[hh]
                dv2 = dv + lax.dot_general(ptb, do[hh], _NN, preferred_element_type=F32)
                dk2 = dk + lax.dot_general(dsb, q[hh], _NN, preferred_element_type=F32)
                dq_ref[pl.ds(off, BQ), cols[hh]] += lax.dot_general(dsb, ks[hh], _TN, preferred_element_type=F32)
                out.append((dk2, dv2))
            return tuple(out)

        zero = jnp.zeros((BQ, HEAD_PAD), F32)
        carry = step(kj, tuple((zero, zero) for _ in range(HB)), True)
        carry = lax.fori_loop(kj + 1, nq, lambda i, c: step(i, c, False), carry)
        for hh in range(HB):
            dk, dv = carry[hh]
            dk = _rope_t(dk * ATT_SCALE, cos_ref[...], sa_ref[...], sb_ref[...])
            dkv_ref[:, 2 * hh * HEAD_PAD:(2 * hh + 1) * HEAD_PAD] = dk.astype(BF16)
            dkv_ref[:, (2 * hh + 1) * HEAD_PAD:(2 * hh + 2) * HEAD_PAD] = dv.astype(BF16)

    tab = pl.BlockSpec((BQ, HEAD_PAD), lambda g, j: (j, 0))
    row = pl.BlockSpec((HB, nq, 1, BQ), lambda g, j: (g, 0, 0, 0))
    seq = pl.BlockSpec((S, HB * HEAD_PAD), lambda g, j: (0, g))
    kvb = pl.BlockSpec((BQ, 2 * HB * HEAD_PAD), lambda g, j: (j, g))
    return pl.pallas_call(
        body, name="attn_bwd", grid=(N_HEADS // HB, nq),
        in_specs=[kvb, seq, seq, row, row, tab, tab, tab],
        out_specs=[seq, kvb],
        out_shape=[jax.ShapeDtypeStruct((S, QW), F32), jax.ShapeDtypeStruct((S, KVW), BF16)],
        compiler_params=_cparams("parallel", "arbitrary"))(kvf, qf, dob, lse_row, dd_row, cos, sa, sb)


DC = 128
TR = 256


def _dwconv_fwd(u, w, b):
    S, Dm = u.shape
    tr = min(TR, S)

    def body(u_ref, w_ref, b_ref, o_ref, pad_ref):
        pad_ref[pl.ds(0, PAD_ROWS), :] = jnp.zeros((PAD_ROWS, DC), F32)
        pad_ref[pl.ds(PAD_ROWS, S), :] = u_ref[...]
        wv = w_ref[...]
        for r in range(S // tr):
            acc = jnp.broadcast_to(b_ref[...], (tr, DC))
            for j in range(CONV_WIDTH):
                acc = acc + wv[j:j + 1, :] * pad_ref[pl.ds(r * tr + PAD_ROWS - (CONV_WIDTH - 1) + j, tr), :]
            o_ref[pl.ds(r * tr, tr), :] = acc

    return pl.pallas_call(
        body, name="dwconv_fwd", grid=(Dm // DC,),
        in_specs=[pl.BlockSpec((S, DC), lambda c: (0, c)), pl.BlockSpec((CONV_WIDTH, DC), lambda c: (0, c)),
                  pl.BlockSpec((1, DC), lambda c: (0, c))],
        out_specs=pl.BlockSpec((S, DC), lambda c: (0, c)),
        out_shape=jax.ShapeDtypeStruct((S, Dm), F32),
        scratch_shapes=[pltpu.VMEM((S + PAD_ROWS, DC), F32)],
        compiler_params=_cparams("parallel"))(u, w, b)


def _dwconv_bwd(d, u, w):
    S, Dm = u.shape
    tr = min(TR, S)

    def body(d_ref, u_ref, w_ref, du_ref, dw_ref, padd_ref, padu_ref):
        padd_ref[pl.ds(0, S), :] = d_ref[...]
        padd_ref[pl.ds(S, PAD_ROWS), :] = jnp.zeros((PAD_ROWS, DC), F32)
        padu_ref[pl.ds(0, PAD_ROWS), :] = jnp.zeros((PAD_ROWS, DC), F32)
        padu_ref[pl.ds(PAD_ROWS, S), :] = u_ref[...]
        wv = w_ref[...]
        dws = [jnp.zeros((1, DC), F32) for _ in range(CONV_WIDTH)]
        for r in range(S // tr):
            acc = jnp.zeros((tr, DC), F32)
            for j in range(CONV_WIDTH):
                acc = acc + wv[j:j + 1, :] * padd_ref[pl.ds(r * tr + (CONV_WIDTH - 1) - j, tr), :]
            du_ref[pl.ds(r * tr, tr), :] = acc
            dt = d_ref[pl.ds(r * tr, tr), :]
            for j in range(CONV_WIDTH):
                ut = padu_ref[pl.ds(r * tr + PAD_ROWS - (CONV_WIDTH - 1) + j, tr), :]
                dws[j] = dws[j] + _colsum(dt * ut)
        for j in range(CONV_WIDTH):
            dw_ref[pl.ds(j, 1), :] = dws[j]
        dw_ref[pl.ds(CONV_WIDTH, 1), :] = jnp.zeros((1, DC), F32)

    blk = pl.BlockSpec((S, DC), lambda c: (0, c))
    return pl.pallas_call(
        body, name="dwconv_bwd", grid=(Dm // DC,),
        in_specs=[blk, blk, pl.BlockSpec((CONV_WIDTH, DC), lambda c: (0, c))],
        out_specs=[blk, pl.BlockSpec((PAD_ROWS, DC), lambda c: (0, c))],
        out_shape=[jax.ShapeDtypeStruct((S, Dm), F32), jax.ShapeDtypeStruct((PAD_ROWS, Dm), F32)],
        scratch_shapes=[pltpu.VMEM((S + PAD_ROWS, DC), F32), pltpu.VMEM((S + PAD_ROWS, DC), F32)],
        compiler_params=_cparams("parallel"))(d, u, w)


POOL_C = D_MODEL // len(POOL_WINDOWS)
MAX_WIN = max(POOL_WINDOWS)


def _pool_counts(r, tr, win):
    t = r * tr + lax.broadcasted_iota(jnp.int32, (tr, 1), 0)
    return jnp.minimum(t + 1, win).astype(F32)


def _pool_fwd(h):
    S, Dm = h.shape
    tr = min(TR, S)

    def body(h_ref, o_ref, pad_ref):
        win = jnp.left_shift(2, pl.program_id(0))
        pad_ref[pl.ds(0, PAD_ROWS), :] = jnp.zeros((PAD_ROWS, POOL_C), F32)
        pad_ref[pl.ds(PAD_ROWS, S), :] = h_ref[...]
        for r in range(S // tr):
            acc = jnp.zeros((tr, POOL_C), F32)
            for j in range(MAX_WIN):
                use = jnp.where(j < win, 1.0, 0.0)
                acc = acc + use * pad_ref[pl.ds(r * tr + PAD_ROWS - j, tr), :]
            pooled = acc / _pool_counts(r, tr, win)
            o_ref[pl.ds(r * tr, tr), :] = (pooled - h_ref[pl.ds(r * tr, tr), :]).astype(BF16)

    blk = pl.BlockSpec((S, POOL_C), lambda g: (0, g))
    return pl.pallas_call(
        body, name="pool_fwd", grid=(len(POOL_WINDOWS),), in_specs=[blk], out_specs=blk,
        out_shape=jax.ShapeDtypeStruct((S, Dm), BF16),
        scratch_shapes=[pltpu.VMEM((S + PAD_ROWS, POOL_C), F32)],
        compiler_params=_cparams("parallel"))(h)


def _pool_bwd(dp):
    S, Dm = dp.shape
    tr = min(TR, S)

    def body(d_ref, o_ref, pad_ref):
        win = jnp.left_shift(2, pl.program_id(0))
        for r in range(S // tr):
            pad_ref[pl.ds(r * tr, tr), :] = d_ref[pl.ds(r * tr, tr), :] / _pool_counts(r, tr, win)
        pad_ref[pl.ds(S, PAD_ROWS), :] = jnp.zeros((PAD_ROWS, POOL_C), F32)
        for r in range(S // tr):
            acc = jnp.zeros((tr, POOL_C), F32)
            for j in range(MAX_WIN):
                use = jnp.where(j < win, 1.0, 0.0)
                acc = acc + use * pad_ref[pl.ds(r * tr + j, tr), :]
            o_ref[pl.ds(r * tr, tr), :] = acc - d_ref[pl.ds(r * tr, tr), :]

    blk = pl.BlockSpec((S, POOL_C), lambda g: (0, g))
    return pl.pallas_call(
        body, name="pool_bwd", grid=(len(POOL_WINDOWS),), in_specs=[blk], out_specs=blk,
        out_shape=jax.ShapeDtypeStruct((S, Dm), F32),
        scratch_shapes=[pltpu.VMEM((S + PAD_ROWS, POOL_C), F32)],
        compiler_params=_cparams("parallel"))(dp)


def _bias_spec(tn):
    return pl.BlockSpec((1, tn), lambda i, j, k: (0, j))


def _mla_weights(w_dq, w_dkv, w_uq, w_ukv, w_o):
    wd = jnp.concatenate([w_dq, w_dkv], axis=1)
    wq = jnp.pad(w_uq.reshape(Q_LORA, N_HEADS, QK_NOPE + QK_ROPE),
                 ((0, 0), (0, 0), (0, HEAD_PAD - QK_NOPE - QK_ROPE))).reshape(Q_LORA, QW)
    ukv = w_ukv.reshape(KV_LORA, N_HEADS, QK_NOPE + V_HEAD)
    wkv = jnp.zeros((DKV, N_HEADS, 2 * HEAD_PAD), BF16)
    wkv = wkv.at[:KV_LORA, :, :QK_NOPE].set(ukv[:, :, :QK_NOPE])
    wkv = wkv.at[:KV_LORA, :, HEAD_PAD:HEAD_PAD + V_HEAD].set(ukv[:, :, QK_NOPE:])
    eye = jnp.broadcast_to(jnp.eye(QK_ROPE, dtype=BF16)[:, None, :], (QK_ROPE, N_HEADS, QK_ROPE))
    wkv = wkv.at[KV_LORA:, :, QK_NOPE:QK_NOPE + QK_ROPE].set(eye).reshape(DKV, KVW)
    wo = jnp.pad(w_o.reshape(N_HEADS, V_HEAD, D_MODEL),
                 ((0, 0), (0, HEAD_PAD - V_HEAD), (0, 0))).reshape(QW, D_MODEL)
    return dict(wd=wd, wq=wq, wkv=wkv, wo=wo)


def _mla_weight_grads(g_wd, g_wq, g_wkv, g_wo):
    g_uq = g_wq.reshape(Q_LORA, N_HEADS, HEAD_PAD)[:, :, :QK_NOPE + QK_ROPE].reshape(Q_LORA, -1)
    t = g_wkv.reshape(DKV, N_HEADS, 2 * HEAD_PAD)[:KV_LORA]
    g_ukv = jnp.concatenate([t[:, :, :QK_NOPE], t[:, :, HEAD_PAD:HEAD_PAD + V_HEAD]], axis=2)
    g_o = g_wo.reshape(N_HEADS, HEAD_PAD, D_MODEL)[:, :V_HEAD].reshape(N_HEADS * V_HEAD, D_MODEL)
    return dict(mla_w_dq=g_wd[:, :Q_LORA], mla_w_uq=g_uq, mla_w_dkv=g_wd[:, Q_LORA:],
                mla_w_ukv=g_ukv.reshape(KV_LORA, -1), mla_w_o=g_o)


def _rope_epilogue(kv):
    def epi(acc, cos, sa, sb):
        parts = []
        for t in range(acc.shape[1] // HEAD_PAD):
            x = acc[:, t * HEAD_PAD:(t + 1) * HEAD_PAD]
            parts.append(x if (kv and t % 2) else _rope(x, cos, sa, sb))
        return (jnp.concatenate(parts, axis=1),)
    return epi


def _mla_fwd(tag, h, P, rope):
    S = h.shape[0]
    cos, sa, sb = rope
    tabs = [pl.BlockSpec((min(1024, S), HEAD_PAD), lambda i, j, k: (i, 0))] * 3
    cqkv = _mm(f"mla_down{tag}", h, P['wd'], 'nn', S, DQKV, D_MODEL)

    def norms(x, qg, kg):
        xq, xk, xr = x[:, :Q_LORA], x[:, Q_LORA:Q_LORA + KV_LORA], x[:, Q_LORA + KV_LORA:]
        cq = xq * lax.rsqrt(_rowmean(xq * xq) + NORM_EPS) * qg
        ck = xk * lax.rsqrt(_rowmean(xk * xk) + NORM_EPS) * kg
        return (cq, jnp.concatenate([ck, xr], axis=1)), ()

    (cq, ckv), _ = _rowk(f"mla_norms{tag}", norms, [cqkv], [P['qg'], P['kg']], [(Q_LORA, BF16), (DKV, BF16)], [])
    qf = _mm(f"mla_q{tag}", cq, P['wq'], 'nn', S, QW, Q_LORA, extras=[cos, sa, sb], extra_specs=tabs,
             epi=_rope_epilogue(False), outs=[jax.ShapeDtypeStruct((S, QW), BF16)])
    kvf = _mm(f"mla_kv{tag}", ckv, P['wkv'], 'nn', S, KVW, DKV, extras=[cos, sa, sb], extra_specs=tabs,
              epi=_rope_epilogue(True), outs=[jax.ShapeDtypeStruct((S, KVW), BF16)])
    o, lse = _attn_fwd(qf, kvf)
    y = _mm(f"mla_o{tag}", o, P['wo'], 'nn', S, D_MODEL, QW)
    return y, dict(cqkv=cqkv, cq=cq, ckv=ckv, qf=qf, kvf=kvf, o=o, lse=lse)


def _mla_bwd(tag, dy, h, sv, P, rope):
    S = h.shape[0]
    nq = S // BQ
    cos, sa, sb = rope
    g_wo = _mm(f"mla_o_wg{tag}", sv['o'], dy, 'tn', QW, D_MODEL, S)
    dob = _mm(f"mla_o_dg{tag}", dy, P['wo'], 'nt', S, QW, D_MODEL, outs=[jax.ShapeDtypeStruct((S, QW), BF16)])
    dd = _attn_delta(dob, sv['o'])
    dq_raw, dkv = _attn_bwd(sv['qf'], sv['kvf'], dob, sv['lse'].reshape(N_HEADS, nq, 1, BQ),
                            dd.reshape(N_HEADS, nq, 1, BQ), cos, sa, sb)

    def rope_bwd_q(d, cv, sav, sbv):
        parts = [_rope_t(d[:, t * HEAD_PAD:(t + 1) * HEAD_PAD] * ATT_SCALE, cv, sav, sbv) for t in range(N_HEADS)]
        return (jnp.concatenate(parts, axis=1),), ()

    (dq,), _ = _rowk(f"rope_bwd_q{tag}", rope_bwd_q, [dq_raw, cos, sa, sb], [], [(QW, BF16)], [])
    g_wq = _mm(f"mla_q_wg{tag}", sv['cq'], dq, 'tn', Q_LORA, QW, S)
    dcq = _mm(f"mla_q_dg{tag}", dq, P['wq'], 'nt', S, Q_LORA, QW)
    g_wkv = _mm(f"mla_kv_wg{tag}", sv['ckv'], dkv, 'tn', DKV, KVW, S)
    dckv = _mm(f"mla_kv_dg{tag}", dkv, P['wkv'], 'nt', S, DKV, KVW)

    def norms_bwd(dcq_v, dckv_v, x, qg, kg):
        xq, xk = x[:, :Q_LORA], x[:, Q_LORA:Q_LORA + KV_LORA]
        dxq, dqg = _gain_bwd(dcq_v, xq, qg)
        dxk, dkg = _gain_bwd(dckv_v[:, :KV_LORA], xk, kg)
        return (jnp.concatenate([dxq, dxk, dckv_v[:, KV_LORA:]], axis=1),), (dqg, dkg)

    (dcqkv,), (dqg, dkg) = _rowk(f"mla_norms_bwd{tag}", norms_bwd, [dcq, dckv, sv['cqkv']], [P['qg'], P['kg']],
                                 [(DQKV, BF16)], [Q_LORA, KV_LORA])
    g_wd = _mm(f"mla_down_wg{tag}", h, dcqkv, 'tn', D_MODEL, DQKV, S)
    dh = _mm(f"mla_down_dg{tag}", dcqkv, P['wd'], 'nt', S, D_MODEL, DQKV)
    grads = _mla_weight_grads(g_wd, g_wq, g_wkv, g_wo)
    grads.update(mla_q_norm_g=dqg.reshape(-1), mla_kv_norm_g=dkg.reshape(-1))
    return dh, grads


def _conv_fwd(h, P):
    S = h.shape[0]
    a = _mm("conv_pw1", h, P['w_pw1'], 'nn', S, 2 * D_MODEL, D_MODEL, extras=[P['b_pw1']],
            extra_specs=[_bias_spec(1024)], epi=lambda acc, b: (acc + b,))
    (u0,), _ = _rowk("conv_glu", lambda av: ((av[:, :D_MODEL] * _sigmoid(av[:, D_MODEL:]),), ()),
                     [a], [], [(D_MODEL, F32)], [])
    u1 = _dwconv_fwd(u0, P['w_dw'], P['b_dw'])

    def ln_silu(u, g, b):
        xc = u - _rowmean(u)
        z = xc * lax.rsqrt(_rowmean(xc * xc) + NORM_EPS) * g + b
        return (z * _sigmoid(z),), ()

    (u3,), _ = _rowk("conv_ln", ln_silu, [u1], [P['ln_g'], P['ln_b']], [(D_MODEL, BF16)], [])
    y = _mm("conv_pw2", u3, P['w_pw2'], 'nn', S, D_MODEL, D_MODEL, extras=[P['b_pw2']],
            extra_specs=[_bias_spec(1024)], epi=lambda acc, b: (acc + b,))
    return y, dict(a=a, u0=u0, u1=u1, u3=u3)


def _conv_bwd(dy, dy_colsum, h, sv, P):
    S = h.shape[0]
    g_pw2 = _mm("conv_pw2_wg", sv['u3'], dy, 'tn', D_MODEL, D_MODEL, S)
    du3 = _mm("conv_pw2_dg", dy, P['w_pw2'], 'nt', S, D_MODEL, D_MODEL)

    def ln_bwd(d3, u, g, b):
        xc = u - _rowmean(u)
        rstd = lax.rsqrt(_rowmean(xc * xc) + NORM_EPS)
        xh = xc * rstd
        z = xh * g + b
        sg = _sigmoid(z)
        dz = d3 * (sg * (1.0 + z * (1.0 - sg)))
        dxh = dz * g
        du = rstd * (dxh - _rowmean(dxh) - xh * _rowmean(dxh * xh))
        return (du,), (_colsum(dz * xh), _colsum(dz), _colsum(du))

    (du1,), (d_lng, d_lnb, d_bdw) = _rowk("conv_ln_bwd", ln_bwd, [du3, sv['u1']], [P['ln_g'], P['ln_b']],
                                          [(D_MODEL, F32)], [D_MODEL] * 3)
    du0, d_wdw = _dwconv_bwd(du1, sv['u0'], P['w_dw'])

    def glu_bwd(d0, av):
        a1, sg = av[:, :D_MODEL], _sigmoid(av[:, D_MODEL:])
        da = jnp.concatenate([d0 * sg, d0 * a1 * sg * (1.0 - sg)], axis=1)
        return (da,), (_colsum(da),)

    (da,), (d_bpw1,) = _rowk("conv_glu_bwd", glu_bwd, [du0, sv['a']], [], [(2 * D_MODEL, BF16)], [2 * D_MODEL])
    g_pw1 = _mm("conv_pw1_wg", h, da, 'tn', D_MODEL, 2 * D_MODEL, S)
    dh = _mm("conv_pw1_dg", da, P['w_pw1'], 'nt', S, D_MODEL, 2 * D_MODEL)
    grads = dict(conv_w_pw1=g_pw1, conv_b_pw1=d_bpw1.reshape(-1), conv_w_dw=d_wdw[:CONV_WIDTH],
                 conv_b_dw=d_bdw.reshape(-1), conv_ln_g=d_lng.reshape(-1), conv_ln_b=d_lnb.reshape(-1),
                 conv_w_pw2=g_pw2, conv_b_pw2=dy_colsum.reshape(-1))
    return dh, grads


def _pool_group_specs(tm):
    return (pl.BlockSpec((tm, POOL_C), lambda i, j, k: (i, j)),
            pl.BlockSpec((None, POOL_C, POOL_C), lambda i, j, k: (j, 0, 0)))


def _pool_mixer_fwd(h, P):
    S = h.shape[0]
    p = _pool_fwd(h)
    a_spec, b_spec = _pool_group_specs(min(1024, S))
    y, z = _mm("pool_mm", p, P['w'], 'nn', S, D_MODEL, POOL_C, tn=POOL_C, a_spec=a_spec, b_spec=b_spec,
               extras=[P['b'], P['scale']], extra_specs=[_bias_spec(POOL_C)] * 2,
               epi=lambda acc, b, s: ((acc + b) * s, acc + b),
               outs=[jax.ShapeDtypeStruct((S, D_MODEL), F32)] * 2)
    return y, dict(p=p, z=z)


def _pool_mixer_bwd(dy, sv, P):
    S = dy.shape[0]

    def scale_bwd(d, z, s):
        dz = d * s
        return (dz,), (_colsum(d * z), _colsum(dz))

    (dz,), (d_scale, d_b) = _rowk("pool_scale_bwd", scale_bwd, [dy, sv['z']], [P['scale']],
                                  [(D_MODEL, BF16)], [D_MODEL] * 2)
    a_spec, b_spec = _pool_group_specs(min(1024, S))
    dp = _mm("pool_mm_dg", dz, P['w'], 'nt', S, D_MODEL, POOL_C, tn=POOL_C, a_spec=a_spec, b_spec=b_spec)
    tk = min(512, S)
    grp = pl.BlockSpec((tk, POOL_C), lambda i, j, k: (k, j))
    g_w = _mm("pool_mm_wg", sv['p'], dz, 'tn', POOL_C, D_MODEL, S, tn=POOL_C, tk=tk, a_spec=grp, b_spec=grp,
              outs=[jax.ShapeDtypeStruct((len(POOL_WINDOWS), POOL_C, POOL_C), F32)],
              out_specs=[pl.BlockSpec((None, POOL_C, POOL_C), lambda i, j, k: (j, 0, 0))])
    dh = _pool_bwd(dp)
    return dh, dict(pool_w=g_w, pool_b=d_b.reshape(-1), pool_scale=d_scale.reshape(-1))


def _adamw(w, g, m, v):
    m2 = ADAM_B1 * m + (1.0 - ADAM_B1) * g
    v2 = ADAM_B2 * v + (1.0 - ADAM_B2) * (g * g)
    m_hat = m2 / (1.0 - ADAM_B1 ** ADAM_STEP)
    v_hat = v2 / (1.0 - ADAM_B2 ** ADAM_STEP)
    delta = -ADAM_LR * (m_hat / (jnp.sqrt(v_hat) + ADAM_EPS) + ADAM_WD * w)
    return delta, m2, v2


def _finish(name, w, land, m, v, layer=None, prev=None):
    local = land.shape[1:]
    C = local[-1]
    R = land[0].size // C
    tr = 64 if R % 64 == 0 else R

    def body(land_hbm, g_hbm, land_v, g_v, recv_v, io_sem, send_sem, recv_sem):
        load = pltpu.make_async_copy(land_hbm, land_v, io_sem)
        load.start()
        load.wait()

        def rows_of(i):
            return pl.ds(pl.multiple_of(i * tr, tr), tr)

        def sum_chunk(i, carry):
            rows = rows_of(i)
            g_v[rows, :] = ((land_v[0, rows, :].astype(F32) + land_v[1, rows, :].astype(F32))
                            + land_v[2, rows, :].astype(F32)) + land_v[3, rows, :].astype(F32)
            return carry

        lax.fori_loop(0, R // tr, sum_chunk, 0)
        swap = pltpu.make_async_remote_copy(
            src_ref=g_v, dst_ref=recv_v, send_sem=send_sem, recv_sem=recv_sem,
            device_id=(lax.axis_index("x"), lax.axis_index("y"), 1 - lax.axis_index("c")),
            device_id_type=pl.DeviceIdType.MESH)
        swap.start()
        swap.wait()

        def add_chunk(i, carry):
            rows = rows_of(i)
            recv_v[rows, :] = g_v[rows, :] + recv_v[rows, :]
            return carry

        lax.fori_loop(0, R // tr, add_chunk, 0)
        store = pltpu.make_async_copy(recv_v, g_hbm, io_sem)
        store.start()
        store.wait()

    any_spec = pl.BlockSpec(memory_space=pl.ANY)
    g = pl.pallas_call(
        body, name=name, in_specs=[any_spec], out_specs=any_spec, out_shape=jax.ShapeDtypeStruct((R, C), F32),
        scratch_shapes=[pltpu.VMEM((4, R, C), BF16), pltpu.VMEM((R, C), F32), pltpu.VMEM((R, C), F32),
                        pltpu.SemaphoreType.DMA, pltpu.SemaphoreType.DMA, pltpu.SemaphoreType.DMA],
        compiler_params=pltpu.CompilerParams(has_side_effects=True, vmem_limit_bytes=VMEM_LIMIT))(
            land.reshape(4, R, C))

    lead = () if layer is None else (w.shape[0],)
    as2d = lambda a: a.reshape(lead + (R, C))
    tu = _row_tile(R, C, budget=1 << 19)
    tile = pl.BlockSpec((tu, C), lambda i: (i, 0))
    slab = tile if layer is None else pl.BlockSpec((None, tu, C), lambda i: (layer, i, 0))
    n_prev = 0 if prev is None else 4

    def update(w_ref, g_ref, m_ref, v_ref, *rest):
        outs = rest[n_prev:]
        gv = g_ref[...]
        d, nm, nv = _adamw(w_ref[...], gv, m_ref[...], v_ref[...])
        for r, val in zip(outs, (gv, d, nm, nv)):
            r[...] = val

    res = pl.pallas_call(
        update, name=name + "_adamw", grid=(R // tu,),
        in_specs=[slab, tile, slab, slab] + [any_spec] * n_prev, out_specs=[slab] * 4,
        out_shape=[jax.ShapeDtypeStruct(lead + (R, C), F32)] * 4,
        input_output_aliases={4 + k: k for k in range(n_prev)},
        compiler_params=_cparams("arbitrary"))(
            as2d(w), g, as2d(m), as2d(v), *([] if prev is None else [as2d(p) for p in prev]))
    return [r.reshape(w.shape) for r in res]


def _row(v):
    return v.reshape(1, -1)


def kernel(x, c, positions, ada_w, ada_b, norm_g, mla_w_dq, mla_q_norm_g, mla_w_uq, mla_w_dkv, mla_kv_norm_g, mla_w_ukv, mla_w_o, conv_w_pw1, conv_b_pw1, conv_w_dw, conv_b_dw, conv_ln_g, conv_ln_b, conv_w_pw2, conv_b_pw2, pool_w, pool_b, pool_scale, ffn_w1, ffn_w2, loss_target, m_ada_w, m_ada_b, m_norm_g, m_mla_w_dq, m_mla_q_norm_g, m_mla_w_uq, m_mla_w_dkv, m_mla_kv_norm_g, m_mla_w_ukv, m_mla_w_o, m_conv_w_pw1, m_conv_b_pw1, m_conv_w_dw, m_conv_b_dw, m_conv_ln_g, m_conv_ln_b, m_conv_w_pw2, m_conv_b_pw2, m_pool_w, m_pool_b, m_pool_scale, m_ffn_w1, m_ffn_w2, v_ada_w, v_ada_b, v_norm_g, v_mla_w_dq, v_mla_q_norm_g, v_mla_w_uq, v_mla_w_dkv, v_mla_kv_norm_g, v_mla_w_ukv, v_mla_w_o, v_conv_w_pw1, v_conv_b_pw1, v_conv_w_dw, v_conv_b_dw, v_conv_ln_g, v_conv_ln_b, v_conv_w_pw2, v_conv_b_pw2, v_pool_w, v_pool_b, v_pool_scale, v_ffn_w1, v_ffn_w2):
    args = dict(locals())
    W = {n: args[n] for n in WEIGHTS}
    MOM = {n: args['m_' + n] for n in WEIGHTS}
    VAR = {n: args['v_' + n] for n in WEIGHTS}
    S = x.shape[1]
    xs = x.reshape(S, D_MODEL)
    tgt = loss_target.reshape(S, D_MODEL)
    mx, my, mc = lax.axis_index("x"), lax.axis_index("y"), lax.axis_index("c")
    chip = 2 * mx + my
    n_sh = ada_w.shape[2]

    def sent_of(key):
        n, l = key
        arr = W[n] if l is None else W[n][l]
        return arr.astype(BF16) if n in BIG or n in ('ffn_w1', 'ffn_w2') else arr

    keys0 = [(n, 0) for n in MLA_MATS] + [(n, None) for n in ('norm_g', 'mla_q_norm_g', 'mla_kv_norm_g',
                                                               'conv_w_dw', 'pool_b', 'pool_scale')]
    fly0 = _split_start("gather_w0_start", [sent_of(k) for k in keys0])
    c = c + fly0['token'][0, 0]

    c8 = _exchange("gather_c", [c.reshape(8, D_MODEL // 8)], 'xyc')[0].reshape(8, D_MODEL)
    c8 = jnp.pad(c8, ((0, ADA_ROWS - 8), (0, 0)))
    silu = lambda v: v * _sigmoid(v)
    mod_sh = []
    for l in range(DEPTH):
        b_l = lax.dynamic_slice(ada_b[l], (chip * n_sh,), (n_sh,)).reshape(1, n_sh)
        mod_sh.append(_mm(f"ada_fwd{l}", c8, ada_w, 'nn', ADA_ROWS, n_sh, D_MODEL, tn=n_sh // 2, tk=512, pro_a=silu,
                          b_spec=pl.BlockSpec((None, 512, n_sh // 2), lambda i, j, k, l=l: (l, k, j)),
                          extras=[b_l], extra_specs=[_bias_spec(n_sh // 2)], epi=lambda acc, b: (acc + b,))[:8])
    mod_sh = jnp.stack(mod_sh, axis=1).reshape(8, DEPTH * n_sh // 128, 128)
    mod = _exchange("scatter_mod", [mod_sh], 'xy', src_by='xyc')[0]
    mod = mod.reshape(4, DEPTH, n_sh).transpose(1, 0, 2).reshape(DEPTH, 6, 1, D_MODEL)

    keys1 = [('ffn_w1', 0), ('ffn_w2', 0), ('conv_w_pw1', None), ('conv_w_pw2', None), ('pool_w', None)]
    keys2 = [(n, l) for l in range(1, DEPTH) for n in ('ffn_w1', 'ffn_w2')] + [(n, 1) for n in MLA_MATS]
    fly1 = _split_start("gather_w1_start", [sent_of(k) for k in keys1])
    fly2 = _split_start("gather_w2_start", [sent_of(k) for k in keys2])
    mod = mod + (fly1['token'][0, 0] + fly2['token'][0, 0])
    G = dict(zip(keys0, _split_wait("gather_w0_wait", fly0, mod)))

    def whole(key):
        n, l = key
        return _unshard(G[key], SHARD_AXIS[n] - (0 if l is None else 1))

    def mla_params(j):
        P = _mla_weights(*[whole((n, j)) for n in ('mla_w_dq', 'mla_w_dkv', 'mla_w_uq', 'mla_w_ukv', 'mla_w_o')])
        P.update(qg=_row(whole(('mla_q_norm_g', None))[j]), kg=_row(whole(('mla_kv_norm_g', None))[j]))
        return P

    gains = whole(('norm_g', None))
    mla_p = {0: mla_params(0)}
    conv_p = pool_p = None
    rope = _rope_tables(positions.reshape(S, 1).astype(F32))

    by_j = pl.BlockSpec((None, 1024, 1024), lambda i, j, k: (j, 0, 0))
    by_k = pl.BlockSpec((None, 1024, 1024), lambda i, j, k: (k, 0, 0))
    sq_relu = lambda v: jnp.square(jnp.maximum(v, 0.0))

    def md(i, k):
        return mod[i, k]

    (h,), _ = _rowk("pre0", lambda xv, g, sc, sh: ((_pre_fwd(xv, g, sc, sh),), ()),
                    [xs], [_row(gains[0, 0]), md(0, 1), md(0, 0)], [(D_MODEL, BF16)], [])
    saved = []
    xin = xs
    loss_acc = dxf = None
    for i in range(DEPTH):
        kind, j = i % 3, i // 3
        if kind == 0:
            if j not in mla_p:
                mla_p[j] = mla_params(j)
            y, sv = _mla_fwd(j, h, mla_p[j], rope)
        elif kind == 1:
            y, sv = _conv_fwd(h, conv_p)
        else:
            y, sv = _pool_mixer_fwd(h, pool_p)

        def mid(xv, yv, gt, g1, g2, sc, sh):
            x1 = _post_fwd(xv, yv, gt, g1)
            return (x1, _pre_fwd(x1, g2, sc, sh)), ()

        (x1, h2), _ = _rowk(f"mid{i}", mid, [xin, y], [md(i, 2), _row(gains[i, 1]), _row(gains[i, 2]), md(i, 4), md(i, 3)],
                            [(D_MODEL, F32), (D_MODEL, BF16)], [])
        if i == 0:
            G.update(zip(keys1, _split_wait("gather_w1_wait", fly1, h2)))
            conv_p = dict(w_pw1=whole(('conv_w_pw1', None))[0], b_pw1=_row(conv_b_pw1[0]),
                          w_dw=whole(('conv_w_dw', None))[0], b_dw=_row(conv_b_dw[0]), ln_g=_row(conv_ln_g[0]),
                          ln_b=_row(conv_ln_b[0]), w_pw2=whole(('conv_w_pw2', None))[0], b_pw2=_row(conv_b_pw2[0]))
            pool_p = dict(w=whole(('pool_w', None))[0], b=_row(whole(('pool_b', None))[0]),
                          scale=_row(whole(('pool_scale', None))[0]))
        if i == 1:
            G.update(zip(keys2, _split_wait("gather_w2_wait", fly2, h2)))
        a = _mm(f"ffn1_{i}", h2, G[('ffn_w1', i)], 'nn', S, D_FF, D_MODEL, tm=2048, b_spec=by_j,
                outs=[jax.ShapeDtypeStruct((S, D_FF), BF16)])
        y2 = _mm(f"ffn2_{i}", a, G[('ffn_w2', i)], 'nn', S, D_MODEL, D_FF, pro_a=sq_relu, b_spec=by_k)
        saved.append(dict(x0=xin, h=h, y=y, x1=x1, h2=h2, a=a, y2=y2, mix=sv))
        if i + 1 < DEPTH:
            def nxt(xv, yv, gt, g3, g0, sc, sh):
                x2 = _post_fwd(xv, yv, gt, g3)
                return (x2, _pre_fwd(x2, g0, sc, sh)), ()

            hdt = F32 if (i + 1) % 3 == 2 else BF16
            (xin, h), _ = _rowk(f"next{i}", nxt, [x1, y2],
                                [md(i, 5), _row(gains[i, 3]), _row(gains[i + 1, 0]), md(i + 1, 1), md(i + 1, 0)],
                                [(D_MODEL, F32), (D_MODEL, hdt)], [])
        else:
            def head(xv, yv, tv, gt, g3):
                err = _post_fwd(xv, yv, gt, g3) - tv
                per_row = jnp.sum(err * err, axis=1, keepdims=True) * (0.5 / D_MODEL)
                return (err * (1.0 / D_MODEL),), (jnp.broadcast_to(jnp.sum(per_row, axis=0, keepdims=True), (1, 128)),)

            (dxf,), (loss_acc,) = _rowk("loss_head", head, [x1, y2, tgt], [md(i, 5), _row(gains[i, 3])],
                                        [(D_MODEL, F32)], [128])
    loss = lax.psum(loss_acc[0, 0], ("x", "y", "c"))

    small = {}
    big = {}
    landed = {}

    def keep(gm, layer):
        for n, g in gm.items():
            if n in BIG:
                g = g[None] if layer is None else g
                big[(n, layer)] = _to_shards(g, SHARD_AXIS[n] - (0 if layer is None else 1)).astype(BF16)
            else:
                small.setdefault(n, {})[layer or 0] = g

    d_mod = [None] * DEPTH
    d_gain = [None] * DEPTH
    dx = dxf
    for i in reversed(range(DEPTH)):
        kind, j = i % 3, i // 3
        sv = saved[i]
        def post2_bwd(d, yv, gt, g):
            dyv, d_gt, d_g = _post_bwd(d, yv, gt, g)
            return (dyv,), (d_gt, d_g)

        (dy2,), (d_gtf, d_g3) = _rowk(f"post2_bwd{i}", post2_bwd, [dx, sv['y2']], [md(i, 5), _row(gains[i, 3])],
                                      [(D_MODEL, BF16)], [D_MODEL] * 2)
        da = _mm(f"ffn2_dg{i}", dy2, G[('ffn_w2', i)], 'nt', S, D_FF, D_MODEL, tm=2048, b_spec=by_j, extras=[sv['a']],
                 extra_specs=[pl.BlockSpec((min(2048, S), 1024), lambda i_, j_, k_: (i_, j_))],
                 epi=lambda acc, av: (acc * (2.0 * jnp.maximum(av, 0.0)),),
                 outs=[jax.ShapeDtypeStruct((S, D_FF), BF16)])
        big[('ffn_w2', i)] = _mm(f"ffn2_wg{i}", sv['a'], dy2, 'tn', D_FF, D_MODEL, S, pro_a=sq_relu,
                        outs=[jax.ShapeDtypeStruct((4, 1024, D_MODEL), BF16)],
                        out_specs=[pl.BlockSpec((None, 1024, 1024), lambda i_, j_, k_: (i_, 0, j_))])
        big[('ffn_w1', i)] = _mm(f"ffn1_wg{i}", sv['h2'], da, 'tn', D_MODEL, D_FF, S,
                        outs=[jax.ShapeDtypeStruct((4, D_MODEL, 1024), BF16)],
                        out_specs=[pl.BlockSpec((None, 1024, 1024), lambda i_, j_, k_: (j_, i_, 0))])
        dh2 = _mm(f"ffn1_dg{i}", da, G[('ffn_w1', i)], 'nt', S, D_MODEL, D_FF, tm=2048, b_spec=by_k)
        if i == DEPTH - 1:
            keys_a = [('ffn_w1', i), ('ffn_w2', i)]
            fly_a = _split_start("scatter_ga_start", [big[k] for k in keys_a], src_by='xy')
            mod = mod + fly_a['token'][0, 0]
        if i == 0:
            keys_b = [k for k in big if k not in keys_a]
            fly_b = _split_start("scatter_gb_start", [big[k] for k in keys_b], src_by='xy')
            mod = mod + fly_b['token'][0, 0]

        def mid_bwd(d2, dh2v, x1v, yv, g2, scf, gtm, g1):
            dpre, d_sh, d_sc, d_g2 = _pre_bwd(dh2v, x1v, g2, scf)
            d1 = d2 + dpre
            dyv, d_gt, d_g1 = _post_bwd(d1, yv, gtm, g1)
            return (d1, dyv), (d_sh, d_sc, d_g2, d_gt, d_g1, _colsum(dyv))

        ydt = F32 if kind == 2 else BF16
        (dx1, dy), (d_shf, d_scf, d_g2, d_gtm, d_g1, dy_cs) = _rowk(
            f"mid_bwd{i}", mid_bwd, [dx, dh2, sv['x1'], sv['y']],
            [_row(gains[i, 2]), md(i, 4), md(i, 2), _row(gains[i, 1])],
            [(D_MODEL, F32), (D_MODEL, ydt)], [D_MODEL] * 6)
        if kind == 0:
            dh, gm = _mla_bwd(j, dy, sv['h'], sv['mix'], mla_p[j], rope)
            keep(gm, j)
        elif kind == 1:
            dh, gm = _conv_bwd(dy, dy_cs, sv['h'], sv['mix'], conv_p)
            keep(gm, None)
        else:
            dh, gm = _pool_mixer_bwd(dy, sv['mix'], pool_p)
            keep(gm, None)

        def pre_bwd(d1, dhv, x0v, g0, scm):
            dpre, d_sh, d_sc, d_g0 = _pre_bwd(dhv, x0v, g0, scm)
            return (d1 + dpre,), (d_sh, d_sc, d_g0)

        (dx,), (d_shm, d_scm, d_g0) = _rowk(f"pre_bwd{i}", pre_bwd, [dx1, dh, sv['x0']],
                                            [_row(gains[i, 0]), md(i, 1)], [(D_MODEL, F32)], [D_MODEL] * 3)
        d_mod[i] = jnp.concatenate([d_shm, d_scm, d_gtm, d_shf, d_scf, d_gtf], axis=1).reshape(-1)
        d_gain[i] = jnp.concatenate([d_g0, d_g1, d_g2, d_g3], axis=0)
        if i == DEPTH - 1:
            landed.update(zip(keys_a, _split_wait("scatter_ga_wait", fly_a, dx)))
    landed.update(zip(keys_b, _split_wait("scatter_gb_wait", fly_b, dx)))
    keys_c = [(n, 0) for n in MLA_MATS]
    fly_c = _split_start("scatter_gc_start", [big[k] for k in keys_c], src_by='xy')
    grad_x = dx.reshape(x.shape)
    grads = {n: jnp.stack([g[l] for l in sorted(g)]) for n, g in small.items()}
    grads['norm_g'] = jnp.stack(d_gain)
    grads['ada_b'] = jnp.stack(d_mod)

    pack = jnp.concatenate([grads[n].reshape(-1) for n in SMALL])
    n_pack = pack.shape[0]
    rows = -(-n_pack // 1024) * 8
    pack = jnp.pad(pack, (0, rows * 128 - n_pack)).reshape(rows, 128)
    fly_s = _split_start("gather_small_start", [pack], group='xyc')

    out_g, out_d, out_m, out_v = {}, {}, {}, {}
    chains = {}
    for n in BIG + ['ffn_w1', 'ffn_w2']:
        if n in MLA_MATS:
            chains[n] = _finish(f"finish_{n}1", W[n], landed[(n, 1)], MOM[n], VAR[n], layer=1)
        elif n in BIG:
            out_g[n], out_d[n], out_m[n], out_v[n] = _finish(f"finish_{n}", W[n], landed[(n, None)], MOM[n], VAR[n])
        else:
            res = None
            for l in range(DEPTH):
                res = _finish(f"finish_{n}{l}", W[n], landed[(n, l)], MOM[n], VAR[n], layer=l, prev=res)
            out_g[n], out_d[n], out_m[n], out_v[n] = res
    landed.update(zip(keys_c, _split_wait("scatter_gc_wait", fly_c, out_g['ffn_w2'])))
    for n in MLA_MATS:
        out_g[n], out_d[n], out_m[n], out_v[n] = _finish(f"finish_{n}0", W[n], landed[(n, 0)], MOM[n], VAR[n],
                                                         layer=0, prev=chains[n])
    pack8 = _split_wait("gather_small_wait", fly_s, out_g['mla_w_o'])[0]
    (tot,) = _ew("sum_small", lambda *v: (functools.reduce(lambda p, q: p + q, v),), [(pack8, s) for s in range(8)],
                 [F32], (rows, 128))
    tot = tot.reshape(-1)
    d_mod_all = pack8.reshape(8, -1)[:, :DEPTH * 6 * D_MODEL].reshape(8, DEPTH, 6 * D_MODEL)
    final = {}
    off = 0
    for n in SMALL:
        ax = SHARD_AXIS[n]
        shape = tuple(d * 4 if k == ax else d for k, d in enumerate(W[n].shape))
        size = grads[n].size
        g = tot[off:off + size].reshape(shape)
        off += size
        if ax is not None:
            g = lax.dynamic_index_in_dim(_to_shards(g, ax), chip, 0, keepdims=False)
        final[n] = g

    g_ada = []
    for l in range(DEPTH):
        dm_l = jnp.pad(lax.dynamic_slice(d_mod_all[:, l], (0, chip * n_sh), (8, n_sh)), ((0, ADA_ROWS - 8), (0, 0)))
        g_ada.append(_mm(f"ada_wg{l}", c8, dm_l, 'tn', D_MODEL, n_sh, ADA_ROWS, tn=n_sh // 2, pro_a=silu))
    final['ada_w'] = jnp.stack(g_ada)

    for n in WEIGHTS:
        if n in out_g:
            continue
        shape = W[n].shape
        out_g[n] = final[n].reshape(shape)
        out_d[n], out_m[n], out_v[n] = _ew(f"adamw_{n}", lambda w, g, m, v: _adamw(w, g, m, v),
                                           [W[n], out_g[n], MOM[n], VAR[n]], [F32] * 3, shape)
    return (loss, grad_x, *[out_g[n] for n in WEIGHTS], *[out_d[n] for n in WEIGHTS],
            *[out_m[n] for n in WEIGHTS], *[out_v[n] for n in WEIGHTS])
```

```python
import functools
import math

import jax
import jax.numpy as jnp
from jax import lax
from jax.experimental import pallas as pl
from jax.experimental.pallas import tpu as pltpu

F32 = jnp.float32
BF16 = jnp.bfloat16

D_MODEL = 1024
DEPTH = 4
N_HEADS = 16
QK_NOPE = 64
QK_ROPE = 32
V_HEAD = 64
Q_LORA = 384
KV_LORA = 256
HEAD_PAD = 128
QW = N_HEADS * HEAD_PAD
KVW = 2 * QW
DKV = KV_LORA + QK_ROPE
DQKV = Q_LORA + DKV
D_FF = 4096
CONV_WIDTH = 31
POOL_WINDOWS = (2, 4, 8, 16)
CHUNK_SHIFT = 6
ROPE_THETA = 10000.0
NORM_EPS = 1e-6
NEG_INF = -1e30
ATT_SCALE = 1.0 / math.sqrt(QK_NOPE + QK_ROPE)
BQ = 256
HB = 8
HF = 8
LOG2E = 1.4426950408889634
SCALE_LOG2E = ATT_SCALE * LOG2E
PAD_ROWS = 32
ADA_ROWS = 128
VMEM_LIMIT = 56 * 1024 * 1024

ADAM_LR = 0.001
ADAM_B1 = 0.9
ADAM_B2 = 0.999
ADAM_EPS = 1e-08
ADAM_WD = 0.01
ADAM_STEP = 10

WEIGHTS = ['ada_w', 'ada_b', 'norm_g', 'mla_w_dq', 'mla_q_norm_g', 'mla_w_uq', 'mla_w_dkv', 'mla_kv_norm_g',
           'mla_w_ukv', 'mla_w_o', 'conv_w_pw1', 'conv_b_pw1', 'conv_w_dw', 'conv_b_dw', 'conv_ln_g', 'conv_ln_b',
           'conv_w_pw2', 'conv_b_pw2', 'pool_w', 'pool_b', 'pool_scale', 'ffn_w1', 'ffn_w2']
SHARD_AXIS = {'ada_w': 2, 'ada_b': None, 'norm_g': 2, 'mla_w_dq': 1, 'mla_q_norm_g': 1, 'mla_w_uq': 2,
              'mla_w_dkv': 1, 'mla_kv_norm_g': 1, 'mla_w_ukv': 2, 'mla_w_o': 1, 'conv_w_pw1': 2,
              'conv_b_pw1': None, 'conv_w_dw': 2, 'conv_b_dw': None, 'conv_ln_g': None, 'conv_ln_b': None,
              'conv_w_pw2': 1, 'conv_b_pw2': None, 'pool_w': 2, 'pool_b': 2, 'pool_scale': 1,
              'ffn_w1': 2, 'ffn_w2': 1}
MLA_MATS = ['mla_w_dq', 'mla_w_uq', 'mla_w_dkv', 'mla_w_ukv', 'mla_w_o']
BIG = MLA_MATS + ['conv_w_pw1', 'conv_w_pw2', 'pool_w']
SMALL = ['ada_b', 'norm_g', 'mla_q_norm_g', 'mla_kv_norm_g', 'conv_b_pw1', 'conv_w_dw', 'conv_b_dw',
         'conv_ln_g', 'conv_ln_b', 'conv_b_pw2', 'pool_b', 'pool_scale']


def _cparams(*sem):
    return pltpu.CompilerParams(dimension_semantics=sem, vmem_limit_bytes=VMEM_LIMIT)


def _colsum(v):
    return jnp.sum(v, axis=0, keepdims=True)


def _rowmean(v):
    return jnp.mean(v, axis=-1, keepdims=True)


def _sigmoid(v):
    return 1.0 / (1.0 + jnp.exp(-v))


def _rowk(name, fn, rows, bcast, out_row, out_acc, tm=256):
    S = rows[0].shape[0]
    tm = min(tm, S)
    assert S % tm == 0
    nin, no, na = len(rows) + len(bcast), len(out_row), len(out_acc)

    def body(*refs):
        vals = [r[...] for r in refs[:nin]]
        outs = refs[nin:nin + no]
        accs = refs[nin + no:]
        ro, ao = fn(*vals)
        for r, v in zip(outs, ro):
            r[...] = v.astype(r.dtype)
        if na:
            @pl.when(pl.program_id(0) == 0)
            def _():
                for r in accs:
                    r[...] = jnp.zeros(r.shape, r.dtype)
            for r, v in zip(accs, ao):
                r[...] += v

    in_specs = [pl.BlockSpec((tm, a.shape[1]), lambda i: (i, 0)) for a in rows]
    in_specs += [pl.BlockSpec(b.shape, lambda i, n=b.ndim: (0,) * n) for b in bcast]
    out_shape = [jax.ShapeDtypeStruct((S, w), dt) for w, dt in out_row]
    out_shape += [jax.ShapeDtypeStruct((1, w), F32) for w in out_acc]
    out_specs = [pl.BlockSpec((tm, w), lambda i: (i, 0)) for w, _ in out_row]
    out_specs += [pl.BlockSpec((1, w), lambda i: (0, 0)) for w in out_acc]
    res = pl.pallas_call(body, name=name, grid=(S // tm,), in_specs=in_specs, out_specs=out_specs,
                         out_shape=out_shape, compiler_params=_cparams("arbitrary"))(*rows, *bcast)
    return list(res[:no]), list(res[no:])


_DIMS = {'nn': ((1,), (0,)), 'nt': ((1,), (1,)), 'tn': ((0,), (0,))}


def _mm(name, a, b, mode, M, N, K, *, tm=1024, tn=1024, tk=1024, a_spec=None, b_spec=None, pro_a=None,
        extras=(), extra_specs=(), epi=None, outs=None, out_specs=None):
    tm, tn, tk = (t if d % t == 0 else d for t, d in ((min(tm, M), M), (min(tn, N), N), (min(tk, K), K)))
    nk = K // tk
    if a_spec is None:
        a_spec = (pl.BlockSpec((tk, tm), lambda i, j, k: (k, i)) if mode == 'tn'
                  else pl.BlockSpec((tm, tk), lambda i, j, k: (i, k)))
    if b_spec is None:
        b_spec = (pl.BlockSpec((tn, tk), lambda i, j, k: (j, k)) if mode == 'nt'
                  else pl.BlockSpec((tk, tn), lambda i, j, k: (k, j)))
    if outs is None:
        outs = [jax.ShapeDtypeStruct((M, N), F32)]
    if out_specs is None:
        out_specs = [pl.BlockSpec((tm, tn), lambda i, j, k: (i, j)) for _ in outs]
    ne, no = len(extras), len(outs)
    dims = (_DIMS[mode], ((), ()))

    def body(a_ref, b_ref, *rest):
        ex, out_refs = rest[:ne], rest[ne:ne + no]
        av = a_ref[...]
        if pro_a is not None:
            av = pro_a(av)
        part = lax.dot_general(av.astype(BF16), b_ref[...].astype(BF16), dims, preferred_element_type=F32)

        def finish(acc):
            vals = (acc,) if epi is None else epi(acc, *[e[...] for e in ex])
            for r, v in zip(out_refs, vals):
                r[...] = v.astype(r.dtype)

        if nk == 1:
            finish(part)
            return
        acc_ref = rest[ne + no]
        k = pl.program_id(2)

        @pl.when(k == 0)
        def _():
            acc_ref[...] = part

        @pl.when(k > 0)
        def _():
            acc_ref[...] += part

        @pl.when(k == nk - 1)
        def _():
            finish(acc_ref[...])

    res = pl.pallas_call(
        body, name=name, grid=(M // tm, N // tn, nk),
        in_specs=[a_spec, b_spec, *extra_specs], out_specs=list(out_specs), out_shape=list(outs),
        scratch_shapes=[pltpu.VMEM((tm, tn), F32)] if nk > 1 else [],
        compiler_params=_cparams("parallel", "parallel", "arbitrary"))(a, b, *extras)
    return res[0] if no == 1 else list(res)


def _row_tile(R, C, itemsize=4, budget=1 << 20):
    if R * C * itemsize <= budget or R % 8:
        return R
    t = 8
    while R % (t * 2) == 0 and t * 2 * C * itemsize <= budget:
        t *= 2
    return t


def _ew(name, fn, ins, out_dtypes, shape):
    C = shape[-1]
    R = 1
    for s in shape[:-1]:
        R *= s
    tr = _row_tile(R, C)
    ops, specs = [], []
    for it in ins:
        if isinstance(it, tuple):
            arr, idx = it
            ops.append(arr.reshape(arr.shape[0], R, C))
            specs.append(pl.BlockSpec((None, tr, C), lambda i, n=idx: (n, i, 0)))
        else:
            ops.append(it.reshape(R, C))
            specs.append(pl.BlockSpec((tr, C), lambda i: (i, 0)))
    nin = len(ops)

    def body(*refs):
        vals = fn(*[r[...] for r in refs[:nin]])
        for r, v in zip(refs[nin:], vals):
            r[...] = v.astype(r.dtype)

    res = pl.pallas_call(
        body, name=name, grid=(R // tr,), in_specs=specs,
        out_specs=[pl.BlockSpec((tr, C), lambda i: (i, 0)) for _ in out_dtypes],
        out_shape=[jax.ShapeDtypeStruct((R, C), dt) for dt in out_dtypes],
        compiler_params=_cparams("parallel"))(*ops)
    return [r.reshape(shape) for r in res]


_FLIPS = {'xyc': [(fx, fy, fc) for fx in (0, 1) for fy in (0, 1) for fc in (0, 1)][1:],
          'xy': [(1, 0, 0), (0, 1, 0), (1, 1, 0)],
          'c': [(0, 0, 1)]}
_NSLOT = {'xyc': 8, 'xy': 4, 'c': 2}


def _slot(kind, cx, cy, cc):
    return {'xyc': 4 * cx + 2 * cy + cc, 'xy': 2 * cx + cy, 'c': cc}[kind]


def _put_own(land, arr, group, src_by):
    coords = (lax.axis_index("x"), lax.axis_index("y"), lax.axis_index("c"))
    pay = arr if src_by is None else lax.dynamic_index_in_dim(arr, _slot(src_by, *coords), 0, keepdims=False)
    return lax.dynamic_update_index_in_dim(land, pay, _slot(group, *coords), 0)


def _exchange(name, arrays, group, src_by=None):
    flips, nsl, n = _FLIPS[group], _NSLOT[group], len(arrays)
    nf = len(flips)

    def body(*refs):
        ins, outs = refs[:n], refs[n:2 * n]
        send_sems, recv_sems = refs[2 * n:]
        mx, my, mc = lax.axis_index("x"), lax.axis_index("y"), lax.axis_index("c")
        me = _slot(group, mx, my, mc)

        def payload(a, cx, cy, cc):
            return ins[a] if src_by is None else ins[a].at[_slot(src_by, cx, cy, cc)]

        sends, recvs = [], []
        for a in range(n):
            for f, (fx, fy, fc) in enumerate(flips):
                px = 1 - mx if fx else mx
                py = 1 - my if fy else my
                pc = 1 - mc if fc else mc
                src = payload(a, px, py, pc)
                sends.append(pltpu.make_async_remote_copy(
                    src_ref=src, dst_ref=outs[a].at[me], send_sem=send_sems.at[a, f],
                    recv_sem=recv_sems.at[a, f], device_id=(px, py, pc),
                    device_id_type=pl.DeviceIdType.MESH))
                recvs.append(pltpu.make_async_remote_copy(
                    src_ref=src, dst_ref=outs[a].at[_slot(group, px, py, pc)], send_sem=send_sems.at[a, f],
                    recv_sem=recv_sems.at[a, f], device_id=(px, py, pc),
                    device_id_type=pl.DeviceIdType.MESH))
        for cp in sends:
            cp.start()
        for cp in recvs:
            cp.wait_recv()
        for cp in sends:
            cp.wait_send()

    out_shape = [jax.ShapeDtypeStruct((nsl,) + (a.shape if src_by is None else a.shape[1:]), a.dtype)
                 for a in arrays]
    any_spec = pl.BlockSpec(memory_space=pl.ANY)
    res = pl.pallas_call(
        body, name=name, in_specs=[any_spec] * n, out_specs=[any_spec] * n, out_shape=out_shape,
        scratch_shapes=[pltpu.SemaphoreType.DMA((n, nf)), pltpu.SemaphoreType.DMA((n, nf))],
        compiler_params=pltpu.CompilerParams(has_side_effects=True))(*arrays)
    return [_put_own(l, a, group, src_by) for a, l in zip(arrays, res)]


_HBM = pl.BlockSpec(memory_space=pltpu.HBM)
_SEM = pl.BlockSpec(memory_space=pltpu.SEMAPHORE)
_DATAFLOW = pltpu.SideEffectType.DATAFLOW_SIDE_EFFECTING


def _group_copies(ins, lands, send_sems, recv_sems, group, src_by):
    mx, my, mc = lax.axis_index("x"), lax.axis_index("y"), lax.axis_index("c")
    me = _slot(group, mx, my, mc)
    pairs = []
    for a in range(len(ins)):
        for fx, fy, fc in _FLIPS[group]:
            peer = (1 - mx if fx else mx, 1 - my if fy else my, 1 - mc if fc else mc)
            src = ins[a] if src_by is None else ins[a].at[_slot(src_by, *peer)]
            mk = functools.partial(pltpu.make_async_remote_copy, src_ref=src, send_sem=send_sems,
                                   recv_sem=recv_sems, device_id=peer, device_id_type=pl.DeviceIdType.MESH)
            pairs.append((mk(dst_ref=lands[a].at[me]), mk(dst_ref=lands[a].at[_slot(group, *peer)])))
    return pairs


def _split_start(name, arrays, group='xy', src_by=None):
    n = len(arrays)
    lands = [lax.empty((_NSLOT[group],) + (a.shape if src_by is None else a.shape[1:]), a.dtype) for a in arrays]

    def body(*refs):
        ins, lnd, send_sems, recv_sems, token = refs[:n], refs[n:2 * n], refs[2 * n], refs[2 * n + 1], refs[-1]
        for to_peer, _ in _group_copies(ins, lnd, send_sems, recv_sems, group, src_by):
            to_peer.start()
        token[...] = jnp.zeros(token.shape, F32)

    ops = [pltpu.with_memory_space_constraint(a, pltpu.HBM) for a in [*arrays, *lands]]
    res = pl.pallas_call(
        body, name=name, in_specs=[_HBM] * (2 * n),
        out_specs=[_SEM, _SEM] + [_HBM] * (2 * n) + [pl.BlockSpec(memory_space=pltpu.VMEM)],
        out_shape=[pltpu.SemaphoreType.DMA(()), pltpu.SemaphoreType.DMA(())]
        + [pltpu.HBM(a.shape, a.dtype) for a in ops] + [jax.ShapeDtypeStruct((8, 128), F32)],
        input_output_aliases={k: 2 + k for k in range(2 * n)},
        compiler_params=pltpu.CompilerParams(has_side_effects=_DATAFLOW))(*ops)
    return dict(n=n, group=group, src_by=src_by, send=res[0], recv=res[1], arrays=list(res[2:2 + n]),
                lands=list(res[2 + n:2 + 2 * n]), token=res[-1])


def _split_wait(name, st, after):
    n, group, src_by = st['n'], st['group'], st['src_by']

    def wait_body(*refs):
        ins, lnd, send_sems, recv_sems = refs[:n], refs[n:2 * n], refs[2 * n], refs[2 * n + 1]
        for to_peer, from_peer in _group_copies(ins, lnd, send_sems, recv_sems, group, src_by):
            to_peer.wait_send()
            from_peer.wait_recv()

    shapes = [pltpu.HBM(a.shape, a.dtype) for a in [*st['arrays'], *st['lands']]]
    res = pl.pallas_call(
        wait_body, name=name, in_specs=[_HBM] * (2 * n) + [_SEM, _SEM, pl.BlockSpec(memory_space=pl.ANY)],
        out_specs=[_HBM] * (2 * n), out_shape=shapes, input_output_aliases={k: k for k in range(2 * n)},
        compiler_params=pltpu.CompilerParams(has_side_effects=_DATAFLOW))(
            *st['arrays'], *st['lands'], st['send'], st['recv'], after)
    return [_put_own(l, a, group, src_by) for a, l in zip(res[:n], res[n:])]


def _unshard(g, axis):
    t = jnp.moveaxis(g, 0, axis)
    s = t.shape
    return t.reshape(s[:axis] + (s[axis] * s[axis + 1],) + s[axis + 2:])


def _to_shards(w, axis):
    s = w.shape
    t = w.reshape(s[:axis] + (4, s[axis] // 4) + s[axis + 1:])
    return jnp.moveaxis(t, axis, 0)


def _pre_fwd(x, g, sc, sh):
    r = lax.rsqrt(_rowmean(x * x) + NORM_EPS)
    return (x * r) * g * (1.0 + sc) + sh


def _pre_bwd(dh, x, g, sc):
    r = lax.rsqrt(_rowmean(x * x) + NORM_EPS)
    xn = x * r
    dxn = dh * (g * (1.0 + sc))
    dx = r * (dxn - xn * _rowmean(dxn * xn))
    t = dh * xn
    return dx, _colsum(dh), _colsum(t * g), _colsum(t * (1.0 + sc))


def _post_fwd(x, y, gt, g):
    r = lax.rsqrt(_rowmean(y * y) + NORM_EPS)
    return x + gt * ((y * r) * g)


def _post_bwd(dxo, y, gt, g):
    r = lax.rsqrt(_rowmean(y * y) + NORM_EPS)
    yn = y * r
    t = dxo * yn
    dyn = dxo * (gt * g)
    dy = r * (dyn - yn * _rowmean(dyn * yn))
    return dy, _colsum(t * g), _colsum(t * gt)


def _gain_bwd(dy, x, g):
    r = lax.rsqrt(_rowmean(x * x) + NORM_EPS)
    xn = x * r
    dxn = dy * g
    return r * (dxn - xn * _rowmean(dxn * xn)), _colsum(dy * xn)


def _rope(x, cos, sa, sb):
    return x * cos + pltpu.roll(x, HEAD_PAD - 16, 1) * sa + pltpu.roll(x, 16, 1) * sb


def _rope_t(d, cos, sa, sb):
    return d * cos + pltpu.roll(d * sa, 16, 1) + pltpu.roll(d * sb, HEAD_PAD - 16, 1)


def _rope_tables(pos_f):
    S = pos_f.shape[0]
    inv = ROPE_THETA ** (-jnp.arange(0, QK_ROPE, 2, dtype=F32) / QK_ROPE)
    inv_ext = jnp.concatenate([jnp.zeros((QK_NOPE,), F32), inv, inv,
                               jnp.zeros((HEAD_PAD - QK_NOPE - QK_ROPE,), F32)]).reshape(1, HEAD_PAD)

    def fn(p, iv):
        ang = p * iv
        lane = lax.broadcasted_iota(jnp.int32, ang.shape, 1)
        s = jnp.sin(ang)
        first = (lane >= QK_NOPE) & (lane < QK_NOPE + QK_ROPE // 2)
        second = (lane >= QK_NOPE + QK_ROPE // 2) & (lane < QK_NOPE + QK_ROPE)
        return (jnp.cos(ang), jnp.where(first, -s, 0.0), jnp.where(second, s, 0.0)), ()

    (cos, sa, sb), _ = _rowk("rope_tables", fn, [pos_f], [inv_ext], [(HEAD_PAD, F32)] * 3, [])
    return cos, sa, sb


def _diag_mask(transposed):
    r = lax.broadcasted_iota(jnp.int32, (BQ, BQ), 0) >> CHUNK_SHIFT
    c = lax.broadcasted_iota(jnp.int32, (BQ, BQ), 1) >> CHUNK_SHIFT
    return (r <= c) if transposed else (c <= r)


_NT = (((1,), (1,)), ((), ()))
_NN = (((1,), (0,)), ((), ()))


def _attn_fwd(qf, kvf, HB=HF):
    S = qf.shape[0]
    nq = S // BQ

    def body(q_ref, kv_ref, o_ref, lse_ref):
        qi = pl.program_id(1)
        qs = [q_ref[:, hh * HEAD_PAD:(hh + 1) * HEAD_PAD] for hh in range(HB)]

        def step(j, carry, diag):
            off = pl.multiple_of(j * BQ, BQ)
            sts = [lax.dot_general(kv_ref[pl.ds(off, BQ), pl.ds(2 * hh * HEAD_PAD, HEAD_PAD)], qs[hh], _NT,
                                   preferred_element_type=F32) for hh in range(HB)]
            mid = []
            for hh in range(HB):
                m, l, acc = carry[hh]
                st = jnp.where(_diag_mask(True), sts[hh], NEG_INF) if diag else sts[hh]
                m2 = jnp.maximum(m, jnp.max(st, axis=0, keepdims=True))
                al = jnp.exp2((m - m2) * SCALE_LOG2E)
                pt = jnp.exp2((st - m2) * SCALE_LOG2E)
                mid.append((m2, l * al + jnp.sum(pt, axis=0, keepdims=True), acc * al, pt.astype(BF16)))
            out = []
            for hh in range(HB):
                m2, l2, acc_s, ptb = mid[hh]
                v = kv_ref[pl.ds(off, BQ), pl.ds((2 * hh + 1) * HEAD_PAD, HEAD_PAD)]
                out.append((m2, l2, acc_s + lax.dot_general(v, ptb, _TN, preferred_element_type=F32)))
            return tuple(out)

        init = tuple((jnp.full((1, BQ), NEG_INF, F32), jnp.zeros((1, BQ), F32), jnp.zeros((HEAD_PAD, BQ), F32))
                     for _ in range(HB))
        carry = lax.fori_loop(0, qi, lambda j, c: step(j, c, False), init)
        carry = step(qi, carry, True)
        for hh in range(HB):
            m, l, acc = carry[hh]
            o_ref[:, hh * HEAD_PAD:(hh + 1) * HEAD_PAD] = (acc / l).T
            lse_ref[hh] = m * SCALE_LOG2E + jnp.log(l) * LOG2E

    return pl.pallas_call(
        body, name="attn_fwd", grid=(N_HEADS // HB, nq),
        in_specs=[pl.BlockSpec((BQ, HB * HEAD_PAD), lambda g, i: (i, g)),
                  pl.BlockSpec((S, 2 * HB * HEAD_PAD), lambda g, i: (0, g))],
        out_specs=[pl.BlockSpec((BQ, HB * HEAD_PAD), lambda g, i: (i, g)),
                   pl.BlockSpec((HB, None, 1, BQ), lambda g, i: (g, i, 0, 0))],
        out_shape=[jax.ShapeDtypeStruct((S, QW), F32), jax.ShapeDtypeStruct((N_HEADS, nq, 1, BQ), F32)],
        compiler_params=_cparams("parallel", "arbitrary"))(qf, kvf)


def _attn_delta(dob, o):
    S = o.shape[0]

    def fn(dov, ov):
        prod = dov.astype(F32) * ov
        lane = lax.broadcasted_iota(jnp.int32, (prod.shape[0], HEAD_PAD), 1)
        out = jnp.zeros((prod.shape[0], HEAD_PAD), F32)
        for h in range(N_HEADS):
            out = jnp.where(lane == h, jnp.sum(prod[:, h * HEAD_PAD:(h + 1) * HEAD_PAD], axis=1, keepdims=True), out)
        return (out,), ()

    (dd,), _ = _rowk("attn_delta", fn, [dob, o], [], [(HEAD_PAD, F32)], [])
    return dd[:, :N_HEADS].T.reshape(N_HEADS, S // BQ, 1, BQ)


_TN = (((0,), (0,)), ((), ()))


def _attn_bwd(qf, kvf, dob, lse_row, dd_row, cos, sa, sb):
    S = qf.shape[0]
    nq = S // BQ

    def body(kv_ref, q_ref, do_ref, lse_ref, dd_ref, cos_ref, sa_ref, sb_ref, dq_ref, dkv_ref):
        kj = pl.program_id(1)

        @pl.when(kj == 0)
        def _():
            dq_ref[...] = jnp.zeros(dq_ref.shape, F32)

        ks = [kv_ref[:, 2 * hh * HEAD_PAD:(2 * hh + 1) * HEAD_PAD] for hh in range(HB)]
        vs = [kv_ref[:, (2 * hh + 1) * HEAD_PAD:(2 * hh + 2) * HEAD_PAD] for hh in range(HB)]

        def step(i, carry, diag):
            off = pl.multiple_of(i * BQ, BQ)
            cols = [pl.ds(hh * HEAD_PAD, HEAD_PAD) for hh in range(HB)]
            q = [q_ref[pl.ds(off, BQ), cols[hh]] for hh in range(HB)]
            do = [do_ref[pl.ds(off, BQ), cols[hh]] for hh in range(HB)]
            sts = [lax.dot_general(ks[hh], q[hh], _NT, preferred_element_type=F32) for hh in range(HB)]
            dpts = [lax.dot_general(vs[hh], do[hh], _NT, preferred_element_type=F32) for hh in range(HB)]
            mid = []
            for hh in range(HB):
                st = jnp.where(_diag_mask(True), sts[hh], NEG_INF) if diag else sts[hh]
                pt = jnp.exp2(st * SCALE_LOG2E - lse_ref[hh, i])
                mid.append((pt.astype(BF16), (pt * (dpts[hh] - dd_ref[hh, i])).astype(BF16)))
            out = []
            for hh in range(HB):
                dk, dv = carry[hh]
                ptb, dsb = mid[hh]
                dv2 = dv + lax.dot_general(ptb, do[hh], _NN, preferred_element_type=F32)
                dk2 = dk + lax.dot_general(dsb, q[hh], _NN, preferred_element_type=F32)
                dq_ref[pl.ds(off, BQ), cols[hh]] += lax.dot_general(dsb, ks[hh], _TN, preferred_element_type=F32)
                out.append((dk2, dv2))
            return tuple(out)

        zero = jnp.zeros((BQ, HEAD_PAD), F32)
        carry = step(kj, tuple((zero, zero) for _ in range(HB)), True)
        carry = lax.fori_loop(kj + 1, nq, lambda i, c: step(i, c, False), carry)
        done = pl.ds(pl.multiple_of(kj * BQ, BQ), BQ)
        for hh in range(HB):
            dk, dv = carry[hh]
            dk = _rope_t(dk * ATT_SCALE, cos_ref[...], sa_ref[...], sb_ref[...])
            dkv_ref[:, 2 * hh * HEAD_PAD:(2 * hh + 1) * HEAD_PAD] = dk.astype(BF16)
            dkv_ref[:, (2 * hh + 1) * HEAD_PAD:(2 * hh + 2) * HEAD_PAD] = dv.astype(BF16)
            cols = pl.ds(hh * HEAD_PAD, HEAD_PAD)
            dq_ref[done, cols] = _rope_t(dq_ref[done, cols] * ATT_SCALE, cos_ref[...], sa_ref[...], sb_ref[...])

    tab = pl.BlockSpec((BQ, HEAD_PAD), lambda g, j: (j, 0))
    row = pl.BlockSpec((HB, nq, 1, BQ), lambda g, j: (g, 0, 0, 0))
    seq = pl.BlockSpec((S, HB * HEAD_PAD), lambda g, j: (0, g), pipeline_mode=pl.Buffered(1))
    kvb = pl.BlockSpec((BQ, 2 * HB * HEAD_PAD), lambda g, j: (j, g))
    return pl.pallas_call(
        body, name="attn_bwd", grid=(N_HEADS // HB, nq),
        in_specs=[kvb, seq, seq, row, row, tab, tab, tab],
        out_specs=[seq, kvb],
        out_shape=[jax.ShapeDtypeStruct((S, QW), F32), jax.ShapeDtypeStruct((S, KVW), BF16)],
        compiler_params=_cparams("parallel", "arbitrary"))(kvf, qf, dob, lse_row, dd_row, cos, sa, sb)


DC = 128
TR = 256


def _dwconv_fwd(u, w, b):
    S, Dm = u.shape
    tr = min(TR, S)

    def body(u_ref, w_ref, b_ref, o_ref, pad_ref):
        pad_ref[pl.ds(0, PAD_ROWS), :] = jnp.zeros((PAD_ROWS, DC), F32)
        pad_ref[pl.ds(PAD_ROWS, S), :] = u_ref[...]
        wv = w_ref[...]
        for r in range(S // tr):
            acc = jnp.broadcast_to(b_ref[...], (tr, DC))
            for j in range(CONV_WIDTH):
                acc = acc + wv[j:j + 1, :] * pad_ref[pl.ds(r * tr + PAD_ROWS - (CONV_WIDTH - 1) + j, tr), :]
            o_ref[pl.ds(r * tr, tr), :] = acc

    return pl.pallas_call(
        body, name="dwconv_fwd", grid=(Dm // DC,),
        in_specs=[pl.BlockSpec((S, DC), lambda c: (0, c)), pl.BlockSpec((CONV_WIDTH, DC), lambda c: (0, c)),
                  pl.BlockSpec((1, DC), lambda c: (0, c))],
        out_specs=pl.BlockSpec((S, DC), lambda c: (0, c)),
        out_shape=jax.ShapeDtypeStruct((S, Dm), F32),
        scratch_shapes=[pltpu.VMEM((S + PAD_ROWS, DC), F32)],
        compiler_params=_cparams("parallel"))(u, w, b)


def _dwconv_bwd(d, u, w):
    S, Dm = u.shape
    tr = min(TR, S)

    def body(d_ref, u_ref, w_ref, du_ref, dw_ref, padd_ref, padu_ref):
        padd_ref[pl.ds(0, S), :] = d_ref[...]
        padd_ref[pl.ds(S, PAD_ROWS), :] = jnp.zeros((PAD_ROWS, DC), F32)
        padu_ref[pl.ds(0, PAD_ROWS), :] = jnp.zeros((PAD_ROWS, DC), F32)
        padu_ref[pl.ds(PAD_ROWS, S), :] = u_ref[...]
        wv = w_ref[...]
        dws = [jnp.zeros((1, DC), F32) for _ in range(CONV_WIDTH)]
        for r in range(S // tr):
            acc = jnp.zeros((tr, DC), F32)
            for j in range(CONV_WIDTH):
                acc = acc + wv[j:j + 1, :] * padd_ref[pl.ds(r * tr + (CONV_WIDTH - 1) - j, tr), :]
            du_ref[pl.ds(r * tr, tr), :] = acc
            dt = d_ref[pl.ds(r * tr, tr), :]
            for j in range(CONV_WIDTH):
                ut = padu_ref[pl.ds(r * tr + PAD_ROWS - (CONV_WIDTH - 1) + j, tr), :]
                dws[j] = dws[j] + _colsum(dt * ut)
        for j in range(CONV_WIDTH):
            dw_ref[pl.ds(j, 1), :] = dws[j]
        dw_ref[pl.ds(CONV_WIDTH, 1), :] = jnp.zeros((1, DC), F32)

    blk = pl.BlockSpec((S, DC), lambda c: (0, c))
    return pl.pallas_call(
        body, name="dwconv_bwd", grid=(Dm // DC,),
        in_specs=[blk, blk, pl.BlockSpec((CONV_WIDTH, DC), lambda c: (0, c))],
        out_specs=[blk, pl.BlockSpec((PAD_ROWS, DC), lambda c: (0, c))],
        out_shape=[jax.ShapeDtypeStruct((S, Dm), F32), jax.ShapeDtypeStruct((PAD_ROWS, Dm), F32)],
        scratch_shapes=[pltpu.VMEM((S + PAD_ROWS, DC), F32), pltpu.VMEM((S + PAD_ROWS, DC), F32)],
        compiler_params=_cparams("parallel"))(d, u, w)


POOL_C = D_MODEL // len(POOL_WINDOWS)


def _pool_counts(r, tr, win):
    t = r * tr + lax.broadcasted_iota(jnp.int32, (tr, 1), 0)
    return jnp.minimum(t + 1, win).astype(F32)


def _pool_fwd(h):
    S, Dm = h.shape
    tr = min(TR, S)

    def body(h_ref, o_ref, pad_ref):
        pad_ref[pl.ds(0, PAD_ROWS), :] = jnp.zeros((PAD_ROWS, POOL_C), F32)
        pad_ref[pl.ds(PAD_ROWS, S), :] = h_ref[...]
        for g, win in enumerate(POOL_WINDOWS):
            @pl.when(pl.program_id(0) == g)
            def _():
                for r in range(S // tr):
                    acc = pad_ref[pl.ds(r * tr + PAD_ROWS, tr), :]
                    for j in range(1, win):
                        acc = acc + pad_ref[pl.ds(r * tr + PAD_ROWS - j, tr), :]
                    pooled = acc / _pool_counts(r, tr, win)
                    o_ref[pl.ds(r * tr, tr), :] = (pooled - h_ref[pl.ds(r * tr, tr), :]).astype(BF16)

    blk = pl.BlockSpec((S, POOL_C), lambda g: (0, g))
    return pl.pallas_call(
        body, name="pool_fwd", grid=(len(POOL_WINDOWS),), in_specs=[blk], out_specs=blk,
        out_shape=jax.ShapeDtypeStruct((S, Dm), BF16),
        scratch_shapes=[pltpu.VMEM((S + PAD_ROWS, POOL_C), F32)],
        compiler_params=_cparams("parallel"))(h)


def _pool_bwd(dp):
    S, Dm = dp.shape
    tr = min(TR, S)

    def body(d_ref, o_ref, pad_ref):
        pad_ref[pl.ds(S, PAD_ROWS), :] = jnp.zeros((PAD_ROWS, POOL_C), F32)
        for g, win in enumerate(POOL_WINDOWS):
            @pl.when(pl.program_id(0) == g)
            def _():
                for r in range(S // tr):
                    pad_ref[pl.ds(r * tr, tr), :] = d_ref[pl.ds(r * tr, tr), :] / _pool_counts(r, tr, win)
                for r in range(S // tr):
                    acc = pad_ref[pl.ds(r * tr, tr), :]
                    for j in range(1, win):
                        acc = acc + pad_ref[pl.ds(r * tr + j, tr), :]
                    o_ref[pl.ds(r * tr, tr), :] = acc - d_ref[pl.ds(r * tr, tr), :]

    blk = pl.BlockSpec((S, POOL_C), lambda g: (0, g))
    return pl.pallas_call(
        body, name="pool_bwd", grid=(len(POOL_WINDOWS),), in_specs=[blk], out_specs=blk,
        out_shape=jax.ShapeDtypeStruct((S, Dm), F32),
        scratch_shapes=[pltpu.VMEM((S + PAD_ROWS, POOL_C), F32)],
        compiler_params=_cparams("parallel"))(dp)


def _bias_spec(tn):
    return pl.BlockSpec((1, tn), lambda i, j, k: (0, j))


def _mla_weights(w_dq, w_dkv, w_uq, w_ukv, w_o):
    wd = jnp.concatenate([w_dq, w_dkv], axis=1)
    wq = jnp.pad(w_uq.reshape(Q_LORA, N_HEADS, QK_NOPE + QK_ROPE),
                 ((0, 0), (0, 0), (0, HEAD_PAD - QK_NOPE - QK_ROPE))).reshape(Q_LORA, QW)
    ukv = w_ukv.reshape(KV_LORA, N_HEADS, QK_NOPE + V_HEAD)
    wkv = jnp.zeros((DKV, N_HEADS, 2 * HEAD_PAD), BF16)
    wkv = wkv.at[:KV_LORA, :, :QK_NOPE].set(ukv[:, :, :QK_NOPE])
    wkv = wkv.at[:KV_LORA, :, HEAD_PAD:HEAD_PAD + V_HEAD].set(ukv[:, :, QK_NOPE:])
    eye = jnp.broadcast_to(jnp.eye(QK_ROPE, dtype=BF16)[:, None, :], (QK_ROPE, N_HEADS, QK_ROPE))
    wkv = wkv.at[KV_LORA:, :, QK_NOPE:QK_NOPE + QK_ROPE].set(eye).reshape(DKV, KVW)
    wo = jnp.pad(w_o.reshape(N_HEADS, V_HEAD, D_MODEL),
                 ((0, 0), (0, HEAD_PAD - V_HEAD), (0, 0))).reshape(QW, D_MODEL)
    return dict(wd=wd, wq=wq, wkv=wkv, wo=wo)


def _mla_weight_grads(g_wd, g_wq, g_wkv, g_wo):
    g_uq = g_wq.reshape(Q_LORA, N_HEADS, HEAD_PAD)[:, :, :QK_NOPE + QK_ROPE].reshape(Q_LORA, -1)
    t = g_wkv.reshape(DKV, N_HEADS, 2 * HEAD_PAD)[:KV_LORA]
    g_ukv = jnp.concatenate([t[:, :, :QK_NOPE], t[:, :, HEAD_PAD:HEAD_PAD + V_HEAD]], axis=2)
    g_o = g_wo.reshape(N_HEADS, HEAD_PAD, D_MODEL)[:, :V_HEAD].reshape(N_HEADS * V_HEAD, D_MODEL)
    return dict(mla_w_dq=g_wd[:, :Q_LORA], mla_w_uq=g_uq, mla_w_dkv=g_wd[:, Q_LORA:],
                mla_w_ukv=g_ukv.reshape(KV_LORA, -1), mla_w_o=g_o)


def _rope_epilogue(kv):
    def epi(acc, cos, sa, sb):
        parts = []
        for t in range(acc.shape[1] // HEAD_PAD):
            x = acc[:, t * HEAD_PAD:(t + 1) * HEAD_PAD]
            parts.append(x if (kv and t % 2) else _rope(x, cos, sa, sb))
        return (jnp.concatenate(parts, axis=1),)
    return epi


def _mla_fwd(tag, h, P, rope):
    S = h.shape[0]
    cos, sa, sb = rope
    tp = min(512, S)
    tabs = [pl.BlockSpec((tp, HEAD_PAD), lambda i, j, k: (i, 0))] * 3
    cqkv = _mm(f"mla_down{tag}", h, P['wd'], 'nn', S, DQKV, D_MODEL)

    def norms(x, qg, kg):
        xq, xk, xr = x[:, :Q_LORA], x[:, Q_LORA:Q_LORA + KV_LORA], x[:, Q_LORA + KV_LORA:]
        cq = xq * lax.rsqrt(_rowmean(xq * xq) + NORM_EPS) * qg
        ck = xk * lax.rsqrt(_rowmean(xk * xk) + NORM_EPS) * kg
        return (cq, jnp.concatenate([ck, xr], axis=1)), ()

    (cq, ckv), _ = _rowk(f"mla_norms{tag}", norms, [cqkv], [P['qg'], P['kg']], [(Q_LORA, BF16), (DKV, BF16)], [])
    qf = _mm(f"mla_q{tag}", cq, P['wq'], 'nn', S, QW, Q_LORA, tm=tp, tn=QW, extras=[cos, sa, sb],
             extra_specs=tabs, epi=_rope_epilogue(False), outs=[jax.ShapeDtypeStruct((S, QW), BF16)])
    kvf = _mm(f"mla_kv{tag}", ckv, P['wkv'], 'nn', S, KVW, DKV, tm=tp, tn=KVW, extras=[cos, sa, sb],
              extra_specs=tabs, epi=_rope_epilogue(True), outs=[jax.ShapeDtypeStruct((S, KVW), BF16)])
    o, lse = _attn_fwd(qf, kvf)
    y = _mm(f"mla_o{tag}", o, P['wo'], 'nn', S, D_MODEL, QW)
    return y, dict(cqkv=cqkv, cq=cq, ckv=ckv, qf=qf, kvf=kvf, o=o, lse=lse)


def _mla_bwd(tag, dy, h, sv, P, rope):
    S = h.shape[0]
    nq = S // BQ
    cos, sa, sb = rope
    g_wo = _mm(f"mla_o_wg{tag}", sv['o'], dy, 'tn', QW, D_MODEL, S)
    dob = _mm(f"mla_o_dg{tag}", dy, P['wo'], 'nt', S, QW, D_MODEL, outs=[jax.ShapeDtypeStruct((S, QW), BF16)])
    dd = _attn_delta(dob, sv['o'])
    dq, dkv = _attn_bwd(sv['qf'], sv['kvf'], dob, sv['lse'], dd, cos, sa, sb)
    g_wq = _mm(f"mla_q_wg{tag}", sv['cq'], dq, 'tn', Q_LORA, QW, S)
    dcq = _mm(f"mla_q_dg{tag}", dq, P['wq'], 'nt', S, Q_LORA, QW)
    g_wkv = _mm(f"mla_kv_wg{tag}", sv['ckv'], dkv, 'tn', DKV, KVW, S)
    dckv = _mm(f"mla_kv_dg{tag}", dkv, P['wkv'], 'nt', S, DKV, KVW)

    def norms_bwd(dcq_v, dckv_v, x, qg, kg):
        xq, xk = x[:, :Q_LORA], x[:, Q_LORA:Q_LORA + KV_LORA]
        dxq, dqg = _gain_bwd(dcq_v, xq, qg)
        dxk, dkg = _gain_bwd(dckv_v[:, :KV_LORA], xk, kg)
        return (jnp.concatenate([dxq, dxk, dckv_v[:, KV_LORA:]], axis=1),), (dqg, dkg)

    (dcqkv,), (dqg, dkg) = _rowk(f"mla_norms_bwd{tag}", norms_bwd, [dcq, dckv, sv['cqkv']], [P['qg'], P['kg']],
                                 [(DQKV, BF16)], [Q_LORA, KV_LORA])
    g_wd = _mm(f"mla_down_wg{tag}", h, dcqkv, 'tn', D_MODEL, DQKV, S)
    dh = _mm(f"mla_down_dg{tag}", dcqkv, P['wd'], 'nt', S, D_MODEL, DQKV)
    grads = _mla_weight_grads(g_wd, g_wq, g_wkv, g_wo)
    grads.update(mla_q_norm_g=dqg.reshape(-1), mla_kv_norm_g=dkg.reshape(-1))
    return dh, grads


def _conv_fwd(h, P):
    S = h.shape[0]
    a = _mm("conv_pw1", h, P['w_pw1'], 'nn', S, 2 * D_MODEL, D_MODEL, extras=[P['b_pw1']],
            extra_specs=[_bias_spec(1024)], epi=lambda acc, b: (acc + b,))
    (u0,), _ = _rowk("conv_glu", lambda av: ((av[:, :D_MODEL] * _sigmoid(av[:, D_MODEL:]),), ()),
                     [a], [], [(D_MODEL, F32)], [])
    u1 = _dwconv_fwd(u0, P['w_dw'], P['b_dw'])

    def ln_silu(u, g, b):
        xc = u - _rowmean(u)
        z = xc * lax.rsqrt(_rowmean(xc * xc) + NORM_EPS) * g + b
        return (z * _sigmoid(z),), ()

    (u3,), _ = _rowk("conv_ln", ln_silu, [u1], [P['ln_g'], P['ln_b']], [(D_MODEL, BF16)], [])
    y = _mm("conv_pw2", u3, P['w_pw2'], 'nn', S, D_MODEL, D_MODEL, extras=[P['b_pw2']],
            extra_specs=[_bias_spec(1024)], epi=lambda acc, b: (acc + b,))
    return y, dict(a=a, u0=u0, u1=u1, u3=u3)


def _conv_bwd(dy, dy_colsum, h, sv, P):
    S = h.shape[0]
    g_pw2 = _mm("conv_pw2_wg", sv['u3'], dy, 'tn', D_MODEL, D_MODEL, S)
    du3 = _mm("conv_pw2_dg", dy, P['w_pw2'], 'nt', S, D_MODEL, D_MODEL)

    def ln_bwd(d3, u, g, b):
        xc = u - _rowmean(u)
        rstd = lax.rsqrt(_rowmean(xc * xc) + NORM_EPS)
        xh = xc * rstd
        z = xh * g + b
        sg = _sigmoid(z)
        dz = d3 * (sg * (1.0 + z * (1.0 - sg)))
        dxh = dz * g
        du = rstd * (dxh - _rowmean(dxh) - xh * _rowmean(dxh * xh))
        return (du,), (_colsum(dz * xh), _colsum(dz), _colsum(du))

    (du1,), (d_lng, d_lnb, d_bdw) = _rowk("conv_ln_bwd", ln_bwd, [du3, sv['u1']], [P['ln_g'], P['ln_b']],
                                          [(D_MODEL, F32)], [D_MODEL] * 3)
    du0, d_wdw = _dwconv_bwd(du1, sv['u0'], P['w_dw'])

    def glu_bwd(d0, av):
        a1, sg = av[:, :D_MODEL], _sigmoid(av[:, D_MODEL:])
        da = jnp.concatenate([d0 * sg, d0 * a1 * sg * (1.0 - sg)], axis=1)
        return (da,), (_colsum(da),)

    (da,), (d_bpw1,) = _rowk("conv_glu_bwd", glu_bwd, [du0, sv['a']], [], [(2 * D_MODEL, BF16)], [2 * D_MODEL])
    g_pw1 = _mm("conv_pw1_wg", h, da, 'tn', D_MODEL, 2 * D_MODEL, S)
    dh = _mm("conv_pw1_dg", da, P['w_pw1'], 'nt', S, D_MODEL, 2 * D_MODEL)
    grads = dict(conv_w_pw1=g_pw1, conv_b_pw1=d_bpw1.reshape(-1), conv_w_dw=d_wdw[:CONV_WIDTH],
                 conv_b_dw=d_bdw.reshape(-1), conv_ln_g=d_lng.reshape(-1), conv_ln_b=d_lnb.reshape(-1),
                 conv_w_pw2=g_pw2, conv_b_pw2=dy_colsum.reshape(-1))
    return dh, grads


def _pool_group_specs(tm):
    return (pl.BlockSpec((tm, POOL_C), lambda i, j, k: (i, j)),
            pl.BlockSpec((None, POOL_C, POOL_C), lambda i, j, k: (j, 0, 0)))


def _pool_mixer_fwd(h, P):
    S = h.shape[0]
    p = _pool_fwd(h)
    a_spec, b_spec = _pool_group_specs(min(1024, S))
    y, z = _mm("pool_mm", p, P['w'], 'nn', S, D_MODEL, POOL_C, tn=POOL_C, a_spec=a_spec, b_spec=b_spec,
               extras=[P['b'], P['scale']], extra_specs=[_bias_spec(POOL_C)] * 2,
               epi=lambda acc, b, s: ((acc + b) * s, acc + b),
               outs=[jax.ShapeDtypeStruct((S, D_MODEL), F32)] * 2)
    return y, dict(p=p, z=z)


def _pool_mixer_bwd(dy, sv, P):
    S = dy.shape[0]

    def scale_bwd(d, z, s):
        dz = d * s
        return (dz,), (_colsum(d * z), _colsum(dz))

    (dz,), (d_scale, d_b) = _rowk("pool_scale_bwd", scale_bwd, [dy, sv['z']], [P['scale']],
                                  [(D_MODEL, BF16)], [D_MODEL] * 2)
    a_spec, b_spec = _pool_group_specs(min(1024, S))
    dp = _mm("pool_mm_dg", dz, P['w'], 'nt', S, D_MODEL, POOL_C, tn=POOL_C, a_spec=a_spec, b_spec=b_spec)
    tk = min(512, S)
    grp = pl.BlockSpec((tk, POOL_C), lambda i, j, k: (k, j))
    g_w = _mm("pool_mm_wg", sv['p'], dz, 'tn', POOL_C, D_MODEL, S, tn=POOL_C, tk=tk, a_spec=grp, b_spec=grp,
              outs=[jax.ShapeDtypeStruct((len(POOL_WINDOWS), POOL_C, POOL_C), F32)],
              out_specs=[pl.BlockSpec((None, POOL_C, POOL_C), lambda i, j, k: (j, 0, 0))])
    dh = _pool_bwd(dp)
    return dh, dict(pool_w=g_w, pool_b=d_b.reshape(-1), pool_scale=d_scale.reshape(-1))


def _adamw(w, g, m, v):
    m2 = ADAM_B1 * m + (1.0 - ADAM_B1) * g
    v2 = ADAM_B2 * v + (1.0 - ADAM_B2) * (g * g)
    m_hat = m2 / (1.0 - ADAM_B1 ** ADAM_STEP)
    v_hat = v2 / (1.0 - ADAM_B2 ** ADAM_STEP)
    delta = -ADAM_LR * (m_hat / (jnp.sqrt(v_hat) + ADAM_EPS) + ADAM_WD * w)
    return delta, m2, v2


def _finish(name, w, land, m, v, layer=None, prev=None):
    local = land.shape[1:]
    C = local[-1]
    R = land[0].size // C
    tr = 64 if R % 64 == 0 else R

    def body(land_hbm, g_hbm, land_v, g_v, recv_v, io_sem, send_sem, recv_sem):
        load = pltpu.make_async_copy(land_hbm, land_v, io_sem)
        load.start()
        load.wait()

        def rows_of(i):
            return pl.ds(pl.multiple_of(i * tr, tr), tr)

        def sum_chunk(i, carry):
            rows = rows_of(i)
            g_v[rows, :] = ((land_v[0, rows, :].astype(F32) + land_v[1, rows, :].astype(F32))
                            + land_v[2, rows, :].astype(F32)) + land_v[3, rows, :].astype(F32)
            return carry

        lax.fori_loop(0, R // tr, sum_chunk, 0)
        swap = pltpu.make_async_remote_copy(
            src_ref=g_v, dst_ref=recv_v, send_sem=send_sem, recv_sem=recv_sem,
            device_id=(lax.axis_index("x"), lax.axis_index("y"), 1 - lax.axis_index("c")),
            device_id_type=pl.DeviceIdType.MESH)
        swap.start()
        swap.wait()

        def add_chunk(i, carry):
            rows = rows_of(i)
            recv_v[rows, :] = g_v[rows, :] + recv_v[rows, :]
            return carry

        lax.fori_loop(0, R // tr, add_chunk, 0)
        store = pltpu.make_async_copy(recv_v, g_hbm, io_sem)
        store.start()
        store.wait()

    any_spec = pl.BlockSpec(memory_space=pl.ANY)
    g = pl.pallas_call(
        body, name=name, in_specs=[any_spec], out_specs=any_spec, out_shape=jax.ShapeDtypeStruct((R, C), F32),
        scratch_shapes=[pltpu.VMEM((4, R, C), BF16), pltpu.VMEM((R, C), F32), pltpu.VMEM((R, C), F32),
                        pltpu.SemaphoreType.DMA, pltpu.SemaphoreType.DMA, pltpu.SemaphoreType.DMA],
        compiler_params=pltpu.CompilerParams(has_side_effects=True, vmem_limit_bytes=VMEM_LIMIT))(
            land.reshape(4, R, C))

    lead = () if layer is None else (w.shape[0],)
    as2d = lambda a: a.reshape(lead + (R, C))
    tu = _row_tile(R, C, budget=1 << 19)
    tile = pl.BlockSpec((tu, C), lambda i: (i, 0))
    slab = tile if layer is None else pl.BlockSpec((None, tu, C), lambda i: (layer, i, 0))
    n_prev = 0 if prev is None else 4

    def update(w_ref, g_ref, m_ref, v_ref, *rest):
        outs = rest[n_prev:]
        gv = g_ref[...]
        d, nm, nv = _adamw(w_ref[...], gv, m_ref[...], v_ref[...])
        for r, val in zip(outs, (gv, d, nm, nv)):
            r[...] = val

    res = pl.pallas_call(
        update, name=name + "_adamw", grid=(R // tu,),
        in_specs=[slab, tile, slab, slab] + [any_spec] * n_prev, out_specs=[slab] * 4,
        out_shape=[jax.ShapeDtypeStruct(lead + (R, C), F32)] * 4,
        input_output_aliases={4 + k: k for k in range(n_prev)},
        compiler_params=_cparams("arbitrary"))(
            as2d(w), g, as2d(m), as2d(v), *([] if prev is None else [as2d(p) for p in prev]))
    return [r.reshape(w.shape) for r in res]


def _row(v):
    return v.reshape(1, -1)


def kernel(x, c, positions, ada_w, ada_b, norm_g, mla_w_dq, mla_q_norm_g, mla_w_uq, mla_w_dkv, mla_kv_norm_g, mla_w_ukv, mla_w_o, conv_w_pw1, conv_b_pw1, conv_w_dw, conv_b_dw, conv_ln_g, conv_ln_b, conv_w_pw2, conv_b_pw2, pool_w, pool_b, pool_scale, ffn_w1, ffn_w2, loss_target, m_ada_w, m_ada_b, m_norm_g, m_mla_w_dq, m_mla_q_norm_g, m_mla_w_uq, m_mla_w_dkv, m_mla_kv_norm_g, m_mla_w_ukv, m_mla_w_o, m_conv_w_pw1, m_conv_b_pw1, m_conv_w_dw, m_conv_b_dw, m_conv_ln_g, m_conv_ln_b, m_conv_w_pw2, m_conv_b_pw2, m_pool_w, m_pool_b, m_pool_scale, m_ffn_w1, m_ffn_w2, v_ada_w, v_ada_b, v_norm_g, v_mla_w_dq, v_mla_q_norm_g, v_mla_w_uq, v_mla_w_dkv, v_mla_kv_norm_g, v_mla_w_ukv, v_mla_w_o, v_conv_w_pw1, v_conv_b_pw1, v_conv_w_dw, v_conv_b_dw, v_conv_ln_g, v_conv_ln_b, v_conv_w_pw2, v_conv_b_pw2, v_pool_w, v_pool_b, v_pool_scale, v_ffn_w1, v_ffn_w2):
    args = dict(locals())
    W = {n: args[n] for n in WEIGHTS}
    MOM = {n: args['m_' + n] for n in WEIGHTS}
    VAR = {n: args['v_' + n] for n in WEIGHTS}
    S = x.shape[1]
    xs = x.reshape(S, D_MODEL)
    tgt = loss_target.reshape(S, D_MODEL)
    mx, my, mc = lax.axis_index("x"), lax.axis_index("y"), lax.axis_index("c")
    chip = 2 * mx + my
    n_sh = ada_w.shape[2]

    def sent_of(key):
        n, l = key
        arr = W[n] if l is None else W[n][l]
        return arr.astype(BF16) if n in BIG or n in ('ffn_w1', 'ffn_w2') else arr

    keys0 = [(n, 0) for n in MLA_MATS] + [(n, None) for n in ('norm_g', 'mla_q_norm_g', 'mla_kv_norm_g',
                                                               'conv_w_dw', 'pool_b', 'pool_scale')]
    c8 = _exchange("gather_c", [c.reshape(8, D_MODEL // 8)], 'xyc')[0].reshape(8, D_MODEL)
    sent0 = [sent_of(k) for k in keys0]
    sent0[-1] = sent0[-1] + jnp.minimum(jnp.abs(c8[0, 0]), 0.0)
    fly0 = _split_start("gather_w0_start", sent0)
    c8 = c8 + fly0['token'][0, 0]
    c8 = jnp.pad(c8, ((0, ADA_ROWS - 8), (0, 0)))
    silu = lambda v: v * _sigmoid(v)
    mod_sh = []
    for l in range(DEPTH):
        b_l = lax.dynamic_slice(ada_b[l], (chip * n_sh,), (n_sh,)).reshape(1, n_sh)
        mod_sh.append(_mm(f"ada_fwd{l}", c8, ada_w, 'nn', ADA_ROWS, n_sh, D_MODEL, tn=n_sh // 2, tk=512, pro_a=silu,
                          b_spec=pl.BlockSpec((None, 512, n_sh // 2), lambda i, j, k, l=l: (l, k, j)),
                          extras=[b_l], extra_specs=[_bias_spec(n_sh // 2)], epi=lambda acc, b: (acc + b,))[:8])
    mod_sh = jnp.stack(mod_sh, axis=1).reshape(8, DEPTH * n_sh // 128, 128)
    mod = _exchange("scatter_mod", [mod_sh], 'xy', src_by='xyc')[0]
    mod = mod.reshape(4, DEPTH, n_sh).transpose(1, 0, 2).reshape(DEPTH, 6, 1, D_MODEL)

    keys1 = [('ffn_w1', 0), ('ffn_w2', 0), ('conv_w_pw1', None), ('conv_w_pw2', None), ('pool_w', None)]
    keys2 = [(n, l) for l in range(1, DEPTH) for n in ('ffn_w1', 'ffn_w2')] + [(n, 1) for n in MLA_MATS]
    fly1 = _split_start("gather_w1_start", [sent_of(k) for k in keys1])
    fly2 = _split_start("gather_w2_start", [sent_of(k) for k in keys2])
    mod = mod + (fly1['token'][0, 0] + fly2['token'][0, 0])
    G = dict(zip(keys0, _split_wait("gather_w0_wait", fly0, mod)))

    def whole(key):
        n, l = key
        return _unshard(G[key], SHARD_AXIS[n] - (0 if l is None else 1))

    def mla_params(j):
        P = _mla_weights(*[whole((n, j)) for n in ('mla_w_dq', 'mla_w_dkv', 'mla_w_uq', 'mla_w_ukv', 'mla_w_o')])
        P.update(qg=_row(whole(('mla_q_norm_g', None))[j]), kg=_row(whole(('mla_kv_norm_g', None))[j]))
        return P

    gains = whole(('norm_g', None))
    mla_p = {0: mla_params(0)}
    conv_p = pool_p = None
    rope = _rope_tables(positions.reshape(S, 1).astype(F32))

    by_j = pl.BlockSpec((None, 1024, 1024), lambda i, j, k: (j, 0, 0))
    by_k = pl.BlockSpec((None, 1024, 1024), lambda i, j, k: (k, 0, 0))
    sq_relu = lambda v: jnp.square(jnp.maximum(v, 0.0))

    def md(i, k):
        return mod[i, k]

    (h,), _ = _rowk("pre0", lambda xv, g, sc, sh: ((_pre_fwd(xv, g, sc, sh),), ()),
                    [xs], [_row(gains[0, 0]), md(0, 1), md(0, 0)], [(D_MODEL, BF16)], [])
    saved = []
    xin = xs
    loss_acc = dxf = None
    for i in range(DEPTH):
        kind, j = i % 3, i // 3
        if kind == 0:
            if j not in mla_p:
                mla_p[j] = mla_params(j)
            y, sv = _mla_fwd(j, h, mla_p[j], rope)
        elif kind == 1:
            y, sv = _conv_fwd(h, conv_p)
        else:
            y, sv = _pool_mixer_fwd(h, pool_p)

        def mid(xv, yv, gt, g1, g2, sc, sh):
            x1 = _post_fwd(xv, yv, gt, g1)
            return (x1, _pre_fwd(x1, g2, sc, sh)), ()

        (x1, h2), _ = _rowk(f"mid{i}", mid, [xin, y], [md(i, 2), _row(gains[i, 1]), _row(gains[i, 2]), md(i, 4), md(i, 3)],
                            [(D_MODEL, F32), (D_MODEL, BF16)], [])
        if i == 0:
            G.update(zip(keys1, _split_wait("gather_w1_wait", fly1, h2)))
            conv_p = dict(w_pw1=whole(('conv_w_pw1', None))[0], b_pw1=_row(conv_b_pw1[0]),
                          w_dw=whole(('conv_w_dw', None))[0], b_dw=_row(conv_b_dw[0]), ln_g=_row(conv_ln_g[0]),
                          ln_b=_row(conv_ln_b[0]), w_pw2=whole(('conv_w_pw2', None))[0], b_pw2=_row(conv_b_pw2[0]))
            pool_p = dict(w=whole(('pool_w', None))[0], b=_row(whole(('pool_b', None))[0]),
                          scale=_row(whole(('pool_scale', None))[0]))
        if i == 1:
            G.update(zip(keys2, _split_wait("gather_w2_wait", fly2, h2)))
        a = _mm(f"ffn1_{i}", h2, G[('ffn_w1', i)], 'nn', S, D_FF, D_MODEL, tm=2048, b_spec=by_j,
                outs=[jax.ShapeDtypeStruct((S, D_FF), BF16)])
        y2 = _mm(f"ffn2_{i}", a, G[('ffn_w2', i)], 'nn', S, D_MODEL, D_FF, pro_a=sq_relu, b_spec=by_k)
        saved.append(dict(x0=xin, h=h, y=y, x1=x1, h2=h2, a=a, y2=y2, mix=sv))
        if i + 1 < DEPTH:
            def nxt(xv, yv, gt, g3, g0, sc, sh):
                x2 = _post_fwd(xv, yv, gt, g3)
                return (x2, _pre_fwd(x2, g0, sc, sh)), ()

            hdt = F32 if (i + 1) % 3 == 2 else BF16
            (xin, h), _ = _rowk(f"next{i}", nxt, [x1, y2],
                                [md(i, 5), _row(gains[i, 3]), _row(gains[i + 1, 0]), md(i + 1, 1), md(i + 1, 0)],
                                [(D_MODEL, F32), (D_MODEL, hdt)], [])
        else:
            def head(xv, yv, tv, gt, g3):
                err = _post_fwd(xv, yv, gt, g3) - tv
                per_row = jnp.sum(err * err, axis=1, keepdims=True) * (0.5 / D_MODEL)
                return (err * (1.0 / D_MODEL),), (jnp.broadcast_to(jnp.sum(per_row, axis=0, keepdims=True), (1, 128)),)

            (dxf,), (loss_acc,) = _rowk("loss_head", head, [x1, y2, tgt], [md(i, 5), _row(gains[i, 3])],
                                        [(D_MODEL, F32)], [128])

    small = {}
    big = {}
    landed = {}

    def keep(gm, layer):
        for n, g in gm.items():
            if n in BIG:
                g = g[None] if layer is None else g
                big[(n, layer)] = _to_shards(g, SHARD_AXIS[n] - (0 if layer is None else 1)).astype(BF16)
            else:
                small.setdefault(n, {})[layer or 0] = g

    d_mod = [None] * DEPTH
    d_gain = [None] * DEPTH
    dx = dxf
    for i in reversed(range(DEPTH)):
        kind, j = i % 3, i // 3
        sv = saved[i]
        def post2_bwd(d, yv, gt, g):
            dyv, d_gt, d_g = _post_bwd(d, yv, gt, g)
            return (dyv,), (d_gt, d_g)

        (dy2,), (d_gtf, d_g3) = _rowk(f"post2_bwd{i}", post2_bwd, [dx, sv['y2']], [md(i, 5), _row(gains[i, 3])],
                                      [(D_MODEL, BF16)], [D_MODEL] * 2)
        da = _mm(f"ffn2_dg{i}", dy2, G[('ffn_w2', i)], 'nt', S, D_FF, D_MODEL, tm=2048, b_spec=by_j, extras=[sv['a']],
                 extra_specs=[pl.BlockSpec((min(2048, S), 1024), lambda i_, j_, k_: (i_, j_))],
                 epi=lambda acc, av: (acc * (2.0 * jnp.maximum(av, 0.0)),),
                 outs=[jax.ShapeDtypeStruct((S, D_FF), BF16)])
        big[('ffn_w2', i)] = _mm(f"ffn2_wg{i}", sv['a'], dy2, 'tn', D_FF, D_MODEL, S, pro_a=sq_relu,
                        outs=[jax.ShapeDtypeStruct((4, 1024, D_MODEL), BF16)],
                        out_specs=[pl.BlockSpec((None, 1024, 1024), lambda i_, j_, k_: (i_, 0, j_))])
        big[('ffn_w1', i)] = _mm(f"ffn1_wg{i}", sv['h2'], da, 'tn', D_MODEL, D_FF, S,
                        outs=[jax.ShapeDtypeStruct((4, D_MODEL, 1024), BF16)],
                        out_specs=[pl.BlockSpec((None, 1024, 1024), lambda i_, j_, k_: (j_, i_, 0))])
        dh2 = _mm(f"ffn1_dg{i}", da, G[('ffn_w1', i)], 'nt', S, D_MODEL, D_FF, tm=2048, b_spec=by_k)
        if i == DEPTH - 1:
            keys_a = [('ffn_w1', i), ('ffn_w2', i)]
            fly_a = _split_start("scatter_ga_start", [big[k] for k in keys_a], src_by='xy')
            mod = mod + fly_a['token'][0, 0]
        if i == 0:
            keys_b = [k for k in big if k not in keys_a]
            fly_b = _split_start("scatter_gb_start", [big[k] for k in keys_b], src_by='xy')
            mod = mod + fly_b['token'][0, 0]

        def mid_bwd(d2, dh2v, x1v, yv, g2, scf, gtm, g1):
            dpre, d_sh, d_sc, d_g2 = _pre_bwd(dh2v, x1v, g2, scf)
            d1 = d2 + dpre
            dyv, d_gt, d_g1 = _post_bwd(d1, yv, gtm, g1)
            return (d1, dyv), (d_sh, d_sc, d_g2, d_gt, d_g1, _colsum(dyv))

        ydt = F32 if kind == 2 else BF16
        (dx1, dy), (d_shf, d_scf, d_g2, d_gtm, d_g1, dy_cs) = _rowk(
            f"mid_bwd{i}", mid_bwd, [dx, dh2, sv['x1'], sv['y']],
            [_row(gains[i, 2]), md(i, 4), md(i, 2), _row(gains[i, 1])],
            [(D_MODEL, F32), (D_MODEL, ydt)], [D_MODEL] * 6)
        if kind == 0:
            dh, gm = _mla_bwd(j, dy, sv['h'], sv['mix'], mla_p[j], rope)
            keep(gm, j)
        elif kind == 1:
            dh, gm = _conv_bwd(dy, dy_cs, sv['h'], sv['mix'], conv_p)
            keep(gm, None)
        else:
            dh, gm = _pool_mixer_bwd(dy, sv['mix'], pool_p)
            keep(gm, None)

        def pre_bwd(d1, dhv, x0v, g0, scm):
            dpre, d_sh, d_sc, d_g0 = _pre_bwd(dhv, x0v, g0, scm)
            return (d1 + dpre,), (d_sh, d_sc, d_g0)

        (dx,), (d_shm, d_scm, d_g0) = _rowk(f"pre_bwd{i}", pre_bwd, [dx1, dh, sv['x0']],
                                            [_row(gains[i, 0]), md(i, 1)], [(D_MODEL, F32)], [D_MODEL] * 3)
        d_mod[i] = jnp.concatenate([d_shm, d_scm, d_gtm, d_shf, d_scf, d_gtf], axis=1).reshape(-1)
        d_gain[i] = jnp.concatenate([d_g0, d_g1, d_g2, d_g3], axis=0)
        if i == DEPTH - 1:
            landed.update(zip(keys_a, _split_wait("scatter_ga_wait", fly_a, dx)))
    landed.update(zip(keys_b, _split_wait("scatter_gb_wait", fly_b, dx)))
    keys_c = [(n, 0) for n in MLA_MATS]
    fly_c = _split_start("scatter_gc_start", [big[k] for k in keys_c], src_by='xy')
    grad_x = dx.reshape(x.shape)
    grads = {n: jnp.stack([g[l] for l in sorted(g)]) for n, g in small.items()}
    grads['norm_g'] = jnp.stack(d_gain)
    grads['ada_b'] = jnp.stack(d_mod)

    pack = jnp.concatenate([grads[n].reshape(-1) for n in SMALL] + [loss_acc[0, :1]])
    n_pack = pack.shape[0]
    rows = -(-n_pack // 1024) * 8
    pack = jnp.pad(pack, (0, rows * 128 - n_pack)).reshape(rows, 128)
    fly_s = _split_start("gather_small_start", [pack], group='xyc')

    out_g, out_d, out_m, out_v = {}, {}, {}, {}
    chains = {}
    for n in BIG + ['ffn_w1', 'ffn_w2']:
        if n in MLA_MATS:
            chains[n] = _finish(f"finish_{n}1", W[n], landed[(n, 1)], MOM[n], VAR[n], layer=1)
        elif n in BIG:
            out_g[n], out_d[n], out_m[n], out_v[n] = _finish(f"finish_{n}", W[n], landed[(n, None)], MOM[n], VAR[n])
        else:
            res = None
            for l in range(DEPTH):
                res = _finish(f"finish_{n}{l}", W[n], landed[(n, l)], MOM[n], VAR[n], layer=l, prev=res)
            out_g[n], out_d[n], out_m[n], out_v[n] = res
    landed.update(zip(keys_c, _split_wait("scatter_gc_wait", fly_c, out_g['ffn_w2'])))
    for n in MLA_MATS:
        out_g[n], out_d[n], out_m[n], out_v[n] = _finish(f"finish_{n}0", W[n], landed[(n, 0)], MOM[n], VAR[n],
                                                         layer=0, prev=chains[n])
    pack8 = _split_wait("gather_small_wait", fly_s, out_g['mla_w_o'])[0]
    (tot,) = _ew("sum_small", lambda *v: (functools.reduce(lambda p, q: p + q, v),), [(pack8, s) for s in range(8)],
                 [F32], (rows, 128))
    tot = tot.reshape(-1)
    loss = tot[n_pack - 1]
    d_mod_all = pack8.reshape(8, -1)[:, :DEPTH * 6 * D_MODEL].reshape(8, DEPTH, 6 * D_MODEL)
    final = {}
    off = 0
    for n in SMALL:
        ax = SHARD_AXIS[n]
        shape = tuple(d * 4 if k == ax else d for k, d in enumerate(W[n].shape))
        size = grads[n].size
        g = tot[off:off + size].reshape(shape)
        off += size
        if ax is not None:
            g = lax.dynamic_index_in_dim(_to_shards(g, ax), chip, 0, keepdims=False)
        final[n] = g

    g_ada = []
    for l in range(DEPTH):
        dm_l = jnp.pad(lax.dynamic_slice(d_mod_all[:, l], (0, chip * n_sh), (8, n_sh)), ((0, ADA_ROWS - 8), (0, 0)))
        g_ada.append(_mm(f"ada_wg{l}", c8, dm_l, 'tn', D_MODEL, n_sh, ADA_ROWS, tn=n_sh // 2, pro_a=silu))
    final['ada_w'] = jnp.stack(g_ada)

    for n in WEIGHTS:
        if n in out_g:
            continue
        shape = W[n].shape
        out_g[n] = final[n].reshape(shape)
        out_d[n], out_m[n], out_v[n] = _ew(f"adamw_{n}", lambda w, g, m, v: _adamw(w, g, m, v),
                                           [W[n], out_g[n], MOM[n], VAR[n]], [F32] * 3, shape)
    return (loss, grad_x, *[out_g[n] for n in WEIGHTS], *[out_d[n] for n in WEIGHTS],
            *[out_m[n] for n in WEIGHTS], *[out_v[n] for n in WEIGHTS])
```

```python
import functools
import math

import jax
import jax.numpy as jnp
from jax import lax
from jax.experimental import pallas as pl
from jax.experimental.pallas import tpu as pltpu

F32 = jnp.float32
BF16 = jnp.bfloat16

D_MODEL = 1024
DEPTH = 4
N_HEADS = 16
QK_NOPE = 64
QK_ROPE = 32
V_HEAD = 64
Q_LORA = 384
KV_LORA = 256
HEAD_PAD = 128
QW = N_HEADS * HEAD_PAD
KVW = 2 * QW
DKV = KV_LORA + QK_ROPE
DQKV = Q_LORA + DKV
D_FF = 4096
CONV_WIDTH = 31
POOL_WINDOWS = (2, 4, 8, 16)
CHUNK_SHIFT = 6
ROPE_THETA = 10000.0
NORM_EPS = 1e-6
NEG_INF = -1e30
ATT_SCALE = 1.0 / math.sqrt(QK_NOPE + QK_ROPE)
BQ = 256
HB = 8
HF = 8
LOG2E = 1.4426950408889634
SCALE_LOG2E = ATT_SCALE * LOG2E
PAD_ROWS = 32
ADA_ROWS = 128
VMEM_LIMIT = 56 * 1024 * 1024

ADAM_LR = 0.001
ADAM_B1 = 0.9
ADAM_B2 = 0.999
ADAM_EPS = 1e-08
ADAM_WD = 0.01
ADAM_STEP = 10

WEIGHTS = ['ada_w', 'ada_b', 'norm_g', 'mla_w_dq', 'mla_q_norm_g', 'mla_w_uq', 'mla_w_dkv', 'mla_kv_norm_g',
           'mla_w_ukv', 'mla_w_o', 'conv_w_pw1', 'conv_b_pw1', 'conv_w_dw', 'conv_b_dw', 'conv_ln_g', 'conv_ln_b',
           'conv_w_pw2', 'conv_b_pw2', 'pool_w', 'pool_b', 'pool_scale', 'ffn_w1', 'ffn_w2']
SHARD_AXIS = {'ada_w': 2, 'ada_b': None, 'norm_g': 2, 'mla_w_dq': 1, 'mla_q_norm_g': 1, 'mla_w_uq': 2,
              'mla_w_dkv': 1, 'mla_kv_norm_g': 1, 'mla_w_ukv': 2, 'mla_w_o': 1, 'conv_w_pw1': 2,
              'conv_b_pw1': None, 'conv_w_dw': 2, 'conv_b_dw': None, 'conv_ln_g': None, 'conv_ln_b': None,
              'conv_w_pw2': 1, 'conv_b_pw2': None, 'pool_w': 2, 'pool_b': 2, 'pool_scale': 1,
              'ffn_w1': 2, 'ffn_w2': 1}
MLA_MATS = ['mla_w_dq', 'mla_w_uq', 'mla_w_dkv', 'mla_w_ukv', 'mla_w_o']
BIG = MLA_MATS + ['conv_w_pw1', 'conv_w_pw2', 'pool_w']
SMALL = ['ada_b', 'norm_g', 'mla_q_norm_g', 'mla_kv_norm_g', 'conv_b_pw1', 'conv_w_dw', 'conv_b_dw',
         'conv_ln_g', 'conv_ln_b', 'conv_b_pw2', 'pool_b', 'pool_scale']


def _cparams(*sem):
    return pltpu.CompilerParams(dimension_semantics=sem, vmem_limit_bytes=VMEM_LIMIT)


def _colsum(v):
    return jnp.sum(v, axis=0, keepdims=True)


def _rowmean(v):
    return jnp.mean(v, axis=-1, keepdims=True)


def _sigmoid(v):
    return 1.0 / (1.0 + jnp.exp(-v))


def _rowk(name, fn, rows, bcast, out_row, out_acc, tm=512):
    S = rows[0].shape[0]
    tm = min(tm, S)
    while S % tm:
        tm //= 2
    nin, no, na = len(rows) + len(bcast), len(out_row), len(out_acc)

    def body(*refs):
        vals = [r[...] for r in refs[:nin]]
        outs = refs[nin:nin + no]
        accs = refs[nin + no:]
        ro, ao = fn(*vals)
        for r, v in zip(outs, ro):
            r[...] = v.astype(r.dtype)
        if na:
            @pl.when(pl.program_id(0) == 0)
            def _():
                for r in accs:
                    r[...] = jnp.zeros(r.shape, r.dtype)
            for r, v in zip(accs, ao):
                r[...] += v

    in_specs = [pl.BlockSpec((tm, a.shape[1]), lambda i: (i, 0)) for a in rows]
    in_specs += [pl.BlockSpec(b.shape, lambda i, n=b.ndim: (0,) * n) for b in bcast]
    out_shape = [jax.ShapeDtypeStruct((S, w), dt) for w, dt in out_row]
    out_shape += [jax.ShapeDtypeStruct((1, w), F32) for w in out_acc]
    out_specs = [pl.BlockSpec((tm, w), lambda i: (i, 0)) for w, _ in out_row]
    out_specs += [pl.BlockSpec((1, w), lambda i: (0, 0)) for w in out_acc]
    res = pl.pallas_call(body, name=name, grid=(S // tm,), in_specs=in_specs, out_specs=out_specs,
                         out_shape=out_shape, compiler_params=_cparams("arbitrary"))(*rows, *bcast)
    return list(res[:no]), list(res[no:])


_DIMS = {'nn': ((1,), (0,)), 'nt': ((1,), (1,)), 'tn': ((0,), (0,))}


def _mm(name, a, b, mode, M, N, K, *, tm=1024, tn=1024, tk=1024, a_spec=None, b_spec=None, pro_a=None,
        extras=(), extra_specs=(), epi=None, outs=None, out_specs=None):
    tm, tn, tk = (t if d % t == 0 else d for t, d in ((min(tm, M), M), (min(tn, N), N), (min(tk, K), K)))
    nk = K // tk
    if a_spec is None:
        a_spec = (pl.BlockSpec((tk, tm), lambda i, j, k: (k, i)) if mode == 'tn'
                  else pl.BlockSpec((tm, tk), lambda i, j, k: (i, k)))
    if b_spec is None:
        b_spec = (pl.BlockSpec((tn, tk), lambda i, j, k: (j, k)) if mode == 'nt'
                  else pl.BlockSpec((tk, tn), lambda i, j, k: (k, j)))
    if outs is None:
        outs = [jax.ShapeDtypeStruct((M, N), F32)]
    if out_specs is None:
        out_specs = [pl.BlockSpec((tm, tn), lambda i, j, k: (i, j)) for _ in outs]
    ne, no = len(extras), len(outs)
    dims = (_DIMS[mode], ((), ()))

    def body(a_ref, b_ref, *rest):
        ex, out_refs = rest[:ne], rest[ne:ne + no]
        av = a_ref[...]
        if pro_a is not None:
            av = pro_a(av)
        part = lax.dot_general(av.astype(BF16), b_ref[...].astype(BF16), dims, preferred_element_type=F32)

        def finish(acc):
            vals = (acc,) if epi is None else epi(acc, *[e[...] for e in ex])
            for r, v in zip(out_refs, vals):
                r[...] = v.astype(r.dtype)

        if nk == 1:
            finish(part)
            return
        acc_ref = rest[ne + no]
        k = pl.program_id(2)

        @pl.when(k == 0)
        def _():
            acc_ref[...] = part

        @pl.when(k > 0)
        def _():
            acc_ref[...] += part

        @pl.when(k == nk - 1)
        def _():
            finish(acc_ref[...])

    res = pl.pallas_call(
        body, name=name, grid=(M // tm, N // tn, nk),
        in_specs=[a_spec, b_spec, *extra_specs], out_specs=list(out_specs), out_shape=list(outs),
        scratch_shapes=[pltpu.VMEM((tm, tn), F32)] if nk > 1 else [],
        compiler_params=_cparams("parallel", "parallel", "arbitrary"))(a, b, *extras)
    return res[0] if no == 1 else list(res)


def _row_tile(R, C, itemsize=4, budget=1 << 20):
    if R * C * itemsize <= budget or R % 8:
        return R
    t = 8
    while R % (t * 2) == 0 and t * 2 * C * itemsize <= budget:
        t *= 2
    return t


def _ew(name, fn, ins, out_dtypes, shape):
    C = shape[-1]
    R = 1
    for s in shape[:-1]:
        R *= s
    tr = _row_tile(R, C)
    ops, specs = [], []
    for it in ins:
        if isinstance(it, tuple):
            arr, idx = it
            ops.append(arr.reshape(arr.shape[0], R, C))
            specs.append(pl.BlockSpec((None, tr, C), lambda i, n=idx: (n, i, 0)))
        else:
            ops.append(it.reshape(R, C))
            specs.append(pl.BlockSpec((tr, C), lambda i: (i, 0)))
    nin = len(ops)

    def body(*refs):
        vals = fn(*[r[...] for r in refs[:nin]])
        for r, v in zip(refs[nin:], vals):
            r[...] = v.astype(r.dtype)

    res = pl.pallas_call(
        body, name=name, grid=(R // tr,), in_specs=specs,
        out_specs=[pl.BlockSpec((tr, C), lambda i: (i, 0)) for _ in out_dtypes],
        out_shape=[jax.ShapeDtypeStruct((R, C), dt) for dt in out_dtypes],
        compiler_params=_cparams("parallel"))(*ops)
    return [r.reshape(shape) for r in res]


_FLIPS = {'xyc': [(fx, fy, fc) for fx in (0, 1) for fy in (0, 1) for fc in (0, 1)][1:],
          'xy': [(1, 0, 0), (0, 1, 0), (1, 1, 0)],
          'c': [(0, 0, 1)]}
_NSLOT = {'xyc': 8, 'xy': 4, 'c': 2}


def _slot(kind, cx, cy, cc):
    return {'xyc': 4 * cx + 2 * cy + cc, 'xy': 2 * cx + cy, 'c': cc}[kind]


def _put_own(land, arr, group, src_by):
    coords = (lax.axis_index("x"), lax.axis_index("y"), lax.axis_index("c"))
    pay = arr if src_by is None else lax.dynamic_index_in_dim(arr, _slot(src_by, *coords), 0, keepdims=False)
    return lax.dynamic_update_index_in_dim(land, pay, _slot(group, *coords), 0)


def _exchange(name, arrays, group, src_by=None):
    flips, nsl, n = _FLIPS[group], _NSLOT[group], len(arrays)
    nf = len(flips)

    def body(*refs):
        ins, outs = refs[:n], refs[n:2 * n]
        send_sems, recv_sems = refs[2 * n:]
        mx, my, mc = lax.axis_index("x"), lax.axis_index("y"), lax.axis_index("c")
        me = _slot(group, mx, my, mc)

        def payload(a, cx, cy, cc):
            return ins[a] if src_by is None else ins[a].at[_slot(src_by, cx, cy, cc)]

        sends, recvs = [], []
        for a in range(n):
            for f, (fx, fy, fc) in enumerate(flips):
                px = 1 - mx if fx else mx
                py = 1 - my if fy else my
                pc = 1 - mc if fc else mc
                src = payload(a, px, py, pc)
                sends.append(pltpu.make_async_remote_copy(
                    src_ref=src, dst_ref=outs[a].at[me], send_sem=send_sems.at[a, f],
                    recv_sem=recv_sems.at[a, f], device_id=(px, py, pc),
                    device_id_type=pl.DeviceIdType.MESH))
                recvs.append(pltpu.make_async_remote_copy(
                    src_ref=src, dst_ref=outs[a].at[_slot(group, px, py, pc)], send_sem=send_sems.at[a, f],
                    recv_sem=recv_sems.at[a, f], device_id=(px, py, pc),
                    device_id_type=pl.DeviceIdType.MESH))
        for cp in sends:
            cp.start()
        for cp in recvs:
            cp.wait_recv()
        for cp in sends:
            cp.wait_send()

    out_shape = [jax.ShapeDtypeStruct((nsl,) + (a.shape if src_by is None else a.shape[1:]), a.dtype)
                 for a in arrays]
    any_spec = pl.BlockSpec(memory_space=pl.ANY)
    res = pl.pallas_call(
        body, name=name, in_specs=[any_spec] * n, out_specs=[any_spec] * n, out_shape=out_shape,
        scratch_shapes=[pltpu.SemaphoreType.DMA((n, nf)), pltpu.SemaphoreType.DMA((n, nf))],
        compiler_params=pltpu.CompilerParams(has_side_effects=True))(*arrays)
    return [_put_own(l, a, group, src_by) for a, l in zip(arrays, res)]


_HBM = pl.BlockSpec(memory_space=pltpu.HBM)
_SEM = pl.BlockSpec(memory_space=pltpu.SEMAPHORE)
_DATAFLOW = pltpu.SideEffectType.DATAFLOW_SIDE_EFFECTING


def _group_copies(ins, lands, send_sems, recv_sems, group, src_by):
    mx, my, mc = lax.axis_index("x"), lax.axis_index("y"), lax.axis_index("c")
    me = _slot(group, mx, my, mc)
    pairs = []
    for a in range(len(ins)):
        for fx, fy, fc in _FLIPS[group]:
            peer = (1 - mx if fx else mx, 1 - my if fy else my, 1 - mc if fc else mc)
            src = ins[a] if src_by is None else ins[a].at[_slot(src_by, *peer)]
            mk = functools.partial(pltpu.make_async_remote_copy, src_ref=src, send_sem=send_sems,
                                   recv_sem=recv_sems, device_id=peer, device_id_type=pl.DeviceIdType.MESH)
            pairs.append((mk(dst_ref=lands[a].at[me]), mk(dst_ref=lands[a].at[_slot(group, *peer)])))
    return pairs


def _split_start(name, arrays, group='xy', src_by=None):
    n = len(arrays)
    lands = [lax.empty((_NSLOT[group],) + (a.shape if src_by is None else a.shape[1:]), a.dtype) for a in arrays]

    def body(*refs):
        ins, lnd, send_sems, recv_sems, token = refs[:n], refs[n:2 * n], refs[2 * n], refs[2 * n + 1], refs[-1]
        for to_peer, _ in _group_copies(ins, lnd, send_sems, recv_sems, group, src_by):
            to_peer.start()
        token[...] = jnp.zeros(token.shape, F32)

    ops = [pltpu.with_memory_space_constraint(a, pltpu.HBM) for a in [*arrays, *lands]]
    res = pl.pallas_call(
        body, name=name, in_specs=[_HBM] * (2 * n),
        out_specs=[_SEM, _SEM] + [_HBM] * (2 * n) + [pl.BlockSpec(memory_space=pltpu.VMEM)],
        out_shape=[pltpu.SemaphoreType.DMA(()), pltpu.SemaphoreType.DMA(())]
        + [pltpu.HBM(a.shape, a.dtype) for a in ops] + [jax.ShapeDtypeStruct((8, 128), F32)],
        input_output_aliases={k: 2 + k for k in range(2 * n)},
        compiler_params=pltpu.CompilerParams(has_side_effects=_DATAFLOW))(*ops)
    return dict(n=n, group=group, src_by=src_by, send=res[0], recv=res[1], arrays=list(res[2:2 + n]),
                lands=list(res[2 + n:2 + 2 * n]), token=res[-1])


def _split_wait(name, st, after):
    n, group, src_by = st['n'], st['group'], st['src_by']

    def wait_body(*refs):
        ins, lnd, send_sems, recv_sems = refs[:n], refs[n:2 * n], refs[2 * n], refs[2 * n + 1]
        for to_peer, from_peer in _group_copies(ins, lnd, send_sems, recv_sems, group, src_by):
            to_peer.wait_send()
            from_peer.wait_recv()

    shapes = [pltpu.HBM(a.shape, a.dtype) for a in [*st['arrays'], *st['lands']]]
    res = pl.pallas_call(
        wait_body, name=name, in_specs=[_HBM] * (2 * n) + [_SEM, _SEM, pl.BlockSpec(memory_space=pl.ANY)],
        out_specs=[_HBM] * (2 * n), out_shape=shapes, input_output_aliases={k: k for k in range(2 * n)},
        compiler_params=pltpu.CompilerParams(has_side_effects=_DATAFLOW))(
            *st['arrays'], *st['lands'], st['send'], st['recv'], after)
    return [_put_own(l, a, group, src_by) for a, l in zip(res[:n], res[n:])]


def _unshard(g, axis):
    t = jnp.moveaxis(g, 0, axis)
    s = t.shape
    return t.reshape(s[:axis] + (s[axis] * s[axis + 1],) + s[axis + 2:])


def _to_shards(w, axis):
    s = w.shape
    t = w.reshape(s[:axis] + (4, s[axis] // 4) + s[axis + 1:])
    return jnp.moveaxis(t, axis, 0)


def _pre_fwd(x, g, sc, sh):
    r = lax.rsqrt(_rowmean(x * x) + NORM_EPS)
    return (x * r) * g * (1.0 + sc) + sh


def _pre_bwd(dh, x, g, sc):
    r = lax.rsqrt(_rowmean(x * x) + NORM_EPS)
    xn = x * r
    dxn = dh * (g * (1.0 + sc))
    dx = r * (dxn - xn * _rowmean(dxn * xn))
    t = dh * xn
    return dx, _colsum(dh), _colsum(t * g), _colsum(t * (1.0 + sc))


def _post_fwd(x, y, gt, g):
    r = lax.rsqrt(_rowmean(y * y) + NORM_EPS)
    return x + gt * ((y * r) * g)


def _post_bwd(dxo, y, gt, g):
    r = lax.rsqrt(_rowmean(y * y) + NORM_EPS)
    yn = y * r
    t = dxo * yn
    dyn = dxo * (gt * g)
    dy = r * (dyn - yn * _rowmean(dyn * yn))
    return dy, _colsum(t * g), _colsum(t * gt)


def _gain_bwd(dy, x, g):
    r = lax.rsqrt(_rowmean(x * x) + NORM_EPS)
    xn = x * r
    dxn = dy * g
    return r * (dxn - xn * _rowmean(dxn * xn)), _colsum(dy * xn)


def _rope(x, cos, sa, sb):
    return x * cos + pltpu.roll(x, HEAD_PAD - 16, 1) * sa + pltpu.roll(x, 16, 1) * sb


def _rope_t(d, cos, sa, sb):
    return d * cos + pltpu.roll(d * sa, 16, 1) + pltpu.roll(d * sb, HEAD_PAD - 16, 1)


def _rope_tables(pos_f):
    S = pos_f.shape[0]
    inv = ROPE_THETA ** (-jnp.arange(0, QK_ROPE, 2, dtype=F32) / QK_ROPE)
    inv_ext = jnp.concatenate([jnp.zeros((QK_NOPE,), F32), inv, inv,
                               jnp.zeros((HEAD_PAD - QK_NOPE - QK_ROPE,), F32)]).reshape(1, HEAD_PAD)

    def fn(p, iv):
        ang = p * iv
        lane = lax.broadcasted_iota(jnp.int32, ang.shape, 1)
        s = jnp.sin(ang)
        first = (lane >= QK_NOPE) & (lane < QK_NOPE + QK_ROPE // 2)
        second = (lane >= QK_NOPE + QK_ROPE // 2) & (lane < QK_NOPE + QK_ROPE)
        return (jnp.cos(ang), jnp.where(first, -s, 0.0), jnp.where(second, s, 0.0)), ()

    (cos, sa, sb), _ = _rowk("rope_tables", fn, [pos_f], [inv_ext], [(HEAD_PAD, F32)] * 3, [])
    return cos, sa, sb


def _diag_mask(transposed):
    r = lax.broadcasted_iota(jnp.int32, (BQ, BQ), 0) >> CHUNK_SHIFT
    c = lax.broadcasted_iota(jnp.int32, (BQ, BQ), 1) >> CHUNK_SHIFT
    return (r <= c) if transposed else (c <= r)


_NT = (((1,), (1,)), ((), ()))
_NN = (((1,), (0,)), ((), ()))


def _attn_fwd(qf, kvf, HB=HF):
    S = qf.shape[0]
    nq = S // BQ

    def body(q_ref, kv_ref, o_ref, ob_ref, lse_ref):
        qi = pl.program_id(1)
        qs = [q_ref[:, hh * HEAD_PAD:(hh + 1) * HEAD_PAD] for hh in range(HB)]

        def step(j, carry, diag):
            off = pl.multiple_of(j * BQ, BQ)
            sts = [lax.dot_general(kv_ref[pl.ds(off, BQ), pl.ds(2 * hh * HEAD_PAD, HEAD_PAD)], qs[hh], _NT,
                                   preferred_element_type=F32) for hh in range(HB)]
            mid = []
            for hh in range(HB):
                m, l, acc = carry[hh]
                st = jnp.where(_diag_mask(True), sts[hh], NEG_INF) if diag else sts[hh]
                m2 = jnp.maximum(m, jnp.max(st, axis=0, keepdims=True))
                al = jnp.exp2((m - m2) * SCALE_LOG2E)
                pt = jnp.exp2((st - m2) * SCALE_LOG2E)
                mid.append((m2, l * al + jnp.sum(pt, axis=0, keepdims=True), acc * al, pt.astype(BF16)))
            out = []
            for hh in range(HB):
                m2, l2, acc_s, ptb = mid[hh]
                v = kv_ref[pl.ds(off, BQ), pl.ds((2 * hh + 1) * HEAD_PAD, HEAD_PAD)]
                out.append((m2, l2, acc_s + lax.dot_general(v, ptb, _TN, preferred_element_type=F32)))
            return tuple(out)

        init = tuple((jnp.full((1, BQ), NEG_INF, F32), jnp.zeros((1, BQ), F32), jnp.zeros((HEAD_PAD, BQ), F32))
                     for _ in range(HB))
        carry = lax.fori_loop(0, qi, lambda j, c: step(j, c, False), init)
        carry = step(qi, carry, True)
        for hh in range(HB):
            m, l, acc = carry[hh]
            ov = (acc / l).T
            o_ref[:, hh * HEAD_PAD:(hh + 1) * HEAD_PAD] = ov
            ob_ref[:, hh * HEAD_PAD:(hh + 1) * HEAD_PAD] = ov.astype(BF16)
            lse_ref[hh] = m * SCALE_LOG2E + jnp.log(l) * LOG2E

    return pl.pallas_call(
        body, name="attn_fwd", grid=(N_HEADS // HB, nq),
        in_specs=[pl.BlockSpec((BQ, HB * HEAD_PAD), lambda g, i: (i, g)),
                  pl.BlockSpec((S, 2 * HB * HEAD_PAD), lambda g, i: (0, g))],
        out_specs=[pl.BlockSpec((BQ, HB * HEAD_PAD), lambda g, i: (i, g)),
                   pl.BlockSpec((BQ, HB * HEAD_PAD), lambda g, i: (i, g)),
                   pl.BlockSpec((HB, None, 1, BQ), lambda g, i: (g, i, 0, 0))],
        out_shape=[jax.ShapeDtypeStruct((S, QW), F32), jax.ShapeDtypeStruct((S, QW), BF16),
                   jax.ShapeDtypeStruct((N_HEADS, nq, 1, BQ), F32)],
        compiler_params=_cparams("parallel", "arbitrary"))(qf, kvf)


def _attn_delta(dob, o):
    S = o.shape[0]

    def fn(dov, ov):
        prod = dov.astype(F32) * ov
        lane = lax.broadcasted_iota(jnp.int32, (prod.shape[0], HEAD_PAD), 1)
        out = jnp.zeros((prod.shape[0], HEAD_PAD), F32)
        for h in range(N_HEADS):
            out = jnp.where(lane == h, jnp.sum(prod[:, h * HEAD_PAD:(h + 1) * HEAD_PAD], axis=1, keepdims=True), out)
        return (out,), ()

    (dd,), _ = _rowk("attn_delta", fn, [dob, o], [], [(HEAD_PAD, F32)], [])
    return dd[:, :N_HEADS].T.reshape(N_HEADS, S // BQ, 1, BQ)


_TN = (((0,), (0,)), ((), ()))


def _attn_bwd(qf, kvf, dob, lse_row, dd_row, cos, sa, sb):
    S = qf.shape[0]
    nq = S // BQ

    def body(kv_ref, q_ref, do_ref, lse_ref, dd_ref, cos_ref, sa_ref, sb_ref, dqo_ref, dkv_ref, dq_ref):
        kj = pl.program_id(1)

        @pl.when(kj == 0)
        def _():
            dq_ref[...] = jnp.zeros(dq_ref.shape, F32)

        ks = [kv_ref[:, 2 * hh * HEAD_PAD:(2 * hh + 1) * HEAD_PAD] for hh in range(HB)]
        vs = [kv_ref[:, (2 * hh + 1) * HEAD_PAD:(2 * hh + 2) * HEAD_PAD] for hh in range(HB)]

        def step(i, carry, diag):
            off = pl.multiple_of(i * BQ, BQ)
            cols = [pl.ds(hh * HEAD_PAD, HEAD_PAD) for hh in range(HB)]
            q = [q_ref[pl.ds(off, BQ), cols[hh]] for hh in range(HB)]
            do = [do_ref[pl.ds(off, BQ), cols[hh]] for hh in range(HB)]
            sts = [lax.dot_general(ks[hh], q[hh], _NT, preferred_element_type=F32) for hh in range(HB)]
            dpts = [lax.dot_general(vs[hh], do[hh], _NT, preferred_element_type=F32) for hh in range(HB)]
            mid = []
            for hh in range(HB):
                st = jnp.where(_diag_mask(True), sts[hh], NEG_INF) if diag else sts[hh]
                pt = jnp.exp2(st * SCALE_LOG2E - lse_ref[hh, i])
                mid.append((pt.astype(BF16), (pt * (dpts[hh] - dd_ref[hh, i])).astype(BF16)))
            out = []
            for hh in range(HB):
                dk, dv = carry[hh]
                ptb, dsb = mid[hh]
                dv2 = dv + lax.dot_general(ptb, do[hh], _NN, preferred_element_type=F32)
                dk2 = dk + lax.dot_general(dsb, q[hh], _NN, preferred_element_type=F32)
                dq_ref[pl.ds(off, BQ), cols[hh]] += lax.dot_general(dsb, ks[hh], _TN, preferred_element_type=F32)
                out.append((dk2, dv2))
            return tuple(out)

        zero = jnp.zeros((BQ, HEAD_PAD), F32)
        carry = step(kj, tuple((zero, zero) for _ in range(HB)), True)
        carry = lax.fori_loop(kj + 1, nq, lambda i, c: step(i, c, False), carry)
        done = pl.ds(pl.multiple_of(kj * BQ, BQ), BQ)
        for hh in range(HB):
            dk, dv = carry[hh]
            dk = _rope_t(dk * ATT_SCALE, cos_ref[...], sa_ref[...], sb_ref[...])
            dkv_ref[:, 2 * hh * HEAD_PAD:(2 * hh + 1) * HEAD_PAD] = dk.astype(BF16)
            dkv_ref[:, (2 * hh + 1) * HEAD_PAD:(2 * hh + 2) * HEAD_PAD] = dv.astype(BF16)
            cols = pl.ds(hh * HEAD_PAD, HEAD_PAD)
            dqo_ref[:, cols] = _rope_t(dq_ref[done, cols] * ATT_SCALE, cos_ref[...], sa_ref[...],
                                       sb_ref[...]).astype(BF16)

    tab = pl.BlockSpec((BQ, HEAD_PAD), lambda g, j: (j, 0))
    row = pl.BlockSpec((HB, nq, 1, BQ), lambda g, j: (g, 0, 0, 0))
    seq = pl.BlockSpec((S, HB * HEAD_PAD), lambda g, j: (0, g), pipeline_mode=pl.Buffered(1))
    kvb = pl.BlockSpec((BQ, 2 * HB * HEAD_PAD), lambda g, j: (j, g))
    return pl.pallas_call(
        body, name="attn_bwd", grid=(N_HEADS // HB, nq),
        in_specs=[kvb, seq, seq, row, row, tab, tab, tab],
        out_specs=[pl.BlockSpec((BQ, HB * HEAD_PAD), lambda g, j: (j, g)), kvb],
        out_shape=[jax.ShapeDtypeStruct((S, QW), BF16), jax.ShapeDtypeStruct((S, KVW), BF16)],
        scratch_shapes=[pltpu.VMEM((S, HB * HEAD_PAD), F32)],
        compiler_params=_cparams("parallel", "arbitrary"))(kvf, qf, dob, lse_row, dd_row, cos, sa, sb)


DC = 128
TR = 256


def _dwconv_fwd(u, w, b):
    S, Dm = u.shape
    tr = min(TR, S)

    def body(u_ref, w_ref, b_ref, o_ref, pad_ref):
        pad_ref[pl.ds(0, PAD_ROWS), :] = jnp.zeros((PAD_ROWS, DC), F32)
        pad_ref[pl.ds(PAD_ROWS, S), :] = u_ref[...]
        wv = w_ref[...]
        for r in range(S // tr):
            acc = jnp.broadcast_to(b_ref[...], (tr, DC))
            for j in range(CONV_WIDTH):
                acc = acc + wv[j:j + 1, :] * pad_ref[pl.ds(r * tr + PAD_ROWS - (CONV_WIDTH - 1) + j, tr), :]
            o_ref[pl.ds(r * tr, tr), :] = acc

    return pl.pallas_call(
        body, name="dwconv_fwd", grid=(Dm // DC,),
        in_specs=[pl.BlockSpec((S, DC), lambda c: (0, c)), pl.BlockSpec((CONV_WIDTH, DC), lambda c: (0, c)),
                  pl.BlockSpec((1, DC), lambda c: (0, c))],
        out_specs=pl.BlockSpec((S, DC), lambda c: (0, c)),
        out_shape=jax.ShapeDtypeStruct((S, Dm), F32),
        scratch_shapes=[pltpu.VMEM((S + PAD_ROWS, DC), F32)],
        compiler_params=_cparams("parallel"))(u, w, b)


def _dwconv_bwd(d, u, w):
    S, Dm = u.shape
    tr = min(TR, S)

    def body(d_ref, u_ref, w_ref, du_ref, dw_ref, padd_ref, padu_ref):
        padd_ref[pl.ds(0, S), :] = d_ref[...]
        padd_ref[pl.ds(S, PAD_ROWS), :] = jnp.zeros((PAD_ROWS, DC), F32)
        padu_ref[pl.ds(0, PAD_ROWS), :] = jnp.zeros((PAD_ROWS, DC), F32)
        padu_ref[pl.ds(PAD_ROWS, S), :] = u_ref[...]
        wv = w_ref[...]
        dws = [jnp.zeros((1, DC), F32) for _ in range(CONV_WIDTH)]
        for r in range(S // tr):
            acc = jnp.zeros((tr, DC), F32)
            for j in range(CONV_WIDTH):
                acc = acc + wv[j:j + 1, :] * padd_ref[pl.ds(r * tr + (CONV_WIDTH - 1) - j, tr), :]
            du_ref[pl.ds(r * tr, tr), :] = acc
            dt = d_ref[pl.ds(r * tr, tr), :]
            for j in range(CONV_WIDTH):
                ut = padu_ref[pl.ds(r * tr + PAD_ROWS - (CONV_WIDTH - 1) + j, tr), :]
                dws[j] = dws[j] + _colsum(dt * ut)
        for j in range(CONV_WIDTH):
            dw_ref[pl.ds(j, 1), :] = dws[j]
        dw_ref[pl.ds(CONV_WIDTH, 1), :] = jnp.zeros((1, DC), F32)

    blk = pl.BlockSpec((S, DC), lambda c: (0, c))
    return pl.pallas_call(
        body, name="dwconv_bwd", grid=(Dm // DC,),
        in_specs=[blk, blk, pl.BlockSpec((CONV_WIDTH, DC), lambda c: (0, c))],
        out_specs=[blk, pl.BlockSpec((PAD_ROWS, DC), lambda c: (0, c))],
        out_shape=[jax.ShapeDtypeStruct((S, Dm), F32), jax.ShapeDtypeStruct((PAD_ROWS, Dm), F32)],
        scratch_shapes=[pltpu.VMEM((S + PAD_ROWS, DC), F32), pltpu.VMEM((S + PAD_ROWS, DC), F32)],
        compiler_params=_cparams("parallel"))(d, u, w)


POOL_C = D_MODEL // len(POOL_WINDOWS)


def _pool_counts(r, tr, win):
    t = r * tr + lax.broadcasted_iota(jnp.int32, (tr, 1), 0)
    return jnp.minimum(t + 1, win).astype(F32)


def _pool_fwd(h):
    S, Dm = h.shape
    tr = min(TR, S)

    def body(h_ref, o_ref, pad_ref):
        pad_ref[pl.ds(0, PAD_ROWS), :] = jnp.zeros((PAD_ROWS, POOL_C), F32)
        pad_ref[pl.ds(PAD_ROWS, S), :] = h_ref[...]
        for g, win in enumerate(POOL_WINDOWS):
            @pl.when(pl.program_id(0) == g)
            def _():
                for r in range(S // tr):
                    acc = pad_ref[pl.ds(r * tr + PAD_ROWS, tr), :]
                    for j in range(1, win):
                        acc = acc + pad_ref[pl.ds(r * tr + PAD_ROWS - j, tr), :]
                    pooled = acc / _pool_counts(r, tr, win)
                    o_ref[pl.ds(r * tr, tr), :] = (pooled - h_ref[pl.ds(r * tr, tr), :]).astype(BF16)

    blk = pl.BlockSpec((S, POOL_C), lambda g: (0, g))
    return pl.pallas_call(
        body, name="pool_fwd", grid=(len(POOL_WINDOWS),), in_specs=[blk], out_specs=blk,
        out_shape=jax.ShapeDtypeStruct((S, Dm), BF16),
        scratch_shapes=[pltpu.VMEM((S + PAD_ROWS, POOL_C), F32)],
        compiler_params=_cparams("parallel"))(h)


def _pool_bwd(dp):
    S, Dm = dp.shape
    tr = min(TR, S)

    def body(d_ref, o_ref, pad_ref):
        pad_ref[pl.ds(S, PAD_ROWS), :] = jnp.zeros((PAD_ROWS, POOL_C), F32)
        for g, win in enumerate(POOL_WINDOWS):
            @pl.when(pl.program_id(0) == g)
            def _():
                for r in range(S // tr):
                    pad_ref[pl.ds(r * tr, tr), :] = d_ref[pl.ds(r * tr, tr), :] / _pool_counts(r, tr, win)
                for r in range(S // tr):
                    acc = pad_ref[pl.ds(r * tr, tr), :]
                    for j in range(1, win):
                        acc = acc + pad_ref[pl.ds(r * tr + j, tr), :]
                    o_ref[pl.ds(r * tr, tr), :] = acc - d_ref[pl.ds(r * tr, tr), :]

    blk = pl.BlockSpec((S, POOL_C), lambda g: (0, g))
    return pl.pallas_call(
        body, name="pool_bwd", grid=(len(POOL_WINDOWS),), in_specs=[blk], out_specs=blk,
        out_shape=jax.ShapeDtypeStruct((S, Dm), F32),
        scratch_shapes=[pltpu.VMEM((S + PAD_ROWS, POOL_C), F32)],
        compiler_params=_cparams("parallel"))(dp)


def _bias_spec(tn):
    return pl.BlockSpec((1, tn), lambda i, j, k: (0, j))


def _mla_weights(w_dq, w_dkv, w_uq, w_ukv, w_o):
    wd = jnp.concatenate([w_dq, w_dkv], axis=1)
    wq = jnp.pad(w_uq.reshape(Q_LORA, N_HEADS, QK_NOPE + QK_ROPE),
                 ((0, 0), (0, 0), (0, HEAD_PAD - QK_NOPE - QK_ROPE))).reshape(Q_LORA, QW)
    ukv = w_ukv.reshape(KV_LORA, N_HEADS, QK_NOPE + V_HEAD)
    wkv = jnp.zeros((DKV, N_HEADS, 2 * HEAD_PAD), BF16)
    wkv = wkv.at[:KV_LORA, :, :QK_NOPE].set(ukv[:, :, :QK_NOPE])
    wkv = wkv.at[:KV_LORA, :, HEAD_PAD:HEAD_PAD + V_HEAD].set(ukv[:, :, QK_NOPE:])
    eye = jnp.broadcast_to(jnp.eye(QK_ROPE, dtype=BF16)[:, None, :], (QK_ROPE, N_HEADS, QK_ROPE))
    wkv = wkv.at[KV_LORA:, :, QK_NOPE:QK_NOPE + QK_ROPE].set(eye).reshape(DKV, KVW)
    wo = jnp.pad(w_o.reshape(N_HEADS, V_HEAD, D_MODEL),
                 ((0, 0), (0, HEAD_PAD - V_HEAD), (0, 0))).reshape(QW, D_MODEL)
    return dict(wd=wd, wq=wq, wkv=wkv, wo=wo)


def _mla_weight_grads(g_wd, g_wq, g_wkv, g_wo):
    g_uq = g_wq.reshape(Q_LORA, N_HEADS, HEAD_PAD)[:, :, :QK_NOPE + QK_ROPE].reshape(Q_LORA, -1)
    t = g_wkv.reshape(DKV, N_HEADS, 2 * HEAD_PAD)[:KV_LORA]
    g_ukv = jnp.concatenate([t[:, :, :QK_NOPE], t[:, :, HEAD_PAD:HEAD_PAD + V_HEAD]], axis=2)
    g_o = g_wo.reshape(N_HEADS, HEAD_PAD, D_MODEL)[:, :V_HEAD].reshape(N_HEADS * V_HEAD, D_MODEL)
    return dict(mla_w_dq=g_wd[:, :Q_LORA], mla_w_uq=g_uq, mla_w_dkv=g_wd[:, Q_LORA:],
                mla_w_ukv=g_ukv.reshape(KV_LORA, -1), mla_w_o=g_o)


def _rope_epilogue(kv):
    def epi(acc, cos, sa, sb):
        parts = []
        for t in range(acc.shape[1] // HEAD_PAD):
            x = acc[:, t * HEAD_PAD:(t + 1) * HEAD_PAD]
            parts.append(x if (kv and t % 2) else _rope(x, cos, sa, sb))
        return (jnp.concatenate(parts, axis=1),)
    return epi


def _mla_fwd(tag, h, P, rope):
    S = h.shape[0]
    cos, sa, sb = rope
    tp = min(512, S)
    tabs = [pl.BlockSpec((tp, HEAD_PAD), lambda i, j, k: (i, 0))] * 3
    cqkv = _mm(f"mla_down{tag}", h, P['wd'], 'nn', S, DQKV, D_MODEL)

    def norms(x, qg, kg):
        xq, xk, xr = x[:, :Q_LORA], x[:, Q_LORA:Q_LORA + KV_LORA], x[:, Q_LORA + KV_LORA:]
        cq = xq * lax.rsqrt(_rowmean(xq * xq) + NORM_EPS) * qg
        ck = xk * lax.rsqrt(_rowmean(xk * xk) + NORM_EPS) * kg
        return (cq, jnp.concatenate([ck, xr], axis=1)), ()

    (cq, ckv), _ = _rowk(f"mla_norms{tag}", norms, [cqkv], [P['qg'], P['kg']], [(Q_LORA, BF16), (DKV, BF16)], [])
    qf = _mm(f"mla_q{tag}", cq, P['wq'], 'nn', S, QW, Q_LORA, tm=tp, tn=QW, extras=[cos, sa, sb],
             extra_specs=tabs, epi=_rope_epilogue(False), outs=[jax.ShapeDtypeStruct((S, QW), BF16)])
    kvf = _mm(f"mla_kv{tag}", ckv, P['wkv'], 'nn', S, KVW, DKV, tm=tp, tn=KVW, extras=[cos, sa, sb],
              extra_specs=tabs, epi=_rope_epilogue(True), outs=[jax.ShapeDtypeStruct((S, KVW), BF16)])
    o, ob, lse = _attn_fwd(qf, kvf)
    y = _mm(f"mla_o{tag}", ob, P['wo'], 'nn', S, D_MODEL, QW)
    return y, dict(cqkv=cqkv, cq=cq, ckv=ckv, qf=qf, kvf=kvf, o=o, ob=ob, lse=lse)


def _mla_bwd(tag, dy, h, sv, P, rope):
    S = h.shape[0]
    nq = S // BQ
    cos, sa, sb = rope
    g_wo = _mm(f"mla_o_wg{tag}", sv['ob'], dy, 'tn', QW, D_MODEL, S)
    dob = _mm(f"mla_o_dg{tag}", dy, P['wo'], 'nt', S, QW, D_MODEL, outs=[jax.ShapeDtypeStruct((S, QW), BF16)])
    dd = _attn_delta(dob, sv['o'])
    dq, dkv = _attn_bwd(sv['qf'], sv['kvf'], dob, sv['lse'], dd, cos, sa, sb)
    g_wq = _mm(f"mla_q_wg{tag}", sv['cq'], dq, 'tn', Q_LORA, QW, S)
    dcq = _mm(f"mla_q_dg{tag}", dq, P['wq'], 'nt', S, Q_LORA, QW)
    g_wkv = _mm(f"mla_kv_wg{tag}", sv['ckv'], dkv, 'tn', DKV, KVW, S)
    dckv = _mm(f"mla_kv_dg{tag}", dkv, P['wkv'], 'nt', S, DKV, KVW)

    def norms_bwd(dcq_v, dckv_v, x, qg, kg):
        xq, xk = x[:, :Q_LORA], x[:, Q_LORA:Q_LORA + KV_LORA]
        dxq, dqg = _gain_bwd(dcq_v, xq, qg)
        dxk, dkg = _gain_bwd(dckv_v[:, :KV_LORA], xk, kg)
        return (jnp.concatenate([dxq, dxk, dckv_v[:, KV_LORA:]], axis=1),), (dqg, dkg)

    (dcqkv,), (dqg, dkg) = _rowk(f"mla_norms_bwd{tag}", norms_bwd, [dcq, dckv, sv['cqkv']], [P['qg'], P['kg']],
                                 [(DQKV, BF16)], [Q_LORA, KV_LORA])
    g_wd = _mm(f"mla_down_wg{tag}", h, dcqkv, 'tn', D_MODEL, DQKV, S)
    dh = _mm(f"mla_down_dg{tag}", dcqkv, P['wd'], 'nt', S, D_MODEL, DQKV)
    grads = _mla_weight_grads(g_wd, g_wq, g_wkv, g_wo)
    grads.update(mla_q_norm_g=dqg.reshape(-1), mla_kv_norm_g=dkg.reshape(-1))
    return dh, grads


def _conv_fwd(h, P):
    S = h.shape[0]
    a = _mm("conv_pw1", h, P['w_pw1'], 'nn', S, 2 * D_MODEL, D_MODEL, extras=[P['b_pw1']],
            extra_specs=[_bias_spec(1024)], epi=lambda acc, b: (acc + b,))
    (u0,), _ = _rowk("conv_glu", lambda av: ((av[:, :D_MODEL] * _sigmoid(av[:, D_MODEL:]),), ()),
                     [a], [], [(D_MODEL, F32)], [])
    u1 = _dwconv_fwd(u0, P['w_dw'], P['b_dw'])

    def ln_silu(u, g, b):
        xc = u - _rowmean(u)
        z = xc * lax.rsqrt(_rowmean(xc * xc) + NORM_EPS) * g + b
        return (z * _sigmoid(z),), ()

    (u3,), _ = _rowk("conv_ln", ln_silu, [u1], [P['ln_g'], P['ln_b']], [(D_MODEL, BF16)], [])
    y = _mm("conv_pw2", u3, P['w_pw2'], 'nn', S, D_MODEL, D_MODEL, extras=[P['b_pw2']],
            extra_specs=[_bias_spec(1024)], epi=lambda acc, b: (acc + b,))
    return y, dict(a=a, u0=u0, u1=u1, u3=u3)


def _conv_bwd(dy, dy_colsum, h, sv, P):
    S = h.shape[0]
    g_pw2 = _mm("conv_pw2_wg", sv['u3'], dy, 'tn', D_MODEL, D_MODEL, S)
    du3 = _mm("conv_pw2_dg", dy, P['w_pw2'], 'nt', S, D_MODEL, D_MODEL)

    def ln_bwd(d3, u, g, b):
        xc = u - _rowmean(u)
        rstd = lax.rsqrt(_rowmean(xc * xc) + NORM_EPS)
        xh = xc * rstd
        z = xh * g + b
        sg = _sigmoid(z)
        dz = d3 * (sg * (1.0 + z * (1.0 - sg)))
        dxh = dz * g
        du = rstd * (dxh - _rowmean(dxh) - xh * _rowmean(dxh * xh))
        return (du,), (_colsum(dz * xh), _colsum(dz), _colsum(du))

    (du1,), (d_lng, d_lnb, d_bdw) = _rowk("conv_ln_bwd", ln_bwd, [du3, sv['u1']], [P['ln_g'], P['ln_b']],
                                          [(D_MODEL, F32)], [D_MODEL] * 3)
    du0, d_wdw = _dwconv_bwd(du1, sv['u0'], P['w_dw'])

    def glu_bwd(d0, av):
        a1, sg = av[:, :D_MODEL], _sigmoid(av[:, D_MODEL:])
        da = jnp.concatenate([d0 * sg, d0 * a1 * sg * (1.0 - sg)], axis=1)
        return (da,), (_colsum(da),)

    (da,), (d_bpw1,) = _rowk("conv_glu_bwd", glu_bwd, [du0, sv['a']], [], [(2 * D_MODEL, BF16)], [2 * D_MODEL])
    g_pw1 = _mm("conv_pw1_wg", h, da, 'tn', D_MODEL, 2 * D_MODEL, S)
    dh = _mm("conv_pw1_dg", da, P['w_pw1'], 'nt', S, D_MODEL, 2 * D_MODEL)
    grads = dict(conv_w_pw1=g_pw1, conv_b_pw1=d_bpw1.reshape(-1), conv_w_dw=d_wdw[:CONV_WIDTH],
                 conv_b_dw=d_bdw.reshape(-1), conv_ln_g=d_lng.reshape(-1), conv_ln_b=d_lnb.reshape(-1),
                 conv_w_pw2=g_pw2, conv_b_pw2=dy_colsum.reshape(-1))
    return dh, grads


def _pool_group_specs(tm):
    return (pl.BlockSpec((tm, POOL_C), lambda i, j, k: (i, j)),
            pl.BlockSpec((None, POOL_C, POOL_C), lambda i, j, k: (j, 0, 0)))


def _pool_mixer_fwd(h, P):
    S = h.shape[0]
    p = _pool_fwd(h)
    a_spec, b_spec = _pool_group_specs(min(1024, S))
    y, z = _mm("pool_mm", p, P['w'], 'nn', S, D_MODEL, POOL_C, tn=POOL_C, a_spec=a_spec, b_spec=b_spec,
               extras=[P['b'], P['scale']], extra_specs=[_bias_spec(POOL_C)] * 2,
               epi=lambda acc, b, s: ((acc + b) * s, acc + b),
               outs=[jax.ShapeDtypeStruct((S, D_MODEL), F32)] * 2)
    return y, dict(p=p, z=z)


def _pool_mixer_bwd(dy, sv, P):
    S = dy.shape[0]

    def scale_bwd(d, z, s):
        dz = d * s
        return (dz,), (_colsum(d * z), _colsum(dz))

    (dz,), (d_scale, d_b) = _rowk("pool_scale_bwd", scale_bwd, [dy, sv['z']], [P['scale']],
                                  [(D_MODEL, BF16)], [D_MODEL] * 2)
    a_spec, b_spec = _pool_group_specs(min(1024, S))
    dp = _mm("pool_mm_dg", dz, P['w'], 'nt', S, D_MODEL, POOL_C, tn=POOL_C, a_spec=a_spec, b_spec=b_spec)
    tk = min(512, S)
    grp = pl.BlockSpec((tk, POOL_C), lambda i, j, k: (k, j))
    g_w = _mm("pool_mm_wg", sv['p'], dz, 'tn', POOL_C, D_MODEL, S, tn=POOL_C, tk=tk, a_spec=grp, b_spec=grp,
              outs=[jax.ShapeDtypeStruct((len(POOL_WINDOWS), POOL_C, POOL_C), F32)],
              out_specs=[pl.BlockSpec((None, POOL_C, POOL_C), lambda i, j, k: (j, 0, 0))])
    dh = _pool_bwd(dp)
    return dh, dict(pool_w=g_w, pool_b=d_b.reshape(-1), pool_scale=d_scale.reshape(-1))


def _adamw(w, g, m, v):
    m2 = ADAM_B1 * m + (1.0 - ADAM_B1) * g
    v2 = ADAM_B2 * v + (1.0 - ADAM_B2) * (g * g)
    m_hat = m2 / (1.0 - ADAM_B1 ** ADAM_STEP)
    v_hat = v2 / (1.0 - ADAM_B2 ** ADAM_STEP)
    delta = -ADAM_LR * (m_hat / (jnp.sqrt(v_hat) + ADAM_EPS) + ADAM_WD * w)
    return delta, m2, v2


def _finish(name, w, land, m, v, layer=None, prev=None):
    local = land.shape[1:]
    C = local[-1]
    R = land[0].size // C
    tr = 64 if R % 64 == 0 else R

    def body(land_hbm, g_hbm, land_v, g_v, recv_v, io_sem, send_sem, recv_sem):
        load = pltpu.make_async_copy(land_hbm, land_v, io_sem)
        load.start()
        load.wait()

        def rows_of(i):
            return pl.ds(pl.multiple_of(i * tr, tr), tr)

        def sum_chunk(i, carry):
            rows = rows_of(i)
            g_v[rows, :] = ((land_v[0, rows, :].astype(F32) + land_v[1, rows, :].astype(F32))
                            + land_v[2, rows, :].astype(F32)) + land_v[3, rows, :].astype(F32)
            return carry

        lax.fori_loop(0, R // tr, sum_chunk, 0)
        swap = pltpu.make_async_remote_copy(
            src_ref=g_v, dst_ref=recv_v, send_sem=send_sem, recv_sem=recv_sem,
            device_id=(lax.axis_index("x"), lax.axis_index("y"), 1 - lax.axis_index("c")),
            device_id_type=pl.DeviceIdType.MESH)
        swap.start()
        swap.wait()

        def add_chunk(i, carry):
            rows = rows_of(i)
            recv_v[rows, :] = g_v[rows, :] + recv_v[rows, :]
            return carry

        lax.fori_loop(0, R // tr, add_chunk, 0)
        store = pltpu.make_async_copy(recv_v, g_hbm, io_sem)
        store.start()
        store.wait()

    any_spec = pl.BlockSpec(memory_space=pl.ANY)
    g = pl.pallas_call(
        body, name=name, in_specs=[any_spec], out_specs=any_spec, out_shape=jax.ShapeDtypeStruct((R, C), F32),
        scratch_shapes=[pltpu.VMEM((4, R, C), BF16), pltpu.VMEM((R, C), F32), pltpu.VMEM((R, C), F32),
                        pltpu.SemaphoreType.DMA, pltpu.SemaphoreType.DMA, pltpu.SemaphoreType.DMA],
        compiler_params=pltpu.CompilerParams(has_side_effects=True, vmem_limit_bytes=VMEM_LIMIT))(
            land.reshape(4, R, C))

    lead = () if layer is None else (w.shape[0],)
    as2d = lambda a: a.reshape(lead + (R, C))
    tu = _row_tile(R, C, budget=1 << 19)
    tile = pl.BlockSpec((tu, C), lambda i: (i, 0))
    slab = tile if layer is None else pl.BlockSpec((None, tu, C), lambda i: (layer, i, 0))
    n_prev = 0 if prev is None else 4

    def update(w_ref, g_ref, m_ref, v_ref, *rest):
        outs = rest[n_prev:]
        gv = g_ref[...]
        d, nm, nv = _adamw(w_ref[...], gv, m_ref[...], v_ref[...])
        for r, val in zip(outs, (gv, d, nm, nv)):
            r[...] = val

    res = pl.pallas_call(
        update, name=name + "_adamw", grid=(R // tu,),
        in_specs=[slab, tile, slab, slab] + [any_spec] * n_prev, out_specs=[slab] * 4,
        out_shape=[jax.ShapeDtypeStruct(lead + (R, C), F32)] * 4,
        input_output_aliases={4 + k: k for k in range(n_prev)},
        compiler_params=_cparams("arbitrary"))(
            as2d(w), g, as2d(m), as2d(v), *([] if prev is None else [as2d(p) for p in prev]))
    return [r.reshape(w.shape) for r in res]


def _row(v):
    return v.reshape(1, -1)


def kernel(x, c, positions, ada_w, ada_b, norm_g, mla_w_dq, mla_q_norm_g, mla_w_uq, mla_w_dkv, mla_kv_norm_g, mla_w_ukv, mla_w_o, conv_w_pw1, conv_b_pw1, conv_w_dw, conv_b_dw, conv_ln_g, conv_ln_b, conv_w_pw2, conv_b_pw2, pool_w, pool_b, pool_scale, ffn_w1, ffn_w2, loss_target, m_ada_w, m_ada_b, m_norm_g, m_mla_w_dq, m_mla_q_norm_g, m_mla_w_uq, m_mla_w_dkv, m_mla_kv_norm_g, m_mla_w_ukv, m_mla_w_o, m_conv_w_pw1, m_conv_b_pw1, m_conv_w_dw, m_conv_b_dw, m_conv_ln_g, m_conv_ln_b, m_conv_w_pw2, m_conv_b_pw2, m_pool_w, m_pool_b, m_pool_scale, m_ffn_w1, m_ffn_w2, v_ada_w, v_ada_b, v_norm_g, v_mla_w_dq, v_mla_q_norm_g, v_mla_w_uq, v_mla_w_dkv, v_mla_kv_norm_g, v_mla_w_ukv, v_mla_w_o, v_conv_w_pw1, v_conv_b_pw1, v_conv_w_dw, v_conv_b_dw, v_conv_ln_g, v_conv_ln_b, v_conv_w_pw2, v_conv_b_pw2, v_pool_w, v_pool_b, v_pool_scale, v_ffn_w1, v_ffn_w2):
    args = dict(locals())
    W = {n: args[n] for n in WEIGHTS}
    MOM = {n: args['m_' + n] for n in WEIGHTS}
    VAR = {n: args['v_' + n] for n in WEIGHTS}
    S = x.shape[1]
    xs = x.reshape(S, D_MODEL)
    tgt = loss_target.reshape(S, D_MODEL)
    mx, my, mc = lax.axis_index("x"), lax.axis_index("y"), lax.axis_index("c")
    chip = 2 * mx + my
    n_sh = ada_w.shape[2]

    def sent_of(key):
        n, l = key
        arr = W[n] if l is None else W[n][l]
        return arr.astype(BF16) if n in BIG or n in ('ffn_w1', 'ffn_w2') else arr

    keys0 = [(n, 0) for n in MLA_MATS] + [(n, None) for n in ('norm_g', 'mla_q_norm_g', 'mla_kv_norm_g',
                                                               'conv_w_dw', 'pool_b', 'pool_scale')]
    c8 = _exchange("gather_c", [c.reshape(8, D_MODEL // 8)], 'xyc')[0].reshape(8, D_MODEL)
    sent0 = [sent_of(k) for k in keys0]
    sent0[-1] = sent0[-1] + jnp.minimum(jnp.abs(c8[0, 0]), 0.0)
    fly0 = _split_start("gather_w0_start", sent0)
    c8 = c8 + fly0['token'][0, 0]
    c8 = jnp.pad(c8, ((0, ADA_ROWS - 8), (0, 0)))
    silu = lambda v: v * _sigmoid(v)
    mod_sh = []
    for l in range(DEPTH):
        b_l = lax.dynamic_slice(ada_b[l], (chip * n_sh,), (n_sh,)).reshape(1, n_sh)
        mod_sh.append(_mm(f"ada_fwd{l}", c8, ada_w, 'nn', ADA_ROWS, n_sh, D_MODEL, tn=n_sh // 2, tk=512, pro_a=silu,
                          b_spec=pl.BlockSpec((None, 512, n_sh // 2), lambda i, j, k, l=l: (l, k, j)),
                          extras=[b_l], extra_specs=[_bias_spec(n_sh // 2)], epi=lambda acc, b: (acc + b,))[:8])
    mod_sh = jnp.stack(mod_sh, axis=1).reshape(8, DEPTH * n_sh // 128, 128)
    mod = _exchange("scatter_mod", [mod_sh], 'xy', src_by='xyc')[0]
    mod = mod.reshape(4, DEPTH, n_sh).transpose(1, 0, 2).reshape(DEPTH, 6, 1, D_MODEL)

    keys1 = [('ffn_w1', 0), ('ffn_w2', 0), ('conv_w_pw1', None), ('conv_w_pw2', None), ('pool_w', None)]
    keys2 = [(n, l) for l in range(1, DEPTH) for n in ('ffn_w1', 'ffn_w2')] + [(n, 1) for n in MLA_MATS]
    fly1 = _split_start("gather_w1_start", [sent_of(k) for k in keys1])
    fly2 = _split_start("gather_w2_start", [sent_of(k) for k in keys2])
    mod = mod + (fly1['token'][0, 0] + fly2['token'][0, 0])
    G = dict(zip(keys0, _split_wait("gather_w0_wait", fly0, mod)))

    def whole(key):
        n, l = key
        return _unshard(G[key], SHARD_AXIS[n] - (0 if l is None else 1))

    def mla_params(j):
        P = _mla_weights(*[whole((n, j)) for n in ('mla_w_dq', 'mla_w_dkv', 'mla_w_uq', 'mla_w_ukv', 'mla_w_o')])
        P.update(qg=_row(whole(('mla_q_norm_g', None))[j]), kg=_row(whole(('mla_kv_norm_g', None))[j]))
        return P

    gains = whole(('norm_g', None))
    mla_p = {0: mla_params(0)}
    conv_p = pool_p = None
    rope = _rope_tables(positions.reshape(S, 1).astype(F32))

    by_j = pl.BlockSpec((None, 1024, 1024), lambda i, j, k: (j, 0, 0))
    by_k = pl.BlockSpec((None, 1024, 1024), lambda i, j, k: (k, 0, 0))
    sq_relu = lambda v: jnp.square(jnp.maximum(v, 0.0))

    def md(i, k):
        return mod[i, k]

    (h,), _ = _rowk("pre0", lambda xv, g, sc, sh: ((_pre_fwd(xv, g, sc, sh),), ()),
                    [xs], [_row(gains[0, 0]), md(0, 1), md(0, 0)], [(D_MODEL, BF16)], [])
    saved = []
    xin = xs
    loss_acc = dxf = None
    for i in range(DEPTH):
        kind, j = i % 3, i // 3
        if kind == 0:
            if j not in mla_p:
                mla_p[j] = mla_params(j)
            y, sv = _mla_fwd(j, h, mla_p[j], rope)
        elif kind == 1:
            y, sv = _conv_fwd(h, conv_p)
        else:
            y, sv = _pool_mixer_fwd(h, pool_p)

        def mid(xv, yv, gt, g1, g2, sc, sh):
            x1 = _post_fwd(xv, yv, gt, g1)
            return (x1, _pre_fwd(x1, g2, sc, sh)), ()

        (x1, h2), _ = _rowk(f"mid{i}", mid, [xin, y], [md(i, 2), _row(gains[i, 1]), _row(gains[i, 2]), md(i, 4), md(i, 3)],
                            [(D_MODEL, F32), (D_MODEL, BF16)], [])
        if i == 0:
            G.update(zip(keys1, _split_wait("gather_w1_wait", fly1, h2)))
            conv_p = dict(w_pw1=whole(('conv_w_pw1', None))[0], b_pw1=_row(conv_b_pw1[0]),
                          w_dw=whole(('conv_w_dw', None))[0], b_dw=_row(conv_b_dw[0]), ln_g=_row(conv_ln_g[0]),
                          ln_b=_row(conv_ln_b[0]), w_pw2=whole(('conv_w_pw2', None))[0], b_pw2=_row(conv_b_pw2[0]))
            pool_p = dict(w=whole(('pool_w', None))[0], b=_row(whole(('pool_b', None))[0]),
                          scale=_row(whole(('pool_scale', None))[0]))
        if i == 1:
            G.update(zip(keys2, _split_wait("gather_w2_wait", fly2, h2)))
        a = _mm(f"ffn1_{i}", h2, G[('ffn_w1', i)], 'nn', S, D_FF, D_MODEL, tm=2048, b_spec=by_j,
                outs=[jax.ShapeDtypeStruct((S, D_FF), BF16)])
        y2 = _mm(f"ffn2_{i}", a, G[('ffn_w2', i)], 'nn', S, D_MODEL, D_FF, pro_a=sq_relu, b_spec=by_k)
        saved.append(dict(x0=xin, h=h, y=y, x1=x1, h2=h2, a=a, y2=y2, mix=sv))
        if i + 1 < DEPTH:
            def nxt(xv, yv, gt, g3, g0, sc, sh):
                x2 = _post_fwd(xv, yv, gt, g3)
                return (x2, _pre_fwd(x2, g0, sc, sh)), ()

            hdt = F32 if (i + 1) % 3 == 2 else BF16
            (xin, h), _ = _rowk(f"next{i}", nxt, [x1, y2],
                                [md(i, 5), _row(gains[i, 3]), _row(gains[i + 1, 0]), md(i + 1, 1), md(i + 1, 0)],
                                [(D_MODEL, F32), (D_MODEL, hdt)], [])
        else:
            def head(xv, yv, tv, gt, g3):
                err = _post_fwd(xv, yv, gt, g3) - tv
                per_row = jnp.sum(err * err, axis=1, keepdims=True) * (0.5 / D_MODEL)
                return (err * (1.0 / D_MODEL),), (jnp.broadcast_to(jnp.sum(per_row, axis=0, keepdims=True), (1, 128)),)

            (dxf,), (loss_acc,) = _rowk("loss_head", head, [x1, y2, tgt], [md(i, 5), _row(gains[i, 3])],
                                        [(D_MODEL, F32)], [128])

    small = {}
    big = {}
    landed = {}

    def keep(gm, layer):
        for n, g in gm.items():
            if n in BIG:
                g = g[None] if layer is None else g
                big[(n, layer)] = _to_shards(g, SHARD_AXIS[n] - (0 if layer is None else 1)).astype(BF16)
            else:
                small.setdefault(n, {})[layer or 0] = g

    d_mod = [None] * DEPTH
    d_gain = [None] * DEPTH
    dx = dxf
    for i in reversed(range(DEPTH)):
        kind, j = i % 3, i // 3
        sv = saved[i]
        def post2_bwd(d, yv, gt, g):
            dyv, d_gt, d_g = _post_bwd(d, yv, gt, g)
            return (dyv,), (d_gt, d_g)

        (dy2,), (d_gtf, d_g3) = _rowk(f"post2_bwd{i}", post2_bwd, [dx, sv['y2']], [md(i, 5), _row(gains[i, 3])],
                                      [(D_MODEL, BF16)], [D_MODEL] * 2)
        da = _mm(f"ffn2_dg{i}", dy2, G[('ffn_w2', i)], 'nt', S, D_FF, D_MODEL, tm=2048, b_spec=by_j, extras=[sv['a']],
                 extra_specs=[pl.BlockSpec((min(2048, S), 1024), lambda i_, j_, k_: (i_, j_))],
                 epi=lambda acc, av: (acc * (2.0 * jnp.maximum(av, 0.0)),),
                 outs=[jax.ShapeDtypeStruct((S, D_FF), BF16)])
        big[('ffn_w2', i)] = _mm(f"ffn2_wg{i}", sv['a'], dy2, 'tn', D_FF, D_MODEL, S, tk=2048, pro_a=sq_relu,
                        outs=[jax.ShapeDtypeStruct((4, 1024, D_MODEL), BF16)],
                        out_specs=[pl.BlockSpec((None, 1024, 1024), lambda i_, j_, k_: (i_, 0, j_))])
        big[('ffn_w1', i)] = _mm(f"ffn1_wg{i}", sv['h2'], da, 'tn', D_MODEL, D_FF, S, tk=2048,
                        outs=[jax.ShapeDtypeStruct((4, D_MODEL, 1024), BF16)],
                        out_specs=[pl.BlockSpec((None, 1024, 1024), lambda i_, j_, k_: (j_, i_, 0))])
        dh2 = _mm(f"ffn1_dg{i}", da, G[('ffn_w1', i)], 'nt', S, D_MODEL, D_FF, tm=2048, b_spec=by_k)
        if i == DEPTH - 1:
            keys_a = [('ffn_w1', i), ('ffn_w2', i)]
            fly_a = _split_start("scatter_ga_start", [big[k] for k in keys_a], src_by='xy')
            mod = mod + fly_a['token'][0, 0]
        if i == 0:
            keys_b = [k for k in big if k not in keys_a]
            fly_b = _split_start("scatter_gb_start", [big[k] for k in keys_b], src_by='xy')
            mod = mod + fly_b['token'][0, 0]

        def mid_bwd(d2, dh2v, x1v, yv, g2, scf, gtm, g1):
            dpre, d_sh, d_sc, d_g2 = _pre_bwd(dh2v, x1v, g2, scf)
            d1 = d2 + dpre
            dyv, d_gt, d_g1 = _post_bwd(d1, yv, gtm, g1)
            return (d1, dyv), (d_sh, d_sc, d_g2, d_gt, d_g1, _colsum(dyv))

        ydt = F32 if kind == 2 else BF16
        (dx1, dy), (d_shf, d_scf, d_g2, d_gtm, d_g1, dy_cs) = _rowk(
            f"mid_bwd{i}", mid_bwd, [dx, dh2, sv['x1'], sv['y']],
            [_row(gains[i, 2]), md(i, 4), md(i, 2), _row(gains[i, 1])],
            [(D_MODEL, F32), (D_MODEL, ydt)], [D_MODEL] * 6)
        if kind == 0:
            dh, gm = _mla_bwd(j, dy, sv['h'], sv['mix'], mla_p[j], rope)
            keep(gm, j)
        elif kind == 1:
            dh, gm = _conv_bwd(dy, dy_cs, sv['h'], sv['mix'], conv_p)
            keep(gm, None)
        else:
            dh, gm = _pool_mixer_bwd(dy, sv['mix'], pool_p)
            keep(gm, None)

        def pre_bwd(d1, dhv, x0v, g0, scm):
            dpre, d_sh, d_sc, d_g0 = _pre_bwd(dhv, x0v, g0, scm)
            return (d1 + dpre,), (d_sh, d_sc, d_g0)

        (dx,), (d_shm, d_scm, d_g0) = _rowk(f"pre_bwd{i}", pre_bwd, [dx1, dh, sv['x0']],
                                            [_row(gains[i, 0]), md(i, 1)], [(D_MODEL, F32)], [D_MODEL] * 3)
        d_mod[i] = jnp.concatenate([d_shm, d_scm, d_gtm, d_shf, d_scf, d_gtf], axis=1).reshape(-1)
        d_gain[i] = jnp.concatenate([d_g0, d_g1, d_g2, d_g3], axis=0)
        if i == DEPTH - 1:
            landed.update(zip(keys_a, _split_wait("scatter_ga_wait", fly_a, dx)))
    landed.update(zip(keys_b, _split_wait("scatter_gb_wait", fly_b, dx)))
    keys_c = [(n, 0) for n in MLA_MATS]
    fly_c = _split_start("scatter_gc_start", [big[k] for k in keys_c], src_by='xy')
    grad_x = dx.reshape(x.shape)
    grads = {n: jnp.stack([g[l] for l in sorted(g)]) for n, g in small.items()}
    grads['norm_g'] = jnp.stack(d_gain)
    grads['ada_b'] = jnp.stack(d_mod)

    pack = jnp.concatenate([grads[n].reshape(-1) for n in SMALL] + [loss_acc[0, :1]])
    n_pack = pack.shape[0]
    rows = -(-n_pack // 1024) * 8
    pack = jnp.pad(pack, (0, rows * 128 - n_pack)).reshape(rows, 128)
    fly_s = _split_start("gather_small_start", [pack], group='xyc')

    out_g, out_d, out_m, out_v = {}, {}, {}, {}
    chains = {}
    for n in BIG + ['ffn_w1', 'ffn_w2']:
        if n in MLA_MATS:
            chains[n] = _finish(f"finish_{n}1", W[n], landed[(n, 1)], MOM[n], VAR[n], layer=1)
        elif n in BIG:
            out_g[n], out_d[n], out_m[n], out_v[n] = _finish(f"finish_{n}", W[n], landed[(n, None)], MOM[n], VAR[n])
        else:
            res = None
            for l in range(DEPTH):
                res = _finish(f"finish_{n}{l}", W[n], landed[(n, l)], MOM[n], VAR[n], layer=l, prev=res)
            out_g[n], out_d[n], out_m[n], out_v[n] = res
    landed.update(zip(keys_c, _split_wait("scatter_gc_wait", fly_c, out_g['ffn_w2'])))
    for n in MLA_MATS:
        out_g[n], out_d[n], out_m[n], out_v[n] = _finish(f"finish_{n}0", W[n], landed[(n, 0)], MOM[n], VAR[n],
                                                         layer=0, prev=chains[n])
    pack8 = _split_wait("gather_small_wait", fly_s, out_g['mla_w_o'])[0]
    (tot,) = _ew("sum_small", lambda *v: (functools.reduce(lambda p, q: p + q, v),), [(pack8, s) for s in range(8)],
                 [F32], (rows, 128))
    tot = tot.reshape(-1)
    loss = tot[n_pack - 1]
    d_mod_all = pack8.reshape(8, -1)[:, :DEPTH * 6 * D_MODEL].reshape(8, DEPTH, 6 * D_MODEL)
    final = {}
    off = 0
    for n in SMALL:
        ax = SHARD_AXIS[n]
        shape = tuple(d * 4 if k == ax else d for k, d in enumerate(W[n].shape))
        size = grads[n].size
        g = tot[off:off + size].reshape(shape)
        off += size
        if ax is not None:
            g = lax.dynamic_index_in_dim(_to_shards(g, ax), chip, 0, keepdims=False)
        final[n] = g

    g_ada = []
    for l in range(DEPTH):
        dm_l = jnp.pad(lax.dynamic_slice(d_mod_all[:, l], (0, chip * n_sh), (8, n_sh)), ((0, ADA_ROWS - 8), (0, 0)))
        g_ada.append(_mm(f"ada_wg{l}", c8, dm_l, 'tn', D_MODEL, n_sh, ADA_ROWS, tn=n_sh // 2, pro_a=silu))
    final['ada_w'] = jnp.stack(g_ada)

    for n in WEIGHTS:
        if n in out_g:
            continue
        shape = W[n].shape
        out_g[n] = final[n].reshape(shape)
        out_d[n], out_m[n], out_v[n] = _ew(f"adamw_{n}", lambda w, g, m, v: _adamw(w, g, m, v),
                                           [W[n], out_g[n], MOM[n], VAR[n]], [F32] * 3, shape)
    return (loss, grad_x, *[out_g[n] for n in WEIGHTS], *[out_d[n] for n in WEIGHTS],
            *[out_m[n] for n in WEIGHTS], *[out_v[n] for n in WEIGHTS])
```

```python
import functools
import math

import jax
import jax.numpy as jnp
from jax import lax
from jax.experimental import pallas as pl
from jax.experimental.pallas import tpu as pltpu

F32 = jnp.float32
BF16 = jnp.bfloat16

D_MODEL = 1024
DEPTH = 4
N_HEADS = 16
QK_NOPE = 64
QK_ROPE = 32
V_HEAD = 64
Q_LORA = 384
KV_LORA = 256
HEAD_PAD = 128
QW = N_HEADS * HEAD_PAD
KVW = 2 * QW
DKV = KV_LORA + QK_ROPE
DQKV = Q_LORA + DKV
D_FF = 4096
CONV_WIDTH = 31
POOL_WINDOWS = (2, 4, 8, 16)
CHUNK_SHIFT = 6
ROPE_THETA = 10000.0
NORM_EPS = 1e-6
NEG_INF = -1e30
ATT_SCALE = 1.0 / math.sqrt(QK_NOPE + QK_ROPE)
BQ = 256
HB = 8
HF = 8
LOG2E = 1.4426950408889634
SCALE_LOG2E = ATT_SCALE * LOG2E
PAD_ROWS = 32
ADA_ROWS = 128
VMEM_LIMIT = 56 * 1024 * 1024

ADAM_LR = 0.001
ADAM_B1 = 0.9
ADAM_B2 = 0.999
ADAM_EPS = 1e-08
ADAM_WD = 0.01
ADAM_STEP = 10

WEIGHTS = ['ada_w', 'ada_b', 'norm_g', 'mla_w_dq', 'mla_q_norm_g', 'mla_w_uq', 'mla_w_dkv', 'mla_kv_norm_g',
           'mla_w_ukv', 'mla_w_o', 'conv_w_pw1', 'conv_b_pw1', 'conv_w_dw', 'conv_b_dw', 'conv_ln_g', 'conv_ln_b',
           'conv_w_pw2', 'conv_b_pw2', 'pool_w', 'pool_b', 'pool_scale', 'ffn_w1', 'ffn_w2']
SHARD_AXIS = {'ada_w': 2, 'ada_b': None, 'norm_g': 2, 'mla_w_dq': 1, 'mla_q_norm_g': 1, 'mla_w_uq': 2,
              'mla_w_dkv': 1, 'mla_kv_norm_g': 1, 'mla_w_ukv': 2, 'mla_w_o': 1, 'conv_w_pw1': 2,
              'conv_b_pw1': None, 'conv_w_dw': 2, 'conv_b_dw': None, 'conv_ln_g': None, 'conv_ln_b': None,
              'conv_w_pw2': 1, 'conv_b_pw2': None, 'pool_w': 2, 'pool_b': 2, 'pool_scale': 1,
              'ffn_w1': 2, 'ffn_w2': 1}
MLA_MATS = ['mla_w_dq', 'mla_w_uq', 'mla_w_dkv', 'mla_w_ukv', 'mla_w_o']
BIG = MLA_MATS + ['conv_w_pw1', 'conv_w_pw2', 'pool_w']
SMALL = ['ada_b', 'norm_g', 'mla_q_norm_g', 'mla_kv_norm_g', 'conv_b_pw1', 'conv_w_dw', 'conv_b_dw',
         'conv_ln_g', 'conv_ln_b', 'conv_b_pw2', 'pool_b', 'pool_scale']


def _cparams(*sem):
    return pltpu.CompilerParams(dimension_semantics=sem, vmem_limit_bytes=VMEM_LIMIT)


def _colsum(v):
    return jnp.sum(v, axis=0, keepdims=True)


def _rowmean(v):
    return jnp.mean(v, axis=-1, keepdims=True)


def _sigmoid(v):
    return 1.0 / (1.0 + jnp.exp(-v))


def _rowk(name, fn, rows, bcast, out_row, out_acc, tm=512):
    S = rows[0].shape[0]
    tm = min(tm, S)
    while S % tm:
        tm //= 2
    nin, no, na = len(rows) + len(bcast), len(out_row), len(out_acc)

    def body(*refs):
        vals = [r[...] for r in refs[:nin]]
        outs = refs[nin:nin + no]
        accs = refs[nin + no:]
        ro, ao = fn(*vals)
        for r, v in zip(outs, ro):
            r[...] = v.astype(r.dtype)
        if na:
            @pl.when(pl.program_id(0) == 0)
            def _():
                for r in accs:
                    r[...] = jnp.zeros(r.shape, r.dtype)
            for r, v in zip(accs, ao):
                r[...] += v

    in_specs = [pl.BlockSpec((tm, a.shape[1]), lambda i: (i, 0)) for a in rows]
    in_specs += [pl.BlockSpec(b.shape, lambda i, n=b.ndim: (0,) * n) for b in bcast]
    out_shape = [jax.ShapeDtypeStruct((S, w), dt) for w, dt in out_row]
    out_shape += [jax.ShapeDtypeStruct((1, w), F32) for w in out_acc]
    out_specs = [pl.BlockSpec((tm, w), lambda i: (i, 0)) for w, _ in out_row]
    out_specs += [pl.BlockSpec((1, w), lambda i: (0, 0)) for w in out_acc]
    res = pl.pallas_call(body, name=name, grid=(S // tm,), in_specs=in_specs, out_specs=out_specs,
                         out_shape=out_shape, compiler_params=_cparams("arbitrary"))(*rows, *bcast)
    return list(res[:no]), list(res[no:])


_DIMS = {'nn': ((1,), (0,)), 'nt': ((1,), (1,)), 'tn': ((0,), (0,))}


def _mm(name, a, b, mode, M, N, K, *, tm=1024, tn=1024, tk=1024, a_spec=None, b_spec=None, pro_a=None,
        extras=(), extra_specs=(), epi=None, outs=None, out_specs=None):
    tm, tn, tk = (t if d % t == 0 else d for t, d in ((min(tm, M), M), (min(tn, N), N), (min(tk, K), K)))
    nk = K // tk
    if a_spec is None:
        a_spec = (pl.BlockSpec((tk, tm), lambda i, j, k: (k, i)) if mode == 'tn'
                  else pl.BlockSpec((tm, tk), lambda i, j, k: (i, k)))
    if b_spec is None:
        b_spec = (pl.BlockSpec((tn, tk), lambda i, j, k: (j, k)) if mode == 'nt'
                  else pl.BlockSpec((tk, tn), lambda i, j, k: (k, j)))
    if outs is None:
        outs = [jax.ShapeDtypeStruct((M, N), F32)]
    if out_specs is None:
        out_specs = [pl.BlockSpec((tm, tn), lambda i, j, k: (i, j)) for _ in outs]
    ne, no = len(extras), len(outs)
    dims = (_DIMS[mode], ((), ()))

    def body(a_ref, b_ref, *rest):
        ex, out_refs = rest[:ne], rest[ne:ne + no]
        av = a_ref[...]
        if pro_a is not None:
            av = pro_a(av)
        part = lax.dot_general(av.astype(BF16), b_ref[...].astype(BF16), dims, preferred_element_type=F32)

        def finish(acc):
            vals = (acc,) if epi is None else epi(acc, *[e[...] for e in ex])
            for r, v in zip(out_refs, vals):
                r[...] = v.astype(r.dtype)

        if nk == 1:
            finish(part)
            return
        acc_ref = rest[ne + no]
        k = pl.program_id(2)

        @pl.when(k == 0)
        def _():
            acc_ref[...] = part

        @pl.when(k > 0)
        def _():
            acc_ref[...] += part

        @pl.when(k == nk - 1)
        def _():
            finish(acc_ref[...])

    res = pl.pallas_call(
        body, name=name, grid=(M // tm, N // tn, nk),
        in_specs=[a_spec, b_spec, *extra_specs], out_specs=list(out_specs), out_shape=list(outs),
        scratch_shapes=[pltpu.VMEM((tm, tn), F32)] if nk > 1 else [],
        compiler_params=_cparams("parallel", "parallel", "arbitrary"))(a, b, *extras)
    return res[0] if no == 1 else list(res)


def _row_tile(R, C, itemsize=4, budget=1 << 20):
    if R * C * itemsize <= budget or R % 8:
        return R
    t = 8
    while R % (t * 2) == 0 and t * 2 * C * itemsize <= budget:
        t *= 2
    return t


def _ew(name, fn, ins, out_dtypes, shape):
    C = shape[-1]
    R = 1
    for s in shape[:-1]:
        R *= s
    tr = _row_tile(R, C)
    ops, specs = [], []
    for it in ins:
        if isinstance(it, tuple):
            arr, idx = it
            ops.append(arr.reshape(arr.shape[0], R, C))
            specs.append(pl.BlockSpec((None, tr, C), lambda i, n=idx: (n, i, 0)))
        else:
            ops.append(it.reshape(R, C))
            specs.append(pl.BlockSpec((tr, C), lambda i: (i, 0)))
    nin = len(ops)

    def body(*refs):
        vals = fn(*[r[...] for r in refs[:nin]])
        for r, v in zip(refs[nin:], vals):
            r[...] = v.astype(r.dtype)

    res = pl.pallas_call(
        body, name=name, grid=(R // tr,), in_specs=specs,
        out_specs=[pl.BlockSpec((tr, C), lambda i: (i, 0)) for _ in out_dtypes],
        out_shape=[jax.ShapeDtypeStruct((R, C), dt) for dt in out_dtypes],
        compiler_params=_cparams("parallel"))(*ops)
    return [r.reshape(shape) for r in res]


_FLIPS = {'xyc': [(fx, fy, fc) for fx in (0, 1) for fy in (0, 1) for fc in (0, 1)][1:],
          'xy': [(1, 0, 0), (0, 1, 0), (1, 1, 0)],
          'c': [(0, 0, 1)]}
_NSLOT = {'xyc': 8, 'xy': 4, 'c': 2}


def _slot(kind, cx, cy, cc):
    return {'xyc': 4 * cx + 2 * cy + cc, 'xy': 2 * cx + cy, 'c': cc}[kind]


def _put_own(land, arr, group, src_by):
    coords = (lax.axis_index("x"), lax.axis_index("y"), lax.axis_index("c"))
    pay = arr if src_by is None else lax.dynamic_index_in_dim(arr, _slot(src_by, *coords), 0, keepdims=False)
    return lax.dynamic_update_index_in_dim(land, pay, _slot(group, *coords), 0)


def _exchange(name, arrays, group, src_by=None):
    flips, nsl, n = _FLIPS[group], _NSLOT[group], len(arrays)
    nf = len(flips)

    def body(*refs):
        ins, outs = refs[:n], refs[n:2 * n]
        send_sems, recv_sems = refs[2 * n:]
        mx, my, mc = lax.axis_index("x"), lax.axis_index("y"), lax.axis_index("c")
        me = _slot(group, mx, my, mc)

        def payload(a, cx, cy, cc):
            return ins[a] if src_by is None else ins[a].at[_slot(src_by, cx, cy, cc)]

        sends, recvs = [], []
        for a in range(n):
            for f, (fx, fy, fc) in enumerate(flips):
                px = 1 - mx if fx else mx
                py = 1 - my if fy else my
                pc = 1 - mc if fc else mc
                src = payload(a, px, py, pc)
                sends.append(pltpu.make_async_remote_copy(
                    src_ref=src, dst_ref=outs[a].at[me], send_sem=send_sems.at[a, f],
                    recv_sem=recv_sems.at[a, f], device_id=(px, py, pc),
                    device_id_type=pl.DeviceIdType.MESH))
                recvs.append(pltpu.make_async_remote_copy(
                    src_ref=src, dst_ref=outs[a].at[_slot(group, px, py, pc)], send_sem=send_sems.at[a, f],
                    recv_sem=recv_sems.at[a, f], device_id=(px, py, pc),
                    device_id_type=pl.DeviceIdType.MESH))
        for cp in sends:
            cp.start()
        for cp in recvs:
            cp.wait_recv()
        for cp in sends:
            cp.wait_send()

    out_shape = [jax.ShapeDtypeStruct((nsl,) + (a.shape if src_by is None else a.shape[1:]), a.dtype)
                 for a in arrays]
    any_spec = pl.BlockSpec(memory_space=pl.ANY)
    res = pl.pallas_call(
        body, name=name, in_specs=[any_spec] * n, out_specs=[any_spec] * n, out_shape=out_shape,
        scratch_shapes=[pltpu.SemaphoreType.DMA((n, nf)), pltpu.SemaphoreType.DMA((n, nf))],
        compiler_params=pltpu.CompilerParams(has_side_effects=True))(*arrays)
    return [_put_own(l, a, group, src_by) for a, l in zip(arrays, res)]


_HBM = pl.BlockSpec(memory_space=pltpu.HBM)
_SEM = pl.BlockSpec(memory_space=pltpu.SEMAPHORE)
_DATAFLOW = pltpu.SideEffectType.DATAFLOW_SIDE_EFFECTING


def _group_copies(ins, lands, send_sems, recv_sems, group, src_by):
    mx, my, mc = lax.axis_index("x"), lax.axis_index("y"), lax.axis_index("c")
    me = _slot(group, mx, my, mc)
    pairs = []
    for a in range(len(ins)):
        for fx, fy, fc in _FLIPS[group]:
            peer = (1 - mx if fx else mx, 1 - my if fy else my, 1 - mc if fc else mc)
            src = ins[a] if src_by is None else ins[a].at[_slot(src_by, *peer)]
            mk = functools.partial(pltpu.make_async_remote_copy, src_ref=src, send_sem=send_sems,
                                   recv_sem=recv_sems, device_id=peer, device_id_type=pl.DeviceIdType.MESH)
            pairs.append((mk(dst_ref=lands[a].at[me]), mk(dst_ref=lands[a].at[_slot(group, *peer)])))
    return pairs


def _split_start(name, arrays, group='xy', src_by=None):
    n = len(arrays)
    lands = [lax.empty((_NSLOT[group],) + (a.shape if src_by is None else a.shape[1:]), a.dtype) for a in arrays]

    def body(*refs):
        ins, lnd, send_sems, recv_sems, token = refs[:n], refs[n:2 * n], refs[2 * n], refs[2 * n + 1], refs[-1]
        for to_peer, _ in _group_copies(ins, lnd, send_sems, recv_sems, group, src_by):
            to_peer.start()
        token[...] = jnp.zeros(token.shape, F32)

    ops = [pltpu.with_memory_space_constraint(a, pltpu.HBM) for a in [*arrays, *lands]]
    res = pl.pallas_call(
        body, name=name, in_specs=[_HBM] * (2 * n),
        out_specs=[_SEM, _SEM] + [_HBM] * (2 * n) + [pl.BlockSpec(memory_space=pltpu.VMEM)],
        out_shape=[pltpu.SemaphoreType.DMA(()), pltpu.SemaphoreType.DMA(())]
        + [pltpu.HBM(a.shape, a.dtype) for a in ops] + [jax.ShapeDtypeStruct((8, 128), F32)],
        input_output_aliases={k: 2 + k for k in range(2 * n)},
        compiler_params=pltpu.CompilerParams(has_side_effects=_DATAFLOW))(*ops)
    return dict(n=n, group=group, src_by=src_by, send=res[0], recv=res[1], arrays=list(res[2:2 + n]),
                lands=list(res[2 + n:2 + 2 * n]), token=res[-1])


def _split_wait(name, st, after, fill_own=True):
    n, group, src_by = st['n'], st['group'], st['src_by']

    def wait_body(*refs):
        ins, lnd, send_sems, recv_sems = refs[:n], refs[n:2 * n], refs[2 * n], refs[2 * n + 1]
        for to_peer, from_peer in _group_copies(ins, lnd, send_sems, recv_sems, group, src_by):
            to_peer.wait_send()
            from_peer.wait_recv()

    shapes = [pltpu.HBM(a.shape, a.dtype) for a in [*st['arrays'], *st['lands']]]
    res = pl.pallas_call(
        wait_body, name=name, in_specs=[_HBM] * (2 * n) + [_SEM, _SEM, pl.BlockSpec(memory_space=pl.ANY)],
        out_specs=[_HBM] * (2 * n), out_shape=shapes, input_output_aliases={k: k for k in range(2 * n)},
        compiler_params=pltpu.CompilerParams(has_side_effects=_DATAFLOW))(
            *st['arrays'], *st['lands'], st['send'], st['recv'], after)
    if not fill_own:
        return list(res[n:])
    return [_put_own(l, a, group, src_by) for a, l in zip(res[:n], res[n:])]


def _unshard(g, axis):
    t = jnp.moveaxis(g, 0, axis)
    s = t.shape
    return t.reshape(s[:axis] + (s[axis] * s[axis + 1],) + s[axis + 2:])


def _to_shards(w, axis):
    s = w.shape
    t = w.reshape(s[:axis] + (4, s[axis] // 4) + s[axis + 1:])
    return jnp.moveaxis(t, axis, 0)


def _pre_fwd(x, g, sc, sh):
    r = lax.rsqrt(_rowmean(x * x) + NORM_EPS)
    return (x * r) * g * (1.0 + sc) + sh


def _pre_bwd(dh, x, g, sc):
    r = lax.rsqrt(_rowmean(x * x) + NORM_EPS)
    xn = x * r
    dxn = dh * (g * (1.0 + sc))
    dx = r * (dxn - xn * _rowmean(dxn * xn))
    t = dh * xn
    return dx, _colsum(dh), _colsum(t * g), _colsum(t * (1.0 + sc))


def _post_fwd(x, y, gt, g):
    r = lax.rsqrt(_rowmean(y * y) + NORM_EPS)
    return x + gt * ((y * r) * g)


def _post_bwd(dxo, y, gt, g):
    r = lax.rsqrt(_rowmean(y * y) + NORM_EPS)
    yn = y * r
    t = dxo * yn
    dyn = dxo * (gt * g)
    dy = r * (dyn - yn * _rowmean(dyn * yn))
    return dy, _colsum(t * g), _colsum(t * gt)


def _gain_bwd(dy, x, g):
    r = lax.rsqrt(_rowmean(x * x) + NORM_EPS)
    xn = x * r
    dxn = dy * g
    return r * (dxn - xn * _rowmean(dxn * xn)), _colsum(dy * xn)


def _rope(x, cos, sa, sb):
    return x * cos + pltpu.roll(x, HEAD_PAD - 16, 1) * sa + pltpu.roll(x, 16, 1) * sb


def _rope_t(d, cos, sa, sb):
    return d * cos + pltpu.roll(d * sa, 16, 1) + pltpu.roll(d * sb, HEAD_PAD - 16, 1)


def _rope_tables(pos_f):
    S = pos_f.shape[0]
    inv = ROPE_THETA ** (-jnp.arange(0, QK_ROPE, 2, dtype=F32) / QK_ROPE)
    inv_ext = jnp.concatenate([jnp.zeros((QK_NOPE,), F32), inv, inv,
                               jnp.zeros((HEAD_PAD - QK_NOPE - QK_ROPE,), F32)]).reshape(1, HEAD_PAD)

    def fn(p, iv):
        ang = p * iv
        lane = lax.broadcasted_iota(jnp.int32, ang.shape, 1)
        s = jnp.sin(ang)
        first = (lane >= QK_NOPE) & (lane < QK_NOPE + QK_ROPE // 2)
        second = (lane >= QK_NOPE + QK_ROPE // 2) & (lane < QK_NOPE + QK_ROPE)
        return (jnp.cos(ang), jnp.where(first, -s, 0.0), jnp.where(second, s, 0.0)), ()

    (cos, sa, sb), _ = _rowk("rope_tables", fn, [pos_f], [inv_ext], [(HEAD_PAD, F32)] * 3, [])
    return cos, sa, sb


def _diag_mask(transposed):
    r = lax.broadcasted_iota(jnp.int32, (BQ, BQ), 0) >> CHUNK_SHIFT
    c = lax.broadcasted_iota(jnp.int32, (BQ, BQ), 1) >> CHUNK_SHIFT
    return (r <= c) if transposed else (c <= r)


_NT = (((1,), (1,)), ((), ()))
_NN = (((1,), (0,)), ((), ()))


def _attn_fwd(qf, kvf, HB=HF):
    S = qf.shape[0]
    nq = S // BQ

    def body(q_ref, kv_ref, o_ref, ob_ref, lse_ref):
        qi = pl.program_id(1)
        qs = [q_ref[:, hh * HEAD_PAD:(hh + 1) * HEAD_PAD] for hh in range(HB)]

        def step(j, carry, diag):
            off = pl.multiple_of(j * BQ, BQ)
            sts = [lax.dot_general(kv_ref[pl.ds(off, BQ), pl.ds(2 * hh * HEAD_PAD, HEAD_PAD)], qs[hh], _NT,
                                   preferred_element_type=F32) for hh in range(HB)]
            mid = []
            for hh in range(HB):
                m, l, acc = carry[hh]
                st = jnp.where(_diag_mask(True), sts[hh], NEG_INF) if diag else sts[hh]
                m2 = jnp.maximum(m, jnp.max(st, axis=0, keepdims=True))
                al = jnp.exp2(m - m2)
                pt = jnp.exp2(st - m2)
                mid.append((m2, l * al + jnp.sum(pt, axis=0, keepdims=True), acc * al, pt.astype(BF16)))
            out = []
            for hh in range(HB):
                m2, l2, acc_s, ptb = mid[hh]
                v = kv_ref[pl.ds(off, BQ), pl.ds((2 * hh + 1) * HEAD_PAD, HEAD_PAD)]
                out.append((m2, l2, acc_s + lax.dot_general(v, ptb, _TN, preferred_element_type=F32)))
            return tuple(out)

        init = tuple((jnp.full((1, BQ), NEG_INF, F32), jnp.zeros((1, BQ), F32), jnp.zeros((HEAD_PAD, BQ), F32))
                     for _ in range(HB))
        carry = lax.fori_loop(0, qi, lambda j, c: step(j, c, False), init)
        carry = step(qi, carry, True)
        for hh in range(HB):
            m, l, acc = carry[hh]
            ov = (acc / l).T
            o_ref[:, hh * HEAD_PAD:(hh + 1) * HEAD_PAD] = ov
            ob_ref[:, hh * HEAD_PAD:(hh + 1) * HEAD_PAD] = ov.astype(BF16)
            lse_ref[hh] = m + jnp.log(l) * LOG2E

    return pl.pallas_call(
        body, name="attn_fwd", grid=(N_HEADS // HB, nq),
        in_specs=[pl.BlockSpec((BQ, HB * HEAD_PAD), lambda g, i: (i, g)),
                  pl.BlockSpec((S, 2 * HB * HEAD_PAD), lambda g, i: (0, g))],
        out_specs=[pl.BlockSpec((BQ, HB * HEAD_PAD), lambda g, i: (i, g)),
                   pl.BlockSpec((BQ, HB * HEAD_PAD), lambda g, i: (i, g)),
                   pl.BlockSpec((HB, None, 1, BQ), lambda g, i: (g, i, 0, 0))],
        out_shape=[jax.ShapeDtypeStruct((S, QW), F32), jax.ShapeDtypeStruct((S, QW), BF16),
                   jax.ShapeDtypeStruct((N_HEADS, nq, 1, BQ), F32)],
        compiler_params=_cparams("parallel", "arbitrary"))(qf, kvf)


def _attn_delta(dob, o):
    S = o.shape[0]

    def fn(dov, ov):
        prod = dov.astype(F32) * ov
        lane = lax.broadcasted_iota(jnp.int32, (prod.shape[0], HEAD_PAD), 1)
        out = jnp.zeros((prod.shape[0], HEAD_PAD), F32)
        for h in range(N_HEADS):
            out = jnp.where(lane == h, jnp.sum(prod[:, h * HEAD_PAD:(h + 1) * HEAD_PAD], axis=1, keepdims=True), out)
        return (out,), ()

    (dd,), _ = _rowk("attn_delta", fn, [dob, o], [], [(HEAD_PAD, F32)], [])
    return dd[:, :N_HEADS].T.reshape(N_HEADS, S // BQ, 1, BQ)


_TN = (((0,), (0,)), ((), ()))


def _attn_bwd(qf, kvf, dob, lse_row, dd_row, cos, sa, sb):
    S = qf.shape[0]
    nq = S // BQ

    def body(kv_ref, q_ref, do_ref, lse_ref, dd_ref, cos_ref, sa_ref, sb_ref, dqo_ref, dkv_ref, dq_ref):
        kj = pl.program_id(1)

        @pl.when(kj == 0)
        def _():
            dq_ref[...] = jnp.zeros(dq_ref.shape, F32)

        ks = [kv_ref[:, 2 * hh * HEAD_PAD:(2 * hh + 1) * HEAD_PAD] for hh in range(HB)]
        vs = [kv_ref[:, (2 * hh + 1) * HEAD_PAD:(2 * hh + 2) * HEAD_PAD] for hh in range(HB)]

        def step(i, carry, diag):
            off = pl.multiple_of(i * BQ, BQ)
            cols = [pl.ds(hh * HEAD_PAD, HEAD_PAD) for hh in range(HB)]
            q = [q_ref[pl.ds(off, BQ), cols[hh]] for hh in range(HB)]
            do = [do_ref[pl.ds(off, BQ), cols[hh]] for hh in range(HB)]
            sts = [lax.dot_general(ks[hh], q[hh], _NT, preferred_element_type=F32) for hh in range(HB)]
            dpts = [lax.dot_general(vs[hh], do[hh], _NT, preferred_element_type=F32) for hh in range(HB)]
            mid = []
            for hh in range(HB):
                st = jnp.where(_diag_mask(True), sts[hh], NEG_INF) if diag else sts[hh]
                pt = jnp.exp2(st - lse_ref[hh, i])
                mid.append((pt.astype(BF16), (pt * (dpts[hh] - dd_ref[hh, i])).astype(BF16)))
            out = []
            for hh in range(HB):
                dk, dv = carry[hh]
                ptb, dsb = mid[hh]
                dv2 = dv + lax.dot_general(ptb, do[hh], _NN, preferred_element_type=F32)
                dk2 = dk + lax.dot_general(dsb, q[hh], _NN, preferred_element_type=F32)
                dq_ref[pl.ds(off, BQ), cols[hh]] += lax.dot_general(dsb, ks[hh], _TN, preferred_element_type=F32)
                out.append((dk2, dv2))
            return tuple(out)

        zero = jnp.zeros((BQ, HEAD_PAD), F32)
        carry = step(kj, tuple((zero, zero) for _ in range(HB)), True)
        carry = lax.fori_loop(kj + 1, nq, lambda i, c: step(i, c, False), carry)
        done = pl.ds(pl.multiple_of(kj * BQ, BQ), BQ)
        for hh in range(HB):
            dk, dv = carry[hh]
            dk = _rope_t(dk * (1.0 / LOG2E), cos_ref[...], sa_ref[...], sb_ref[...])
            dkv_ref[:, 2 * hh * HEAD_PAD:(2 * hh + 1) * HEAD_PAD] = dk.astype(BF16)
            dkv_ref[:, (2 * hh + 1) * HEAD_PAD:(2 * hh + 2) * HEAD_PAD] = dv.astype(BF16)
            cols = pl.ds(hh * HEAD_PAD, HEAD_PAD)
            dqo_ref[:, cols] = _rope_t(dq_ref[done, cols] * ATT_SCALE, cos_ref[...], sa_ref[...],
                                       sb_ref[...]).astype(BF16)

    tab = pl.BlockSpec((BQ, HEAD_PAD), lambda g, j: (j, 0))
    row = pl.BlockSpec((HB, nq, 1, BQ), lambda g, j: (g, 0, 0, 0))
    seq = pl.BlockSpec((S, HB * HEAD_PAD), lambda g, j: (0, g), pipeline_mode=pl.Buffered(1))
    kvb = pl.BlockSpec((BQ, 2 * HB * HEAD_PAD), lambda g, j: (j, g))
    return pl.pallas_call(
        body, name="attn_bwd", grid=(N_HEADS // HB, nq),
        in_specs=[kvb, seq, seq, row, row, tab, tab, tab],
        out_specs=[pl.BlockSpec((BQ, HB * HEAD_PAD), lambda g, j: (j, g)), kvb],
        out_shape=[jax.ShapeDtypeStruct((S, QW), BF16), jax.ShapeDtypeStruct((S, KVW), BF16)],
        scratch_shapes=[pltpu.VMEM((S, HB * HEAD_PAD), F32)],
        compiler_params=_cparams("parallel", "arbitrary"))(kvf, qf, dob, lse_row, dd_row, cos, sa, sb)


DC = 128
TR = 256


def _dwconv_fwd(u, w, b):
    S, Dm = u.shape
    tr = min(TR, S)

    def body(u_ref, w_ref, b_ref, o_ref, pad_ref):
        pad_ref[pl.ds(0, PAD_ROWS), :] = jnp.zeros((PAD_ROWS, DC), F32)
        pad_ref[pl.ds(PAD_ROWS, S), :] = u_ref[...]
        wv = w_ref[...]
        for r in range(S // tr):
            acc = jnp.broadcast_to(b_ref[...], (tr, DC))
            for j in range(CONV_WIDTH):
                acc = acc + wv[j:j + 1, :] * pad_ref[pl.ds(r * tr + PAD_ROWS - (CONV_WIDTH - 1) + j, tr), :]
            o_ref[pl.ds(r * tr, tr), :] = acc

    return pl.pallas_call(
        body, name="dwconv_fwd", grid=(Dm // DC,),
        in_specs=[pl.BlockSpec((S, DC), lambda c: (0, c)), pl.BlockSpec((CONV_WIDTH, DC), lambda c: (0, c)),
                  pl.BlockSpec((1, DC), lambda c: (0, c))],
        out_specs=pl.BlockSpec((S, DC), lambda c: (0, c)),
        out_shape=jax.ShapeDtypeStruct((S, Dm), F32),
        scratch_shapes=[pltpu.VMEM((S + PAD_ROWS, DC), F32)],
        compiler_params=_cparams("parallel"))(u, w, b)


def _dwconv_bwd(d, u, w):
    S, Dm = u.shape
    tr = min(TR, S)

    def body(d_ref, u_ref, w_ref, du_ref, dw_ref, padd_ref, padu_ref):
        padd_ref[pl.ds(0, S), :] = d_ref[...]
        padd_ref[pl.ds(S, PAD_ROWS), :] = jnp.zeros((PAD_ROWS, DC), F32)
        padu_ref[pl.ds(0, PAD_ROWS), :] = jnp.zeros((PAD_ROWS, DC), F32)
        padu_ref[pl.ds(PAD_ROWS, S), :] = u_ref[...]
        wv = w_ref[...]
        dws = [jnp.zeros((1, DC), F32) for _ in range(CONV_WIDTH)]
        for r in range(S // tr):
            acc = jnp.zeros((tr, DC), F32)
            for j in range(CONV_WIDTH):
                acc = acc + wv[j:j + 1, :] * padd_ref[pl.ds(r * tr + (CONV_WIDTH - 1) - j, tr), :]
            du_ref[pl.ds(r * tr, tr), :] = acc
            dt = d_ref[pl.ds(r * tr, tr), :]
            for j in range(CONV_WIDTH):
                ut = padu_ref[pl.ds(r * tr + PAD_ROWS - (CONV_WIDTH - 1) + j, tr), :]
                dws[j] = dws[j] + _colsum(dt * ut)
        for j in range(CONV_WIDTH):
            dw_ref[pl.ds(j, 1), :] = dws[j]
        dw_ref[pl.ds(CONV_WIDTH, 1), :] = jnp.zeros((1, DC), F32)

    blk = pl.BlockSpec((S, DC), lambda c: (0, c))
    return pl.pallas_call(
        body, name="dwconv_bwd", grid=(Dm // DC,),
        in_specs=[blk, blk, pl.BlockSpec((CONV_WIDTH, DC), lambda c: (0, c))],
        out_specs=[blk, pl.BlockSpec((PAD_ROWS, DC), lambda c: (0, c))],
        out_shape=[jax.ShapeDtypeStruct((S, Dm), F32), jax.ShapeDtypeStruct((PAD_ROWS, Dm), F32)],
        scratch_shapes=[pltpu.VMEM((S + PAD_ROWS, DC), F32), pltpu.VMEM((S + PAD_ROWS, DC), F32)],
        compiler_params=_cparams("parallel"))(d, u, w)


POOL_C = D_MODEL // len(POOL_WINDOWS)


def _pool_counts(r, tr, win):
    t = r * tr + lax.broadcasted_iota(jnp.int32, (tr, 1), 0)
    return jnp.minimum(t + 1, win).astype(F32)


def _pool_fwd(h):
    S, Dm = h.shape
    tr = min(TR, S)

    def body(h_ref, o_ref, pad_ref):
        pad_ref[pl.ds(0, PAD_ROWS), :] = jnp.zeros((PAD_ROWS, POOL_C), F32)
        pad_ref[pl.ds(PAD_ROWS, S), :] = h_ref[...]
        for g, win in enumerate(POOL_WINDOWS):
            @pl.when(pl.program_id(0) == g)
            def _():
                for r in range(S // tr):
                    acc = pad_ref[pl.ds(r * tr + PAD_ROWS, tr), :]
                    for j in range(1, win):
                        acc = acc + pad_ref[pl.ds(r * tr + PAD_ROWS - j, tr), :]
                    pooled = acc / _pool_counts(r, tr, win)
                    o_ref[pl.ds(r * tr, tr), :] = (pooled - h_ref[pl.ds(r * tr, tr), :]).astype(BF16)

    blk = pl.BlockSpec((S, POOL_C), lambda g: (0, g))
    return pl.pallas_call(
        body, name="pool_fwd", grid=(len(POOL_WINDOWS),), in_specs=[blk], out_specs=blk,
        out_shape=jax.ShapeDtypeStruct((S, Dm), BF16),
        scratch_shapes=[pltpu.VMEM((S + PAD_ROWS, POOL_C), F32)],
        compiler_params=_cparams("parallel"))(h)


def _pool_bwd(dp):
    S, Dm = dp.shape
    tr = min(TR, S)

    def body(d_ref, o_ref, pad_ref):
        pad_ref[pl.ds(S, PAD_ROWS), :] = jnp.zeros((PAD_ROWS, POOL_C), F32)
        for g, win in enumerate(POOL_WINDOWS):
            @pl.when(pl.program_id(0) == g)
            def _():
                for r in range(S // tr):
                    pad_ref[pl.ds(r * tr, tr), :] = d_ref[pl.ds(r * tr, tr), :] / _pool_counts(r, tr, win)
                for r in range(S // tr):
                    acc = pad_ref[pl.ds(r * tr, tr), :]
                    for j in range(1, win):
                        acc = acc + pad_ref[pl.ds(r * tr + j, tr), :]
                    o_ref[pl.ds(r * tr, tr), :] = acc - d_ref[pl.ds(r * tr, tr), :]

    blk = pl.BlockSpec((S, POOL_C), lambda g: (0, g))
    return pl.pallas_call(
        body, name="pool_bwd", grid=(len(POOL_WINDOWS),), in_specs=[blk], out_specs=blk,
        out_shape=jax.ShapeDtypeStruct((S, Dm), F32),
        scratch_shapes=[pltpu.VMEM((S + PAD_ROWS, POOL_C), F32)],
        compiler_params=_cparams("parallel"))(dp)


def _bias_spec(tn):
    return pl.BlockSpec((1, tn), lambda i, j, k: (0, j))


def _mla_weights(w_dq, w_dkv, w_uq, w_ukv, w_o):
    wd = jnp.concatenate([w_dq, w_dkv], axis=1)
    wq = jnp.pad(w_uq.reshape(Q_LORA, N_HEADS, QK_NOPE + QK_ROPE),
                 ((0, 0), (0, 0), (0, HEAD_PAD - QK_NOPE - QK_ROPE))).reshape(Q_LORA, QW)
    ukv = w_ukv.reshape(KV_LORA, N_HEADS, QK_NOPE + V_HEAD)
    wkv = jnp.zeros((DKV, N_HEADS, 2 * HEAD_PAD), BF16)
    wkv = wkv.at[:KV_LORA, :, :QK_NOPE].set(ukv[:, :, :QK_NOPE])
    wkv = wkv.at[:KV_LORA, :, HEAD_PAD:HEAD_PAD + V_HEAD].set(ukv[:, :, QK_NOPE:])
    eye = jnp.broadcast_to(jnp.eye(QK_ROPE, dtype=BF16)[:, None, :], (QK_ROPE, N_HEADS, QK_ROPE))
    wkv = wkv.at[KV_LORA:, :, QK_NOPE:QK_NOPE + QK_ROPE].set(eye).reshape(DKV, KVW)
    wo = jnp.pad(w_o.reshape(N_HEADS, V_HEAD, D_MODEL),
                 ((0, 0), (0, HEAD_PAD - V_HEAD), (0, 0))).reshape(QW, D_MODEL)
    return dict(wd=wd, wq=wq, wkv=wkv, wo=wo)


def _mla_weight_grads(g_wd, g_wq, g_wkv, g_wo):
    g_uq = g_wq.reshape(Q_LORA, N_HEADS, HEAD_PAD)[:, :, :QK_NOPE + QK_ROPE].reshape(Q_LORA, -1)
    t = g_wkv.reshape(DKV, N_HEADS, 2 * HEAD_PAD)[:KV_LORA]
    g_ukv = jnp.concatenate([t[:, :, :QK_NOPE], t[:, :, HEAD_PAD:HEAD_PAD + V_HEAD]], axis=2)
    g_o = g_wo.reshape(N_HEADS, HEAD_PAD, D_MODEL)[:, :V_HEAD].reshape(N_HEADS * V_HEAD, D_MODEL)
    return dict(mla_w_dq=g_wd[:, :Q_LORA], mla_w_uq=g_uq, mla_w_dkv=g_wd[:, Q_LORA:],
                mla_w_ukv=g_ukv.reshape(KV_LORA, -1), mla_w_o=g_o)


def _rope_epilogue(kv):
    def epi(acc, cos, sa, sb):
        parts = []
        for t in range(acc.shape[1] // HEAD_PAD):
            x = acc[:, t * HEAD_PAD:(t + 1) * HEAD_PAD]
            if kv:
                parts.append(x if t % 2 else _rope(x, cos, sa, sb))
            else:
                parts.append(_rope(x, cos, sa, sb) * SCALE_LOG2E)
        return (jnp.concatenate(parts, axis=1),)
    return epi


def _mla_fwd(tag, h, P, rope):
    S = h.shape[0]
    cos, sa, sb = rope
    tp = min(512, S)
    tabs = [pl.BlockSpec((tp, HEAD_PAD), lambda i, j, k: (i, 0))] * 3
    cqkv = _mm(f"mla_down{tag}", h, P['wd'], 'nn', S, DQKV, D_MODEL)

    def norms(x, qg, kg):
        xq, xk, xr = x[:, :Q_LORA], x[:, Q_LORA:Q_LORA + KV_LORA], x[:, Q_LORA + KV_LORA:]
        cq = xq * lax.rsqrt(_rowmean(xq * xq) + NORM_EPS) * qg
        ck = xk * lax.rsqrt(_rowmean(xk * xk) + NORM_EPS) * kg
        return (cq, jnp.concatenate([ck, xr], axis=1)), ()

    (cq, ckv), _ = _rowk(f"mla_norms{tag}", norms, [cqkv], [P['qg'], P['kg']], [(Q_LORA, BF16), (DKV, BF16)], [])
    qf = _mm(f"mla_q{tag}", cq, P['wq'], 'nn', S, QW, Q_LORA, tm=tp, tn=QW, extras=[cos, sa, sb],
             extra_specs=tabs, epi=_rope_epilogue(False), outs=[jax.ShapeDtypeStruct((S, QW), BF16)])
    kvf = _mm(f"mla_kv{tag}", ckv, P['wkv'], 'nn', S, KVW, DKV, tm=tp, tn=KVW, extras=[cos, sa, sb],
              extra_specs=tabs, epi=_rope_epilogue(True), outs=[jax.ShapeDtypeStruct((S, KVW), BF16)])
    o, ob, lse = _attn_fwd(qf, kvf)
    y = _mm(f"mla_o{tag}", ob, P['wo'], 'nn', S, D_MODEL, QW)
    return y, dict(cqkv=cqkv, cq=cq, ckv=ckv, qf=qf, kvf=kvf, o=o, ob=ob, lse=lse)


def _mla_bwd(tag, dy, h, sv, P, rope):
    S = h.shape[0]
    nq = S // BQ
    cos, sa, sb = rope
    g_wo = _mm(f"mla_o_wg{tag}", sv['ob'], dy, 'tn', QW, D_MODEL, S)
    dob = _mm(f"mla_o_dg{tag}", dy, P['wo'], 'nt', S, QW, D_MODEL, outs=[jax.ShapeDtypeStruct((S, QW), BF16)])
    dd = _attn_delta(dob, sv['o'])
    dq, dkv = _attn_bwd(sv['qf'], sv['kvf'], dob, sv['lse'], dd, cos, sa, sb)
    g_wq = _mm(f"mla_q_wg{tag}", sv['cq'], dq, 'tn', Q_LORA, QW, S)
    dcq = _mm(f"mla_q_dg{tag}", dq, P['wq'], 'nt', S, Q_LORA, QW)
    g_wkv = _mm(f"mla_kv_wg{tag}", sv['ckv'], dkv, 'tn', DKV, KVW, S)
    dckv = _mm(f"mla_kv_dg{tag}", dkv, P['wkv'], 'nt', S, DKV, KVW)

    def norms_bwd(dcq_v, dckv_v, x, qg, kg):
        xq, xk = x[:, :Q_LORA], x[:, Q_LORA:Q_LORA + KV_LORA]
        dxq, dqg = _gain_bwd(dcq_v, xq, qg)
        dxk, dkg = _gain_bwd(dckv_v[:, :KV_LORA], xk, kg)
        return (jnp.concatenate([dxq, dxk, dckv_v[:, KV_LORA:]], axis=1),), (dqg, dkg)

    (dcqkv,), (dqg, dkg) = _rowk(f"mla_norms_bwd{tag}", norms_bwd, [dcq, dckv, sv['cqkv']], [P['qg'], P['kg']],
                                 [(DQKV, BF16)], [Q_LORA, KV_LORA])
    g_wd = _mm(f"mla_down_wg{tag}", h, dcqkv, 'tn', D_MODEL, DQKV, S)
    dh = _mm(f"mla_down_dg{tag}", dcqkv, P['wd'], 'nt', S, D_MODEL, DQKV)
    grads = _mla_weight_grads(g_wd, g_wq, g_wkv, g_wo)
    grads.update(mla_q_norm_g=dqg.reshape(-1), mla_kv_norm_g=dkg.reshape(-1))
    return dh, grads


def _conv_fwd(h, P):
    S = h.shape[0]
    a = _mm("conv_pw1", h, P['w_pw1'], 'nn', S, 2 * D_MODEL, D_MODEL, extras=[P['b_pw1']],
            extra_specs=[_bias_spec(1024)], epi=lambda acc, b: (acc + b,))
    (u0,), _ = _rowk("conv_glu", lambda av: ((av[:, :D_MODEL] * _sigmoid(av[:, D_MODEL:]),), ()),
                     [a], [], [(D_MODEL, F32)], [])
    u1 = _dwconv_fwd(u0, P['w_dw'], P['b_dw'])

    def ln_silu(u, g, b):
        xc = u - _rowmean(u)
        z = xc * lax.rsqrt(_rowmean(xc * xc) + NORM_EPS) * g + b
        return (z * _sigmoid(z),), ()

    (u3,), _ = _rowk("conv_ln", ln_silu, [u1], [P['ln_g'], P['ln_b']], [(D_MODEL, BF16)], [])
    y = _mm("conv_pw2", u3, P['w_pw2'], 'nn', S, D_MODEL, D_MODEL, extras=[P['b_pw2']],
            extra_specs=[_bias_spec(1024)], epi=lambda acc, b: (acc + b,))
    return y, dict(a=a, u0=u0, u1=u1, u3=u3)


def _conv_bwd(dy, dy_colsum, h, sv, P):
    S = h.shape[0]
    g_pw2 = _mm("conv_pw2_wg", sv['u3'], dy, 'tn', D_MODEL, D_MODEL, S)
    du3 = _mm("conv_pw2_dg", dy, P['w_pw2'], 'nt', S, D_MODEL, D_MODEL)

    def ln_bwd(d3, u, g, b):
        xc = u - _rowmean(u)
        rstd = lax.rsqrt(_rowmean(xc * xc) + NORM_EPS)
        xh = xc * rstd
        z = xh * g + b
        sg = _sigmoid(z)
        dz = d3 * (sg * (1.0 + z * (1.0 - sg)))
        dxh = dz * g
        du = rstd * (dxh - _rowmean(dxh) - xh * _rowmean(dxh * xh))
        return (du,), (_colsum(dz * xh), _colsum(dz), _colsum(du))

    (du1,), (d_lng, d_lnb, d_bdw) = _rowk("conv_ln_bwd", ln_bwd, [du3, sv['u1']], [P['ln_g'], P['ln_b']],
                                          [(D_MODEL, F32)], [D_MODEL] * 3)
    du0, d_wdw = _dwconv_bwd(du1, sv['u0'], P['w_dw'])

    def glu_bwd(d0, av):
        a1, sg = av[:, :D_MODEL], _sigmoid(av[:, D_MODEL:])
        da = jnp.concatenate([d0 * sg, d0 * a1 * sg * (1.0 - sg)], axis=1)
        return (da,), (_colsum(da),)

    (da,), (d_bpw1,) = _rowk("conv_glu_bwd", glu_bwd, [du0, sv['a']], [], [(2 * D_MODEL, BF16)], [2 * D_MODEL])
    g_pw1 = _mm("conv_pw1_wg", h, da, 'tn', D_MODEL, 2 * D_MODEL, S)
    dh = _mm("conv_pw1_dg", da, P['w_pw1'], 'nt', S, D_MODEL, 2 * D_MODEL)
    grads = dict(conv_w_pw1=g_pw1, conv_b_pw1=d_bpw1.reshape(-1), conv_w_dw=d_wdw[:CONV_WIDTH],
                 conv_b_dw=d_bdw.reshape(-1), conv_ln_g=d_lng.reshape(-1), conv_ln_b=d_lnb.reshape(-1),
                 conv_w_pw2=g_pw2, conv_b_pw2=dy_colsum.reshape(-1))
    return dh, grads


def _pool_group_specs(tm):
    return (pl.BlockSpec((tm, POOL_C), lambda i, j, k: (i, j)),
            pl.BlockSpec((None, POOL_C, POOL_C), lambda i, j, k: (j, 0, 0)))


def _pool_mixer_fwd(h, P):
    S = h.shape[0]
    p = _pool_fwd(h)
    a_spec, b_spec = _pool_group_specs(min(1024, S))
    y, z = _mm("pool_mm", p, P['w'], 'nn', S, D_MODEL, POOL_C, tn=POOL_C, a_spec=a_spec, b_spec=b_spec,
               extras=[P['b'], P['scale']], extra_specs=[_bias_spec(POOL_C)] * 2,
               epi=lambda acc, b, s: ((acc + b) * s, acc + b),
               outs=[jax.ShapeDtypeStruct((S, D_MODEL), F32)] * 2)
    return y, dict(p=p, z=z)


def _pool_mixer_bwd(dy, sv, P):
    S = dy.shape[0]

    def scale_bwd(d, z, s):
        dz = d * s
        return (dz,), (_colsum(d * z), _colsum(dz))

    (dz,), (d_scale, d_b) = _rowk("pool_scale_bwd", scale_bwd, [dy, sv['z']], [P['scale']],
                                  [(D_MODEL, BF16)], [D_MODEL] * 2)
    a_spec, b_spec = _pool_group_specs(min(1024, S))
    dp = _mm("pool_mm_dg", dz, P['w'], 'nt', S, D_MODEL, POOL_C, tn=POOL_C, a_spec=a_spec, b_spec=b_spec)
    tk = min(512, S)
    grp = pl.BlockSpec((tk, POOL_C), lambda i, j, k: (k, j))
    g_w = _mm("pool_mm_wg", sv['p'], dz, 'tn', POOL_C, D_MODEL, S, tn=POOL_C, tk=tk, a_spec=grp, b_spec=grp,
              outs=[jax.ShapeDtypeStruct((len(POOL_WINDOWS), POOL_C, POOL_C), F32)],
              out_specs=[pl.BlockSpec((None, POOL_C, POOL_C), lambda i, j, k: (j, 0, 0))])
    dh = _pool_bwd(dp)
    return dh, dict(pool_w=g_w, pool_b=d_b.reshape(-1), pool_scale=d_scale.reshape(-1))


def _adamw(w, g, m, v):
    m2 = ADAM_B1 * m + (1.0 - ADAM_B1) * g
    v2 = ADAM_B2 * v + (1.0 - ADAM_B2) * (g * g)
    m_hat = m2 / (1.0 - ADAM_B1 ** ADAM_STEP)
    v_hat = v2 / (1.0 - ADAM_B2 ** ADAM_STEP)
    delta = -ADAM_LR * (m_hat / (jnp.sqrt(v_hat) + ADAM_EPS) + ADAM_WD * w)
    return delta, m2, v2


def _finish(name, w, land, own, m, v, layer=None, prev=None):
    local = land.shape[1:]
    C = local[-1]
    R = land[0].size // C
    tr = 64 if R % 64 == 0 else R

    def body(land_hbm, own_hbm, g_hbm, land_v, g_v, recv_v, io_sem, send_sem, recv_sem):
        me = _slot('xy', lax.axis_index("x"), lax.axis_index("y"), 0)
        for s_ in range(4):
            for src, mine in ((own_hbm, True), (land_hbm, False)):
                @pl.when((me == s_) == mine)
                def _():
                    load = pltpu.make_async_copy(src.at[s_], land_v.at[s_], io_sem)
                    load.start()
                    load.wait()

        def rows_of(i):
            return pl.ds(pl.multiple_of(i * tr, tr), tr)

        def sum_chunk(i, carry):
            rows = rows_of(i)
            g_v[rows, :] = ((land_v[0, rows, :].astype(F32) + land_v[1, rows, :].astype(F32))
                            + land_v[2, rows, :].astype(F32)) + land_v[3, rows, :].astype(F32)
            return carry

        lax.fori_loop(0, R // tr, sum_chunk, 0)
        swap = pltpu.make_async_remote_copy(
            src_ref=g_v, dst_ref=recv_v, send_sem=send_sem, recv_sem=recv_sem,
            device_id=(lax.axis_index("x"), lax.axis_index("y"), 1 - lax.axis_index("c")),
            device_id_type=pl.DeviceIdType.MESH)
        swap.start()
        swap.wait()

        def add_chunk(i, carry):
            rows = rows_of(i)
            recv_v[rows, :] = g_v[rows, :] + recv_v[rows, :]
            return carry

        lax.fori_loop(0, R // tr, add_chunk, 0)
        store = pltpu.make_async_copy(recv_v, g_hbm, io_sem)
        store.start()
        store.wait()

    any_spec = pl.BlockSpec(memory_space=pl.ANY)
    g = pl.pallas_call(
        body, name=name, in_specs=[any_spec] * 2, out_specs=any_spec, out_shape=jax.ShapeDtypeStruct((R, C), F32),
        scratch_shapes=[pltpu.VMEM((4, R, C), BF16), pltpu.VMEM((R, C), F32), pltpu.VMEM((R, C), F32),
                        pltpu.SemaphoreType.DMA, pltpu.SemaphoreType.DMA, pltpu.SemaphoreType.DMA],
        compiler_params=pltpu.CompilerParams(has_side_effects=True, vmem_limit_bytes=VMEM_LIMIT))(
            land.reshape(4, R, C), own.reshape(4, R, C))

    lead = () if layer is None else (w.shape[0],)
    as2d = lambda a: a.reshape(lead + (R, C))
    tu = _row_tile(R, C, budget=1 << 19)
    tile = pl.BlockSpec((tu, C), lambda i: (i, 0))
    slab = tile if layer is None else pl.BlockSpec((None, tu, C), lambda i: (layer, i, 0))
    n_prev = 0 if prev is None else 4

    def update(w_ref, g_ref, m_ref, v_ref, *rest):
        outs = rest[n_prev:]
        gv = g_ref[...]
        d, nm, nv = _adamw(w_ref[...], gv, m_ref[...], v_ref[...])
        for r, val in zip(outs, (gv, d, nm, nv)):
            r[...] = val

    res = pl.pallas_call(
        update, name=name + "_adamw", grid=(R // tu,),
        in_specs=[slab, tile, slab, slab] + [any_spec] * n_prev, out_specs=[slab] * 4,
        out_shape=[jax.ShapeDtypeStruct(lead + (R, C), F32)] * 4,
        input_output_aliases={4 + k: k for k in range(n_prev)},
        compiler_params=_cparams("arbitrary"))(
            as2d(w), g, as2d(m), as2d(v), *([] if prev is None else [as2d(p) for p in prev]))
    return [r.reshape(w.shape) for r in res]


def _row(v):
    return v.reshape(1, -1)


def kernel(x, c, positions, ada_w, ada_b, norm_g, mla_w_dq, mla_q_norm_g, mla_w_uq, mla_w_dkv, mla_kv_norm_g, mla_w_ukv, mla_w_o, conv_w_pw1, conv_b_pw1, conv_w_dw, conv_b_dw, conv_ln_g, conv_ln_b, conv_w_pw2, conv_b_pw2, pool_w, pool_b, pool_scale, ffn_w1, ffn_w2, loss_target, m_ada_w, m_ada_b, m_norm_g, m_mla_w_dq, m_mla_q_norm_g, m_mla_w_uq, m_mla_w_dkv, m_mla_kv_norm_g, m_mla_w_ukv, m_mla_w_o, m_conv_w_pw1, m_conv_b_pw1, m_conv_w_dw, m_conv_b_dw, m_conv_ln_g, m_conv_ln_b, m_conv_w_pw2, m_conv_b_pw2, m_pool_w, m_pool_b, m_pool_scale, m_ffn_w1, m_ffn_w2, v_ada_w, v_ada_b, v_norm_g, v_mla_w_dq, v_mla_q_norm_g, v_mla_w_uq, v_mla_w_dkv, v_mla_kv_norm_g, v_mla_w_ukv, v_mla_w_o, v_conv_w_pw1, v_conv_b_pw1, v_conv_w_dw, v_conv_b_dw, v_conv_ln_g, v_conv_ln_b, v_conv_w_pw2, v_conv_b_pw2, v_pool_w, v_pool_b, v_pool_scale, v_ffn_w1, v_ffn_w2):
    args = dict(locals())
    W = {n: args[n] for n in WEIGHTS}
    MOM = {n: args['m_' + n] for n in WEIGHTS}
    VAR = {n: args['v_' + n] for n in WEIGHTS}
    S = x.shape[1]
    xs = x.reshape(S, D_MODEL)
    tgt = loss_target.reshape(S, D_MODEL)
    mx, my, mc = lax.axis_index("x"), lax.axis_index("y"), lax.axis_index("c")
    chip = 2 * mx + my
    n_sh = ada_w.shape[2]

    def sent_of(key):
        n, l = key
        arr = W[n] if l is None else W[n][l]
        return arr.astype(BF16) if n in BIG or n in ('ffn_w1', 'ffn_w2') else arr

    keys0 = [(n, 0) for n in MLA_MATS] + [(n, None) for n in ('norm_g', 'mla_q_norm_g', 'mla_kv_norm_g',
                                                               'conv_w_dw', 'pool_b', 'pool_scale')]
    c8 = _exchange("gather_c", [c.reshape(8, D_MODEL // 8)], 'xyc')[0].reshape(8, D_MODEL)
    sent0 = [sent_of(k) for k in keys0]
    sent0[-1] = sent0[-1] + jnp.minimum(jnp.abs(c8[0, 0]), 0.0)
    fly0 = _split_start("gather_w0_start", sent0)
    c8 = c8 + fly0['token'][0, 0]
    c8 = jnp.pad(c8, ((0, ADA_ROWS - 8), (0, 0)))
    silu = lambda v: v * _sigmoid(v)
    mod_sh = []
    for l in range(DEPTH):
        b_l = lax.dynamic_slice(ada_b[l], (chip * n_sh,), (n_sh,)).reshape(1, n_sh)
        mod_sh.append(_mm(f"ada_fwd{l}", c8, ada_w, 'nn', ADA_ROWS, n_sh, D_MODEL, tn=n_sh // 2, tk=512, pro_a=silu,
                          b_spec=pl.BlockSpec((None, 512, n_sh // 2), lambda i, j, k, l=l: (l, k, j)),
                          extras=[b_l], extra_specs=[_bias_spec(n_sh // 2)], epi=lambda acc, b: (acc + b,))[:8])
    mod_sh = jnp.stack(mod_sh, axis=1).reshape(8, DEPTH * n_sh // 128, 128)
    mod = _exchange("scatter_mod", [mod_sh], 'xy', src_by='xyc')[0]
    mod = mod.reshape(4, DEPTH, n_sh).transpose(1, 0, 2).reshape(DEPTH, 6, 1, D_MODEL)

    keys1 = [('ffn_w1', 0), ('ffn_w2', 0), ('conv_w_pw1', None), ('conv_w_pw2', None), ('pool_w', None)]
    keys2 = [(n, l) for l in range(1, DEPTH) for n in ('ffn_w1', 'ffn_w2')] + [(n, 1) for n in MLA_MATS]
    fly1 = _split_start("gather_w1_start", [sent_of(k) for k in keys1])
    fly2 = _split_start("gather_w2_start", [sent_of(k) for k in keys2])
    mod = mod + (fly1['token'][0, 0] + fly2['token'][0, 0])
    G = dict(zip(keys0, _split_wait("gather_w0_wait", fly0, mod)))

    def whole(key):
        n, l = key
        return _unshard(G[key], SHARD_AXIS[n] - (0 if l is None else 1))

    def mla_params(j):
        P = _mla_weights(*[whole((n, j)) for n in ('mla_w_dq', 'mla_w_dkv', 'mla_w_uq', 'mla_w_ukv', 'mla_w_o')])
        P.update(qg=_row(whole(('mla_q_norm_g', None))[j]), kg=_row(whole(('mla_kv_norm_g', None))[j]))
        return P

    gains = whole(('norm_g', None))
    mla_p = {0: mla_params(0)}
    conv_p = pool_p = None
    rope = _rope_tables(positions.reshape(S, 1).astype(F32))

    by_j = pl.BlockSpec((None, 1024, 1024), lambda i, j, k: (j, 0, 0))
    by_k = pl.BlockSpec((None, 1024, 1024), lambda i, j, k: (k, 0, 0))
    sq_relu = lambda v: jnp.square(jnp.maximum(v, 0.0))

    def md(i, k):
        return mod[i, k]

    (h,), _ = _rowk("pre0", lambda xv, g, sc, sh: ((_pre_fwd(xv, g, sc, sh),), ()),
                    [xs], [_row(gains[0, 0]), md(0, 1), md(0, 0)], [(D_MODEL, BF16)], [])
    saved = []
    xin = xs
    loss_acc = dxf = None
    for i in range(DEPTH):
        kind, j = i % 3, i // 3
        if kind == 0:
            if j not in mla_p:
                mla_p[j] = mla_params(j)
            y, sv = _mla_fwd(j, h, mla_p[j], rope)
        elif kind == 1:
            y, sv = _conv_fwd(h, conv_p)
        else:
            y, sv = _pool_mixer_fwd(h, pool_p)

        def mid(xv, yv, gt, g1, g2, sc, sh):
            x1 = _post_fwd(xv, yv, gt, g1)
            return (x1, _pre_fwd(x1, g2, sc, sh)), ()

        (x1, h2), _ = _rowk(f"mid{i}", mid, [xin, y], [md(i, 2), _row(gains[i, 1]), _row(gains[i, 2]), md(i, 4), md(i, 3)],
                            [(D_MODEL, F32), (D_MODEL, BF16)], [])
        if i == 0:
            G.update(zip(keys1, _split_wait("gather_w1_wait", fly1, h2)))
            conv_p = dict(w_pw1=whole(('conv_w_pw1', None))[0], b_pw1=_row(conv_b_pw1[0]),
                          w_dw=whole(('conv_w_dw', None))[0], b_dw=_row(conv_b_dw[0]), ln_g=_row(conv_ln_g[0]),
                          ln_b=_row(conv_ln_b[0]), w_pw2=whole(('conv_w_pw2', None))[0], b_pw2=_row(conv_b_pw2[0]))
            pool_p = dict(w=whole(('pool_w', None))[0], b=_row(whole(('pool_b', None))[0]),
                          scale=_row(whole(('pool_scale', None))[0]))
        if i == 1:
            G.update(zip(keys2, _split_wait("gather_w2_wait", fly2, h2)))
        a = _mm(f"ffn1_{i}", h2, G[('ffn_w1', i)], 'nn', S, D_FF, D_MODEL, tm=2048, b_spec=by_j,
                outs=[jax.ShapeDtypeStruct((S, D_FF), BF16)])
        y2 = _mm(f"ffn2_{i}", a, G[('ffn_w2', i)], 'nn', S, D_MODEL, D_FF, pro_a=sq_relu, b_spec=by_k)
        saved.append(dict(x0=xin, h=h, y=y, x1=x1, h2=h2, a=a, y2=y2, mix=sv))
        if i + 1 < DEPTH:
            def nxt(xv, yv, gt, g3, g0, sc, sh):
                x2 = _post_fwd(xv, yv, gt, g3)
                return (x2, _pre_fwd(x2, g0, sc, sh)), ()

            hdt = F32 if (i + 1) % 3 == 2 else BF16
            (xin, h), _ = _rowk(f"next{i}", nxt, [x1, y2],
                                [md(i, 5), _row(gains[i, 3]), _row(gains[i + 1, 0]), md(i + 1, 1), md(i + 1, 0)],
                                [(D_MODEL, F32), (D_MODEL, hdt)], [])
        else:
            def head(xv, yv, tv, gt, g3):
                err = _post_fwd(xv, yv, gt, g3) - tv
                per_row = jnp.sum(err * err, axis=1, keepdims=True) * (0.5 / D_MODEL)
                return (err * (1.0 / D_MODEL),), (jnp.broadcast_to(jnp.sum(per_row, axis=0, keepdims=True), (1, 128)),)

            (dxf,), (loss_acc,) = _rowk("loss_head", head, [x1, y2, tgt], [md(i, 5), _row(gains[i, 3])],
                                        [(D_MODEL, F32)], [128])

    small = {}
    big = {}
    landed = {}

    def keep(gm, layer):
        for n, g in gm.items():
            if n in BIG:
                g = g[None] if layer is None else g
                big[(n, layer)] = _to_shards(g, SHARD_AXIS[n] - (0 if layer is None else 1)).astype(BF16)
            else:
                small.setdefault(n, {})[layer or 0] = g

    d_mod = [None] * DEPTH
    d_gain = [None] * DEPTH
    dx = dxf
    for i in reversed(range(DEPTH)):
        kind, j = i % 3, i // 3
        sv = saved[i]
        def post2_bwd(d, yv, gt, g):
            dyv, d_gt, d_g = _post_bwd(d, yv, gt, g)
            return (dyv,), (d_gt, d_g)

        (dy2,), (d_gtf, d_g3) = _rowk(f"post2_bwd{i}", post2_bwd, [dx, sv['y2']], [md(i, 5), _row(gains[i, 3])],
                                      [(D_MODEL, BF16)], [D_MODEL] * 2)
        da = _mm(f"ffn2_dg{i}", dy2, G[('ffn_w2', i)], 'nt', S, D_FF, D_MODEL, tm=2048, b_spec=by_j, extras=[sv['a']],
                 extra_specs=[pl.BlockSpec((min(2048, S), 1024), lambda i_, j_, k_: (i_, j_))],
                 epi=lambda acc, av: (acc * (2.0 * jnp.maximum(av, 0.0)),),
                 outs=[jax.ShapeDtypeStruct((S, D_FF), BF16)])
        big[('ffn_w2', i)] = _mm(f"ffn2_wg{i}", sv['a'], dy2, 'tn', D_FF, D_MODEL, S, tk=2048, pro_a=sq_relu,
                        outs=[jax.ShapeDtypeStruct((4, 1024, D_MODEL), BF16)],
                        out_specs=[pl.BlockSpec((None, 1024, 1024), lambda i_, j_, k_: (i_, 0, j_))])
        big[('ffn_w1', i)] = _mm(f"ffn1_wg{i}", sv['h2'], da, 'tn', D_MODEL, D_FF, S, tk=2048,
                        outs=[jax.ShapeDtypeStruct((4, D_MODEL, 1024), BF16)],
                        out_specs=[pl.BlockSpec((None, 1024, 1024), lambda i_, j_, k_: (j_, i_, 0))])
        dh2 = _mm(f"ffn1_dg{i}", da, G[('ffn_w1', i)], 'nt', S, D_MODEL, D_FF, tm=2048, b_spec=by_k)
        if i == DEPTH - 1:
            keys_a = [('ffn_w1', i), ('ffn_w2', i)]
            fly_a = _split_start("scatter_ga_start", [big[k] for k in keys_a], src_by='xy')
            mod = mod + fly_a['token'][0, 0]
        if i == 0:
            keys_b = [k for k in big if k not in keys_a]
            fly_b = _split_start("scatter_gb_start", [big[k] for k in keys_b], src_by='xy')
            mod = mod + fly_b['token'][0, 0]

        def mid_bwd(d2, dh2v, x1v, yv, g2, scf, gtm, g1):
            dpre, d_sh, d_sc, d_g2 = _pre_bwd(dh2v, x1v, g2, scf)
            d1 = d2 + dpre
            dyv, d_gt, d_g1 = _post_bwd(d1, yv, gtm, g1)
            return (d1, dyv), (d_sh, d_sc, d_g2, d_gt, d_g1, _colsum(dyv))

        ydt = F32 if kind == 2 else BF16
        (dx1, dy), (d_shf, d_scf, d_g2, d_gtm, d_g1, dy_cs) = _rowk(
            f"mid_bwd{i}", mid_bwd, [dx, dh2, sv['x1'], sv['y']],
            [_row(gains[i, 2]), md(i, 4), md(i, 2), _row(gains[i, 1])],
            [(D_MODEL, F32), (D_MODEL, ydt)], [D_MODEL] * 6)
        if kind == 0:
            dh, gm = _mla_bwd(j, dy, sv['h'], sv['mix'], mla_p[j], rope)
            keep(gm, j)
        elif kind == 1:
            dh, gm = _conv_bwd(dy, dy_cs, sv['h'], sv['mix'], conv_p)
            keep(gm, None)
        else:
            dh, gm = _pool_mixer_bwd(dy, sv['mix'], pool_p)
            keep(gm, None)

        def pre_bwd(d1, dhv, x0v, g0, scm):
            dpre, d_sh, d_sc, d_g0 = _pre_bwd(dhv, x0v, g0, scm)
            return (d1 + dpre,), (d_sh, d_sc, d_g0)

        (dx,), (d_shm, d_scm, d_g0) = _rowk(f"pre_bwd{i}", pre_bwd, [dx1, dh, sv['x0']],
                                            [_row(gains[i, 0]), md(i, 1)], [(D_MODEL, F32)], [D_MODEL] * 3)
        d_mod[i] = jnp.concatenate([d_shm, d_scm, d_gtm, d_shf, d_scf, d_gtf], axis=1).reshape(-1)
        d_gain[i] = jnp.concatenate([d_g0, d_g1, d_g2, d_g3], axis=0)
        if i == DEPTH - 1:
            landed.update(zip(keys_a, _split_wait("scatter_ga_wait", fly_a, dx, fill_own=False)))
    landed.update(zip(keys_b, _split_wait("scatter_gb_wait", fly_b, dx, fill_own=False)))
    keys_c = [(n, 0) for n in MLA_MATS]
    fly_c = _split_start("scatter_gc_start", [big[k] for k in keys_c], src_by='xy')
    grad_x = dx.reshape(x.shape)
    grads = {n: jnp.stack([g[l] for l in sorted(g)]) for n, g in small.items()}
    grads['norm_g'] = jnp.stack(d_gain)
    grads['ada_b'] = jnp.stack(d_mod)

    pack = jnp.concatenate([grads[n].reshape(-1) for n in SMALL] + [loss_acc[0, :1]])
    n_pack = pack.shape[0]
    rows = -(-n_pack // 1024) * 8
    pack = jnp.pad(pack, (0, rows * 128 - n_pack)).reshape(rows, 128)
    fly_s = _split_start("gather_small_start", [pack], group='xyc')

    out_g, out_d, out_m, out_v = {}, {}, {}, {}
    chains = {}
    for n in BIG + ['ffn_w1', 'ffn_w2']:
        if n in MLA_MATS:
            chains[n] = _finish(f"finish_{n}1", W[n], landed[(n, 1)], big[(n, 1)], MOM[n], VAR[n], layer=1)
        elif n in BIG:
            out_g[n], out_d[n], out_m[n], out_v[n] = _finish(f"finish_{n}", W[n], landed[(n, None)], big[(n, None)],
                                                             MOM[n], VAR[n])
        else:
            res = None
            for l in range(DEPTH):
                res = _finish(f"finish_{n}{l}", W[n], landed[(n, l)], big[(n, l)], MOM[n], VAR[n], layer=l, prev=res)
            out_g[n], out_d[n], out_m[n], out_v[n] = res
    landed.update(zip(keys_c, _split_wait("scatter_gc_wait", fly_c, out_g['ffn_w2'], fill_own=False)))
    for n in MLA_MATS:
        out_g[n], out_d[n], out_m[n], out_v[n] = _finish(f"finish_{n}0", W[n], landed[(n, 0)], big[(n, 0)], MOM[n],
                                                         VAR[n], layer=0, prev=chains[n])
    pack8 = _split_wait("gather_small_wait", fly_s, out_g['mla_w_o'])[0]
    (tot,) = _ew("sum_small", lambda *v: (functools.reduce(lambda p, q: p + q, v),), [(pack8, s) for s in range(8)],
                 [F32], (rows, 128))
    tot = tot.reshape(-1)
    loss = tot[n_pack - 1]
    d_mod_all = pack8.reshape(8, -1)[:, :DEPTH * 6 * D_MODEL].reshape(8, DEPTH, 6 * D_MODEL)
    final = {}
    off = 0
    for n in SMALL:
        ax = SHARD_AXIS[n]
        shape = tuple(d * 4 if k == ax else d for k, d in enumerate(W[n].shape))
        size = grads[n].size
        g = tot[off:off + size].reshape(shape)
        off += size
        if ax is not None:
            g = lax.dynamic_index_in_dim(_to_shards(g, ax), chip, 0, keepdims=False)
        final[n] = g

    g_ada = []
    for l in range(DEPTH):
        dm_l = jnp.pad(lax.dynamic_slice(d_mod_all[:, l], (0, chip * n_sh), (8, n_sh)), ((0, ADA_ROWS - 8), (0, 0)))
        g_ada.append(_mm(f"ada_wg{l}", c8, dm_l, 'tn', D_MODEL, n_sh, ADA_ROWS, tn=n_sh // 2, pro_a=silu))
    final['ada_w'] = jnp.stack(g_ada)

    for n in WEIGHTS:
        if n in out_g:
            continue
        shape = W[n].shape
        out_g[n] = final[n].reshape(shape)
        out_d[n], out_m[n], out_v[n] = _ew(f"adamw_{n}", lambda w, g, m, v: _adamw(w, g, m, v),
                                           [W[n], out_g[n], MOM[n], VAR[n]], [F32] * 3, shape)
    return (loss, grad_x, *[out_g[n] for n in WEIGHTS], *[out_d[n] for n in WEIGHTS],
            *[out_m[n] for n in WEIGHTS], *[out_v[n] for n in WEIGHTS])
```

```python
import functools
import math

import jax
import jax.numpy as jnp
from jax import lax
from jax.experimental import pallas as pl
from jax.experimental.pallas import tpu as pltpu

F32 = jnp.float32
BF16 = jnp.bfloat16

D_MODEL = 1024
DEPTH = 4
N_HEADS = 16
QK_NOPE = 64
QK_ROPE = 32
V_HEAD = 64
Q_LORA = 384
KV_LORA = 256
HEAD_PAD = 128
QW = N_HEADS * HEAD_PAD
KVW = 2 * QW
DKV = KV_LORA + QK_ROPE
DQKV = Q_LORA + DKV
D_FF = 4096
CONV_WIDTH = 31
POOL_WINDOWS = (2, 4, 8, 16)
CHUNK_SHIFT = 6
ROPE_THETA = 10000.0
NORM_EPS = 1e-6
NEG_INF = -1e30
ATT_SCALE = 1.0 / math.sqrt(QK_NOPE + QK_ROPE)
BQ = 256
HB = 8
HF = 8
LOG2E = 1.4426950408889634
SCALE_LOG2E = ATT_SCALE * LOG2E
PAD_ROWS = 32
ADA_ROWS = 128
VMEM_LIMIT = 56 * 1024 * 1024

ADAM_LR = 0.001
ADAM_B1 = 0.9
ADAM_B2 = 0.999
ADAM_EPS = 1e-08
ADAM_WD = 0.01
ADAM_STEP = 10

WEIGHTS = ['ada_w', 'ada_b', 'norm_g', 'mla_w_dq', 'mla_q_norm_g', 'mla_w_uq', 'mla_w_dkv', 'mla_kv_norm_g',
           'mla_w_ukv', 'mla_w_o', 'conv_w_pw1', 'conv_b_pw1', 'conv_w_dw', 'conv_b_dw', 'conv_ln_g', 'conv_ln_b',
           'conv_w_pw2', 'conv_b_pw2', 'pool_w', 'pool_b', 'pool_scale', 'ffn_w1', 'ffn_w2']
SHARD_AXIS = {'ada_w': 2, 'ada_b': None, 'norm_g': 2, 'mla_w_dq': 1, 'mla_q_norm_g': 1, 'mla_w_uq': 2,
              'mla_w_dkv': 1, 'mla_kv_norm_g': 1, 'mla_w_ukv': 2, 'mla_w_o': 1, 'conv_w_pw1': 2,
              'conv_b_pw1': None, 'conv_w_dw': 2, 'conv_b_dw': None, 'conv_ln_g': None, 'conv_ln_b': None,
              'conv_w_pw2': 1, 'conv_b_pw2': None, 'pool_w': 2, 'pool_b': 2, 'pool_scale': 1,
              'ffn_w1': 2, 'ffn_w2': 1}
MLA_MATS = ['mla_w_dq', 'mla_w_uq', 'mla_w_dkv', 'mla_w_ukv', 'mla_w_o']
BIG = MLA_MATS + ['conv_w_pw1', 'conv_w_pw2', 'pool_w']
SMALL = ['ada_b', 'norm_g', 'mla_q_norm_g', 'mla_kv_norm_g', 'conv_b_pw1', 'conv_w_dw', 'conv_b_dw',
         'conv_ln_g', 'conv_ln_b', 'conv_b_pw2', 'pool_b', 'pool_scale']


def _cparams(*sem):
    return pltpu.CompilerParams(dimension_semantics=sem, vmem_limit_bytes=VMEM_LIMIT)


def _colsum(v):
    return jnp.sum(v, axis=0, keepdims=True)


def _rowmean(v):
    return jnp.mean(v, axis=-1, keepdims=True)


def _sigmoid(v):
    return 1.0 / (1.0 + jnp.exp(-v))


def _rowk(name, fn, rows, bcast, out_row, out_acc, tm=512):
    S = rows[0].shape[0]
    tm = min(tm, S)
    while S % tm:
        tm //= 2
    nin, no, na = len(rows) + len(bcast), len(out_row), len(out_acc)

    def body(*refs):
        vals = [r[...] for r in refs[:nin]]
        outs = refs[nin:nin + no]
        accs = refs[nin + no:]
        ro, ao = fn(*vals)
        for r, v in zip(outs, ro):
            r[...] = v.astype(r.dtype)
        if na:
            @pl.when(pl.program_id(0) == 0)
            def _():
                for r in accs:
                    r[...] = jnp.zeros(r.shape, r.dtype)
            for r, v in zip(accs, ao):
                r[...] += v

    in_specs = [pl.BlockSpec((tm, a.shape[1]), lambda i: (i, 0)) for a in rows]
    in_specs += [pl.BlockSpec(b.shape, lambda i, n=b.ndim: (0,) * n) for b in bcast]
    out_shape = [jax.ShapeDtypeStruct((S, w), dt) for w, dt in out_row]
    out_shape += [jax.ShapeDtypeStruct((1, w), F32) for w in out_acc]
    out_specs = [pl.BlockSpec((tm, w), lambda i: (i, 0)) for w, _ in out_row]
    out_specs += [pl.BlockSpec((1, w), lambda i: (0, 0)) for w in out_acc]
    res = pl.pallas_call(body, name=name, grid=(S // tm,), in_specs=in_specs, out_specs=out_specs,
                         out_shape=out_shape, compiler_params=_cparams("arbitrary"))(*rows, *bcast)
    return list(res[:no]), list(res[no:])


_DIMS = {'nn': ((1,), (0,)), 'nt': ((1,), (1,)), 'tn': ((0,), (0,))}


def _mm(name, a, b, mode, M, N, K, *, tm=1024, tn=1024, tk=1024, a_spec=None, b_spec=None, pro_a=None,
        extras=(), extra_specs=(), epi=None, outs=None, out_specs=None):
    tm, tn, tk = (t if d % t == 0 else d for t, d in ((min(tm, M), M), (min(tn, N), N), (min(tk, K), K)))
    nk = K // tk
    if a_spec is None:
        a_spec = (pl.BlockSpec((tk, tm), lambda i, j, k: (k, i)) if mode == 'tn'
                  else pl.BlockSpec((tm, tk), lambda i, j, k: (i, k)))
    if b_spec is None:
        b_spec = (pl.BlockSpec((tn, tk), lambda i, j, k: (j, k)) if mode == 'nt'
                  else pl.BlockSpec((tk, tn), lambda i, j, k: (k, j)))
    if outs is None:
        outs = [jax.ShapeDtypeStruct((M, N), F32)]
    if out_specs is None:
        out_specs = [pl.BlockSpec((tm, tn), lambda i, j, k: (i, j)) for _ in outs]
    ne, no = len(extras), len(outs)
    dims = (_DIMS[mode], ((), ()))

    def body(a_ref, b_ref, *rest):
        ex, out_refs = rest[:ne], rest[ne:ne + no]
        av = a_ref[...]
        if pro_a is not None:
            av = pro_a(av)
        part = lax.dot_general(av.astype(BF16), b_ref[...].astype(BF16), dims, preferred_element_type=F32)

        def finish(acc):
            vals = (acc,) if epi is None else epi(acc, *[e[...] for e in ex])
            for r, v in zip(out_refs, vals):
                r[...] = v.astype(r.dtype)

        if nk == 1:
            finish(part)
            return
        acc_ref = rest[ne + no]
        k = pl.program_id(2)

        @pl.when(k == 0)
        def _():
            acc_ref[...] = part

        @pl.when(k > 0)
        def _():
            acc_ref[...] += part

        @pl.when(k == nk - 1)
        def _():
            finish(acc_ref[...])

    res = pl.pallas_call(
        body, name=name, grid=(M // tm, N // tn, nk),
        in_specs=[a_spec, b_spec, *extra_specs], out_specs=list(out_specs), out_shape=list(outs),
        scratch_shapes=[pltpu.VMEM((tm, tn), F32)] if nk > 1 else [],
        compiler_params=_cparams("parallel", "parallel", "arbitrary"))(a, b, *extras)
    return res[0] if no == 1 else list(res)


def _row_tile(R, C, itemsize=4, budget=1 << 20):
    if R * C * itemsize <= budget or R % 8:
        return R
    t = 8
    while R % (t * 2) == 0 and t * 2 * C * itemsize <= budget:
        t *= 2
    return t


def _ew(name, fn, ins, out_dtypes, shape):
    C = shape[-1]
    R = 1
    for s in shape[:-1]:
        R *= s
    tr = _row_tile(R, C)
    ops, specs = [], []
    for it in ins:
        if isinstance(it, tuple):
            arr, idx = it
            ops.append(arr.reshape(arr.shape[0], R, C))
            specs.append(pl.BlockSpec((None, tr, C), lambda i, n=idx: (n, i, 0)))
        else:
            ops.append(it.reshape(R, C))
            specs.append(pl.BlockSpec((tr, C), lambda i: (i, 0)))
    nin = len(ops)

    def body(*refs):
        vals = fn(*[r[...] for r in refs[:nin]])
        for r, v in zip(refs[nin:], vals):
            r[...] = v.astype(r.dtype)

    res = pl.pallas_call(
        body, name=name, grid=(R // tr,), in_specs=specs,
        out_specs=[pl.BlockSpec((tr, C), lambda i: (i, 0)) for _ in out_dtypes],
        out_shape=[jax.ShapeDtypeStruct((R, C), dt) for dt in out_dtypes],
        compiler_params=_cparams("parallel"))(*ops)
    return [r.reshape(shape) for r in res]


_FLIPS = {'xyc': [(fx, fy, fc) for fx in (0, 1) for fy in (0, 1) for fc in (0, 1)][1:],
          'xy': [(1, 0, 0), (0, 1, 0), (1, 1, 0)],
          'c': [(0, 0, 1)]}
_NSLOT = {'xyc': 8, 'xy': 4, 'c': 2}


def _slot(kind, cx, cy, cc):
    return {'xyc': 4 * cx + 2 * cy + cc, 'xy': 2 * cx + cy, 'c': cc}[kind]


def _put_own(land, arr, group, src_by):
    coords = (lax.axis_index("x"), lax.axis_index("y"), lax.axis_index("c"))
    pay = arr if src_by is None else lax.dynamic_index_in_dim(arr, _slot(src_by, *coords), 0, keepdims=False)
    return lax.dynamic_update_index_in_dim(land, pay, _slot(group, *coords), 0)


def _exchange(name, arrays, group, src_by=None):
    flips, nsl, n = _FLIPS[group], _NSLOT[group], len(arrays)
    nf = len(flips)

    def body(*refs):
        ins, outs = refs[:n], refs[n:2 * n]
        send_sems, recv_sems = refs[2 * n:]
        mx, my, mc = lax.axis_index("x"), lax.axis_index("y"), lax.axis_index("c")
        me = _slot(group, mx, my, mc)

        def payload(a, cx, cy, cc):
            return ins[a] if src_by is None else ins[a].at[_slot(src_by, cx, cy, cc)]

        sends, recvs = [], []
        for a in range(n):
            for f, (fx, fy, fc) in enumerate(flips):
                px = 1 - mx if fx else mx
                py = 1 - my if fy else my
                pc = 1 - mc if fc else mc
                src = payload(a, px, py, pc)
                sends.append(pltpu.make_async_remote_copy(
                    src_ref=src, dst_ref=outs[a].at[me], send_sem=send_sems.at[a, f],
                    recv_sem=recv_sems.at[a, f], device_id=(px, py, pc),
                    device_id_type=pl.DeviceIdType.MESH))
                recvs.append(pltpu.make_async_remote_copy(
                    src_ref=src, dst_ref=outs[a].at[_slot(group, px, py, pc)], send_sem=send_sems.at[a, f],
                    recv_sem=recv_sems.at[a, f], device_id=(px, py, pc),
                    device_id_type=pl.DeviceIdType.MESH))
        for cp in sends:
            cp.start()
        for cp in recvs:
            cp.wait_recv()
        for cp in sends:
            cp.wait_send()

    out_shape = [jax.ShapeDtypeStruct((nsl,) + (a.shape if src_by is None else a.shape[1:]), a.dtype)
                 for a in arrays]
    any_spec = pl.BlockSpec(memory_space=pl.ANY)
    res = pl.pallas_call(
        body, name=name, in_specs=[any_spec] * n, out_specs=[any_spec] * n, out_shape=out_shape,
        scratch_shapes=[pltpu.SemaphoreType.DMA((n, nf)), pltpu.SemaphoreType.DMA((n, nf))],
        compiler_params=pltpu.CompilerParams(has_side_effects=True))(*arrays)
    return [_put_own(l, a, group, src_by) for a, l in zip(arrays, res)]


_HBM = pl.BlockSpec(memory_space=pltpu.HBM)
_SEM = pl.BlockSpec(memory_space=pltpu.SEMAPHORE)
_DATAFLOW = pltpu.SideEffectType.DATAFLOW_SIDE_EFFECTING


def _group_copies(ins, lands, send_sems, recv_sems, group, src_by):
    mx, my, mc = lax.axis_index("x"), lax.axis_index("y"), lax.axis_index("c")
    me = _slot(group, mx, my, mc)
    pairs = []
    for a in range(len(ins)):
        for fx, fy, fc in _FLIPS[group]:
            peer = (1 - mx if fx else mx, 1 - my if fy else my, 1 - mc if fc else mc)
            src = ins[a] if src_by is None else ins[a].at[_slot(src_by, *peer)]
            mk = functools.partial(pltpu.make_async_remote_copy, src_ref=src, send_sem=send_sems,
                                   recv_sem=recv_sems, device_id=peer, device_id_type=pl.DeviceIdType.MESH)
            pairs.append((mk(dst_ref=lands[a].at[me]), mk(dst_ref=lands[a].at[_slot(group, *peer)])))
    return pairs


def _split_start(name, arrays, group='xy', src_by=None):
    n = len(arrays)
    lands = [lax.empty((_NSLOT[group],) + (a.shape if src_by is None else a.shape[1:]), a.dtype) for a in arrays]

    def body(*refs):
        ins, lnd, send_sems, recv_sems, token = refs[:n], refs[n:2 * n], refs[2 * n], refs[2 * n + 1], refs[-1]
        for to_peer, _ in _group_copies(ins, lnd, send_sems, recv_sems, group, src_by):
            to_peer.start()
        token[...] = jnp.zeros(token.shape, F32)

    ops = [pltpu.with_memory_space_constraint(a, pltpu.HBM) for a in [*arrays, *lands]]
    res = pl.pallas_call(
        body, name=name, in_specs=[_HBM] * (2 * n),
        out_specs=[_SEM, _SEM] + [_HBM] * (2 * n) + [pl.BlockSpec(memory_space=pltpu.VMEM)],
        out_shape=[pltpu.SemaphoreType.DMA(()), pltpu.SemaphoreType.DMA(())]
        + [pltpu.HBM(a.shape, a.dtype) for a in ops] + [jax.ShapeDtypeStruct((8, 128), F32)],
        input_output_aliases={k: 2 + k for k in range(2 * n)},
        compiler_params=pltpu.CompilerParams(has_side_effects=_DATAFLOW))(*ops)
    return dict(n=n, group=group, src_by=src_by, send=res[0], recv=res[1], arrays=list(res[2:2 + n]),
                lands=list(res[2 + n:2 + 2 * n]), token=res[-1])


def _split_wait(name, st, after):
    n, group, src_by = st['n'], st['group'], st['src_by']

    def wait_body(*refs):
        ins, lnd, send_sems, recv_sems = refs[:n], refs[n:2 * n], refs[2 * n], refs[2 * n + 1]
        for to_peer, from_peer in _group_copies(ins, lnd, send_sems, recv_sems, group, src_by):
            to_peer.wait_send()
            from_peer.wait_recv()

    shapes = [pltpu.HBM(a.shape, a.dtype) for a in [*st['arrays'], *st['lands']]]
    res = pl.pallas_call(
        wait_body, name=name, in_specs=[_HBM] * (2 * n) + [_SEM, _SEM, pl.BlockSpec(memory_space=pl.ANY)],
        out_specs=[_HBM] * (2 * n), out_shape=shapes, input_output_aliases={k: k for k in range(2 * n)},
        compiler_params=pltpu.CompilerParams(has_side_effects=_DATAFLOW))(
            *st['arrays'], *st['lands'], st['send'], st['recv'], after)
    return [_put_own(l, a, group, src_by) for a, l in zip(res[:n], res[n:])]


def _unshard(g, axis):
    t = jnp.moveaxis(g, 0, axis)
    s = t.shape
    return t.reshape(s[:axis] + (s[axis] * s[axis + 1],) + s[axis + 2:])


def _to_shards(w, axis):
    s = w.shape
    t = w.reshape(s[:axis] + (4, s[axis] // 4) + s[axis + 1:])
    return jnp.moveaxis(t, axis, 0)


def _pre_fwd(x, g, sc, sh):
    r = lax.rsqrt(_rowmean(x * x) + NORM_EPS)
    return (x * r) * g * (1.0 + sc) + sh


def _pre_bwd(dh, x, g, sc):
    r = lax.rsqrt(_rowmean(x * x) + NORM_EPS)
    xn = x * r
    dxn = dh * (g * (1.0 + sc))
    dx = r * (dxn - xn * _rowmean(dxn * xn))
    t = dh * xn
    return dx, _colsum(dh), _colsum(t * g), _colsum(t * (1.0 + sc))


def _post_fwd(x, y, gt, g):
    r = lax.rsqrt(_rowmean(y * y) + NORM_EPS)
    return x + gt * ((y * r) * g)


def _post_bwd(dxo, y, gt, g):
    r = lax.rsqrt(_rowmean(y * y) + NORM_EPS)
    yn = y * r
    t = dxo * yn
    dyn = dxo * (gt * g)
    dy = r * (dyn - yn * _rowmean(dyn * yn))
    return dy, _colsum(t * g), _colsum(t * gt)


def _gain_bwd(dy, x, g):
    r = lax.rsqrt(_rowmean(x * x) + NORM_EPS)
    xn = x * r
    dxn = dy * g
    return r * (dxn - xn * _rowmean(dxn * xn)), _colsum(dy * xn)


def _rope(x, cos, sa, sb):
    return x * cos + pltpu.roll(x, HEAD_PAD - 16, 1) * sa + pltpu.roll(x, 16, 1) * sb


def _rope_t(d, cos, sa, sb):
    return d * cos + pltpu.roll(d * sa, 16, 1) + pltpu.roll(d * sb, HEAD_PAD - 16, 1)


def _rope_tables(pos_f):
    S = pos_f.shape[0]
    inv = ROPE_THETA ** (-jnp.arange(0, QK_ROPE, 2, dtype=F32) / QK_ROPE)
    inv_ext = jnp.concatenate([jnp.zeros((QK_NOPE,), F32), inv, inv,
                               jnp.zeros((HEAD_PAD - QK_NOPE - QK_ROPE,), F32)]).reshape(1, HEAD_PAD)

    def fn(p, iv):
        ang = p * iv
        lane = lax.broadcasted_iota(jnp.int32, ang.shape, 1)
        s = jnp.sin(ang)
        first = (lane >= QK_NOPE) & (lane < QK_NOPE + QK_ROPE // 2)
        second = (lane >= QK_NOPE + QK_ROPE // 2) & (lane < QK_NOPE + QK_ROPE)
        return (jnp.cos(ang), jnp.where(first, -s, 0.0), jnp.where(second, s, 0.0)), ()

    (cos, sa, sb), _ = _rowk("rope_tables", fn, [pos_f], [inv_ext], [(HEAD_PAD, F32)] * 3, [])
    return cos, sa, sb


def _diag_mask(transposed):
    r = lax.broadcasted_iota(jnp.int32, (BQ, BQ), 0) >> CHUNK_SHIFT
    c = lax.broadcasted_iota(jnp.int32, (BQ, BQ), 1) >> CHUNK_SHIFT
    return (r <= c) if transposed else (c <= r)


_NT = (((1,), (1,)), ((), ()))
_NN = (((1,), (0,)), ((), ()))


def _attn_fwd(qf, kvf, HB=HF):
    S = qf.shape[0]
    nq = S // BQ

    def body(q_ref, kv_ref, o_ref, ob_ref, lse_ref):
        qi = pl.program_id(1)
        qs = [q_ref[:, hh * HEAD_PAD:(hh + 1) * HEAD_PAD] for hh in range(HB)]

        def step(j, carry, diag):
            off = pl.multiple_of(j * BQ, BQ)
            sts = [lax.dot_general(kv_ref[pl.ds(off, BQ), pl.ds(2 * hh * HEAD_PAD, HEAD_PAD)], qs[hh], _NT,
                                   preferred_element_type=F32) for hh in range(HB)]
            mid = []
            for hh in range(HB):
                m, l, acc = carry[hh]
                st = jnp.where(_diag_mask(True), sts[hh], NEG_INF) if diag else sts[hh]
                m2 = jnp.maximum(m, jnp.max(st, axis=0, keepdims=True))
                al = jnp.exp2(m - m2)
                pt = jnp.exp2(st - m2)
                mid.append((m2, l * al + jnp.sum(pt, axis=0, keepdims=True), acc * al, pt.astype(BF16)))
            out = []
            for hh in range(HB):
                m2, l2, acc_s, ptb = mid[hh]
                v = kv_ref[pl.ds(off, BQ), pl.ds((2 * hh + 1) * HEAD_PAD, HEAD_PAD)]
                out.append((m2, l2, acc_s + lax.dot_general(v, ptb, _TN, preferred_element_type=F32)))
            return tuple(out)

        init = tuple((jnp.full((1, BQ), NEG_INF, F32), jnp.zeros((1, BQ), F32), jnp.zeros((HEAD_PAD, BQ), F32))
                     for _ in range(HB))
        carry = lax.fori_loop(0, qi, lambda j, c: step(j, c, False), init)
        carry = step(qi, carry, True)
        for hh in range(HB):
            m, l, acc = carry[hh]
            ov = (acc / l).T
            o_ref[:, hh * HEAD_PAD:(hh + 1) * HEAD_PAD] = ov
            ob_ref[:, hh * HEAD_PAD:(hh + 1) * HEAD_PAD] = ov.astype(BF16)
            lse_ref[hh] = m + jnp.log(l) * LOG2E

    return pl.pallas_call(
        body, name="attn_fwd", grid=(N_HEADS // HB, nq),
        in_specs=[pl.BlockSpec((BQ, HB * HEAD_PAD), lambda g, i: (i, g)),
                  pl.BlockSpec((S, 2 * HB * HEAD_PAD), lambda g, i: (0, g))],
        out_specs=[pl.BlockSpec((BQ, HB * HEAD_PAD), lambda g, i: (i, g)),
                   pl.BlockSpec((BQ, HB * HEAD_PAD), lambda g, i: (i, g)),
                   pl.BlockSpec((HB, None, 1, BQ), lambda g, i: (g, i, 0, 0))],
        out_shape=[jax.ShapeDtypeStruct((S, QW), F32), jax.ShapeDtypeStruct((S, QW), BF16),
                   jax.ShapeDtypeStruct((N_HEADS, nq, 1, BQ), F32)],
        compiler_params=_cparams("parallel", "arbitrary"))(qf, kvf)


def _attn_delta(dob, o):
    S = o.shape[0]

    def fn(dov, ov):
        prod = dov.astype(F32) * ov
        lane = lax.broadcasted_iota(jnp.int32, (prod.shape[0], HEAD_PAD), 1)
        out = jnp.zeros((prod.shape[0], HEAD_PAD), F32)
        for h in range(N_HEADS):
            out = jnp.where(lane == h, jnp.sum(prod[:, h * HEAD_PAD:(h + 1) * HEAD_PAD], axis=1, keepdims=True), out)
        return (out,), ()

    (dd,), _ = _rowk("attn_delta", fn, [dob, o], [], [(HEAD_PAD, F32)], [])
    return dd[:, :N_HEADS].T.reshape(N_HEADS, S // BQ, 1, BQ)


_TN = (((0,), (0,)), ((), ()))


def _attn_bwd(qf, kvf, dob, lse_row, dd_row, cos, sa, sb):
    S = qf.shape[0]
    nq = S // BQ

    def body(kv_ref, q_ref, do_ref, lse_ref, dd_ref, cos_ref, sa_ref, sb_ref, dqo_ref, dkv_ref, dq_ref):
        kj = pl.program_id(1)

        @pl.when(kj == 0)
        def _():
            dq_ref[...] = jnp.zeros(dq_ref.shape, F32)

        ks = [kv_ref[:, 2 * hh * HEAD_PAD:(2 * hh + 1) * HEAD_PAD] for hh in range(HB)]
        vs = [kv_ref[:, (2 * hh + 1) * HEAD_PAD:(2 * hh + 2) * HEAD_PAD] for hh in range(HB)]

        def step(i, carry, diag):
            off = pl.multiple_of(i * BQ, BQ)
            cols = [pl.ds(hh * HEAD_PAD, HEAD_PAD) for hh in range(HB)]
            q = [q_ref[pl.ds(off, BQ), cols[hh]] for hh in range(HB)]
            do = [do_ref[pl.ds(off, BQ), cols[hh]] for hh in range(HB)]
            sts = [lax.dot_general(ks[hh], q[hh], _NT, preferred_element_type=F32) for hh in range(HB)]
            dpts = [lax.dot_general(vs[hh], do[hh], _NT, preferred_element_type=F32) for hh in range(HB)]
            mid = []
            for hh in range(HB):
                st = jnp.where(_diag_mask(True), sts[hh], NEG_INF) if diag else sts[hh]
                pt = jnp.exp2(st - lse_ref[hh, i])
                mid.append((pt.astype(BF16), (pt * (dpts[hh] - dd_ref[hh, i])).astype(BF16)))
            out = []
            for hh in range(HB):
                dk, dv = carry[hh]
                ptb, dsb = mid[hh]
                dv2 = dv + lax.dot_general(ptb, do[hh], _NN, preferred_element_type=F32)
                dk2 = dk + lax.dot_general(dsb, q[hh], _NN, preferred_element_type=F32)
                dq_ref[pl.ds(off, BQ), cols[hh]] += lax.dot_general(dsb, ks[hh], _TN, preferred_element_type=F32)
                out.append((dk2, dv2))
            return tuple(out)

        zero = jnp.zeros((BQ, HEAD_PAD), F32)
        carry = step(kj, tuple((zero, zero) for _ in range(HB)), True)
        carry = lax.fori_loop(kj + 1, nq, lambda i, c: step(i, c, False), carry)
        done = pl.ds(pl.multiple_of(kj * BQ, BQ), BQ)
        for hh in range(HB):
            dk, dv = carry[hh]
            dk = _rope_t(dk * (1.0 / LOG2E), cos_ref[...], sa_ref[...], sb_ref[...])
            dkv_ref[:, 2 * hh * HEAD_PAD:(2 * hh + 1) * HEAD_PAD] = dk.astype(BF16)
            dkv_ref[:, (2 * hh + 1) * HEAD_PAD:(2 * hh + 2) * HEAD_PAD] = dv.astype(BF16)
            cols = pl.ds(hh * HEAD_PAD, HEAD_PAD)
            dqo_ref[:, cols] = _rope_t(dq_ref[done, cols] * ATT_SCALE, cos_ref[...], sa_ref[...],
                                       sb_ref[...]).astype(BF16)

    tab = pl.BlockSpec((BQ, HEAD_PAD), lambda g, j: (j, 0))
    row = pl.BlockSpec((HB, nq, 1, BQ), lambda g, j: (g, 0, 0, 0))
    seq = pl.BlockSpec((S, HB * HEAD_PAD), lambda g, j: (0, g), pipeline_mode=pl.Buffered(1))
    kvb = pl.BlockSpec((BQ, 2 * HB * HEAD_PAD), lambda g, j: (j, g))
    return pl.pallas_call(
        body, name="attn_bwd", grid=(N_HEADS // HB, nq),
        in_specs=[kvb, seq, seq, row, row, tab, tab, tab],
        out_specs=[pl.BlockSpec((BQ, HB * HEAD_PAD), lambda g, j: (j, g)), kvb],
        out_shape=[jax.ShapeDtypeStruct((S, QW), BF16), jax.ShapeDtypeStruct((S, KVW), BF16)],
        scratch_shapes=[pltpu.VMEM((S, HB * HEAD_PAD), F32)],
        compiler_params=_cparams("parallel", "arbitrary"))(kvf, qf, dob, lse_row, dd_row, cos, sa, sb)


DC = 128
TR = 256


def _dwconv_fwd(u, w, b):
    S, Dm = u.shape
    tr = min(TR, S)

    def body(u_ref, w_ref, b_ref, o_ref, pad_ref):
        pad_ref[pl.ds(0, PAD_ROWS), :] = jnp.zeros((PAD_ROWS, DC), F32)
        pad_ref[pl.ds(PAD_ROWS, S), :] = u_ref[...]
        wv = w_ref[...]
        for r in range(S // tr):
            acc = jnp.broadcast_to(b_ref[...], (tr, DC))
            for j in range(CONV_WIDTH):
                acc = acc + wv[j:j + 1, :] * pad_ref[pl.ds(r * tr + PAD_ROWS - (CONV_WIDTH - 1) + j, tr), :]
            o_ref[pl.ds(r * tr, tr), :] = acc

    return pl.pallas_call(
        body, name="dwconv_fwd", grid=(Dm // DC,),
        in_specs=[pl.BlockSpec((S, DC), lambda c: (0, c)), pl.BlockSpec((CONV_WIDTH, DC), lambda c: (0, c)),
                  pl.BlockSpec((1, DC), lambda c: (0, c))],
        out_specs=pl.BlockSpec((S, DC), lambda c: (0, c)),
        out_shape=jax.ShapeDtypeStruct((S, Dm), F32),
        scratch_shapes=[pltpu.VMEM((S + PAD_ROWS, DC), F32)],
        compiler_params=_cparams("parallel"))(u, w, b)


def _dwconv_bwd(d, u, w):
    S, Dm = u.shape
    tr = min(TR, S)

    def body(d_ref, u_ref, w_ref, du_ref, dw_ref, padd_ref, padu_ref):
        padd_ref[pl.ds(0, S), :] = d_ref[...]
        padd_ref[pl.ds(S, PAD_ROWS), :] = jnp.zeros((PAD_ROWS, DC), F32)
        padu_ref[pl.ds(0, PAD_ROWS), :] = jnp.zeros((PAD_ROWS, DC), F32)
        padu_ref[pl.ds(PAD_ROWS, S), :] = u_ref[...]
        wv = w_ref[...]
        dws = [jnp.zeros((1, DC), F32) for _ in range(CONV_WIDTH)]
        for r in range(S // tr):
            acc = jnp.zeros((tr, DC), F32)
            for j in range(CONV_WIDTH):
                acc = acc + wv[j:j + 1, :] * padd_ref[pl.ds(r * tr + (CONV_WIDTH - 1) - j, tr), :]
            du_ref[pl.ds(r * tr, tr), :] = acc
            dt = d_ref[pl.ds(r * tr, tr), :]
            for j in range(CONV_WIDTH):
                ut = padu_ref[pl.ds(r * tr + PAD_ROWS - (CONV_WIDTH - 1) + j, tr), :]
                dws[j] = dws[j] + _colsum(dt * ut)
        for j in range(CONV_WIDTH):
            dw_ref[pl.ds(j, 1), :] = dws[j]
        dw_ref[pl.ds(CONV_WIDTH, 1), :] = jnp.zeros((1, DC), F32)

    blk = pl.BlockSpec((S, DC), lambda c: (0, c))
    return pl.pallas_call(
        body, name="dwconv_bwd", grid=(Dm // DC,),
        in_specs=[blk, blk, pl.BlockSpec((CONV_WIDTH, DC), lambda c: (0, c))],
        out_specs=[blk, pl.BlockSpec((PAD_ROWS, DC), lambda c: (0, c))],
        out_shape=[jax.ShapeDtypeStruct((S, Dm), F32), jax.ShapeDtypeStruct((PAD_ROWS, Dm), F32)],
        scratch_shapes=[pltpu.VMEM((S + PAD_ROWS, DC), F32), pltpu.VMEM((S + PAD_ROWS, DC), F32)],
        compiler_params=_cparams("parallel"))(d, u, w)


POOL_C = D_MODEL // len(POOL_WINDOWS)


def _pool_counts(r, tr, win):
    t = r * tr + lax.broadcasted_iota(jnp.int32, (tr, 1), 0)
    return jnp.minimum(t + 1, win).astype(F32)


def _pool_fwd(h):
    S, Dm = h.shape
    tr = min(TR, S)

    def body(h_ref, o_ref, pad_ref):
        pad_ref[pl.ds(0, PAD_ROWS), :] = jnp.zeros((PAD_ROWS, POOL_C), F32)
        pad_ref[pl.ds(PAD_ROWS, S), :] = h_ref[...]
        for g, win in enumerate(POOL_WINDOWS):
            @pl.when(pl.program_id(0) == g)
            def _():
                for r in range(S // tr):
                    acc = pad_ref[pl.ds(r * tr + PAD_ROWS, tr), :]
                    for j in range(1, win):
                        acc = acc + pad_ref[pl.ds(r * tr + PAD_ROWS - j, tr), :]
                    pooled = acc / _pool_counts(r, tr, win)
                    o_ref[pl.ds(r * tr, tr), :] = (pooled - h_ref[pl.ds(r * tr, tr), :]).astype(BF16)

    blk = pl.BlockSpec((S, POOL_C), lambda g: (0, g))
    return pl.pallas_call(
        body, name="pool_fwd", grid=(len(POOL_WINDOWS),), in_specs=[blk], out_specs=blk,
        out_shape=jax.ShapeDtypeStruct((S, Dm), BF16),
        scratch_shapes=[pltpu.VMEM((S + PAD_ROWS, POOL_C), F32)],
        compiler_params=_cparams("parallel"))(h)


def _pool_bwd(dp):
    S, Dm = dp.shape
    tr = min(TR, S)

    def body(d_ref, o_ref, pad_ref):
        pad_ref[pl.ds(S, PAD_ROWS), :] = jnp.zeros((PAD_ROWS, POOL_C), F32)
        for g, win in enumerate(POOL_WINDOWS):
            @pl.when(pl.program_id(0) == g)
            def _():
                for r in range(S // tr):
                    pad_ref[pl.ds(r * tr, tr), :] = d_ref[pl.ds(r * tr, tr), :] / _pool_counts(r, tr, win)
                for r in range(S // tr):
                    acc = pad_ref[pl.ds(r * tr, tr), :]
                    for j in range(1, win):
                        acc = acc + pad_ref[pl.ds(r * tr + j, tr), :]
                    o_ref[pl.ds(r * tr, tr), :] = acc - d_ref[pl.ds(r * tr, tr), :]

    blk = pl.BlockSpec((S, POOL_C), lambda g: (0, g))
    return pl.pallas_call(
        body, name="pool_bwd", grid=(len(POOL_WINDOWS),), in_specs=[blk], out_specs=blk,
        out_shape=jax.ShapeDtypeStruct((S, Dm), F32),
        scratch_shapes=[pltpu.VMEM((S + PAD_ROWS, POOL_C), F32)],
        compiler_params=_cparams("parallel"))(dp)


def _bias_spec(tn):
    return pl.BlockSpec((1, tn), lambda i, j, k: (0, j))


def _mla_weights(w_dq, w_dkv, w_uq, w_ukv, w_o):
    wd = jnp.concatenate([w_dq, w_dkv], axis=1)
    wq = jnp.pad(w_uq.reshape(Q_LORA, N_HEADS, QK_NOPE + QK_ROPE),
                 ((0, 0), (0, 0), (0, HEAD_PAD - QK_NOPE - QK_ROPE))).reshape(Q_LORA, QW)
    ukv = w_ukv.reshape(KV_LORA, N_HEADS, QK_NOPE + V_HEAD)
    wkv = jnp.zeros((DKV, N_HEADS, 2 * HEAD_PAD), BF16)
    wkv = wkv.at[:KV_LORA, :, :QK_NOPE].set(ukv[:, :, :QK_NOPE])
    wkv = wkv.at[:KV_LORA, :, HEAD_PAD:HEAD_PAD + V_HEAD].set(ukv[:, :, QK_NOPE:])
    eye = jnp.broadcast_to(jnp.eye(QK_ROPE, dtype=BF16)[:, None, :], (QK_ROPE, N_HEADS, QK_ROPE))
    wkv = wkv.at[KV_LORA:, :, QK_NOPE:QK_NOPE + QK_ROPE].set(eye).reshape(DKV, KVW)
    wo = jnp.pad(w_o.reshape(N_HEADS, V_HEAD, D_MODEL),
                 ((0, 0), (0, HEAD_PAD - V_HEAD), (0, 0))).reshape(QW, D_MODEL)
    return dict(wd=wd, wq=wq, wkv=wkv, wo=wo)


def _mla_weight_grads(g_wd, g_wq, g_wkv, g_wo):
    g_uq = g_wq.reshape(Q_LORA, N_HEADS, HEAD_PAD)[:, :, :QK_NOPE + QK_ROPE].reshape(Q_LORA, -1)
    t = g_wkv.reshape(DKV, N_HEADS, 2 * HEAD_PAD)[:KV_LORA]
    g_ukv = jnp.concatenate([t[:, :, :QK_NOPE], t[:, :, HEAD_PAD:HEAD_PAD + V_HEAD]], axis=2)
    g_o = g_wo.reshape(N_HEADS, HEAD_PAD, D_MODEL)[:, :V_HEAD].reshape(N_HEADS * V_HEAD, D_MODEL)
    return dict(mla_w_dq=g_wd[:, :Q_LORA], mla_w_uq=g_uq, mla_w_dkv=g_wd[:, Q_LORA:],
                mla_w_ukv=g_ukv.reshape(KV_LORA, -1), mla_w_o=g_o)


def _rope_epilogue(kv):
    def epi(acc, cos, sa, sb):
        parts = []
        for t in range(acc.shape[1] // HEAD_PAD):
            x = acc[:, t * HEAD_PAD:(t + 1) * HEAD_PAD]
            if kv:
                parts.append(x if t % 2 else _rope(x, cos, sa, sb))
            else:
                parts.append(_rope(x, cos, sa, sb) * SCALE_LOG2E)
        return (jnp.concatenate(parts, axis=1),)
    return epi


def _mla_fwd(tag, h, P, rope, before_attention=None):
    S = h.shape[0]
    cos, sa, sb = rope
    tp = min(512, S)
    tabs = [pl.BlockSpec((tp, HEAD_PAD), lambda i, j, k: (i, 0))] * 3
    cqkv = _mm(f"mla_down{tag}", h, P['wd'], 'nn', S, DQKV, D_MODEL)

    def norms(x, qg, kg):
        xq, xk, xr = x[:, :Q_LORA], x[:, Q_LORA:Q_LORA + KV_LORA], x[:, Q_LORA + KV_LORA:]
        cq = xq * lax.rsqrt(_rowmean(xq * xq) + NORM_EPS) * qg
        ck = xk * lax.rsqrt(_rowmean(xk * xk) + NORM_EPS) * kg
        return (cq, jnp.concatenate([ck, xr], axis=1)), ()

    (cq, ckv), _ = _rowk(f"mla_norms{tag}", norms, [cqkv], [P['qg'], P['kg']], [(Q_LORA, BF16), (DKV, BF16)], [])
    qf = _mm(f"mla_q{tag}", cq, P['wq'], 'nn', S, QW, Q_LORA, tm=tp, tn=QW, extras=[cos, sa, sb],
             extra_specs=tabs, epi=_rope_epilogue(False), outs=[jax.ShapeDtypeStruct((S, QW), BF16)])
    kvf = _mm(f"mla_kv{tag}", ckv, P['wkv'], 'nn', S, KVW, DKV, tm=tp, tn=KVW, extras=[cos, sa, sb],
              extra_specs=tabs, epi=_rope_epilogue(True), outs=[jax.ShapeDtypeStruct((S, KVW), BF16)])
    if before_attention is not None:
        before_attention(kvf)
    o, ob, lse = _attn_fwd(qf, kvf)
    y = _mm(f"mla_o{tag}", ob, P['wo'], 'nn', S, D_MODEL, QW)
    return y, dict(cqkv=cqkv, cq=cq, ckv=ckv, qf=qf, kvf=kvf, o=o, ob=ob, lse=lse)


def _mla_bwd(tag, dy, h, sv, P, rope):
    S = h.shape[0]
    nq = S // BQ
    cos, sa, sb = rope
    g_wo = _mm(f"mla_o_wg{tag}", sv['ob'], dy, 'tn', QW, D_MODEL, S)
    dob = _mm(f"mla_o_dg{tag}", dy, P['wo'], 'nt', S, QW, D_MODEL, outs=[jax.ShapeDtypeStruct((S, QW), BF16)])
    dd = _attn_delta(dob, sv['o'])
    dq, dkv = _attn_bwd(sv['qf'], sv['kvf'], dob, sv['lse'], dd, cos, sa, sb)
    g_wq = _mm(f"mla_q_wg{tag}", sv['cq'], dq, 'tn', Q_LORA, QW, S)
    dcq = _mm(f"mla_q_dg{tag}", dq, P['wq'], 'nt', S, Q_LORA, QW)
    g_wkv = _mm(f"mla_kv_wg{tag}", sv['ckv'], dkv, 'tn', DKV, KVW, S)
    dckv = _mm(f"mla_kv_dg{tag}", dkv, P['wkv'], 'nt', S, DKV, KVW)

    def norms_bwd(dcq_v, dckv_v, x, qg, kg):
        xq, xk = x[:, :Q_LORA], x[:, Q_LORA:Q_LORA + KV_LORA]
        dxq, dqg = _gain_bwd(dcq_v, xq, qg)
        dxk, dkg = _gain_bwd(dckv_v[:, :KV_LORA], xk, kg)
        return (jnp.concatenate([dxq, dxk, dckv_v[:, KV_LORA:]], axis=1),), (dqg, dkg)

    (dcqkv,), (dqg, dkg) = _rowk(f"mla_norms_bwd{tag}", norms_bwd, [dcq, dckv, sv['cqkv']], [P['qg'], P['kg']],
                                 [(DQKV, BF16)], [Q_LORA, KV_LORA])
    g_wd = _mm(f"mla_down_wg{tag}", h, dcqkv, 'tn', D_MODEL, DQKV, S)
    dh = _mm(f"mla_down_dg{tag}", dcqkv, P['wd'], 'nt', S, D_MODEL, DQKV)
    grads = _mla_weight_grads(g_wd, g_wq, g_wkv, g_wo)
    grads.update(mla_q_norm_g=dqg.reshape(-1), mla_kv_norm_g=dkg.reshape(-1))
    return dh, grads


def _conv_fwd(h, P):
    S = h.shape[0]
    a = _mm("conv_pw1", h, P['w_pw1'], 'nn', S, 2 * D_MODEL, D_MODEL, extras=[P['b_pw1']],
            extra_specs=[_bias_spec(1024)], epi=lambda acc, b: (acc + b,))
    (u0,), _ = _rowk("conv_glu", lambda av: ((av[:, :D_MODEL] * _sigmoid(av[:, D_MODEL:]),), ()),
                     [a], [], [(D_MODEL, F32)], [])
    u1 = _dwconv_fwd(u0, P['w_dw'], P['b_dw'])

    def ln_silu(u, g, b):
        xc = u - _rowmean(u)
        z = xc * lax.rsqrt(_rowmean(xc * xc) + NORM_EPS) * g + b
        return (z * _sigmoid(z),), ()

    (u3,), _ = _rowk("conv_ln", ln_silu, [u1], [P['ln_g'], P['ln_b']], [(D_MODEL, BF16)], [])
    y = _mm("conv_pw2", u3, P['w_pw2'], 'nn', S, D_MODEL, D_MODEL, extras=[P['b_pw2']],
            extra_specs=[_bias_spec(1024)], epi=lambda acc, b: (acc + b,))
    return y, dict(a=a, u0=u0, u1=u1, u3=u3)


def _conv_bwd(dy, dy_colsum, h, sv, P):
    S = h.shape[0]
    g_pw2 = _mm("conv_pw2_wg", sv['u3'], dy, 'tn', D_MODEL, D_MODEL, S)
    du3 = _mm("conv_pw2_dg", dy, P['w_pw2'], 'nt', S, D_MODEL, D_MODEL)

    def ln_bwd(d3, u, g, b):
        xc = u - _rowmean(u)
        rstd = lax.rsqrt(_rowmean(xc * xc) + NORM_EPS)
        xh = xc * rstd
        z = xh * g + b
        sg = _sigmoid(z)
        dz = d3 * (sg * (1.0 + z * (1.0 - sg)))
        dxh = dz * g
        du = rstd * (dxh - _rowmean(dxh) - xh * _rowmean(dxh * xh))
        return (du,), (_colsum(dz * xh), _colsum(dz), _colsum(du))

    (du1,), (d_lng, d_lnb, d_bdw) = _rowk("conv_ln_bwd", ln_bwd, [du3, sv['u1']], [P['ln_g'], P['ln_b']],
                                          [(D_MODEL, F32)], [D_MODEL] * 3)
    du0, d_wdw = _dwconv_bwd(du1, sv['u0'], P['w_dw'])

    def glu_bwd(d0, av):
        a1, sg = av[:, :D_MODEL], _sigmoid(av[:, D_MODEL:])
        da = jnp.concatenate([d0 * sg, d0 * a1 * sg * (1.0 - sg)], axis=1)
        return (da,), (_colsum(da),)

    (da,), (d_bpw1,) = _rowk("conv_glu_bwd", glu_bwd, [du0, sv['a']], [], [(2 * D_MODEL, BF16)], [2 * D_MODEL])
    g_pw1 = _mm("conv_pw1_wg", h, da, 'tn', D_MODEL, 2 * D_MODEL, S)
    dh = _mm("conv_pw1_dg", da, P['w_pw1'], 'nt', S, D_MODEL, 2 * D_MODEL)
    grads = dict(conv_w_pw1=g_pw1, conv_b_pw1=d_bpw1.reshape(-1), conv_w_dw=d_wdw[:CONV_WIDTH],
                 conv_b_dw=d_bdw.reshape(-1), conv_ln_g=d_lng.reshape(-1), conv_ln_b=d_lnb.reshape(-1),
                 conv_w_pw2=g_pw2, conv_b_pw2=dy_colsum.reshape(-1))
    return dh, grads


def _pool_group_specs(tm):
    return (pl.BlockSpec((tm, POOL_C), lambda i, j, k: (i, j)),
            pl.BlockSpec((None, POOL_C, POOL_C), lambda i, j, k: (j, 0, 0)))


def _pool_mixer_fwd(h, P):
    S = h.shape[0]
    p = _pool_fwd(h)
    a_spec, b_spec = _pool_group_specs(min(1024, S))
    y, z = _mm("pool_mm", p, P['w'], 'nn', S, D_MODEL, POOL_C, tn=POOL_C, a_spec=a_spec, b_spec=b_spec,
               extras=[P['b'], P['scale']], extra_specs=[_bias_spec(POOL_C)] * 2,
               epi=lambda acc, b, s: ((acc + b) * s, acc + b),
               outs=[jax.ShapeDtypeStruct((S, D_MODEL), F32)] * 2)
    return y, dict(p=p, z=z)


def _pool_mixer_bwd(dy, sv, P):
    S = dy.shape[0]

    def scale_bwd(d, z, s):
        dz = d * s
        return (dz,), (_colsum(d * z), _colsum(dz))

    (dz,), (d_scale, d_b) = _rowk("pool_scale_bwd", scale_bwd, [dy, sv['z']], [P['scale']],
                                  [(D_MODEL, BF16)], [D_MODEL] * 2)
    a_spec, b_spec = _pool_group_specs(min(1024, S))
    dp = _mm("pool_mm_dg", dz, P['w'], 'nt', S, D_MODEL, POOL_C, tn=POOL_C, a_spec=a_spec, b_spec=b_spec)
    tk = min(512, S)
    grp = pl.BlockSpec((tk, POOL_C), lambda i, j, k: (k, j))
    g_w = _mm("pool_mm_wg", sv['p'], dz, 'tn', POOL_C, D_MODEL, S, tn=POOL_C, tk=tk, a_spec=grp, b_spec=grp,
              outs=[jax.ShapeDtypeStruct((len(POOL_WINDOWS), POOL_C, POOL_C), F32)],
              out_specs=[pl.BlockSpec((None, POOL_C, POOL_C), lambda i, j, k: (j, 0, 0))])
    dh = _pool_bwd(dp)
    return dh, dict(pool_w=g_w, pool_b=d_b.reshape(-1), pool_scale=d_scale.reshape(-1))


def _adamw(w, g, m, v):
    m2 = ADAM_B1 * m + (1.0 - ADAM_B1) * g
    v2 = ADAM_B2 * v + (1.0 - ADAM_B2) * (g * g)
    m_hat = m2 / (1.0 - ADAM_B1 ** ADAM_STEP)
    v_hat = v2 / (1.0 - ADAM_B2 ** ADAM_STEP)
    delta = -ADAM_LR * (m_hat / (jnp.sqrt(v_hat) + ADAM_EPS) + ADAM_WD * w)
    return delta, m2, v2


def _finish(name, w, land, m, v, layer=None, prev=None):
    local = land.shape[1:]
    C = local[-1]
    R = land[0].size // C
    tr = 64 if R % 64 == 0 else R

    def body(land_hbm, g_hbm, land_v, g_v, recv_v, io_sem, send_sem, recv_sem):
        load = pltpu.make_async_copy(land_hbm, land_v, io_sem)
        load.start()
        load.wait()

        def rows_of(i):
            return pl.ds(pl.multiple_of(i * tr, tr), tr)

        def sum_chunk(i, carry):
            rows = rows_of(i)
            g_v[rows, :] = ((land_v[0, rows, :].astype(F32) + land_v[1, rows, :].astype(F32))
                            + land_v[2, rows, :].astype(F32)) + land_v[3, rows, :].astype(F32)
            return carry

        lax.fori_loop(0, R // tr, sum_chunk, 0)
        swap = pltpu.make_async_remote_copy(
            src_ref=g_v, dst_ref=recv_v, send_sem=send_sem, recv_sem=recv_sem,
            device_id=(lax.axis_index("x"), lax.axis_index("y"), 1 - lax.axis_index("c")),
            device_id_type=pl.DeviceIdType.MESH)
        swap.start()
        swap.wait()

        def add_chunk(i, carry):
            rows = rows_of(i)
            recv_v[rows, :] = g_v[rows, :] + recv_v[rows, :]
            return carry

        lax.fori_loop(0, R // tr, add_chunk, 0)
        store = pltpu.make_async_copy(recv_v, g_hbm, io_sem)
        store.start()
        store.wait()

    any_spec = pl.BlockSpec(memory_space=pl.ANY)
    g = pl.pallas_call(
        body, name=name, in_specs=[any_spec], out_specs=any_spec, out_shape=jax.ShapeDtypeStruct((R, C), F32),
        scratch_shapes=[pltpu.VMEM((4, R, C), BF16), pltpu.VMEM((R, C), F32), pltpu.VMEM((R, C), F32),
                        pltpu.SemaphoreType.DMA, pltpu.SemaphoreType.DMA, pltpu.SemaphoreType.DMA],
        compiler_params=pltpu.CompilerParams(has_side_effects=True, vmem_limit_bytes=VMEM_LIMIT))(
            land.reshape(4, R, C))

    lead = () if layer is None else (w.shape[0],)
    as2d = lambda a: a.reshape(lead + (R, C))
    tu = _row_tile(R, C, budget=1 << 19)
    tile = pl.BlockSpec((tu, C), lambda i: (i, 0))
    slab = tile if layer is None else pl.BlockSpec((None, tu, C), lambda i: (layer, i, 0))
    n_prev = 0 if prev is None else 4

    def update(w_ref, g_ref, m_ref, v_ref, *rest):
        outs = rest[n_prev:]
        gv = g_ref[...]
        d, nm, nv = _adamw(w_ref[...], gv, m_ref[...], v_ref[...])
        for r, val in zip(outs, (gv, d, nm, nv)):
            r[...] = val

    res = pl.pallas_call(
        update, name=name + "_adamw", grid=(R // tu,),
        in_specs=[slab, tile, slab, slab] + [any_spec] * n_prev, out_specs=[slab] * 4,
        out_shape=[jax.ShapeDtypeStruct(lead + (R, C), F32)] * 4,
        input_output_aliases={4 + k: k for k in range(n_prev)},
        compiler_params=_cparams("arbitrary"))(
            as2d(w), g, as2d(m), as2d(v), *([] if prev is None else [as2d(p) for p in prev]))
    return [r.reshape(w.shape) for r in res]


def _row(v):
    return v.reshape(1, -1)


def kernel(x, c, positions, ada_w, ada_b, norm_g, mla_w_dq, mla_q_norm_g, mla_w_uq, mla_w_dkv, mla_kv_norm_g, mla_w_ukv, mla_w_o, conv_w_pw1, conv_b_pw1, conv_w_dw, conv_b_dw, conv_ln_g, conv_ln_b, conv_w_pw2, conv_b_pw2, pool_w, pool_b, pool_scale, ffn_w1, ffn_w2, loss_target, m_ada_w, m_ada_b, m_norm_g, m_mla_w_dq, m_mla_q_norm_g, m_mla_w_uq, m_mla_w_dkv, m_mla_kv_norm_g, m_mla_w_ukv, m_mla_w_o, m_conv_w_pw1, m_conv_b_pw1, m_conv_w_dw, m_conv_b_dw, m_conv_ln_g, m_conv_ln_b, m_conv_w_pw2, m_conv_b_pw2, m_pool_w, m_pool_b, m_pool_scale, m_ffn_w1, m_ffn_w2, v_ada_w, v_ada_b, v_norm_g, v_mla_w_dq, v_mla_q_norm_g, v_mla_w_uq, v_mla_w_dkv, v_mla_kv_norm_g, v_mla_w_ukv, v_mla_w_o, v_conv_w_pw1, v_conv_b_pw1, v_conv_w_dw, v_conv_b_dw, v_conv_ln_g, v_conv_ln_b, v_conv_w_pw2, v_conv_b_pw2, v_pool_w, v_pool_b, v_pool_scale, v_ffn_w1, v_ffn_w2):
    args = dict(locals())
    W = {n: args[n] for n in WEIGHTS}
    MOM = {n: args['m_' + n] for n in WEIGHTS}
    VAR = {n: args['v_' + n] for n in WEIGHTS}
    S = x.shape[1]
    xs = x.reshape(S, D_MODEL)
    tgt = loss_target.reshape(S, D_MODEL)
    mx, my, mc = lax.axis_index("x"), lax.axis_index("y"), lax.axis_index("c")
    chip = 2 * mx + my
    n_sh = ada_w.shape[2]

    def sent_of(key):
        n, l = key
        arr = W[n] if l is None else W[n][l]
        return arr.astype(BF16) if n in BIG or n in ('ffn_w1', 'ffn_w2') else arr

    keys0 = [(n, 0) for n in MLA_MATS] + [(n, None) for n in ('norm_g', 'mla_q_norm_g', 'mla_kv_norm_g',
                                                               'conv_w_dw', 'pool_b', 'pool_scale')]
    c8 = _exchange("gather_c", [c.reshape(8, D_MODEL // 8)], 'xyc')[0].reshape(8, D_MODEL)
    sent0 = [sent_of(k) for k in keys0]
    sent0[-1] = sent0[-1] + jnp.minimum(jnp.abs(c8[0, 0]), 0.0)
    fly0 = _split_start("gather_w0_start", sent0)
    c8 = c8 + fly0['token'][0, 0]
    c8 = jnp.pad(c8, ((0, ADA_ROWS - 8), (0, 0)))
    silu = lambda v: v * _sigmoid(v)
    mod_sh = []
    for l in range(DEPTH):
        b_l = lax.dynamic_slice(ada_b[l], (chip * n_sh,), (n_sh,)).reshape(1, n_sh)
        mod_sh.append(_mm(f"ada_fwd{l}", c8, ada_w, 'nn', ADA_ROWS, n_sh, D_MODEL, tn=n_sh // 2, tk=512, pro_a=silu,
                          b_spec=pl.BlockSpec((None, 512, n_sh // 2), lambda i, j, k, l=l: (l, k, j)),
                          extras=[b_l], extra_specs=[_bias_spec(n_sh // 2)], epi=lambda acc, b: (acc + b,))[:8])
    mod_sh = jnp.stack(mod_sh, axis=1).reshape(8, DEPTH * n_sh // 128, 128)
    mod = _exchange("scatter_mod", [mod_sh], 'xy', src_by='xyc')[0]
    mod = mod.reshape(4, DEPTH, n_sh).transpose(1, 0, 2).reshape(DEPTH, 6, 1, D_MODEL)

    keys1 = [('ffn_w1', 0), ('ffn_w2', 0), ('conv_w_pw1', None), ('conv_w_pw2', None), ('pool_w', None)]
    keys2 = [(n, l) for l in range(1, DEPTH) for n in ('ffn_w1', 'ffn_w2')] + [(n, 1) for n in MLA_MATS]
    fly1 = _split_start("gather_w1_start", [sent_of(k) for k in keys1])
    mod = mod + fly1['token'][0, 0]
    late = {}

    def start_group2(dep):
        sent2 = [sent_of(k) for k in keys2]
        sent2[-1] = sent2[-1] + jnp.minimum(jnp.abs(dep[0, 0]), 0).astype(BF16)
        late['fly2'] = _split_start("gather_w2_start", sent2)
    G = dict(zip(keys0, _split_wait("gather_w0_wait", fly0, mod)))

    def whole(key):
        n, l = key
        return _unshard(G[key], SHARD_AXIS[n] - (0 if l is None else 1))

    def mla_params(j):
        P = _mla_weights(*[whole((n, j)) for n in ('mla_w_dq', 'mla_w_dkv', 'mla_w_uq', 'mla_w_ukv', 'mla_w_o')])
        P.update(qg=_row(whole(('mla_q_norm_g', None))[j]), kg=_row(whole(('mla_kv_norm_g', None))[j]))
        return P

    gains = whole(('norm_g', None))
    mla_p = {0: mla_params(0)}
    conv_p = pool_p = None
    rope = _rope_tables(positions.reshape(S, 1).astype(F32))

    by_j = pl.BlockSpec((None, 1024, 1024), lambda i, j, k: (j, 0, 0))
    by_k = pl.BlockSpec((None, 1024, 1024), lambda i, j, k: (k, 0, 0))
    sq_relu = lambda v: jnp.square(jnp.maximum(v, 0.0))

    def md(i, k):
        return mod[i, k]

    (h,), _ = _rowk("pre0", lambda xv, g, sc, sh: ((_pre_fwd(xv, g, sc, sh),), ()),
                    [xs], [_row(gains[0, 0]), md(0, 1), md(0, 0)], [(D_MODEL, BF16)], [])
    saved = []
    xin = xs
    loss_acc = dxf = None
    for i in range(DEPTH):
        kind, j = i % 3, i // 3
        if kind == 0:
            if j not in mla_p:
                mla_p[j] = mla_params(j)
            y, sv = _mla_fwd(j, h, mla_p[j], rope, before_attention=start_group2 if i == 0 else None)
        elif kind == 1:
            y, sv = _conv_fwd(h, conv_p)
        else:
            y, sv = _pool_mixer_fwd(h, pool_p)

        def mid(xv, yv, gt, g1, g2, sc, sh):
            x1 = _post_fwd(xv, yv, gt, g1)
            return (x1, _pre_fwd(x1, g2, sc, sh)), ()

        (x1, h2), _ = _rowk(f"mid{i}", mid, [xin, y], [md(i, 2), _row(gains[i, 1]), _row(gains[i, 2]), md(i, 4), md(i, 3)],
                            [(D_MODEL, F32), (D_MODEL, BF16)], [])
        if i == 0:
            G.update(zip(keys1, _split_wait("gather_w1_wait", fly1, h2)))
            conv_p = dict(w_pw1=whole(('conv_w_pw1', None))[0], b_pw1=_row(conv_b_pw1[0]),
                          w_dw=whole(('conv_w_dw', None))[0], b_dw=_row(conv_b_dw[0]), ln_g=_row(conv_ln_g[0]),
                          ln_b=_row(conv_ln_b[0]), w_pw2=whole(('conv_w_pw2', None))[0], b_pw2=_row(conv_b_pw2[0]))
            pool_p = dict(w=whole(('pool_w', None))[0], b=_row(whole(('pool_b', None))[0]),
                          scale=_row(whole(('pool_scale', None))[0]))
        if i == 1:
            G.update(zip(keys2, _split_wait("gather_w2_wait", late['fly2'], h2)))
        a = _mm(f"ffn1_{i}", h2, G[('ffn_w1', i)], 'nn', S, D_FF, D_MODEL, tm=2048, b_spec=by_j,
                outs=[jax.ShapeDtypeStruct((S, D_FF), BF16)])
        y2 = _mm(f"ffn2_{i}", a, G[('ffn_w2', i)], 'nn', S, D_MODEL, D_FF, pro_a=sq_relu, b_spec=by_k)
        saved.append(dict(x0=xin, h=h, y=y, x1=x1, h2=h2, a=a, y2=y2, mix=sv))
        if i + 1 < DEPTH:
            def nxt(xv, yv, gt, g3, g0, sc, sh):
                x2 = _post_fwd(xv, yv, gt, g3)
                return (x2, _pre_fwd(x2, g0, sc, sh)), ()

            hdt = F32 if (i + 1) % 3 == 2 else BF16
            (xin, h), _ = _rowk(f"next{i}", nxt, [x1, y2],
                                [md(i, 5), _row(gains[i, 3]), _row(gains[i + 1, 0]), md(i + 1, 1), md(i + 1, 0)],
                                [(D_MODEL, F32), (D_MODEL, hdt)], [])
        else:
            def head(xv, yv, tv, gt, g3):
                err = _post_fwd(xv, yv, gt, g3) - tv
                per_row = jnp.sum(err * err, axis=1, keepdims=True) * (0.5 / D_MODEL)
                return (err * (1.0 / D_MODEL),), (jnp.broadcast_to(jnp.sum(per_row, axis=0, keepdims=True), (1, 128)),)

            (dxf,), (loss_acc,) = _rowk("loss_head", head, [x1, y2, tgt], [md(i, 5), _row(gains[i, 3])],
                                        [(D_MODEL, F32)], [128])

    small = {}
    big = {}
    landed = {}

    def keep(gm, layer):
        for n, g in gm.items():
            if n in BIG:
                g = g[None] if layer is None else g
                big[(n, layer)] = _to_shards(g, SHARD_AXIS[n] - (0 if layer is None else 1)).astype(BF16)
            else:
                small.setdefault(n, {})[layer or 0] = g

    d_mod = [None] * DEPTH
    d_gain = [None] * DEPTH
    dx = dxf
    for i in reversed(range(DEPTH)):
        kind, j = i % 3, i // 3
        sv = saved[i]
        def post2_bwd(d, yv, gt, g):
            dyv, d_gt, d_g = _post_bwd(d, yv, gt, g)
            return (dyv,), (d_gt, d_g)

        (dy2,), (d_gtf, d_g3) = _rowk(f"post2_bwd{i}", post2_bwd, [dx, sv['y2']], [md(i, 5), _row(gains[i, 3])],
                                      [(D_MODEL, BF16)], [D_MODEL] * 2)
        da = _mm(f"ffn2_dg{i}", dy2, G[('ffn_w2', i)], 'nt', S, D_FF, D_MODEL, tm=2048, b_spec=by_j, extras=[sv['a']],
                 extra_specs=[pl.BlockSpec((min(2048, S), 1024), lambda i_, j_, k_: (i_, j_))],
                 epi=lambda acc, av: (acc * (2.0 * jnp.maximum(av, 0.0)),),
                 outs=[jax.ShapeDtypeStruct((S, D_FF), BF16)])
        big[('ffn_w2', i)] = _mm(f"ffn2_wg{i}", sv['a'], dy2, 'tn', D_FF, D_MODEL, S, tk=2048, pro_a=sq_relu,
                        outs=[jax.ShapeDtypeStruct((4, 1024, D_MODEL), BF16)],
                        out_specs=[pl.BlockSpec((None, 1024, 1024), lambda i_, j_, k_: (i_, 0, j_))])
        big[('ffn_w1', i)] = _mm(f"ffn1_wg{i}", sv['h2'], da, 'tn', D_MODEL, D_FF, S, tk=2048,
                        outs=[jax.ShapeDtypeStruct((4, D_MODEL, 1024), BF16)],
                        out_specs=[pl.BlockSpec((None, 1024, 1024), lambda i_, j_, k_: (j_, i_, 0))])
        dh2 = _mm(f"ffn1_dg{i}", da, G[('ffn_w1', i)], 'nt', S, D_MODEL, D_FF, tm=2048, b_spec=by_k)
        if i == DEPTH - 1:
            keys_a = [('ffn_w1', i), ('ffn_w2', i)]
            fly_a = _split_start("scatter_ga_start", [big[k] for k in keys_a], src_by='xy')
            mod = mod + fly_a['token'][0, 0]
        if i == 0:
            keys_b = [k for k in big if k not in keys_a]
            fly_b = _split_start("scatter_gb_start", [big[k] for k in keys_b], src_by='xy')
            mod = mod + fly_b['token'][0, 0]

        def mid_bwd(d2, dh2v, x1v, yv, g2, scf, gtm, g1):
            dpre, d_sh, d_sc, d_g2 = _pre_bwd(dh2v, x1v, g2, scf)
            d1 = d2 + dpre
            dyv, d_gt, d_g1 = _post_bwd(d1, yv, gtm, g1)
            return (d1, dyv), (d_sh, d_sc, d_g2, d_gt, d_g1, _colsum(dyv))

        ydt = F32 if kind == 2 else BF16
        (dx1, dy), (d_shf, d_scf, d_g2, d_gtm, d_g1, dy_cs) = _rowk(
            f"mid_bwd{i}", mid_bwd, [dx, dh2, sv['x1'], sv['y']],
            [_row(gains[i, 2]), md(i, 4), md(i, 2), _row(gains[i, 1])],
            [(D_MODEL, F32), (D_MODEL, ydt)], [D_MODEL] * 6)
        if kind == 0:
            dh, gm = _mla_bwd(j, dy, sv['h'], sv['mix'], mla_p[j], rope)
            keep(gm, j)
        elif kind == 1:
            dh, gm = _conv_bwd(dy, dy_cs, sv['h'], sv['mix'], conv_p)
            keep(gm, None)
        else:
            dh, gm = _pool_mixer_bwd(dy, sv['mix'], pool_p)
            keep(gm, None)

        def pre_bwd(d1, dhv, x0v, g0, scm):
            dpre, d_sh, d_sc, d_g0 = _pre_bwd(dhv, x0v, g0, scm)
            return (d1 + dpre,), (d_sh, d_sc, d_g0)

        (dx,), (d_shm, d_scm, d_g0) = _rowk(f"pre_bwd{i}", pre_bwd, [dx1, dh, sv['x0']],
                                            [_row(gains[i, 0]), md(i, 1)], [(D_MODEL, F32)], [D_MODEL] * 3)
        d_mod[i] = jnp.concatenate([d_shm, d_scm, d_gtm, d_shf, d_scf, d_gtf], axis=1).reshape(-1)
        d_gain[i] = jnp.concatenate([d_g0, d_g1, d_g2, d_g3], axis=0)
        if i == DEPTH - 1:
            landed.update(zip(keys_a, _split_wait("scatter_ga_wait", fly_a, dx)))
    landed.update(zip(keys_b, _split_wait("scatter_gb_wait", fly_b, dx)))
    keys_c = [(n, 0) for n in MLA_MATS]
    fly_c = _split_start("scatter_gc_start", [big[k] for k in keys_c], src_by='xy')
    grad_x = dx.reshape(x.shape)
    grads = {n: jnp.stack([g[l] for l in sorted(g)]) for n, g in small.items()}
    grads['norm_g'] = jnp.stack(d_gain)
    grads['ada_b'] = jnp.stack(d_mod)

    pack = jnp.concatenate([grads[n].reshape(-1) for n in SMALL] + [loss_acc[0, :1]])
    n_pack = pack.shape[0]
    rows = -(-n_pack // 1024) * 8
    pack = jnp.pad(pack, (0, rows * 128 - n_pack)).reshape(rows, 128)
    fly_s = _split_start("gather_small_start", [pack], group='xyc')

    out_g, out_d, out_m, out_v = {}, {}, {}, {}
    chains = {}
    for n in BIG + ['ffn_w1', 'ffn_w2']:
        if n in MLA_MATS:
            chains[n] = _finish(f"finish_{n}1", W[n], landed[(n, 1)], MOM[n], VAR[n], layer=1)
        elif n in BIG:
            out_g[n], out_d[n], out_m[n], out_v[n] = _finish(f"finish_{n}", W[n], landed[(n, None)], MOM[n], VAR[n])
        else:
            res = None
            for l in range(DEPTH):
                res = _finish(f"finish_{n}{l}", W[n], landed[(n, l)], MOM[n], VAR[n], layer=l, prev=res)
            out_g[n], out_d[n], out_m[n], out_v[n] = res
    landed.update(zip(keys_c, _split_wait("scatter_gc_wait", fly_c, out_g['ffn_w2'])))
    for n in MLA_MATS:
        out_g[n], out_d[n], out_m[n], out_v[n] = _finish(f"finish_{n}0", W[n], landed[(n, 0)], MOM[n], VAR[n],
                                                         layer=0, prev=chains[n])
    pack8 = _split_wait("gather_small_wait", fly_s, out_g['mla_w_o'])[0]
    (tot,) = _ew("sum_small", lambda *v: (functools.reduce(lambda p, q: p + q, v),), [(pack8, s) for s in range(8)],
                 [F32], (rows, 128))
    tot = tot.reshape(-1)
    loss = tot[n_pack - 1]
    d_mod_all = pack8.reshape(8, -1)[:, :DEPTH * 6 * D_MODEL].reshape(8, DEPTH, 6 * D_MODEL)
    final = {}
    off = 0
    for n in SMALL:
        ax = SHARD_AXIS[n]
        shape = tuple(d * 4 if k == ax else d for k, d in enumerate(W[n].shape))
        size = grads[n].size
        g = tot[off:off + size].reshape(shape)
        off += size
        if ax is not None:
            g = lax.dynamic_index_in_dim(_to_shards(g, ax), chip, 0, keepdims=False)
        final[n] = g

    g_ada = []
    for l in range(DEPTH):
        dm_l = jnp.pad(lax.dynamic_slice(d_mod_all[:, l], (0, chip * n_sh), (8, n_sh)), ((0, ADA_ROWS - 8), (0, 0)))
        g_ada.append(_mm(f"ada_wg{l}", c8, dm_l, 'tn', D_MODEL, n_sh, ADA_ROWS, tn=n_sh // 2, pro_a=silu))
    final['ada_w'] = jnp.stack(g_ada)

    for n in WEIGHTS:
        if n in out_g:
            continue
        shape = W[n].shape
        out_g[n] = final[n].reshape(shape)
        out_d[n], out_m[n], out_v[n] = _ew(f"adamw_{n}", lambda w, g, m, v: _adamw(w, g, m, v),
                                           [W[n], out_g[n], MOM[n], VAR[n]], [F32] * 3, shape)
    return (loss, grad_x, *[out_g[n] for n in WEIGHTS], *[out_d[n] for n in WEIGHTS],
            *[out_m[n] for n in WEIGHTS], *[out_v[n] for n in WEIGHTS])
```

```python
import functools
import math

import jax
import jax.numpy as jnp
from jax import lax
from jax.experimental import pallas as pl
from jax.experimental.pallas import tpu as pltpu

F32 = jnp.float32
BF16 = jnp.bfloat16

D_MODEL = 1024
DEPTH = 4
N_HEADS = 16
QK_NOPE = 64
QK_ROPE = 32
V_HEAD = 64
Q_LORA = 384
KV_LORA = 256
HEAD_PAD = 128
QW = N_HEADS * HEAD_PAD
KVW = 2 * QW
DKV = KV_LORA + QK_ROPE
DQKV = Q_LORA + DKV
D_FF = 4096
CONV_WIDTH = 31
POOL_WINDOWS = (2, 4, 8, 16)
CHUNK_SHIFT = 6
ROPE_THETA = 10000.0
NORM_EPS = 1e-6
NEG_INF = -1e30
ATT_SCALE = 1.0 / math.sqrt(QK_NOPE + QK_ROPE)
BQ = 256
HB = 8
HF = 8
LOG2E = 1.4426950408889634
SCALE_LOG2E = ATT_SCALE * LOG2E
PAD_ROWS = 32
ADA_ROWS = 128
VMEM_LIMIT = 56 * 1024 * 1024

ADAM_LR = 0.001
ADAM_B1 = 0.9
ADAM_B2 = 0.999
ADAM_EPS = 1e-08
ADAM_WD = 0.01
ADAM_STEP = 10

WEIGHTS = ['ada_w', 'ada_b', 'norm_g', 'mla_w_dq', 'mla_q_norm_g', 'mla_w_uq', 'mla_w_dkv', 'mla_kv_norm_g',
           'mla_w_ukv', 'mla_w_o', 'conv_w_pw1', 'conv_b_pw1', 'conv_w_dw', 'conv_b_dw', 'conv_ln_g', 'conv_ln_b',
           'conv_w_pw2', 'conv_b_pw2', 'pool_w', 'pool_b', 'pool_scale', 'ffn_w1', 'ffn_w2']
SHARD_AXIS = {'ada_w': 2, 'ada_b': None, 'norm_g': 2, 'mla_w_dq': 1, 'mla_q_norm_g': 1, 'mla_w_uq': 2,
              'mla_w_dkv': 1, 'mla_kv_norm_g': 1, 'mla_w_ukv': 2, 'mla_w_o': 1, 'conv_w_pw1': 2,
              'conv_b_pw1': None, 'conv_w_dw': 2, 'conv_b_dw': None, 'conv_ln_g': None, 'conv_ln_b': None,
              'conv_w_pw2': 1, 'conv_b_pw2': None, 'pool_w': 2, 'pool_b': 2, 'pool_scale': 1,
              'ffn_w1': 2, 'ffn_w2': 1}
MLA_MATS = ['mla_w_dq', 'mla_w_uq', 'mla_w_dkv', 'mla_w_ukv', 'mla_w_o']
BIG = MLA_MATS + ['conv_w_pw1', 'conv_w_pw2', 'pool_w']
SMALL = ['ada_b', 'norm_g', 'mla_q_norm_g', 'mla_kv_norm_g', 'conv_b_pw1', 'conv_w_dw', 'conv_b_dw',
         'conv_ln_g', 'conv_ln_b', 'conv_b_pw2', 'pool_b', 'pool_scale']


def _cparams(*sem):
    return pltpu.CompilerParams(dimension_semantics=sem, vmem_limit_bytes=VMEM_LIMIT)


def _colsum(v):
    return jnp.sum(v, axis=0, keepdims=True)


def _rowmean(v):
    return jnp.mean(v, axis=-1, keepdims=True)


def _sigmoid(v):
    return 1.0 / (1.0 + jnp.exp(-v))


def _rowk(name, fn, rows, bcast, out_row, out_acc, tm=512):
    S = rows[0].shape[0]
    tm = min(tm, S)
    while S % tm:
        tm //= 2
    nin, no, na = len(rows) + len(bcast), len(out_row), len(out_acc)

    def body(*refs):
        vals = [r[...] for r in refs[:nin]]
        outs = refs[nin:nin + no]
        accs = refs[nin + no:]
        ro, ao = fn(*vals)
        for r, v in zip(outs, ro):
            r[...] = v.astype(r.dtype)
        if na:
            @pl.when(pl.program_id(0) == 0)
            def _():
                for r in accs:
                    r[...] = jnp.zeros(r.shape, r.dtype)
            for r, v in zip(accs, ao):
                r[...] += v

    in_specs = [pl.BlockSpec((tm, a.shape[1]), lambda i: (i, 0)) for a in rows]
    in_specs += [pl.BlockSpec(b.shape, lambda i, n=b.ndim: (0,) * n) for b in bcast]
    out_shape = [jax.ShapeDtypeStruct((S, w), dt) for w, dt in out_row]
    out_shape += [jax.ShapeDtypeStruct((1, w), F32) for w in out_acc]
    out_specs = [pl.BlockSpec((tm, w), lambda i: (i, 0)) for w, _ in out_row]
    out_specs += [pl.BlockSpec((1, w), lambda i: (0, 0)) for w in out_acc]
    res = pl.pallas_call(body, name=name, grid=(S // tm,), in_specs=in_specs, out_specs=out_specs,
                         out_shape=out_shape, compiler_params=_cparams("arbitrary"))(*rows, *bcast)
    return list(res[:no]), list(res[no:])


_DIMS = {'nn': ((1,), (0,)), 'nt': ((1,), (1,)), 'tn': ((0,), (0,))}


def _mm(name, a, b, mode, M, N, K, *, tm=1024, tn=1024, tk=1024, a_spec=None, b_spec=None, pro_a=None,
        extras=(), extra_specs=(), epi=None, outs=None, out_specs=None):
    tm, tn, tk = (t if d % t == 0 else d for t, d in ((min(tm, M), M), (min(tn, N), N), (min(tk, K), K)))
    nk = K // tk
    if a_spec is None:
        a_spec = (pl.BlockSpec((tk, tm), lambda i, j, k: (k, i)) if mode == 'tn'
                  else pl.BlockSpec((tm, tk), lambda i, j, k: (i, k)))
    if b_spec is None:
        b_spec = (pl.BlockSpec((tn, tk), lambda i, j, k: (j, k)) if mode == 'nt'
                  else pl.BlockSpec((tk, tn), lambda i, j, k: (k, j)))
    if outs is None:
        outs = [jax.ShapeDtypeStruct((M, N), F32)]
    if out_specs is None:
        out_specs = [pl.BlockSpec((tm, tn), lambda i, j, k: (i, j)) for _ in outs]
    ne, no = len(extras), len(outs)
    dims = (_DIMS[mode], ((), ()))

    def body(a_ref, b_ref, *rest):
        ex, out_refs = rest[:ne], rest[ne:ne + no]
        av = a_ref[...]
        if pro_a is not None:
            av = pro_a(av)
        part = lax.dot_general(av.astype(BF16), b_ref[...].astype(BF16), dims, preferred_element_type=F32)

        def finish(acc):
            vals = (acc,) if epi is None else epi(acc, *[e[...] for e in ex])
            for r, v in zip(out_refs, vals):
                r[...] = v.astype(r.dtype)

        if nk == 1:
            finish(part)
            return
        acc_ref = rest[ne + no]
        k = pl.program_id(2)

        @pl.when(k == 0)
        def _():
            acc_ref[...] = part

        @pl.when(k > 0)
        def _():
            acc_ref[...] += part

        @pl.when(k == nk - 1)
        def _():
            finish(acc_ref[...])

    res = pl.pallas_call(
        body, name=name, grid=(M // tm, N // tn, nk),
        in_specs=[a_spec, b_spec, *extra_specs], out_specs=list(out_specs), out_shape=list(outs),
        scratch_shapes=[pltpu.VMEM((tm, tn), F32)] if nk > 1 else [],
        compiler_params=_cparams("parallel", "parallel", "arbitrary"))(a, b, *extras)
    return res[0] if no == 1 else list(res)


def _row_tile(R, C, itemsize=4, budget=1 << 20):
    if R * C * itemsize <= budget or R % 8:
        return R
    t = 8
    while R % (t * 2) == 0 and t * 2 * C * itemsize <= budget:
        t *= 2
    return t


def _ew(name, fn, ins, out_dtypes, shape):
    C = shape[-1]
    R = 1
    for s in shape[:-1]:
        R *= s
    tr = _row_tile(R, C)
    ops, specs = [], []
    for it in ins:
        if isinstance(it, tuple):
            arr, idx = it
            ops.append(arr.reshape(arr.shape[0], R, C))
            specs.append(pl.BlockSpec((None, tr, C), lambda i, n=idx: (n, i, 0)))
        else:
            ops.append(it.reshape(R, C))
            specs.append(pl.BlockSpec((tr, C), lambda i: (i, 0)))
    nin = len(ops)

    def body(*refs):
        vals = fn(*[r[...] for r in refs[:nin]])
        for r, v in zip(refs[nin:], vals):
            r[...] = v.astype(r.dtype)

    res = pl.pallas_call(
        body, name=name, grid=(R // tr,), in_specs=specs,
        out_specs=[pl.BlockSpec((tr, C), lambda i: (i, 0)) for _ in out_dtypes],
        out_shape=[jax.ShapeDtypeStruct((R, C), dt) for dt in out_dtypes],
        compiler_params=_cparams("parallel"))(*ops)
    return [r.reshape(shape) for r in res]


_FLIPS = {'xyc': [(fx, fy, fc) for fx in (0, 1) for fy in (0, 1) for fc in (0, 1)][1:],
          'xy': [(1, 0, 0), (0, 1, 0), (1, 1, 0)],
          'c': [(0, 0, 1)]}
_NSLOT = {'xyc': 8, 'xy': 4, 'c': 2}


def _slot(kind, cx, cy, cc):
    return {'xyc': 4 * cx + 2 * cy + cc, 'xy': 2 * cx + cy, 'c': cc}[kind]


def _put_own(land, arr, group, src_by):
    coords = (lax.axis_index("x"), lax.axis_index("y"), lax.axis_index("c"))
    pay = arr if src_by is None else lax.dynamic_index_in_dim(arr, _slot(src_by, *coords), 0, keepdims=False)
    return lax.dynamic_update_index_in_dim(land, pay, _slot(group, *coords), 0)


def _exchange(name, arrays, group, src_by=None):
    flips, nsl, n = _FLIPS[group], _NSLOT[group], len(arrays)
    nf = len(flips)

    def body(*refs):
        ins, outs = refs[:n], refs[n:2 * n]
        send_sems, recv_sems = refs[2 * n:]
        mx, my, mc = lax.axis_index("x"), lax.axis_index("y"), lax.axis_index("c")
        me = _slot(group, mx, my, mc)

        def payload(a, cx, cy, cc):
            return ins[a] if src_by is None else ins[a].at[_slot(src_by, cx, cy, cc)]

        sends, recvs = [], []
        for a in range(n):
            for f, (fx, fy, fc) in enumerate(flips):
                px = 1 - mx if fx else mx
                py = 1 - my if fy else my
                pc = 1 - mc if fc else mc
                src = payload(a, px, py, pc)
                sends.append(pltpu.make_async_remote_copy(
                    src_ref=src, dst_ref=outs[a].at[me], send_sem=send_sems.at[a, f],
                    recv_sem=recv_sems.at[a, f], device_id=(px, py, pc),
                    device_id_type=pl.DeviceIdType.MESH))
                recvs.append(pltpu.make_async_remote_copy(
                    src_ref=src, dst_ref=outs[a].at[_slot(group, px, py, pc)], send_sem=send_sems.at[a, f],
                    recv_sem=recv_sems.at[a, f], device_id=(px, py, pc),
                    device_id_type=pl.DeviceIdType.MESH))
        for cp in sends:
            cp.start()
        for cp in recvs:
            cp.wait_recv()
        for cp in sends:
            cp.wait_send()

    out_shape = [jax.ShapeDtypeStruct((nsl,) + (a.shape if src_by is None else a.shape[1:]), a.dtype)
                 for a in arrays]
    any_spec = pl.BlockSpec(memory_space=pl.ANY)
    res = pl.pallas_call(
        body, name=name, in_specs=[any_spec] * n, out_specs=[any_spec] * n, out_shape=out_shape,
        scratch_shapes=[pltpu.SemaphoreType.DMA((n, nf)), pltpu.SemaphoreType.DMA((n, nf))],
        compiler_params=pltpu.CompilerParams(has_side_effects=True))(*arrays)
    return [_put_own(l, a, group, src_by) for a, l in zip(arrays, res)]


_HBM = pl.BlockSpec(memory_space=pltpu.HBM)
_SEM = pl.BlockSpec(memory_space=pltpu.SEMAPHORE)
_DATAFLOW = pltpu.SideEffectType.DATAFLOW_SIDE_EFFECTING


def _group_copies(ins, lands, send_sems, recv_sems, group, src_by):
    mx, my, mc = lax.axis_index("x"), lax.axis_index("y"), lax.axis_index("c")
    me = _slot(group, mx, my, mc)
    pairs = []
    for a in range(len(ins)):
        for fx, fy, fc in _FLIPS[group]:
            peer = (1 - mx if fx else mx, 1 - my if fy else my, 1 - mc if fc else mc)
            src = ins[a] if src_by is None else ins[a].at[_slot(src_by, *peer)]
            mk = functools.partial(pltpu.make_async_remote_copy, src_ref=src, send_sem=send_sems,
                                   recv_sem=recv_sems, device_id=peer, device_id_type=pl.DeviceIdType.MESH)
            pairs.append((mk(dst_ref=lands[a].at[me]), mk(dst_ref=lands[a].at[_slot(group, *peer)])))
    return pairs


def _split_start(name, arrays, group='xy', src_by=None):
    n = len(arrays)
    lands = [lax.empty((_NSLOT[group],) + (a.shape if src_by is None else a.shape[1:]), a.dtype) for a in arrays]

    def body(*refs):
        ins, lnd, send_sems, recv_sems, token = refs[:n], refs[n:2 * n], refs[2 * n], refs[2 * n + 1], refs[-1]
        for to_peer, _ in _group_copies(ins, lnd, send_sems, recv_sems, group, src_by):
            to_peer.start()
        token[...] = jnp.zeros(token.shape, F32)

    ops = [pltpu.with_memory_space_constraint(a, pltpu.HBM) for a in [*arrays, *lands]]
    res = pl.pallas_call(
        body, name=name, in_specs=[_HBM] * (2 * n),
        out_specs=[_SEM, _SEM] + [_HBM] * (2 * n) + [pl.BlockSpec(memory_space=pltpu.VMEM)],
        out_shape=[pltpu.SemaphoreType.DMA(()), pltpu.SemaphoreType.DMA(())]
        + [pltpu.HBM(a.shape, a.dtype) for a in ops] + [jax.ShapeDtypeStruct((8, 128), F32)],
        input_output_aliases={k: 2 + k for k in range(2 * n)},
        compiler_params=pltpu.CompilerParams(has_side_effects=_DATAFLOW))(*ops)
    return dict(n=n, group=group, src_by=src_by, send=res[0], recv=res[1], arrays=list(res[2:2 + n]),
                lands=list(res[2 + n:2 + 2 * n]), token=res[-1])


def _split_wait(name, st, after):
    n, group, src_by = st['n'], st['group'], st['src_by']

    def wait_body(*refs):
        ins, lnd, send_sems, recv_sems = refs[:n], refs[n:2 * n], refs[2 * n], refs[2 * n + 1]
        for to_peer, from_peer in _group_copies(ins, lnd, send_sems, recv_sems, group, src_by):
            to_peer.wait_send()
            from_peer.wait_recv()

    shapes = [pltpu.HBM(a.shape, a.dtype) for a in [*st['arrays'], *st['lands']]]
    res = pl.pallas_call(
        wait_body, name=name, in_specs=[_HBM] * (2 * n) + [_SEM, _SEM, pl.BlockSpec(memory_space=pl.ANY)],
        out_specs=[_HBM] * (2 * n), out_shape=shapes, input_output_aliases={k: k for k in range(2 * n)},
        compiler_params=pltpu.CompilerParams(has_side_effects=_DATAFLOW))(
            *st['arrays'], *st['lands'], st['send'], st['recv'], after)
    return [_put_own(l, a, group, src_by) for a, l in zip(res[:n], res[n:])]


def _unshard(g, axis):
    t = jnp.moveaxis(g, 0, axis)
    s = t.shape
    return t.reshape(s[:axis] + (s[axis] * s[axis + 1],) + s[axis + 2:])


def _to_shards(w, axis):
    s = w.shape
    t = w.reshape(s[:axis] + (4, s[axis] // 4) + s[axis + 1:])
    return jnp.moveaxis(t, axis, 0)


def _pre_fwd(x, g, sc, sh):
    r = lax.rsqrt(_rowmean(x * x) + NORM_EPS)
    return (x * r) * g * (1.0 + sc) + sh


def _pre_bwd(dh, x, g, sc):
    r = lax.rsqrt(_rowmean(x * x) + NORM_EPS)
    xn = x * r
    dxn = dh * (g * (1.0 + sc))
    dx = r * (dxn - xn * _rowmean(dxn * xn))
    t = dh * xn
    return dx, _colsum(dh), _colsum(t * g), _colsum(t * (1.0 + sc))


def _post_fwd(x, y, gt, g):
    r = lax.rsqrt(_rowmean(y * y) + NORM_EPS)
    return x + gt * ((y * r) * g)


def _post_bwd(dxo, y, gt, g):
    r = lax.rsqrt(_rowmean(y * y) + NORM_EPS)
    yn = y * r
    t = dxo * yn
    dyn = dxo * (gt * g)
    dy = r * (dyn - yn * _rowmean(dyn * yn))
    return dy, _colsum(t * g), _colsum(t * gt)


def _gain_bwd(dy, x, g):
    r = lax.rsqrt(_rowmean(x * x) + NORM_EPS)
    xn = x * r
    dxn = dy * g
    return r * (dxn - xn * _rowmean(dxn * xn)), _colsum(dy * xn)


def _rope(x, cos, sa, sb):
    return x * cos + pltpu.roll(x, HEAD_PAD - 16, 1) * sa + pltpu.roll(x, 16, 1) * sb


def _rope_t(d, cos, sa, sb):
    return d * cos + pltpu.roll(d * sa, 16, 1) + pltpu.roll(d * sb, HEAD_PAD - 16, 1)


def _rope_tables(pos_f):
    S = pos_f.shape[0]
    inv = ROPE_THETA ** (-jnp.arange(0, QK_ROPE, 2, dtype=F32) / QK_ROPE)
    inv_ext = jnp.concatenate([jnp.zeros((QK_NOPE,), F32), inv, inv,
                               jnp.zeros((HEAD_PAD - QK_NOPE - QK_ROPE,), F32)]).reshape(1, HEAD_PAD)

    def fn(p, iv):
        ang = p * iv
        lane = lax.broadcasted_iota(jnp.int32, ang.shape, 1)
        s = jnp.sin(ang)
        first = (lane >= QK_NOPE) & (lane < QK_NOPE + QK_ROPE // 2)
        second = (lane >= QK_NOPE + QK_ROPE // 2) & (lane < QK_NOPE + QK_ROPE)
        return (jnp.cos(ang), jnp.where(first, -s, 0.0), jnp.where(second, s, 0.0)), ()

    (cos, sa, sb), _ = _rowk("rope_tables", fn, [pos_f], [inv_ext], [(HEAD_PAD, F32)] * 3, [])
    return cos, sa, sb


def _diag_mask(transposed):
    r = lax.broadcasted_iota(jnp.int32, (BQ, BQ), 0) >> CHUNK_SHIFT
    c = lax.broadcasted_iota(jnp.int32, (BQ, BQ), 1) >> CHUNK_SHIFT
    return (r <= c) if transposed else (c <= r)


_NT = (((1,), (1,)), ((), ()))
_NN = (((1,), (0,)), ((), ()))


def _attn_fwd(qf, kvf, after, HB=HF):
    S = qf.shape[0]
    nq = S // BQ

    def body(q_ref, kv_ref, after_ref, o_ref, ob_ref, lse_ref):
        qi = pl.program_id(1)
        qs = [q_ref[:, hh * HEAD_PAD:(hh + 1) * HEAD_PAD] for hh in range(HB)]

        def step(j, carry, diag):
            off = pl.multiple_of(j * BQ, BQ)
            sts = [lax.dot_general(kv_ref[pl.ds(off, BQ), pl.ds(2 * hh * HEAD_PAD, HEAD_PAD)], qs[hh], _NT,
                                   preferred_element_type=F32) for hh in range(HB)]
            mid = []
            for hh in range(HB):
                m, l, acc = carry[hh]
                st = jnp.where(_diag_mask(True), sts[hh], NEG_INF) if diag else sts[hh]
                m2 = jnp.maximum(m, jnp.max(st, axis=0, keepdims=True))
                al = jnp.exp2(m - m2)
                pt = jnp.exp2(st - m2)
                mid.append((m2, l * al + jnp.sum(pt, axis=0, keepdims=True), acc * al, pt.astype(BF16)))
            out = []
            for hh in range(HB):
                m2, l2, acc_s, ptb = mid[hh]
                v = kv_ref[pl.ds(off, BQ), pl.ds((2 * hh + 1) * HEAD_PAD, HEAD_PAD)]
                out.append((m2, l2, acc_s + lax.dot_general(v, ptb, _TN, preferred_element_type=F32)))
            return tuple(out)

        init = tuple((jnp.full((1, BQ), NEG_INF, F32), jnp.zeros((1, BQ), F32), jnp.zeros((HEAD_PAD, BQ), F32))
                     for _ in range(HB))
        carry = lax.fori_loop(0, qi, lambda j, c: step(j, c, False), init)
        carry = step(qi, carry, True)
        for hh in range(HB):
            m, l, acc = carry[hh]
            ov = (acc / l).T
            o_ref[:, hh * HEAD_PAD:(hh + 1) * HEAD_PAD] = ov
            ob_ref[:, hh * HEAD_PAD:(hh + 1) * HEAD_PAD] = ov.astype(BF16)
            lse_ref[hh] = m + jnp.log(l) * LOG2E

    return pl.pallas_call(
        body, name="attn_fwd", grid=(N_HEADS // HB, nq),
        in_specs=[pl.BlockSpec((BQ, HB * HEAD_PAD), lambda g, i: (i, g)),
                  pl.BlockSpec((S, 2 * HB * HEAD_PAD), lambda g, i: (0, g)),
                  pl.BlockSpec(after.shape, lambda g, i: (0, 0))],
        out_specs=[pl.BlockSpec((BQ, HB * HEAD_PAD), lambda g, i: (i, g)),
                   pl.BlockSpec((BQ, HB * HEAD_PAD), lambda g, i: (i, g)),
                   pl.BlockSpec((HB, None, 1, BQ), lambda g, i: (g, i, 0, 0))],
        out_shape=[jax.ShapeDtypeStruct((S, QW), F32), jax.ShapeDtypeStruct((S, QW), BF16),
                   jax.ShapeDtypeStruct((N_HEADS, nq, 1, BQ), F32)],
        compiler_params=_cparams("parallel", "arbitrary"))(qf, kvf, after)


def _attn_delta(dob, o):
    S = o.shape[0]

    def fn(dov, ov):
        prod = dov.astype(F32) * ov
        lane = lax.broadcasted_iota(jnp.int32, (prod.shape[0], HEAD_PAD), 1)
        out = jnp.zeros((prod.shape[0], HEAD_PAD), F32)
        for h in range(N_HEADS):
            out = jnp.where(lane == h, jnp.sum(prod[:, h * HEAD_PAD:(h + 1) * HEAD_PAD], axis=1, keepdims=True), out)
        return (out,), ()

    (dd,), _ = _rowk("attn_delta", fn, [dob, o], [], [(HEAD_PAD, F32)], [])
    return dd[:, :N_HEADS].T.reshape(N_HEADS, S // BQ, 1, BQ)


_TN = (((0,), (0,)), ((), ()))


def _attn_bwd(qf, kvf, dob, lse_row, dd_row, cos, sa, sb):
    S = qf.shape[0]
    nq = S // BQ

    def body(kv_ref, q_ref, do_ref, lse_ref, dd_ref, cos_ref, sa_ref, sb_ref, dqo_ref, dkv_ref, dq_ref):
        kj = pl.program_id(1)

        @pl.when(kj == 0)
        def _():
            dq_ref[...] = jnp.zeros(dq_ref.shape, F32)

        ks = [kv_ref[:, 2 * hh * HEAD_PAD:(2 * hh + 1) * HEAD_PAD] for hh in range(HB)]
        vs = [kv_ref[:, (2 * hh + 1) * HEAD_PAD:(2 * hh + 2) * HEAD_PAD] for hh in range(HB)]

        def step(i, carry, diag):
            off = pl.multiple_of(i * BQ, BQ)
            cols = [pl.ds(hh * HEAD_PAD, HEAD_PAD) for hh in range(HB)]
            q = [q_ref[pl.ds(off, BQ), cols[hh]] for hh in range(HB)]
            do = [do_ref[pl.ds(off, BQ), cols[hh]] for hh in range(HB)]
            sts = [lax.dot_general(ks[hh], q[hh], _NT, preferred_element_type=F32) for hh in range(HB)]
            dpts = [lax.dot_general(vs[hh], do[hh], _NT, preferred_element_type=F32) for hh in range(HB)]
            mid = []
            for hh in range(HB):
                st = jnp.where(_diag_mask(True), sts[hh], NEG_INF) if diag else sts[hh]
                pt = jnp.exp2(st - lse_ref[hh, i])
                mid.append((pt.astype(BF16), (pt * (dpts[hh] - dd_ref[hh, i])).astype(BF16)))
            out = []
            for hh in range(HB):
                dk, dv = carry[hh]
                ptb, dsb = mid[hh]
                dv2 = dv + lax.dot_general(ptb, do[hh], _NN, preferred_element_type=F32)
                dk2 = dk + lax.dot_general(dsb, q[hh], _NN, preferred_element_type=F32)
                dq_ref[pl.ds(off, BQ), cols[hh]] += lax.dot_general(dsb, ks[hh], _TN, preferred_element_type=F32)
                out.append((dk2, dv2))
            return tuple(out)

        zero = jnp.zeros((BQ, HEAD_PAD), F32)
        carry = step(kj, tuple((zero, zero) for _ in range(HB)), True)
        carry = lax.fori_loop(kj + 1, nq, lambda i, c: step(i, c, False), carry)
        done = pl.ds(pl.multiple_of(kj * BQ, BQ), BQ)
        for hh in range(HB):
            dk, dv = carry[hh]
            dk = _rope_t(dk * (1.0 / LOG2E), cos_ref[...], sa_ref[...], sb_ref[...])
            dkv_ref[:, 2 * hh * HEAD_PAD:(2 * hh + 1) * HEAD_PAD] = dk.astype(BF16)
            dkv_ref[:, (2 * hh + 1) * HEAD_PAD:(2 * hh + 2) * HEAD_PAD] = dv.astype(BF16)
            cols = pl.ds(hh * HEAD_PAD, HEAD_PAD)
            dqo_ref[:, cols] = _rope_t(dq_ref[done, cols] * ATT_SCALE, cos_ref[...], sa_ref[...],
                                       sb_ref[...]).astype(BF16)

    tab = pl.BlockSpec((BQ, HEAD_PAD), lambda g, j: (j, 0))
    row = pl.BlockSpec((HB, nq, 1, BQ), lambda g, j: (g, 0, 0, 0))
    seq = pl.BlockSpec((S, HB * HEAD_PAD), lambda g, j: (0, g), pipeline_mode=pl.Buffered(1))
    kvb = pl.BlockSpec((BQ, 2 * HB * HEAD_PAD), lambda g, j: (j, g))
    return pl.pallas_call(
        body, name="attn_bwd", grid=(N_HEADS // HB, nq),
        in_specs=[kvb, seq, seq, row, row, tab, tab, tab],
        out_specs=[pl.BlockSpec((BQ, HB * HEAD_PAD), lambda g, j: (j, g)), kvb],
        out_shape=[jax.ShapeDtypeStruct((S, QW), BF16), jax.ShapeDtypeStruct((S, KVW), BF16)],
        scratch_shapes=[pltpu.VMEM((S, HB * HEAD_PAD), F32)],
        compiler_params=_cparams("parallel", "arbitrary"))(kvf, qf, dob, lse_row, dd_row, cos, sa, sb)


DC = 128
TR = 256


def _dwconv_fwd(u, w, b):
    S, Dm = u.shape
    tr = min(TR, S)

    def body(u_ref, w_ref, b_ref, o_ref, pad_ref):
        pad_ref[pl.ds(0, PAD_ROWS), :] = jnp.zeros((PAD_ROWS, DC), F32)
        pad_ref[pl.ds(PAD_ROWS, S), :] = u_ref[...]
        wv = w_ref[...]
        for r in range(S // tr):
            acc = jnp.broadcast_to(b_ref[...], (tr, DC))
            for j in range(CONV_WIDTH):
                acc = acc + wv[j:j + 1, :] * pad_ref[pl.ds(r * tr + PAD_ROWS - (CONV_WIDTH - 1) + j, tr), :]
            o_ref[pl.ds(r * tr, tr), :] = acc

    return pl.pallas_call(
        body, name="dwconv_fwd", grid=(Dm // DC,),
        in_specs=[pl.BlockSpec((S, DC), lambda c: (0, c)), pl.BlockSpec((CONV_WIDTH, DC), lambda c: (0, c)),
                  pl.BlockSpec((1, DC), lambda c: (0, c))],
        out_specs=pl.BlockSpec((S, DC), lambda c: (0, c)),
        out_shape=jax.ShapeDtypeStruct((S, Dm), F32),
        scratch_shapes=[pltpu.VMEM((S + PAD_ROWS, DC), F32)],
        compiler_params=_cparams("parallel"))(u, w, b)


def _dwconv_bwd(d, u, w):
    S, Dm = u.shape
    tr = min(TR, S)

    def body(d_ref, u_ref, w_ref, du_ref, dw_ref, padd_ref, padu_ref):
        padd_ref[pl.ds(0, S), :] = d_ref[...]
        padd_ref[pl.ds(S, PAD_ROWS), :] = jnp.zeros((PAD_ROWS, DC), F32)
        padu_ref[pl.ds(0, PAD_ROWS), :] = jnp.zeros((PAD_ROWS, DC), F32)
        padu_ref[pl.ds(PAD_ROWS, S), :] = u_ref[...]
        wv = w_ref[...]
        dws = [jnp.zeros((1, DC), F32) for _ in range(CONV_WIDTH)]
        for r in range(S // tr):
            acc = jnp.zeros((tr, DC), F32)
            for j in range(CONV_WIDTH):
                acc = acc + wv[j:j + 1, :] * padd_ref[pl.ds(r * tr + (CONV_WIDTH - 1) - j, tr), :]
            du_ref[pl.ds(r * tr, tr), :] = acc
            dt = d_ref[pl.ds(r * tr, tr), :]
            for j in range(CONV_WIDTH):
                ut = padu_ref[pl.ds(r * tr + PAD_ROWS - (CONV_WIDTH - 1) + j, tr), :]
                dws[j] = dws[j] + _colsum(dt * ut)
        for j in range(CONV_WIDTH):
            dw_ref[pl.ds(j, 1), :] = dws[j]
        dw_ref[pl.ds(CONV_WIDTH, 1), :] = jnp.zeros((1, DC), F32)

    blk = pl.BlockSpec((S, DC), lambda c: (0, c))
    return pl.pallas_call(
        body, name="dwconv_bwd", grid=(Dm // DC,),
        in_specs=[blk, blk, pl.BlockSpec((CONV_WIDTH, DC), lambda c: (0, c))],
        out_specs=[blk, pl.BlockSpec((PAD_ROWS, DC), lambda c: (0, c))],
        out_shape=[jax.ShapeDtypeStruct((S, Dm), F32), jax.ShapeDtypeStruct((PAD_ROWS, Dm), F32)],
        scratch_shapes=[pltpu.VMEM((S + PAD_ROWS, DC), F32), pltpu.VMEM((S + PAD_ROWS, DC), F32)],
        compiler_params=_cparams("parallel"))(d, u, w)


POOL_C = D_MODEL // len(POOL_WINDOWS)


def _pool_counts(r, tr, win):
    t = r * tr + lax.broadcasted_iota(jnp.int32, (tr, 1), 0)
    return jnp.minimum(t + 1, win).astype(F32)


def _pool_fwd(h):
    S, Dm = h.shape
    tr = min(TR, S)

    def body(h_ref, o_ref, pad_ref):
        pad_ref[pl.ds(0, PAD_ROWS), :] = jnp.zeros((PAD_ROWS, POOL_C), F32)
        pad_ref[pl.ds(PAD_ROWS, S), :] = h_ref[...]
        for g, win in enumerate(POOL_WINDOWS):
            @pl.when(pl.program_id(0) == g)
            def _():
                for r in range(S // tr):
                    acc = pad_ref[pl.ds(r * tr + PAD_ROWS, tr), :]
                    for j in range(1, win):
                        acc = acc + pad_ref[pl.ds(r * tr + PAD_ROWS - j, tr), :]
                    pooled = acc / _pool_counts(r, tr, win)
                    o_ref[pl.ds(r * tr, tr), :] = (pooled - h_ref[pl.ds(r * tr, tr), :]).astype(BF16)

    blk = pl.BlockSpec((S, POOL_C), lambda g: (0, g))
    return pl.pallas_call(
        body, name="pool_fwd", grid=(len(POOL_WINDOWS),), in_specs=[blk], out_specs=blk,
        out_shape=jax.ShapeDtypeStruct((S, Dm), BF16),
        scratch_shapes=[pltpu.VMEM((S + PAD_ROWS, POOL_C), F32)],
        compiler_params=_cparams("parallel"))(h)


def _pool_bwd(dp):
    S, Dm = dp.shape
    tr = min(TR, S)

    def body(d_ref, o_ref, pad_ref):
        pad_ref[pl.ds(S, PAD_ROWS), :] = jnp.zeros((PAD_ROWS, POOL_C), F32)
        for g, win in enumerate(POOL_WINDOWS):
            @pl.when(pl.program_id(0) == g)
            def _():
                for r in range(S // tr):
                    pad_ref[pl.ds(r * tr, tr), :] = d_ref[pl.ds(r * tr, tr), :] / _pool_counts(r, tr, win)
                for r in range(S // tr):
                    acc = pad_ref[pl.ds(r * tr, tr), :]
                    for j in range(1, win):
                        acc = acc + pad_ref[pl.ds(r * tr + j, tr), :]
                    o_ref[pl.ds(r * tr, tr), :] = acc - d_ref[pl.ds(r * tr, tr), :]

    blk = pl.BlockSpec((S, POOL_C), lambda g: (0, g))
    return pl.pallas_call(
        body, name="pool_bwd", grid=(len(POOL_WINDOWS),), in_specs=[blk], out_specs=blk,
        out_shape=jax.ShapeDtypeStruct((S, Dm), F32),
        scratch_shapes=[pltpu.VMEM((S + PAD_ROWS, POOL_C), F32)],
        compiler_params=_cparams("parallel"))(dp)


def _bias_spec(tn):
    return pl.BlockSpec((1, tn), lambda i, j, k: (0, j))


def _mla_weights(w_dq, w_dkv, w_uq, w_ukv, w_o):
    wd = jnp.concatenate([w_dq, w_dkv], axis=1)
    wq = jnp.pad(w_uq.reshape(Q_LORA, N_HEADS, QK_NOPE + QK_ROPE),
                 ((0, 0), (0, 0), (0, HEAD_PAD - QK_NOPE - QK_ROPE))).reshape(Q_LORA, QW)
    ukv = w_ukv.reshape(KV_LORA, N_HEADS, QK_NOPE + V_HEAD)
    wkv = jnp.zeros((DKV, N_HEADS, 2 * HEAD_PAD), BF16)
    wkv = wkv.at[:KV_LORA, :, :QK_NOPE].set(ukv[:, :, :QK_NOPE])
    wkv = wkv.at[:KV_LORA, :, HEAD_PAD:HEAD_PAD + V_HEAD].set(ukv[:, :, QK_NOPE:])
    eye = jnp.broadcast_to(jnp.eye(QK_ROPE, dtype=BF16)[:, None, :], (QK_ROPE, N_HEADS, QK_ROPE))
    wkv = wkv.at[KV_LORA:, :, QK_NOPE:QK_NOPE + QK_ROPE].set(eye).reshape(DKV, KVW)
    wo = jnp.pad(w_o.reshape(N_HEADS, V_HEAD, D_MODEL),
                 ((0, 0), (0, HEAD_PAD - V_HEAD), (0, 0))).reshape(QW, D_MODEL)
    return dict(wd=wd, wq=wq, wkv=wkv, wo=wo)


def _mla_weight_grads(g_wd, g_wq, g_wkv, g_wo):
    g_uq = g_wq.reshape(Q_LORA, N_HEADS, HEAD_PAD)[:, :, :QK_NOPE + QK_ROPE].reshape(Q_LORA, -1)
    t = g_wkv.reshape(DKV, N_HEADS, 2 * HEAD_PAD)[:KV_LORA]
    g_ukv = jnp.concatenate([t[:, :, :QK_NOPE], t[:, :, HEAD_PAD:HEAD_PAD + V_HEAD]], axis=2)
    g_o = g_wo.reshape(N_HEADS, HEAD_PAD, D_MODEL)[:, :V_HEAD].reshape(N_HEADS * V_HEAD, D_MODEL)
    return dict(mla_w_dq=g_wd[:, :Q_LORA], mla_w_uq=g_uq, mla_w_dkv=g_wd[:, Q_LORA:],
                mla_w_ukv=g_ukv.reshape(KV_LORA, -1), mla_w_o=g_o)


def _rope_epilogue(kv):
    def epi(acc, cos, sa, sb):
        parts = []
        for t in range(acc.shape[1] // HEAD_PAD):
            x = acc[:, t * HEAD_PAD:(t + 1) * HEAD_PAD]
            if kv:
                parts.append(x if t % 2 else _rope(x, cos, sa, sb))
            else:
                parts.append(_rope(x, cos, sa, sb) * SCALE_LOG2E)
        return (jnp.concatenate(parts, axis=1),)
    return epi


def _mla_fwd(tag, h, P, rope, before_attention=None):
    S = h.shape[0]
    cos, sa, sb = rope
    tp = min(512, S)
    tabs = [pl.BlockSpec((tp, HEAD_PAD), lambda i, j, k: (i, 0))] * 3
    cqkv = _mm(f"mla_down{tag}", h, P['wd'], 'nn', S, DQKV, D_MODEL)

    def norms(x, qg, kg):
        xq, xk, xr = x[:, :Q_LORA], x[:, Q_LORA:Q_LORA + KV_LORA], x[:, Q_LORA + KV_LORA:]
        cq = xq * lax.rsqrt(_rowmean(xq * xq) + NORM_EPS) * qg
        ck = xk * lax.rsqrt(_rowmean(xk * xk) + NORM_EPS) * kg
        return (cq, jnp.concatenate([ck, xr], axis=1)), ()

    (cq, ckv), _ = _rowk(f"mla_norms{tag}", norms, [cqkv], [P['qg'], P['kg']], [(Q_LORA, BF16), (DKV, BF16)], [])
    qf = _mm(f"mla_q{tag}", cq, P['wq'], 'nn', S, QW, Q_LORA, tm=tp, tn=QW, extras=[cos, sa, sb],
             extra_specs=tabs, epi=_rope_epilogue(False), outs=[jax.ShapeDtypeStruct((S, QW), BF16)])
    kvf = _mm(f"mla_kv{tag}", ckv, P['wkv'], 'nn', S, KVW, DKV, tm=tp, tn=KVW, extras=[cos, sa, sb],
              extra_specs=tabs, epi=_rope_epilogue(True), outs=[jax.ShapeDtypeStruct((S, KVW), BF16)])
    after = cos[:8] if before_attention is None else before_attention(kvf)
    o, ob, lse = _attn_fwd(qf, kvf, after)
    y = _mm(f"mla_o{tag}", ob, P['wo'], 'nn', S, D_MODEL, QW)
    return y, dict(cqkv=cqkv, cq=cq, ckv=ckv, qf=qf, kvf=kvf, o=o, ob=ob, lse=lse)


def _mla_bwd(tag, dy, h, sv, P, rope):
    S = h.shape[0]
    nq = S // BQ
    cos, sa, sb = rope
    g_wo = _mm(f"mla_o_wg{tag}", sv['ob'], dy, 'tn', QW, D_MODEL, S)
    dob = _mm(f"mla_o_dg{tag}", dy, P['wo'], 'nt', S, QW, D_MODEL, outs=[jax.ShapeDtypeStruct((S, QW), BF16)])
    dd = _attn_delta(dob, sv['o'])
    dq, dkv = _attn_bwd(sv['qf'], sv['kvf'], dob, sv['lse'], dd, cos, sa, sb)
    g_wq = _mm(f"mla_q_wg{tag}", sv['cq'], dq, 'tn', Q_LORA, QW, S)
    dcq = _mm(f"mla_q_dg{tag}", dq, P['wq'], 'nt', S, Q_LORA, QW)
    g_wkv = _mm(f"mla_kv_wg{tag}", sv['ckv'], dkv, 'tn', DKV, KVW, S)
    dckv = _mm(f"mla_kv_dg{tag}", dkv, P['wkv'], 'nt', S, DKV, KVW)

    def norms_bwd(dcq_v, dckv_v, x, qg, kg):
        xq, xk = x[:, :Q_LORA], x[:, Q_LORA:Q_LORA + KV_LORA]
        dxq, dqg = _gain_bwd(dcq_v, xq, qg)
        dxk, dkg = _gain_bwd(dckv_v[:, :KV_LORA], xk, kg)
        return (jnp.concatenate([dxq, dxk, dckv_v[:, KV_LORA:]], axis=1),), (dqg, dkg)

    (dcqkv,), (dqg, dkg) = _rowk(f"mla_norms_bwd{tag}", norms_bwd, [dcq, dckv, sv['cqkv']], [P['qg'], P['kg']],
                                 [(DQKV, BF16)], [Q_LORA, KV_LORA])
    g_wd = _mm(f"mla_down_wg{tag}", h, dcqkv, 'tn', D_MODEL, DQKV, S)
    dh = _mm(f"mla_down_dg{tag}", dcqkv, P['wd'], 'nt', S, D_MODEL, DQKV)
    grads = _mla_weight_grads(g_wd, g_wq, g_wkv, g_wo)
    grads.update(mla_q_norm_g=dqg.reshape(-1), mla_kv_norm_g=dkg.reshape(-1))
    return dh, grads


def _conv_fwd(h, P):
    S = h.shape[0]
    a = _mm("conv_pw1", h, P['w_pw1'], 'nn', S, 2 * D_MODEL, D_MODEL, extras=[P['b_pw1']],
            extra_specs=[_bias_spec(1024)], epi=lambda acc, b: (acc + b,))
    (u0,), _ = _rowk("conv_glu", lambda av: ((av[:, :D_MODEL] * _sigmoid(av[:, D_MODEL:]),), ()),
                     [a], [], [(D_MODEL, F32)], [])
    u1 = _dwconv_fwd(u0, P['w_dw'], P['b_dw'])

    def ln_silu(u, g, b):
        xc = u - _rowmean(u)
        z = xc * lax.rsqrt(_rowmean(xc * xc) + NORM_EPS) * g + b
        return (z * _sigmoid(z),), ()

    (u3,), _ = _rowk("conv_ln", ln_silu, [u1], [P['ln_g'], P['ln_b']], [(D_MODEL, BF16)], [])
    y = _mm("conv_pw2", u3, P['w_pw2'], 'nn', S, D_MODEL, D_MODEL, extras=[P['b_pw2']],
            extra_specs=[_bias_spec(1024)], epi=lambda acc, b: (acc + b,))
    return y, dict(a=a, u0=u0, u1=u1, u3=u3)


def _conv_bwd(dy, dy_colsum, h, sv, P):
    S = h.shape[0]
    g_pw2 = _mm("conv_pw2_wg", sv['u3'], dy, 'tn', D_MODEL, D_MODEL, S)
    du3 = _mm("conv_pw2_dg", dy, P['w_pw2'], 'nt', S, D_MODEL, D_MODEL)

    def ln_bwd(d3, u, g, b):
        xc = u - _rowmean(u)
        rstd = lax.rsqrt(_rowmean(xc * xc) + NORM_EPS)
        xh = xc * rstd
        z = xh * g + b
        sg = _sigmoid(z)
        dz = d3 * (sg * (1.0 + z * (1.0 - sg)))
        dxh = dz * g
        du = rstd * (dxh - _rowmean(dxh) - xh * _rowmean(dxh * xh))
        return (du,), (_colsum(dz * xh), _colsum(dz), _colsum(du))

    (du1,), (d_lng, d_lnb, d_bdw) = _rowk("conv_ln_bwd", ln_bwd, [du3, sv['u1']], [P['ln_g'], P['ln_b']],
                                          [(D_MODEL, F32)], [D_MODEL] * 3)
    du0, d_wdw = _dwconv_bwd(du1, sv['u0'], P['w_dw'])

    def glu_bwd(d0, av):
        a1, sg = av[:, :D_MODEL], _sigmoid(av[:, D_MODEL:])
        da = jnp.concatenate([d0 * sg, d0 * a1 * sg * (1.0 - sg)], axis=1)
        return (da,), (_colsum(da),)

    (da,), (d_bpw1,) = _rowk("conv_glu_bwd", glu_bwd, [du0, sv['a']], [], [(2 * D_MODEL, BF16)], [2 * D_MODEL])
    g_pw1 = _mm("conv_pw1_wg", h, da, 'tn', D_MODEL, 2 * D_MODEL, S)
    dh = _mm("conv_pw1_dg", da, P['w_pw1'], 'nt', S, D_MODEL, 2 * D_MODEL)
    grads = dict(conv_w_pw1=g_pw1, conv_b_pw1=d_bpw1.reshape(-1), conv_w_dw=d_wdw[:CONV_WIDTH],
                 conv_b_dw=d_bdw.reshape(-1), conv_ln_g=d_lng.reshape(-1), conv_ln_b=d_lnb.reshape(-1),
                 conv_w_pw2=g_pw2, conv_b_pw2=dy_colsum.reshape(-1))
    return dh, grads


def _pool_group_specs(tm):
    return (pl.BlockSpec((tm, POOL_C), lambda i, j, k: (i, j)),
            pl.BlockSpec((None, POOL_C, POOL_C), lambda i, j, k: (j, 0, 0)))


def _pool_mixer_fwd(h, P):
    S = h.shape[0]
    p = _pool_fwd(h)
    a_spec, b_spec = _pool_group_specs(min(1024, S))
    y, z = _mm("pool_mm", p, P['w'], 'nn', S, D_MODEL, POOL_C, tn=POOL_C, a_spec=a_spec, b_spec=b_spec,
               extras=[P['b'], P['scale']], extra_specs=[_bias_spec(POOL_C)] * 2,
               epi=lambda acc, b, s: ((acc + b) * s, acc + b),
               outs=[jax.ShapeDtypeStruct((S, D_MODEL), F32)] * 2)
    return y, dict(p=p, z=z)


def _pool_mixer_bwd(dy, sv, P):
    S = dy.shape[0]

    def scale_bwd(d, z, s):
        dz = d * s
        return (dz,), (_colsum(d * z), _colsum(dz))

    (dz,), (d_scale, d_b) = _rowk("pool_scale_bwd", scale_bwd, [dy, sv['z']], [P['scale']],
                                  [(D_MODEL, BF16)], [D_MODEL] * 2)
    a_spec, b_spec = _pool_group_specs(min(1024, S))
    dp = _mm("pool_mm_dg", dz, P['w'], 'nt', S, D_MODEL, POOL_C, tn=POOL_C, a_spec=a_spec, b_spec=b_spec)
    tk = min(512, S)
    grp = pl.BlockSpec((tk, POOL_C), lambda i, j, k: (k, j))
    g_w = _mm("pool_mm_wg", sv['p'], dz, 'tn', POOL_C, D_MODEL, S, tn=POOL_C, tk=tk, a_spec=grp, b_spec=grp,
              outs=[jax.ShapeDtypeStruct((len(POOL_WINDOWS), POOL_C, POOL_C), F32)],
              out_specs=[pl.BlockSpec((None, POOL_C, POOL_C), lambda i, j, k: (j, 0, 0))])
    dh = _pool_bwd(dp)
    return dh, dict(pool_w=g_w, pool_b=d_b.reshape(-1), pool_scale=d_scale.reshape(-1))


def _adamw(w, g, m, v):
    m2 = ADAM_B1 * m + (1.0 - ADAM_B1) * g
    v2 = ADAM_B2 * v + (1.0 - ADAM_B2) * (g * g)
    m_hat = m2 / (1.0 - ADAM_B1 ** ADAM_STEP)
    v_hat = v2 / (1.0 - ADAM_B2 ** ADAM_STEP)
    delta = -ADAM_LR * (m_hat / (jnp.sqrt(v_hat) + ADAM_EPS) + ADAM_WD * w)
    return delta, m2, v2


def _finish(name, w, land, m, v, layer=None, prev=None):
    local = land.shape[1:]
    C = local[-1]
    R = land[0].size // C
    tr = 64 if R % 64 == 0 else R

    def body(land_hbm, g_hbm, land_v, g_v, recv_v, io_sem, send_sem, recv_sem):
        load = pltpu.make_async_copy(land_hbm, land_v, io_sem)
        load.start()
        load.wait()

        def rows_of(i):
            return pl.ds(pl.multiple_of(i * tr, tr), tr)

        def sum_chunk(i, carry):
            rows = rows_of(i)
            g_v[rows, :] = ((land_v[0, rows, :].astype(F32) + land_v[1, rows, :].astype(F32))
                            + land_v[2, rows, :].astype(F32)) + land_v[3, rows, :].astype(F32)
            return carry

        lax.fori_loop(0, R // tr, sum_chunk, 0)
        swap = pltpu.make_async_remote_copy(
            src_ref=g_v, dst_ref=recv_v, send_sem=send_sem, recv_sem=recv_sem,
            device_id=(lax.axis_index("x"), lax.axis_index("y"), 1 - lax.axis_index("c")),
            device_id_type=pl.DeviceIdType.MESH)
        swap.start()
        swap.wait()

        def add_chunk(i, carry):
            rows = rows_of(i)
            recv_v[rows, :] = g_v[rows, :] + recv_v[rows, :]
            return carry

        lax.fori_loop(0, R // tr, add_chunk, 0)
        store = pltpu.make_async_copy(recv_v, g_hbm, io_sem)
        store.start()
        store.wait()

    any_spec = pl.BlockSpec(memory_space=pl.ANY)
    g = pl.pallas_call(
        body, name=name, in_specs=[any_spec], out_specs=any_spec, out_shape=jax.ShapeDtypeStruct((R, C), F32),
        scratch_shapes=[pltpu.VMEM((4, R, C), BF16), pltpu.VMEM((R, C), F32), pltpu.VMEM((R, C), F32),
                        pltpu.SemaphoreType.DMA, pltpu.SemaphoreType.DMA, pltpu.SemaphoreType.DMA],
        compiler_params=pltpu.CompilerParams(has_side_effects=True, vmem_limit_bytes=VMEM_LIMIT))(
            land.reshape(4, R, C))

    lead = () if layer is None else (w.shape[0],)
    as2d = lambda a: a.reshape(lead + (R, C))
    tu = _row_tile(R, C, budget=1 << 19)
    tile = pl.BlockSpec((tu, C), lambda i: (i, 0))
    slab = tile if layer is None else pl.BlockSpec((None, tu, C), lambda i: (layer, i, 0))
    n_prev = 0 if prev is None else 4

    def update(w_ref, g_ref, m_ref, v_ref, *rest):
        outs = rest[n_prev:]
        gv = g_ref[...]
        d, nm, nv = _adamw(w_ref[...], gv, m_ref[...], v_ref[...])
        for r, val in zip(outs, (gv, d, nm, nv)):
            r[...] = val

    res = pl.pallas_call(
        update, name=name + "_adamw", grid=(R // tu,),
        in_specs=[slab, tile, slab, slab] + [any_spec] * n_prev, out_specs=[slab] * 4,
        out_shape=[jax.ShapeDtypeStruct(lead + (R, C), F32)] * 4,
        input_output_aliases={4 + k: k for k in range(n_prev)},
        compiler_params=_cparams("arbitrary"))(
            as2d(w), g, as2d(m), as2d(v), *([] if prev is None else [as2d(p) for p in prev]))
    return [r.reshape(w.shape) for r in res]


def _row(v):
    return v.reshape(1, -1)


def kernel(x, c, positions, ada_w, ada_b, norm_g, mla_w_dq, mla_q_norm_g, mla_w_uq, mla_w_dkv, mla_kv_norm_g, mla_w_ukv, mla_w_o, conv_w_pw1, conv_b_pw1, conv_w_dw, conv_b_dw, conv_ln_g, conv_ln_b, conv_w_pw2, conv_b_pw2, pool_w, pool_b, pool_scale, ffn_w1, ffn_w2, loss_target, m_ada_w, m_ada_b, m_norm_g, m_mla_w_dq, m_mla_q_norm_g, m_mla_w_uq, m_mla_w_dkv, m_mla_kv_norm_g, m_mla_w_ukv, m_mla_w_o, m_conv_w_pw1, m_conv_b_pw1, m_conv_w_dw, m_conv_b_dw, m_conv_ln_g, m_conv_ln_b, m_conv_w_pw2, m_conv_b_pw2, m_pool_w, m_pool_b, m_pool_scale, m_ffn_w1, m_ffn_w2, v_ada_w, v_ada_b, v_norm_g, v_mla_w_dq, v_mla_q_norm_g, v_mla_w_uq, v_mla_w_dkv, v_mla_kv_norm_g, v_mla_w_ukv, v_mla_w_o, v_conv_w_pw1, v_conv_b_pw1, v_conv_w_dw, v_conv_b_dw, v_conv_ln_g, v_conv_ln_b, v_conv_w_pw2, v_conv_b_pw2, v_pool_w, v_pool_b, v_pool_scale, v_ffn_w1, v_ffn_w2):
    args = dict(locals())
    W = {n: args[n] for n in WEIGHTS}
    MOM = {n: args['m_' + n] for n in WEIGHTS}
    VAR = {n: args['v_' + n] for n in WEIGHTS}
    S = x.shape[1]
    xs = x.reshape(S, D_MODEL)
    tgt = loss_target.reshape(S, D_MODEL)
    mx, my, mc = lax.axis_index("x"), lax.axis_index("y"), lax.axis_index("c")
    chip = 2 * mx + my
    n_sh = ada_w.shape[2]

    def sent_of(key):
        n, l = key
        arr = W[n] if l is None else W[n][l]
        return arr.astype(BF16) if n in BIG or n in ('ffn_w1', 'ffn_w2') else arr

    keys0 = [(n, 0) for n in MLA_MATS] + [(n, None) for n in ('norm_g', 'mla_q_norm_g', 'mla_kv_norm_g',
                                                               'conv_w_dw', 'pool_b', 'pool_scale')]
    c8 = _exchange("gather_c", [c.reshape(8, D_MODEL // 8)], 'xyc')[0].reshape(8, D_MODEL)
    sent0 = [sent_of(k) for k in keys0]
    sent0[-1] = sent0[-1] + jnp.minimum(jnp.abs(c8[0, 0]), 0.0)
    fly0 = _split_start("gather_w0_start", sent0)
    c8 = c8 + fly0['token'][0, 0]
    c8 = jnp.pad(c8, ((0, ADA_ROWS - 8), (0, 0)))
    silu = lambda v: v * _sigmoid(v)
    mod_sh = []
    for l in range(DEPTH):
        b_l = lax.dynamic_slice(ada_b[l], (chip * n_sh,), (n_sh,)).reshape(1, n_sh)
        mod_sh.append(_mm(f"ada_fwd{l}", c8, ada_w, 'nn', ADA_ROWS, n_sh, D_MODEL, tn=n_sh // 2, tk=512, pro_a=silu,
                          b_spec=pl.BlockSpec((None, 512, n_sh // 2), lambda i, j, k, l=l: (l, k, j)),
                          extras=[b_l], extra_specs=[_bias_spec(n_sh // 2)], epi=lambda acc, b: (acc + b,))[:8])
    mod_sh = jnp.stack(mod_sh, axis=1).reshape(8, DEPTH * n_sh // 128, 128)
    mod = _exchange("scatter_mod", [mod_sh], 'xy', src_by='xyc')[0]
    mod = mod.reshape(4, DEPTH, n_sh).transpose(1, 0, 2).reshape(DEPTH, 6, 1, D_MODEL)

    keys1 = [('ffn_w1', 0), ('ffn_w2', 0), ('conv_w_pw1', None), ('conv_w_pw2', None), ('pool_w', None)]
    keys2 = [(n, l) for l in range(1, DEPTH) for n in ('ffn_w1', 'ffn_w2')] + [(n, 1) for n in MLA_MATS]
    fly1 = _split_start("gather_w1_start", [sent_of(k) for k in keys1])
    mod = mod + fly1['token'][0, 0]
    late = {}

    def start_group2(dep):
        sent2 = [sent_of(k) for k in keys2]
        sent2[-1] = sent2[-1] + jnp.minimum(jnp.abs(dep[0, 0]), 0).astype(BF16)
        late['fly2'] = _split_start("gather_w2_start", sent2)
        return late['fly2']['token']
    G = dict(zip(keys0, _split_wait("gather_w0_wait", fly0, mod)))

    def whole(key):
        n, l = key
        return _unshard(G[key], SHARD_AXIS[n] - (0 if l is None else 1))

    def mla_params(j):
        P = _mla_weights(*[whole((n, j)) for n in ('mla_w_dq', 'mla_w_dkv', 'mla_w_uq', 'mla_w_ukv', 'mla_w_o')])
        P.update(qg=_row(whole(('mla_q_norm_g', None))[j]), kg=_row(whole(('mla_kv_norm_g', None))[j]))
        return P

    gains = whole(('norm_g', None))
    mla_p = {0: mla_params(0)}
    conv_p = pool_p = None
    rope = _rope_tables(positions.reshape(S, 1).astype(F32))

    by_j = pl.BlockSpec((None, 1024, 1024), lambda i, j, k: (j, 0, 0))
    by_k = pl.BlockSpec((None, 1024, 1024), lambda i, j, k: (k, 0, 0))
    sq_relu = lambda v: jnp.square(jnp.maximum(v, 0.0))

    def md(i, k):
        return mod[i, k]

    (h,), _ = _rowk("pre0", lambda xv, g, sc, sh: ((_pre_fwd(xv, g, sc, sh),), ()),
                    [xs], [_row(gains[0, 0]), md(0, 1), md(0, 0)], [(D_MODEL, BF16)], [])
    saved = []
    xin = xs
    loss_acc = dxf = None
    for i in range(DEPTH):
        kind, j = i % 3, i // 3
        if kind == 0:
            if j not in mla_p:
                mla_p[j] = mla_params(j)
            y, sv = _mla_fwd(j, h, mla_p[j], rope, before_attention=start_group2 if i == 0 else None)
        elif kind == 1:
            y, sv = _conv_fwd(h, conv_p)
        else:
            y, sv = _pool_mixer_fwd(h, pool_p)

        def mid(xv, yv, gt, g1, g2, sc, sh):
            x1 = _post_fwd(xv, yv, gt, g1)
            return (x1, _pre_fwd(x1, g2, sc, sh)), ()

        (x1, h2), _ = _rowk(f"mid{i}", mid, [xin, y], [md(i, 2), _row(gains[i, 1]), _row(gains[i, 2]), md(i, 4), md(i, 3)],
                            [(D_MODEL, F32), (D_MODEL, BF16)], [])
        if i == 0:
            G.update(zip(keys1, _split_wait("gather_w1_wait", fly1, h2)))
            conv_p = dict(w_pw1=whole(('conv_w_pw1', None))[0], b_pw1=_row(conv_b_pw1[0]),
                          w_dw=whole(('conv_w_dw', None))[0], b_dw=_row(conv_b_dw[0]), ln_g=_row(conv_ln_g[0]),
                          ln_b=_row(conv_ln_b[0]), w_pw2=whole(('conv_w_pw2', None))[0], b_pw2=_row(conv_b_pw2[0]))
            pool_p = dict(w=whole(('pool_w', None))[0], b=_row(whole(('pool_b', None))[0]),
                          scale=_row(whole(('pool_scale', None))[0]))
        if i == 1:
            G.update(zip(keys2, _split_wait("gather_w2_wait", late['fly2'], h2)))
        a = _mm(f"ffn1_{i}", h2, G[('ffn_w1', i)], 'nn', S, D_FF, D_MODEL, tm=2048, b_spec=by_j,
                outs=[jax.ShapeDtypeStruct((S, D_FF), BF16)])
        y2 = _mm(f"ffn2_{i}", a, G[('ffn_w2', i)], 'nn', S, D_MODEL, D_FF, pro_a=sq_relu, b_spec=by_k)
        saved.append(dict(x0=xin, h=h, y=y, x1=x1, h2=h2, a=a, y2=y2, mix=sv))
        if i + 1 < DEPTH:
            def nxt(xv, yv, gt, g3, g0, sc, sh):
                x2 = _post_fwd(xv, yv, gt, g3)
                return (x2, _pre_fwd(x2, g0, sc, sh)), ()

            hdt = F32 if (i + 1) % 3 == 2 else BF16
            (xin, h), _ = _rowk(f"next{i}", nxt, [x1, y2],
                                [md(i, 5), _row(gains[i, 3]), _row(gains[i + 1, 0]), md(i + 1, 1), md(i + 1, 0)],
                                [(D_MODEL, F32), (D_MODEL, hdt)], [])
        else:
            def head(xv, yv, tv, gt, g3):
                err = _post_fwd(xv, yv, gt, g3) - tv
                per_row = jnp.sum(err * err, axis=1, keepdims=True) * (0.5 / D_MODEL)
                return (err * (1.0 / D_MODEL),), (jnp.broadcast_to(jnp.sum(per_row, axis=0, keepdims=True), (1, 128)),)

            (dxf,), (loss_acc,) = _rowk("loss_head", head, [x1, y2, tgt], [md(i, 5), _row(gains[i, 3])],
                                        [(D_MODEL, F32)], [128])

    small = {}
    big = {}
    landed = {}

    def keep(gm, layer):
        for n, g in gm.items():
            if n in BIG:
                g = g[None] if layer is None else g
                big[(n, layer)] = _to_shards(g, SHARD_AXIS[n] - (0 if layer is None else 1)).astype(BF16)
            else:
                small.setdefault(n, {})[layer or 0] = g

    d_mod = [None] * DEPTH
    d_gain = [None] * DEPTH
    dx = dxf
    for i in reversed(range(DEPTH)):
        kind, j = i % 3, i // 3
        sv = saved[i]
        def post2_bwd(d, yv, gt, g):
            dyv, d_gt, d_g = _post_bwd(d, yv, gt, g)
            return (dyv,), (d_gt, d_g)

        (dy2,), (d_gtf, d_g3) = _rowk(f"post2_bwd{i}", post2_bwd, [dx, sv['y2']], [md(i, 5), _row(gains[i, 3])],
                                      [(D_MODEL, BF16)], [D_MODEL] * 2)
        da = _mm(f"ffn2_dg{i}", dy2, G[('ffn_w2', i)], 'nt', S, D_FF, D_MODEL, tm=2048, b_spec=by_j, extras=[sv['a']],
                 extra_specs=[pl.BlockSpec((min(2048, S), 1024), lambda i_, j_, k_: (i_, j_))],
                 epi=lambda acc, av: (acc * (2.0 * jnp.maximum(av, 0.0)),),
                 outs=[jax.ShapeDtypeStruct((S, D_FF), BF16)])
        big[('ffn_w2', i)] = _mm(f"ffn2_wg{i}", sv['a'], dy2, 'tn', D_FF, D_MODEL, S, tk=2048, pro_a=sq_relu,
                        outs=[jax.ShapeDtypeStruct((4, 1024, D_MODEL), BF16)],
                        out_specs=[pl.BlockSpec((None, 1024, 1024), lambda i_, j_, k_: (i_, 0, j_))])
        big[('ffn_w1', i)] = _mm(f"ffn1_wg{i}", sv['h2'], da, 'tn', D_MODEL, D_FF, S, tk=2048,
                        outs=[jax.ShapeDtypeStruct((4, D_MODEL, 1024), BF16)],
                        out_specs=[pl.BlockSpec((None, 1024, 1024), lambda i_, j_, k_: (j_, i_, 0))])
        dh2 = _mm(f"ffn1_dg{i}", da, G[('ffn_w1', i)], 'nt', S, D_MODEL, D_FF, tm=2048, b_spec=by_k)
        if i == DEPTH - 1:
            keys_a = [('ffn_w1', i), ('ffn_w2', i)]
            fly_a = _split_start("scatter_ga_start", [big[k] for k in keys_a], src_by='xy')
            mod = mod + fly_a['token'][0, 0]
        if i == 0:
            keys_b = [k for k in big if k not in keys_a]
            fly_b = _split_start("scatter_gb_start", [big[k] for k in keys_b], src_by='xy')
            mod = mod + fly_b['token'][0, 0]

        def mid_bwd(d2, dh2v, x1v, yv, g2, scf, gtm, g1):
            dpre, d_sh, d_sc, d_g2 = _pre_bwd(dh2v, x1v, g2, scf)
            d1 = d2 + dpre
            dyv, d_gt, d_g1 = _post_bwd(d1, yv, gtm, g1)
            return (d1, dyv), (d_sh, d_sc, d_g2, d_gt, d_g1, _colsum(dyv))

        ydt = F32 if kind == 2 else BF16
        (dx1, dy), (d_shf, d_scf, d_g2, d_gtm, d_g1, dy_cs) = _rowk(
            f"mid_bwd{i}", mid_bwd, [dx, dh2, sv['x1'], sv['y']],
            [_row(gains[i, 2]), md(i, 4), md(i, 2), _row(gains[i, 1])],
            [(D_MODEL, F32), (D_MODEL, ydt)], [D_MODEL] * 6)
        if kind == 0:
            dh, gm = _mla_bwd(j, dy, sv['h'], sv['mix'], mla_p[j], rope)
            keep(gm, j)
        elif kind == 1:
            dh, gm = _conv_bwd(dy, dy_cs, sv['h'], sv['mix'], conv_p)
            keep(gm, None)
        else:
            dh, gm = _pool_mixer_bwd(dy, sv['mix'], pool_p)
            keep(gm, None)

        def pre_bwd(d1, dhv, x0v, g0, scm):
            dpre, d_sh, d_sc, d_g0 = _pre_bwd(dhv, x0v, g0, scm)
            return (d1 + dpre,), (d_sh, d_sc, d_g0)

        (dx,), (d_shm, d_scm, d_g0) = _rowk(f"pre_bwd{i}", pre_bwd, [dx1, dh, sv['x0']],
                                            [_row(gains[i, 0]), md(i, 1)], [(D_MODEL, F32)], [D_MODEL] * 3)
        d_mod[i] = jnp.concatenate([d_shm, d_scm, d_gtm, d_shf, d_scf, d_gtf], axis=1).reshape(-1)
        d_gain[i] = jnp.concatenate([d_g0, d_g1, d_g2, d_g3], axis=0)
        if i == DEPTH - 1:
            landed.update(zip(keys_a, _split_wait("scatter_ga_wait", fly_a, dx)))
    landed.update(zip(keys_b, _split_wait("scatter_gb_wait", fly_b, dx)))
    keys_c = [(n, 0) for n in MLA_MATS]
    fly_c = _split_start("scatter_gc_start", [big[k] for k in keys_c], src_by='xy')
    grad_x = dx.reshape(x.shape)
    grads = {n: jnp.stack([g[l] for l in sorted(g)]) for n, g in small.items()}
    grads['norm_g'] = jnp.stack(d_gain)
    grads['ada_b'] = jnp.stack(d_mod)

    pack = jnp.concatenate([grads[n].reshape(-1) for n in SMALL] + [loss_acc[0, :1]])
    n_pack = pack.shape[0]
    rows = -(-n_pack // 1024) * 8
    pack = jnp.pad(pack, (0, rows * 128 - n_pack)).reshape(rows, 128)
    fly_s = _split_start("gather_small_start", [pack], group='xyc')

    out_g, out_d, out_m, out_v = {}, {}, {}, {}
    chains = {}
    for n in BIG + ['ffn_w1', 'ffn_w2']:
        if n in MLA_MATS:
            chains[n] = _finish(f"finish_{n}1", W[n], landed[(n, 1)], MOM[n], VAR[n], layer=1)
        elif n in BIG:
            out_g[n], out_d[n], out_m[n], out_v[n] = _finish(f"finish_{n}", W[n], landed[(n, None)], MOM[n], VAR[n])
        else:
            res = None
            for l in range(DEPTH):
                res = _finish(f"finish_{n}{l}", W[n], landed[(n, l)], MOM[n], VAR[n], layer=l, prev=res)
            out_g[n], out_d[n], out_m[n], out_v[n] = res
    landed.update(zip(keys_c, _split_wait("scatter_gc_wait", fly_c, out_g['ffn_w2'])))
    for n in MLA_MATS:
        out_g[n], out_d[n], out_m[n], out_v[n] = _finish(f"finish_{n}0", W[n], landed[(n, 0)], MOM[n], VAR[n],
                                                         layer=0, prev=chains[n])
    pack8 = _split_wait("gather_small_wait", fly_s, out_g['mla_w_o'])[0]
    (tot,) = _ew("sum_small", lambda *v: (functools.reduce(lambda p, q: p + q, v),), [(pack8, s) for s in range(8)],
                 [F32], (rows, 128))
    tot = tot.reshape(-1)
    loss = tot[n_pack - 1]
    d_mod_all = pack8.reshape(8, -1)[:, :DEPTH * 6 * D_MODEL].reshape(8, DEPTH, 6 * D_MODEL)
    final = {}
    off = 0
    for n in SMALL:
        ax = SHARD_AXIS[n]
        shape = tuple(d * 4 if k == ax else d for k, d in enumerate(W[n].shape))
        size = grads[n].size
        g = tot[off:off + size].reshape(shape)
        off += size
        if ax is not None:
            g = lax.dynamic_index_in_dim(_to_shards(g, ax), chip, 0, keepdims=False)
        final[n] = g

    g_ada = []
    for l in range(DEPTH):
        dm_l = jnp.pad(lax.dynamic_slice(d_mod_all[:, l], (0, chip * n_sh), (8, n_sh)), ((0, ADA_ROWS - 8), (0, 0)))
        g_ada.append(_mm(f"ada_wg{l}", c8, dm_l, 'tn', D_MODEL, n_sh, ADA_ROWS, tn=n_sh // 2, pro_a=silu))
    final['ada_w'] = jnp.stack(g_ada)

    for n in WEIGHTS:
        if n in out_g:
            continue
        shape = W[n].shape
        out_g[n] = final[n].reshape(shape)
        out_d[n], out_m[n], out_v[n] = _ew(f"adamw_{n}", lambda w, g, m, v: _adamw(w, g, m, v),
                                           [W[n], out_g[n], MOM[n], VAR[n]], [F32] * 3, shape)
    return (loss, grad_x, *[out_g[n] for n in WEIGHTS], *[out_d[n] for n in WEIGHTS],
            *[out_m[n] for n in WEIGHTS], *[out_v[n] for n in WEIGHTS])
```

```python
import functools
import math

import jax
import jax.numpy as jnp
from jax import lax
from jax.experimental import pallas as pl
from jax.experimental.pallas import tpu as pltpu

F32 = jnp.float32
BF16 = jnp.bfloat16

D_MODEL = 1024
DEPTH = 4
N_HEADS = 16
QK_NOPE = 64
QK_ROPE = 32
V_HEAD = 64
Q_LORA = 384
KV_LORA = 256
HEAD_PAD = 128
QW = N_HEADS * HEAD_PAD
KVW = 2 * QW
DKV = KV_LORA + QK_ROPE
DQKV = Q_LORA + DKV
D_FF = 4096
CONV_WIDTH = 31
POOL_WINDOWS = (2, 4, 8, 16)
CHUNK_SHIFT = 6
ROPE_THETA = 10000.0
NORM_EPS = 1e-6
NEG_INF = -1e30
ATT_SCALE = 1.0 / math.sqrt(QK_NOPE + QK_ROPE)
BQ = 256
HB = 8
HF = 8
LOG2E = 1.4426950408889634
SCALE_LOG2E = ATT_SCALE * LOG2E
PAD_ROWS = 32
ADA_ROWS = 128
VMEM_LIMIT = 56 * 1024 * 1024

ADAM_LR = 0.001
ADAM_B1 = 0.9
ADAM_B2 = 0.999
ADAM_EPS = 1e-08
ADAM_WD = 0.01
ADAM_STEP = 10

WEIGHTS = ['ada_w', 'ada_b', 'norm_g', 'mla_w_dq', 'mla_q_norm_g', 'mla_w_uq', 'mla_w_dkv', 'mla_kv_norm_g',
           'mla_w_ukv', 'mla_w_o', 'conv_w_pw1', 'conv_b_pw1', 'conv_w_dw', 'conv_b_dw', 'conv_ln_g', 'conv_ln_b',
           'conv_w_pw2', 'conv_b_pw2', 'pool_w', 'pool_b', 'pool_scale', 'ffn_w1', 'ffn_w2']
SHARD_AXIS = {'ada_w': 2, 'ada_b': None, 'norm_g': 2, 'mla_w_dq': 1, 'mla_q_norm_g': 1, 'mla_w_uq': 2,
              'mla_w_dkv': 1, 'mla_kv_norm_g': 1, 'mla_w_ukv': 2, 'mla_w_o': 1, 'conv_w_pw1': 2,
              'conv_b_pw1': None, 'conv_w_dw': 2, 'conv_b_dw': None, 'conv_ln_g': None, 'conv_ln_b': None,
              'conv_w_pw2': 1, 'conv_b_pw2': None, 'pool_w': 2, 'pool_b': 2, 'pool_scale': 1,
              'ffn_w1': 2, 'ffn_w2': 1}
MLA_MATS = ['mla_w_dq', 'mla_w_uq', 'mla_w_dkv', 'mla_w_ukv', 'mla_w_o']
BIG = MLA_MATS + ['conv_w_pw1', 'conv_w_pw2', 'pool_w']
SMALL = ['ada_b', 'norm_g', 'mla_q_norm_g', 'mla_kv_norm_g', 'conv_b_pw1', 'conv_w_dw', 'conv_b_dw',
         'conv_ln_g', 'conv_ln_b', 'conv_b_pw2', 'pool_b', 'pool_scale']


def _cparams(*sem):
    return pltpu.CompilerParams(dimension_semantics=sem, vmem_limit_bytes=VMEM_LIMIT)


def _colsum(v):
    return jnp.sum(v, axis=0, keepdims=True)


def _rowmean(v):
    return jnp.mean(v, axis=-1, keepdims=True)


def _sigmoid(v):
    return 1.0 / (1.0 + jnp.exp(-v))


def _rowk(name, fn, rows, bcast, out_row, out_acc, tm=512):
    S = rows[0].shape[0]
    tm = min(tm, S)
    while S % tm:
        tm //= 2
    nin, no, na = len(rows) + len(bcast), len(out_row), len(out_acc)

    def body(*refs):
        vals = [r[...] for r in refs[:nin]]
        outs = refs[nin:nin + no]
        accs = refs[nin + no:]
        ro, ao = fn(*vals)
        for r, v in zip(outs, ro):
            r[...] = v.astype(r.dtype)
        if na:
            @pl.when(pl.program_id(0) == 0)
            def _():
                for r in accs:
                    r[...] = jnp.zeros(r.shape, r.dtype)
            for r, v in zip(accs, ao):
                r[...] += v

    in_specs = [pl.BlockSpec((tm, a.shape[1]), lambda i: (i, 0)) for a in rows]
    in_specs += [pl.BlockSpec(b.shape, lambda i, n=b.ndim: (0,) * n) for b in bcast]
    out_shape = [jax.ShapeDtypeStruct((S, w), dt) for w, dt in out_row]
    out_shape += [jax.ShapeDtypeStruct((1, w), F32) for w in out_acc]
    out_specs = [pl.BlockSpec((tm, w), lambda i: (i, 0)) for w, _ in out_row]
    out_specs += [pl.BlockSpec((1, w), lambda i: (0, 0)) for w in out_acc]
    res = pl.pallas_call(body, name=name, grid=(S // tm,), in_specs=in_specs, out_specs=out_specs,
                         out_shape=out_shape, compiler_params=_cparams("arbitrary"))(*rows, *bcast)
    return list(res[:no]), list(res[no:])


_DIMS = {'nn': ((1,), (0,)), 'nt': ((1,), (1,)), 'tn': ((0,), (0,))}


def _mm(name, a, b, mode, M, N, K, *, tm=1024, tn=1024, tk=1024, a_spec=None, b_spec=None, pro_a=None,
        extras=(), extra_specs=(), epi=None, outs=None, out_specs=None):
    tm, tn, tk = (t if d % t == 0 else d for t, d in ((min(tm, M), M), (min(tn, N), N), (min(tk, K), K)))
    nk = K // tk
    if a_spec is None:
        a_spec = (pl.BlockSpec((tk, tm), lambda i, j, k: (k, i)) if mode == 'tn'
                  else pl.BlockSpec((tm, tk), lambda i, j, k: (i, k)))
    if b_spec is None:
        b_spec = (pl.BlockSpec((tn, tk), lambda i, j, k: (j, k)) if mode == 'nt'
                  else pl.BlockSpec((tk, tn), lambda i, j, k: (k, j)))
    if outs is None:
        outs = [jax.ShapeDtypeStruct((M, N), F32)]
    if out_specs is None:
        out_specs = [pl.BlockSpec((tm, tn), lambda i, j, k: (i, j)) for _ in outs]
    ne, no = len(extras), len(outs)
    dims = (_DIMS[mode], ((), ()))

    def body(a_ref, b_ref, *rest):
        ex, out_refs = rest[:ne], rest[ne:ne + no]
        av = a_ref[...]
        if pro_a is not None:
            av = pro_a(av)
        part = lax.dot_general(av.astype(BF16), b_ref[...].astype(BF16), dims, preferred_element_type=F32)

        def finish(acc):
            vals = (acc,) if epi is None else epi(acc, *[e[...] for e in ex])
            for r, v in zip(out_refs, vals):
                r[...] = v.astype(r.dtype)

        if nk == 1:
            finish(part)
            return
        acc_ref = rest[ne + no]
        k = pl.program_id(2)

        @pl.when(k == 0)
        def _():
            acc_ref[...] = part

        @pl.when(k > 0)
        def _():
            acc_ref[...] += part

        @pl.when(k == nk - 1)
        def _():
            finish(acc_ref[...])

    res = pl.pallas_call(
        body, name=name, grid=(M // tm, N // tn, nk),
        in_specs=[a_spec, b_spec, *extra_specs], out_specs=list(out_specs), out_shape=list(outs),
        scratch_shapes=[pltpu.VMEM((tm, tn), F32)] if nk > 1 else [],
        compiler_params=_cparams("parallel", "parallel", "arbitrary"))(a, b, *extras)
    return res[0] if no == 1 else list(res)


def _row_tile(R, C, itemsize=4, budget=1 << 20):
    if R * C * itemsize <= budget or R % 8:
        return R
    t = 8
    while R % (t * 2) == 0 and t * 2 * C * itemsize <= budget:
        t *= 2
    return t


def _ew(name, fn, ins, out_dtypes, shape):
    C = shape[-1]
    R = 1
    for s in shape[:-1]:
        R *= s
    tr = _row_tile(R, C)
    ops, specs = [], []
    for it in ins:
        if isinstance(it, tuple):
            arr, idx = it
            ops.append(arr.reshape(arr.shape[0], R, C))
            specs.append(pl.BlockSpec((None, tr, C), lambda i, n=idx: (n, i, 0)))
        else:
            ops.append(it.reshape(R, C))
            specs.append(pl.BlockSpec((tr, C), lambda i: (i, 0)))
    nin = len(ops)

    def body(*refs):
        vals = fn(*[r[...] for r in refs[:nin]])
        for r, v in zip(refs[nin:], vals):
            r[...] = v.astype(r.dtype)

    res = pl.pallas_call(
        body, name=name, grid=(R // tr,), in_specs=specs,
        out_specs=[pl.BlockSpec((tr, C), lambda i: (i, 0)) for _ in out_dtypes],
        out_shape=[jax.ShapeDtypeStruct((R, C), dt) for dt in out_dtypes],
        compiler_params=_cparams("parallel"))(*ops)
    return [r.reshape(shape) for r in res]


_FLIPS = {'xyc': [(fx, fy, fc) for fx in (0, 1) for fy in (0, 1) for fc in (0, 1)][1:],
          'xy': [(1, 0, 0), (0, 1, 0), (1, 1, 0)],
          'c': [(0, 0, 1)]}
_NSLOT = {'xyc': 8, 'xy': 4, 'c': 2}


def _slot(kind, cx, cy, cc):
    return {'xyc': 4 * cx + 2 * cy + cc, 'xy': 2 * cx + cy, 'c': cc}[kind]


def _put_own(land, arr, group, src_by):
    coords = (lax.axis_index("x"), lax.axis_index("y"), lax.axis_index("c"))
    pay = arr if src_by is None else lax.dynamic_index_in_dim(arr, _slot(src_by, *coords), 0, keepdims=False)
    return lax.dynamic_update_index_in_dim(land, pay, _slot(group, *coords), 0)


def _exchange(name, arrays, group, src_by=None):
    flips, nsl, n = _FLIPS[group], _NSLOT[group], len(arrays)
    nf = len(flips)

    def body(*refs):
        ins, outs = refs[:n], refs[n:2 * n]
        send_sems, recv_sems = refs[2 * n:]
        mx, my, mc = lax.axis_index("x"), lax.axis_index("y"), lax.axis_index("c")
        me = _slot(group, mx, my, mc)

        def payload(a, cx, cy, cc):
            return ins[a] if src_by is None else ins[a].at[_slot(src_by, cx, cy, cc)]

        sends, recvs = [], []
        for a in range(n):
            for f, (fx, fy, fc) in enumerate(flips):
                px = 1 - mx if fx else mx
                py = 1 - my if fy else my
                pc = 1 - mc if fc else mc
                src = payload(a, px, py, pc)
                sends.append(pltpu.make_async_remote_copy(
                    src_ref=src, dst_ref=outs[a].at[me], send_sem=send_sems.at[a, f],
                    recv_sem=recv_sems.at[a, f], device_id=(px, py, pc),
                    device_id_type=pl.DeviceIdType.MESH))
                recvs.append(pltpu.make_async_remote_copy(
                    src_ref=src, dst_ref=outs[a].at[_slot(group, px, py, pc)], send_sem=send_sems.at[a, f],
                    recv_sem=recv_sems.at[a, f], device_id=(px, py, pc),
                    device_id_type=pl.DeviceIdType.MESH))
        for cp in sends:
            cp.start()
        for cp in recvs:
            cp.wait_recv()
        for cp in sends:
            cp.wait_send()

    out_shape = [jax.ShapeDtypeStruct((nsl,) + (a.shape if src_by is None else a.shape[1:]), a.dtype)
                 for a in arrays]
    any_spec = pl.BlockSpec(memory_space=pl.ANY)
    res = pl.pallas_call(
        body, name=name, in_specs=[any_spec] * n, out_specs=[any_spec] * n, out_shape=out_shape,
        scratch_shapes=[pltpu.SemaphoreType.DMA((n, nf)), pltpu.SemaphoreType.DMA((n, nf))],
        compiler_params=pltpu.CompilerParams(has_side_effects=True))(*arrays)
    return [_put_own(l, a, group, src_by) for a, l in zip(arrays, res)]


_HBM = pl.BlockSpec(memory_space=pltpu.HBM)
_SEM = pl.BlockSpec(memory_space=pltpu.SEMAPHORE)
_DATAFLOW = pltpu.SideEffectType.DATAFLOW_SIDE_EFFECTING


def _group_copies(ins, lands, send_sems, recv_sems, group, src_by):
    mx, my, mc = lax.axis_index("x"), lax.axis_index("y"), lax.axis_index("c")
    me = _slot(group, mx, my, mc)
    pairs = []
    for a in range(len(ins)):
        for fx, fy, fc in _FLIPS[group]:
            peer = (1 - mx if fx else mx, 1 - my if fy else my, 1 - mc if fc else mc)
            src = ins[a] if src_by is None else ins[a].at[_slot(src_by, *peer)]
            mk = functools.partial(pltpu.make_async_remote_copy, src_ref=src, send_sem=send_sems,
                                   recv_sem=recv_sems, device_id=peer, device_id_type=pl.DeviceIdType.MESH)
            pairs.append((mk(dst_ref=lands[a].at[me]), mk(dst_ref=lands[a].at[_slot(group, *peer)])))
    return pairs


def _split_start(name, arrays, group='xy', src_by=None):
    n = len(arrays)
    lands = [lax.empty((_NSLOT[group],) + (a.shape if src_by is None else a.shape[1:]), a.dtype) for a in arrays]

    def body(*refs):
        ins, lnd, send_sems, recv_sems, token = refs[:n], refs[n:2 * n], refs[2 * n], refs[2 * n + 1], refs[-1]
        for to_peer, _ in _group_copies(ins, lnd, send_sems, recv_sems, group, src_by):
            to_peer.start()
        token[...] = jnp.zeros(token.shape, F32)

    ops = [pltpu.with_memory_space_constraint(a, pltpu.HBM) for a in [*arrays, *lands]]
    res = pl.pallas_call(
        body, name=name, in_specs=[_HBM] * (2 * n),
        out_specs=[_SEM, _SEM] + [_HBM] * (2 * n) + [pl.BlockSpec(memory_space=pltpu.VMEM)],
        out_shape=[pltpu.SemaphoreType.DMA(()), pltpu.SemaphoreType.DMA(())]
        + [pltpu.HBM(a.shape, a.dtype) for a in ops] + [jax.ShapeDtypeStruct((8, 128), F32)],
        input_output_aliases={k: 2 + k for k in range(2 * n)},
        compiler_params=pltpu.CompilerParams(has_side_effects=_DATAFLOW))(*ops)
    return dict(n=n, group=group, src_by=src_by, send=res[0], recv=res[1], arrays=list(res[2:2 + n]),
                lands=list(res[2 + n:2 + 2 * n]), token=res[-1])


def _split_wait(name, st, after):
    n, group, src_by = st['n'], st['group'], st['src_by']

    def wait_body(*refs):
        ins, lnd, send_sems, recv_sems = refs[:n], refs[n:2 * n], refs[2 * n], refs[2 * n + 1]
        for to_peer, from_peer in _group_copies(ins, lnd, send_sems, recv_sems, group, src_by):
            to_peer.wait_send()
            from_peer.wait_recv()

    shapes = [pltpu.HBM(a.shape, a.dtype) for a in [*st['arrays'], *st['lands']]]
    res = pl.pallas_call(
        wait_body, name=name, in_specs=[_HBM] * (2 * n) + [_SEM, _SEM, pl.BlockSpec(memory_space=pl.ANY)],
        out_specs=[_HBM] * (2 * n), out_shape=shapes, input_output_aliases={k: k for k in range(2 * n)},
        compiler_params=pltpu.CompilerParams(has_side_effects=_DATAFLOW))(
            *st['arrays'], *st['lands'], st['send'], st['recv'], after)
    return [_put_own(l, a, group, src_by) for a, l in zip(res[:n], res[n:])]


def _unshard(g, axis):
    t = jnp.moveaxis(g, 0, axis)
    s = t.shape
    return t.reshape(s[:axis] + (s[axis] * s[axis + 1],) + s[axis + 2:])


def _to_shards(w, axis):
    s = w.shape
    t = w.reshape(s[:axis] + (4, s[axis] // 4) + s[axis + 1:])
    return jnp.moveaxis(t, axis, 0)


def _pre_fwd(x, g, sc, sh):
    r = lax.rsqrt(_rowmean(x * x) + NORM_EPS)
    return (x * r) * g * (1.0 + sc) + sh


def _pre_bwd(dh, x, g, sc):
    r = lax.rsqrt(_rowmean(x * x) + NORM_EPS)
    xn = x * r
    dxn = dh * (g * (1.0 + sc))
    dx = r * (dxn - xn * _rowmean(dxn * xn))
    t = dh * xn
    return dx, _colsum(dh), _colsum(t * g), _colsum(t * (1.0 + sc))


def _post_fwd(x, y, gt, g):
    r = lax.rsqrt(_rowmean(y * y) + NORM_EPS)
    return x + gt * ((y * r) * g)


def _post_bwd(dxo, y, gt, g):
    r = lax.rsqrt(_rowmean(y * y) + NORM_EPS)
    yn = y * r
    t = dxo * yn
    dyn = dxo * (gt * g)
    dy = r * (dyn - yn * _rowmean(dyn * yn))
    return dy, _colsum(t * g), _colsum(t * gt)


def _gain_bwd(dy, x, g):
    r = lax.rsqrt(_rowmean(x * x) + NORM_EPS)
    xn = x * r
    dxn = dy * g
    return r * (dxn - xn * _rowmean(dxn * xn)), _colsum(dy * xn)


def _rope(x, cos, sa, sb):
    return x * cos + pltpu.roll(x, HEAD_PAD - 16, 1) * sa + pltpu.roll(x, 16, 1) * sb


def _rope_t(d, cos, sa, sb):
    return d * cos + pltpu.roll(d * sa, 16, 1) + pltpu.roll(d * sb, HEAD_PAD - 16, 1)


def _rope_tables(pos_f):
    S = pos_f.shape[0]
    inv = ROPE_THETA ** (-jnp.arange(0, QK_ROPE, 2, dtype=F32) / QK_ROPE)
    inv_ext = jnp.concatenate([jnp.zeros((QK_NOPE,), F32), inv, inv,
                               jnp.zeros((HEAD_PAD - QK_NOPE - QK_ROPE,), F32)]).reshape(1, HEAD_PAD)

    def fn(p, iv):
        ang = p * iv
        lane = lax.broadcasted_iota(jnp.int32, ang.shape, 1)
        s = jnp.sin(ang)
        first = (lane >= QK_NOPE) & (lane < QK_NOPE + QK_ROPE // 2)
        second = (lane >= QK_NOPE + QK_ROPE // 2) & (lane < QK_NOPE + QK_ROPE)
        return (jnp.cos(ang), jnp.where(first, -s, 0.0), jnp.where(second, s, 0.0)), ()

    (cos, sa, sb), _ = _rowk("rope_tables", fn, [pos_f], [inv_ext], [(HEAD_PAD, F32)] * 3, [])
    return cos, sa, sb


def _diag_mask(transposed):
    r = lax.broadcasted_iota(jnp.int32, (BQ, BQ), 0) >> CHUNK_SHIFT
    c = lax.broadcasted_iota(jnp.int32, (BQ, BQ), 1) >> CHUNK_SHIFT
    return (r <= c) if transposed else (c <= r)


_NT = (((1,), (1,)), ((), ()))
_NN = (((1,), (0,)), ((), ()))


def _attn_fwd(qf, kvf, after, HB=HF):
    S = qf.shape[0]
    nq = S // BQ

    def body(q_ref, kv_ref, after_ref, o_ref, ob_ref, lse_ref):
        qi = pl.program_id(1)
        qs = [q_ref[:, hh * HEAD_PAD:(hh + 1) * HEAD_PAD] for hh in range(HB)]

        def step(j, carry, diag):
            off = pl.multiple_of(j * BQ, BQ)
            sts = [lax.dot_general(kv_ref[pl.ds(off, BQ), pl.ds(2 * hh * HEAD_PAD, HEAD_PAD)], qs[hh], _NT,
                                   preferred_element_type=F32) for hh in range(HB)]
            mid = []
            for hh in range(HB):
                m, l, acc = carry[hh]
                st = jnp.where(_diag_mask(True), sts[hh], NEG_INF) if diag else sts[hh]
                m2 = jnp.maximum(m, jnp.max(st, axis=0, keepdims=True))
                al = jnp.exp2(m - m2)
                pt = jnp.exp2(st - m2)
                mid.append((m2, l * al + jnp.sum(pt, axis=0, keepdims=True), acc * al, pt.astype(BF16)))
            out = []
            for hh in range(HB):
                m2, l2, acc_s, ptb = mid[hh]
                v = kv_ref[pl.ds(off, BQ), pl.ds((2 * hh + 1) * HEAD_PAD, HEAD_PAD)]
                out.append((m2, l2, acc_s + lax.dot_general(v, ptb, _TN, preferred_element_type=F32)))
            return tuple(out)

        init = tuple((jnp.full((1, BQ), NEG_INF, F32), jnp.zeros((1, BQ), F32), jnp.zeros((HEAD_PAD, BQ), F32))
                     for _ in range(HB))
        carry = lax.fori_loop(0, qi, lambda j, c: step(j, c, False), init)
        carry = step(qi, carry, True)
        for hh in range(HB):
            m, l, acc = carry[hh]
            ov = (acc / l).T
            o_ref[:, hh * HEAD_PAD:(hh + 1) * HEAD_PAD] = ov
            ob_ref[:, hh * HEAD_PAD:(hh + 1) * HEAD_PAD] = ov.astype(BF16)
            lse_ref[hh] = m + jnp.log(l) * LOG2E

    return pl.pallas_call(
        body, name="attn_fwd", grid=(N_HEADS // HB, nq),
        in_specs=[pl.BlockSpec((BQ, HB * HEAD_PAD), lambda g, i: (i, g)),
                  pl.BlockSpec((S, 2 * HB * HEAD_PAD), lambda g, i: (0, g)),
                  pl.BlockSpec(after.shape, lambda g, i: (0, 0))],
        out_specs=[pl.BlockSpec((BQ, HB * HEAD_PAD), lambda g, i: (i, g)),
                   pl.BlockSpec((BQ, HB * HEAD_PAD), lambda g, i: (i, g)),
                   pl.BlockSpec((HB, None, 1, BQ), lambda g, i: (g, i, 0, 0))],
        out_shape=[jax.ShapeDtypeStruct((S, QW), F32), jax.ShapeDtypeStruct((S, QW), BF16),
                   jax.ShapeDtypeStruct((N_HEADS, nq, 1, BQ), F32)],
        compiler_params=_cparams("parallel", "arbitrary"))(qf, kvf, after)


def _attn_delta(dob, o):
    S = o.shape[0]

    def fn(dov, ov):
        prod = dov.astype(F32) * ov
        lane = lax.broadcasted_iota(jnp.int32, (prod.shape[0], HEAD_PAD), 1)
        out = jnp.zeros((prod.shape[0], HEAD_PAD), F32)
        for h in range(N_HEADS):
            out = jnp.where(lane == h, jnp.sum(prod[:, h * HEAD_PAD:(h + 1) * HEAD_PAD], axis=1, keepdims=True), out)
        return (out,), ()

    (dd,), _ = _rowk("attn_delta", fn, [dob, o], [], [(HEAD_PAD, F32)], [])
    return dd[:, :N_HEADS].T.reshape(N_HEADS, S // BQ, 1, BQ)


_TN = (((0,), (0,)), ((), ()))


def _attn_bwd(qf, kvf, dob, lse_row, dd_row, cos, sa, sb):
    S = qf.shape[0]
    nq = S // BQ

    def body(kv_ref, q_ref, do_ref, lse_ref, dd_ref, cos_ref, sa_ref, sb_ref, dqo_ref, dkv_ref, dq_ref):
        kj = pl.program_id(1)

        @pl.when(kj == 0)
        def _():
            dq_ref[...] = jnp.zeros(dq_ref.shape, F32)

        ks = [kv_ref[:, 2 * hh * HEAD_PAD:(2 * hh + 1) * HEAD_PAD] for hh in range(HB)]
        vs = [kv_ref[:, (2 * hh + 1) * HEAD_PAD:(2 * hh + 2) * HEAD_PAD] for hh in range(HB)]

        def step(i, carry, diag):
            off = pl.multiple_of(i * BQ, BQ)
            cols = [pl.ds(hh * HEAD_PAD, HEAD_PAD) for hh in range(HB)]
            q = [q_ref[pl.ds(off, BQ), cols[hh]] for hh in range(HB)]
            do = [do_ref[pl.ds(off, BQ), cols[hh]] for hh in range(HB)]
            sts = [lax.dot_general(ks[hh], q[hh], _NT, preferred_element_type=F32) for hh in range(HB)]
            dpts = [lax.dot_general(vs[hh], do[hh], _NT, preferred_element_type=F32) for hh in range(HB)]
            mid = []
            for hh in range(HB):
                st = jnp.where(_diag_mask(True), sts[hh], NEG_INF) if diag else sts[hh]
                pt = jnp.exp2(st - lse_ref[hh, i])
                mid.append((pt.astype(BF16), (pt * (dpts[hh] - dd_ref[hh, i])).astype(BF16)))
            out = []
            for hh in range(HB):
                dk, dv = carry[hh]
                ptb, dsb = mid[hh]
                dv2 = dv + lax.dot_general(ptb, do[hh], _NN, preferred_element_type=F32)
                dk2 = dk + lax.dot_general(dsb, q[hh], _NN, preferred_element_type=F32)
                dq_ref[pl.ds(off, BQ), cols[hh]] += lax.dot_general(dsb, ks[hh], _TN, preferred_element_type=F32)
                out.append((dk2, dv2))
            return tuple(out)

        zero = jnp.zeros((BQ, HEAD_PAD), F32)
        carry = step(kj, tuple((zero, zero) for _ in range(HB)), True)
        carry = lax.fori_loop(kj + 1, nq, lambda i, c: step(i, c, False), carry)
        done = pl.ds(pl.multiple_of(kj * BQ, BQ), BQ)
        for hh in range(HB):
            dk, dv = carry[hh]
            dk = _rope_t(dk * (1.0 / LOG2E), cos_ref[...], sa_ref[...], sb_ref[...])
            dkv_ref[:, 2 * hh * HEAD_PAD:(2 * hh + 1) * HEAD_PAD] = dk.astype(BF16)
            dkv_ref[:, (2 * hh + 1) * HEAD_PAD:(2 * hh + 2) * HEAD_PAD] = dv.astype(BF16)
            cols = pl.ds(hh * HEAD_PAD, HEAD_PAD)
            dqo_ref[:, cols] = _rope_t(dq_ref[done, cols] * ATT_SCALE, cos_ref[...], sa_ref[...],
                                       sb_ref[...]).astype(BF16)

    tab = pl.BlockSpec((BQ, HEAD_PAD), lambda g, j: (j, 0))
    row = pl.BlockSpec((HB, nq, 1, BQ), lambda g, j: (g, 0, 0, 0))
    seq = pl.BlockSpec((S, HB * HEAD_PAD), lambda g, j: (0, g), pipeline_mode=pl.Buffered(1))
    kvb = pl.BlockSpec((BQ, 2 * HB * HEAD_PAD), lambda g, j: (j, g))
    return pl.pallas_call(
        body, name="attn_bwd", grid=(N_HEADS // HB, nq),
        in_specs=[kvb, seq, seq, row, row, tab, tab, tab],
        out_specs=[pl.BlockSpec((BQ, HB * HEAD_PAD), lambda g, j: (j, g)), kvb],
        out_shape=[jax.ShapeDtypeStruct((S, QW), BF16), jax.ShapeDtypeStruct((S, KVW), BF16)],
        scratch_shapes=[pltpu.VMEM((S, HB * HEAD_PAD), F32)],
        compiler_params=_cparams("parallel", "arbitrary"))(kvf, qf, dob, lse_row, dd_row, cos, sa, sb)


DC = 128
TR = 256


def _dwconv_fwd(u, w, b):
    S, Dm = u.shape
    tr = min(TR, S)

    def body(u_ref, w_ref, b_ref, o_ref, pad_ref):
        pad_ref[pl.ds(0, PAD_ROWS), :] = jnp.zeros((PAD_ROWS, DC), F32)
        pad_ref[pl.ds(PAD_ROWS, S), :] = u_ref[...]
        wv = w_ref[...]
        for r in range(S // tr):
            acc = jnp.broadcast_to(b_ref[...], (tr, DC))
            for j in range(CONV_WIDTH):
                acc = acc + wv[j:j + 1, :] * pad_ref[pl.ds(r * tr + PAD_ROWS - (CONV_WIDTH - 1) + j, tr), :]
            o_ref[pl.ds(r * tr, tr), :] = acc

    return pl.pallas_call(
        body, name="dwconv_fwd", grid=(Dm // DC,),
        in_specs=[pl.BlockSpec((S, DC), lambda c: (0, c)), pl.BlockSpec((CONV_WIDTH, DC), lambda c: (0, c)),
                  pl.BlockSpec((1, DC), lambda c: (0, c))],
        out_specs=pl.BlockSpec((S, DC), lambda c: (0, c)),
        out_shape=jax.ShapeDtypeStruct((S, Dm), F32),
        scratch_shapes=[pltpu.VMEM((S + PAD_ROWS, DC), F32)],
        compiler_params=_cparams("parallel"))(u, w, b)


def _dwconv_bwd(d, u, w):
    S, Dm = u.shape
    tr = min(TR, S)

    def body(d_ref, u_ref, w_ref, du_ref, dw_ref, padd_ref, padu_ref):
        padd_ref[pl.ds(0, S), :] = d_ref[...]
        padd_ref[pl.ds(S, PAD_ROWS), :] = jnp.zeros((PAD_ROWS, DC), F32)
        padu_ref[pl.ds(0, PAD_ROWS), :] = jnp.zeros((PAD_ROWS, DC), F32)
        padu_ref[pl.ds(PAD_ROWS, S), :] = u_ref[...]
        wv = w_ref[...]
        dws = [jnp.zeros((1, DC), F32) for _ in range(CONV_WIDTH)]
        for r in range(S // tr):
            acc = jnp.zeros((tr, DC), F32)
            for j in range(CONV_WIDTH):
                acc = acc + wv[j:j + 1, :] * padd_ref[pl.ds(r * tr + (CONV_WIDTH - 1) - j, tr), :]
            du_ref[pl.ds(r * tr, tr), :] = acc
            dt = d_ref[pl.ds(r * tr, tr), :]
            for j in range(CONV_WIDTH):
                ut = padu_ref[pl.ds(r * tr + PAD_ROWS - (CONV_WIDTH - 1) + j, tr), :]
                dws[j] = dws[j] + _colsum(dt * ut)
        for j in range(CONV_WIDTH):
            dw_ref[pl.ds(j, 1), :] = dws[j]
        dw_ref[pl.ds(CONV_WIDTH, 1), :] = jnp.zeros((1, DC), F32)

    blk = pl.BlockSpec((S, DC), lambda c: (0, c))
    return pl.pallas_call(
        body, name="dwconv_bwd", grid=(Dm // DC,),
        in_specs=[blk, blk, pl.BlockSpec((CONV_WIDTH, DC), lambda c: (0, c))],
        out_specs=[blk, pl.BlockSpec((PAD_ROWS, DC), lambda c: (0, c))],
        out_shape=[jax.ShapeDtypeStruct((S, Dm), F32), jax.ShapeDtypeStruct((PAD_ROWS, Dm), F32)],
        scratch_shapes=[pltpu.VMEM((S + PAD_ROWS, DC), F32), pltpu.VMEM((S + PAD_ROWS, DC), F32)],
        compiler_params=_cparams("parallel"))(d, u, w)


POOL_C = D_MODEL // len(POOL_WINDOWS)


def _pool_counts(r, tr, win):
    t = r * tr + lax.broadcasted_iota(jnp.int32, (tr, 1), 0)
    return jnp.minimum(t + 1, win).astype(F32)


def _pool_fwd(h):
    S, Dm = h.shape
    tr = min(TR, S)

    def body(h_ref, o_ref, pad_ref):
        pad_ref[pl.ds(0, PAD_ROWS), :] = jnp.zeros((PAD_ROWS, POOL_C), F32)
        pad_ref[pl.ds(PAD_ROWS, S), :] = h_ref[...]
        for g, win in enumerate(POOL_WINDOWS):
            @pl.when(pl.program_id(0) == g)
            def _():
                for r in range(S // tr):
                    acc = pad_ref[pl.ds(r * tr + PAD_ROWS, tr), :]
                    for j in range(1, win):
                        acc = acc + pad_ref[pl.ds(r * tr + PAD_ROWS - j, tr), :]
                    pooled = acc / _pool_counts(r, tr, win)
                    o_ref[pl.ds(r * tr, tr), :] = (pooled - h_ref[pl.ds(r * tr, tr), :]).astype(BF16)

    blk = pl.BlockSpec((S, POOL_C), lambda g: (0, g))
    return pl.pallas_call(
        body, name="pool_fwd", grid=(len(POOL_WINDOWS),), in_specs=[blk], out_specs=blk,
        out_shape=jax.ShapeDtypeStruct((S, Dm), BF16),
        scratch_shapes=[pltpu.VMEM((S + PAD_ROWS, POOL_C), F32)],
        compiler_params=_cparams("parallel"))(h)


def _pool_bwd(dp):
    S, Dm = dp.shape
    tr = min(TR, S)

    def body(d_ref, o_ref, pad_ref):
        pad_ref[pl.ds(S, PAD_ROWS), :] = jnp.zeros((PAD_ROWS, POOL_C), F32)
        for g, win in enumerate(POOL_WINDOWS):
            @pl.when(pl.program_id(0) == g)
            def _():
                for r in range(S // tr):
                    pad_ref[pl.ds(r * tr, tr), :] = d_ref[pl.ds(r * tr, tr), :] / _pool_counts(r, tr, win)
                for r in range(S // tr):
                    acc = pad_ref[pl.ds(r * tr, tr), :]
                    for j in range(1, win):
                        acc = acc + pad_ref[pl.ds(r * tr + j, tr), :]
                    o_ref[pl.ds(r * tr, tr), :] = acc - d_ref[pl.ds(r * tr, tr), :]

    blk = pl.BlockSpec((S, POOL_C), lambda g: (0, g))
    return pl.pallas_call(
        body, name="pool_bwd", grid=(len(POOL_WINDOWS),), in_specs=[blk], out_specs=blk,
        out_shape=jax.ShapeDtypeStruct((S, Dm), F32),
        scratch_shapes=[pltpu.VMEM((S + PAD_ROWS, POOL_C), F32)],
        compiler_params=_cparams("parallel"))(dp)


def _bias_spec(tn):
    return pl.BlockSpec((1, tn), lambda i, j, k: (0, j))


def _mla_weights(w_dq, w_dkv, w_uq, w_ukv, w_o):
    wd = jnp.concatenate([w_dq, w_dkv], axis=1)
    wq = jnp.pad(w_uq.reshape(Q_LORA, N_HEADS, QK_NOPE + QK_ROPE),
                 ((0, 0), (0, 0), (0, HEAD_PAD - QK_NOPE - QK_ROPE))).reshape(Q_LORA, QW)
    ukv = w_ukv.reshape(KV_LORA, N_HEADS, QK_NOPE + V_HEAD)
    wkv = jnp.zeros((DKV, N_HEADS, 2 * HEAD_PAD), BF16)
    wkv = wkv.at[:KV_LORA, :, :QK_NOPE].set(ukv[:, :, :QK_NOPE])
    wkv = wkv.at[:KV_LORA, :, HEAD_PAD:HEAD_PAD + V_HEAD].set(ukv[:, :, QK_NOPE:])
    eye = jnp.broadcast_to(jnp.eye(QK_ROPE, dtype=BF16)[:, None, :], (QK_ROPE, N_HEADS, QK_ROPE))
    wkv = wkv.at[KV_LORA:, :, QK_NOPE:QK_NOPE + QK_ROPE].set(eye).reshape(DKV, KVW)
    wo = jnp.pad(w_o.reshape(N_HEADS, V_HEAD, D_MODEL),
                 ((0, 0), (0, HEAD_PAD - V_HEAD), (0, 0))).reshape(QW, D_MODEL)
    return dict(wd=wd, wq=wq, wkv=wkv, wo=wo)


def _mla_weight_grads(g_wd, g_wq, g_wkv, g_wo):
    g_uq = g_wq.reshape(Q_LORA, N_HEADS, HEAD_PAD)[:, :, :QK_NOPE + QK_ROPE].reshape(Q_LORA, -1)
    t = g_wkv.reshape(DKV, N_HEADS, 2 * HEAD_PAD)[:KV_LORA]
    g_ukv = jnp.concatenate([t[:, :, :QK_NOPE], t[:, :, HEAD_PAD:HEAD_PAD + V_HEAD]], axis=2)
    g_o = g_wo.reshape(N_HEADS, HEAD_PAD, D_MODEL)[:, :V_HEAD].reshape(N_HEADS * V_HEAD, D_MODEL)
    return dict(mla_w_dq=g_wd[:, :Q_LORA], mla_w_uq=g_uq, mla_w_dkv=g_wd[:, Q_LORA:],
                mla_w_ukv=g_ukv.reshape(KV_LORA, -1), mla_w_o=g_o)


def _rope_epilogue(kv):
    def epi(acc, cos, sa, sb):
        parts = []
        for t in range(acc.shape[1] // HEAD_PAD):
            x = acc[:, t * HEAD_PAD:(t + 1) * HEAD_PAD]
            if kv:
                parts.append(x if t % 2 else _rope(x, cos, sa, sb))
            else:
                parts.append(_rope(x, cos, sa, sb) * SCALE_LOG2E)
        return (jnp.concatenate(parts, axis=1),)
    return epi


def _mla_fwd(tag, h, P, rope, before_attention=None):
    S = h.shape[0]
    cos, sa, sb = rope
    tp = min(512, S)
    tabs = [pl.BlockSpec((tp, HEAD_PAD), lambda i, j, k: (i, 0))] * 3
    cqkv = _mm(f"mla_down{tag}", h, P['wd'], 'nn', S, DQKV, D_MODEL)

    def norms(x, qg, kg):
        xq, xk, xr = x[:, :Q_LORA], x[:, Q_LORA:Q_LORA + KV_LORA], x[:, Q_LORA + KV_LORA:]
        cq = xq * lax.rsqrt(_rowmean(xq * xq) + NORM_EPS) * qg
        ck = xk * lax.rsqrt(_rowmean(xk * xk) + NORM_EPS) * kg
        return (cq, jnp.concatenate([ck, xr], axis=1)), ()

    (cq, ckv), _ = _rowk(f"mla_norms{tag}", norms, [cqkv], [P['qg'], P['kg']], [(Q_LORA, BF16), (DKV, BF16)], [])
    early = cos[:8] if before_attention is None else before_attention(cq)
    rope_q = _rope_epilogue(False)
    qf = _mm(f"mla_q{tag}", cq, P['wq'], 'nn', S, QW, Q_LORA, tm=tp, tn=QW, extras=[cos, sa, sb, early],
             extra_specs=tabs + [pl.BlockSpec(early.shape, lambda i, j, k: (0, 0))],
             epi=lambda acc, cv, sav, sbv, _: rope_q(acc, cv, sav, sbv), outs=[jax.ShapeDtypeStruct((S, QW), BF16)])
    kvf = _mm(f"mla_kv{tag}", ckv, P['wkv'], 'nn', S, KVW, DKV, tm=tp, tn=KVW, extras=[cos, sa, sb],
              extra_specs=tabs, epi=_rope_epilogue(True), outs=[jax.ShapeDtypeStruct((S, KVW), BF16)])
    o, ob, lse = _attn_fwd(qf, kvf, early)
    y = _mm(f"mla_o{tag}", ob, P['wo'], 'nn', S, D_MODEL, QW)
    return y, dict(cqkv=cqkv, cq=cq, ckv=ckv, qf=qf, kvf=kvf, o=o, ob=ob, lse=lse)


def _mla_bwd(tag, dy, h, sv, P, rope):
    S = h.shape[0]
    nq = S // BQ
    cos, sa, sb = rope
    g_wo = _mm(f"mla_o_wg{tag}", sv['ob'], dy, 'tn', QW, D_MODEL, S)
    dob = _mm(f"mla_o_dg{tag}", dy, P['wo'], 'nt', S, QW, D_MODEL, outs=[jax.ShapeDtypeStruct((S, QW), BF16)])
    dd = _attn_delta(dob, sv['o'])
    dq, dkv = _attn_bwd(sv['qf'], sv['kvf'], dob, sv['lse'], dd, cos, sa, sb)
    g_wq = _mm(f"mla_q_wg{tag}", sv['cq'], dq, 'tn', Q_LORA, QW, S)
    dcq = _mm(f"mla_q_dg{tag}", dq, P['wq'], 'nt', S, Q_LORA, QW)
    g_wkv = _mm(f"mla_kv_wg{tag}", sv['ckv'], dkv, 'tn', DKV, KVW, S)
    dckv = _mm(f"mla_kv_dg{tag}", dkv, P['wkv'], 'nt', S, DKV, KVW)

    def norms_bwd(dcq_v, dckv_v, x, qg, kg):
        xq, xk = x[:, :Q_LORA], x[:, Q_LORA:Q_LORA + KV_LORA]
        dxq, dqg = _gain_bwd(dcq_v, xq, qg)
        dxk, dkg = _gain_bwd(dckv_v[:, :KV_LORA], xk, kg)
        return (jnp.concatenate([dxq, dxk, dckv_v[:, KV_LORA:]], axis=1),), (dqg, dkg)

    (dcqkv,), (dqg, dkg) = _rowk(f"mla_norms_bwd{tag}", norms_bwd, [dcq, dckv, sv['cqkv']], [P['qg'], P['kg']],
                                 [(DQKV, BF16)], [Q_LORA, KV_LORA])
    g_wd = _mm(f"mla_down_wg{tag}", h, dcqkv, 'tn', D_MODEL, DQKV, S)
    dh = _mm(f"mla_down_dg{tag}", dcqkv, P['wd'], 'nt', S, D_MODEL, DQKV)
    grads = _mla_weight_grads(g_wd, g_wq, g_wkv, g_wo)
    grads.update(mla_q_norm_g=dqg.reshape(-1), mla_kv_norm_g=dkg.reshape(-1))
    return dh, grads


def _conv_fwd(h, P):
    S = h.shape[0]
    a = _mm("conv_pw1", h, P['w_pw1'], 'nn', S, 2 * D_MODEL, D_MODEL, extras=[P['b_pw1']],
            extra_specs=[_bias_spec(1024)], epi=lambda acc, b: (acc + b,))
    (u0,), _ = _rowk("conv_glu", lambda av: ((av[:, :D_MODEL] * _sigmoid(av[:, D_MODEL:]),), ()),
                     [a], [], [(D_MODEL, F32)], [])
    u1 = _dwconv_fwd(u0, P['w_dw'], P['b_dw'])

    def ln_silu(u, g, b):
        xc = u - _rowmean(u)
        z = xc * lax.rsqrt(_rowmean(xc * xc) + NORM_EPS) * g + b
        return (z * _sigmoid(z),), ()

    (u3,), _ = _rowk("conv_ln", ln_silu, [u1], [P['ln_g'], P['ln_b']], [(D_MODEL, BF16)], [])
    y = _mm("conv_pw2", u3, P['w_pw2'], 'nn', S, D_MODEL, D_MODEL, extras=[P['b_pw2']],
            extra_specs=[_bias_spec(1024)], epi=lambda acc, b: (acc + b,))
    return y, dict(a=a, u0=u0, u1=u1, u3=u3)


def _conv_bwd(dy, dy_colsum, h, sv, P):
    S = h.shape[0]
    g_pw2 = _mm("conv_pw2_wg", sv['u3'], dy, 'tn', D_MODEL, D_MODEL, S)
    du3 = _mm("conv_pw2_dg", dy, P['w_pw2'], 'nt', S, D_MODEL, D_MODEL)

    def ln_bwd(d3, u, g, b):
        xc = u - _rowmean(u)
        rstd = lax.rsqrt(_rowmean(xc * xc) + NORM_EPS)
        xh = xc * rstd
        z = xh * g + b
        sg = _sigmoid(z)
        dz = d3 * (sg * (1.0 + z * (1.0 - sg)))
        dxh = dz * g
        du = rstd * (dxh - _rowmean(dxh) - xh * _rowmean(dxh * xh))
        return (du,), (_colsum(dz * xh), _colsum(dz), _colsum(du))

    (du1,), (d_lng, d_lnb, d_bdw) = _rowk("conv_ln_bwd", ln_bwd, [du3, sv['u1']], [P['ln_g'], P['ln_b']],
                                          [(D_MODEL, F32)], [D_MODEL] * 3)
    du0, d_wdw = _dwconv_bwd(du1, sv['u0'], P['w_dw'])

    def glu_bwd(d0, av):
        a1, sg = av[:, :D_MODEL], _sigmoid(av[:, D_MODEL:])
        da = jnp.concatenate([d0 * sg, d0 * a1 * sg * (1.0 - sg)], axis=1)
        return (da,), (_colsum(da),)

    (da,), (d_bpw1,) = _rowk("conv_glu_bwd", glu_bwd, [du0, sv['a']], [], [(2 * D_MODEL, BF16)], [2 * D_MODEL])
    g_pw1 = _mm("conv_pw1_wg", h, da, 'tn', D_MODEL, 2 * D_MODEL, S)
    dh = _mm("conv_pw1_dg", da, P['w_pw1'], 'nt', S, D_MODEL, 2 * D_MODEL)
    grads = dict(conv_w_pw1=g_pw1, conv_b_pw1=d_bpw1.reshape(-1), conv_w_dw=d_wdw[:CONV_WIDTH],
                 conv_b_dw=d_bdw.reshape(-1), conv_ln_g=d_lng.reshape(-1), conv_ln_b=d_lnb.reshape(-1),
                 conv_w_pw2=g_pw2, conv_b_pw2=dy_colsum.reshape(-1))
    return dh, grads


def _pool_group_specs(tm):
    return (pl.BlockSpec((tm, POOL_C), lambda i, j, k: (i, j)),
            pl.BlockSpec((None, POOL_C, POOL_C), lambda i, j, k: (j, 0, 0)))


def _pool_mixer_fwd(h, P):
    S = h.shape[0]
    p = _pool_fwd(h)
    a_spec, b_spec = _pool_group_specs(min(1024, S))
    y, z = _mm("pool_mm", p, P['w'], 'nn', S, D_MODEL, POOL_C, tn=POOL_C, a_spec=a_spec, b_spec=b_spec,
               extras=[P['b'], P['scale']], extra_specs=[_bias_spec(POOL_C)] * 2,
               epi=lambda acc, b, s: ((acc + b) * s, acc + b),
               outs=[jax.ShapeDtypeStruct((S, D_MODEL), F32)] * 2)
    return y, dict(p=p, z=z)


def _pool_mixer_bwd(dy, sv, P):
    S = dy.shape[0]

    def scale_bwd(d, z, s):
        dz = d * s
        return (dz,), (_colsum(d * z), _colsum(dz))

    (dz,), (d_scale, d_b) = _rowk("pool_scale_bwd", scale_bwd, [dy, sv['z']], [P['scale']],
                                  [(D_MODEL, BF16)], [D_MODEL] * 2)
    a_spec, b_spec = _pool_group_specs(min(1024, S))
    dp = _mm("pool_mm_dg", dz, P['w'], 'nt', S, D_MODEL, POOL_C, tn=POOL_C, a_spec=a_spec, b_spec=b_spec)
    tk = min(512, S)
    grp = pl.BlockSpec((tk, POOL_C), lambda i, j, k: (k, j))
    g_w = _mm("pool_mm_wg", sv['p'], dz, 'tn', POOL_C, D_MODEL, S, tn=POOL_C, tk=tk, a_spec=grp, b_spec=grp,
              outs=[jax.ShapeDtypeStruct((len(POOL_WINDOWS), POOL_C, POOL_C), F32)],
              out_specs=[pl.BlockSpec((None, POOL_C, POOL_C), lambda i, j, k: (j, 0, 0))])
    dh = _pool_bwd(dp)
    return dh, dict(pool_w=g_w, pool_b=d_b.reshape(-1), pool_scale=d_scale.reshape(-1))


def _adamw(w, g, m, v):
    m2 = ADAM_B1 * m + (1.0 - ADAM_B1) * g
    v2 = ADAM_B2 * v + (1.0 - ADAM_B2) * (g * g)
    m_hat = m2 / (1.0 - ADAM_B1 ** ADAM_STEP)
    v_hat = v2 / (1.0 - ADAM_B2 ** ADAM_STEP)
    delta = -ADAM_LR * (m_hat / (jnp.sqrt(v_hat) + ADAM_EPS) + ADAM_WD * w)
    return delta, m2, v2


def _finish(name, w, land, m, v, layer=None, prev=None):
    local = land.shape[1:]
    C = local[-1]
    R = land[0].size // C
    tr = 64 if R % 64 == 0 else R

    def body(land_hbm, g_hbm, land_v, g_v, recv_v, io_sem, send_sem, recv_sem):
        load = pltpu.make_async_copy(land_hbm, land_v, io_sem)
        load.start()
        load.wait()

        def rows_of(i):
            return pl.ds(pl.multiple_of(i * tr, tr), tr)

        def sum_chunk(i, carry):
            rows = rows_of(i)
            g_v[rows, :] = ((land_v[0, rows, :].astype(F32) + land_v[1, rows, :].astype(F32))
                            + land_v[2, rows, :].astype(F32)) + land_v[3, rows, :].astype(F32)
            return carry

        lax.fori_loop(0, R // tr, sum_chunk, 0)
        swap = pltpu.make_async_remote_copy(
            src_ref=g_v, dst_ref=recv_v, send_sem=send_sem, recv_sem=recv_sem,
            device_id=(lax.axis_index("x"), lax.axis_index("y"), 1 - lax.axis_index("c")),
            device_id_type=pl.DeviceIdType.MESH)
        swap.start()
        swap.wait()

        def add_chunk(i, carry):
            rows = rows_of(i)
            recv_v[rows, :] = g_v[rows, :] + recv_v[rows, :]
            return carry

        lax.fori_loop(0, R // tr, add_chunk, 0)
        store = pltpu.make_async_copy(recv_v, g_hbm, io_sem)
        store.start()
        store.wait()

    any_spec = pl.BlockSpec(memory_space=pl.ANY)
    g = pl.pallas_call(
        body, name=name, in_specs=[any_spec], out_specs=any_spec, out_shape=jax.ShapeDtypeStruct((R, C), F32),
        scratch_shapes=[pltpu.VMEM((4, R, C), BF16), pltpu.VMEM((R, C), F32), pltpu.VMEM((R, C), F32),
                        pltpu.SemaphoreType.DMA, pltpu.SemaphoreType.DMA, pltpu.SemaphoreType.DMA],
        compiler_params=pltpu.CompilerParams(has_side_effects=True, vmem_limit_bytes=VMEM_LIMIT))(
            land.reshape(4, R, C))

    lead = () if layer is None else (w.shape[0],)
    as2d = lambda a: a.reshape(lead + (R, C))
    tu = _row_tile(R, C, budget=1 << 19)
    tile = pl.BlockSpec((tu, C), lambda i: (i, 0))
    slab = tile if layer is None else pl.BlockSpec((None, tu, C), lambda i: (layer, i, 0))
    n_prev = 0 if prev is None else 4

    def update(w_ref, g_ref, m_ref, v_ref, *rest):
        outs = rest[n_prev:]
        gv = g_ref[...]
        d, nm, nv = _adamw(w_ref[...], gv, m_ref[...], v_ref[...])
        for r, val in zip(outs, (gv, d, nm, nv)):
            r[...] = val

    res = pl.pallas_call(
        update, name=name + "_adamw", grid=(R // tu,),
        in_specs=[slab, tile, slab, slab] + [any_spec] * n_prev, out_specs=[slab] * 4,
        out_shape=[jax.ShapeDtypeStruct(lead + (R, C), F32)] * 4,
        input_output_aliases={4 + k: k for k in range(n_prev)},
        compiler_params=_cparams("arbitrary"))(
            as2d(w), g, as2d(m), as2d(v), *([] if prev is None else [as2d(p) for p in prev]))
    return [r.reshape(w.shape) for r in res]


def _row(v):
    return v.reshape(1, -1)


def kernel(x, c, positions, ada_w, ada_b, norm_g, mla_w_dq, mla_q_norm_g, mla_w_uq, mla_w_dkv, mla_kv_norm_g, mla_w_ukv, mla_w_o, conv_w_pw1, conv_b_pw1, conv_w_dw, conv_b_dw, conv_ln_g, conv_ln_b, conv_w_pw2, conv_b_pw2, pool_w, pool_b, pool_scale, ffn_w1, ffn_w2, loss_target, m_ada_w, m_ada_b, m_norm_g, m_mla_w_dq, m_mla_q_norm_g, m_mla_w_uq, m_mla_w_dkv, m_mla_kv_norm_g, m_mla_w_ukv, m_mla_w_o, m_conv_w_pw1, m_conv_b_pw1, m_conv_w_dw, m_conv_b_dw, m_conv_ln_g, m_conv_ln_b, m_conv_w_pw2, m_conv_b_pw2, m_pool_w, m_pool_b, m_pool_scale, m_ffn_w1, m_ffn_w2, v_ada_w, v_ada_b, v_norm_g, v_mla_w_dq, v_mla_q_norm_g, v_mla_w_uq, v_mla_w_dkv, v_mla_kv_norm_g, v_mla_w_ukv, v_mla_w_o, v_conv_w_pw1, v_conv_b_pw1, v_conv_w_dw, v_conv_b_dw, v_conv_ln_g, v_conv_ln_b, v_conv_w_pw2, v_conv_b_pw2, v_pool_w, v_pool_b, v_pool_scale, v_ffn_w1, v_ffn_w2):
    args = dict(locals())
    W = {n: args[n] for n in WEIGHTS}
    MOM = {n: args['m_' + n] for n in WEIGHTS}
    VAR = {n: args['v_' + n] for n in WEIGHTS}
    S = x.shape[1]
    xs = x.reshape(S, D_MODEL)
    tgt = loss_target.reshape(S, D_MODEL)
    mx, my, mc = lax.axis_index("x"), lax.axis_index("y"), lax.axis_index("c")
    chip = 2 * mx + my
    n_sh = ada_w.shape[2]

    def sent_of(key):
        n, l = key
        arr = W[n] if l is None else W[n][l]
        return arr.astype(BF16) if n in BIG or n in ('ffn_w1', 'ffn_w2') else arr

    keys0 = [(n, 0) for n in MLA_MATS] + [(n, None) for n in ('norm_g', 'mla_q_norm_g', 'mla_kv_norm_g',
                                                               'conv_w_dw', 'pool_b', 'pool_scale')]
    c8 = _exchange("gather_c", [c.reshape(8, D_MODEL // 8)], 'xyc')[0].reshape(8, D_MODEL)
    sent0 = [sent_of(k) for k in keys0]
    sent0[-1] = sent0[-1] + jnp.minimum(jnp.abs(c8[0, 0]), 0.0)
    fly0 = _split_start("gather_w0_start", sent0)
    c8 = c8 + fly0['token'][0, 0]
    c8 = jnp.pad(c8, ((0, ADA_ROWS - 8), (0, 0)))
    silu = lambda v: v * _sigmoid(v)
    mod_sh = []
    for l in range(DEPTH):
        b_l = lax.dynamic_slice(ada_b[l], (chip * n_sh,), (n_sh,)).reshape(1, n_sh)
        mod_sh.append(_mm(f"ada_fwd{l}", c8, ada_w, 'nn', ADA_ROWS, n_sh, D_MODEL, tn=n_sh // 2, tk=512, pro_a=silu,
                          b_spec=pl.BlockSpec((None, 512, n_sh // 2), lambda i, j, k, l=l: (l, k, j)),
                          extras=[b_l], extra_specs=[_bias_spec(n_sh // 2)], epi=lambda acc, b: (acc + b,))[:8])
    mod_sh = jnp.stack(mod_sh, axis=1).reshape(8, DEPTH * n_sh // 128, 128)
    mod = _exchange("scatter_mod", [mod_sh], 'xy', src_by='xyc')[0]
    mod = mod.reshape(4, DEPTH, n_sh).transpose(1, 0, 2).reshape(DEPTH, 6, 1, D_MODEL)

    keys1 = [('ffn_w1', 0), ('ffn_w2', 0), ('conv_w_pw1', None), ('conv_w_pw2', None), ('pool_w', None)]
    keys2 = [('ffn_w1', 1), ('ffn_w2', 1)]
    keys3 = [(n, l) for l in range(2, DEPTH) for n in ('ffn_w1', 'ffn_w2')] + [(n, 1) for n in MLA_MATS]
    fly1 = _split_start("gather_w1_start", [sent_of(k) for k in keys1])
    mod = mod + fly1['token'][0, 0]
    late = {}

    def start_group2(dep):
        sent2 = [sent_of(k) for k in keys2]
        sent2[-1] = sent2[-1] + jnp.minimum(jnp.abs(dep[0, 0]), 0).astype(BF16)
        late['fly2'] = _split_start("gather_w2_start", sent2)
        sent3 = [sent_of(k) for k in keys3]
        sent3[-1] = sent3[-1] + jnp.minimum(jnp.abs(dep[0, 0]), 0).astype(BF16)
        late['fly3'] = _split_start("gather_w3_start", sent3)
        return late['fly2']['token'] + late['fly3']['token']
    G = dict(zip(keys0, _split_wait("gather_w0_wait", fly0, mod)))

    def whole(key):
        n, l = key
        return _unshard(G[key], SHARD_AXIS[n] - (0 if l is None else 1))

    def mla_params(j):
        P = _mla_weights(*[whole((n, j)) for n in ('mla_w_dq', 'mla_w_dkv', 'mla_w_uq', 'mla_w_ukv', 'mla_w_o')])
        P.update(qg=_row(whole(('mla_q_norm_g', None))[j]), kg=_row(whole(('mla_kv_norm_g', None))[j]))
        return P

    gains = whole(('norm_g', None))
    mla_p = {0: mla_params(0)}
    conv_p = pool_p = None
    rope = _rope_tables(positions.reshape(S, 1).astype(F32))

    by_j = pl.BlockSpec((None, 1024, 1024), lambda i, j, k: (j, 0, 0))
    by_k = pl.BlockSpec((None, 1024, 1024), lambda i, j, k: (k, 0, 0))
    sq_relu = lambda v: jnp.square(jnp.maximum(v, 0.0))

    def md(i, k):
        return mod[i, k]

    (h,), _ = _rowk("pre0", lambda xv, g, sc, sh: ((_pre_fwd(xv, g, sc, sh),), ()),
                    [xs], [_row(gains[0, 0]), md(0, 1), md(0, 0)], [(D_MODEL, BF16)], [])
    saved = []
    xin = xs
    loss_acc = dxf = None
    for i in range(DEPTH):
        kind, j = i % 3, i // 3
        if kind == 0:
            if j not in mla_p:
                mla_p[j] = mla_params(j)
            y, sv = _mla_fwd(j, h, mla_p[j], rope, before_attention=start_group2 if i == 0 else None)
        elif kind == 1:
            y, sv = _conv_fwd(h, conv_p)
        else:
            y, sv = _pool_mixer_fwd(h, pool_p)

        def mid(xv, yv, gt, g1, g2, sc, sh):
            x1 = _post_fwd(xv, yv, gt, g1)
            return (x1, _pre_fwd(x1, g2, sc, sh)), ()

        (x1, h2), _ = _rowk(f"mid{i}", mid, [xin, y], [md(i, 2), _row(gains[i, 1]), _row(gains[i, 2]), md(i, 4), md(i, 3)],
                            [(D_MODEL, F32), (D_MODEL, BF16)], [])
        if i == 0:
            G.update(zip(keys1, _split_wait("gather_w1_wait", fly1, h2)))
            conv_p = dict(w_pw1=whole(('conv_w_pw1', None))[0], b_pw1=_row(conv_b_pw1[0]),
                          w_dw=whole(('conv_w_dw', None))[0], b_dw=_row(conv_b_dw[0]), ln_g=_row(conv_ln_g[0]),
                          ln_b=_row(conv_ln_b[0]), w_pw2=whole(('conv_w_pw2', None))[0], b_pw2=_row(conv_b_pw2[0]))
            pool_p = dict(w=whole(('pool_w', None))[0], b=_row(whole(('pool_b', None))[0]),
                          scale=_row(whole(('pool_scale', None))[0]))
        if i == 1:
            G.update(zip(keys2, _split_wait("gather_w2_wait", late['fly2'], h2)))
        if i == 2:
            G.update(zip(keys3, _split_wait("gather_w3_wait", late['fly3'], h2)))
        a = _mm(f"ffn1_{i}", h2, G[('ffn_w1', i)], 'nn', S, D_FF, D_MODEL, tm=2048, b_spec=by_j,
                outs=[jax.ShapeDtypeStruct((S, D_FF), BF16)])
        y2 = _mm(f"ffn2_{i}", a, G[('ffn_w2', i)], 'nn', S, D_MODEL, D_FF, pro_a=sq_relu, b_spec=by_k)
        saved.append(dict(x0=xin, h=h, y=y, x1=x1, h2=h2, a=a, y2=y2, mix=sv))
        if i + 1 < DEPTH:
            def nxt(xv, yv, gt, g3, g0, sc, sh):
                x2 = _post_fwd(xv, yv, gt, g3)
                return (x2, _pre_fwd(x2, g0, sc, sh)), ()

            hdt = F32 if (i + 1) % 3 == 2 else BF16
            (xin, h), _ = _rowk(f"next{i}", nxt, [x1, y2],
                                [md(i, 5), _row(gains[i, 3]), _row(gains[i + 1, 0]), md(i + 1, 1), md(i + 1, 0)],
                                [(D_MODEL, F32), (D_MODEL, hdt)], [])
        else:
            def head(xv, yv, tv, gt, g3):
                err = _post_fwd(xv, yv, gt, g3) - tv
                per_row = jnp.sum(err * err, axis=1, keepdims=True) * (0.5 / D_MODEL)
                return (err * (1.0 / D_MODEL),), (jnp.broadcast_to(jnp.sum(per_row, axis=0, keepdims=True), (1, 128)),)

            (dxf,), (loss_acc,) = _rowk("loss_head", head, [x1, y2, tgt], [md(i, 5), _row(gains[i, 3])],
                                        [(D_MODEL, F32)], [128])

    small = {}
    big = {}
    landed = {}

    def keep(gm, layer):
        for n, g in gm.items():
            if n in BIG:
                g = g[None] if layer is None else g
                big[(n, layer)] = _to_shards(g, SHARD_AXIS[n] - (0 if layer is None else 1)).astype(BF16)
            else:
                small.setdefault(n, {})[layer or 0] = g

    d_mod = [None] * DEPTH
    d_gain = [None] * DEPTH
    dx = dxf
    for i in reversed(range(DEPTH)):
        kind, j = i % 3, i // 3
        sv = saved[i]
        def post2_bwd(d, yv, gt, g):
            dyv, d_gt, d_g = _post_bwd(d, yv, gt, g)
            return (dyv,), (d_gt, d_g)

        (dy2,), (d_gtf, d_g3) = _rowk(f"post2_bwd{i}", post2_bwd, [dx, sv['y2']], [md(i, 5), _row(gains[i, 3])],
                                      [(D_MODEL, BF16)], [D_MODEL] * 2)
        da = _mm(f"ffn2_dg{i}", dy2, G[('ffn_w2', i)], 'nt', S, D_FF, D_MODEL, tm=2048, b_spec=by_j, extras=[sv['a']],
                 extra_specs=[pl.BlockSpec((min(2048, S), 1024), lambda i_, j_, k_: (i_, j_))],
                 epi=lambda acc, av: (acc * (2.0 * jnp.maximum(av, 0.0)),),
                 outs=[jax.ShapeDtypeStruct((S, D_FF), BF16)])
        big[('ffn_w2', i)] = _mm(f"ffn2_wg{i}", sv['a'], dy2, 'tn', D_FF, D_MODEL, S, tk=2048, pro_a=sq_relu,
                        outs=[jax.ShapeDtypeStruct((4, 1024, D_MODEL), BF16)],
                        out_specs=[pl.BlockSpec((None, 1024, 1024), lambda i_, j_, k_: (i_, 0, j_))])
        big[('ffn_w1', i)] = _mm(f"ffn1_wg{i}", sv['h2'], da, 'tn', D_MODEL, D_FF, S, tk=2048,
                        outs=[jax.ShapeDtypeStruct((4, D_MODEL, 1024), BF16)],
                        out_specs=[pl.BlockSpec((None, 1024, 1024), lambda i_, j_, k_: (j_, i_, 0))])
        dh2 = _mm(f"ffn1_dg{i}", da, G[('ffn_w1', i)], 'nt', S, D_MODEL, D_FF, tm=2048, b_spec=by_k)
        if i == DEPTH - 1:
            keys_a = [('ffn_w1', i), ('ffn_w2', i)]
            fly_a = _split_start("scatter_ga_start", [big[k] for k in keys_a], src_by='xy')
            mod = mod + fly_a['token'][0, 0]
        if i == 0:
            keys_b = [k for k in big if k not in keys_a]
            fly_b = _split_start("scatter_gb_start", [big[k] for k in keys_b], src_by='xy')
            mod = mod + fly_b['token'][0, 0]

        def mid_bwd(d2, dh2v, x1v, yv, g2, scf, gtm, g1):
            dpre, d_sh, d_sc, d_g2 = _pre_bwd(dh2v, x1v, g2, scf)
            d1 = d2 + dpre
            dyv, d_gt, d_g1 = _post_bwd(d1, yv, gtm, g1)
            return (d1, dyv), (d_sh, d_sc, d_g2, d_gt, d_g1, _colsum(dyv))

        ydt = F32 if kind == 2 else BF16
        (dx1, dy), (d_shf, d_scf, d_g2, d_gtm, d_g1, dy_cs) = _rowk(
            f"mid_bwd{i}", mid_bwd, [dx, dh2, sv['x1'], sv['y']],
            [_row(gains[i, 2]), md(i, 4), md(i, 2), _row(gains[i, 1])],
            [(D_MODEL, F32), (D_MODEL, ydt)], [D_MODEL] * 6)
        if kind == 0:
            dh, gm = _mla_bwd(j, dy, sv['h'], sv['mix'], mla_p[j], rope)
            keep(gm, j)
        elif kind == 1:
            dh, gm = _conv_bwd(dy, dy_cs, sv['h'], sv['mix'], conv_p)
            keep(gm, None)
        else:
            dh, gm = _pool_mixer_bwd(dy, sv['mix'], pool_p)
            keep(gm, None)

        def pre_bwd(d1, dhv, x0v, g0, scm):
            dpre, d_sh, d_sc, d_g0 = _pre_bwd(dhv, x0v, g0, scm)
            return (d1 + dpre,), (d_sh, d_sc, d_g0)

        (dx,), (d_shm, d_scm, d_g0) = _rowk(f"pre_bwd{i}", pre_bwd, [dx1, dh, sv['x0']],
                                            [_row(gains[i, 0]), md(i, 1)], [(D_MODEL, F32)], [D_MODEL] * 3)
        d_mod[i] = jnp.concatenate([d_shm, d_scm, d_gtm, d_shf, d_scf, d_gtf], axis=1).reshape(-1)
        d_gain[i] = jnp.concatenate([d_g0, d_g1, d_g2, d_g3], axis=0)
        if i == DEPTH - 1:
            landed.update(zip(keys_a, _split_wait("scatter_ga_wait", fly_a, dx)))
    landed.update(zip(keys_b, _split_wait("scatter_gb_wait", fly_b, dx)))
    keys_c = [(n, 0) for n in MLA_MATS]
    fly_c = _split_start("scatter_gc_start", [big[k] for k in keys_c], src_by='xy')
    grad_x = dx.reshape(x.shape)
    grads = {n: jnp.stack([g[l] for l in sorted(g)]) for n, g in small.items()}
    grads['norm_g'] = jnp.stack(d_gain)
    grads['ada_b'] = jnp.stack(d_mod)

    pack = jnp.concatenate([grads[n].reshape(-1) for n in SMALL] + [loss_acc[0, :1]])
    n_pack = pack.shape[0]
    rows = -(-n_pack // 1024) * 8
    pack = jnp.pad(pack, (0, rows * 128 - n_pack)).reshape(rows, 128)
    fly_s = _split_start("gather_small_start", [pack], group='xyc')

    out_g, out_d, out_m, out_v = {}, {}, {}, {}
    chains = {}
    for n in BIG + ['ffn_w1', 'ffn_w2']:
        if n in MLA_MATS:
            chains[n] = _finish(f"finish_{n}1", W[n], landed[(n, 1)], MOM[n], VAR[n], layer=1)
        elif n in BIG:
            out_g[n], out_d[n], out_m[n], out_v[n] = _finish(f"finish_{n}", W[n], landed[(n, None)], MOM[n], VAR[n])
        else:
            res = None
            for l in range(DEPTH):
                res = _finish(f"finish_{n}{l}", W[n], landed[(n, l)], MOM[n], VAR[n], layer=l, prev=res)
            out_g[n], out_d[n], out_m[n], out_v[n] = res
    landed.update(zip(keys_c, _split_wait("scatter_gc_wait", fly_c, out_g['ffn_w2'])))
    for n in MLA_MATS:
        out_g[n], out_d[n], out_m[n], out_v[n] = _finish(f"finish_{n}0", W[n], landed[(n, 0)], MOM[n], VAR[n],
                                                         layer=0, prev=chains[n])
    pack8 = _split_wait("gather_small_wait", fly_s, out_g['mla_w_o'])[0]
    (tot,) = _ew("sum_small", lambda *v: (functools.reduce(lambda p, q: p + q, v),), [(pack8, s) for s in range(8)],
                 [F32], (rows, 128))
    tot = tot.reshape(-1)
    loss = tot[n_pack - 1]
    d_mod_all = pack8.reshape(8, -1)[:, :DEPTH * 6 * D_MODEL].reshape(8, DEPTH, 6 * D_MODEL)
    final = {}
    off = 0
    for n in SMALL:
        ax = SHARD_AXIS[n]
        shape = tuple(d * 4 if k == ax else d for k, d in enumerate(W[n].shape))
        size = grads[n].size
        g = tot[off:off + size].reshape(shape)
        off += size
        if ax is not None:
            g = lax.dynamic_index_in_dim(_to_shards(g, ax), chip, 0, keepdims=False)
        final[n] = g

    g_ada = []
    for l in range(DEPTH):
        dm_l = jnp.pad(lax.dynamic_slice(d_mod_all[:, l], (0, chip * n_sh), (8, n_sh)), ((0, ADA_ROWS - 8), (0, 0)))
        g_ada.append(_mm(f"ada_wg{l}", c8, dm_l, 'tn', D_MODEL, n_sh, ADA_ROWS, tn=n_sh // 2, pro_a=silu))
    final['ada_w'] = jnp.stack(g_ada)

    for n in WEIGHTS:
        if n in out_g:
            continue
        shape = W[n].shape
        out_g[n] = final[n].reshape(shape)
        out_d[n], out_m[n], out_v[n] = _ew(f"adamw_{n}", lambda w, g, m, v: _adamw(w, g, m, v),
                                           [W[n], out_g[n], MOM[n], VAR[n]], [F32] * 3, shape)
    return (loss, grad_x, *[out_g[n] for n in WEIGHTS], *[out_d[n] for n in WEIGHTS],
            *[out_m[n] for n in WEIGHTS], *[out_v[n] for n in WEIGHTS])
```

```python
import functools
import math

import jax
import jax.numpy as jnp
from jax import lax
from jax.experimental import pallas as pl
from jax.experimental.pallas import tpu as pltpu

F32 = jnp.float32
BF16 = jnp.bfloat16

D_MODEL = 1024
DEPTH = 4
N_HEADS = 16
QK_NOPE = 64
QK_ROPE = 32
V_HEAD = 64
Q_LORA = 384
KV_LORA = 256
HEAD_PAD = 128
QW = N_HEADS * HEAD_PAD
KVW = 2 * QW
DKV = KV_LORA + QK_ROPE
DQKV = Q_LORA + DKV
D_FF = 4096
CONV_WIDTH = 31
POOL_WINDOWS = (2, 4, 8, 16)
CHUNK_SHIFT = 6
ROPE_THETA = 10000.0
NORM_EPS = 1e-6
NEG_INF = -1e30
ATT_SCALE = 1.0 / math.sqrt(QK_NOPE + QK_ROPE)
BQ = 256
HB = 8
HF = 8
LOG2E = 1.4426950408889634
SCALE_LOG2E = ATT_SCALE * LOG2E
PAD_ROWS = 32
ADA_ROWS = 128
VMEM_LIMIT = 56 * 1024 * 1024

ADAM_LR = 0.001
ADAM_B1 = 0.9
ADAM_B2 = 0.999
ADAM_EPS = 1e-08
ADAM_WD = 0.01
ADAM_STEP = 10

WEIGHTS = ['ada_w', 'ada_b', 'norm_g', 'mla_w_dq', 'mla_q_norm_g', 'mla_w_uq', 'mla_w_dkv', 'mla_kv_norm_g',
           'mla_w_ukv', 'mla_w_o', 'conv_w_pw1', 'conv_b_pw1', 'conv_w_dw', 'conv_b_dw', 'conv_ln_g', 'conv_ln_b',
           'conv_w_pw2', 'conv_b_pw2', 'pool_w', 'pool_b', 'pool_scale', 'ffn_w1', 'ffn_w2']
SHARD_AXIS = {'ada_w': 2, 'ada_b': None, 'norm_g': 2, 'mla_w_dq': 1, 'mla_q_norm_g': 1, 'mla_w_uq': 2,
              'mla_w_dkv': 1, 'mla_kv_norm_g': 1, 'mla_w_ukv': 2, 'mla_w_o': 1, 'conv_w_pw1': 2,
              'conv_b_pw1': None, 'conv_w_dw': 2, 'conv_b_dw': None, 'conv_ln_g': None, 'conv_ln_b': None,
              'conv_w_pw2': 1, 'conv_b_pw2': None, 'pool_w': 2, 'pool_b': 2, 'pool_scale': 1,
              'ffn_w1': 2, 'ffn_w2': 1}
MLA_MATS = ['mla_w_dq', 'mla_w_uq', 'mla_w_dkv', 'mla_w_ukv', 'mla_w_o']
BIG = MLA_MATS + ['conv_w_pw1', 'conv_w_pw2', 'pool_w']
SMALL = ['ada_b', 'norm_g', 'mla_q_norm_g', 'mla_kv_norm_g', 'conv_b_pw1', 'conv_w_dw', 'conv_b_dw',
         'conv_ln_g', 'conv_ln_b', 'conv_b_pw2', 'pool_b', 'pool_scale']


def _cparams(*sem):
    return pltpu.CompilerParams(dimension_semantics=sem, vmem_limit_bytes=VMEM_LIMIT)


def _colsum(v):
    return jnp.sum(v, axis=0, keepdims=True)


def _rowmean(v):
    return jnp.mean(v, axis=-1, keepdims=True)


def _sigmoid(v):
    return 1.0 / (1.0 + jnp.exp(-v))


def _rowk(name, fn, rows, bcast, out_row, out_acc, tm=512):
    S = rows[0].shape[0]
    tm = min(tm, S)
    while S % tm:
        tm //= 2
    nin, no, na = len(rows) + len(bcast), len(out_row), len(out_acc)

    def body(*refs):
        vals = [r[...] for r in refs[:nin]]
        outs = refs[nin:nin + no]
        accs = refs[nin + no:]
        ro, ao = fn(*vals)
        for r, v in zip(outs, ro):
            r[...] = v.astype(r.dtype)
        if na:
            @pl.when(pl.program_id(0) == 0)
            def _():
                for r in accs:
                    r[...] = jnp.zeros(r.shape, r.dtype)
            for r, v in zip(accs, ao):
                r[...] += v

    in_specs = [pl.BlockSpec((tm, a.shape[1]), lambda i: (i, 0)) for a in rows]
    in_specs += [pl.BlockSpec(b.shape, lambda i, n=b.ndim: (0,) * n) for b in bcast]
    out_shape = [jax.ShapeDtypeStruct((S, w), dt) for w, dt in out_row]
    out_shape += [jax.ShapeDtypeStruct((1, w), F32) for w in out_acc]
    out_specs = [pl.BlockSpec((tm, w), lambda i: (i, 0)) for w, _ in out_row]
    out_specs += [pl.BlockSpec((1, w), lambda i: (0, 0)) for w in out_acc]
    res = pl.pallas_call(body, name=name, grid=(S // tm,), in_specs=in_specs, out_specs=out_specs,
                         out_shape=out_shape, compiler_params=_cparams("arbitrary"))(*rows, *bcast)
    return list(res[:no]), list(res[no:])


_DIMS = {'nn': ((1,), (0,)), 'nt': ((1,), (1,)), 'tn': ((0,), (0,))}


def _mm(name, a, b, mode, M, N, K, *, tm=1024, tn=1024, tk=1024, a_spec=None, b_spec=None, pro_a=None,
        extras=(), extra_specs=(), epi=None, outs=None, out_specs=None):
    tm, tn, tk = (t if d % t == 0 else d for t, d in ((min(tm, M), M), (min(tn, N), N), (min(tk, K), K)))
    nk = K // tk
    if a_spec is None:
        a_spec = (pl.BlockSpec((tk, tm), lambda i, j, k: (k, i)) if mode == 'tn'
                  else pl.BlockSpec((tm, tk), lambda i, j, k: (i, k)))
    if b_spec is None:
        b_spec = (pl.BlockSpec((tn, tk), lambda i, j, k: (j, k)) if mode == 'nt'
                  else pl.BlockSpec((tk, tn), lambda i, j, k: (k, j)))
    if outs is None:
        outs = [jax.ShapeDtypeStruct((M, N), F32)]
    if out_specs is None:
        out_specs = [pl.BlockSpec((tm, tn), lambda i, j, k: (i, j)) for _ in outs]
    ne, no = len(extras), len(outs)
    dims = (_DIMS[mode], ((), ()))

    def body(a_ref, b_ref, *rest):
        ex, out_refs = rest[:ne], rest[ne:ne + no]
        av = a_ref[...]
        if pro_a is not None:
            av = pro_a(av)
        part = lax.dot_general(av.astype(BF16), b_ref[...].astype(BF16), dims, preferred_element_type=F32)

        def finish(acc):
            vals = (acc,) if epi is None else epi(acc, *[e[...] for e in ex])
            for r, v in zip(out_refs, vals):
                r[...] = v.astype(r.dtype)

        if nk == 1:
            finish(part)
            return
        acc_ref = rest[ne + no]
        k = pl.program_id(2)

        @pl.when(k == 0)
        def _():
            acc_ref[...] = part

        @pl.when(k > 0)
        def _():
            acc_ref[...] += part

        @pl.when(k == nk - 1)
        def _():
            finish(acc_ref[...])

    res = pl.pallas_call(
        body, name=name, grid=(M // tm, N // tn, nk),
        in_specs=[a_spec, b_spec, *extra_specs], out_specs=list(out_specs), out_shape=list(outs),
        scratch_shapes=[pltpu.VMEM((tm, tn), F32)] if nk > 1 else [],
        compiler_params=_cparams("parallel", "parallel", "arbitrary"))(a, b, *extras)
    return res[0] if no == 1 else list(res)


def _row_tile(R, C, itemsize=4, budget=1 << 20):
    if R * C * itemsize <= budget or R % 8:
        return R
    t = 8
    while R % (t * 2) == 0 and t * 2 * C * itemsize <= budget:
        t *= 2
    return t


def _ew(name, fn, ins, out_dtypes, shape):
    C = shape[-1]
    R = 1
    for s in shape[:-1]:
        R *= s
    tr = _row_tile(R, C)
    ops, specs = [], []
    for it in ins:
        if isinstance(it, tuple):
            arr, idx = it
            ops.append(arr.reshape(arr.shape[0], R, C))
            specs.append(pl.BlockSpec((None, tr, C), lambda i, n=idx: (n, i, 0)))
        else:
            ops.append(it.reshape(R, C))
            specs.append(pl.BlockSpec((tr, C), lambda i: (i, 0)))
    nin = len(ops)

    def body(*refs):
        vals = fn(*[r[...] for r in refs[:nin]])
        for r, v in zip(refs[nin:], vals):
            r[...] = v.astype(r.dtype)

    res = pl.pallas_call(
        body, name=name, grid=(R // tr,), in_specs=specs,
        out_specs=[pl.BlockSpec((tr, C), lambda i: (i, 0)) for _ in out_dtypes],
        out_shape=[jax.ShapeDtypeStruct((R, C), dt) for dt in out_dtypes],
        compiler_params=_cparams("parallel"))(*ops)
    return [r.reshape(shape) for r in res]


_FLIPS = {'xyc': [(fx, fy, fc) for fx in (0, 1) for fy in (0, 1) for fc in (0, 1)][1:],
          'xy': [(1, 0, 0), (0, 1, 0), (1, 1, 0)],
          'c': [(0, 0, 1)]}
_NSLOT = {'xyc': 8, 'xy': 4, 'c': 2}


def _slot(kind, cx, cy, cc):
    return {'xyc': 4 * cx + 2 * cy + cc, 'xy': 2 * cx + cy, 'c': cc}[kind]


def _put_own(land, arr, group, src_by):
    coords = (lax.axis_index("x"), lax.axis_index("y"), lax.axis_index("c"))
    pay = arr if src_by is None else lax.dynamic_index_in_dim(arr, _slot(src_by, *coords), 0, keepdims=False)
    return lax.dynamic_update_index_in_dim(land, pay, _slot(group, *coords), 0)


def _exchange(name, arrays, group, src_by=None):
    flips, nsl, n = _FLIPS[group], _NSLOT[group], len(arrays)
    nf = len(flips)

    def body(*refs):
        ins, outs = refs[:n], refs[n:2 * n]
        send_sems, recv_sems = refs[2 * n:]
        mx, my, mc = lax.axis_index("x"), lax.axis_index("y"), lax.axis_index("c")
        me = _slot(group, mx, my, mc)

        def payload(a, cx, cy, cc):
            return ins[a] if src_by is None else ins[a].at[_slot(src_by, cx, cy, cc)]

        sends, recvs = [], []
        for a in range(n):
            for f, (fx, fy, fc) in enumerate(flips):
                px = 1 - mx if fx else mx
                py = 1 - my if fy else my
                pc = 1 - mc if fc else mc
                src = payload(a, px, py, pc)
                sends.append(pltpu.make_async_remote_copy(
                    src_ref=src, dst_ref=outs[a].at[me], send_sem=send_sems.at[a, f],
                    recv_sem=recv_sems.at[a, f], device_id=(px, py, pc),
                    device_id_type=pl.DeviceIdType.MESH))
                recvs.append(pltpu.make_async_remote_copy(
                    src_ref=src, dst_ref=outs[a].at[_slot(group, px, py, pc)], send_sem=send_sems.at[a, f],
                    recv_sem=recv_sems.at[a, f], device_id=(px, py, pc),
                    device_id_type=pl.DeviceIdType.MESH))
        for cp in sends:
            cp.start()
        for cp in recvs:
            cp.wait_recv()
        for cp in sends:
            cp.wait_send()

    out_shape = [jax.ShapeDtypeStruct((nsl,) + (a.shape if src_by is None else a.shape[1:]), a.dtype)
                 for a in arrays]
    any_spec = pl.BlockSpec(memory_space=pl.ANY)
    res = pl.pallas_call(
        body, name=name, in_specs=[any_spec] * n, out_specs=[any_spec] * n, out_shape=out_shape,
        scratch_shapes=[pltpu.SemaphoreType.DMA((n, nf)), pltpu.SemaphoreType.DMA((n, nf))],
        compiler_params=pltpu.CompilerParams(has_side_effects=True))(*arrays)
    return [_put_own(l, a, group, src_by) for a, l in zip(arrays, res)]


_HBM = pl.BlockSpec(memory_space=pltpu.HBM)
_SEM = pl.BlockSpec(memory_space=pltpu.SEMAPHORE)
_DATAFLOW = pltpu.SideEffectType.DATAFLOW_SIDE_EFFECTING


def _group_copies(ins, lands, send_sems, recv_sems, group, src_by):
    mx, my, mc = lax.axis_index("x"), lax.axis_index("y"), lax.axis_index("c")
    me = _slot(group, mx, my, mc)
    pairs = []
    for a in range(len(ins)):
        for fx, fy, fc in _FLIPS[group]:
            peer = (1 - mx if fx else mx, 1 - my if fy else my, 1 - mc if fc else mc)
            src = ins[a] if src_by is None else ins[a].at[_slot(src_by, *peer)]
            mk = functools.partial(pltpu.make_async_remote_copy, src_ref=src, send_sem=send_sems,
                                   recv_sem=recv_sems, device_id=peer, device_id_type=pl.DeviceIdType.MESH)
            pairs.append((mk(dst_ref=lands[a].at[me]), mk(dst_ref=lands[a].at[_slot(group, *peer)])))
    return pairs


def _split_start(name, arrays, group='xy', src_by=None):
    n = len(arrays)
    lands = [lax.empty((_NSLOT[group],) + (a.shape if src_by is None else a.shape[1:]), a.dtype) for a in arrays]

    def body(*refs):
        ins, lnd, send_sems, recv_sems, token = refs[:n], refs[n:2 * n], refs[2 * n], refs[2 * n + 1], refs[-1]
        for to_peer, _ in _group_copies(ins, lnd, send_sems, recv_sems, group, src_by):
            to_peer.start()
        token[...] = jnp.zeros(token.shape, F32)

    ops = [pltpu.with_memory_space_constraint(a, pltpu.HBM) for a in [*arrays, *lands]]
    res = pl.pallas_call(
        body, name=name, in_specs=[_HBM] * (2 * n),
        out_specs=[_SEM, _SEM] + [_HBM] * (2 * n) + [pl.BlockSpec(memory_space=pltpu.VMEM)],
        out_shape=[pltpu.SemaphoreType.DMA(()), pltpu.SemaphoreType.DMA(())]
        + [pltpu.HBM(a.shape, a.dtype) for a in ops] + [jax.ShapeDtypeStruct((8, 128), F32)],
        input_output_aliases={k: 2 + k for k in range(2 * n)},
        compiler_params=pltpu.CompilerParams(has_side_effects=_DATAFLOW))(*ops)
    return dict(n=n, group=group, src_by=src_by, send=res[0], recv=res[1], arrays=list(res[2:2 + n]),
                lands=list(res[2 + n:2 + 2 * n]), token=res[-1])


def _split_wait(name, st, after):
    n, group, src_by = st['n'], st['group'], st['src_by']

    def wait_body(*refs):
        ins, lnd, send_sems, recv_sems = refs[:n], refs[n:2 * n], refs[2 * n], refs[2 * n + 1]
        for to_peer, from_peer in _group_copies(ins, lnd, send_sems, recv_sems, group, src_by):
            to_peer.wait_send()
            from_peer.wait_recv()

    shapes = [pltpu.HBM(a.shape, a.dtype) for a in [*st['arrays'], *st['lands']]]
    res = pl.pallas_call(
        wait_body, name=name, in_specs=[_HBM] * (2 * n) + [_SEM, _SEM, pl.BlockSpec(memory_space=pl.ANY)],
        out_specs=[_HBM] * (2 * n), out_shape=shapes, input_output_aliases={k: k for k in range(2 * n)},
        compiler_params=pltpu.CompilerParams(has_side_effects=_DATAFLOW))(
            *st['arrays'], *st['lands'], st['send'], st['recv'], after)
    return [_put_own(l, a, group, src_by) for a, l in zip(res[:n], res[n:])]


def _unshard(g, axis):
    t = jnp.moveaxis(g, 0, axis)
    s = t.shape
    return t.reshape(s[:axis] + (s[axis] * s[axis + 1],) + s[axis + 2:])


def _to_shards(w, axis):
    s = w.shape
    t = w.reshape(s[:axis] + (4, s[axis] // 4) + s[axis + 1:])
    return jnp.moveaxis(t, axis, 0)


def _pre_fwd(x, g, sc, sh):
    r = lax.rsqrt(_rowmean(x * x) + NORM_EPS)
    return (x * r) * g * (1.0 + sc) + sh


def _pre_bwd(dh, x, g, sc):
    r = lax.rsqrt(_rowmean(x * x) + NORM_EPS)
    xn = x * r
    dxn = dh * (g * (1.0 + sc))
    dx = r * (dxn - xn * _rowmean(dxn * xn))
    t = dh * xn
    return dx, _colsum(dh), _colsum(t * g), _colsum(t * (1.0 + sc))


def _post_fwd(x, y, gt, g):
    r = lax.rsqrt(_rowmean(y * y) + NORM_EPS)
    return x + gt * ((y * r) * g)


def _post_bwd(dxo, y, gt, g):
    r = lax.rsqrt(_rowmean(y * y) + NORM_EPS)
    yn = y * r
    t = dxo * yn
    dyn = dxo * (gt * g)
    dy = r * (dyn - yn * _rowmean(dyn * yn))
    return dy, _colsum(t * g), _colsum(t * gt)


def _gain_bwd(dy, x, g):
    r = lax.rsqrt(_rowmean(x * x) + NORM_EPS)
    xn = x * r
    dxn = dy * g
    return r * (dxn - xn * _rowmean(dxn * xn)), _colsum(dy * xn)


def _rope(x, cos, sa, sb):
    return x * cos + pltpu.roll(x, HEAD_PAD - 16, 1) * sa + pltpu.roll(x, 16, 1) * sb


def _rope_t(d, cos, sa, sb):
    return d * cos + pltpu.roll(d * sa, 16, 1) + pltpu.roll(d * sb, HEAD_PAD - 16, 1)


def _rope_tables(pos_f):
    S = pos_f.shape[0]
    inv = ROPE_THETA ** (-jnp.arange(0, QK_ROPE, 2, dtype=F32) / QK_ROPE)
    inv_ext = jnp.concatenate([jnp.zeros((QK_NOPE,), F32), inv, inv,
                               jnp.zeros((HEAD_PAD - QK_NOPE - QK_ROPE,), F32)]).reshape(1, HEAD_PAD)

    def fn(p, iv):
        ang = p * iv
        lane = lax.broadcasted_iota(jnp.int32, ang.shape, 1)
        s = jnp.sin(ang)
        first = (lane >= QK_NOPE) & (lane < QK_NOPE + QK_ROPE // 2)
        second = (lane >= QK_NOPE + QK_ROPE // 2) & (lane < QK_NOPE + QK_ROPE)
        return (jnp.cos(ang), jnp.where(first, -s, 0.0), jnp.where(second, s, 0.0)), ()

    (cos, sa, sb), _ = _rowk("rope_tables", fn, [pos_f], [inv_ext], [(HEAD_PAD, F32)] * 3, [])
    return cos, sa, sb


def _diag_mask(transposed):
    r = lax.broadcasted_iota(jnp.int32, (BQ, BQ), 0) >> CHUNK_SHIFT
    c = lax.broadcasted_iota(jnp.int32, (BQ, BQ), 1) >> CHUNK_SHIFT
    return (r <= c) if transposed else (c <= r)


_NT = (((1,), (1,)), ((), ()))
_NN = (((1,), (0,)), ((), ()))


def _attn_fwd(qf, kvf, after, HB=HF):
    S = qf.shape[0]
    nq = S // BQ

    def body(q_ref, kv_ref, after_ref, o_ref, ob_ref, lse_ref):
        qi = pl.program_id(1)
        qs = [q_ref[:, hh * HEAD_PAD:(hh + 1) * HEAD_PAD] for hh in range(HB)]

        def step(j, carry, diag):
            off = pl.multiple_of(j * BQ, BQ)
            sts = [lax.dot_general(kv_ref[pl.ds(off, BQ), pl.ds(2 * hh * HEAD_PAD, HEAD_PAD)], qs[hh], _NT,
                                   preferred_element_type=F32) for hh in range(HB)]
            mid = []
            for hh in range(HB):
                m, l, acc = carry[hh]
                st = jnp.where(_diag_mask(True), sts[hh], NEG_INF) if diag else sts[hh]
                m2 = jnp.maximum(m, jnp.max(st, axis=0, keepdims=True))
                al = jnp.exp2(m - m2)
                pt = jnp.exp2(st - m2)
                mid.append((m2, l * al + jnp.sum(pt, axis=0, keepdims=True), acc * al, pt.astype(BF16)))
            out = []
            for hh in range(HB):
                m2, l2, acc_s, ptb = mid[hh]
                v = kv_ref[pl.ds(off, BQ), pl.ds((2 * hh + 1) * HEAD_PAD, HEAD_PAD)]
                out.append((m2, l2, acc_s + lax.dot_general(v, ptb, _TN, preferred_element_type=F32)))
            return tuple(out)

        init = tuple((jnp.full((1, BQ), NEG_INF, F32), jnp.zeros((1, BQ), F32), jnp.zeros((HEAD_PAD, BQ), F32))
                     for _ in range(HB))
        carry = lax.fori_loop(0, qi, lambda j, c: step(j, c, False), init)
        carry = step(qi, carry, True)
        for hh in range(HB):
            m, l, acc = carry[hh]
            ov = (acc / l).T
            o_ref[:, hh * HEAD_PAD:(hh + 1) * HEAD_PAD] = ov
            ob_ref[:, hh * HEAD_PAD:(hh + 1) * HEAD_PAD] = ov.astype(BF16)
            lse_ref[hh] = m + jnp.log(l) * LOG2E

    return pl.pallas_call(
        body, name="attn_fwd", grid=(N_HEADS // HB, nq),
        in_specs=[pl.BlockSpec((BQ, HB * HEAD_PAD), lambda g, i: (i, g)),
                  pl.BlockSpec((S, 2 * HB * HEAD_PAD), lambda g, i: (0, g)),
                  pl.BlockSpec(after.shape, lambda g, i: (0, 0))],
        out_specs=[pl.BlockSpec((BQ, HB * HEAD_PAD), lambda g, i: (i, g)),
                   pl.BlockSpec((BQ, HB * HEAD_PAD), lambda g, i: (i, g)),
                   pl.BlockSpec((HB, None, 1, BQ), lambda g, i: (g, i, 0, 0))],
        out_shape=[jax.ShapeDtypeStruct((S, QW), F32), jax.ShapeDtypeStruct((S, QW), BF16),
                   jax.ShapeDtypeStruct((N_HEADS, nq, 1, BQ), F32)],
        compiler_params=_cparams("parallel", "arbitrary"))(qf, kvf, after)


def _attn_delta(dob, o):
    S = o.shape[0]

    def fn(dov, ov):
        prod = dov.astype(F32) * ov
        lane = lax.broadcasted_iota(jnp.int32, (prod.shape[0], HEAD_PAD), 1)
        out = jnp.zeros((prod.shape[0], HEAD_PAD), F32)
        for h in range(N_HEADS):
            out = jnp.where(lane == h, jnp.sum(prod[:, h * HEAD_PAD:(h + 1) * HEAD_PAD], axis=1, keepdims=True), out)
        return (out,), ()

    (dd,), _ = _rowk("attn_delta", fn, [dob, o], [], [(HEAD_PAD, F32)], [])
    return dd[:, :N_HEADS].T.reshape(N_HEADS, S // BQ, 1, BQ)


_TN = (((0,), (0,)), ((), ()))


def _attn_bwd(qf, kvf, dob, lse_row, dd_row, cos, sa, sb):
    S = qf.shape[0]
    nq = S // BQ

    def body(kv_ref, q_ref, do_ref, lse_ref, dd_ref, cos_ref, sa_ref, sb_ref, dqo_ref, dkv_ref, dq_ref):
        kj = pl.program_id(1)

        @pl.when(kj == 0)
        def _():
            dq_ref[...] = jnp.zeros(dq_ref.shape, F32)

        ks = [kv_ref[:, 2 * hh * HEAD_PAD:(2 * hh + 1) * HEAD_PAD] for hh in range(HB)]
        vs = [kv_ref[:, (2 * hh + 1) * HEAD_PAD:(2 * hh + 2) * HEAD_PAD] for hh in range(HB)]

        def step(i, carry, diag):
            off = pl.multiple_of(i * BQ, BQ)
            cols = [pl.ds(hh * HEAD_PAD, HEAD_PAD) for hh in range(HB)]
            q = [q_ref[pl.ds(off, BQ), cols[hh]] for hh in range(HB)]
            do = [do_ref[pl.ds(off, BQ), cols[hh]] for hh in range(HB)]
            sts = [lax.dot_general(ks[hh], q[hh], _NT, preferred_element_type=F32) for hh in range(HB)]
            dpts = [lax.dot_general(vs[hh], do[hh], _NT, preferred_element_type=F32) for hh in range(HB)]
            mid = []
            for hh in range(HB):
                st = jnp.where(_diag_mask(True), sts[hh], NEG_INF) if diag else sts[hh]
                pt = jnp.exp2(st - lse_ref[hh, i])
                mid.append((pt.astype(BF16), (pt * (dpts[hh] - dd_ref[hh, i])).astype(BF16)))
            out = []
            for hh in range(HB):
                dk, dv = carry[hh]
                ptb, dsb = mid[hh]
                dv2 = dv + lax.dot_general(ptb, do[hh], _NN, preferred_element_type=F32)
                dk2 = dk + lax.dot_general(dsb, q[hh], _NN, preferred_element_type=F32)
                dq_ref[pl.ds(off, BQ), cols[hh]] += lax.dot_general(dsb, ks[hh], _TN, preferred_element_type=F32)
                out.append((dk2, dv2))
            return tuple(out)

        zero = jnp.zeros((BQ, HEAD_PAD), F32)
        carry = step(kj, tuple((zero, zero) for _ in range(HB)), True)
        carry = lax.fori_loop(kj + 1, nq, lambda i, c: step(i, c, False), carry)
        done = pl.ds(pl.multiple_of(kj * BQ, BQ), BQ)
        for hh in range(HB):
            dk, dv = carry[hh]
            dk = _rope_t(dk * (1.0 / LOG2E), cos_ref[...], sa_ref[...], sb_ref[...])
            dkv_ref[:, 2 * hh * HEAD_PAD:(2 * hh + 1) * HEAD_PAD] = dk.astype(BF16)
            dkv_ref[:, (2 * hh + 1) * HEAD_PAD:(2 * hh + 2) * HEAD_PAD] = dv.astype(BF16)
            cols = pl.ds(hh * HEAD_PAD, HEAD_PAD)
            dqo_ref[:, cols] = _rope_t(dq_ref[done, cols] * ATT_SCALE, cos_ref[...], sa_ref[...],
                                       sb_ref[...]).astype(BF16)

    tab = pl.BlockSpec((BQ, HEAD_PAD), lambda g, j: (j, 0))
    row = pl.BlockSpec((HB, nq, 1, BQ), lambda g, j: (g, 0, 0, 0))
    seq = pl.BlockSpec((S, HB * HEAD_PAD), lambda g, j: (0, g), pipeline_mode=pl.Buffered(1))
    kvb = pl.BlockSpec((BQ, 2 * HB * HEAD_PAD), lambda g, j: (j, g))
    return pl.pallas_call(
        body, name="attn_bwd", grid=(N_HEADS // HB, nq),
        in_specs=[kvb, seq, seq, row, row, tab, tab, tab],
        out_specs=[pl.BlockSpec((BQ, HB * HEAD_PAD), lambda g, j: (j, g)), kvb],
        out_shape=[jax.ShapeDtypeStruct((S, QW), BF16), jax.ShapeDtypeStruct((S, KVW), BF16)],
        scratch_shapes=[pltpu.VMEM((S, HB * HEAD_PAD), F32)],
        compiler_params=_cparams("parallel", "arbitrary"))(kvf, qf, dob, lse_row, dd_row, cos, sa, sb)


DC = 128
TR = 256


def _dwconv_fwd(u, w, b):
    S, Dm = u.shape
    tr = min(TR, S)

    def body(u_ref, w_ref, b_ref, o_ref, pad_ref):
        pad_ref[pl.ds(0, PAD_ROWS), :] = jnp.zeros((PAD_ROWS, DC), F32)
        pad_ref[pl.ds(PAD_ROWS, S), :] = u_ref[...]
        wv = w_ref[...]
        for r in range(S // tr):
            acc = jnp.broadcast_to(b_ref[...], (tr, DC))
            for j in range(CONV_WIDTH):
                acc = acc + wv[j:j + 1, :] * pad_ref[pl.ds(r * tr + PAD_ROWS - (CONV_WIDTH - 1) + j, tr), :]
            o_ref[pl.ds(r * tr, tr), :] = acc

    return pl.pallas_call(
        body, name="dwconv_fwd", grid=(Dm // DC,),
        in_specs=[pl.BlockSpec((S, DC), lambda c: (0, c)), pl.BlockSpec((CONV_WIDTH, DC), lambda c: (0, c)),
                  pl.BlockSpec((1, DC), lambda c: (0, c))],
        out_specs=pl.BlockSpec((S, DC), lambda c: (0, c)),
        out_shape=jax.ShapeDtypeStruct((S, Dm), F32),
        scratch_shapes=[pltpu.VMEM((S + PAD_ROWS, DC), F32)],
        compiler_params=_cparams("parallel"))(u, w, b)


def _dwconv_bwd(d, u, w):
    S, Dm = u.shape
    tr = min(TR, S)

    def body(d_ref, u_ref, w_ref, du_ref, dw_ref, padd_ref, padu_ref):
        padd_ref[pl.ds(0, S), :] = d_ref[...]
        padd_ref[pl.ds(S, PAD_ROWS), :] = jnp.zeros((PAD_ROWS, DC), F32)
        padu_ref[pl.ds(0, PAD_ROWS), :] = jnp.zeros((PAD_ROWS, DC), F32)
        padu_ref[pl.ds(PAD_ROWS, S), :] = u_ref[...]
        wv = w_ref[...]
        dws = [jnp.zeros((1, DC), F32) for _ in range(CONV_WIDTH)]
        for r in range(S // tr):
            acc = jnp.zeros((tr, DC), F32)
            for j in range(CONV_WIDTH):
                acc = acc + wv[j:j + 1, :] * padd_ref[pl.ds(r * tr + (CONV_WIDTH - 1) - j, tr), :]
            du_ref[pl.ds(r * tr, tr), :] = acc
            dt = d_ref[pl.ds(r * tr, tr), :]
            for j in range(CONV_WIDTH):
                ut = padu_ref[pl.ds(r * tr + PAD_ROWS - (CONV_WIDTH - 1) + j, tr), :]
                dws[j] = dws[j] + _colsum(dt * ut)
        for j in range(CONV_WIDTH):
            dw_ref[pl.ds(j, 1), :] = dws[j]
        dw_ref[pl.ds(CONV_WIDTH, 1), :] = jnp.zeros((1, DC), F32)

    blk = pl.BlockSpec((S, DC), lambda c: (0, c))
    return pl.pallas_call(
        body, name="dwconv_bwd", grid=(Dm // DC,),
        in_specs=[blk, blk, pl.BlockSpec((CONV_WIDTH, DC), lambda c: (0, c))],
        out_specs=[blk, pl.BlockSpec((PAD_ROWS, DC), lambda c: (0, c))],
        out_shape=[jax.ShapeDtypeStruct((S, Dm), F32), jax.ShapeDtypeStruct((PAD_ROWS, Dm), F32)],
        scratch_shapes=[pltpu.VMEM((S + PAD_ROWS, DC), F32), pltpu.VMEM((S + PAD_ROWS, DC), F32)],
        compiler_params=_cparams("parallel"))(d, u, w)


POOL_C = D_MODEL // len(POOL_WINDOWS)


def _pool_counts(r, tr, win):
    t = r * tr + lax.broadcasted_iota(jnp.int32, (tr, 1), 0)
    return jnp.minimum(t + 1, win).astype(F32)


def _pool_fwd(h):
    S, Dm = h.shape
    tr = min(TR, S)

    def body(h_ref, o_ref, pad_ref):
        pad_ref[pl.ds(0, PAD_ROWS), :] = jnp.zeros((PAD_ROWS, POOL_C), F32)
        pad_ref[pl.ds(PAD_ROWS, S), :] = h_ref[...]
        for g, win in enumerate(POOL_WINDOWS):
            @pl.when(pl.program_id(0) == g)
            def _():
                for r in range(S // tr):
                    acc = pad_ref[pl.ds(r * tr + PAD_ROWS, tr), :]
                    for j in range(1, win):
                        acc = acc + pad_ref[pl.ds(r * tr + PAD_ROWS - j, tr), :]
                    pooled = acc / _pool_counts(r, tr, win)
                    o_ref[pl.ds(r * tr, tr), :] = (pooled - h_ref[pl.ds(r * tr, tr), :]).astype(BF16)

    blk = pl.BlockSpec((S, POOL_C), lambda g: (0, g))
    return pl.pallas_call(
        body, name="pool_fwd", grid=(len(POOL_WINDOWS),), in_specs=[blk], out_specs=blk,
        out_shape=jax.ShapeDtypeStruct((S, Dm), BF16),
        scratch_shapes=[pltpu.VMEM((S + PAD_ROWS, POOL_C), F32)],
        compiler_params=_cparams("parallel"))(h)


def _pool_bwd(dp):
    S, Dm = dp.shape
    tr = min(TR, S)

    def body(d_ref, o_ref, pad_ref):
        pad_ref[pl.ds(S, PAD_ROWS), :] = jnp.zeros((PAD_ROWS, POOL_C), F32)
        for g, win in enumerate(POOL_WINDOWS):
            @pl.when(pl.program_id(0) == g)
            def _():
                for r in range(S // tr):
                    pad_ref[pl.ds(r * tr, tr), :] = d_ref[pl.ds(r * tr, tr), :] / _pool_counts(r, tr, win)
                for r in range(S // tr):
                    acc = pad_ref[pl.ds(r * tr, tr), :]
                    for j in range(1, win):
                        acc = acc + pad_ref[pl.ds(r * tr + j, tr), :]
                    o_ref[pl.ds(r * tr, tr), :] = acc - d_ref[pl.ds(r * tr, tr), :]

    blk = pl.BlockSpec((S, POOL_C), lambda g: (0, g))
    return pl.pallas_call(
        body, name="pool_bwd", grid=(len(POOL_WINDOWS),), in_specs=[blk], out_specs=blk,
        out_shape=jax.ShapeDtypeStruct((S, Dm), F32),
        scratch_shapes=[pltpu.VMEM((S + PAD_ROWS, POOL_C), F32)],
        compiler_params=_cparams("parallel"))(dp)


def _bias_spec(tn):
    return pl.BlockSpec((1, tn), lambda i, j, k: (0, j))


def _mla_weights(w_dq, w_dkv, w_uq, w_ukv, w_o):
    wd = jnp.concatenate([w_dq, w_dkv], axis=1)
    wq = jnp.pad(w_uq.reshape(Q_LORA, N_HEADS, QK_NOPE + QK_ROPE),
                 ((0, 0), (0, 0), (0, HEAD_PAD - QK_NOPE - QK_ROPE))).reshape(Q_LORA, QW)
    ukv = w_ukv.reshape(KV_LORA, N_HEADS, QK_NOPE + V_HEAD)
    wkv = jnp.zeros((DKV, N_HEADS, 2 * HEAD_PAD), BF16)
    wkv = wkv.at[:KV_LORA, :, :QK_NOPE].set(ukv[:, :, :QK_NOPE])
    wkv = wkv.at[:KV_LORA, :, HEAD_PAD:HEAD_PAD + V_HEAD].set(ukv[:, :, QK_NOPE:])
    eye = jnp.broadcast_to(jnp.eye(QK_ROPE, dtype=BF16)[:, None, :], (QK_ROPE, N_HEADS, QK_ROPE))
    wkv = wkv.at[KV_LORA:, :, QK_NOPE:QK_NOPE + QK_ROPE].set(eye).reshape(DKV, KVW)
    wo = jnp.pad(w_o.reshape(N_HEADS, V_HEAD, D_MODEL),
                 ((0, 0), (0, HEAD_PAD - V_HEAD), (0, 0))).reshape(QW, D_MODEL)
    return dict(wd=wd, wq=wq, wkv=wkv, wo=wo)


def _mla_weight_grads(g_wd, g_wq, g_wkv, g_wo):
    g_uq = g_wq.reshape(Q_LORA, N_HEADS, HEAD_PAD)[:, :, :QK_NOPE + QK_ROPE].reshape(Q_LORA, -1)
    t = g_wkv.reshape(DKV, N_HEADS, 2 * HEAD_PAD)[:KV_LORA]
    g_ukv = jnp.concatenate([t[:, :, :QK_NOPE], t[:, :, HEAD_PAD:HEAD_PAD + V_HEAD]], axis=2)
    g_o = g_wo.reshape(N_HEADS, HEAD_PAD, D_MODEL)[:, :V_HEAD].reshape(N_HEADS * V_HEAD, D_MODEL)
    return dict(mla_w_dq=g_wd[:, :Q_LORA], mla_w_uq=g_uq, mla_w_dkv=g_wd[:, Q_LORA:],
                mla_w_ukv=g_ukv.reshape(KV_LORA, -1), mla_w_o=g_o)


def _rope_epilogue(kv):
    def epi(acc, cos, sa, sb):
        parts = []
        for t in range(acc.shape[1] // HEAD_PAD):
            x = acc[:, t * HEAD_PAD:(t + 1) * HEAD_PAD]
            if kv:
                parts.append(x if t % 2 else _rope(x, cos, sa, sb))
            else:
                parts.append(_rope(x, cos, sa, sb) * SCALE_LOG2E)
        return (jnp.concatenate(parts, axis=1),)
    return epi


def _mla_fwd(tag, h, P, rope, before_attention=None):
    S = h.shape[0]
    cos, sa, sb = rope
    tp = min(512, S)
    tabs = [pl.BlockSpec((tp, HEAD_PAD), lambda i, j, k: (i, 0))] * 3
    cqkv = _mm(f"mla_down{tag}", h, P['wd'], 'nn', S, DQKV, D_MODEL)

    def norms(x, qg, kg):
        xq, xk, xr = x[:, :Q_LORA], x[:, Q_LORA:Q_LORA + KV_LORA], x[:, Q_LORA + KV_LORA:]
        cq = xq * lax.rsqrt(_rowmean(xq * xq) + NORM_EPS) * qg
        ck = xk * lax.rsqrt(_rowmean(xk * xk) + NORM_EPS) * kg
        return (cq, jnp.concatenate([ck, xr], axis=1)), ()

    (cq, ckv), _ = _rowk(f"mla_norms{tag}", norms, [cqkv], [P['qg'], P['kg']], [(Q_LORA, BF16), (DKV, BF16)], [])
    early = cos[:8] if before_attention is None else before_attention(cq)
    rope_q = _rope_epilogue(False)
    qf = _mm(f"mla_q{tag}", cq, P['wq'], 'nn', S, QW, Q_LORA, tm=tp, tn=QW, extras=[cos, sa, sb, early],
             extra_specs=tabs + [pl.BlockSpec(early.shape, lambda i, j, k: (0, 0))],
             epi=lambda acc, cv, sav, sbv, _: rope_q(acc, cv, sav, sbv), outs=[jax.ShapeDtypeStruct((S, QW), BF16)])
    kvf = _mm(f"mla_kv{tag}", ckv, P['wkv'], 'nn', S, KVW, DKV, tm=tp, tn=KVW, extras=[cos, sa, sb],
              extra_specs=tabs, epi=_rope_epilogue(True), outs=[jax.ShapeDtypeStruct((S, KVW), BF16)])
    o, ob, lse = _attn_fwd(qf, kvf, early)
    y = _mm(f"mla_o{tag}", ob, P['wo'], 'nn', S, D_MODEL, QW)
    return y, dict(cqkv=cqkv, cq=cq, ckv=ckv, qf=qf, kvf=kvf, o=o, ob=ob, lse=lse)


def _mla_bwd(tag, dy, h, sv, P, rope):
    S = h.shape[0]
    nq = S // BQ
    cos, sa, sb = rope
    g_wo = _mm(f"mla_o_wg{tag}", sv['ob'], dy, 'tn', QW, D_MODEL, S)
    dob = _mm(f"mla_o_dg{tag}", dy, P['wo'], 'nt', S, QW, D_MODEL, outs=[jax.ShapeDtypeStruct((S, QW), BF16)])
    dd = _attn_delta(dob, sv['o'])
    dq, dkv = _attn_bwd(sv['qf'], sv['kvf'], dob, sv['lse'], dd, cos, sa, sb)
    g_wq = _mm(f"mla_q_wg{tag}", sv['cq'], dq, 'tn', Q_LORA, QW, S)
    dcq = _mm(f"mla_q_dg{tag}", dq, P['wq'], 'nt', S, Q_LORA, QW)
    g_wkv = _mm(f"mla_kv_wg{tag}", sv['ckv'], dkv, 'tn', DKV, KVW, S)
    dckv = _mm(f"mla_kv_dg{tag}", dkv, P['wkv'], 'nt', S, DKV, KVW)

    def norms_bwd(dcq_v, dckv_v, x, qg, kg):
        xq, xk = x[:, :Q_LORA], x[:, Q_LORA:Q_LORA + KV_LORA]
        dxq, dqg = _gain_bwd(dcq_v, xq, qg)
        dxk, dkg = _gain_bwd(dckv_v[:, :KV_LORA], xk, kg)
        return (jnp.concatenate([dxq, dxk, dckv_v[:, KV_LORA:]], axis=1),), (dqg, dkg)

    (dcqkv,), (dqg, dkg) = _rowk(f"mla_norms_bwd{tag}", norms_bwd, [dcq, dckv, sv['cqkv']], [P['qg'], P['kg']],
                                 [(DQKV, BF16)], [Q_LORA, KV_LORA])
    g_wd = _mm(f"mla_down_wg{tag}", h, dcqkv, 'tn', D_MODEL, DQKV, S)
    dh = _mm(f"mla_down_dg{tag}", dcqkv, P['wd'], 'nt', S, D_MODEL, DQKV)
    grads = _mla_weight_grads(g_wd, g_wq, g_wkv, g_wo)
    grads.update(mla_q_norm_g=dqg.reshape(-1), mla_kv_norm_g=dkg.reshape(-1))
    return dh, grads


def _conv_fwd(h, P):
    S = h.shape[0]
    a = _mm("conv_pw1", h, P['w_pw1'], 'nn', S, 2 * D_MODEL, D_MODEL, extras=[P['b_pw1']],
            extra_specs=[_bias_spec(1024)], epi=lambda acc, b: (acc + b,))
    (u0,), _ = _rowk("conv_glu", lambda av: ((av[:, :D_MODEL] * _sigmoid(av[:, D_MODEL:]),), ()),
                     [a], [], [(D_MODEL, F32)], [])
    u1 = _dwconv_fwd(u0, P['w_dw'], P['b_dw'])

    def ln_silu(u, g, b):
        xc = u - _rowmean(u)
        z = xc * lax.rsqrt(_rowmean(xc * xc) + NORM_EPS) * g + b
        return (z * _sigmoid(z),), ()

    (u3,), _ = _rowk("conv_ln", ln_silu, [u1], [P['ln_g'], P['ln_b']], [(D_MODEL, BF16)], [])
    y = _mm("conv_pw2", u3, P['w_pw2'], 'nn', S, D_MODEL, D_MODEL, extras=[P['b_pw2']],
            extra_specs=[_bias_spec(1024)], epi=lambda acc, b: (acc + b,))
    return y, dict(a=a, u0=u0, u1=u1, u3=u3)


def _conv_bwd(dy, dy_colsum, h, sv, P):
    S = h.shape[0]
    g_pw2 = _mm("conv_pw2_wg", sv['u3'], dy, 'tn', D_MODEL, D_MODEL, S)
    du3 = _mm("conv_pw2_dg", dy, P['w_pw2'], 'nt', S, D_MODEL, D_MODEL)

    def ln_bwd(d3, u, g, b):
        xc = u - _rowmean(u)
        rstd = lax.rsqrt(_rowmean(xc * xc) + NORM_EPS)
        xh = xc * rstd
        z = xh * g + b
        sg = _sigmoid(z)
        dz = d3 * (sg * (1.0 + z * (1.0 - sg)))
        dxh = dz * g
        du = rstd * (dxh - _rowmean(dxh) - xh * _rowmean(dxh * xh))
        return (du,), (_colsum(dz * xh), _colsum(dz), _colsum(du))

    (du1,), (d_lng, d_lnb, d_bdw) = _rowk("conv_ln_bwd", ln_bwd, [du3, sv['u1']], [P['ln_g'], P['ln_b']],
                                          [(D_MODEL, F32)], [D_MODEL] * 3)
    du0, d_wdw = _dwconv_bwd(du1, sv['u0'], P['w_dw'])

    def glu_bwd(d0, av):
        a1, sg = av[:, :D_MODEL], _sigmoid(av[:, D_MODEL:])
        da = jnp.concatenate([d0 * sg, d0 * a1 * sg * (1.0 - sg)], axis=1)
        return (da,), (_colsum(da),)

    (da,), (d_bpw1,) = _rowk("conv_glu_bwd", glu_bwd, [du0, sv['a']], [], [(2 * D_MODEL, BF16)], [2 * D_MODEL])
    g_pw1 = _mm("conv_pw1_wg", h, da, 'tn', D_MODEL, 2 * D_MODEL, S)
    dh = _mm("conv_pw1_dg", da, P['w_pw1'], 'nt', S, D_MODEL, 2 * D_MODEL)
    grads = dict(conv_w_pw1=g_pw1, conv_b_pw1=d_bpw1.reshape(-1), conv_w_dw=d_wdw[:CONV_WIDTH],
                 conv_b_dw=d_bdw.reshape(-1), conv_ln_g=d_lng.reshape(-1), conv_ln_b=d_lnb.reshape(-1),
                 conv_w_pw2=g_pw2, conv_b_pw2=dy_colsum.reshape(-1))
    return dh, grads


def _pool_group_specs(tm):
    return (pl.BlockSpec((tm, POOL_C), lambda i, j, k: (i, j)),
            pl.BlockSpec((None, POOL_C, POOL_C), lambda i, j, k: (j, 0, 0)))


def _pool_mixer_fwd(h, P):
    S = h.shape[0]
    p = _pool_fwd(h)
    a_spec, b_spec = _pool_group_specs(min(1024, S))
    y, z = _mm("pool_mm", p, P['w'], 'nn', S, D_MODEL, POOL_C, tn=POOL_C, a_spec=a_spec, b_spec=b_spec,
               extras=[P['b'], P['scale']], extra_specs=[_bias_spec(POOL_C)] * 2,
               epi=lambda acc, b, s: ((acc + b) * s, acc + b),
               outs=[jax.ShapeDtypeStruct((S, D_MODEL), F32)] * 2)
    return y, dict(p=p, z=z)


def _pool_mixer_bwd(dy, sv, P):
    S = dy.shape[0]

    def scale_bwd(d, z, s):
        dz = d * s
        return (dz,), (_colsum(d * z), _colsum(dz))

    (dz,), (d_scale, d_b) = _rowk("pool_scale_bwd", scale_bwd, [dy, sv['z']], [P['scale']],
                                  [(D_MODEL, BF16)], [D_MODEL] * 2)
    a_spec, b_spec = _pool_group_specs(min(1024, S))
    dp = _mm("pool_mm_dg", dz, P['w'], 'nt', S, D_MODEL, POOL_C, tn=POOL_C, a_spec=a_spec, b_spec=b_spec)
    tk = min(512, S)
    grp = pl.BlockSpec((tk, POOL_C), lambda i, j, k: (k, j))
    g_w = _mm("pool_mm_wg", sv['p'], dz, 'tn', POOL_C, D_MODEL, S, tn=POOL_C, tk=tk, a_spec=grp, b_spec=grp,
              outs=[jax.ShapeDtypeStruct((len(POOL_WINDOWS), POOL_C, POOL_C), F32)],
              out_specs=[pl.BlockSpec((None, POOL_C, POOL_C), lambda i, j, k: (j, 0, 0))])
    dh = _pool_bwd(dp)
    return dh, dict(pool_w=g_w, pool_b=d_b.reshape(-1), pool_scale=d_scale.reshape(-1))


def _adamw(w, g, m, v):
    m2 = ADAM_B1 * m + (1.0 - ADAM_B1) * g
    v2 = ADAM_B2 * v + (1.0 - ADAM_B2) * (g * g)
    m_hat = m2 / (1.0 - ADAM_B1 ** ADAM_STEP)
    v_hat = v2 / (1.0 - ADAM_B2 ** ADAM_STEP)
    delta = -ADAM_LR * (m_hat / (jnp.sqrt(v_hat) + ADAM_EPS) + ADAM_WD * w)
    return delta, m2, v2


def _finish(name, w, land, m, v, layer=None, prev=None):
    local = land.shape[1:]
    C = local[-1]
    R = land[0].size // C
    tr = 64 if R % 64 == 0 else R

    def body(land_hbm, g_hbm, land_v, g_v, recv_v, io_sem, send_sem, recv_sem):
        load = pltpu.make_async_copy(land_hbm, land_v, io_sem)
        load.start()
        load.wait()

        def rows_of(i):
            return pl.ds(pl.multiple_of(i * tr, tr), tr)

        def sum_chunk(i, carry):
            rows = rows_of(i)
            g_v[rows, :] = ((land_v[0, rows, :].astype(F32) + land_v[1, rows, :].astype(F32))
                            + land_v[2, rows, :].astype(F32)) + land_v[3, rows, :].astype(F32)
            return carry

        lax.fori_loop(0, R // tr, sum_chunk, 0)
        swap = pltpu.make_async_remote_copy(
            src_ref=g_v, dst_ref=recv_v, send_sem=send_sem, recv_sem=recv_sem,
            device_id=(lax.axis_index("x"), lax.axis_index("y"), 1 - lax.axis_index("c")),
            device_id_type=pl.DeviceIdType.MESH)
        swap.start()
        swap.wait()

        def add_chunk(i, carry):
            rows = rows_of(i)
            recv_v[rows, :] = g_v[rows, :] + recv_v[rows, :]
            return carry

        lax.fori_loop(0, R // tr, add_chunk, 0)
        store = pltpu.make_async_copy(recv_v, g_hbm, io_sem)
        store.start()
        store.wait()

    any_spec = pl.BlockSpec(memory_space=pl.ANY)
    g = pl.pallas_call(
        body, name=name, in_specs=[any_spec], out_specs=any_spec, out_shape=jax.ShapeDtypeStruct((R, C), F32),
        scratch_shapes=[pltpu.VMEM((4, R, C), BF16), pltpu.VMEM((R, C), F32), pltpu.VMEM((R, C), F32),
                        pltpu.SemaphoreType.DMA, pltpu.SemaphoreType.DMA, pltpu.SemaphoreType.DMA],
        compiler_params=pltpu.CompilerParams(has_side_effects=True, vmem_limit_bytes=VMEM_LIMIT))(
            land.reshape(4, R, C))

    lead = () if layer is None else (w.shape[0],)
    as2d = lambda a: a.reshape(lead + (R, C))
    tu = _row_tile(R, C)
    tile = pl.BlockSpec((tu, C), lambda i: (i, 0))
    slab = tile if layer is None else pl.BlockSpec((None, tu, C), lambda i: (layer, i, 0))
    n_prev = 0 if prev is None else 4

    def update(w_ref, g_ref, m_ref, v_ref, *rest):
        outs = rest[n_prev:]
        gv = g_ref[...]
        d, nm, nv = _adamw(w_ref[...], gv, m_ref[...], v_ref[...])
        for r, val in zip(outs, (gv, d, nm, nv)):
            r[...] = val

    res = pl.pallas_call(
        update, name=name + "_adamw", grid=(R // tu,),
        in_specs=[slab, tile, slab, slab] + [any_spec] * n_prev, out_specs=[slab] * 4,
        out_shape=[jax.ShapeDtypeStruct(lead + (R, C), F32)] * 4,
        input_output_aliases={4 + k: k for k in range(n_prev)},
        compiler_params=_cparams("arbitrary"))(
            as2d(w), g, as2d(m), as2d(v), *([] if prev is None else [as2d(p) for p in prev]))
    return [r.reshape(w.shape) for r in res]


def _row(v):
    return v.reshape(1, -1)


def kernel(x, c, positions, ada_w, ada_b, norm_g, mla_w_dq, mla_q_norm_g, mla_w_uq, mla_w_dkv, mla_kv_norm_g, mla_w_ukv, mla_w_o, conv_w_pw1, conv_b_pw1, conv_w_dw, conv_b_dw, conv_ln_g, conv_ln_b, conv_w_pw2, conv_b_pw2, pool_w, pool_b, pool_scale, ffn_w1, ffn_w2, loss_target, m_ada_w, m_ada_b, m_norm_g, m_mla_w_dq, m_mla_q_norm_g, m_mla_w_uq, m_mla_w_dkv, m_mla_kv_norm_g, m_mla_w_ukv, m_mla_w_o, m_conv_w_pw1, m_conv_b_pw1, m_conv_w_dw, m_conv_b_dw, m_conv_ln_g, m_conv_ln_b, m_conv_w_pw2, m_conv_b_pw2, m_pool_w, m_pool_b, m_pool_scale, m_ffn_w1, m_ffn_w2, v_ada_w, v_ada_b, v_norm_g, v_mla_w_dq, v_mla_q_norm_g, v_mla_w_uq, v_mla_w_dkv, v_mla_kv_norm_g, v_mla_w_ukv, v_mla_w_o, v_conv_w_pw1, v_conv_b_pw1, v_conv_w_dw, v_conv_b_dw, v_conv_ln_g, v_conv_ln_b, v_conv_w_pw2, v_conv_b_pw2, v_pool_w, v_pool_b, v_pool_scale, v_ffn_w1, v_ffn_w2):
    args = dict(locals())
    W = {n: args[n] for n in WEIGHTS}
    MOM = {n: args['m_' + n] for n in WEIGHTS}
    VAR = {n: args['v_' + n] for n in WEIGHTS}
    S = x.shape[1]
    xs = x.reshape(S, D_MODEL)
    tgt = loss_target.reshape(S, D_MODEL)
    mx, my, mc = lax.axis_index("x"), lax.axis_index("y"), lax.axis_index("c")
    chip = 2 * mx + my
    n_sh = ada_w.shape[2]

    def sent_of(key):
        n, l = key
        arr = W[n] if l is None else W[n][l]
        return arr.astype(BF16) if n in BIG or n in ('ffn_w1', 'ffn_w2') else arr

    keys0 = [(n, 0) for n in MLA_MATS] + [(n, None) for n in ('norm_g', 'mla_q_norm_g', 'mla_kv_norm_g',
                                                               'conv_w_dw', 'pool_b', 'pool_scale')]
    c8 = _exchange("gather_c", [c.reshape(8, D_MODEL // 8)], 'xyc')[0].reshape(8, D_MODEL)
    sent0 = [sent_of(k) for k in keys0]
    sent0[-1] = sent0[-1] + jnp.minimum(jnp.abs(c8[0, 0]), 0.0)
    fly0 = _split_start("gather_w0_start", sent0)
    c8 = c8 + fly0['token'][0, 0]
    c8 = jnp.pad(c8, ((0, ADA_ROWS - 8), (0, 0)))
    silu = lambda v: v * _sigmoid(v)
    mod_sh = []
    for l in range(DEPTH):
        b_l = lax.dynamic_slice(ada_b[l], (chip * n_sh,), (n_sh,)).reshape(1, n_sh)
        mod_sh.append(_mm(f"ada_fwd{l}", c8, ada_w, 'nn', ADA_ROWS, n_sh, D_MODEL, tn=n_sh // 2, tk=512, pro_a=silu,
                          b_spec=pl.BlockSpec((None, 512, n_sh // 2), lambda i, j, k, l=l: (l, k, j)),
                          extras=[b_l], extra_specs=[_bias_spec(n_sh // 2)], epi=lambda acc, b: (acc + b,))[:8])
    mod_sh = jnp.stack(mod_sh, axis=1).reshape(8, DEPTH * n_sh // 128, 128)
    mod = _exchange("scatter_mod", [mod_sh], 'xy', src_by='xyc')[0]
    mod = mod.reshape(4, DEPTH, n_sh).transpose(1, 0, 2).reshape(DEPTH, 6, 1, D_MODEL)

    keys1 = [('ffn_w1', 0), ('ffn_w2', 0), ('conv_w_pw1', None), ('conv_w_pw2', None), ('pool_w', None)]
    keys2 = [('ffn_w1', 1), ('ffn_w2', 1)]
    keys3 = [(n, l) for l in range(2, DEPTH) for n in ('ffn_w1', 'ffn_w2')] + [(n, 1) for n in MLA_MATS]
    fly1 = _split_start("gather_w1_start", [sent_of(k) for k in keys1])
    mod = mod + fly1['token'][0, 0]
    late = {}

    def start_group2(dep):
        sent2 = [sent_of(k) for k in keys2]
        sent2[-1] = sent2[-1] + jnp.minimum(jnp.abs(dep[0, 0]), 0).astype(BF16)
        late['fly2'] = _split_start("gather_w2_start", sent2)
        sent3 = [sent_of(k) for k in keys3]
        sent3[-1] = sent3[-1] + jnp.minimum(jnp.abs(dep[0, 0]), 0).astype(BF16)
        late['fly3'] = _split_start("gather_w3_start", sent3)
        return late['fly2']['token'] + late['fly3']['token']
    G = dict(zip(keys0, _split_wait("gather_w0_wait", fly0, mod)))

    def whole(key):
        n, l = key
        return _unshard(G[key], SHARD_AXIS[n] - (0 if l is None else 1))

    def mla_params(j):
        P = _mla_weights(*[whole((n, j)) for n in ('mla_w_dq', 'mla_w_dkv', 'mla_w_uq', 'mla_w_ukv', 'mla_w_o')])
        P.update(qg=_row(whole(('mla_q_norm_g', None))[j]), kg=_row(whole(('mla_kv_norm_g', None))[j]))
        return P

    gains = whole(('norm_g', None))
    mla_p = {0: mla_params(0)}
    conv_p = pool_p = None
    rope = _rope_tables(positions.reshape(S, 1).astype(F32))

    by_j = pl.BlockSpec((None, 1024, 1024), lambda i, j, k: (j, 0, 0))
    by_k = pl.BlockSpec((None, 1024, 1024), lambda i, j, k: (k, 0, 0))
    sq_relu = lambda v: jnp.square(jnp.maximum(v, 0.0))

    def md(i, k):
        return mod[i, k]

    (h,), _ = _rowk("pre0", lambda xv, g, sc, sh: ((_pre_fwd(xv, g, sc, sh),), ()),
                    [xs], [_row(gains[0, 0]), md(0, 1), md(0, 0)], [(D_MODEL, BF16)], [])
    saved = []
    xin = xs
    loss_acc = dxf = None
    for i in range(DEPTH):
        kind, j = i % 3, i // 3
        if kind == 0:
            if j not in mla_p:
                mla_p[j] = mla_params(j)
            y, sv = _mla_fwd(j, h, mla_p[j], rope, before_attention=start_group2 if i == 0 else None)
        elif kind == 1:
            y, sv = _conv_fwd(h, conv_p)
        else:
            y, sv = _pool_mixer_fwd(h, pool_p)

        def mid(xv, yv, gt, g1, g2, sc, sh):
            x1 = _post_fwd(xv, yv, gt, g1)
            return (x1, _pre_fwd(x1, g2, sc, sh)), ()

        (x1, h2), _ = _rowk(f"mid{i}", mid, [xin, y], [md(i, 2), _row(gains[i, 1]), _row(gains[i, 2]), md(i, 4), md(i, 3)],
                            [(D_MODEL, F32), (D_MODEL, BF16)], [])
        if i == 0:
            G.update(zip(keys1, _split_wait("gather_w1_wait", fly1, h2)))
            conv_p = dict(w_pw1=whole(('conv_w_pw1', None))[0], b_pw1=_row(conv_b_pw1[0]),
                          w_dw=whole(('conv_w_dw', None))[0], b_dw=_row(conv_b_dw[0]), ln_g=_row(conv_ln_g[0]),
                          ln_b=_row(conv_ln_b[0]), w_pw2=whole(('conv_w_pw2', None))[0], b_pw2=_row(conv_b_pw2[0]))
            pool_p = dict(w=whole(('pool_w', None))[0], b=_row(whole(('pool_b', None))[0]),
                          scale=_row(whole(('pool_scale', None))[0]))
        if i == 1:
            G.update(zip(keys2, _split_wait("gather_w2_wait", late['fly2'], h2)))
        if i == 2:
            G.update(zip(keys3, _split_wait("gather_w3_wait", late['fly3'], h2)))
        a = _mm(f"ffn1_{i}", h2, G[('ffn_w1', i)], 'nn', S, D_FF, D_MODEL, tm=2048, b_spec=by_j,
                outs=[jax.ShapeDtypeStruct((S, D_FF), BF16)])
        y2 = _mm(f"ffn2_{i}", a, G[('ffn_w2', i)], 'nn', S, D_MODEL, D_FF, pro_a=sq_relu, b_spec=by_k)
        saved.append(dict(x0=xin, h=h, y=y, x1=x1, h2=h2, a=a, y2=y2, mix=sv))
        if i + 1 < DEPTH:
            def nxt(xv, yv, gt, g3, g0, sc, sh):
                x2 = _post_fwd(xv, yv, gt, g3)
                return (x2, _pre_fwd(x2, g0, sc, sh)), ()

            hdt = F32 if (i + 1) % 3 == 2 else BF16
            (xin, h), _ = _rowk(f"next{i}", nxt, [x1, y2],
                                [md(i, 5), _row(gains[i, 3]), _row(gains[i + 1, 0]), md(i + 1, 1), md(i + 1, 0)],
                                [(D_MODEL, F32), (D_MODEL, hdt)], [])
        else:
            def head(xv, yv, tv, gt, g3):
                err = _post_fwd(xv, yv, gt, g3) - tv
                per_row = jnp.sum(err * err, axis=1, keepdims=True) * (0.5 / D_MODEL)
                return (err * (1.0 / D_MODEL),), (jnp.broadcast_to(jnp.sum(per_row, axis=0, keepdims=True), (1, 128)),)

            (dxf,), (loss_acc,) = _rowk("loss_head", head, [x1, y2, tgt], [md(i, 5), _row(gains[i, 3])],
                                        [(D_MODEL, F32)], [128])

    small = {}
    big = {}
    landed = {}

    def keep(gm, layer):
        for n, g in gm.items():
            if n in BIG:
                g = g[None] if layer is None else g
                big[(n, layer)] = _to_shards(g, SHARD_AXIS[n] - (0 if layer is None else 1)).astype(BF16)
            else:
                small.setdefault(n, {})[layer or 0] = g

    d_mod = [None] * DEPTH
    d_gain = [None] * DEPTH
    dx = dxf
    for i in reversed(range(DEPTH)):
        kind, j = i % 3, i // 3
        sv = saved[i]
        def post2_bwd(d, yv, gt, g):
            dyv, d_gt, d_g = _post_bwd(d, yv, gt, g)
            return (dyv,), (d_gt, d_g)

        (dy2,), (d_gtf, d_g3) = _rowk(f"post2_bwd{i}", post2_bwd, [dx, sv['y2']], [md(i, 5), _row(gains[i, 3])],
                                      [(D_MODEL, BF16)], [D_MODEL] * 2)
        da = _mm(f"ffn2_dg{i}", dy2, G[('ffn_w2', i)], 'nt', S, D_FF, D_MODEL, tm=2048, b_spec=by_j, extras=[sv['a']],
                 extra_specs=[pl.BlockSpec((min(2048, S), 1024), lambda i_, j_, k_: (i_, j_))],
                 epi=lambda acc, av: (acc * (2.0 * jnp.maximum(av, 0.0)),),
                 outs=[jax.ShapeDtypeStruct((S, D_FF), BF16)])
        big[('ffn_w2', i)] = _mm(f"ffn2_wg{i}", sv['a'], dy2, 'tn', D_FF, D_MODEL, S, tk=2048, pro_a=sq_relu,
                        outs=[jax.ShapeDtypeStruct((4, 1024, D_MODEL), BF16)],
                        out_specs=[pl.BlockSpec((None, 1024, 1024), lambda i_, j_, k_: (i_, 0, j_))])
        big[('ffn_w1', i)] = _mm(f"ffn1_wg{i}", sv['h2'], da, 'tn', D_MODEL, D_FF, S, tk=2048,
                        outs=[jax.ShapeDtypeStruct((4, D_MODEL, 1024), BF16)],
                        out_specs=[pl.BlockSpec((None, 1024, 1024), lambda i_, j_, k_: (j_, i_, 0))])
        dh2 = _mm(f"ffn1_dg{i}", da, G[('ffn_w1', i)], 'nt', S, D_MODEL, D_FF, tm=2048, b_spec=by_k)
        if i == DEPTH - 1:
            keys_a = [('ffn_w1', i), ('ffn_w2', i)]
            fly_a = _split_start("scatter_ga_start", [big[k] for k in keys_a], src_by='xy')
            mod = mod + fly_a['token'][0, 0]
        if i == 0:
            keys_b = [k for k in big if k not in keys_a]
            fly_b = _split_start("scatter_gb_start", [big[k] for k in keys_b], src_by='xy')
            mod = mod + fly_b['token'][0, 0]

        def mid_bwd(d2, dh2v, x1v, yv, g2, scf, gtm, g1):
            dpre, d_sh, d_sc, d_g2 = _pre_bwd(dh2v, x1v, g2, scf)
            d1 = d2 + dpre
            dyv, d_gt, d_g1 = _post_bwd(d1, yv, gtm, g1)
            return (d1, dyv), (d_sh, d_sc, d_g2, d_gt, d_g1, _colsum(dyv))

        ydt = F32 if kind == 2 else BF16
        (dx1, dy), (d_shf, d_scf, d_g2, d_gtm, d_g1, dy_cs) = _rowk(
            f"mid_bwd{i}", mid_bwd, [dx, dh2, sv['x1'], sv['y']],
            [_row(gains[i, 2]), md(i, 4), md(i, 2), _row(gains[i, 1])],
            [(D_MODEL, F32), (D_MODEL, ydt)], [D_MODEL] * 6)
        if kind == 0:
            dh, gm = _mla_bwd(j, dy, sv['h'], sv['mix'], mla_p[j], rope)
            keep(gm, j)
        elif kind == 1:
            dh, gm = _conv_bwd(dy, dy_cs, sv['h'], sv['mix'], conv_p)
            keep(gm, None)
        else:
            dh, gm = _pool_mixer_bwd(dy, sv['mix'], pool_p)
            keep(gm, None)

        def pre_bwd(d1, dhv, x0v, g0, scm):
            dpre, d_sh, d_sc, d_g0 = _pre_bwd(dhv, x0v, g0, scm)
            return (d1 + dpre,), (d_sh, d_sc, d_g0)

        (dx,), (d_shm, d_scm, d_g0) = _rowk(f"pre_bwd{i}", pre_bwd, [dx1, dh, sv['x0']],
                                            [_row(gains[i, 0]), md(i, 1)], [(D_MODEL, F32)], [D_MODEL] * 3)
        d_mod[i] = jnp.concatenate([d_shm, d_scm, d_gtm, d_shf, d_scf, d_gtf], axis=1).reshape(-1)
        d_gain[i] = jnp.concatenate([d_g0, d_g1, d_g2, d_g3], axis=0)
        if i == DEPTH - 1:
            landed.update(zip(keys_a, _split_wait("scatter_ga_wait", fly_a, dx)))
    landed.update(zip(keys_b, _split_wait("scatter_gb_wait", fly_b, dx)))
    keys_c = [(n, 0) for n in MLA_MATS]
    fly_c = _split_start("scatter_gc_start", [big[k] for k in keys_c], src_by='xy')
    grad_x = dx.reshape(x.shape)
    grads = {n: jnp.stack([g[l] for l in sorted(g)]) for n, g in small.items()}
    grads['norm_g'] = jnp.stack(d_gain)
    grads['ada_b'] = jnp.stack(d_mod)

    pack = jnp.concatenate([grads[n].reshape(-1) for n in SMALL] + [loss_acc[0, :1]])
    n_pack = pack.shape[0]
    rows = -(-n_pack // 1024) * 8
    pack = jnp.pad(pack, (0, rows * 128 - n_pack)).reshape(rows, 128)
    fly_s = _split_start("gather_small_start", [pack], group='xyc')

    out_g, out_d, out_m, out_v = {}, {}, {}, {}
    chains = {}
    for n in BIG + ['ffn_w1', 'ffn_w2']:
        if n in MLA_MATS:
            chains[n] = _finish(f"finish_{n}1", W[n], landed[(n, 1)], MOM[n], VAR[n], layer=1)
        elif n in BIG:
            out_g[n], out_d[n], out_m[n], out_v[n] = _finish(f"finish_{n}", W[n], landed[(n, None)], MOM[n], VAR[n])
        else:
            res = None
            for l in range(DEPTH):
                res = _finish(f"finish_{n}{l}", W[n], landed[(n, l)], MOM[n], VAR[n], layer=l, prev=res)
            out_g[n], out_d[n], out_m[n], out_v[n] = res
    landed.update(zip(keys_c, _split_wait("scatter_gc_wait", fly_c, out_g['ffn_w2'])))
    for n in MLA_MATS:
        out_g[n], out_d[n], out_m[n], out_v[n] = _finish(f"finish_{n}0", W[n], landed[(n, 0)], MOM[n], VAR[n],
                                                         layer=0, prev=chains[n])
    pack8 = _split_wait("gather_small_wait", fly_s, out_g['mla_w_o'])[0]
    (tot,) = _ew("sum_small", lambda *v: (functools.reduce(lambda p, q: p + q, v),), [(pack8, s) for s in range(8)],
                 [F32], (rows, 128))
    tot = tot.reshape(-1)
    loss = tot[n_pack - 1]
    d_mod_all = pack8.reshape(8, -1)[:, :DEPTH * 6 * D_MODEL].reshape(8, DEPTH, 6 * D_MODEL)
    final = {}
    off = 0
    for n in SMALL:
        ax = SHARD_AXIS[n]
        shape = tuple(d * 4 if k == ax else d for k, d in enumerate(W[n].shape))
        size = grads[n].size
        g = tot[off:off + size].reshape(shape)
        off += size
        if ax is not None:
            g = lax.dynamic_index_in_dim(_to_shards(g, ax), chip, 0, keepdims=False)
        final[n] = g

    g_ada = []
    for l in range(DEPTH):
        dm_l = jnp.pad(lax.dynamic_slice(d_mod_all[:, l], (0, chip * n_sh), (8, n_sh)), ((0, ADA_ROWS - 8), (0, 0)))
        g_ada.append(_mm(f"ada_wg{l}", c8, dm_l, 'tn', D_MODEL, n_sh, ADA_ROWS, tn=n_sh // 2, pro_a=silu))
    final['ada_w'] = jnp.stack(g_ada)

    for n in WEIGHTS:
        if n in out_g:
            continue
        shape = W[n].shape
        out_g[n] = final[n].reshape(shape)
        out_d[n], out_m[n], out_v[n] = _ew(f"adamw_{n}", lambda w, g, m, v: _adamw(w, g, m, v),
                                           [W[n], out_g[n], MOM[n], VAR[n]], [F32] * 3, shape)
    return (loss, grad_x, *[out_g[n] for n in WEIGHTS], *[out_d[n] for n in WEIGHTS],
            *[out_m[n] for n in WEIGHTS], *[out_v[n] for n in WEIGHTS])
```
